```python
import jax, jax.numpy as jnp
from jax import lax
import numpy as np

D_MODEL = 1024
BATCH = 8
SEQ = 8192
DEPTH = 1

HEAD_DIM = 64
D_MIX = D_MODEL
D_CONV = D_MIX // 2
D_ATTN = D_MIX - D_CONV
N_HEADS = D_ATTN // HEAD_DIM
CONV_WIDTH = 3
DILATED_BRANCHES = ((128, 1), (512, 4), (2048, 16))
BLOCK = 128
D_FF = ((-(-8 * D_MODEL // 3) + 255) // 256) * 256
D_IN = 3 * D_CONV + 3 * D_ATTN
EPS = 1e-6

kernel_name = "hybrid_shortconv_dilated_swa_swiglu"


def rms_norm(x, g):
    xf = x.astype(jnp.float32)
    y = xf * lax.rsqrt(jnp.mean(xf * xf, axis=-1, keepdims=True) + EPS)
    return (y * g.astype(jnp.float32)).astype(x.dtype)


def short_conv(u, w):
    return lax.conv_general_dilated(
        u, w[:, None, :].astype(u.dtype), window_strides=(1,),
        padding=[(CONV_WIDTH - 1, 0)], dimension_numbers=("NWC", "WIO", "NWC"),
        feature_group_count=u.shape[-1])


def dilated_branch(q, k, v, window, dil):
    B, S, H, dh = q.shape
    M = S // dil
    L = window // dil
    nb = -(-M // BLOCK)
    Mp = nb * BLOCK
    pad = Mp - M

    def to_sub(t):
        return t.reshape(B, M, dil, H, dh).transpose(0, 2, 1, 3, 4).reshape(B * dil, M, H, dh)

    qs, ks, vs = to_sub(q), to_sub(k), to_sub(v)
    Bd = B * dil
    qb = jnp.pad(qs, ((0, 0), (0, pad), (0, 0), (0, 0))).reshape(Bd, nb, BLOCK, H, dh)

    def band(t):
        tp = jnp.pad(t, ((0, 0), (BLOCK, pad), (0, 0), (0, 0)))
        prev = tp[:, :Mp].reshape(Bd, nb, BLOCK, H, dh)
        cur = tp[:, BLOCK:].reshape(Bd, nb, BLOCK, H, dh)
        return jnp.concatenate([prev, cur], axis=2)

    kw, vw = band(ks), band(vs)
    s = jnp.einsum("bnqhd,bnkhd->bnhqk", qb, kw, preferred_element_type=jnp.float32)

    i = jnp.arange(BLOCK)[:, None]
    j = jnp.arange(2 * BLOCK)[None, :]
    dist = BLOCK + i - j
    kpos = (jnp.arange(nb)[:, None, None] - 1) * BLOCK + j[None]
    valid = ((dist >= 0) & (dist <= L))[None] & (kpos >= 0)
    s = jnp.where(valid[None, :, None], s, -jnp.inf)

    m = jnp.max(s, axis=-1, keepdims=True)
    e = jnp.exp(s - m)
    den = jnp.sum(e, axis=-1, keepdims=True)
    p = e / den
    lse = (m + jnp.log(den))[..., 0]
    o = jnp.einsum("bnhqk,bnkhd->bnqhd", p.astype(vw.dtype), vw)

    o = o.reshape(Bd, Mp, H, dh)[:, :M]
    lse = lse.transpose(0, 1, 3, 2).reshape(Bd, Mp, H)[:, :M]
    o = o.reshape(B, dil, M, H, dh).transpose(0, 2, 1, 3, 4).reshape(B, S, H, dh)
    lse = lse.reshape(B, dil, M, H).transpose(0, 2, 1, 3).reshape(B, S, H)
    return o, lse


def dilated_mixture(q, k, v):
    outs, lses = [], []
    for window, dil in DILATED_BRANCHES:
        o, lse = dilated_branch(q, k, v, window, dil)
        outs.append(o)
        lses.append(lse)
    w = jax.nn.softmax(jnp.stack(lses, axis=0), axis=0)
    o = jnp.sum(w[..., None] * jnp.stack(outs, axis=0).astype(jnp.float32), axis=0)
    return o.astype(q.dtype)


def _fwd_setup_inputs(seed: int = 0) -> dict:
    key = jax.random.key(seed)
    ks = jax.random.split(key, 13)
    f32 = jnp.float32

    def gain(k_, n):
        return 1.0 + 0.05 * jax.random.normal(k_, (DEPTH, n), f32)

    return {
        "x": jax.random.normal(ks[0], (BATCH, SEQ, D_MODEL), f32),
        "g_mix": gain(ks[1], D_MODEL),
        "w_in": jax.random.normal(ks[2], (DEPTH, D_MODEL, D_IN), f32) * D_MODEL ** -0.5,
        "conv_w": jax.random.normal(ks[3], (DEPTH, CONV_WIDTH, D_CONV), f32) * CONV_WIDTH ** -0.5,
        "g_q": gain(ks[4], HEAD_DIM),
        "g_k": gain(ks[5], HEAD_DIM),
        "g_conv_out": gain(ks[6], D_CONV),
        "g_attn_out": gain(ks[7], D_ATTN),
        "w_out": jax.random.normal(ks[8], (DEPTH, D_MIX, D_MODEL), f32) * D_MIX ** -0.5,
        "g_ffn": gain(ks[9], D_MODEL),
        "w_gate": jax.random.normal(ks[10], (DEPTH, D_MODEL, D_FF), f32) * D_MODEL ** -0.5,
        "w_up": jax.random.normal(ks[11], (DEPTH, D_MODEL, D_FF), f32) * D_MODEL ** -0.5,
        "w_down": jax.random.normal(ks[12], (DEPTH, D_FF, D_MODEL), f32) * D_FF ** -0.5,
    }


def _fwd_reference(x, g_mix, w_in, conv_w, g_q, g_k, g_conv_out, g_attn_out, w_out,
              g_ffn, w_gate, w_up, w_down):
    B, S, _ = x.shape
    splits = [D_CONV, 2 * D_CONV, 3 * D_CONV, 3 * D_CONV + D_ATTN, 3 * D_CONV + 2 * D_ATTN]
    for l in range(DEPTH):
        h = rms_norm(x, g_mix[l])
        z = h @ w_in[l]
        u, gb, gc, q, k, v = jnp.split(z, splits, axis=-1)

        y_conv = gb * short_conv(gc * u, conv_w[l])

        q = rms_norm(q.reshape(B, S, N_HEADS, HEAD_DIM), g_q[l]) * (HEAD_DIM ** -0.5)
        k = rms_norm(k.reshape(B, S, N_HEADS, HEAD_DIM), g_k[l])
        v = v.reshape(B, S, N_HEADS, HEAD_DIM)
        y_attn = dilated_mixture(q, k, v).reshape(B, S, D_ATTN)

        mix = jnp.concatenate([rms_norm(y_conv, g_conv_out[l]),
                               rms_norm(y_attn, g_attn_out[l])], axis=-1)
        x = x + mix @ w_out[l]

        h = rms_norm(x, g_ffn[l])
        x = x + (jax.nn.silu(h @ w_gate[l]) * (h @ w_up[l])) @ w_down[l]
    return x


import jax as _jax
import jax.numpy as _jnp

TWIN_FORMAT = 'train_step'
FWD_PARAMS = ['x', 'g_mix', 'w_in', 'conv_w', 'g_q', 'g_k', 'g_conv_out', 'g_attn_out', 'w_out', 'g_ffn', 'w_gate', 'w_up', 'w_down']
TWIN_WEIGHTS = ['g_mix', 'w_in', 'conv_w', 'g_q', 'g_k', 'g_conv_out', 'g_attn_out', 'w_out', 'g_ffn', 'w_gate', 'w_up', 'w_down']
TWIN_DIFF_INPUT = 'x'
TWIN_INPUTS = ['x', 'g_mix', 'w_in', 'conv_w', 'g_q', 'g_k', 'g_conv_out', 'g_attn_out', 'w_out', 'g_ffn', 'w_gate', 'w_up', 'w_down', 'loss_target', 'm_g_mix', 'm_w_in', 'm_conv_w', 'm_g_q', 'm_g_k', 'm_g_conv_out', 'm_g_attn_out', 'm_w_out', 'm_g_ffn', 'm_w_gate', 'm_w_up', 'm_w_down', 'v_g_mix', 'v_w_in', 'v_conv_w', 'v_g_q', 'v_g_k', 'v_g_conv_out', 'v_g_attn_out', 'v_w_out', 'v_g_ffn', 'v_w_gate', 'v_w_up', 'v_w_down']
TWIN_OUTPUTS = ['loss', 'grad_x', 'grad_g_mix', 'grad_w_in', 'grad_conv_w', 'grad_g_q', 'grad_g_k', 'grad_g_conv_out', 'grad_g_attn_out', 'grad_w_out', 'grad_g_ffn', 'grad_w_gate', 'grad_w_up', 'grad_w_down', 'delta_g_mix', 'delta_w_in', 'delta_conv_w', 'delta_g_q', 'delta_g_k', 'delta_g_conv_out', 'delta_g_attn_out', 'delta_w_out', 'delta_g_ffn', 'delta_w_gate', 'delta_w_up', 'delta_w_down', 'new_m_g_mix', 'new_m_w_in', 'new_m_conv_w', 'new_m_g_q', 'new_m_g_k', 'new_m_g_conv_out', 'new_m_g_attn_out', 'new_m_w_out', 'new_m_g_ffn', 'new_m_w_gate', 'new_m_w_up', 'new_m_w_down', 'new_v_g_mix', 'new_v_w_in', 'new_v_conv_w', 'new_v_g_q', 'new_v_g_k', 'new_v_g_conv_out', 'new_v_g_attn_out', 'new_v_w_out', 'new_v_g_ffn', 'new_v_w_gate', 'new_v_w_up', 'new_v_w_down']
TWIN_LEAF_KINDS = {'loss': 'loss', 'grad_x': 'grad_x', 'grad_g_mix': 'grad_w', 'grad_w_in': 'grad_w', 'grad_conv_w': 'grad_w', 'grad_g_q': 'grad_w', 'grad_g_k': 'grad_w', 'grad_g_conv_out': 'grad_w', 'grad_g_attn_out': 'grad_w', 'grad_w_out': 'grad_w', 'grad_g_ffn': 'grad_w', 'grad_w_gate': 'grad_w', 'grad_w_up': 'grad_w', 'grad_w_down': 'grad_w', 'delta_g_mix': 'delta_w', 'delta_w_in': 'delta_w', 'delta_conv_w': 'delta_w', 'delta_g_q': 'delta_w', 'delta_g_k': 'delta_w', 'delta_g_conv_out': 'delta_w', 'delta_g_attn_out': 'delta_w', 'delta_w_out': 'delta_w', 'delta_g_ffn': 'delta_w', 'delta_w_gate': 'delta_w', 'delta_w_up': 'delta_w', 'delta_w_down': 'delta_w', 'new_m_g_mix': 'new_m', 'new_m_w_in': 'new_m', 'new_m_conv_w': 'new_m', 'new_m_g_q': 'new_m', 'new_m_g_k': 'new_m', 'new_m_g_conv_out': 'new_m', 'new_m_g_attn_out': 'new_m', 'new_m_w_out': 'new_m', 'new_m_g_ffn': 'new_m', 'new_m_w_gate': 'new_m', 'new_m_w_up': 'new_m', 'new_m_w_down': 'new_m', 'new_v_g_mix': 'new_v', 'new_v_w_in': 'new_v', 'new_v_conv_w': 'new_v', 'new_v_g_q': 'new_v', 'new_v_g_k': 'new_v', 'new_v_g_conv_out': 'new_v', 'new_v_g_attn_out': 'new_v', 'new_v_w_out': 'new_v', 'new_v_g_ffn': 'new_v', 'new_v_w_gate': 'new_v', 'new_v_w_up': 'new_v', 'new_v_w_down': 'new_v'}


def _forward(args):
    return _fwd_reference(*[args[k] for k in FWD_PARAMS])


def _output_shape():
    out = _jax.eval_shape(lambda: _forward(_fwd_setup_inputs(0)))
    return out.shape, out.dtype

N_MICROBATCH = 1
ADAM_LR = 0.001
ADAM_B1 = 0.9
ADAM_B2 = 0.999
ADAM_EPS = 1e-08
ADAM_WD = 0.01
ADAM_STEP = 10
PER_EXAMPLE_BATCH_AXIS = {'x': 0, 'loss_target': 0}
SHARED_INPUTS = []
_WEIGHT_DTYPES = {'g_mix': _jnp.float32, 'w_in': _jnp.float32, 'conv_w': _jnp.float32, 'g_q': _jnp.float32, 'g_k': _jnp.float32, 'g_conv_out': _jnp.float32, 'g_attn_out': _jnp.float32, 'w_out': _jnp.float32, 'g_ffn': _jnp.float32, 'w_gate': _jnp.float32, 'w_up': _jnp.float32, 'w_down': _jnp.float32}
MOMENT_SCALE = {'g_mix': 1.308798e+00, 'w_in': 6.773002e-01, 'conv_w': 2.801624e+00, 'g_q': 1.768769e+00, 'g_k': 1.619637e+00, 'g_conv_out': 7.847602e+01, 'g_attn_out': 6.288054e+01, 'w_out': 1.397262e+00, 'g_ffn': 4.946556e+01, 'w_gate': 2.976582e-01, 'w_up': 3.076621e-01, 'w_down': 4.766184e-01}


def _to_microbatches(a, axis):
    t = _jnp.moveaxis(a, axis, 0)
    t = t.reshape((N_MICROBATCH, t.shape[0] // N_MICROBATCH) + t.shape[1:])
    return _jnp.moveaxis(t, 1, axis + 1)


def setup_inputs(seed: int = 0) -> dict:
    inp = _fwd_setup_inputs(seed)
    key = _jax.random.fold_in(_jax.random.key(seed), 7919)
    shape, _ = _output_shape()
    out = dict(inp)
    out["loss_target"] = _jax.random.normal(_jax.random.fold_in(key, 0), shape, _jnp.float32)
    for i, name in enumerate(TWIN_WEIGHTS):
        w = inp[name].astype(_jnp.float32)
        if MOMENT_SCALE is None:
            s = _jnp.sqrt(_jnp.mean(_jnp.square(w)) + 1e-30)
        else:
            s = MOMENT_SCALE[name]
        km, kv = _jax.random.split(_jax.random.fold_in(key, i + 1))
        out[name] = w
        out["m_" + name] = s * _jax.random.normal(km, w.shape, _jnp.float32)
        out["v_" + name] = (s * s) * _jax.random.uniform(kv, w.shape, _jnp.float32, 0.5, 1.5)
    if N_MICROBATCH > 1:
        for name, axis in PER_EXAMPLE_BATCH_AXIS.items():
            out[name] = _to_microbatches(out[name], axis)
    return {'x': out['x'], 'g_mix': out['g_mix'], 'w_in': out['w_in'], 'conv_w': out['conv_w'], 'g_q': out['g_q'], 'g_k': out['g_k'], 'g_conv_out': out['g_conv_out'], 'g_attn_out': out['g_attn_out'], 'w_out': out['w_out'], 'g_ffn': out['g_ffn'], 'w_gate': out['w_gate'], 'w_up': out['w_up'], 'w_down': out['w_down'], 'loss_target': out['loss_target'], 'm_g_mix': out['m_g_mix'], 'm_w_in': out['m_w_in'], 'm_conv_w': out['m_conv_w'], 'm_g_q': out['m_g_q'], 'm_g_k': out['m_g_k'], 'm_g_conv_out': out['m_g_conv_out'], 'm_g_attn_out': out['m_g_attn_out'], 'm_w_out': out['m_w_out'], 'm_g_ffn': out['m_g_ffn'], 'm_w_gate': out['m_w_gate'], 'm_w_up': out['m_w_up'], 'm_w_down': out['m_w_down'], 'v_g_mix': out['v_g_mix'], 'v_w_in': out['v_w_in'], 'v_conv_w': out['v_conv_w'], 'v_g_q': out['v_g_q'], 'v_g_k': out['v_g_k'], 'v_g_conv_out': out['v_g_conv_out'], 'v_g_attn_out': out['v_g_attn_out'], 'v_w_out': out['v_w_out'], 'v_g_ffn': out['v_g_ffn'], 'v_w_gate': out['v_w_gate'], 'v_w_up': out['v_w_up'], 'v_w_down': out['v_w_down']}


def _loss(weights, diff, rest, loss_target):
    with _jax.named_scope("forward"):
        args = {**rest, TWIN_DIFF_INPUT: diff, **{k: w.astype(_WEIGHT_DTYPES[k]) for k, w in weights.items()}}
        y = _forward(args)
    with _jax.named_scope("loss_head"):
        err = _jnp.square(y.astype(_jnp.float32) - loss_target)
        return 0.5 * _jnp.sum(_jnp.mean(err, axis=-1)) if err.ndim else 0.5 * err


def _adamw(w, g, m, v):
    m = ADAM_B1 * m + (1.0 - ADAM_B1) * g
    v = ADAM_B2 * v + (1.0 - ADAM_B2) * _jnp.square(g)
    m_hat = m / (1.0 - ADAM_B1 ** ADAM_STEP)
    v_hat = v / (1.0 - ADAM_B2 ** ADAM_STEP)
    delta = -ADAM_LR * (m_hat / (_jnp.sqrt(v_hat) + ADAM_EPS) + ADAM_WD * w)
    return delta, m, v


def reference(x, g_mix, w_in, conv_w, g_q, g_k, g_conv_out, g_attn_out, w_out, g_ffn, w_gate, w_up, w_down, loss_target, m_g_mix, m_w_in, m_conv_w, m_g_q, m_g_k, m_g_conv_out, m_g_attn_out, m_w_out, m_g_ffn, m_w_gate, m_w_up, m_w_down, v_g_mix, v_w_in, v_conv_w, v_g_q, v_g_k, v_g_conv_out, v_g_attn_out, v_w_out, v_g_ffn, v_w_gate, v_w_up, v_w_down):
    given = dict(x=x, g_mix=g_mix, w_in=w_in, conv_w=conv_w, g_q=g_q, g_k=g_k, g_conv_out=g_conv_out, g_attn_out=g_attn_out, w_out=w_out, g_ffn=g_ffn, w_gate=w_gate, w_up=w_up, w_down=w_down, loss_target=loss_target, m_g_mix=m_g_mix, m_w_in=m_w_in, m_conv_w=m_conv_w, m_g_q=m_g_q, m_g_k=m_g_k, m_g_conv_out=m_g_conv_out, m_g_attn_out=m_g_attn_out, m_w_out=m_w_out, m_g_ffn=m_g_ffn, m_w_gate=m_w_gate, m_w_up=m_w_up, m_w_down=m_w_down, v_g_mix=v_g_mix, v_w_in=v_w_in, v_conv_w=v_conv_w, v_g_q=v_g_q, v_g_k=v_g_k, v_g_conv_out=v_g_conv_out, v_g_attn_out=v_g_attn_out, v_w_out=v_w_out, v_g_ffn=v_g_ffn, v_w_gate=v_w_gate, v_w_up=v_w_up, v_w_down=v_w_down)
    weights = {n: given[n] for n in TWIN_WEIGHTS}
    shared = {n: given[n] for n in SHARED_INPUTS}
    per_example = {n: given[n] for n in ['x']}
    grad_fn = _jax.value_and_grad(_loss, argnums=(0, 1))

    def one_microbatch(ex, loss_target):
        ex = dict(ex)
        diff = ex.pop(TWIN_DIFF_INPUT)
        return grad_fn(weights, diff, {**shared, **ex}, loss_target)

    if N_MICROBATCH == 1:
        loss, (grad_w, grad_x) = one_microbatch(per_example, given["loss_target"])
    else:
        def body(carry, xs):
            loss_sum, grad_sum = carry
            l_k, (gw_k, gx_k) = one_microbatch(xs[0], xs[1])
            with _jax.named_scope("update"):
                return (loss_sum + l_k, _jax.tree.map(_jnp.add, grad_sum, gw_k)), gx_k

        init = (_jnp.zeros((), _jnp.float32), _jax.tree.map(_jnp.zeros_like, weights))
        (loss, grad_w), grad_x = _jax.lax.scan(body, init, (per_example, given["loss_target"]))
    with _jax.named_scope("update"):
        delta_w, new_m, new_v = {}, {}, {}
        for n in TWIN_WEIGHTS:
            delta_w[n], new_m[n], new_v[n] = _adamw(weights[n], grad_w[n], given["m_" + n], given["v_" + n])
    return (loss, grad_x, *[grad_w[n] for n in TWIN_WEIGHTS], *[delta_w[n] for n in TWIN_WEIGHTS],
            *[new_m[n] for n in TWIN_WEIGHTS], *[new_v[n] for n in TWIN_WEIGHTS])
```

```python
import functools

import jax
import jax.numpy as jnp
from jax import lax
from jax.experimental import pallas as pl
from jax.experimental.pallas import tpu as pltpu

F32 = jnp.float32
BF16 = jnp.bfloat16
MESH = pl.DeviceIdType.MESH

D_MODEL = 1024
D_CONV = 512
D_ATTN = 512
HEAD_DIM = 64
D_FF = 2816
D_IN = 3 * D_CONV + 3 * D_ATTN
DILATIONS = (1, 4, 16)
BAND = 128
EPS = 1e-6
NEG = -1e30
N_CHIPS = 4

ADAM_LR = 0.001
ADAM_B1 = 0.9
ADAM_B2 = 0.999
ADAM_EPS = 1e-08
ADAM_WD = 0.01
ADAM_STEP = 10

V7X_VMEM_BYTES = 64 * 1024 * 1024
VMEM_LIMIT = V7X_VMEM_BYTES - 8 * 1024 * 1024
ANY = pl.BlockSpec(memory_space=pl.ANY)
VMEM_WHOLE = pl.BlockSpec(memory_space=pltpu.VMEM)


def _params(*sem):
    return pltpu.CompilerParams(dimension_semantics=sem, vmem_limit_bytes=VMEM_LIMIT)


def _sds(shape, dtype):
    return jax.ShapeDtypeStruct(shape, dtype)


def _place():
    x, y, c = lax.axis_index("x"), lax.axis_index("y"), lax.axis_index("c")
    chips = [(1 - x, y), (x, 1 - y), (1 - x, 1 - y)]
    return x, y, c, 2 * x + y, chips, [2 * cx + cy for cx, cy in chips]


def _all_gather(shards):
    n = len(shards)

    def body(*refs):
        ins, outs = refs[:n], refs[n:2 * n]
        ssem, rsem, fsem, gsem, lsem = refs[2 * n:]
        x, y, c, me, chips, cids = _place()
        sib = (x, y, 1 - c)

        def half(w, which):
            h = shards[w].shape[0] // 2
            return pl.ds(pl.multiple_of(which * h, 8), h)

        local = [pltpu.make_async_copy(ins[w], outs[w].at[me], lsem.at[w]) for w in range(n)]
        for cp in local:
            cp.start()

        def chip_copy(w, j, src_slot):
            rows = half(w, c)
            return pltpu.make_async_remote_copy(
                src_ref=ins[w].at[rows], dst_ref=outs[w].at[src_slot, rows],
                send_sem=ssem.at[3 * w + j], recv_sem=rsem.at[3 * w + j],
                device_id=(*chips[j], c), device_id_type=MESH)

        def sib_copy(w, j, which):
            rows = half(w, which)
            return pltpu.make_async_remote_copy(
                src_ref=outs[w].at[cids[j], rows], dst_ref=outs[w].at[cids[j], rows],
                send_sem=fsem.at[3 * w + j], recv_sem=gsem.at[3 * w + j],
                device_id=sib, device_id_type=MESH)

        sends = [chip_copy(w, j, me) for w in range(n) for j in range(3)]
        for cp in sends:
            cp.start()
        passed = []
        for w in range(n):
            for j in range(3):
                chip_copy(w, j, cids[j]).wait_recv()
                cp = sib_copy(w, j, c)
                cp.start()
                passed.append(cp)
        for w in range(n):
            for j in range(3):
                sib_copy(w, j, 1 - c).wait_recv()
        for cp in sends + passed:
            cp.wait_send()
        for cp in local:
            cp.wait()

    return pl.pallas_call(
        body, name="all_gather_weights",
        out_shape=[_sds((N_CHIPS,) + s.shape, s.dtype) for s in shards],
        in_specs=[ANY] * n, out_specs=[ANY] * n,
        scratch_shapes=[pltpu.SemaphoreType.DMA((3 * n,))] * 4 + [pltpu.SemaphoreType.DMA((n,))],
    )(*shards)


def _pair_exchange(grads):
    n = len(grads)

    def body(*refs):
        ins, own, got = refs[:n], refs[n:2 * n], refs[2 * n:3 * n]
        ssem, rsem, lsem = refs[3 * n:]
        x, y, c, _, _, _ = _place()
        local, swaps = [], []
        for w in range(n):
            h = grads[w].shape[1] // 2
            mine = pl.ds(pl.multiple_of(c * h, 8), h)
            theirs = pl.ds(pl.multiple_of((1 - c) * h, 8), h)
            local.append(pltpu.make_async_copy(ins[w].at[:, mine, :], own[w], lsem.at[w]))
            swaps.append(pltpu.make_async_remote_copy(
                src_ref=ins[w].at[:, theirs, :], dst_ref=got[w],
                send_sem=ssem.at[w], recv_sem=rsem.at[w],
                device_id=(x, y, 1 - c), device_id_type=MESH))
        for cp in local + swaps:
            cp.start()
        for cp in swaps + local:
            cp.wait()

    halves = [_sds((N_CHIPS, g.shape[1] // 2, g.shape[2]), g.dtype) for g in grads]
    res = pl.pallas_call(
        body, name="grad_pair_exchange", out_shape=halves + halves,
        in_specs=[ANY] * n, out_specs=[ANY] * (2 * n),
        scratch_shapes=[pltpu.SemaphoreType.DMA((n,))] * 3,
    )(*grads)
    return res[:n], res[n:]


def _chip_exchange(parts):
    n = len(parts)

    def body(*refs):
        ins, own, got = refs[:n], refs[n:2 * n], refs[2 * n:3 * n]
        ssem, rsem, lsem = refs[3 * n:]
        _, _, c, me, chips, cids = _place()
        local = [pltpu.make_async_copy(ins[w].at[me], own[w], lsem.at[w]) for w in range(n)]
        sends = [pltpu.make_async_remote_copy(
            src_ref=ins[w].at[cids[j]], dst_ref=got[w].at[j],
            send_sem=ssem.at[3 * w + j], recv_sem=rsem.at[3 * w + j],
            device_id=(*chips[j], c), device_id_type=MESH) for w in range(n) for j in range(3)]
        for cp in local + sends:
            cp.start()
        for cp in sends + local:
            cp.wait()

    res = pl.pallas_call(
        body, name="grad_chip_exchange",
        out_shape=[_sds(p.shape[1:], p.dtype) for p in parts]
        + [_sds((3,) + p.shape[1:], p.dtype) for p in parts],
        in_specs=[ANY] * n, out_specs=[ANY] * (2 * n),
        scratch_shapes=[pltpu.SemaphoreType.DMA((3 * n,))] * 2 + [pltpu.SemaphoreType.DMA((n,))],
    )(*parts)
    return res[:n], res[n:]


def _pair_share(halves):
    n = len(halves)

    def body(*refs):
        ins, outs = refs[:n], refs[n:2 * n]
        ssem, rsem, lsem = refs[2 * n:]
        x, y, c, _, _, _ = _place()
        local = [pltpu.make_async_copy(ins[w], outs[w].at[c], lsem.at[w]) for w in range(n)]
        sends = [pltpu.make_async_remote_copy(
            src_ref=ins[w], dst_ref=outs[w].at[c], send_sem=ssem.at[w], recv_sem=rsem.at[w],
            device_id=(x, y, 1 - c), device_id_type=MESH) for w in range(n)]
        for cp in local + sends:
            cp.start()
        for w in range(n):
            sends[w].wait_send()
            pltpu.make_async_remote_copy(
                src_ref=ins[w], dst_ref=outs[w].at[1 - c], send_sem=ssem.at[w],
                recv_sem=rsem.at[w], device_id=(x, y, 1 - c), device_id_type=MESH).wait_recv()
        for cp in local:
            cp.wait()

    return pl.pallas_call(
        body, name="grad_pair_share",
        out_shape=[_sds((2,) + h.shape, h.dtype) for h in halves],
        in_specs=[ANY] * n, out_specs=[ANY] * n,
        scratch_shapes=[pltpu.SemaphoreType.DMA((n,))] * 3,
    )(*halves)


SMALL_ROWS = 16
SMALL_LAYOUT = (
    ("g_mix", 0, 0, 1, 1024), ("g_ffn", 1, 0, 1, 1024), ("g_conv_out", 2, 0, 1, 512),
    ("g_attn_out", 2, 512, 1, 512), ("g_q", 3, 0, 1, 512), ("g_k", 3, 512, 1, 512),
    ("loss", 4, 0, 1, 128), ("conv_w", 8, 0, 8, 512))


def _small_all_reduce(parts):
    names = [s[0] for s in SMALL_LAYOUT]

    def body(*refs):
        ins = refs[:len(names)]
        out_ref, stage, buf, ssem, rsem = refs[len(names):]
        x, y, c, _, _, _ = _place()
        me = 4 * x + 2 * y + c
        stage[...] = jnp.zeros_like(stage)
        for ref, (_, r0, c0, nr, nc) in zip(ins, SMALL_LAYOUT):
            stage[r0:r0 + nr, c0:c0 + nc] = ref[0:nr, :]
        buf[me] = stage[...]
        peers = []
        for d in range(1, 8):
            px = 1 - x if d & 4 else x
            py = 1 - y if d & 2 else y
            pc = 1 - c if d & 1 else c
            peers.append(((px, py, pc), 4 * px + 2 * py + pc))
        sends = [pltpu.make_async_remote_copy(
            src_ref=stage, dst_ref=buf.at[me], send_sem=ssem.at[k], recv_sem=rsem.at[k],
            device_id=peer, device_id_type=MESH) for k, (peer, _) in enumerate(peers)]
        for cp in sends:
            cp.start()
        for k, (peer, pid) in enumerate(peers):
            pltpu.make_async_remote_copy(
                src_ref=stage, dst_ref=buf.at[pid], send_sem=ssem.at[k], recv_sem=rsem.at[k],
                device_id=peer, device_id_type=MESH).wait_recv()
        for cp in sends:
            cp.wait_send()
        acc = buf[0]
        for k in range(1, 8):
            acc = acc + buf[k]
        out_ref[...] = acc

    return pl.pallas_call(
        body, name="small_all_reduce", out_shape=_sds((SMALL_ROWS, 1024), F32),
        in_specs=[VMEM_WHOLE] * len(names), out_specs=VMEM_WHOLE,
        scratch_shapes=[pltpu.VMEM((SMALL_ROWS, 1024), F32), pltpu.VMEM((8, SMALL_ROWS, 1024), F32),
                        pltpu.SemaphoreType.DMA((7,)), pltpu.SemaphoreType.DMA((7,))],
    )(*[parts[k] for k in names])


def _dot(a, b):
    return jnp.dot(a, b, preferred_element_type=F32)


def _dot_nt(a, b):
    return lax.dot_general(a, b, (((1,), (1,)), ((), ())), preferred_element_type=F32)


def _dot_tn(a, b):
    return lax.dot_general(a, b, (((0,), (0,)), ((), ())), preferred_element_type=F32)


def _sigmoid(v):
    return 1.0 / (1.0 + jnp.exp(-v))


def _rms_scale(v):
    return lax.rsqrt(jnp.mean(v * v, axis=-1, keepdims=True) + EPS)


def _rms_bwd(v, r, g, dy):
    vh = v * r
    dh = dy * g
    return r * (dh - vh * jnp.mean(dh * vh, axis=-1, keepdims=True)), vh


def _head_sum(a, ones_bd):
    hi = a.astype(BF16)
    lo = (a - hi.astype(F32)).astype(BF16)
    return _dot(hi, ones_bd) + _dot(lo, ones_bd)


def _head_rms_scale(v, ones_bd):
    return lax.rsqrt(_head_sum(v * v, ones_bd) * (1.0 / HEAD_DIM) + EPS)


def _norm_matmul(name, x, g, ws, tm, tn, swiglu):
    t, d = x.shape
    n = ws[0].shape[1]
    nw = len(ws)

    def body(x_ref, g_ref, *refs):
        w_refs, h_ref, o_refs = refs[:nw], refs[nw], refs[nw + 1:2 * nw + 1]
        hs = refs[-1]

        @pl.when(pl.program_id(1) == 0)
        def _():
            xv = x_ref[...]
            h = (xv * _rms_scale(xv) * g_ref[...]).astype(BF16)
            hs[...] = h
            h_ref[...] = h

        h = hs[...]
        outs = [_dot(h, w[...]) for w in w_refs]
        for o_ref, o in zip(o_refs, outs):
            o_ref[...] = o
        if swiglu:
            refs[2 * nw + 1][...] = (outs[0] * _sigmoid(outs[0]) * outs[1]).astype(BF16)

    row = pl.BlockSpec((tm, d), lambda i, j: (i, 0))
    col = pl.BlockSpec((tm, tn), lambda i, j: (i, j))
    out_shape = [_sds((t, d), BF16)] + [_sds((t, n), F32)] * nw
    out_specs = [row] + [col] * nw
    if swiglu:
        out_shape.append(_sds((t, n), BF16))
        out_specs.append(col)
    return pl.pallas_call(
        body, name=name, grid=(t // tm, n // tn), out_shape=out_shape,
        in_specs=[row, pl.BlockSpec((1, d), lambda i, j: (0, 0))]
        + [pl.BlockSpec((d, tn), lambda i, j: (0, j))] * nw,
        out_specs=out_specs, scratch_shapes=[pltpu.VMEM((tm, d), BF16)],
        compiler_params=_params("parallel", "arbitrary"),
    )(x, g, *ws)


def _matmul(name, a, w, extras, out_dtypes, epilogue, tm, tn, transposed_w=False, loss=False):
    t, k = a.shape
    n = w.shape[0] if transposed_w else w.shape[1]
    ne, no = len(extras), len(out_dtypes)

    def body(a_ref, w_ref, *refs):
        e_refs, o_refs = refs[:ne], refs[ne:]
        acc = _dot_nt(a_ref[...], w_ref[...]) if transposed_w else _dot(a_ref[...], w_ref[...])
        res = epilogue(acc, *[e[...] for e in e_refs])
        for o_ref, r in zip(o_refs[:no], res[:no]):
            o_ref[...] = r.astype(o_ref.dtype)
        if loss:
            first = jnp.logical_and(pl.program_id(0) == 0, pl.program_id(1) == 0)

            @pl.when(first)
            def _():
                o_refs[no][...] = jnp.zeros_like(o_refs[no])

            o_refs[no][...] += res[no]

    col = pl.BlockSpec((tm, tn), lambda i, j: (i, j))
    w_spec = (pl.BlockSpec((tn, k), lambda i, j: (j, 0)) if transposed_w
              else pl.BlockSpec((k, tn), lambda i, j: (0, j)))
    out_shape = [_sds((t, n), dt) for dt in out_dtypes]
    out_specs = [col] * no
    if loss:
        out_shape.append(_sds((8, 128), F32))
        out_specs.append(pl.BlockSpec((8, 128), lambda i, j: (0, 0)))
    return pl.pallas_call(
        body, name=name, grid=(t // tm, n // tn), out_shape=out_shape,
        in_specs=[pl.BlockSpec((tm, k), lambda i, j: (i, 0)), w_spec] + [col] * ne,
        out_specs=out_specs,
        compiler_params=_params(*(("arbitrary", "arbitrary") if loss else ("parallel", "parallel"))),
    )(a, w, *extras)


def _matmul_norm_bwd(name, pairs, x, dres, g, tm):
    t, d = x.shape
    npairs = len(pairs)

    def body(*refs):
        a_refs, w_refs = refs[:npairs], refs[npairs:2 * npairs]
        x_ref, r_ref, g_ref, dx_ref, dxb_ref, dg_ref = refs[2 * npairs:]
        dy = _dot_nt(a_refs[0][...], w_refs[0][...])
        for a_ref, w_ref in zip(a_refs[1:], w_refs[1:]):
            dy = dy + _dot_nt(a_ref[...], w_ref[...])
        xv = x_ref[...]
        dx, xh = _rms_bwd(xv, _rms_scale(xv), g_ref[...], dy)
        dx = dx + r_ref[...]
        dx_ref[...] = dx
        dxb_ref[...] = dx.astype(BF16)

        @pl.when(pl.program_id(0) == 0)
        def _():
            dg_ref[...] = jnp.zeros_like(dg_ref)

        dg_ref[...] += jnp.sum(dy * xh, axis=0, keepdims=True)

    row = pl.BlockSpec((tm, d), lambda i: (i, 0))
    vec = pl.BlockSpec((1, d), lambda i: (0, 0))
    return pl.pallas_call(
        body, name=name, grid=(t // tm,),
        out_shape=[_sds((t, d), F32), _sds((t, d), BF16), _sds((1, d), F32)],
        in_specs=[pl.BlockSpec((tm, a.shape[1]), lambda i: (i, 0)) for a, _ in pairs]
        + [pl.BlockSpec(w.shape, lambda i: (0, 0)) for _, w in pairs] + [row, row, vec],
        out_specs=[row, row, vec],
        compiler_params=_params("arbitrary"),
    )(*[a for a, _ in pairs], *[w for _, w in pairs], x, dres, g)


def _matmul_tn(name, a, g, tn, tk):
    t, ka = a.shape
    n = g.shape[1]

    def body(a_ref, g_ref, o_ref):
        @pl.when(pl.program_id(1) == 0)
        def _():
            o_ref[...] = jnp.zeros_like(o_ref)

        o_ref[...] += _dot_tn(a_ref[...], g_ref[...])

    return pl.pallas_call(
        body, name=name, grid=(n // tn, t // tk), out_shape=_sds((ka, n), F32),
        in_specs=[pl.BlockSpec((tk, ka), lambda j, s: (s, 0)),
                  pl.BlockSpec((tk, tn), lambda j, s: (s, j))],
        out_specs=pl.BlockSpec((ka, tn), lambda j, s: (0, j)),
        compiler_params=_params("parallel", "arbitrary"),
    )(a, g)


def _elementwise(name, fn, ins, out_dtypes, tr):
    r, n = ins[0].shape
    tr = max(d for d in range(1, min(tr, r) + 1) if r % d == 0 and (d % 8 == 0 or d == r))
    ni = len(ins)

    def body(*refs):
        res = fn(*[ref[...] for ref in refs[:ni]])
        for o_ref, v in zip(refs[ni:], res):
            o_ref[...] = v.astype(o_ref.dtype)

    blk = pl.BlockSpec((tr, n), lambda i: (i, 0))
    return pl.pallas_call(
        body, name=name, grid=(r // tr,), out_shape=[_sds((r, n), dt) for dt in out_dtypes],
        in_specs=[blk] * ni, out_specs=[blk] * len(out_dtypes),
        compiler_params=_params("parallel"),
    )(*ins)


def _adamw(name, w, g, m, v):
    def fn(w, g, m, v):
        m = ADAM_B1 * m + (1.0 - ADAM_B1) * g
        v = ADAM_B2 * v + (1.0 - ADAM_B2) * (g * g)
        m_hat = m / (1.0 - ADAM_B1 ** ADAM_STEP)
        v_hat = v / (1.0 - ADAM_B2 ** ADAM_STEP)
        return -ADAM_LR * (m_hat / (jnp.sqrt(v_hat) + ADAM_EPS) + ADAM_WD * w), m, v

    return _elementwise(name, fn, [w, g, m, v], [F32] * 3, 256)


def _qkv_prepare(z, gq, gk, ones_bd, tm):
    t = z.shape[0]

    def body(z_ref, gq_ref, gk_ref, bd_ref, q_ref, k_ref, v_ref):
        bd = bd_ref[...]
        q = z_ref[:, 0:512]
        k = z_ref[:, 512:1024]
        q_ref[...] = ((q * _head_rms_scale(q, bd) * gq_ref[...]) * HEAD_DIM ** -0.5).astype(BF16)
        k_ref[...] = (k * _head_rms_scale(k, bd) * gk_ref[...]).astype(BF16)
        v_ref[...] = z_ref[:, 1024:1536].astype(BF16)

    vec = pl.BlockSpec((1, 512), lambda i: (0, 0))
    out = pl.BlockSpec((tm, 512), lambda i: (i, 0))
    return pl.pallas_call(
        body, name="qkv_prepare", grid=(t // tm,), out_shape=[_sds((t, 512), BF16)] * 3,
        in_specs=[pl.BlockSpec((tm, 1536), lambda i: (i, 1)), vec, vec,
                  pl.BlockSpec((512, 512), lambda i: (0, 0))],
        out_specs=[out] * 3, compiler_params=_params("parallel"),
    )(z, gq, gk, ones_bd)


def _band_masks():
    lane = lax.broadcasted_iota(jnp.int32, (BAND, BAND), 1)
    row = lax.broadcasted_iota(jnp.int32, (BAND, BAND), 0)
    head0 = lane < HEAD_DIM
    ones = [jnp.where(head0, 1.0, 0.0).astype(BF16), jnp.where(head0, 0.0, 1.0).astype(BF16)]
    return lane - row, head0, ones


def _attn_fwd(name, qv, kv, vv, bm):
    m, w = qv.shape
    sub = bm // BAND

    def body(q_ref, kp_ref, k_ref, vp_ref, v_ref, o_ref, l_ref):
        i = pl.program_id(1)
        diff, head0, hmask = _band_masks()
        first = jnp.where(i > 0, 0, 2 * BAND)
        for b in range(sub):
            rows = slice(b * BAND, (b + 1) * BAND)
            q = q_ref[rows, :]
            if b == 0:
                kp, vp, lo = kp_ref[...], vp_ref[...], first
            else:
                prev = slice((b - 1) * BAND, b * BAND)
                kp, vp, lo = k_ref[prev, :], v_ref[prev, :], 0
            kc, vc = k_ref[rows, :], v_ref[rows, :]
            mp, mc = diff >= lo, diff <= 0
            o_h, l_h = [], []
            for h in range(2):
                qh = q * hmask[h]
                sp = jnp.where(mp, _dot_nt(qh, kp), NEG)
                sc = jnp.where(mc, _dot_nt(qh, kc), NEG)
                mx = jnp.maximum(jnp.max(sp, axis=-1, keepdims=True),
                                 jnp.max(sc, axis=-1, keepdims=True))
                ep, ec = jnp.exp(sp - mx), jnp.exp(sc - mx)
                den = jnp.sum(ep, axis=-1, keepdims=True) + jnp.sum(ec, axis=-1, keepdims=True)
                o_h.append((_dot(ep.astype(BF16), vp) + _dot(ec.astype(BF16), vc)) / den)
                l_h.append(jnp.broadcast_to(mx + jnp.log(den), (BAND, BAND)))
            o_ref[rows, :] = jnp.where(head0, o_h[0], o_h[1])
            l_ref[rows, :] = jnp.where(head0, l_h[0], l_h[1])

    main = pl.BlockSpec((bm, BAND), lambda j, i: (i, j))
    prev = pl.BlockSpec((BAND, BAND), lambda j, i: (jnp.maximum(i * sub - 1, 0), j))
    return pl.pallas_call(
        body, name=name, grid=(w // BAND, m // bm), out_shape=[_sds((m, w), F32)] * 2,
        in_specs=[main, prev, main, prev, main], out_specs=[main, main],
        compiler_params=_params("parallel", "parallel"),
    )(qv, kv, kv, vv, vv)


def _attn_bwd(name, qv, kv, vv, dov, lv, dv_, bm):
    m, w = qv.shape
    sub = bm // BAND
    nrb = m // bm

    def body(q_ref, qn_ref, kp_ref, k_ref, vp_ref, v_ref, do_ref, don_ref, l_ref, ln_ref,
             d_ref, dn_ref, dq_ref, dk_ref, dv_ref):
        i = pl.program_id(1)
        diff, head0, hmask = _band_masks()
        first = jnp.where(i > 0, 0, 2 * BAND)
        last = jnp.where(i < nrb - 1, 0, 2 * BAND)
        dk_acc = [[None, None] for _ in range(sub)]
        dv_acc = [[None, None] for _ in range(sub)]

        def add(acc, b, h, val):
            acc[b][h] = val if acc[b][h] is None else acc[b][h] + val

        for qb in range(sub + 1):
            inside = qb < sub
            rows = slice(qb * BAND, (qb + 1) * BAND)
            if inside:
                q, do, lse, dd = q_ref[rows, :], do_ref[rows, :], l_ref[rows, :], d_ref[rows, :]
            else:
                q, do, lse, dd = qn_ref[...], don_ref[...], ln_ref[...], dn_ref[...]
            if qb == 0:
                kp, vp, lo = kp_ref[...], vp_ref[...], first
            else:
                prev = slice((qb - 1) * BAND, qb * BAND)
                kp, vp, lo = k_ref[prev, :], v_ref[prev, :], (0 if inside else last)
            mp = diff >= lo
            dq_h = []
            for h in range(2):
                qh = q * hmask[h]
                doh = do * hmask[h]
                lh = lse[:, h * HEAD_DIM:h * HEAD_DIM + 1]
                dh = dd[:, h * HEAD_DIM:h * HEAD_DIM + 1]
                pp = jnp.where(mp, jnp.exp(_dot_nt(qh, kp) - lh), 0.0)
                dsp = pp * (_dot_nt(doh, vp) - dh)
                ppb, dspb = pp.astype(BF16), dsp.astype(BF16)
                if qb > 0:
                    add(dk_acc, qb - 1, h, _dot_tn(dspb, q))
                    add(dv_acc, qb - 1, h, _dot_tn(ppb, do))
                if inside:
                    kc, vc = k_ref[rows, :], v_ref[rows, :]
                    pc = jnp.where(diff <= 0, jnp.exp(_dot_nt(qh, kc) - lh), 0.0)
                    dsc = pc * (_dot_nt(doh, vc) - dh)
                    pcb, dscb = pc.astype(BF16), dsc.astype(BF16)
                    add(dk_acc, qb, h, _dot_tn(dscb, q))
                    add(dv_acc, qb, h, _dot_tn(pcb, do))
                    dq_h.append(_dot(dspb, kp) + _dot(dscb, kc))
            if inside:
                dq_ref[rows, :] = jnp.where(head0, dq_h[0], dq_h[1])
        for b in range(sub):
            rows = slice(b * BAND, (b + 1) * BAND)
            dk_ref[rows, :] = jnp.where(head0, dk_acc[b][0], dk_acc[b][1])
            dv_ref[rows, :] = jnp.where(head0, dv_acc[b][0], dv_acc[b][1])

    nblk = m // BAND
    main = pl.BlockSpec((bm, BAND), lambda j, i: (i, j))
    prev = pl.BlockSpec((BAND, BAND), lambda j, i: (jnp.maximum(i * sub - 1, 0), j))
    nxt = pl.BlockSpec((BAND, BAND), lambda j, i: (jnp.minimum((i + 1) * sub, nblk - 1), j))
    return pl.pallas_call(
        body, name=name, grid=(w // BAND, nrb), out_shape=[_sds((m, w), F32)] * 3,
        in_specs=[main, nxt, prev, main, prev, main, main, nxt, main, nxt, main, nxt],
        out_specs=[main] * 3, compiler_params=_params("parallel", "parallel"),
    )(qv, qv, kv, kv, vv, vv, dov, dov, lv, lv, dv_, dv_)


def _halo_rows(tm, t):
    per = tm // 8
    prev = lambda i: (jnp.maximum(i * per - 1, 0), 0)
    nxt = lambda i: (jnp.minimum((i + 1) * per, t // 8 - 1), 0)
    return prev, nxt


def _mixer_out(z, cw, attn, g_conv, g_attn, tm):
    t = z.shape[0]
    prev, _ = _halo_rows(tm, t)

    def body(z_ref, zp_ref, cw_ref, o1, l1, o2, l2, o3, l3, gc_ref, ga_ref, mix_ref, y_ref, lse_ref):
        i = pl.program_id(0)
        keep = jnp.where(i > 0, 1.0, 0.0)
        cu = jnp.concatenate([zp_ref[:, 0:512] * zp_ref[:, 1024:1536] * keep,
                              z_ref[:, 0:512] * z_ref[:, 1024:1536]], axis=0)
        c = (cw_ref[0:1, :] * pltpu.roll(cu, 2, 0) + cw_ref[1:2, :] * pltpu.roll(cu, 1, 0)
             + cw_ref[2:3, :] * cu)[8:, :]
        yc = z_ref[:, 512:1024] * c
        mix_ref[:, 0:512] = (yc * _rms_scale(yc) * gc_ref[...]).astype(BF16)
        ls = [l1[...], l2[...], l3[...]]
        mx = jnp.maximum(jnp.maximum(ls[0], ls[1]), ls[2])
        es = [jnp.exp(l - mx) for l in ls]
        tot = es[0] + es[1] + es[2]
        ya = (es[0] * o1[...] + es[1] * o2[...] + es[2] * o3[...]) / tot
        y_ref[...] = ya
        lse_ref[...] = mx + jnp.log(tot)
        mix_ref[:, 512:1024] = (ya * _rms_scale(ya) * ga_ref[...]).astype(BF16)

    blk = pl.BlockSpec((tm, 512), lambda i: (i, 0))
    vec = pl.BlockSpec((1, 512), lambda i: (0, 0))
    return pl.pallas_call(
        body, name="mixer_out", grid=(t // tm,),
        out_shape=[_sds((t, 1024), BF16), _sds((t, 512), F32), _sds((t, 512), F32)],
        in_specs=[pl.BlockSpec((tm, 1536), lambda i: (i, 0)), pl.BlockSpec((8, 1536), prev),
                  pl.BlockSpec((8, 512), lambda i: (0, 0))] + [blk] * 6 + [vec, vec],
        out_specs=[pl.BlockSpec((tm, 1024), lambda i: (i, 0)), blk, blk],
        compiler_params=_params("parallel"),
    )(z, z, cw, *attn, g_conv, g_attn)


def _mixer_bwd(z, dmix, y_attn, cw, g_conv, g_attn, ones_bd, tm):
    t = z.shape[0]
    nblk = t // tm
    prev, nxt = _halo_rows(tm, t)
    e = tm + 16

    def body(z_ref, zp_ref, zn_ref, dm_ref, dmn_ref, y_ref, cw_ref, gc_ref, ga_ref, bd_ref,
             dz_ref, do_ref, dd_ref, dcw_ref, dgc_ref, dga_ref):
        i = pl.program_id(0)
        rows = lax.broadcasted_iota(jnp.int32, (e, 1), 0)
        lo = jnp.where(i > 0, 0, 8)
        hi = jnp.where(i < nblk - 1, e, tm + 8)
        ze = jnp.concatenate([zp_ref[...], z_ref[...], zn_ref[...]], axis=0)
        u, gb, gcv = ze[:, 0:512], ze[:, 512:1024], ze[:, 1024:1536]
        w0, w1, w2 = cw_ref[0:1, :], cw_ref[1:2, :], cw_ref[2:3, :]
        cu = jnp.where(rows >= lo, gcv * u, 0.0)
        cu1, cu2 = pltpu.roll(cu, 1, 0), pltpu.roll(cu, 2, 0)
        c = w0 * cu2 + w1 * cu1 + w2 * cu
        yc = gb * c
        dma = jnp.concatenate([jnp.zeros((8, 512), F32), dm_ref[:, 0:512], dmn_ref[...]], axis=0)
        dyc, ych = _rms_bwd(yc, _rms_scale(yc), gc_ref[...], dma)
        dc = jnp.where(jnp.logical_and(rows >= 8, rows < hi), dyc * gb, 0.0)
        dcu = w0 * pltpu.roll(dc, e - 2, 0) + w1 * pltpu.roll(dc, e - 1, 0) + w2 * dc
        mid = slice(8, 8 + tm)
        dz_ref[:, 0:512] = (dcu * gcv)[mid, :].astype(BF16)
        dz_ref[:, 512:1024] = (dyc * c)[mid, :].astype(BF16)
        dz_ref[:, 1024:1536] = (dcu * u)[mid, :].astype(BF16)

        ya = y_ref[...]
        dmb = dm_ref[:, 512:1024]
        dya, yah = _rms_bwd(ya, _rms_scale(ya), ga_ref[...], dmb)
        do_ref[...] = dya.astype(BF16)
        dd_ref[...] = _head_sum(dya * ya, bd_ref[...])

        @pl.when(i == 0)
        def _():
            dcw_ref[...] = jnp.zeros_like(dcw_ref)
            dgc_ref[...] = jnp.zeros_like(dgc_ref)
            dga_ref[...] = jnp.zeros_like(dga_ref)

        dcm = jnp.where(rows < tm + 8, dc, 0.0)
        dcw_ref[0:1, :] += jnp.sum(dcm * cu2, axis=0, keepdims=True)
        dcw_ref[1:2, :] += jnp.sum(dcm * cu1, axis=0, keepdims=True)
        dcw_ref[2:3, :] += jnp.sum(dcm * cu, axis=0, keepdims=True)
        dgc_ref[...] += jnp.sum((dma * ych)[mid, :], axis=0, keepdims=True)
        dga_ref[...] += jnp.sum(dmb * yah, axis=0, keepdims=True)

    blk = pl.BlockSpec((tm, 512), lambda i: (i, 0))
    vec = pl.BlockSpec((1, 512), lambda i: (0, 0))
    cwb = pl.BlockSpec((8, 512), lambda i: (0, 0))
    return pl.pallas_call(
        body, name="mixer_bwd", grid=(nblk,),
        out_shape=[_sds((t, 1536), BF16), _sds((t, 512), BF16), _sds((t, 512), F32),
                   _sds((8, 512), F32), _sds((1, 512), F32), _sds((1, 512), F32)],
        in_specs=[pl.BlockSpec((tm, 1536), lambda i: (i, 0)), pl.BlockSpec((8, 1536), prev),
                  pl.BlockSpec((8, 1536), nxt), pl.BlockSpec((tm, 1024), lambda i: (i, 0)),
                  pl.BlockSpec((8, 512), nxt), blk, cwb, vec, vec,
                  pl.BlockSpec((512, 512), lambda i: (0, 0))],
        out_specs=[pl.BlockSpec((tm, 1536), lambda i: (i, 0)), blk, blk, cwb, vec, vec],
        compiler_params=_params("arbitrary"),
    )(z, z, z, dmix, dmix, y_attn, cw, g_conv, g_attn, ones_bd)


def _qkv_bwd(z, dzc, dqs, dks, dvs, gq, gk, ones_bd, tm):
    t = z.shape[0]

    def body(zq_ref, zk_ref, dzc_ref, q1, q2, q3, k1, k2, k3, v1, v2, v3, gq_ref, gk_ref, bd_ref,
             dz_ref, dgq_ref, dgk_ref):
        bd = bd_ref[...]

        @pl.when(pl.program_id(0) == 0)
        def _():
            dgq_ref[...] = jnp.zeros_like(dgq_ref)
            dgk_ref[...] = jnp.zeros_like(dgk_ref)

        def back(v, dn, g, scale):
            r = _head_rms_scale(v, bd)
            vh = v * r
            dh = dn * (g * scale)
            dv = r * (dh - vh * (_head_sum(dh * vh, bd) * (1.0 / HEAD_DIM)))
            return dv, jnp.sum(dn * scale * vh, axis=0, keepdims=True)

        dq, dgq = back(zq_ref[...], q1[...] + q2[...] + q3[...], gq_ref[...], HEAD_DIM ** -0.5)
        dk, dgk = back(zk_ref[...], k1[...] + k2[...] + k3[...], gk_ref[...], 1.0)
        dgq_ref[...] += dgq
        dgk_ref[...] += dgk
        dz_ref[:, 0:1536] = dzc_ref[...]
        dz_ref[:, 1536:2048] = dq.astype(BF16)
        dz_ref[:, 2048:2560] = dk.astype(BF16)
        dz_ref[:, 2560:3072] = (v1[...] + v2[...] + v3[...]).astype(BF16)

    blk = pl.BlockSpec((tm, 512), lambda i: (i, 0))
    vec = pl.BlockSpec((1, 512), lambda i: (0, 0))
    return pl.pallas_call(
        body, name="qkv_bwd", grid=(t // tm,),
        out_shape=[_sds((t, D_IN), BF16), _sds((1, 512), F32), _sds((1, 512), F32)],
        in_specs=[pl.BlockSpec((tm, 512), lambda i: (i, 3)), pl.BlockSpec((tm, 512), lambda i: (i, 4)),
                  pl.BlockSpec((tm, 1536), lambda i: (i, 0))] + [blk] * 9
        + [vec, vec, pl.BlockSpec((512, 512), lambda i: (0, 0))],
        out_specs=[pl.BlockSpec((tm, D_IN), lambda i: (i, 0)), vec, vec],
        compiler_params=_params("arbitrary"),
    )(z, z, dzc, *dqs, *dks, *dvs, gq, gk, ones_bd)


def _columns_from_chips(g):
    return g.transpose(1, 0, 2).reshape(g.shape[1], N_CHIPS * g.shape[2])


def _columns_to_chips(w):
    k, n4 = w.shape
    return w.reshape(k, N_CHIPS, n4 // N_CHIPS).transpose(1, 0, 2)


def kernel(x, g_mix, w_in, conv_w, g_q, g_k, g_conv_out, g_attn_out, w_out, g_ffn, w_gate, w_up, w_down, loss_target, m_g_mix, m_w_in, m_conv_w, m_g_q, m_g_k, m_g_conv_out, m_g_attn_out, m_w_out, m_g_ffn, m_w_gate, m_w_up, m_w_down, v_g_mix, v_w_in, v_conv_w, v_g_q, v_g_k, v_g_conv_out, v_g_attn_out, v_w_out, v_g_ffn, v_w_gate, v_w_up, v_w_down):
    t = x.shape[1]
    xs = x[0]
    target = loss_target[0]
    tm = min(512, t)

    cw_pad = jnp.pad(conv_w[0], ((0, 13), (0, 0)))
    gathered = _all_gather([w_in[0].astype(BF16), w_out[0].astype(BF16), w_gate[0].astype(BF16),
                            w_up[0].astype(BF16), w_down[0].astype(BF16), cw_pad])
    win = _columns_from_chips(gathered[0])
    wout = gathered[1].reshape(D_MODEL, D_MODEL)
    wgate = _columns_from_chips(gathered[2])
    wup = _columns_from_chips(gathered[3])
    wdown = gathered[4].reshape(D_FF, D_MODEL)
    cw = jnp.pad(gathered[5][:, 0:3, :].transpose(1, 0, 2).reshape(3, D_CONV), ((0, 5), (0, 0)))

    head_id = jnp.arange(D_ATTN) // HEAD_DIM
    ones_bd = (head_id[:, None] == head_id[None, :]).astype(BF16)
    gq_t = jnp.tile(g_q, (1, D_ATTN // HEAD_DIM))
    gk_t = jnp.tile(g_k, (1, D_ATTN // HEAD_DIM))

    h1, z = _norm_matmul("in_proj", xs, g_mix, [win], tm, 768, False)
    q, k, v = _qkv_prepare(z, gq_t, gk_t, ones_bd, tm)
    views = [(t // d, d * D_ATTN) for d in DILATIONS]
    bms = [min(512, m) for m, _ in views]
    attn = []
    for (m, w), bm, d in zip(views, bms, DILATIONS):
        o, l = _attn_fwd(f"attn_fwd_d{d}", q.reshape(m, w), k.reshape(m, w), v.reshape(m, w), bm)
        attn += [o.reshape(t, D_ATTN), l.reshape(t, D_ATTN)]
    mix, y_attn, lse = _mixer_out(z, cw, attn, g_conv_out, g_attn_out, tm)
    (x1,) = _matmul("out_proj", mix, wout, [xs], [F32], lambda acc, r: (r + acc,), tm, 512)
    h2, gate, up, act = _norm_matmul("ffn_up", x1, g_ffn, [wgate, wup], tm, 1408, True)

    def loss_epilogue(acc, r, tgt):
        err = r + acc - tgt
        dy = err * (1.0 / D_MODEL)
        return dy, dy, jnp.sum(err * err)

    dx2, dx2b, loss_sum = _matmul("ffn_down_loss", act, wdown, [x1, target], [F32, BF16],
                                  loss_epilogue, tm, 512, loss=True)

    def swiglu_bwd(da, gt, u):
        s = _sigmoid(gt)
        return da * u * (s * (1.0 + gt * (1.0 - s))), da * (gt * s)

    dgate, dup = _matmul("ffn_down_bwd", dx2b, wdown, [gate, up], [BF16, BF16], swiglu_bwd,
                         tm, 1408, transposed_w=True)
    gw_down = _matmul_tn("grad_w_down", act, dx2b, 512, tm)
    gw_gate = _matmul_tn("grad_w_gate", h2, dgate, 1408, tm)
    gw_up = _matmul_tn("grad_w_up", h2, dup, 1408, tm)
    dx1, dx1b, gg_ffn = _matmul_norm_bwd("ffn_up_bwd", [(dgate, wgate), (dup, wup)], x1, dx2, g_ffn,
                                         min(256, t))
    (dmix,) = _matmul("out_proj_bwd", dx1b, wout, [], [F32], lambda acc: (acc,), tm, 512,
                      transposed_w=True)
    gw_out = _matmul_tn("grad_w_out", mix, dx1b, 512, tm)
    dzc, do, dd, gcw, gg_conv, gg_attn = _mixer_bwd(z, dmix, y_attn, cw, g_conv_out, g_attn_out,
                                                    ones_bd, tm)
    dqs, dks, dvs = [], [], []
    for (m, w), bm, d in zip(views, bms, DILATIONS):
        dq, dk, dv = _attn_bwd(f"attn_bwd_d{d}", q.reshape(m, w), k.reshape(m, w), v.reshape(m, w),
                               do.reshape(m, w), lse.reshape(m, w), dd.reshape(m, w), bm)
        dqs.append(dq.reshape(t, D_ATTN))
        dks.append(dk.reshape(t, D_ATTN))
        dvs.append(dv.reshape(t, D_ATTN))
    dz, gg_q, gg_k = _qkv_bwd(z, dzc, dqs, dks, dvs, gq_t, gk_t, ones_bd, tm)
    gw_in = _matmul_tn("grad_w_in", h1, dz, 768, tm)
    grad_x, _, gg_mix = _matmul_norm_bwd("in_proj_bwd", [(dz, win)], xs, dx1, g_mix, min(256, t))

    full = [_columns_to_chips(gw_in), gw_out.reshape(N_CHIPS, D_MODEL // N_CHIPS, D_MODEL),
            _columns_to_chips(gw_gate), _columns_to_chips(gw_up),
            gw_down.reshape(N_CHIPS, D_FF // N_CHIPS, D_MODEL)]
    own, got = _pair_exchange(full)
    pair = [_elementwise(f"pair_sum_{i}", lambda a, b: (a + b,),
                         [a.reshape(-1, a.shape[2]), b.reshape(-1, b.shape[2])], [F32], 256)[0]
            .reshape(a.shape) for i, (a, b) in enumerate(zip(own, got))]
    own, got = _chip_exchange(pair)
    halves = [_elementwise(f"chip_sum_{i}", lambda a, b, c_, d_: (((a + b) + c_) + d_,),
                           [a, b[0], b[1], b[2]], [F32], 256)[0]
              for i, (a, b) in enumerate(zip(own, got))]
    shards = [s.reshape(2 * s.shape[1], s.shape[2]) for s in _pair_share(halves)]

    small = _small_all_reduce({
        "g_mix": gg_mix, "g_ffn": gg_ffn, "g_conv_out": gg_conv, "g_attn_out": gg_attn,
        "g_q": gg_q, "g_k": gg_k, "loss": loss_sum, "conv_w": gcw})
    me = 2 * lax.axis_index("x") + lax.axis_index("y")
    heads = D_ATTN // HEAD_DIM
    grads = {
        "g_mix": small[0:1, :], "g_ffn": small[1:2, :],
        "g_conv_out": small[2:3, 0:512], "g_attn_out": small[2:3, 512:1024],
        "g_q": small[3, 0:512].reshape(heads, HEAD_DIM).sum(axis=0)[None, :],
        "g_k": small[3, 512:1024].reshape(heads, HEAD_DIM).sum(axis=0)[None, :],
        "conv_w": lax.dynamic_slice(small[8:11, 0:512], (0, me * (D_CONV // N_CHIPS)),
                                    (3, D_CONV // N_CHIPS)),
        "w_in": shards[0], "w_out": shards[1], "w_gate": shards[2], "w_up": shards[3],
        "w_down": shards[4],
    }
    loss = small[4, 0] * 0.5 * (1.0 / D_MODEL)

    weights = dict(g_mix=g_mix, w_in=w_in, conv_w=conv_w, g_q=g_q, g_k=g_k, g_conv_out=g_conv_out,
                   g_attn_out=g_attn_out, w_out=w_out, g_ffn=g_ffn, w_gate=w_gate, w_up=w_up,
                   w_down=w_down)
    moments_m = dict(g_mix=m_g_mix, w_in=m_w_in, conv_w=m_conv_w, g_q=m_g_q, g_k=m_g_k,
                     g_conv_out=m_g_conv_out, g_attn_out=m_g_attn_out, w_out=m_w_out, g_ffn=m_g_ffn,
                     w_gate=m_w_gate, w_up=m_w_up, w_down=m_w_down)
    moments_v = dict(g_mix=v_g_mix, w_in=v_w_in, conv_w=v_conv_w, g_q=v_g_q, g_k=v_g_k,
                     g_conv_out=v_g_conv_out, g_attn_out=v_g_attn_out, w_out=v_w_out, g_ffn=v_g_ffn,
                     w_gate=v_w_gate, w_up=v_w_up, w_down=v_w_down)
    names = list(weights)
    out_g, out_d, out_m, out_v = [], [], [], []
    for nme in names:
        wgt = weights[nme]
        shape2 = wgt.shape[-2:] if wgt.ndim == 3 else wgt.shape
        g2 = grads[nme].reshape(shape2)
        dlt, nm, nv = _adamw(f"adamw_{nme}", wgt.reshape(shape2), g2, moments_m[nme].reshape(shape2),
                             moments_v[nme].reshape(shape2))
        out_g.append(g2.reshape(wgt.shape))
        out_d.append(dlt.reshape(wgt.shape))
        out_m.append(nm.reshape(wgt.shape))
        out_v.append(nv.reshape(wgt.shape))
    return (loss, grad_x[None], *out_g, *out_d, *out_m, *out_v)
```

```python
import functools

import jax
import jax.numpy as jnp
from jax import lax
from jax.experimental import pallas as pl
from jax.experimental.pallas import tpu as pltpu

F32 = jnp.float32
BF16 = jnp.bfloat16
MESH = pl.DeviceIdType.MESH

D_MODEL = 1024
D_CONV = 512
D_ATTN = 512
HEAD_DIM = 64
D_FF = 2816
D_IN = 3 * D_CONV + 3 * D_ATTN
DILATIONS = (1, 4, 16)
BAND = 128
EPS = 1e-6
NEG = -1e30
N_CHIPS = 4

ADAM_LR = 0.001
ADAM_B1 = 0.9
ADAM_B2 = 0.999
ADAM_EPS = 1e-08
ADAM_WD = 0.01
ADAM_STEP = 10

V7X_VMEM_BYTES = 64 * 1024 * 1024
VMEM_LIMIT = V7X_VMEM_BYTES - 8 * 1024 * 1024
ANY = pl.BlockSpec(memory_space=pl.ANY)
VMEM_WHOLE = pl.BlockSpec(memory_space=pltpu.VMEM)


def _params(*sem):
    return pltpu.CompilerParams(dimension_semantics=sem, vmem_limit_bytes=VMEM_LIMIT)


def _sds(shape, dtype):
    return jax.ShapeDtypeStruct(shape, dtype)


def _place():
    x, y, c = lax.axis_index("x"), lax.axis_index("y"), lax.axis_index("c")
    chips = [(1 - x, y), (x, 1 - y), (1 - x, 1 - y)]
    return x, y, c, 2 * x + y, chips, [2 * cx + cy for cx, cy in chips]


def _all_gather(shards):
    n = len(shards)

    def body(*refs):
        ins, outs, stage = refs[:n], refs[n:2 * n], refs[2 * n:3 * n]
        ssem, rsem, fsem, gsem, lsem, osem = refs[3 * n:]
        x, y, c, me, chips, cids = _place()
        sib = (x, y, 1 - c)

        def half(w, which):
            h = shards[w].shape[0] // 2
            return pl.ds(pl.multiple_of(which * h, 8), h)

        loads = [pltpu.make_async_copy(ins[w], stage[w], lsem.at[w]) for w in range(n)]
        local = [pltpu.make_async_copy(stage[w], outs[w].at[me], osem.at[w]) for w in range(n)]
        for cp in loads:
            cp.start()

        def chip_copy(w, j, src_slot):
            rows = half(w, c)
            return pltpu.make_async_remote_copy(
                src_ref=ins[w].at[rows], dst_ref=outs[w].at[src_slot, rows],
                send_sem=ssem.at[3 * w + j], recv_sem=rsem.at[3 * w + j],
                device_id=(*chips[j], c), device_id_type=MESH)

        def sib_copy(w, j, which):
            rows = half(w, which)
            return pltpu.make_async_remote_copy(
                src_ref=outs[w].at[cids[j], rows], dst_ref=outs[w].at[cids[j], rows],
                send_sem=fsem.at[3 * w + j], recv_sem=gsem.at[3 * w + j],
                device_id=sib, device_id_type=MESH)

        sends = [chip_copy(w, j, me) for w in range(n) for j in range(3)]
        for cp in sends:
            cp.start()
        for w in range(n):
            loads[w].wait()
            local[w].start()
        passed = []
        for w in range(n):
            for j in range(3):
                chip_copy(w, j, cids[j]).wait_recv()
                cp = sib_copy(w, j, c)
                cp.start()
                passed.append(cp)
        for w in range(n):
            for j in range(3):
                sib_copy(w, j, 1 - c).wait_recv()
        for cp in sends + passed:
            cp.wait_send()
        for cp in local:
            cp.wait()

    return pl.pallas_call(
        body, name="all_gather_weights",
        out_shape=[_sds((N_CHIPS,) + s.shape, s.dtype) for s in shards],
        in_specs=[ANY] * n, out_specs=[ANY] * n,
        scratch_shapes=[pltpu.VMEM(s.shape, s.dtype) for s in shards]
        + [pltpu.SemaphoreType.DMA((3 * n,))] * 4 + [pltpu.SemaphoreType.DMA((n,))] * 2,
        compiler_params=pltpu.CompilerParams(vmem_limit_bytes=VMEM_LIMIT),
    )(*shards)


def _pair_exchange(grads):
    n = len(grads)

    def body(*refs):
        ins, got = refs[:n], refs[n:2 * n]
        ssem, rsem = refs[2 * n:]
        x, y, c, _, _, _ = _place()
        swaps = []
        for w in range(n):
            h = grads[w].shape[1] // 2
            theirs = pl.ds(pl.multiple_of((1 - c) * h, 8), h)
            swaps.append(pltpu.make_async_remote_copy(
                src_ref=ins[w].at[:, theirs, :], dst_ref=got[w],
                send_sem=ssem.at[w], recv_sem=rsem.at[w],
                device_id=(x, y, 1 - c), device_id_type=MESH))
        for cp in swaps:
            cp.start()
        for cp in swaps:
            cp.wait()

    return pl.pallas_call(
        body, name="grad_pair_exchange",
        out_shape=[_sds((N_CHIPS, g.shape[1] // 2, g.shape[2]), g.dtype) for g in grads],
        in_specs=[ANY] * n, out_specs=[ANY] * n,
        scratch_shapes=[pltpu.SemaphoreType.DMA((n,))] * 2,
    )(*grads)


def _chip_exchange(parts):
    n = len(parts)

    def body(*refs):
        ins, got = refs[:n], refs[n:2 * n]
        ssem, rsem = refs[2 * n:]
        _, _, c, _, chips, cids = _place()
        sends = [pltpu.make_async_remote_copy(
            src_ref=ins[w].at[cids[j]], dst_ref=got[w].at[j],
            send_sem=ssem.at[3 * w + j], recv_sem=rsem.at[3 * w + j],
            device_id=(*chips[j], c), device_id_type=MESH) for w in range(n) for j in range(3)]
        for cp in sends:
            cp.start()
        for cp in sends:
            cp.wait()

    return pl.pallas_call(
        body, name="grad_chip_exchange",
        out_shape=[_sds((3,) + p.shape[1:], p.dtype) for p in parts],
        in_specs=[ANY] * n, out_specs=[ANY] * n,
        scratch_shapes=[pltpu.SemaphoreType.DMA((3 * n,))] * 2,
    )(*parts)


def _pair_share(halves):
    n = len(halves)

    def body(*refs):
        ins, got = refs[:n], refs[n:2 * n]
        ssem, rsem = refs[2 * n:]
        x, y, c, _, _, _ = _place()
        swaps = [pltpu.make_async_remote_copy(
            src_ref=ins[w], dst_ref=got[w], send_sem=ssem.at[w], recv_sem=rsem.at[w],
            device_id=(x, y, 1 - c), device_id_type=MESH) for w in range(n)]
        for cp in swaps:
            cp.start()
        for cp in swaps:
            cp.wait()

    return pl.pallas_call(
        body, name="grad_pair_share", out_shape=[_sds(h.shape, h.dtype) for h in halves],
        in_specs=[ANY] * n, out_specs=[ANY] * n,
        scratch_shapes=[pltpu.SemaphoreType.DMA((n,))] * 2,
    )(*halves)


def _row_block(r, want):
    return max(d for d in range(1, min(want, r) + 1) if r % d == 0 and (d % 8 == 0 or d == r))


def _pair_sum(name, full, got, where):
    _, r, n = full.shape
    h = r // 2
    tr = _row_block(h, 256)
    nb = h // tr

    def body(w_ref, a_ref, b_ref, o_ref):
        o_ref[...] = a_ref[...] + b_ref[...]

    blk = pl.BlockSpec((1, tr, n), lambda s, i, w: (s, i, 0))
    return pl.pallas_call(
        body, name=name, out_shape=_sds(got.shape, F32),
        grid_spec=pltpu.PrefetchScalarGridSpec(
            num_scalar_prefetch=1, grid=(N_CHIPS, nb),
            in_specs=[pl.BlockSpec((1, tr, n), lambda s, i, w: (s, w[0] * nb + i, 0)), blk],
            out_specs=blk),
        compiler_params=_params("parallel", "parallel"),
    )(where, full, got)


def _chip_sum(name, pair, got, where):
    _, h, n = pair.shape
    tr = _row_block(h, 256)

    def body(w_ref, a_ref, b0, b1, b2, o_ref):
        o_ref[...] = ((a_ref[0] + b0[0]) + b1[0]) + b2[0]

    def slot(j):
        return pl.BlockSpec((1, tr, n), lambda i, w: (j, i, 0))

    return pl.pallas_call(
        body, name=name, out_shape=_sds((h, n), F32),
        grid_spec=pltpu.PrefetchScalarGridSpec(
            num_scalar_prefetch=1, grid=(h // tr,),
            in_specs=[pl.BlockSpec((1, tr, n), lambda i, w: (w[1], i, 0)), slot(0), slot(1), slot(2)],
            out_specs=pl.BlockSpec((tr, n), lambda i, w: (i, 0))),
        compiler_params=_params("parallel"),
    )(where, pair, got, got, got)


SMALL_ROWS = 16
SMALL_LAYOUT = (
    ("g_mix", 0, 0, 1, 1024), ("g_ffn", 1, 0, 1, 1024), ("g_conv_out", 2, 0, 1, 512),
    ("g_attn_out", 2, 512, 1, 512), ("g_q", 3, 0, 1, 512), ("g_k", 3, 512, 1, 512),
    ("loss", 4, 0, 1, 128), ("conv_w", 8, 0, 8, 512))


def _small_all_reduce(parts):
    names = [s[0] for s in SMALL_LAYOUT]

    def body(*refs):
        ins = refs[:len(names)]
        out_ref, stage, buf, ssem, rsem = refs[len(names):]
        x, y, c, _, _, _ = _place()
        me = 4 * x + 2 * y + c
        stage[...] = jnp.zeros_like(stage)
        for ref, (_, r0, c0, nr, nc) in zip(ins, SMALL_LAYOUT):
            stage[r0:r0 + nr, c0:c0 + nc] = ref[0:nr, :]
        buf[me] = stage[...]
        peers = []
        for d in range(1, 8):
            px = 1 - x if d & 4 else x
            py = 1 - y if d & 2 else y
            pc = 1 - c if d & 1 else c
            peers.append(((px, py, pc), 4 * px + 2 * py + pc))
        sends = [pltpu.make_async_remote_copy(
            src_ref=stage, dst_ref=buf.at[me], send_sem=ssem.at[k], recv_sem=rsem.at[k],
            device_id=peer, device_id_type=MESH) for k, (peer, _) in enumerate(peers)]
        for cp in sends:
            cp.start()
        for k, (peer, pid) in enumerate(peers):
            pltpu.make_async_remote_copy(
                src_ref=stage, dst_ref=buf.at[pid], send_sem=ssem.at[k], recv_sem=rsem.at[k],
                device_id=peer, device_id_type=MESH).wait_recv()
        for cp in sends:
            cp.wait_send()
        acc = buf[0]
        for k in range(1, 8):
            acc = acc + buf[k]
        out_ref[...] = acc

    return pl.pallas_call(
        body, name="small_all_reduce", out_shape=_sds((SMALL_ROWS, 1024), F32),
        in_specs=[VMEM_WHOLE] * len(names), out_specs=VMEM_WHOLE,
        scratch_shapes=[pltpu.VMEM((SMALL_ROWS, 1024), F32), pltpu.VMEM((8, SMALL_ROWS, 1024), F32),
                        pltpu.SemaphoreType.DMA((7,)), pltpu.SemaphoreType.DMA((7,))],
    )(*[parts[k] for k in names])


def _dot(a, b):
    return jnp.dot(a, b, preferred_element_type=F32)


def _dot_nt(a, b):
    return lax.dot_general(a, b, (((1,), (1,)), ((), ())), preferred_element_type=F32)


def _dot_tn(a, b):
    return lax.dot_general(a, b, (((0,), (0,)), ((), ())), preferred_element_type=F32)


def _sigmoid(v):
    return 1.0 / (1.0 + jnp.exp(-v))


def _rms_scale(v):
    return lax.rsqrt(jnp.mean(v * v, axis=-1, keepdims=True) + EPS)


def _rms_bwd(v, r, g, dy):
    vh = v * r
    dh = dy * g
    return r * (dh - vh * jnp.mean(dh * vh, axis=-1, keepdims=True)), vh


def _head_sum(a, ones_bd):
    hi = a.astype(BF16)
    lo = (a - hi.astype(F32)).astype(BF16)
    return _dot(hi, ones_bd) + _dot(lo, ones_bd)


def _head_rms_scale(v, ones_bd):
    return lax.rsqrt(_head_sum(v * v, ones_bd) * (1.0 / HEAD_DIM) + EPS)


def _norm_matmul(name, x, g, ws, tm, tn, swiglu):
    t, d = x.shape
    n = ws[0].shape[1]
    nw = len(ws)

    def body(x_ref, g_ref, *refs):
        w_refs, h_ref, o_refs = refs[:nw], refs[nw], refs[nw + 1:2 * nw + 1]
        hs = refs[-1]

        @pl.when(pl.program_id(1) == 0)
        def _():
            xv = x_ref[...]
            h = (xv * _rms_scale(xv) * g_ref[...]).astype(BF16)
            hs[...] = h
            h_ref[...] = h

        h = hs[...]
        outs = [_dot(h, w[...]) for w in w_refs]
        for o_ref, o in zip(o_refs, outs):
            o_ref[...] = o
        if swiglu:
            refs[2 * nw + 1][...] = (outs[0] * _sigmoid(outs[0]) * outs[1]).astype(BF16)

    row = pl.BlockSpec((tm, d), lambda i, j: (i, 0))
    col = pl.BlockSpec((tm, tn), lambda i, j: (i, j))
    out_shape = [_sds((t, d), BF16)] + [_sds((t, n), F32)] * nw
    out_specs = [row] + [col] * nw
    if swiglu:
        out_shape.append(_sds((t, n), BF16))
        out_specs.append(col)
    return pl.pallas_call(
        body, name=name, grid=(t // tm, n // tn), out_shape=out_shape,
        in_specs=[row, pl.BlockSpec((1, d), lambda i, j: (0, 0))]
        + [pl.BlockSpec((d, tn), lambda i, j: (0, j))] * nw,
        out_specs=out_specs, scratch_shapes=[pltpu.VMEM((tm, d), BF16)],
        compiler_params=_params("parallel", "arbitrary"),
    )(x, g, *ws)


def _matmul(name, a, w, extras, out_dtypes, epilogue, tm, tn, transposed_w=False, loss=False):
    t, k = a.shape
    n = w.shape[0] if transposed_w else w.shape[1]
    ne, no = len(extras), len(out_dtypes)

    def body(a_ref, w_ref, *refs):
        e_refs, o_refs = refs[:ne], refs[ne:]
        acc = _dot_nt(a_ref[...], w_ref[...]) if transposed_w else _dot(a_ref[...], w_ref[...])
        res = epilogue(acc, *[e[...] for e in e_refs])
        for o_ref, r in zip(o_refs[:no], res[:no]):
            o_ref[...] = r.astype(o_ref.dtype)
        if loss:
            first = jnp.logical_and(pl.program_id(0) == 0, pl.program_id(1) == 0)

            @pl.when(first)
            def _():
                o_refs[no][...] = jnp.zeros_like(o_refs[no])

            o_refs[no][...] += res[no]

    col = pl.BlockSpec((tm, tn), lambda i, j: (i, j))
    w_spec = (pl.BlockSpec((tn, k), lambda i, j: (j, 0)) if transposed_w
              else pl.BlockSpec((k, tn), lambda i, j: (0, j)))
    out_shape = [_sds((t, n), dt) for dt in out_dtypes]
    out_specs = [col] * no
    if loss:
        out_shape.append(_sds((8, 128), F32))
        out_specs.append(pl.BlockSpec((8, 128), lambda i, j: (0, 0)))
    return pl.pallas_call(
        body, name=name, grid=(t // tm, n // tn), out_shape=out_shape,
        in_specs=[pl.BlockSpec((tm, k), lambda i, j: (i, 0)), w_spec] + [col] * ne,
        out_specs=out_specs,
        compiler_params=_params(*(("arbitrary", "arbitrary") if loss else ("parallel", "parallel"))),
    )(a, w, *extras)


def _matmul_norm_bwd(name, pairs, x, dres, g, tm):
    t, d = x.shape
    npairs = len(pairs)

    def body(*refs):
        a_refs, w_refs = refs[:npairs], refs[npairs:2 * npairs]
        x_ref, r_ref, g_ref, dx_ref, dxb_ref, dg_ref = refs[2 * npairs:]
        dy = _dot_nt(a_refs[0][...], w_refs[0][...])
        for a_ref, w_ref in zip(a_refs[1:], w_refs[1:]):
            dy = dy + _dot_nt(a_ref[...], w_ref[...])
        xv = x_ref[...]
        dx, xh = _rms_bwd(xv, _rms_scale(xv), g_ref[...], dy)
        dx = dx + r_ref[...]
        dx_ref[...] = dx
        dxb_ref[...] = dx.astype(BF16)

        @pl.when(pl.program_id(0) == 0)
        def _():
            dg_ref[...] = jnp.zeros_like(dg_ref)

        dg_ref[...] += jnp.sum(dy * xh, axis=0, keepdims=True)

    row = pl.BlockSpec((tm, d), lambda i: (i, 0))
    vec = pl.BlockSpec((1, d), lambda i: (0, 0))
    return pl.pallas_call(
        body, name=name, grid=(t // tm,),
        out_shape=[_sds((t, d), F32), _sds((t, d), BF16), _sds((1, d), F32)],
        in_specs=[pl.BlockSpec((tm, a.shape[1]), lambda i: (i, 0)) for a, _ in pairs]
        + [pl.BlockSpec(w.shape, lambda i: (0, 0)) for _, w in pairs] + [row, row, vec],
        out_specs=[row, row, vec],
        compiler_params=_params("arbitrary"),
    )(*[a for a, _ in pairs], *[w for _, w in pairs], x, dres, g)


def _matmul_tn(name, a, g, tn, tk):
    t, ka = a.shape
    n = g.shape[1]

    def body(a_ref, g_ref, o_ref):
        @pl.when(pl.program_id(1) == 0)
        def _():
            o_ref[...] = jnp.zeros_like(o_ref)

        o_ref[...] += _dot_tn(a_ref[...], g_ref[...])

    return pl.pallas_call(
        body, name=name, grid=(n // tn, t // tk), out_shape=_sds((ka, n), F32),
        in_specs=[pl.BlockSpec((tk, ka), lambda j, s: (s, 0)),
                  pl.BlockSpec((tk, tn), lambda j, s: (s, j))],
        out_specs=pl.BlockSpec((ka, tn), lambda j, s: (0, j)),
        compiler_params=_params("parallel", "arbitrary"),
    )(a, g)


def _elementwise(name, fn, ins, out_dtypes, tr):
    r, n = ins[0].shape
    tr = _row_block(r, tr)
    ni = len(ins)

    def body(*refs):
        res = fn(*[ref[...] for ref in refs[:ni]])
        for o_ref, v in zip(refs[ni:], res):
            o_ref[...] = v.astype(o_ref.dtype)

    blk = pl.BlockSpec((tr, n), lambda i: (i, 0))
    return pl.pallas_call(
        body, name=name, grid=(r // tr,), out_shape=[_sds((r, n), dt) for dt in out_dtypes],
        in_specs=[blk] * ni, out_specs=[blk] * len(out_dtypes),
        compiler_params=_params("parallel"),
    )(*ins)


def _adamw_update(w, g, m, v):
    m = ADAM_B1 * m + (1.0 - ADAM_B1) * g
    v = ADAM_B2 * v + (1.0 - ADAM_B2) * (g * g)
    m_hat = m / (1.0 - ADAM_B1 ** ADAM_STEP)
    v_hat = v / (1.0 - ADAM_B2 ** ADAM_STEP)
    return -ADAM_LR * (m_hat / (jnp.sqrt(v_hat) + ADAM_EPS) + ADAM_WD * w), m, v


def _adamw(name, w, g, m, v):
    return _elementwise(name, _adamw_update, [w, g, m, v], [F32] * 3, 256)


def _adamw_shard(name, w, m, v, mine, theirs, where):
    r, n = w.shape
    h = r // 2
    tr = _row_block(h, 256)
    nb = h // tr

    def body(w_ref, p_ref, m_ref, v_ref, a_ref, b_ref, g_ref, d_ref, nm_ref, nv_ref):
        g = jnp.where(pl.program_id(0) == w_ref[0], a_ref[...], b_ref[...])
        g_ref[...] = g
        d_ref[...], nm_ref[...], nv_ref[...] = _adamw_update(p_ref[...], g, m_ref[...], v_ref[...])

    whole = pl.BlockSpec((tr, n), lambda s, i, c: (s * nb + i, 0))
    half = pl.BlockSpec((tr, n), lambda s, i, c: (i, 0))
    return pl.pallas_call(
        body, name=name, out_shape=[_sds((r, n), F32)] * 4,
        grid_spec=pltpu.PrefetchScalarGridSpec(
            num_scalar_prefetch=1, grid=(2, nb), in_specs=[whole] * 3 + [half] * 2,
            out_specs=[whole] * 4),
        compiler_params=_params("parallel", "parallel"),
    )(where, w, m, v, mine, theirs)


def _qkv_prepare(z, gq, gk, ones_bd, tm):
    t = z.shape[0]

    def body(z_ref, gq_ref, gk_ref, bd_ref, q_ref, k_ref, v_ref):
        bd = bd_ref[...]
        q = z_ref[:, 0:512]
        k = z_ref[:, 512:1024]
        q_ref[...] = ((q * _head_rms_scale(q, bd) * gq_ref[...]) * HEAD_DIM ** -0.5).astype(BF16)
        k_ref[...] = (k * _head_rms_scale(k, bd) * gk_ref[...]).astype(BF16)
        v_ref[...] = z_ref[:, 1024:1536].astype(BF16)

    vec = pl.BlockSpec((1, 512), lambda i: (0, 0))
    out = pl.BlockSpec((tm, 512), lambda i: (i, 0))
    return pl.pallas_call(
        body, name="qkv_prepare", grid=(t // tm,), out_shape=[_sds((t, 512), BF16)] * 3,
        in_specs=[pl.BlockSpec((tm, 1536), lambda i: (i, 1)), vec, vec,
                  pl.BlockSpec((512, 512), lambda i: (0, 0))],
        out_specs=[out] * 3, compiler_params=_params("parallel"),
    )(z, gq, gk, ones_bd)


def _band_masks():
    lane = lax.broadcasted_iota(jnp.int32, (BAND, BAND), 1)
    row = lax.broadcasted_iota(jnp.int32, (BAND, BAND), 0)
    head0 = lane < HEAD_DIM
    ones = [jnp.where(head0, 1.0, 0.0).astype(BF16), jnp.where(head0, 0.0, 1.0).astype(BF16)]
    return lane - row, head0, ones


def _attn_fwd(name, qv, kv, vv, bm):
    m, w = qv.shape
    sub = bm // BAND

    def body(q_ref, kp_ref, k_ref, vp_ref, v_ref, o_ref, l_ref):
        i = pl.program_id(1)
        diff, head0, hmask = _band_masks()
        first = jnp.where(i > 0, 0, 2 * BAND)
        for b in range(sub):
            rows = slice(b * BAND, (b + 1) * BAND)
            q = q_ref[rows, :]
            if b == 0:
                kp, vp, lo = kp_ref[...], vp_ref[...], first
            else:
                prev = slice((b - 1) * BAND, b * BAND)
                kp, vp, lo = k_ref[prev, :], v_ref[prev, :], 0
            kc, vc = k_ref[rows, :], v_ref[rows, :]
            mp, mc = diff >= lo, diff <= 0
            o_h, l_h = [], []
            for h in range(2):
                qh = q * hmask[h]
                sp = jnp.where(mp, _dot_nt(qh, kp), NEG)
                sc = jnp.where(mc, _dot_nt(qh, kc), NEG)
                mx = jnp.maximum(jnp.max(sp, axis=-1, keepdims=True),
                                 jnp.max(sc, axis=-1, keepdims=True))
                ep, ec = jnp.exp(sp - mx), jnp.exp(sc - mx)
                den = jnp.sum(ep, axis=-1, keepdims=True) + jnp.sum(ec, axis=-1, keepdims=True)
                o_h.append((_dot(ep.astype(BF16), vp) + _dot(ec.astype(BF16), vc)) / den)
                l_h.append(jnp.broadcast_to(mx + jnp.log(den), (BAND, BAND)))
            o_ref[rows, :] = jnp.where(head0, o_h[0], o_h[1])
            l_ref[rows, :] = jnp.where(head0, l_h[0], l_h[1])

    main = pl.BlockSpec((bm, BAND), lambda j, i: (i, j))
    prev = pl.BlockSpec((BAND, BAND), lambda j, i: (jnp.maximum(i * sub - 1, 0), j))
    return pl.pallas_call(
        body, name=name, grid=(w // BAND, m // bm), out_shape=[_sds((m, w), F32)] * 2,
        in_specs=[main, prev, main, prev, main], out_specs=[main, main],
        compiler_params=_params("parallel", "parallel"),
    )(qv, kv, kv, vv, vv)


def _attn_bwd(name, qv, kv, vv, dov, lv, dv_, bm):
    m, w = qv.shape
    sub = bm // BAND
    nrb = m // bm

    def body(q_ref, qn_ref, kp_ref, k_ref, vp_ref, v_ref, do_ref, don_ref, l_ref, ln_ref,
             d_ref, dn_ref, dq_ref, dk_ref, dv_ref):
        i = pl.program_id(1)
        diff, head0, hmask = _band_masks()
        first = jnp.where(i > 0, 0, 2 * BAND)
        last = jnp.where(i < nrb - 1, 0, 2 * BAND)
        dk_acc = [[None, None] for _ in range(sub)]
        dv_acc = [[None, None] for _ in range(sub)]

        def add(acc, b, h, val):
            acc[b][h] = val if acc[b][h] is None else acc[b][h] + val

        for qb in range(sub + 1):
            inside = qb < sub
            rows = slice(qb * BAND, (qb + 1) * BAND)
            if inside:
                q, do, lse, dd = q_ref[rows, :], do_ref[rows, :], l_ref[rows, :], d_ref[rows, :]
            else:
                q, do, lse, dd = qn_ref[...], don_ref[...], ln_ref[...], dn_ref[...]
            if qb == 0:
                kp, vp, lo = kp_ref[...], vp_ref[...], first
            else:
                prev = slice((qb - 1) * BAND, qb * BAND)
                kp, vp, lo = k_ref[prev, :], v_ref[prev, :], (0 if inside else last)
            mp = diff >= lo
            dq_h = []
            for h in range(2):
                qh = q * hmask[h]
                doh = do * hmask[h]
                lh = lse[:, h * HEAD_DIM:h * HEAD_DIM + 1]
                dh = dd[:, h * HEAD_DIM:h * HEAD_DIM + 1]
                pp = jnp.where(mp, jnp.exp(_dot_nt(qh, kp) - lh), 0.0)
                dsp = pp * (_dot_nt(doh, vp) - dh)
                ppb, dspb = pp.astype(BF16), dsp.astype(BF16)
                if qb > 0:
                    add(dk_acc, qb - 1, h, _dot_tn(dspb, q))
                    add(dv_acc, qb - 1, h, _dot_tn(ppb, do))
                if inside:
                    kc, vc = k_ref[rows, :], v_ref[rows, :]
                    pc = jnp.where(diff <= 0, jnp.exp(_dot_nt(qh, kc) - lh), 0.0)
                    dsc = pc * (_dot_nt(doh, vc) - dh)
                    pcb, dscb = pc.astype(BF16), dsc.astype(BF16)
                    add(dk_acc, qb, h, _dot_tn(dscb, q))
                    add(dv_acc, qb, h, _dot_tn(pcb, do))
                    dq_h.append(_dot(dspb, kp) + _dot(dscb, kc))
            if inside:
                dq_ref[rows, :] = jnp.where(head0, dq_h[0], dq_h[1])
        for b in range(sub):
            rows = slice(b * BAND, (b + 1) * BAND)
            dk_ref[rows, :] = jnp.where(head0, dk_acc[b][0], dk_acc[b][1])
            dv_ref[rows, :] = jnp.where(head0, dv_acc[b][0], dv_acc[b][1])

    nblk = m // BAND
    main = pl.BlockSpec((bm, BAND), lambda j, i: (i, j))
    prev = pl.BlockSpec((BAND, BAND), lambda j, i: (jnp.maximum(i * sub - 1, 0), j))
    nxt = pl.BlockSpec((BAND, BAND), lambda j, i: (jnp.minimum((i + 1) * sub, nblk - 1), j))
    return pl.pallas_call(
        body, name=name, grid=(w // BAND, nrb), out_shape=[_sds((m, w), F32)] * 3,
        in_specs=[main, nxt, prev, main, prev, main, main, nxt, main, nxt, main, nxt],
        out_specs=[main] * 3, compiler_params=_params("parallel", "parallel"),
    )(qv, qv, kv, kv, vv, vv, dov, dov, lv, lv, dv_, dv_)


def _halo_rows(tm, t):
    per = tm // 8
    prev = lambda i: (jnp.maximum(i * per - 1, 0), 0)
    nxt = lambda i: (jnp.minimum((i + 1) * per, t // 8 - 1), 0)
    return prev, nxt


def _mixer_out(z, cw, attn, g_conv, g_attn, tm):
    t = z.shape[0]
    prev, _ = _halo_rows(tm, t)

    def body(z_ref, zp_ref, cw_ref, o1, l1, o2, l2, o3, l3, gc_ref, ga_ref, mix_ref, y_ref, lse_ref):
        i = pl.program_id(0)
        keep = jnp.where(i > 0, 1.0, 0.0)
        cu = jnp.concatenate([zp_ref[:, 0:512] * zp_ref[:, 1024:1536] * keep,
                              z_ref[:, 0:512] * z_ref[:, 1024:1536]], axis=0)
        c = (cw_ref[0:1, :] * pltpu.roll(cu, 2, 0) + cw_ref[1:2, :] * pltpu.roll(cu, 1, 0)
             + cw_ref[2:3, :] * cu)[8:, :]
        yc = z_ref[:, 512:1024] * c
        mix_ref[:, 0:512] = (yc * _rms_scale(yc) * gc_ref[...]).astype(BF16)
        ls = [l1[...], l2[...], l3[...]]
        mx = jnp.maximum(jnp.maximum(ls[0], ls[1]), ls[2])
        es = [jnp.exp(l - mx) for l in ls]
        tot = es[0] + es[1] + es[2]
        ya = (es[0] * o1[...] + es[1] * o2[...] + es[2] * o3[...]) / tot
        y_ref[...] = ya
        lse_ref[...] = mx + jnp.log(tot)
        mix_ref[:, 512:1024] = (ya * _rms_scale(ya) * ga_ref[...]).astype(BF16)

    blk = pl.BlockSpec((tm, 512), lambda i: (i, 0))
    vec = pl.BlockSpec((1, 512), lambda i: (0, 0))
    return pl.pallas_call(
        body, name="mixer_out", grid=(t // tm,),
        out_shape=[_sds((t, 1024), BF16), _sds((t, 512), F32), _sds((t, 512), F32)],
        in_specs=[pl.BlockSpec((tm, 1536), lambda i: (i, 0)), pl.BlockSpec((8, 1536), prev),
                  pl.BlockSpec((8, 512), lambda i: (0, 0))] + [blk] * 6 + [vec, vec],
        out_specs=[pl.BlockSpec((tm, 1024), lambda i: (i, 0)), blk, blk],
        compiler_params=_params("parallel"),
    )(z, z, cw, *attn, g_conv, g_attn)


def _mixer_bwd(z, dmix, y_attn, cw, g_conv, g_attn, ones_bd, tm):
    t = z.shape[0]
    nblk = t // tm
    prev, nxt = _halo_rows(tm, t)
    e = tm + 16

    def body(z_ref, zp_ref, zn_ref, dm_ref, dmn_ref, y_ref, cw_ref, gc_ref, ga_ref, bd_ref,
             dz_ref, do_ref, dd_ref, dcw_ref, dgc_ref, dga_ref):
        i = pl.program_id(0)
        rows = lax.broadcasted_iota(jnp.int32, (e, 1), 0)
        lo = jnp.where(i > 0, 0, 8)
        hi = jnp.where(i < nblk - 1, e, tm + 8)
        ze = jnp.concatenate([zp_ref[...], z_ref[...], zn_ref[...]], axis=0)
        u, gb, gcv = ze[:, 0:512], ze[:, 512:1024], ze[:, 1024:1536]
        w0, w1, w2 = cw_ref[0:1, :], cw_ref[1:2, :], cw_ref[2:3, :]
        cu = jnp.where(rows >= lo, gcv * u, 0.0)
        cu1, cu2 = pltpu.roll(cu, 1, 0), pltpu.roll(cu, 2, 0)
        c = w0 * cu2 + w1 * cu1 + w2 * cu
        yc = gb * c
        dma = jnp.concatenate([jnp.zeros((8, 512), F32), dm_ref[:, 0:512], dmn_ref[...]], axis=0)
        dyc, ych = _rms_bwd(yc, _rms_scale(yc), gc_ref[...], dma)
        dc = jnp.where(jnp.logical_and(rows >= 8, rows < hi), dyc * gb, 0.0)
        dcu = w0 * pltpu.roll(dc, e - 2, 0) + w1 * pltpu.roll(dc, e - 1, 0) + w2 * dc
        mid = slice(8, 8 + tm)
        dz_ref[:, 0:512] = (dcu * gcv)[mid, :].astype(BF16)
        dz_ref[:, 512:1024] = (dyc * c)[mid, :].astype(BF16)
        dz_ref[:, 1024:1536] = (dcu * u)[mid, :].astype(BF16)

        ya = y_ref[...]
        dmb = dm_ref[:, 512:1024]
        dya, yah = _rms_bwd(ya, _rms_scale(ya), ga_ref[...], dmb)
        do_ref[...] = dya.astype(BF16)
        dd_ref[...] = _head_sum(dya * ya, bd_ref[...])

        @pl.when(i == 0)
        def _():
            dcw_ref[...] = jnp.zeros_like(dcw_ref)
            dgc_ref[...] = jnp.zeros_like(dgc_ref)
            dga_ref[...] = jnp.zeros_like(dga_ref)

        dcm = jnp.where(rows < tm + 8, dc, 0.0)
        dcw_ref[0:1, :] += jnp.sum(dcm * cu2, axis=0, keepdims=True)
        dcw_ref[1:2, :] += jnp.sum(dcm * cu1, axis=0, keepdims=True)
        dcw_ref[2:3, :] += jnp.sum(dcm * cu, axis=0, keepdims=True)
        dgc_ref[...] += jnp.sum((dma * ych)[mid, :], axis=0, keepdims=True)
        dga_ref[...] += jnp.sum(dmb * yah, axis=0, keepdims=True)

    blk = pl.BlockSpec((tm, 512), lambda i: (i, 0))
    vec = pl.BlockSpec((1, 512), lambda i: (0, 0))
    cwb = pl.BlockSpec((8, 512), lambda i: (0, 0))
    return pl.pallas_call(
        body, name="mixer_bwd", grid=(nblk,),
        out_shape=[_sds((t, 1536), BF16), _sds((t, 512), BF16), _sds((t, 512), F32),
                   _sds((8, 512), F32), _sds((1, 512), F32), _sds((1, 512), F32)],
        in_specs=[pl.BlockSpec((tm, 1536), lambda i: (i, 0)), pl.BlockSpec((8, 1536), prev),
                  pl.BlockSpec((8, 1536), nxt), pl.BlockSpec((tm, 1024), lambda i: (i, 0)),
                  pl.BlockSpec((8, 512), nxt), blk, cwb, vec, vec,
                  pl.BlockSpec((512, 512), lambda i: (0, 0))],
        out_specs=[pl.BlockSpec((tm, 1536), lambda i: (i, 0)), blk, blk, cwb, vec, vec],
        compiler_params=_params("arbitrary"),
    )(z, z, z, dmix, dmix, y_attn, cw, g_conv, g_attn, ones_bd)


def _qkv_bwd(z, dzc, dqs, dks, dvs, gq, gk, ones_bd, tm):
    t = z.shape[0]

    def body(zq_ref, zk_ref, dzc_ref, q1, q2, q3, k1, k2, k3, v1, v2, v3, gq_ref, gk_ref, bd_ref,
             dz_ref, dgq_ref, dgk_ref):
        bd = bd_ref[...]

        @pl.when(pl.program_id(0) == 0)
        def _():
            dgq_ref[...] = jnp.zeros_like(dgq_ref)
            dgk_ref[...] = jnp.zeros_like(dgk_ref)

        def back(v, dn, g, scale):
            r = _head_rms_scale(v, bd)
            vh = v * r
            dh = dn * (g * scale)
            dv = r * (dh - vh * (_head_sum(dh * vh, bd) * (1.0 / HEAD_DIM)))
            return dv, jnp.sum(dn * scale * vh, axis=0, keepdims=True)

        dq, dgq = back(zq_ref[...], q1[...] + q2[...] + q3[...], gq_ref[...], HEAD_DIM ** -0.5)
        dk, dgk = back(zk_ref[...], k1[...] + k2[...] + k3[...], gk_ref[...], 1.0)
        dgq_ref[...] += dgq
        dgk_ref[...] += dgk
        dz_ref[:, 0:1536] = dzc_ref[...]
        dz_ref[:, 1536:2048] = dq.astype(BF16)
        dz_ref[:, 2048:2560] = dk.astype(BF16)
        dz_ref[:, 2560:3072] = (v1[...] + v2[...] + v3[...]).astype(BF16)

    blk = pl.BlockSpec((tm, 512), lambda i: (i, 0))
    vec = pl.BlockSpec((1, 512), lambda i: (0, 0))
    return pl.pallas_call(
        body, name="qkv_bwd", grid=(t // tm,),
        out_shape=[_sds((t, D_IN), BF16), _sds((1, 512), F32), _sds((1, 512), F32)],
        in_specs=[pl.BlockSpec((tm, 512), lambda i: (i, 3)), pl.BlockSpec((tm, 512), lambda i: (i, 4)),
                  pl.BlockSpec((tm, 1536), lambda i: (i, 0))] + [blk] * 9
        + [vec, vec, pl.BlockSpec((512, 512), lambda i: (0, 0))],
        out_specs=[pl.BlockSpec((tm, D_IN), lambda i: (i, 0)), vec, vec],
        compiler_params=_params("arbitrary"),
    )(z, z, dzc, *dqs, *dks, *dvs, gq, gk, ones_bd)


def _columns_from_chips(g):
    return g.transpose(1, 0, 2).reshape(g.shape[1], N_CHIPS * g.shape[2])


def _columns_to_chips(w):
    k, n4 = w.shape
    return w.reshape(k, N_CHIPS, n4 // N_CHIPS).transpose(1, 0, 2)


def kernel(x, g_mix, w_in, conv_w, g_q, g_k, g_conv_out, g_attn_out, w_out, g_ffn, w_gate, w_up, w_down, loss_target, m_g_mix, m_w_in, m_conv_w, m_g_q, m_g_k, m_g_conv_out, m_g_attn_out, m_w_out, m_g_ffn, m_w_gate, m_w_up, m_w_down, v_g_mix, v_w_in, v_conv_w, v_g_q, v_g_k, v_g_conv_out, v_g_attn_out, v_w_out, v_g_ffn, v_w_gate, v_w_up, v_w_down):
    t = x.shape[1]
    xs = x[0]
    target = loss_target[0]
    tm = min(512, t)

    cw_pad = jnp.pad(conv_w[0], ((0, 13), (0, 0)))
    gathered = _all_gather([w_in[0].astype(BF16), w_out[0].astype(BF16), w_gate[0].astype(BF16),
                            w_up[0].astype(BF16), w_down[0].astype(BF16), cw_pad])
    win = _columns_from_chips(gathered[0])
    wout = gathered[1].reshape(D_MODEL, D_MODEL)
    wgate = _columns_from_chips(gathered[2])
    wup = _columns_from_chips(gathered[3])
    wdown = gathered[4].reshape(D_FF, D_MODEL)
    cw = jnp.pad(gathered[5][:, 0:3, :].transpose(1, 0, 2).reshape(3, D_CONV), ((0, 5), (0, 0)))

    head_id = jnp.arange(D_ATTN) // HEAD_DIM
    ones_bd = (head_id[:, None] == head_id[None, :]).astype(BF16)
    gq_t = jnp.tile(g_q, (1, D_ATTN // HEAD_DIM))
    gk_t = jnp.tile(g_k, (1, D_ATTN // HEAD_DIM))

    h1, z = _norm_matmul("in_proj", xs, g_mix, [win], tm, 768, False)
    q, k, v = _qkv_prepare(z, gq_t, gk_t, ones_bd, tm)
    views = [(t // d, d * D_ATTN) for d in DILATIONS]
    bms = [min(512, m) for m, _ in views]
    attn = []
    for (m, w), bm, d in zip(views, bms, DILATIONS):
        o, l = _attn_fwd(f"attn_fwd_d{d}", q.reshape(m, w), k.reshape(m, w), v.reshape(m, w), bm)
        attn += [o.reshape(t, D_ATTN), l.reshape(t, D_ATTN)]
    mix, y_attn, lse = _mixer_out(z, cw, attn, g_conv_out, g_attn_out, tm)
    (x1,) = _matmul("out_proj", mix, wout, [xs], [F32], lambda acc, r: (r + acc,), tm, 512)
    h2, gate, up, act = _norm_matmul("ffn_up", x1, g_ffn, [wgate, wup], tm, 1408, True)

    def loss_epilogue(acc, r, tgt):
        err = r + acc - tgt
        dy = err * (1.0 / D_MODEL)
        return dy, dy, jnp.sum(err * err)

    dx2, dx2b, loss_sum = _matmul("ffn_down_loss", act, wdown, [x1, target], [F32, BF16],
                                  loss_epilogue, tm, 512, loss=True)

    def swiglu_bwd(da, gt, u):
        s = _sigmoid(gt)
        return da * u * (s * (1.0 + gt * (1.0 - s))), da * (gt * s)

    dgate, dup = _matmul("ffn_down_bwd", dx2b, wdown, [gate, up], [BF16, BF16], swiglu_bwd,
                         tm, 1408, transposed_w=True)
    gw_down = _matmul_tn("grad_w_down", act, dx2b, 512, tm)
    gw_gate = _matmul_tn("grad_w_gate", h2, dgate, 1408, tm)
    gw_up = _matmul_tn("grad_w_up", h2, dup, 1408, tm)
    dx1, dx1b, gg_ffn = _matmul_norm_bwd("ffn_up_bwd", [(dgate, wgate), (dup, wup)], x1, dx2, g_ffn,
                                         min(256, t))
    (dmix,) = _matmul("out_proj_bwd", dx1b, wout, [], [F32], lambda acc: (acc,), tm, 512,
                      transposed_w=True)
    gw_out = _matmul_tn("grad_w_out", mix, dx1b, 512, tm)
    dzc, do, dd, gcw, gg_conv, gg_attn = _mixer_bwd(z, dmix, y_attn, cw, g_conv_out, g_attn_out,
                                                    ones_bd, tm)
    dqs, dks, dvs = [], [], []
    for (m, w), bm, d in zip(views, bms, DILATIONS):
        dq, dk, dv = _attn_bwd(f"attn_bwd_d{d}", q.reshape(m, w), k.reshape(m, w), v.reshape(m, w),
                               do.reshape(m, w), lse.reshape(m, w), dd.reshape(m, w), bm)
        dqs.append(dq.reshape(t, D_ATTN))
        dks.append(dk.reshape(t, D_ATTN))
        dvs.append(dv.reshape(t, D_ATTN))
    dz, gg_q, gg_k = _qkv_bwd(z, dzc, dqs, dks, dvs, gq_t, gk_t, ones_bd, tm)
    gw_in = _matmul_tn("grad_w_in", h1, dz, 768, tm)
    grad_x, _, gg_mix = _matmul_norm_bwd("in_proj_bwd", [(dz, win)], xs, dx1, g_mix, min(256, t))

    full = [_columns_to_chips(gw_in), gw_out.reshape(N_CHIPS, D_MODEL // N_CHIPS, D_MODEL),
            _columns_to_chips(gw_gate), _columns_to_chips(gw_up),
            gw_down.reshape(N_CHIPS, D_FF // N_CHIPS, D_MODEL)]
    me = 2 * lax.axis_index("x") + lax.axis_index("y")
    where = jnp.stack([lax.axis_index("c"), me]).astype(jnp.int32)
    big = ["w_in", "w_out", "w_gate", "w_up", "w_down"]
    got = _pair_exchange(full)
    pair = [_pair_sum(f"pair_sum_{nme}", a, b, where) for nme, a, b in zip(big, full, got)]
    got = _chip_exchange(pair)
    mine = [_chip_sum(f"chip_sum_{nme}", a, b, where) for nme, a, b in zip(big, pair, got)]
    theirs = _pair_share(mine)

    small = _small_all_reduce({
        "g_mix": gg_mix, "g_ffn": gg_ffn, "g_conv_out": gg_conv, "g_attn_out": gg_attn,
        "g_q": gg_q, "g_k": gg_k, "loss": loss_sum, "conv_w": gcw})
    heads = D_ATTN // HEAD_DIM
    grads = {
        "g_mix": small[0:1, :], "g_ffn": small[1:2, :],
        "g_conv_out": small[2:3, 0:512], "g_attn_out": small[2:3, 512:1024],
        "g_q": small[3, 0:512].reshape(heads, HEAD_DIM).sum(axis=0)[None, :],
        "g_k": small[3, 512:1024].reshape(heads, HEAD_DIM).sum(axis=0)[None, :],
        "conv_w": lax.dynamic_slice(small[8:11, 0:512], (0, me * (D_CONV // N_CHIPS)),
                                    (3, D_CONV // N_CHIPS)),
    }
    halves = dict(zip(big, zip(mine, theirs)))
    loss = small[4, 0] * 0.5 * (1.0 / D_MODEL)

    weights = dict(g_mix=g_mix, w_in=w_in, conv_w=conv_w, g_q=g_q, g_k=g_k, g_conv_out=g_conv_out,
                   g_attn_out=g_attn_out, w_out=w_out, g_ffn=g_ffn, w_gate=w_gate, w_up=w_up,
                   w_down=w_down)
    moments_m = dict(g_mix=m_g_mix, w_in=m_w_in, conv_w=m_conv_w, g_q=m_g_q, g_k=m_g_k,
                     g_conv_out=m_g_conv_out, g_attn_out=m_g_attn_out, w_out=m_w_out, g_ffn=m_g_ffn,
                     w_gate=m_w_gate, w_up=m_w_up, w_down=m_w_down)
    moments_v = dict(g_mix=v_g_mix, w_in=v_w_in, conv_w=v_conv_w, g_q=v_g_q, g_k=v_g_k,
                     g_conv_out=v_g_conv_out, g_attn_out=v_g_attn_out, w_out=v_w_out, g_ffn=v_g_ffn,
                     w_gate=v_w_gate, w_up=v_w_up, w_down=v_w_down)
    names = list(weights)
    out_g, out_d, out_m, out_v = [], [], [], []
    for nme in names:
        wgt = weights[nme]
        shape2 = wgt.shape[-2:] if wgt.ndim == 3 else wgt.shape
        state = (wgt.reshape(shape2), moments_m[nme].reshape(shape2), moments_v[nme].reshape(shape2))
        if nme in halves:
            g2, dlt, nm, nv = _adamw_shard(f"adamw_{nme}", *state, *halves[nme], where)
        else:
            g2 = grads[nme].reshape(shape2)
            dlt, nm, nv = _adamw(f"adamw_{nme}", state[0], g2, state[1], state[2])
        out_g.append(g2.reshape(wgt.shape))
        out_d.append(dlt.reshape(wgt.shape))
        out_m.append(nm.reshape(wgt.shape))
        out_v.append(nv.reshape(wgt.shape))
    return (loss, grad_x[None], *out_g, *out_d, *out_m, *out_v)
```

```python
import functools

import jax
import jax.numpy as jnp
from jax import lax
from jax.experimental import pallas as pl
from jax.experimental.pallas import tpu as pltpu

F32 = jnp.float32
BF16 = jnp.bfloat16
MESH = pl.DeviceIdType.MESH

D_MODEL = 1024
D_CONV = 512
D_ATTN = 512
HEAD_DIM = 64
D_FF = 2816
D_IN = 3 * D_CONV + 3 * D_ATTN
DILATIONS = (1, 4, 16)
BAND = 128
EPS = 1e-6
NEG = -1e30
N_CHIPS = 4

ADAM_LR = 0.001
ADAM_B1 = 0.9
ADAM_B2 = 0.999
ADAM_EPS = 1e-08
ADAM_WD = 0.01
ADAM_STEP = 10

V7X_VMEM_BYTES = 64 * 1024 * 1024
VMEM_LIMIT = V7X_VMEM_BYTES - 8 * 1024 * 1024
ANY = pl.BlockSpec(memory_space=pl.ANY)
VMEM_WHOLE = pl.BlockSpec(memory_space=pltpu.VMEM)


def _params(*sem):
    return pltpu.CompilerParams(dimension_semantics=sem, vmem_limit_bytes=VMEM_LIMIT)


def _sds(shape, dtype):
    return jax.ShapeDtypeStruct(shape, dtype)


def _place():
    x, y, c = lax.axis_index("x"), lax.axis_index("y"), lax.axis_index("c")
    chips = [(1 - x, y), (x, 1 - y), (1 - x, 1 - y)]
    return x, y, c, 2 * x + y, chips, [2 * cx + cy for cx, cy in chips]


def _all_gather(shards):
    n = len(shards)

    def body(*refs):
        ins, outs, stage = refs[:n], refs[n:2 * n], refs[2 * n:3 * n]
        ssem, rsem, fsem, gsem, lsem, osem = refs[3 * n:]
        x, y, c, me, chips, cids = _place()
        sib = (x, y, 1 - c)

        def half(w, which):
            h = shards[w].shape[0] // 2
            return pl.ds(pl.multiple_of(which * h, 8), h)

        loads = [pltpu.make_async_copy(ins[w], stage[w], lsem.at[w]) for w in range(n)]
        local = [pltpu.make_async_copy(stage[w], outs[w].at[me], osem.at[w]) for w in range(n)]
        for cp in loads:
            cp.start()

        def chip_copy(w, j, src_slot):
            rows = half(w, c)
            return pltpu.make_async_remote_copy(
                src_ref=ins[w].at[rows], dst_ref=outs[w].at[src_slot, rows],
                send_sem=ssem.at[3 * w + j], recv_sem=rsem.at[3 * w + j],
                device_id=(*chips[j], c), device_id_type=MESH)

        def sib_copy(w, j, which):
            rows = half(w, which)
            return pltpu.make_async_remote_copy(
                src_ref=outs[w].at[cids[j], rows], dst_ref=outs[w].at[cids[j], rows],
                send_sem=fsem.at[3 * w + j], recv_sem=gsem.at[3 * w + j],
                device_id=sib, device_id_type=MESH)

        sends = [chip_copy(w, j, me) for w in range(n) for j in range(3)]
        for cp in sends:
            cp.start()
        for w in range(n):
            loads[w].wait()
            local[w].start()
        passed = []
        for w in range(n):
            for j in range(3):
                chip_copy(w, j, cids[j]).wait_recv()
                cp = sib_copy(w, j, c)
                cp.start()
                passed.append(cp)
        for w in range(n):
            for j in range(3):
                sib_copy(w, j, 1 - c).wait_recv()
        for cp in sends + passed:
            cp.wait_send()
        for cp in local:
            cp.wait()

    return pl.pallas_call(
        body, name="all_gather_weights",
        out_shape=[_sds((N_CHIPS,) + s.shape, s.dtype) for s in shards],
        in_specs=[ANY] * n, out_specs=[ANY] * n,
        scratch_shapes=[pltpu.VMEM(s.shape, s.dtype) for s in shards]
        + [pltpu.SemaphoreType.DMA((3 * n,))] * 4 + [pltpu.SemaphoreType.DMA((n,))] * 2,
        compiler_params=pltpu.CompilerParams(vmem_limit_bytes=VMEM_LIMIT),
    )(*shards)


def _pair_exchange(grads):
    n = len(grads)

    def body(*refs):
        ins, got = refs[:n], refs[n:2 * n]
        ssem, rsem = refs[2 * n:]
        x, y, c, _, _, _ = _place()
        swaps = []
        for w in range(n):
            h = grads[w].shape[1] // 2
            theirs = pl.ds(pl.multiple_of((1 - c) * h, 8), h)
            swaps.append(pltpu.make_async_remote_copy(
                src_ref=ins[w].at[:, theirs, :], dst_ref=got[w],
                send_sem=ssem.at[w], recv_sem=rsem.at[w],
                device_id=(x, y, 1 - c), device_id_type=MESH))
        for cp in swaps:
            cp.start()
        for cp in swaps:
            cp.wait()

    return pl.pallas_call(
        body, name="grad_pair_exchange",
        out_shape=[_sds((N_CHIPS, g.shape[1] // 2, g.shape[2]), g.dtype) for g in grads],
        in_specs=[ANY] * n, out_specs=[ANY] * n,
        scratch_shapes=[pltpu.SemaphoreType.DMA((n,))] * 2,
    )(*grads)


def _chip_exchange(parts):
    n = len(parts)

    def body(*refs):
        ins, got = refs[:n], refs[n:2 * n]
        ssem, rsem = refs[2 * n:]
        _, _, c, _, chips, cids = _place()
        sends = [pltpu.make_async_remote_copy(
            src_ref=ins[w].at[cids[j]], dst_ref=got[w].at[j],
            send_sem=ssem.at[3 * w + j], recv_sem=rsem.at[3 * w + j],
            device_id=(*chips[j], c), device_id_type=MESH) for w in range(n) for j in range(3)]
        for cp in sends:
            cp.start()
        for cp in sends:
            cp.wait()

    return pl.pallas_call(
        body, name="grad_chip_exchange",
        out_shape=[_sds((3,) + p.shape[1:], p.dtype) for p in parts],
        in_specs=[ANY] * n, out_specs=[ANY] * n,
        scratch_shapes=[pltpu.SemaphoreType.DMA((3 * n,))] * 2,
    )(*parts)


def _pair_share(halves):
    n = len(halves)

    def body(*refs):
        ins, got = refs[:n], refs[n:2 * n]
        ssem, rsem = refs[2 * n:]
        x, y, c, _, _, _ = _place()
        swaps = [pltpu.make_async_remote_copy(
            src_ref=ins[w], dst_ref=got[w], send_sem=ssem.at[w], recv_sem=rsem.at[w],
            device_id=(x, y, 1 - c), device_id_type=MESH) for w in range(n)]
        for cp in swaps:
            cp.start()
        for cp in swaps:
            cp.wait()

    return pl.pallas_call(
        body, name="grad_pair_share", out_shape=[_sds(h.shape, h.dtype) for h in halves],
        in_specs=[ANY] * n, out_specs=[ANY] * n,
        scratch_shapes=[pltpu.SemaphoreType.DMA((n,))] * 2,
    )(*halves)


def _row_block(r, want):
    return max(d for d in range(1, min(want, r) + 1) if r % d == 0 and (d % 8 == 0 or d == r))


def _pair_sum(name, full, got, where):
    _, r, n = full.shape
    h = r // 2
    tr = _row_block(h, 256)
    nb = h // tr

    def body(w_ref, a_ref, b_ref, o_ref):
        o_ref[...] = a_ref[...] + b_ref[...]

    blk = pl.BlockSpec((1, tr, n), lambda s, i, w: (s, i, 0))
    return pl.pallas_call(
        body, name=name, out_shape=_sds(got.shape, F32),
        grid_spec=pltpu.PrefetchScalarGridSpec(
            num_scalar_prefetch=1, grid=(N_CHIPS, nb),
            in_specs=[pl.BlockSpec((1, tr, n), lambda s, i, w: (s, w[0] * nb + i, 0)), blk],
            out_specs=blk),
        compiler_params=_params("parallel", "parallel"),
    )(where, full, got)


def _chip_sum(name, pair, got, where):
    _, h, n = pair.shape
    tr = _row_block(h, 256)

    def body(w_ref, a_ref, b0, b1, b2, o_ref):
        o_ref[...] = ((a_ref[0] + b0[0]) + b1[0]) + b2[0]

    def slot(j):
        return pl.BlockSpec((1, tr, n), lambda i, w: (j, i, 0))

    return pl.pallas_call(
        body, name=name, out_shape=_sds((h, n), F32),
        grid_spec=pltpu.PrefetchScalarGridSpec(
            num_scalar_prefetch=1, grid=(h // tr,),
            in_specs=[pl.BlockSpec((1, tr, n), lambda i, w: (w[1], i, 0)), slot(0), slot(1), slot(2)],
            out_specs=pl.BlockSpec((tr, n), lambda i, w: (i, 0))),
        compiler_params=_params("parallel"),
    )(where, pair, got, got, got)


SMALL_ROWS = 16
SMALL_LAYOUT = (
    ("g_mix", 0, 0, 1, 1024), ("g_ffn", 1, 0, 1, 1024), ("g_conv_out", 2, 0, 1, 512),
    ("g_attn_out", 2, 512, 1, 512), ("g_q", 3, 0, 1, 512), ("g_k", 3, 512, 1, 512),
    ("loss", 4, 0, 1, 128), ("conv_w", 8, 0, 8, 512))


def _small_all_reduce(parts):
    names = [s[0] for s in SMALL_LAYOUT]

    def body(*refs):
        ins = refs[:len(names)]
        out_ref, stage, buf, ssem, rsem = refs[len(names):]
        x, y, c, _, _, _ = _place()
        me = 4 * x + 2 * y + c
        stage[...] = jnp.zeros_like(stage)
        for ref, (_, r0, c0, nr, nc) in zip(ins, SMALL_LAYOUT):
            stage[r0:r0 + nr, c0:c0 + nc] = ref[0:nr, :]
        buf[me] = stage[...]
        peers = []
        for d in range(1, 8):
            px = 1 - x if d & 4 else x
            py = 1 - y if d & 2 else y
            pc = 1 - c if d & 1 else c
            peers.append(((px, py, pc), 4 * px + 2 * py + pc))
        sends = [pltpu.make_async_remote_copy(
            src_ref=stage, dst_ref=buf.at[me], send_sem=ssem.at[k], recv_sem=rsem.at[k],
            device_id=peer, device_id_type=MESH) for k, (peer, _) in enumerate(peers)]
        for cp in sends:
            cp.start()
        for k, (peer, pid) in enumerate(peers):
            pltpu.make_async_remote_copy(
                src_ref=stage, dst_ref=buf.at[pid], send_sem=ssem.at[k], recv_sem=rsem.at[k],
                device_id=peer, device_id_type=MESH).wait_recv()
        for cp in sends:
            cp.wait_send()
        acc = buf[0]
        for k in range(1, 8):
            acc = acc + buf[k]
        out_ref[...] = acc

    return pl.pallas_call(
        body, name="small_all_reduce", out_shape=_sds((SMALL_ROWS, 1024), F32),
        in_specs=[VMEM_WHOLE] * len(names), out_specs=VMEM_WHOLE,
        scratch_shapes=[pltpu.VMEM((SMALL_ROWS, 1024), F32), pltpu.VMEM((8, SMALL_ROWS, 1024), F32),
                        pltpu.SemaphoreType.DMA((7,)), pltpu.SemaphoreType.DMA((7,))],
    )(*[parts[k] for k in names])


def _dot(a, b):
    return jnp.dot(a, b, preferred_element_type=F32)


def _dot_nt(a, b):
    return lax.dot_general(a, b, (((1,), (1,)), ((), ())), preferred_element_type=F32)


def _dot_tn(a, b):
    return lax.dot_general(a, b, (((0,), (0,)), ((), ())), preferred_element_type=F32)


def _sigmoid(v):
    return 1.0 / (1.0 + jnp.exp(-v))


def _rms_scale(v):
    return lax.rsqrt(jnp.mean(v * v, axis=-1, keepdims=True) + EPS)


def _rms_bwd(v, r, g, dy):
    vh = v * r
    dh = dy * g
    return r * (dh - vh * jnp.mean(dh * vh, axis=-1, keepdims=True)), vh


def _head_sum(a, ones_bd):
    hi = a.astype(BF16)
    lo = (a - hi.astype(F32)).astype(BF16)
    return _dot(hi, ones_bd) + _dot(lo, ones_bd)


def _head_rms_scale(v, ones_bd):
    return lax.rsqrt(_head_sum(v * v, ones_bd) * (1.0 / HEAD_DIM) + EPS)


def _norm_matmul(name, x, g, ws, tm, tn, swiglu):
    t, d = x.shape
    n = ws[0].shape[1]
    nw = len(ws)

    def body(x_ref, g_ref, *refs):
        w_refs, h_ref, o_refs = refs[:nw], refs[nw], refs[nw + 1:2 * nw + 1]
        hs = refs[-1]

        @pl.when(pl.program_id(1) == 0)
        def _():
            xv = x_ref[...]
            h = (xv * _rms_scale(xv) * g_ref[...]).astype(BF16)
            hs[...] = h
            h_ref[...] = h

        h = hs[...]
        outs = [_dot(h, w[...]) for w in w_refs]
        for o_ref, o in zip(o_refs, outs):
            o_ref[...] = o
        if swiglu:
            refs[2 * nw + 1][...] = (outs[0] * _sigmoid(outs[0]) * outs[1]).astype(BF16)

    row = pl.BlockSpec((tm, d), lambda i, j: (i, 0))
    col = pl.BlockSpec((tm, tn), lambda i, j: (i, j))
    out_shape = [_sds((t, d), BF16)] + [_sds((t, n), F32)] * nw
    out_specs = [row] + [col] * nw
    if swiglu:
        out_shape.append(_sds((t, n), BF16))
        out_specs.append(col)
    return pl.pallas_call(
        body, name=name, grid=(t // tm, n // tn), out_shape=out_shape,
        in_specs=[row, pl.BlockSpec((1, d), lambda i, j: (0, 0))]
        + [pl.BlockSpec((d, tn), lambda i, j: (0, j))] * nw,
        out_specs=out_specs, scratch_shapes=[pltpu.VMEM((tm, d), BF16)],
        compiler_params=_params("parallel", "arbitrary"),
    )(x, g, *ws)


def _matmul(name, a, w, extras, out_dtypes, epilogue, tm, tn, transposed_w=False, loss=False):
    t, k = a.shape
    n = w.shape[0] if transposed_w else w.shape[1]
    ne, no = len(extras), len(out_dtypes)

    def body(a_ref, w_ref, *refs):
        e_refs, o_refs = refs[:ne], refs[ne:]
        acc = _dot_nt(a_ref[...], w_ref[...]) if transposed_w else _dot(a_ref[...], w_ref[...])
        res = epilogue(acc, *[e[...] for e in e_refs])
        for o_ref, r in zip(o_refs[:no], res[:no]):
            o_ref[...] = r.astype(o_ref.dtype)
        if loss:
            first = jnp.logical_and(pl.program_id(0) == 0, pl.program_id(1) == 0)

            @pl.when(first)
            def _():
                o_refs[no][...] = jnp.zeros_like(o_refs[no])

            o_refs[no][...] += res[no]

    col = pl.BlockSpec((tm, tn), lambda i, j: (i, j))
    w_spec = (pl.BlockSpec((tn, k), lambda i, j: (j, 0)) if transposed_w
              else pl.BlockSpec((k, tn), lambda i, j: (0, j)))
    out_shape = [_sds((t, n), dt) for dt in out_dtypes]
    out_specs = [col] * no
    if loss:
        out_shape.append(_sds((8, 128), F32))
        out_specs.append(pl.BlockSpec((8, 128), lambda i, j: (0, 0)))
    return pl.pallas_call(
        body, name=name, grid=(t // tm, n // tn), out_shape=out_shape,
        in_specs=[pl.BlockSpec((tm, k), lambda i, j: (i, 0)), w_spec] + [col] * ne,
        out_specs=out_specs,
        compiler_params=_params(*(("arbitrary", "arbitrary") if loss else ("parallel", "parallel"))),
    )(a, w, *extras)


def _matmul_norm_bwd(name, pairs, x, dres, g, tm):
    t, d = x.shape
    npairs = len(pairs)

    def body(*refs):
        a_refs, w_refs = refs[:npairs], refs[npairs:2 * npairs]
        x_ref, r_ref, g_ref, dx_ref, dxb_ref, dg_ref = refs[2 * npairs:]
        dy = _dot_nt(a_refs[0][...], w_refs[0][...])
        for a_ref, w_ref in zip(a_refs[1:], w_refs[1:]):
            dy = dy + _dot_nt(a_ref[...], w_ref[...])
        xv = x_ref[...]
        dx, xh = _rms_bwd(xv, _rms_scale(xv), g_ref[...], dy)
        dx = dx + r_ref[...]
        dx_ref[...] = dx
        dxb_ref[...] = dx.astype(BF16)

        @pl.when(pl.program_id(0) == 0)
        def _():
            dg_ref[...] = jnp.zeros_like(dg_ref)

        dg_ref[...] += jnp.sum(dy * xh, axis=0, keepdims=True)

    row = pl.BlockSpec((tm, d), lambda i: (i, 0))
    vec = pl.BlockSpec((1, d), lambda i: (0, 0))
    return pl.pallas_call(
        body, name=name, grid=(t // tm,),
        out_shape=[_sds((t, d), F32), _sds((t, d), BF16), _sds((1, d), F32)],
        in_specs=[pl.BlockSpec((tm, a.shape[1]), lambda i: (i, 0)) for a, _ in pairs]
        + [pl.BlockSpec(w.shape, lambda i: (0, 0)) for _, w in pairs] + [row, row, vec],
        out_specs=[row, row, vec],
        compiler_params=_params("arbitrary"),
    )(*[a for a, _ in pairs], *[w for _, w in pairs], x, dres, g)


def _matmul_tn(name, a, g, tn, tk):
    t, ka = a.shape
    n = g.shape[1]

    def body(a_ref, g_ref, o_ref):
        @pl.when(pl.program_id(1) == 0)
        def _():
            o_ref[...] = jnp.zeros_like(o_ref)

        o_ref[...] += _dot_tn(a_ref[...], g_ref[...])

    return pl.pallas_call(
        body, name=name, grid=(n // tn, t // tk), out_shape=_sds((ka, n), F32),
        in_specs=[pl.BlockSpec((tk, ka), lambda j, s: (s, 0)),
                  pl.BlockSpec((tk, tn), lambda j, s: (s, j))],
        out_specs=pl.BlockSpec((ka, tn), lambda j, s: (0, j)),
        compiler_params=_params("parallel", "arbitrary"),
    )(a, g)


def _elementwise(name, fn, ins, out_dtypes, tr):
    r, n = ins[0].shape
    tr = _row_block(r, tr)
    ni = len(ins)

    def body(*refs):
        res = fn(*[ref[...] for ref in refs[:ni]])
        for o_ref, v in zip(refs[ni:], res):
            o_ref[...] = v.astype(o_ref.dtype)

    blk = pl.BlockSpec((tr, n), lambda i: (i, 0))
    return pl.pallas_call(
        body, name=name, grid=(r // tr,), out_shape=[_sds((r, n), dt) for dt in out_dtypes],
        in_specs=[blk] * ni, out_specs=[blk] * len(out_dtypes),
        compiler_params=_params("parallel"),
    )(*ins)


def _adamw_update(w, g, m, v):
    m = ADAM_B1 * m + (1.0 - ADAM_B1) * g
    v = ADAM_B2 * v + (1.0 - ADAM_B2) * (g * g)
    m_hat = m / (1.0 - ADAM_B1 ** ADAM_STEP)
    v_hat = v / (1.0 - ADAM_B2 ** ADAM_STEP)
    return -ADAM_LR * (m_hat / (jnp.sqrt(v_hat) + ADAM_EPS) + ADAM_WD * w), m, v


def _adamw(name, w, g, m, v):
    return _elementwise(name, _adamw_update, [w, g, m, v], [F32] * 3, 256)


def _adamw_shard(name, w, m, v, mine, theirs, where):
    r, n = w.shape
    h = r // 2
    tr = _row_block(h, 256)
    nb = h // tr

    def body(w_ref, p_ref, m_ref, v_ref, a_ref, b_ref, g_ref, d_ref, nm_ref, nv_ref):
        g = jnp.where(pl.program_id(0) == w_ref[0], a_ref[...], b_ref[...])
        g_ref[...] = g
        d_ref[...], nm_ref[...], nv_ref[...] = _adamw_update(p_ref[...], g, m_ref[...], v_ref[...])

    whole = pl.BlockSpec((tr, n), lambda s, i, c: (s * nb + i, 0))
    half = pl.BlockSpec((tr, n), lambda s, i, c: (i, 0))
    return pl.pallas_call(
        body, name=name, out_shape=[_sds((r, n), F32)] * 4,
        grid_spec=pltpu.PrefetchScalarGridSpec(
            num_scalar_prefetch=1, grid=(2, nb), in_specs=[whole] * 3 + [half] * 2,
            out_specs=[whole] * 4),
        compiler_params=_params("parallel", "parallel"),
    )(where, w, m, v, mine, theirs)


def _qkv_prepare(z, gq, gk, ones_bd, tm):
    t = z.shape[0]

    def body(zq_ref, zk_ref, gq_ref, gk_ref, bd_ref, q_ref, k_ref):
        bd = bd_ref[...]
        q = zq_ref[...]
        k = zk_ref[...]
        q_ref[...] = (q * _head_rms_scale(q, bd) * gq_ref[...]) * HEAD_DIM ** -0.5
        k_ref[...] = k * _head_rms_scale(k, bd) * gk_ref[...]

    vec = pl.BlockSpec((1, 512), lambda i: (0, 0))
    out = pl.BlockSpec((tm, 512), lambda i: (i, 0))
    return pl.pallas_call(
        body, name="qkv_prepare", grid=(t // tm,), out_shape=[_sds((t, 512), F32)] * 2,
        in_specs=[pl.BlockSpec((tm, 512), lambda i: (i, 3)), pl.BlockSpec((tm, 512), lambda i: (i, 4)),
                  vec, vec, pl.BlockSpec((512, 512), lambda i: (0, 0))],
        out_specs=[out] * 2, compiler_params=_params("parallel"),
    )(z, z, gq, gk, ones_bd)


TOK = 2048
UNITS = TOK // BAND


def _stack_masks():
    row = lax.broadcasted_iota(jnp.int32, (2 * BAND, 2 * BAND), 0) & (BAND - 1)
    col = lax.broadcasted_iota(jnp.int32, (2 * BAND, 2 * BAND), 1)
    lane = lax.broadcasted_iota(jnp.int32, (BAND, BAND), 1)
    head0 = lane < HEAD_DIM
    ones = [jnp.where(head0, 1.0, 0.0).astype(BF16), jnp.where(head0, 0.0, 1.0).astype(BF16)]
    return col - row, col, head0, ones


def _gather(srcs, dst, d):
    per = TOK // d
    n = len(srcs)
    for r in range(d):
        for s, src in enumerate(srcs):
            rows = src[pl.ds(r, per, stride=d), :] if d > 1 else src[...]
            dst[pl.ds((r * n + s) * per, per), :] = rows.astype(dst.dtype)


def _scatter_add(out_ref, src, d, per_src, offset, first):
    per = TOK // d
    if d == 1:
        val = src[pl.ds(offset, per), :]
        out_ref[...] = val if first else out_ref[...] + val
        return
    for r in range(d):
        val = src[pl.ds(r * per_src + offset, per), :]
        idx = pl.ds(r, per, stride=d)
        out_ref[idx, :] = val if first else out_ref[idx, :] + val


def _attn_fwd(q, k, v, v_col):
    t = q.shape[0]
    nblk = t // TOK

    def body(q_ref, kp_ref, k_ref, vp_ref, v_ref, y_ref, l_ref, qs, ks, vs, ob, lb, on, ln):
        i = pl.program_id(1)
        diff, col, head0, hm = _stack_masks()
        band_ok = jnp.logical_and(diff >= 0, diff <= BAND)
        for g, d in enumerate(DILATIONS):
            per = TOK // d
            nb = per // BAND
            _gather([q_ref], qs, d)
            _gather([kp_ref, k_ref], ks, d)
            _gather([vp_ref, v_ref], vs, d)

            def unit(u, carry):
                r, b = u // nb, u % nb
                qu = qs[pl.ds(pl.multiple_of(u * BAND, BAND), BAND), :]
                start = pl.multiple_of(r * 2 * per + per + (b - 1) * BAND, BAND)
                kw = ks[pl.ds(start, 2 * BAND), :]
                vw = vs[pl.ds(start, 2 * BAND), :]
                lo = jnp.where(jnp.logical_and(i == 0, b == 0), BAND, 0)
                s = _dot_nt(jnp.concatenate([qu * hm[0], qu * hm[1]], axis=0), kw)
                s = jnp.where(jnp.logical_and(band_ok, col >= lo), s, NEG)
                mx = jnp.max(s, axis=-1, keepdims=True)
                e = jnp.exp(s - mx)
                den = jnp.sum(e, axis=-1, keepdims=True)
                o2 = _dot(e.astype(BF16), vw) / den
                l2 = jnp.broadcast_to(mx + jnp.log(den), (2 * BAND, BAND))
                rows = pl.ds(pl.multiple_of(u * BAND, BAND), BAND)
                ob[rows, :] = jnp.where(head0, o2[:BAND], o2[BAND:])
                lb[rows, :] = jnp.where(head0, l2[:BAND], l2[BAND:])
                return carry

            lax.fori_loop(0, UNITS, unit, 0, unroll=2)
            _scatter_add(on.at[g], ob, d, per, 0, True)
            _scatter_add(ln.at[g], lb, d, per, 0, True)
        ls = [ln[0], ln[1], ln[2]]
        mx = jnp.maximum(jnp.maximum(ls[0], ls[1]), ls[2])
        es = [jnp.exp(l - mx) for l in ls]
        tot = es[0] + es[1] + es[2]
        y_ref[...] = (es[0] * on[0] + es[1] * on[1] + es[2] * on[2]) / tot
        l_ref[...] = mx + jnp.log(tot)

    main = pl.BlockSpec((TOK, BAND), lambda j, i: (i, j))
    prev = pl.BlockSpec((TOK, BAND), lambda j, i: (jnp.maximum(i - 1, 0), j))
    vmain = pl.BlockSpec((TOK, BAND), lambda j, i: (i, j + v_col))
    vprev = pl.BlockSpec((TOK, BAND), lambda j, i: (jnp.maximum(i - 1, 0), j + v_col))
    return pl.pallas_call(
        body, name="attn_fwd", grid=(D_ATTN // BAND, nblk), out_shape=[_sds((t, D_ATTN), F32)] * 2,
        in_specs=[main, prev, main, vprev, vmain], out_specs=[main, main],
        scratch_shapes=[pltpu.VMEM((TOK, BAND), BF16), pltpu.VMEM((2 * TOK, BAND), BF16),
                        pltpu.VMEM((2 * TOK, BAND), BF16), pltpu.VMEM((TOK, BAND), F32),
                        pltpu.VMEM((TOK, BAND), F32), pltpu.VMEM((3, TOK, BAND), F32),
                        pltpu.VMEM((3, TOK, BAND), F32)],
        compiler_params=_params("parallel", "parallel"),
    )(q, k, k, v, v)


def _attn_bwd(q, k, v, v_col, do, lse, dd):
    t = q.shape[0]
    nblk = t // TOK

    def body(q_ref, qn_ref, kp_ref, k_ref, vp_ref, v_ref, do_ref, don_ref, l_ref, ln_ref, d_ref,
             dn_ref, dq_ref, dk_ref, dv_ref, qs, dos, ks, vs, lsc, dsc, dqb, dkb, dvb):
        i = pl.program_id(1)
        diff, col, head0, hm = _stack_masks()
        band_ok = jnp.logical_and(diff >= 0, diff <= BAND)
        for g, d in enumerate(DILATIONS):
            per = TOK // d
            nb = per // BAND
            pad = per + BAND
            _gather([q_ref, qn_ref], qs, d)
            _gather([do_ref, don_ref], dos, d)
            _gather([l_ref, ln_ref], lsc, d)
            _gather([d_ref, dn_ref], dsc, d)
            _gather([kp_ref, k_ref], ks, d)
            _gather([vp_ref, v_ref], vs, d)
            dkb[...] = jnp.zeros_like(dkb)
            dvb[...] = jnp.zeros_like(dvb)

            def stacked(qrow):
                rows = pl.ds(qrow, BAND)
                qu, dou, lu, du = qs[rows, :], dos[rows, :], lsc[rows, :], dsc[rows, :]
                q2 = jnp.concatenate([qu * hm[0], qu * hm[1]], axis=0)
                do2 = jnp.concatenate([dou * hm[0], dou * hm[1]], axis=0)
                l2 = jnp.concatenate([lu[:, 0:1], lu[:, HEAD_DIM:HEAD_DIM + 1]], axis=0)
                d2 = jnp.concatenate([du[:, 0:1], du[:, HEAD_DIM:HEAD_DIM + 1]], axis=0)
                return q2, do2, l2, d2

            def unit(u, carry):
                r, b = u // nb, u % nb
                q2, do2, l2, d2 = stacked(pl.multiple_of(r * 2 * per + b * BAND, BAND))
                start = pl.multiple_of(r * 2 * per + per + (b - 1) * BAND, BAND)
                kw = ks[pl.ds(start, 2 * BAND), :]
                vw = vs[pl.ds(start, 2 * BAND), :]
                lo = jnp.where(jnp.logical_and(i == 0, b == 0), BAND, 0)
                ok = jnp.logical_and(band_ok, col >= lo)
                p = jnp.where(ok, jnp.exp(_dot_nt(q2, kw) - l2), 0.0)
                ds = (p * (_dot_nt(do2, vw) - d2)).astype(BF16)
                dq2 = _dot(ds, kw)
                dqb[pl.ds(pl.multiple_of(u * BAND, BAND), BAND), :] = jnp.where(
                    head0, dq2[:BAND], dq2[BAND:])
                acc = pl.ds(pl.multiple_of(r * pad + b * BAND, BAND), 2 * BAND)
                dkb[acc, :] += _dot_tn(ds, q2)
                dvb[acc, :] += _dot_tn(p.astype(BF16), do2)
                return carry

            lax.fori_loop(0, UNITS, unit, 0, unroll=2)

            def halo(r, carry):
                q2, do2, l2, d2 = stacked(pl.multiple_of(r * 2 * per + per, BAND))
                start = pl.multiple_of(r * 2 * per + per + (nb - 1) * BAND, BAND)
                kw = ks[pl.ds(start, BAND), :]
                vw = vs[pl.ds(start, BAND), :]
                ok = diff[:, :BAND] >= jnp.where(i < nblk - 1, 0, 2 * BAND)
                p = jnp.where(ok, jnp.exp(_dot_nt(q2, kw) - l2), 0.0)
                ds = (p * (_dot_nt(do2, vw) - d2)).astype(BF16)
                acc = pl.ds(pl.multiple_of(r * pad + nb * BAND, BAND), BAND)
                dkb[acc, :] += _dot_tn(ds, q2)
                dvb[acc, :] += _dot_tn(p.astype(BF16), do2)
                return carry

            lax.fori_loop(0, d, halo, 0)
            _scatter_add(dq_ref, dqb, d, per, 0, g == 0)
            _scatter_add(dk_ref, dkb, d, pad, BAND, g == 0)
            _scatter_add(dv_ref, dvb, d, pad, BAND, g == 0)

    main = pl.BlockSpec((TOK, BAND), lambda j, i: (i, j))
    prev = pl.BlockSpec((TOK, BAND), lambda j, i: (jnp.maximum(i - 1, 0), j))
    nxt = pl.BlockSpec((TOK, BAND), lambda j, i: (jnp.minimum(i + 1, nblk - 1), j))
    vmain = pl.BlockSpec((TOK, BAND), lambda j, i: (i, j + v_col))
    vprev = pl.BlockSpec((TOK, BAND), lambda j, i: (jnp.maximum(i - 1, 0), j + v_col))
    acc_rows = max(d * (TOK // d + BAND) for d in DILATIONS)
    return pl.pallas_call(
        body, name="attn_bwd", grid=(D_ATTN // BAND, nblk), out_shape=[_sds((t, D_ATTN), F32)] * 3,
        in_specs=[main, nxt, prev, main, vprev, vmain, main, nxt, main, nxt, main, nxt],
        out_specs=[main] * 3,
        scratch_shapes=[pltpu.VMEM((2 * TOK, BAND), BF16)] * 4
        + [pltpu.VMEM((2 * TOK, BAND), F32)] * 2 + [pltpu.VMEM((TOK, BAND), F32)]
        + [pltpu.VMEM((acc_rows, BAND), F32)] * 2,
        compiler_params=_params("parallel", "parallel"),
    )(q, q, k, k, v, v, do, do, lse, lse, dd, dd)


def _halo_rows(tm, t):
    per = tm // 8
    prev = lambda i: (jnp.maximum(i * per - 1, 0), 0)
    nxt = lambda i: (jnp.minimum((i + 1) * per, t // 8 - 1), 0)
    return prev, nxt


def _mixer_out(z, cw, y_attn, g_conv, g_attn, tm):
    t = z.shape[0]
    prev, _ = _halo_rows(tm, t)

    def body(z_ref, zp_ref, cw_ref, y_ref, gc_ref, ga_ref, mix_ref):
        i = pl.program_id(0)
        keep = jnp.where(i > 0, 1.0, 0.0)
        cu = jnp.concatenate([zp_ref[:, 0:512] * zp_ref[:, 1024:1536] * keep,
                              z_ref[:, 0:512] * z_ref[:, 1024:1536]], axis=0)
        c = (cw_ref[0:1, :] * pltpu.roll(cu, 2, 0) + cw_ref[1:2, :] * pltpu.roll(cu, 1, 0)
             + cw_ref[2:3, :] * cu)[8:, :]
        yc = z_ref[:, 512:1024] * c
        mix_ref[:, 0:512] = (yc * _rms_scale(yc) * gc_ref[...]).astype(BF16)
        ya = y_ref[...]
        mix_ref[:, 512:1024] = (ya * _rms_scale(ya) * ga_ref[...]).astype(BF16)

    blk = pl.BlockSpec((tm, 512), lambda i: (i, 0))
    vec = pl.BlockSpec((1, 512), lambda i: (0, 0))
    return pl.pallas_call(
        body, name="mixer_out", grid=(t // tm,), out_shape=_sds((t, 1024), BF16),
        in_specs=[pl.BlockSpec((tm, 1536), lambda i: (i, 0)), pl.BlockSpec((8, 1536), prev),
                  pl.BlockSpec((8, 512), lambda i: (0, 0)), blk, vec, vec],
        out_specs=pl.BlockSpec((tm, 1024), lambda i: (i, 0)),
        compiler_params=_params("parallel"),
    )(z, z, cw, y_attn, g_conv, g_attn)


def _mixer_bwd(z, dmix, y_attn, cw, g_conv, g_attn, ones_bd, tm):
    t = z.shape[0]
    nblk = t // tm
    prev, nxt = _halo_rows(tm, t)
    e = tm + 16

    def body(z_ref, zp_ref, zn_ref, dm_ref, dmn_ref, y_ref, cw_ref, gc_ref, ga_ref, bd_ref,
             dz_ref, do_ref, dd_ref, dcw_ref, dgc_ref, dga_ref):
        i = pl.program_id(0)
        rows = lax.broadcasted_iota(jnp.int32, (e, 1), 0)
        lo = jnp.where(i > 0, 0, 8)
        hi = jnp.where(i < nblk - 1, e, tm + 8)
        ze = jnp.concatenate([zp_ref[...], z_ref[...], zn_ref[...]], axis=0)
        u, gb, gcv = ze[:, 0:512], ze[:, 512:1024], ze[:, 1024:1536]
        w0, w1, w2 = cw_ref[0:1, :], cw_ref[1:2, :], cw_ref[2:3, :]
        cu = jnp.where(rows >= lo, gcv * u, 0.0)
        cu1, cu2 = pltpu.roll(cu, 1, 0), pltpu.roll(cu, 2, 0)
        c = w0 * cu2 + w1 * cu1 + w2 * cu
        yc = gb * c
        dma = jnp.concatenate([jnp.zeros((8, 512), F32), dm_ref[:, 0:512], dmn_ref[...]], axis=0)
        dyc, ych = _rms_bwd(yc, _rms_scale(yc), gc_ref[...], dma)
        dc = jnp.where(jnp.logical_and(rows >= 8, rows < hi), dyc * gb, 0.0)
        dcu = w0 * pltpu.roll(dc, e - 2, 0) + w1 * pltpu.roll(dc, e - 1, 0) + w2 * dc
        mid = slice(8, 8 + tm)
        dz_ref[:, 0:512] = (dcu * gcv)[mid, :].astype(BF16)
        dz_ref[:, 512:1024] = (dyc * c)[mid, :].astype(BF16)
        dz_ref[:, 1024:1536] = (dcu * u)[mid, :].astype(BF16)

        ya = y_ref[...]
        dmb = dm_ref[:, 512:1024]
        dya, yah = _rms_bwd(ya, _rms_scale(ya), ga_ref[...], dmb)
        do_ref[...] = dya
        dd_ref[...] = _head_sum(dya * ya, bd_ref[...])

        @pl.when(i == 0)
        def _():
            dcw_ref[...] = jnp.zeros_like(dcw_ref)
            dgc_ref[...] = jnp.zeros_like(dgc_ref)
            dga_ref[...] = jnp.zeros_like(dga_ref)

        dcm = jnp.where(rows < tm + 8, dc, 0.0)
        dcw_ref[0:1, :] += jnp.sum(dcm * cu2, axis=0, keepdims=True)
        dcw_ref[1:2, :] += jnp.sum(dcm * cu1, axis=0, keepdims=True)
        dcw_ref[2:3, :] += jnp.sum(dcm * cu, axis=0, keepdims=True)
        dgc_ref[...] += jnp.sum((dma * ych)[mid, :], axis=0, keepdims=True)
        dga_ref[...] += jnp.sum(dmb * yah, axis=0, keepdims=True)

    blk = pl.BlockSpec((tm, 512), lambda i: (i, 0))
    vec = pl.BlockSpec((1, 512), lambda i: (0, 0))
    cwb = pl.BlockSpec((8, 512), lambda i: (0, 0))
    return pl.pallas_call(
        body, name="mixer_bwd", grid=(nblk,),
        out_shape=[_sds((t, 1536), BF16), _sds((t, 512), F32), _sds((t, 512), F32),
                   _sds((8, 512), F32), _sds((1, 512), F32), _sds((1, 512), F32)],
        in_specs=[pl.BlockSpec((tm, 1536), lambda i: (i, 0)), pl.BlockSpec((8, 1536), prev),
                  pl.BlockSpec((8, 1536), nxt), pl.BlockSpec((tm, 1024), lambda i: (i, 0)),
                  pl.BlockSpec((8, 512), nxt), blk, cwb, vec, vec,
                  pl.BlockSpec((512, 512), lambda i: (0, 0))],
        out_specs=[pl.BlockSpec((tm, 1536), lambda i: (i, 0)), blk, blk, cwb, vec, vec],
        compiler_params=_params("arbitrary"),
    )(z, z, z, dmix, dmix, y_attn, cw, g_conv, g_attn, ones_bd)


def _qkv_bwd(z, dzc, dqn, dkn, dv, gq, gk, ones_bd, tm):
    t = z.shape[0]

    def body(zq_ref, zk_ref, dzc_ref, dqn_ref, dkn_ref, dv_ref, gq_ref, gk_ref, bd_ref,
             dz_ref, dgq_ref, dgk_ref):
        bd = bd_ref[...]

        @pl.when(pl.program_id(0) == 0)
        def _():
            dgq_ref[...] = jnp.zeros_like(dgq_ref)
            dgk_ref[...] = jnp.zeros_like(dgk_ref)

        def back(v, dn, g, scale):
            r = _head_rms_scale(v, bd)
            vh = v * r
            dh = dn * (g * scale)
            dv = r * (dh - vh * (_head_sum(dh * vh, bd) * (1.0 / HEAD_DIM)))
            return dv, jnp.sum(dn * scale * vh, axis=0, keepdims=True)

        dq, dgq = back(zq_ref[...], dqn_ref[...], gq_ref[...], HEAD_DIM ** -0.5)
        dk, dgk = back(zk_ref[...], dkn_ref[...], gk_ref[...], 1.0)
        dgq_ref[...] += dgq
        dgk_ref[...] += dgk
        dz_ref[:, 0:1536] = dzc_ref[...]
        dz_ref[:, 1536:2048] = dq.astype(BF16)
        dz_ref[:, 2048:2560] = dk.astype(BF16)
        dz_ref[:, 2560:3072] = dv_ref[...].astype(BF16)

    blk = pl.BlockSpec((tm, 512), lambda i: (i, 0))
    vec = pl.BlockSpec((1, 512), lambda i: (0, 0))
    return pl.pallas_call(
        body, name="qkv_bwd", grid=(t // tm,),
        out_shape=[_sds((t, D_IN), BF16), _sds((1, 512), F32), _sds((1, 512), F32)],
        in_specs=[pl.BlockSpec((tm, 512), lambda i: (i, 3)), pl.BlockSpec((tm, 512), lambda i: (i, 4)),
                  pl.BlockSpec((tm, 1536), lambda i: (i, 0))] + [blk] * 3
        + [vec, vec, pl.BlockSpec((512, 512), lambda i: (0, 0))],
        out_specs=[pl.BlockSpec((tm, D_IN), lambda i: (i, 0)), vec, vec],
        compiler_params=_params("arbitrary"),
    )(z, z, dzc, dqn, dkn, dv, gq, gk, ones_bd)


def _columns_from_chips(g):
    return g.transpose(1, 0, 2).reshape(g.shape[1], N_CHIPS * g.shape[2])


def _columns_to_chips(w):
    k, n4 = w.shape
    return w.reshape(k, N_CHIPS, n4 // N_CHIPS).transpose(1, 0, 2)


def kernel(x, g_mix, w_in, conv_w, g_q, g_k, g_conv_out, g_attn_out, w_out, g_ffn, w_gate, w_up, w_down, loss_target, m_g_mix, m_w_in, m_conv_w, m_g_q, m_g_k, m_g_conv_out, m_g_attn_out, m_w_out, m_g_ffn, m_w_gate, m_w_up, m_w_down, v_g_mix, v_w_in, v_conv_w, v_g_q, v_g_k, v_g_conv_out, v_g_attn_out, v_w_out, v_g_ffn, v_w_gate, v_w_up, v_w_down):
    t = x.shape[1]
    xs = x[0]
    target = loss_target[0]
    tm = min(512, t)

    cw_pad = jnp.pad(conv_w[0], ((0, 13), (0, 0)))
    gathered = _all_gather([w_in[0].astype(BF16), w_out[0].astype(BF16), w_gate[0].astype(BF16),
                            w_up[0].astype(BF16), w_down[0].astype(BF16), cw_pad])
    win = _columns_from_chips(gathered[0])
    wout = gathered[1].reshape(D_MODEL, D_MODEL)
    wgate = _columns_from_chips(gathered[2])
    wup = _columns_from_chips(gathered[3])
    wdown = gathered[4].reshape(D_FF, D_MODEL)
    cw = jnp.pad(gathered[5][:, 0:3, :].transpose(1, 0, 2).reshape(3, D_CONV), ((0, 5), (0, 0)))

    head_id = jnp.arange(D_ATTN) // HEAD_DIM
    ones_bd = (head_id[:, None] == head_id[None, :]).astype(BF16)
    gq_t = jnp.tile(g_q, (1, D_ATTN // HEAD_DIM))
    gk_t = jnp.tile(g_k, (1, D_ATTN // HEAD_DIM))

    h1, z = _norm_matmul("in_proj", xs, g_mix, [win], tm, 768, False)
    q, k = _qkv_prepare(z, gq_t, gk_t, ones_bd, tm)
    v_col = (3 * D_CONV + 2 * D_ATTN) // BAND
    y_attn, lse = _attn_fwd(q, k, z, v_col)
    mix = _mixer_out(z, cw, y_attn, g_conv_out, g_attn_out, tm)
    (x1,) = _matmul("out_proj", mix, wout, [xs], [F32], lambda acc, r: (r + acc,), tm, 512)
    h2, gate, up, act = _norm_matmul("ffn_up", x1, g_ffn, [wgate, wup], tm, 1408, True)

    def loss_epilogue(acc, r, tgt):
        err = r + acc - tgt
        dy = err * (1.0 / D_MODEL)
        return dy, dy, jnp.sum(err * err)

    dx2, dx2b, loss_sum = _matmul("ffn_down_loss", act, wdown, [x1, target], [F32, BF16],
                                  loss_epilogue, tm, 512, loss=True)

    def swiglu_bwd(da, gt, u):
        s = _sigmoid(gt)
        return da * u * (s * (1.0 + gt * (1.0 - s))), da * (gt * s)

    dgate, dup = _matmul("ffn_down_bwd", dx2b, wdown, [gate, up], [BF16, BF16], swiglu_bwd,
                         tm, 1408, transposed_w=True)
    gw_down = _matmul_tn("grad_w_down", act, dx2b, 512, tm)
    gw_gate = _matmul_tn("grad_w_gate", h2, dgate, 1408, tm)
    gw_up = _matmul_tn("grad_w_up", h2, dup, 1408, tm)
    dx1, dx1b, gg_ffn = _matmul_norm_bwd("ffn_up_bwd", [(dgate, wgate), (dup, wup)], x1, dx2, g_ffn,
                                         min(256, t))
    (dmix,) = _matmul("out_proj_bwd", dx1b, wout, [], [F32], lambda acc: (acc,), tm, 512,
                      transposed_w=True)
    gw_out = _matmul_tn("grad_w_out", mix, dx1b, 512, tm)
    dzc, do, dd, gcw, gg_conv, gg_attn = _mixer_bwd(z, dmix, y_attn, cw, g_conv_out, g_attn_out,
                                                    ones_bd, tm)
    dqn, dkn, dv = _attn_bwd(q, k, z, v_col, do, lse, dd)
    dz, gg_q, gg_k = _qkv_bwd(z, dzc, dqn, dkn, dv, gq_t, gk_t, ones_bd, tm)
    gw_in = _matmul_tn("grad_w_in", h1, dz, 768, tm)
    grad_x, _, gg_mix = _matmul_norm_bwd("in_proj_bwd", [(dz, win)], xs, dx1, g_mix, min(256, t))

    full = [_columns_to_chips(gw_in), gw_out.reshape(N_CHIPS, D_MODEL // N_CHIPS, D_MODEL),
            _columns_to_chips(gw_gate), _columns_to_chips(gw_up),
            gw_down.reshape(N_CHIPS, D_FF // N_CHIPS, D_MODEL)]
    me = 2 * lax.axis_index("x") + lax.axis_index("y")
    where = jnp.stack([lax.axis_index("c"), me]).astype(jnp.int32)
    big = ["w_in", "w_out", "w_gate", "w_up", "w_down"]
    got = _pair_exchange(full)
    pair = [_pair_sum(f"pair_sum_{nme}", a, b, where) for nme, a, b in zip(big, full, got)]
    got = _chip_exchange(pair)
    mine = [_chip_sum(f"chip_sum_{nme}", a, b, where) for nme, a, b in zip(big, pair, got)]
    theirs = _pair_share(mine)

    small = _small_all_reduce({
        "g_mix": gg_mix, "g_ffn": gg_ffn, "g_conv_out": gg_conv, "g_attn_out": gg_attn,
        "g_q": gg_q, "g_k": gg_k, "loss": loss_sum, "conv_w": gcw})
    heads = D_ATTN // HEAD_DIM
    grads = {
        "g_mix": small[0:1, :], "g_ffn": small[1:2, :],
        "g_conv_out": small[2:3, 0:512], "g_attn_out": small[2:3, 512:1024],
        "g_q": small[3, 0:512].reshape(heads, HEAD_DIM).sum(axis=0)[None, :],
        "g_k": small[3, 512:1024].reshape(heads, HEAD_DIM).sum(axis=0)[None, :],
        "conv_w": lax.dynamic_slice(small[8:11, 0:512], (0, me * (D_CONV // N_CHIPS)),
                                    (3, D_CONV // N_CHIPS)),
    }
    halves = dict(zip(big, zip(mine, theirs)))
    loss = small[4, 0] * 0.5 * (1.0 / D_MODEL)

    weights = dict(g_mix=g_mix, w_in=w_in, conv_w=conv_w, g_q=g_q, g_k=g_k, g_conv_out=g_conv_out,
                   g_attn_out=g_attn_out, w_out=w_out, g_ffn=g_ffn, w_gate=w_gate, w_up=w_up,
                   w_down=w_down)
    moments_m = dict(g_mix=m_g_mix, w_in=m_w_in, conv_w=m_conv_w, g_q=m_g_q, g_k=m_g_k,
                     g_conv_out=m_g_conv_out, g_attn_out=m_g_attn_out, w_out=m_w_out, g_ffn=m_g_ffn,
                     w_gate=m_w_gate, w_up=m_w_up, w_down=m_w_down)
    moments_v = dict(g_mix=v_g_mix, w_in=v_w_in, conv_w=v_conv_w, g_q=v_g_q, g_k=v_g_k,
                     g_conv_out=v_g_conv_out, g_attn_out=v_g_attn_out, w_out=v_w_out, g_ffn=v_g_ffn,
                     w_gate=v_w_gate, w_up=v_w_up, w_down=v_w_down)
    names = list(weights)
    out_g, out_d, out_m, out_v = [], [], [], []
    for nme in names:
        wgt = weights[nme]
        shape2 = wgt.shape[-2:] if wgt.ndim == 3 else wgt.shape
        state = (wgt.reshape(shape2), moments_m[nme].reshape(shape2), moments_v[nme].reshape(shape2))
        if nme in halves:
            g2, dlt, nm, nv = _adamw_shard(f"adamw_{nme}", *state, *halves[nme], where)
        else:
            g2 = grads[nme].reshape(shape2)
            dlt, nm, nv = _adamw(f"adamw_{nme}", state[0], g2, state[1], state[2])
        out_g.append(g2.reshape(wgt.shape))
        out_d.append(dlt.reshape(wgt.shape))
        out_m.append(nm.reshape(wgt.shape))
        out_v.append(nv.reshape(wgt.shape))
    return (loss, grad_x[None], *out_g, *out_d, *out_m, *out_v)
```

```python
import functools

import jax
import jax.numpy as jnp
from jax import lax
from jax.experimental import pallas as pl
from jax.experimental.pallas import tpu as pltpu

F32 = jnp.float32
BF16 = jnp.bfloat16
MESH = pl.DeviceIdType.MESH

D_MODEL = 1024
D_CONV = 512
D_ATTN = 512
HEAD_DIM = 64
D_FF = 2816
D_IN = 3 * D_CONV + 3 * D_ATTN
DILATIONS = (1, 4, 16)
BAND = 128
EPS = 1e-6
NEG = -1e30
N_CHIPS = 4

ADAM_LR = 0.001
ADAM_B1 = 0.9
ADAM_B2 = 0.999
ADAM_EPS = 1e-08
ADAM_WD = 0.01
ADAM_STEP = 10

V7X_VMEM_BYTES = 64 * 1024 * 1024
VMEM_LIMIT = V7X_VMEM_BYTES - 8 * 1024 * 1024
ANY = pl.BlockSpec(memory_space=pl.ANY)
VMEM_WHOLE = pl.BlockSpec(memory_space=pltpu.VMEM)


def _params(*sem):
    return pltpu.CompilerParams(dimension_semantics=sem, vmem_limit_bytes=VMEM_LIMIT)


def _sds(shape, dtype):
    return jax.ShapeDtypeStruct(shape, dtype)


def _place():
    x, y, c = lax.axis_index("x"), lax.axis_index("y"), lax.axis_index("c")
    chips = [(1 - x, y), (x, 1 - y), (1 - x, 1 - y)]
    return x, y, c, 2 * x + y, chips, [2 * cx + cy for cx, cy in chips]


def _all_gather(shards):
    n = len(shards)

    def body(*refs):
        ins, outs, stage = refs[:n], refs[n:2 * n], refs[2 * n:3 * n]
        ssem, rsem, fsem, gsem, lsem, osem = refs[3 * n:]
        x, y, c, me, chips, cids = _place()
        sib = (x, y, 1 - c)

        def half(w, which):
            h = shards[w].shape[0] // 2
            return pl.ds(pl.multiple_of(which * h, 8), h)

        loads = [pltpu.make_async_copy(ins[w], stage[w], lsem.at[w]) for w in range(n)]
        local = [pltpu.make_async_copy(stage[w], outs[w].at[me], osem.at[w]) for w in range(n)]
        for cp in loads:
            cp.start()

        def chip_copy(w, j, src_slot):
            rows = half(w, c)
            return pltpu.make_async_remote_copy(
                src_ref=ins[w].at[rows], dst_ref=outs[w].at[src_slot, rows],
                send_sem=ssem.at[3 * w + j], recv_sem=rsem.at[3 * w + j],
                device_id=(*chips[j], c), device_id_type=MESH)

        def sib_copy(w, j, which):
            rows = half(w, which)
            return pltpu.make_async_remote_copy(
                src_ref=outs[w].at[cids[j], rows], dst_ref=outs[w].at[cids[j], rows],
                send_sem=fsem.at[3 * w + j], recv_sem=gsem.at[3 * w + j],
                device_id=sib, device_id_type=MESH)

        sends = [chip_copy(w, j, me) for w in range(n) for j in range(3)]
        for cp in sends:
            cp.start()
        for w in range(n):
            loads[w].wait()
            local[w].start()
        passed = []
        for w in range(n):
            for j in range(3):
                chip_copy(w, j, cids[j]).wait_recv()
                cp = sib_copy(w, j, c)
                cp.start()
                passed.append(cp)
        for w in range(n):
            for j in range(3):
                sib_copy(w, j, 1 - c).wait_recv()
        for cp in sends + passed:
            cp.wait_send()
        for cp in local:
            cp.wait()

    return pl.pallas_call(
        body, name="all_gather_weights",
        out_shape=[_sds((N_CHIPS,) + s.shape, s.dtype) for s in shards],
        in_specs=[ANY] * n, out_specs=[ANY] * n,
        scratch_shapes=[pltpu.VMEM(s.shape, s.dtype) for s in shards]
        + [pltpu.SemaphoreType.DMA((3 * n,))] * 4 + [pltpu.SemaphoreType.DMA((n,))] * 2,
        compiler_params=pltpu.CompilerParams(vmem_limit_bytes=VMEM_LIMIT),
    )(*shards)


def _pair_exchange(grads):
    n = len(grads)

    def body(*refs):
        ins, got = refs[:n], refs[n:2 * n]
        ssem, rsem = refs[2 * n:]
        x, y, c, _, _, _ = _place()
        swaps = []
        for w in range(n):
            h = grads[w].shape[1] // 2
            theirs = pl.ds(pl.multiple_of((1 - c) * h, 8), h)
            swaps.append(pltpu.make_async_remote_copy(
                src_ref=ins[w].at[:, theirs, :], dst_ref=got[w],
                send_sem=ssem.at[w], recv_sem=rsem.at[w],
                device_id=(x, y, 1 - c), device_id_type=MESH))
        for cp in swaps:
            cp.start()
        for cp in swaps:
            cp.wait()

    return pl.pallas_call(
        body, name="grad_pair_exchange",
        out_shape=[_sds((N_CHIPS, g.shape[1] // 2, g.shape[2]), g.dtype) for g in grads],
        in_specs=[ANY] * n, out_specs=[ANY] * n,
        scratch_shapes=[pltpu.SemaphoreType.DMA((n,))] * 2,
    )(*grads)


def _chip_exchange(parts):
    n = len(parts)

    def body(*refs):
        ins, got = refs[:n], refs[n:2 * n]
        ssem, rsem = refs[2 * n:]
        _, _, c, _, chips, cids = _place()
        sends = [pltpu.make_async_remote_copy(
            src_ref=ins[w].at[cids[j]], dst_ref=got[w].at[j],
            send_sem=ssem.at[3 * w + j], recv_sem=rsem.at[3 * w + j],
            device_id=(*chips[j], c), device_id_type=MESH) for w in range(n) for j in range(3)]
        for cp in sends:
            cp.start()
        for cp in sends:
            cp.wait()

    return pl.pallas_call(
        body, name="grad_chip_exchange",
        out_shape=[_sds((3,) + p.shape[1:], p.dtype) for p in parts],
        in_specs=[ANY] * n, out_specs=[ANY] * n,
        scratch_shapes=[pltpu.SemaphoreType.DMA((3 * n,))] * 2,
    )(*parts)


def _pair_share(halves):
    n = len(halves)

    def body(*refs):
        ins, got = refs[:n], refs[n:2 * n]
        ssem, rsem = refs[2 * n:]
        x, y, c, _, _, _ = _place()
        swaps = [pltpu.make_async_remote_copy(
            src_ref=ins[w], dst_ref=got[w], send_sem=ssem.at[w], recv_sem=rsem.at[w],
            device_id=(x, y, 1 - c), device_id_type=MESH) for w in range(n)]
        for cp in swaps:
            cp.start()
        for cp in swaps:
            cp.wait()

    return pl.pallas_call(
        body, name="grad_pair_share", out_shape=[_sds(h.shape, h.dtype) for h in halves],
        in_specs=[ANY] * n, out_specs=[ANY] * n,
        scratch_shapes=[pltpu.SemaphoreType.DMA((n,))] * 2,
    )(*halves)


def _row_block(r, want):
    return max(d for d in range(1, min(want, r) + 1) if r % d == 0 and (d % 8 == 0 or d == r))


def _pair_sum(name, full, got, where):
    _, r, n = full.shape
    h = r // 2
    tr = _row_block(h, 256)
    nb = h // tr

    def body(w_ref, a_ref, b_ref, o_ref, own_ref):
        total = a_ref[...] + b_ref[...]
        o_ref[...] = total.astype(BF16)

        @pl.when(pl.program_id(1) == w_ref[1])
        def _():
            own_ref[...] = total[0]

    blk = pl.BlockSpec((1, tr, n), lambda i, s, w: (s, i, 0))
    return pl.pallas_call(
        body, name=name, out_shape=[_sds(got.shape, BF16), _sds((h, n), F32)],
        grid_spec=pltpu.PrefetchScalarGridSpec(
            num_scalar_prefetch=1, grid=(nb, N_CHIPS),
            in_specs=[pl.BlockSpec((1, tr, n), lambda i, s, w: (s, w[0] * nb + i, 0)), blk],
            out_specs=[blk, pl.BlockSpec((tr, n), lambda i, s, w: (i, 0))]),
        compiler_params=_params("parallel", "arbitrary"),
    )(where, full, got)


def _chip_sum(name, own, got):
    h, n = own.shape
    tr = _row_block(h, 256)

    def body(a_ref, b0, b1, b2, o_ref):
        o_ref[...] = ((a_ref[...] + b0[0].astype(F32)) + b1[0].astype(F32)) + b2[0].astype(F32)

    def slot(j):
        return pl.BlockSpec((1, tr, n), lambda i: (j, i, 0))

    blk = pl.BlockSpec((tr, n), lambda i: (i, 0))
    return pl.pallas_call(
        body, name=name, grid=(h // tr,), out_shape=_sds((h, n), F32),
        in_specs=[blk, slot(0), slot(1), slot(2)], out_specs=blk,
        compiler_params=_params("parallel"),
    )(own, got, got, got)


SMALL_ROWS = 16
SMALL_LAYOUT = (
    ("g_mix", 0, 0, 1, 1024), ("g_ffn", 1, 0, 1, 1024), ("g_conv_out", 2, 0, 1, 512),
    ("g_attn_out", 2, 512, 1, 512), ("g_q", 3, 0, 1, 512), ("g_k", 3, 512, 1, 512),
    ("loss", 4, 0, 1, 128), ("conv_w", 8, 0, 8, 512))


def _small_all_reduce(parts):
    names = [s[0] for s in SMALL_LAYOUT]

    def body(*refs):
        ins = refs[:len(names)]
        out_ref, stage, buf, ssem, rsem = refs[len(names):]
        x, y, c, _, _, _ = _place()
        me = 4 * x + 2 * y + c
        stage[...] = jnp.zeros_like(stage)
        for ref, (_, r0, c0, nr, nc) in zip(ins, SMALL_LAYOUT):
            stage[r0:r0 + nr, c0:c0 + nc] = ref[0:nr, :]
        buf[me] = stage[...]
        peers = []
        for d in range(1, 8):
            px = 1 - x if d & 4 else x
            py = 1 - y if d & 2 else y
            pc = 1 - c if d & 1 else c
            peers.append(((px, py, pc), 4 * px + 2 * py + pc))
        sends = [pltpu.make_async_remote_copy(
            src_ref=stage, dst_ref=buf.at[me], send_sem=ssem.at[k], recv_sem=rsem.at[k],
            device_id=peer, device_id_type=MESH) for k, (peer, _) in enumerate(peers)]
        for cp in sends:
            cp.start()
        for k, (peer, pid) in enumerate(peers):
            pltpu.make_async_remote_copy(
                src_ref=stage, dst_ref=buf.at[pid], send_sem=ssem.at[k], recv_sem=rsem.at[k],
                device_id=peer, device_id_type=MESH).wait_recv()
        for cp in sends:
            cp.wait_send()
        acc = buf[0]
        for k in range(1, 8):
            acc = acc + buf[k]
        out_ref[...] = acc

    return pl.pallas_call(
        body, name="small_all_reduce", out_shape=_sds((SMALL_ROWS, 1024), F32),
        in_specs=[VMEM_WHOLE] * len(names), out_specs=VMEM_WHOLE,
        scratch_shapes=[pltpu.VMEM((SMALL_ROWS, 1024), F32), pltpu.VMEM((8, SMALL_ROWS, 1024), F32),
                        pltpu.SemaphoreType.DMA((7,)), pltpu.SemaphoreType.DMA((7,))],
    )(*[parts[k] for k in names])


def _dot(a, b):
    return jnp.dot(a, b, preferred_element_type=F32)


def _dot_nt(a, b):
    return lax.dot_general(a, b, (((1,), (1,)), ((), ())), preferred_element_type=F32)


def _dot_tn(a, b):
    return lax.dot_general(a, b, (((0,), (0,)), ((), ())), preferred_element_type=F32)


def _sigmoid(v):
    return 1.0 / (1.0 + jnp.exp(-v))


def _rms_scale(v):
    return lax.rsqrt(jnp.mean(v * v, axis=-1, keepdims=True) + EPS)


def _rms_bwd(v, r, g, dy):
    vh = v * r
    dh = dy * g
    return r * (dh - vh * jnp.mean(dh * vh, axis=-1, keepdims=True)), vh


def _head_sum(a, ones_bd):
    hi = a.astype(BF16)
    lo = (a - hi.astype(F32)).astype(BF16)
    return _dot(hi, ones_bd) + _dot(lo, ones_bd)


def _head_rms_scale(v, ones_bd):
    return lax.rsqrt(_head_sum(v * v, ones_bd) * (1.0 / HEAD_DIM) + EPS)


def _norm_matmul(name, x, g, ws, tm, tn, swiglu, out_dtype=F32):
    t, d = x.shape
    n = ws[0].shape[1]
    nw = len(ws)

    def body(x_ref, g_ref, *refs):
        w_refs, h_ref, o_refs = refs[:nw], refs[nw], refs[nw + 1:2 * nw + 1]
        hs = refs[-1]

        @pl.when(pl.program_id(1) == 0)
        def _():
            xv = x_ref[...]
            h = (xv * _rms_scale(xv) * g_ref[...]).astype(BF16)
            hs[...] = h
            h_ref[...] = h

        h = hs[...]
        outs = [_dot(h, w[...]) for w in w_refs]
        for o_ref, o in zip(o_refs, outs):
            o_ref[...] = o.astype(out_dtype)
        if swiglu:
            refs[2 * nw + 1][...] = (outs[0] * _sigmoid(outs[0]) * outs[1]).astype(BF16)

    row = pl.BlockSpec((tm, d), lambda i, j: (i, 0))
    col = pl.BlockSpec((tm, tn), lambda i, j: (i, j))
    out_shape = [_sds((t, d), BF16)] + [_sds((t, n), out_dtype)] * nw
    out_specs = [row] + [col] * nw
    if swiglu:
        out_shape.append(_sds((t, n), BF16))
        out_specs.append(col)
    return pl.pallas_call(
        body, name=name, grid=(t // tm, n // tn), out_shape=out_shape,
        in_specs=[row, pl.BlockSpec((1, d), lambda i, j: (0, 0))]
        + [pl.BlockSpec((d, tn), lambda i, j: (0, j))] * nw,
        out_specs=out_specs, scratch_shapes=[pltpu.VMEM((tm, d), BF16)],
        compiler_params=_params("parallel", "arbitrary"),
    )(x, g, *ws)


def _matmul(name, a, w, extras, out_dtypes, epilogue, tm, tn, transposed_w=False, loss=False):
    t, k = a.shape
    n = w.shape[0] if transposed_w else w.shape[1]
    ne, no = len(extras), len(out_dtypes)

    def body(a_ref, w_ref, *refs):
        e_refs, o_refs = refs[:ne], refs[ne:]
        acc = _dot_nt(a_ref[...], w_ref[...]) if transposed_w else _dot(a_ref[...], w_ref[...])
        res = epilogue(acc, *[e[...] for e in e_refs])
        for o_ref, r in zip(o_refs[:no], res[:no]):
            o_ref[...] = r.astype(o_ref.dtype)
        if loss:
            first = jnp.logical_and(pl.program_id(0) == 0, pl.program_id(1) == 0)

            @pl.when(first)
            def _():
                o_refs[no][...] = jnp.zeros_like(o_refs[no])

            o_refs[no][...] += res[no]

    col = pl.BlockSpec((tm, tn), lambda i, j: (i, j))
    w_spec = (pl.BlockSpec((tn, k), lambda i, j: (j, 0)) if transposed_w
              else pl.BlockSpec((k, tn), lambda i, j: (0, j)))
    out_shape = [_sds((t, n), dt) for dt in out_dtypes]
    out_specs = [col] * no
    if loss:
        out_shape.append(_sds((8, 128), F32))
        out_specs.append(pl.BlockSpec((8, 128), lambda i, j: (0, 0)))
    return pl.pallas_call(
        body, name=name, grid=(t // tm, n // tn), out_shape=out_shape,
        in_specs=[pl.BlockSpec((tm, k), lambda i, j: (i, 0)), w_spec] + [col] * ne,
        out_specs=out_specs,
        compiler_params=_params(*(("arbitrary", "arbitrary") if loss else ("parallel", "parallel"))),
    )(a, w, *extras)


def _matmul_norm_bwd(name, pairs, x, dres, g, tm):
    t, d = x.shape
    npairs = len(pairs)

    def body(*refs):
        a_refs, w_refs = refs[:npairs], refs[npairs:2 * npairs]
        x_ref, r_ref, g_ref, dx_ref, dxb_ref, dg_ref = refs[2 * npairs:]
        dy = _dot_nt(a_refs[0][...], w_refs[0][...])
        for a_ref, w_ref in zip(a_refs[1:], w_refs[1:]):
            dy = dy + _dot_nt(a_ref[...], w_ref[...])
        xv = x_ref[...]
        dx, xh = _rms_bwd(xv, _rms_scale(xv), g_ref[...], dy)
        dx = dx + r_ref[...]
        dx_ref[...] = dx
        dxb_ref[...] = dx.astype(BF16)

        @pl.when(pl.program_id(0) == 0)
        def _():
            dg_ref[...] = jnp.zeros_like(dg_ref)

        dg_ref[...] += jnp.sum(dy * xh, axis=0, keepdims=True)

    row = pl.BlockSpec((tm, d), lambda i: (i, 0))
    vec = pl.BlockSpec((1, d), lambda i: (0, 0))
    return pl.pallas_call(
        body, name=name, grid=(t // tm,),
        out_shape=[_sds((t, d), F32), _sds((t, d), BF16), _sds((1, d), F32)],
        in_specs=[pl.BlockSpec((tm, a.shape[1]), lambda i: (i, 0)) for a, _ in pairs]
        + [pl.BlockSpec(w.shape, lambda i: (0, 0)) for _, w in pairs] + [row, row, vec],
        out_specs=[row, row, vec],
        compiler_params=_params("arbitrary"),
    )(*[a for a, _ in pairs], *[w for _, w in pairs], x, dres, g)


def _matmul_tn(name, a, g, tn, tk):
    t, ka = a.shape
    n = g.shape[1]

    def body(a_ref, g_ref, o_ref):
        @pl.when(pl.program_id(1) == 0)
        def _():
            o_ref[...] = jnp.zeros_like(o_ref)

        o_ref[...] += _dot_tn(a_ref[...], g_ref[...])

    return pl.pallas_call(
        body, name=name, grid=(n // tn, t // tk), out_shape=_sds((ka, n), F32),
        in_specs=[pl.BlockSpec((tk, ka), lambda j, s: (s, 0)),
                  pl.BlockSpec((tk, tn), lambda j, s: (s, j))],
        out_specs=pl.BlockSpec((ka, tn), lambda j, s: (0, j)),
        compiler_params=_params("parallel", "arbitrary"),
    )(a, g)


def _elementwise(name, fn, ins, out_dtypes, tr):
    r, n = ins[0].shape
    tr = _row_block(r, tr)
    ni = len(ins)

    def body(*refs):
        res = fn(*[ref[...] for ref in refs[:ni]])
        for o_ref, v in zip(refs[ni:], res):
            o_ref[...] = v.astype(o_ref.dtype)

    blk = pl.BlockSpec((tr, n), lambda i: (i, 0))
    return pl.pallas_call(
        body, name=name, grid=(r // tr,), out_shape=[_sds((r, n), dt) for dt in out_dtypes],
        in_specs=[blk] * ni, out_specs=[blk] * len(out_dtypes),
        compiler_params=_params("parallel"),
    )(*ins)


def _adamw_update(w, g, m, v):
    m = ADAM_B1 * m + (1.0 - ADAM_B1) * g
    v = ADAM_B2 * v + (1.0 - ADAM_B2) * (g * g)
    m_hat = m / (1.0 - ADAM_B1 ** ADAM_STEP)
    v_hat = v / (1.0 - ADAM_B2 ** ADAM_STEP)
    return -ADAM_LR * (m_hat / (jnp.sqrt(v_hat) + ADAM_EPS) + ADAM_WD * w), m, v


def _adamw(name, w, g, m, v):
    return _elementwise(name, _adamw_update, [w, g, m, v], [F32] * 3, 256)


def _adamw_shard(name, w, m, v, mine, theirs, where):
    r, n = w.shape
    h = r // 2
    tr = _row_block(h, 256)
    nb = h // tr

    def body(w_ref, p_ref, m_ref, v_ref, a_ref, b_ref, g_ref, d_ref, nm_ref, nv_ref):
        g = jnp.where(pl.program_id(0) == w_ref[0], a_ref[...], b_ref[...])
        g_ref[...] = g
        d_ref[...], nm_ref[...], nv_ref[...] = _adamw_update(p_ref[...], g, m_ref[...], v_ref[...])

    whole = pl.BlockSpec((tr, n), lambda s, i, c: (s * nb + i, 0))
    half = pl.BlockSpec((tr, n), lambda s, i, c: (i, 0))
    return pl.pallas_call(
        body, name=name, out_shape=[_sds((r, n), F32)] * 4,
        grid_spec=pltpu.PrefetchScalarGridSpec(
            num_scalar_prefetch=1, grid=(2, nb), in_specs=[whole] * 3 + [half] * 2,
            out_specs=[whole] * 4),
        compiler_params=_params("parallel", "parallel"),
    )(where, w, m, v, mine, theirs)


def _qkv_prepare(z, gq, gk, ones_bd, tm):
    t = z.shape[0]

    def body(zq_ref, zk_ref, gq_ref, gk_ref, bd_ref, q_ref, k_ref):
        bd = bd_ref[...]
        q = zq_ref[...]
        k = zk_ref[...]
        q_ref[...] = (q * _head_rms_scale(q, bd) * gq_ref[...]) * HEAD_DIM ** -0.5
        k_ref[...] = k * _head_rms_scale(k, bd) * gk_ref[...]

    vec = pl.BlockSpec((1, 512), lambda i: (0, 0))
    out = pl.BlockSpec((tm, 512), lambda i: (i, 0))
    return pl.pallas_call(
        body, name="qkv_prepare", grid=(t // tm,), out_shape=[_sds((t, 512), F32)] * 2,
        in_specs=[pl.BlockSpec((tm, 512), lambda i: (i, 3)), pl.BlockSpec((tm, 512), lambda i: (i, 4)),
                  vec, vec, pl.BlockSpec((512, 512), lambda i: (0, 0))],
        out_specs=[out] * 2, compiler_params=_params("parallel"),
    )(z, z, gq, gk, ones_bd)


TOK = 2048
UNITS = TOK // BAND


def _stack_masks():
    row = lax.broadcasted_iota(jnp.int32, (2 * BAND, 2 * BAND), 0) & (BAND - 1)
    col = lax.broadcasted_iota(jnp.int32, (2 * BAND, 2 * BAND), 1)
    lane = lax.broadcasted_iota(jnp.int32, (BAND, BAND), 1)
    head0 = lane < HEAD_DIM
    ones = [jnp.where(head0, 1.0, 0.0).astype(BF16), jnp.where(head0, 0.0, 1.0).astype(BF16)]
    return col - row, col, head0, ones


def _gather(srcs, dst, d, last_rows=None):
    per = TOK // d
    counts = [per] * len(srcs)
    if last_rows is not None:
        counts[-1] = last_rows
    at = 0
    for r in range(d):
        for src, n in zip(srcs, counts):
            rows = src[pl.ds(r, n, stride=d), :] if d > 1 else src[pl.ds(0, n), :]
            dst[pl.ds(at, n), :] = rows.astype(dst.dtype)
            at += n


def _scatter_add(out_ref, src, d, per_src, offset, first):
    per = TOK // d
    if d == 1:
        val = src[pl.ds(offset, per), :]
        out_ref[...] = val if first else out_ref[...] + val
        return
    for r in range(d):
        val = src[pl.ds(r * per_src + offset, per), :]
        idx = pl.ds(r, per, stride=d)
        out_ref[idx, :] = val if first else out_ref[idx, :] + val


def _attn_fwd(q, k, v, v_col):
    t = q.shape[0]
    nblk = t // TOK

    def body(q_ref, kp_ref, k_ref, vp_ref, v_ref, y_ref, l_ref, qs, ks, vs, ob, lb, on, ln):
        i = pl.program_id(1)
        diff, col, head0, hm = _stack_masks()
        band_ok = jnp.logical_and(diff >= 0, diff <= BAND)
        for g, d in enumerate(DILATIONS):
            per = TOK // d
            nb = per // BAND
            _gather([q_ref], qs, d)
            _gather([kp_ref, k_ref], ks, d)
            _gather([vp_ref, v_ref], vs, d)

            def unit(u, carry):
                r, b = u // nb, u % nb
                qu = qs[pl.ds(pl.multiple_of(u * BAND, BAND), BAND), :]
                start = pl.multiple_of(r * 2 * per + per + (b - 1) * BAND, BAND)
                kw = ks[pl.ds(start, 2 * BAND), :]
                vw = vs[pl.ds(start, 2 * BAND), :]
                lo = jnp.where(jnp.logical_and(i == 0, b == 0), BAND, 0)
                s = _dot_nt(jnp.concatenate([qu * hm[0], qu * hm[1]], axis=0), kw)
                s = jnp.where(jnp.logical_and(band_ok, col >= lo), s, NEG)
                mx = jnp.max(s, axis=-1, keepdims=True)
                e = jnp.exp(s - mx)
                den = jnp.sum(e, axis=-1, keepdims=True)
                o2 = _dot(e.astype(BF16), vw) / den
                l2 = jnp.broadcast_to(mx + jnp.log(den), (2 * BAND, BAND))
                rows = pl.ds(pl.multiple_of(u * BAND, BAND), BAND)
                ob[rows, :] = jnp.where(head0, o2[:BAND], o2[BAND:])
                lb[rows, :] = jnp.where(head0, l2[:BAND], l2[BAND:])
                return carry

            lax.fori_loop(0, UNITS, unit, 0, unroll=8)
            _scatter_add(on.at[g], ob, d, per, 0, True)
            _scatter_add(ln.at[g], lb, d, per, 0, True)
        ls = [ln[0], ln[1], ln[2]]
        mx = jnp.maximum(jnp.maximum(ls[0], ls[1]), ls[2])
        es = [jnp.exp(l - mx) for l in ls]
        tot = es[0] + es[1] + es[2]
        y_ref[...] = (es[0] * on[0] + es[1] * on[1] + es[2] * on[2]) / tot
        l_ref[...] = mx + jnp.log(tot)

    main = pl.BlockSpec((TOK, BAND), lambda j, i: (i, j))
    prev = pl.BlockSpec((TOK, BAND), lambda j, i: (jnp.maximum(i - 1, 0), j))
    vmain = pl.BlockSpec((TOK, BAND), lambda j, i: (i, j + v_col))
    vprev = pl.BlockSpec((TOK, BAND), lambda j, i: (jnp.maximum(i - 1, 0), j + v_col))
    return pl.pallas_call(
        body, name="attn_fwd", grid=(D_ATTN // BAND, nblk), out_shape=[_sds((t, D_ATTN), F32)] * 2,
        in_specs=[main, prev, main, vprev, vmain], out_specs=[main, main],
        scratch_shapes=[pltpu.VMEM((TOK, BAND), BF16), pltpu.VMEM((2 * TOK, BAND), BF16),
                        pltpu.VMEM((2 * TOK, BAND), BF16), pltpu.VMEM((TOK, BAND), F32),
                        pltpu.VMEM((TOK, BAND), F32), pltpu.VMEM((3, TOK, BAND), F32),
                        pltpu.VMEM((3, TOK, BAND), F32)],
        compiler_params=_params("parallel", "parallel"),
    )(q, k, k, v, v)


def _attn_bwd(q, k, v, v_col, do, lse, dd):
    t = q.shape[0]
    nblk = t // TOK

    def body(q_ref, qn_ref, kp_ref, k_ref, vp_ref, v_ref, do_ref, don_ref, l_ref, ln_ref, d_ref,
             dn_ref, dq_ref, dk_ref, dv_ref, qs, dos, ks, vs, lsc, dsc, dqb, dkb, dvb):
        i = pl.program_id(1)
        diff, col, head0, hm = _stack_masks()
        band_ok = jnp.logical_and(diff >= 0, diff <= BAND)
        for g, d in enumerate(DILATIONS):
            per = TOK // d
            nb = per // BAND
            pad = per + BAND
            _gather([q_ref, qn_ref], qs, d, BAND)
            _gather([do_ref, don_ref], dos, d, BAND)
            _gather([l_ref, ln_ref], lsc, d, BAND)
            _gather([d_ref, dn_ref], dsc, d, BAND)
            _gather([kp_ref, k_ref], ks, d)
            _gather([vp_ref, v_ref], vs, d)
            dkb[...] = jnp.zeros_like(dkb)
            dvb[...] = jnp.zeros_like(dvb)

            def stacked(qrow):
                rows = pl.ds(qrow, BAND)
                qu, dou, lu, du = qs[rows, :], dos[rows, :], lsc[rows, :], dsc[rows, :]
                q2 = jnp.concatenate([qu * hm[0], qu * hm[1]], axis=0)
                do2 = jnp.concatenate([dou * hm[0], dou * hm[1]], axis=0)
                l2 = jnp.concatenate([lu[:, 0:1], lu[:, HEAD_DIM:HEAD_DIM + 1]], axis=0)
                d2 = jnp.concatenate([du[:, 0:1], du[:, HEAD_DIM:HEAD_DIM + 1]], axis=0)
                return q2, do2, l2, d2

            def unit(u, carry):
                r, b = u // nb, u % nb
                q2, do2, l2, d2 = stacked(pl.multiple_of(r * pad + b * BAND, BAND))
                start = pl.multiple_of(r * 2 * per + per + (b - 1) * BAND, BAND)
                kw = ks[pl.ds(start, 2 * BAND), :]
                vw = vs[pl.ds(start, 2 * BAND), :]
                lo = jnp.where(jnp.logical_and(i == 0, b == 0), BAND, 0)
                ok = jnp.logical_and(band_ok, col >= lo)
                p = jnp.where(ok, jnp.exp(_dot_nt(q2, kw) - l2), 0.0)
                ds = (p * (_dot_nt(do2, vw) - d2)).astype(BF16)
                dq2 = _dot(ds, kw)
                dqb[pl.ds(pl.multiple_of(u * BAND, BAND), BAND), :] = jnp.where(
                    head0, dq2[:BAND], dq2[BAND:])
                acc = pl.ds(pl.multiple_of(r * pad + b * BAND, BAND), 2 * BAND)
                dkb[acc, :] += _dot_tn(ds, q2)
                dvb[acc, :] += _dot_tn(p.astype(BF16), do2)
                return carry

            lax.fori_loop(0, UNITS, unit, 0, unroll=8)

            def halo(r, carry):
                q2, do2, l2, d2 = stacked(pl.multiple_of(r * pad + per, BAND))
                start = pl.multiple_of(r * 2 * per + per + (nb - 1) * BAND, BAND)
                kw = ks[pl.ds(start, BAND), :]
                vw = vs[pl.ds(start, BAND), :]
                ok = diff[:, :BAND] >= jnp.where(i < nblk - 1, 0, 2 * BAND)
                p = jnp.where(ok, jnp.exp(_dot_nt(q2, kw) - l2), 0.0)
                ds = (p * (_dot_nt(do2, vw) - d2)).astype(BF16)
                acc = pl.ds(pl.multiple_of(r * pad + nb * BAND, BAND), BAND)
                dkb[acc, :] += _dot_tn(ds, q2)
                dvb[acc, :] += _dot_tn(p.astype(BF16), do2)
                return carry

            lax.fori_loop(0, d, halo, 0, unroll=min(d, 8))
            _scatter_add(dq_ref, dqb, d, per, 0, g == 0)
            _scatter_add(dk_ref, dkb, d, pad, BAND, g == 0)
            _scatter_add(dv_ref, dvb, d, pad, BAND, g == 0)

    main = pl.BlockSpec((TOK, BAND), lambda j, i: (i, j))
    prev = pl.BlockSpec((TOK, BAND), lambda j, i: (jnp.maximum(i - 1, 0), j))
    nxt = pl.BlockSpec((TOK, BAND), lambda j, i: (jnp.minimum(i + 1, nblk - 1), j))
    vmain = pl.BlockSpec((TOK, BAND), lambda j, i: (i, j + v_col))
    vprev = pl.BlockSpec((TOK, BAND), lambda j, i: (jnp.maximum(i - 1, 0), j + v_col))
    acc_rows = max(d * (TOK // d + BAND) for d in DILATIONS)
    return pl.pallas_call(
        body, name="attn_bwd", grid=(D_ATTN // BAND, nblk), out_shape=[_sds((t, D_ATTN), F32)] * 3,
        in_specs=[main, nxt, prev, main, vprev, vmain, main, nxt, main, nxt, main, nxt],
        out_specs=[main] * 3,
        scratch_shapes=[pltpu.VMEM((2 * TOK, BAND), BF16)] * 4
        + [pltpu.VMEM((2 * TOK, BAND), F32)] * 2 + [pltpu.VMEM((TOK, BAND), F32)]
        + [pltpu.VMEM((acc_rows, BAND), F32)] * 2,
        compiler_params=_params("parallel", "parallel"),
    )(q, q, k, k, v, v, do, do, lse, lse, dd, dd)


def _halo_rows(tm, t):
    per = tm // 8
    prev = lambda i: (jnp.maximum(i * per - 1, 0), 0)
    nxt = lambda i: (jnp.minimum((i + 1) * per, t // 8 - 1), 0)
    return prev, nxt


def _mixer_out(z, cw, y_attn, g_conv, g_attn, tm):
    t = z.shape[0]
    prev, _ = _halo_rows(tm, t)

    def body(z_ref, zp_ref, cw_ref, y_ref, gc_ref, ga_ref, mix_ref):
        i = pl.program_id(0)
        keep = jnp.where(i > 0, 1.0, 0.0)
        cu = jnp.concatenate([zp_ref[:, 0:512] * zp_ref[:, 1024:1536] * keep,
                              z_ref[:, 0:512] * z_ref[:, 1024:1536]], axis=0)
        c = (cw_ref[0:1, :] * pltpu.roll(cu, 2, 0) + cw_ref[1:2, :] * pltpu.roll(cu, 1, 0)
             + cw_ref[2:3, :] * cu)[8:, :]
        yc = z_ref[:, 512:1024] * c
        mix_ref[:, 0:512] = (yc * _rms_scale(yc) * gc_ref[...]).astype(BF16)
        ya = y_ref[...]
        mix_ref[:, 512:1024] = (ya * _rms_scale(ya) * ga_ref[...]).astype(BF16)

    blk = pl.BlockSpec((tm, 512), lambda i: (i, 0))
    vec = pl.BlockSpec((1, 512), lambda i: (0, 0))
    return pl.pallas_call(
        body, name="mixer_out", grid=(t // tm,), out_shape=_sds((t, 1024), BF16),
        in_specs=[pl.BlockSpec((tm, 1536), lambda i: (i, 0)), pl.BlockSpec((8, 1536), prev),
                  pl.BlockSpec((8, 512), lambda i: (0, 0)), blk, vec, vec],
        out_specs=pl.BlockSpec((tm, 1024), lambda i: (i, 0)),
        compiler_params=_params("parallel"),
    )(z, z, cw, y_attn, g_conv, g_attn)


def _mixer_bwd(z, dmix, y_attn, cw, g_conv, g_attn, ones_bd, tm):
    t = z.shape[0]
    nblk = t // tm
    prev, nxt = _halo_rows(tm, t)
    e = tm + 16

    def body(z_ref, zp_ref, zn_ref, dm_ref, dmn_ref, y_ref, cw_ref, gc_ref, ga_ref, bd_ref,
             dz_ref, do_ref, dd_ref, dcw_ref, dgc_ref, dga_ref):
        i = pl.program_id(0)
        rows = lax.broadcasted_iota(jnp.int32, (e, 1), 0)
        lo = jnp.where(i > 0, 0, 8)
        hi = jnp.where(i < nblk - 1, e, tm + 8)
        ze = jnp.concatenate([zp_ref[...], z_ref[...], zn_ref[...]], axis=0)
        u, gb, gcv = ze[:, 0:512], ze[:, 512:1024], ze[:, 1024:1536]
        w0, w1, w2 = cw_ref[0:1, :], cw_ref[1:2, :], cw_ref[2:3, :]
        cu = jnp.where(rows >= lo, gcv * u, 0.0)
        cu1, cu2 = pltpu.roll(cu, 1, 0), pltpu.roll(cu, 2, 0)
        c = w0 * cu2 + w1 * cu1 + w2 * cu
        yc = gb * c
        dma = jnp.concatenate([jnp.zeros((8, 512), F32), dm_ref[:, 0:512], dmn_ref[...]], axis=0)
        dyc, ych = _rms_bwd(yc, _rms_scale(yc), gc_ref[...], dma)
        dc = jnp.where(jnp.logical_and(rows >= 8, rows < hi), dyc * gb, 0.0)
        dcu = w0 * pltpu.roll(dc, e - 2, 0) + w1 * pltpu.roll(dc, e - 1, 0) + w2 * dc
        mid = slice(8, 8 + tm)
        dz_ref[:, 0:512] = (dcu * gcv)[mid, :].astype(BF16)
        dz_ref[:, 512:1024] = (dyc * c)[mid, :].astype(BF16)
        dz_ref[:, 1024:1536] = (dcu * u)[mid, :].astype(BF16)

        ya = y_ref[...]
        dmb = dm_ref[:, 512:1024]
        dya, yah = _rms_bwd(ya, _rms_scale(ya), ga_ref[...], dmb)
        do_ref[...] = dya
        dd_ref[...] = _head_sum(dya * ya, bd_ref[...])

        @pl.when(i == 0)
        def _():
            dcw_ref[...] = jnp.zeros_like(dcw_ref)
            dgc_ref[...] = jnp.zeros_like(dgc_ref)
            dga_ref[...] = jnp.zeros_like(dga_ref)

        dcm = jnp.where(rows < tm + 8, dc, 0.0)
        dcw_ref[0:1, :] += jnp.sum(dcm * cu2, axis=0, keepdims=True)
        dcw_ref[1:2, :] += jnp.sum(dcm * cu1, axis=0, keepdims=True)
        dcw_ref[2:3, :] += jnp.sum(dcm * cu, axis=0, keepdims=True)
        dgc_ref[...] += jnp.sum((dma * ych)[mid, :], axis=0, keepdims=True)
        dga_ref[...] += jnp.sum(dmb * yah, axis=0, keepdims=True)

    blk = pl.BlockSpec((tm, 512), lambda i: (i, 0))
    vec = pl.BlockSpec((1, 512), lambda i: (0, 0))
    cwb = pl.BlockSpec((8, 512), lambda i: (0, 0))
    return pl.pallas_call(
        body, name="mixer_bwd", grid=(nblk,),
        out_shape=[_sds((t, 1536), BF16), _sds((t, 512), F32), _sds((t, 512), F32),
                   _sds((8, 512), F32), _sds((1, 512), F32), _sds((1, 512), F32)],
        in_specs=[pl.BlockSpec((tm, 1536), lambda i: (i, 0)), pl.BlockSpec((8, 1536), prev),
                  pl.BlockSpec((8, 1536), nxt), pl.BlockSpec((tm, 1024), lambda i: (i, 0)),
                  pl.BlockSpec((8, 512), nxt), blk, cwb, vec, vec,
                  pl.BlockSpec((512, 512), lambda i: (0, 0))],
        out_specs=[pl.BlockSpec((tm, 1536), lambda i: (i, 0)), blk, blk, cwb, vec, vec],
        compiler_params=_params("arbitrary"),
    )(z, z, z, dmix, dmix, y_attn, cw, g_conv, g_attn, ones_bd)


def _qkv_bwd(z, dzc, dqn, dkn, dv, gq, gk, ones_bd, tm):
    t = z.shape[0]

    def body(zq_ref, zk_ref, dzc_ref, dqn_ref, dkn_ref, dv_ref, gq_ref, gk_ref, bd_ref,
             dz_ref, dgq_ref, dgk_ref):
        bd = bd_ref[...]

        @pl.when(pl.program_id(0) == 0)
        def _():
            dgq_ref[...] = jnp.zeros_like(dgq_ref)
            dgk_ref[...] = jnp.zeros_like(dgk_ref)

        def back(v, dn, g, scale):
            r = _head_rms_scale(v, bd)
            vh = v * r
            dh = dn * (g * scale)
            dv = r * (dh - vh * (_head_sum(dh * vh, bd) * (1.0 / HEAD_DIM)))
            return dv, jnp.sum(dn * scale * vh, axis=0, keepdims=True)

        dq, dgq = back(zq_ref[...], dqn_ref[...], gq_ref[...], HEAD_DIM ** -0.5)
        dk, dgk = back(zk_ref[...], dkn_ref[...], gk_ref[...], 1.0)
        dgq_ref[...] += dgq
        dgk_ref[...] += dgk
        dz_ref[:, 0:1536] = dzc_ref[...]
        dz_ref[:, 1536:2048] = dq.astype(BF16)
        dz_ref[:, 2048:2560] = dk.astype(BF16)
        dz_ref[:, 2560:3072] = dv_ref[...].astype(BF16)

    blk = pl.BlockSpec((tm, 512), lambda i: (i, 0))
    vec = pl.BlockSpec((1, 512), lambda i: (0, 0))
    return pl.pallas_call(
        body, name="qkv_bwd", grid=(t // tm,),
        out_shape=[_sds((t, D_IN), BF16), _sds((1, 512), F32), _sds((1, 512), F32)],
        in_specs=[pl.BlockSpec((tm, 512), lambda i: (i, 3)), pl.BlockSpec((tm, 512), lambda i: (i, 4)),
                  pl.BlockSpec((tm, 1536), lambda i: (i, 0))] + [blk] * 3
        + [vec, vec, pl.BlockSpec((512, 512), lambda i: (0, 0))],
        out_specs=[pl.BlockSpec((tm, D_IN), lambda i: (i, 0)), vec, vec],
        compiler_params=_params("arbitrary"),
    )(z, z, dzc, dqn, dkn, dv, gq, gk, ones_bd)


def _columns_from_chips(g):
    return g.transpose(1, 0, 2).reshape(g.shape[1], N_CHIPS * g.shape[2])


def _columns_to_chips(w):
    k, n4 = w.shape
    return w.reshape(k, N_CHIPS, n4 // N_CHIPS).transpose(1, 0, 2)


def kernel(x, g_mix, w_in, conv_w, g_q, g_k, g_conv_out, g_attn_out, w_out, g_ffn, w_gate, w_up, w_down, loss_target, m_g_mix, m_w_in, m_conv_w, m_g_q, m_g_k, m_g_conv_out, m_g_attn_out, m_w_out, m_g_ffn, m_w_gate, m_w_up, m_w_down, v_g_mix, v_w_in, v_conv_w, v_g_q, v_g_k, v_g_conv_out, v_g_attn_out, v_w_out, v_g_ffn, v_w_gate, v_w_up, v_w_down):
    t = x.shape[1]
    xs = x[0]
    target = loss_target[0]
    tm = min(512, t)
    tmm = min(1024, t)

    cw_pad = jnp.pad(conv_w[0], ((0, 13), (0, 0)))
    gathered = _all_gather([w_in[0].astype(BF16), w_out[0].astype(BF16), w_gate[0].astype(BF16),
                            w_up[0].astype(BF16), w_down[0].astype(BF16), cw_pad])
    win = _columns_from_chips(gathered[0])
    wout = gathered[1].reshape(D_MODEL, D_MODEL)
    wgate = _columns_from_chips(gathered[2])
    wup = _columns_from_chips(gathered[3])
    wdown = gathered[4].reshape(D_FF, D_MODEL)
    cw = jnp.pad(gathered[5][:, 0:3, :].transpose(1, 0, 2).reshape(3, D_CONV), ((0, 5), (0, 0)))

    head_id = jnp.arange(D_ATTN) // HEAD_DIM
    ones_bd = (head_id[:, None] == head_id[None, :]).astype(BF16)
    gq_t = jnp.tile(g_q, (1, D_ATTN // HEAD_DIM))
    gk_t = jnp.tile(g_k, (1, D_ATTN // HEAD_DIM))

    h1, z = _norm_matmul("in_proj", xs, g_mix, [win], tmm, 768, False)
    q, k = _qkv_prepare(z, gq_t, gk_t, ones_bd, tm)
    v_col = (3 * D_CONV + 2 * D_ATTN) // BAND
    y_attn, lse = _attn_fwd(q, k, z, v_col)
    mix = _mixer_out(z, cw, y_attn, g_conv_out, g_attn_out, tm)
    (x1,) = _matmul("out_proj", mix, wout, [xs], [F32], lambda acc, r: (r + acc,), tmm, 512)
    h2, gate, up, act = _norm_matmul("ffn_up", x1, g_ffn, [wgate, wup], tmm, 1408, True, BF16)

    def loss_epilogue(acc, r, tgt):
        err = r + acc - tgt
        dy = err * (1.0 / D_MODEL)
        return dy, dy, jnp.sum(err * err)

    dx2, dx2b, loss_sum = _matmul("ffn_down_loss", act, wdown, [x1, target], [F32, BF16],
                                  loss_epilogue, tmm, 512, loss=True)

    def swiglu_bwd(da, gt, u):
        gt, u = gt.astype(F32), u.astype(F32)
        s = _sigmoid(gt)
        return da * u * (s * (1.0 + gt * (1.0 - s))), da * (gt * s)

    dgate, dup = _matmul("ffn_down_bwd", dx2b, wdown, [gate, up], [BF16, BF16], swiglu_bwd,
                         tmm, 1408, transposed_w=True)
    gw_down = _matmul_tn("grad_w_down", act, dx2b, 512, tmm)
    gw_gate = _matmul_tn("grad_w_gate", h2, dgate, 1408, tmm)
    gw_up = _matmul_tn("grad_w_up", h2, dup, 1408, tmm)
    dx1, dx1b, gg_ffn = _matmul_norm_bwd("ffn_up_bwd", [(dgate, wgate), (dup, wup)], x1, dx2, g_ffn,
                                         min(256, t))
    (dmix,) = _matmul("out_proj_bwd", dx1b, wout, [], [F32], lambda acc: (acc,), tmm, 512,
                      transposed_w=True)
    gw_out = _matmul_tn("grad_w_out", mix, dx1b, 512, tmm)
    dzc, do, dd, gcw, gg_conv, gg_attn = _mixer_bwd(z, dmix, y_attn, cw, g_conv_out, g_attn_out,
                                                    ones_bd, tm)
    dqn, dkn, dv = _attn_bwd(q, k, z, v_col, do, lse, dd)
    dz, gg_q, gg_k = _qkv_bwd(z, dzc, dqn, dkn, dv, gq_t, gk_t, ones_bd, tm)
    gw_in = _matmul_tn("grad_w_in", h1, dz, 768, tmm)
    grad_x, _, gg_mix = _matmul_norm_bwd("in_proj_bwd", [(dz, win)], xs, dx1, g_mix, min(256, t))

    full = [_columns_to_chips(gw_in), gw_out.reshape(N_CHIPS, D_MODEL // N_CHIPS, D_MODEL),
            _columns_to_chips(gw_gate), _columns_to_chips(gw_up),
            gw_down.reshape(N_CHIPS, D_FF // N_CHIPS, D_MODEL)]
    me = 2 * lax.axis_index("x") + lax.axis_index("y")
    where = jnp.stack([lax.axis_index("c"), me]).astype(jnp.int32)
    big = ["w_in", "w_out", "w_gate", "w_up", "w_down"]
    got = _pair_exchange(full)
    pair = [_pair_sum(f"pair_sum_{nme}", a, b, where) for nme, a, b in zip(big, full, got)]
    got = _chip_exchange([p for p, _ in pair])
    mine = [_chip_sum(f"chip_sum_{nme}", own, b) for nme, (_, own), b in zip(big, pair, got)]
    theirs = _pair_share(mine)

    small = _small_all_reduce({
        "g_mix": gg_mix, "g_ffn": gg_ffn, "g_conv_out": gg_conv, "g_attn_out": gg_attn,
        "g_q": gg_q, "g_k": gg_k, "loss": loss_sum, "conv_w": gcw})
    heads = D_ATTN // HEAD_DIM
    grads = {
        "g_mix": small[0:1, :], "g_ffn": small[1:2, :],
        "g_conv_out": small[2:3, 0:512], "g_attn_out": small[2:3, 512:1024],
        "g_q": small[3, 0:512].reshape(heads, HEAD_DIM).sum(axis=0)[None, :],
        "g_k": small[3, 512:1024].reshape(heads, HEAD_DIM).sum(axis=0)[None, :],
        "conv_w": lax.dynamic_slice(small[8:11, 0:512], (0, me * (D_CONV // N_CHIPS)),
                                    (3, D_CONV // N_CHIPS)),
    }
    halves = dict(zip(big, zip(mine, theirs)))
    loss = small[4, 0] * 0.5 * (1.0 / D_MODEL)

    weights = dict(g_mix=g_mix, w_in=w_in, conv_w=conv_w, g_q=g_q, g_k=g_k, g_conv_out=g_conv_out,
                   g_attn_out=g_attn_out, w_out=w_out, g_ffn=g_ffn, w_gate=w_gate, w_up=w_up,
                   w_down=w_down)
    moments_m = dict(g_mix=m_g_mix, w_in=m_w_in, conv_w=m_conv_w, g_q=m_g_q, g_k=m_g_k,
                     g_conv_out=m_g_conv_out, g_attn_out=m_g_attn_out, w_out=m_w_out, g_ffn=m_g_ffn,
                     w_gate=m_w_gate, w_up=m_w_up, w_down=m_w_down)
    moments_v = dict(g_mix=v_g_mix, w_in=v_w_in, conv_w=v_conv_w, g_q=v_g_q, g_k=v_g_k,
                     g_conv_out=v_g_conv_out, g_attn_out=v_g_attn_out, w_out=v_w_out, g_ffn=v_g_ffn,
                     w_gate=v_w_gate, w_up=v_w_up, w_down=v_w_down)
    names = list(weights)
    out_g, out_d, out_m, out_v = [], [], [], []
    for nme in names:
        wgt = weights[nme]
        shape2 = wgt.shape[-2:] if wgt.ndim == 3 else wgt.shape
        state = (wgt.reshape(shape2), moments_m[nme].reshape(shape2), moments_v[nme].reshape(shape2))
        if nme in halves:
            g2, dlt, nm, nv = _adamw_shard(f"adamw_{nme}", *state, *halves[nme], where)
        else:
            g2 = grads[nme].reshape(shape2)
            dlt, nm, nv = _adamw(f"adamw_{nme}", state[0], g2, state[1], state[2])
        out_g.append(g2.reshape(wgt.shape))
        out_d.append(dlt.reshape(wgt.shape))
        out_m.append(nm.reshape(wgt.shape))
        out_v.append(nv.reshape(wgt.shape))
    return (loss, grad_x[None], *out_g, *out_d, *out_m, *out_v)
```

```python
import functools
from typing import Any, Callable, NamedTuple, Sequence

import jax
import jax.numpy as jnp
from jax import lax
from jax.experimental import pallas as pl
from jax.experimental.pallas import tpu as pltpu

F32 = jnp.float32
BF16 = jnp.bfloat16
MESH = pl.DeviceIdType.MESH

D_MODEL = 1024
D_CONV = 512
D_ATTN = 512
HEAD_DIM = 64
D_FF = 2816
D_IN = 3 * D_CONV + 3 * D_ATTN
DILATIONS = (1, 4, 16)
BAND = 128
EPS = 1e-6
NEG = -1e30
N_CHIPS = 4

ADAM_LR = 0.001
ADAM_B1 = 0.9
ADAM_B2 = 0.999
ADAM_EPS = 1e-08
ADAM_WD = 0.01
ADAM_STEP = 10

V7X_VMEM_BYTES = 64 * 1024 * 1024
VMEM_LIMIT = V7X_VMEM_BYTES - 8 * 1024 * 1024
ANY = pl.BlockSpec(memory_space=pl.ANY)
VMEM_WHOLE = pl.BlockSpec(memory_space=pltpu.VMEM)


def _params(*sem):
    return pltpu.CompilerParams(dimension_semantics=sem, vmem_limit_bytes=VMEM_LIMIT)


def _sds(shape, dtype):
    return jax.ShapeDtypeStruct(shape, dtype)


def _place():
    x, y, c = lax.axis_index("x"), lax.axis_index("y"), lax.axis_index("c")
    chips = [(1 - x, y), (x, 1 - y), (1 - x, 1 - y)]
    return x, y, c, 2 * x + y, chips, [2 * cx + cy for cx, cy in chips]


def _all_gather(shards):
    n = len(shards)

    def body(*refs):
        ins, outs, stage = refs[:n], refs[n:2 * n], refs[2 * n:3 * n]
        ssem, rsem, fsem, gsem, lsem, osem = refs[3 * n:]
        x, y, c, me, chips, cids = _place()
        sib = (x, y, 1 - c)

        def half(w, which):
            h = shards[w].shape[0] // 2
            return pl.ds(pl.multiple_of(which * h, 8), h)

        loads = [pltpu.make_async_copy(ins[w], stage[w], lsem.at[w]) for w in range(n)]
        local = [pltpu.make_async_copy(stage[w], outs[w].at[me], osem.at[w]) for w in range(n)]
        for cp in loads:
            cp.start()

        def chip_copy(w, j, src_slot):
            rows = half(w, c)
            return pltpu.make_async_remote_copy(
                src_ref=ins[w].at[rows], dst_ref=outs[w].at[src_slot, rows],
                send_sem=ssem.at[3 * w + j], recv_sem=rsem.at[3 * w + j],
                device_id=(*chips[j], c), device_id_type=MESH)

        def sib_copy(w, j, which):
            rows = half(w, which)
            return pltpu.make_async_remote_copy(
                src_ref=outs[w].at[cids[j], rows], dst_ref=outs[w].at[cids[j], rows],
                send_sem=fsem.at[3 * w + j], recv_sem=gsem.at[3 * w + j],
                device_id=sib, device_id_type=MESH)

        sends = [chip_copy(w, j, me) for w in range(n) for j in range(3)]
        for cp in sends:
            cp.start()
        for w in range(n):
            loads[w].wait()
            local[w].start()
        passed = []
        for w in range(n):
            for j in range(3):
                chip_copy(w, j, cids[j]).wait_recv()
                cp = sib_copy(w, j, c)
                cp.start()
                passed.append(cp)
        for w in range(n):
            for j in range(3):
                sib_copy(w, j, 1 - c).wait_recv()
        for cp in sends + passed:
            cp.wait_send()
        for cp in local:
            cp.wait()

    return pl.pallas_call(
        body, name="all_gather_weights",
        out_shape=[_sds((N_CHIPS,) + s.shape, s.dtype) for s in shards],
        in_specs=[ANY] * n, out_specs=[ANY] * n,
        scratch_shapes=[pltpu.VMEM(s.shape, s.dtype) for s in shards]
        + [pltpu.SemaphoreType.DMA((3 * n,))] * 4 + [pltpu.SemaphoreType.DMA((n,))] * 2,
        compiler_params=pltpu.CompilerParams(vmem_limit_bytes=VMEM_LIMIT),
    )(*shards)


class _Exchange(NamedTuple):
    srcs: Sequence[Any]
    lands: Sequence[Any]
    outs: Sequence[Any]
    n_sems: int
    copies: Callable


def _remote(src, dst, ssem, rsem, k, to):
    return pltpu.make_async_remote_copy(src_ref=src, dst_ref=dst, send_sem=ssem.at[k],
                                        recv_sem=rsem.at[k], device_id=to, device_id_type=MESH)


def _x_gather_chips(shards):
    def copies(srcs, lands, outs, ssem, rsem):
        _, _, c, me, chips, cids = _place()
        go, arrive = [], []
        for w, s in enumerate(shards):
            h = s.shape[0] // 2
            rows = pl.ds(pl.multiple_of(c * h, 8), h)
            for j in range(3):
                to = (*chips[j], c)
                go.append(_remote(srcs[w].at[rows], lands[w].at[me, rows], ssem, rsem, 3 * w + j, to))
                arrive.append(_remote(srcs[w].at[rows], lands[w].at[cids[j], rows], ssem, rsem,
                                      3 * w + j, to))
        return go, arrive

    lands = [jnp.broadcast_to(s[None], (N_CHIPS,) + s.shape) for s in shards]
    return _Exchange(shards, lands, [], 3 * len(shards), copies)


def _x_gather_sibling(gathered):
    def copies(srcs, lands, outs, ssem, rsem):
        x, y, c, _, _, cids = _place()
        go, arrive = [], []
        for w, g in enumerate(gathered):
            h = g.shape[1] // 2
            mine = pl.ds(pl.multiple_of(c * h, 8), h)
            theirs = pl.ds(pl.multiple_of((1 - c) * h, 8), h)
            for j in range(3):
                slab = lands[w].at[cids[j]]
                go.append(_remote(slab.at[mine], slab.at[mine], ssem, rsem, 3 * w + j, (x, y, 1 - c)))
                arrive.append(_remote(slab.at[theirs], slab.at[theirs], ssem, rsem, 3 * w + j,
                                      (x, y, 1 - c)))
        return go, arrive

    return _Exchange([], gathered, [], 3 * len(gathered), copies)


def _x_pair(grads):
    def copies(srcs, lands, outs, ssem, rsem):
        x, y, c, _, _, _ = _place()
        go = []
        for w, g in enumerate(grads):
            h = g.shape[1] // 2
            theirs = pl.ds(pl.multiple_of((1 - c) * h, 8), h)
            go.append(_remote(srcs[w].at[:, theirs, :], outs[w], ssem, rsem, w, (x, y, 1 - c)))
        return go, go

    outs = [_sds((N_CHIPS, g.shape[1] // 2, g.shape[2]), g.dtype) for g in grads]
    return _Exchange(grads, [], outs, len(grads), copies)


def _x_chips(parts):
    def copies(srcs, lands, outs, ssem, rsem):
        _, _, c, _, chips, cids = _place()
        go = [_remote(srcs[w].at[cids[j]], outs[w].at[j], ssem, rsem, 3 * w + j, (*chips[j], c))
              for w in range(len(parts)) for j in range(3)]
        return go, go

    outs = [_sds((3,) + p.shape[1:], p.dtype) for p in parts]
    return _Exchange(parts, [], outs, 3 * len(parts), copies)


def _x_share(halves):
    def copies(srcs, lands, outs, ssem, rsem):
        x, y, c, _, _, _ = _place()
        go = [_remote(srcs[w], outs[w], ssem, rsem, w, (x, y, 1 - c)) for w in range(len(halves))]
        return go, go

    return _Exchange(halves, [], [_sds(h.shape, h.dtype) for h in halves], len(halves), copies)


def _call(body, args, *, name, grid, in_specs, out_specs, out_shape, scratch_shapes=(),
          semantics=None, carry=None):
    single = not isinstance(out_shape, (list, tuple))
    out_shape = [out_shape] if single else list(out_shape)
    out_specs = [out_specs] if single else list(out_specs)
    if carry is None:
        res = pl.pallas_call(
            body, name=name, grid=grid, in_specs=list(in_specs), out_specs=out_specs,
            out_shape=out_shape, scratch_shapes=list(scratch_shapes),
            compiler_params=_params(*(semantics or ("arbitrary",) * len(grid))))(*args)
        return res[0] if single else res
    n_in, n_out, n_scr = len(args), len(out_shape), len(scratch_shapes)
    n_src, n_land, n_new = len(carry.srcs), len(carry.lands), len(carry.outs)

    def carrying(*refs):
        at = 0
        parts = []
        for n in (n_in, n_src, n_land, n_out, n_land, n_new, n_scr, 2):
            parts.append(refs[at:at + n])
            at += n
        ins, srcs, _, outs, lands, news, scratch, (ssem, rsem) = parts
        ids = [pl.program_id(a) for a in range(len(grid))]
        first = functools.reduce(jnp.logical_and, [i == 0 for i in ids])
        last = functools.reduce(jnp.logical_and, [i == g - 1 for i, g in zip(ids, grid)])
        go, arrive = carry.copies(srcs, lands, news, ssem, rsem)

        @pl.when(first)
        def _():
            for cp in go:
                cp.start()

        body(*ins, *outs, *scratch)

        @pl.when(last)
        def _():
            for cp in go:
                cp.wait_send()
            for cp in arrive:
                cp.wait_recv()

    res = pl.pallas_call(
        carrying, name=name, grid=grid,
        in_specs=list(in_specs) + [ANY] * (n_src + n_land),
        out_specs=out_specs + [ANY] * (n_land + n_new),
        out_shape=out_shape + [_sds(a.shape, a.dtype) for a in carry.lands] + list(carry.outs),
        input_output_aliases={n_in + n_src + i: n_out + i for i in range(n_land)},
        scratch_shapes=list(scratch_shapes) + [pltpu.SemaphoreType.DMA((carry.n_sems,))] * 2,
        compiler_params=_params(*(("arbitrary",) * len(grid))))(*args, *carry.srcs, *carry.lands)
    own = res[:n_out]
    return (own[0] if single else own), res[n_out:]


def _exchange_alone(name, exchange):
    def body(x_ref, o_ref):
        o_ref[...] = x_ref[...]

    blk = pl.BlockSpec((8, 128), lambda i: (0, 0))
    _, res = _call(body, [jnp.zeros((8, 128), F32)], name=name, grid=(1,), in_specs=[blk],
                   out_specs=blk, out_shape=_sds((8, 128), F32), carry=exchange)
    return res


def _row_block(r, want):
    return max(d for d in range(1, min(want, r) + 1) if r % d == 0 and (d % 8 == 0 or d == r))


def _pair_sum(name, full, got, where):
    _, r, n = full.shape
    h = r // 2
    tr = _row_block(h, 256)
    nb = h // tr

    def body(w_ref, a_ref, b_ref, o_ref, own_ref):
        total = a_ref[...] + b_ref[...]
        o_ref[...] = total.astype(BF16)

        @pl.when(pl.program_id(1) == w_ref[1])
        def _():
            own_ref[...] = total[0]

    blk = pl.BlockSpec((1, tr, n), lambda i, s, w: (s, i, 0))
    return pl.pallas_call(
        body, name=name, out_shape=[_sds(got.shape, BF16), _sds((h, n), F32)],
        grid_spec=pltpu.PrefetchScalarGridSpec(
            num_scalar_prefetch=1, grid=(nb, N_CHIPS),
            in_specs=[pl.BlockSpec((1, tr, n), lambda i, s, w: (s, w[0] * nb + i, 0)), blk],
            out_specs=[blk, pl.BlockSpec((tr, n), lambda i, s, w: (i, 0))]),
        compiler_params=_params("parallel", "arbitrary"),
    )(where, full, got)


def _chip_sum(name, own, got):
    h, n = own.shape
    tr = _row_block(h, 256)

    def body(a_ref, b0, b1, b2, o_ref):
        o_ref[...] = ((a_ref[...] + b0[0].astype(F32)) + b1[0].astype(F32)) + b2[0].astype(F32)

    def slot(j):
        return pl.BlockSpec((1, tr, n), lambda i: (j, i, 0))

    blk = pl.BlockSpec((tr, n), lambda i: (i, 0))
    return pl.pallas_call(
        body, name=name, grid=(h // tr,), out_shape=_sds((h, n), F32),
        in_specs=[blk, slot(0), slot(1), slot(2)], out_specs=blk,
        compiler_params=_params("parallel"),
    )(own, got, got, got)


SMALL_ROWS = 16
SMALL_LAYOUT = (
    ("g_mix", 0, 0, 1, 1024), ("g_ffn", 1, 0, 1, 1024), ("g_conv_out", 2, 0, 1, 512),
    ("g_attn_out", 2, 512, 1, 512), ("g_q", 3, 0, 1, 512), ("g_k", 3, 512, 1, 512),
    ("loss", 4, 0, 1, 128), ("conv_w", 8, 0, 8, 512))


def _small_all_reduce(parts):
    names = [s[0] for s in SMALL_LAYOUT]

    def body(*refs):
        ins = refs[:len(names)]
        out_ref, stage, buf, ssem, rsem = refs[len(names):]
        x, y, c, _, _, _ = _place()
        me = 4 * x + 2 * y + c
        stage[...] = jnp.zeros_like(stage)
        for ref, (_, r0, c0, nr, nc) in zip(ins, SMALL_LAYOUT):
            stage[r0:r0 + nr, c0:c0 + nc] = ref[0:nr, :]
        buf[me] = stage[...]
        peers = []
        for d in range(1, 8):
            px = 1 - x if d & 4 else x
            py = 1 - y if d & 2 else y
            pc = 1 - c if d & 1 else c
            peers.append(((px, py, pc), 4 * px + 2 * py + pc))
        sends = [pltpu.make_async_remote_copy(
            src_ref=stage, dst_ref=buf.at[me], send_sem=ssem.at[k], recv_sem=rsem.at[k],
            device_id=peer, device_id_type=MESH) for k, (peer, _) in enumerate(peers)]
        for cp in sends:
            cp.start()
        for k, (peer, pid) in enumerate(peers):
            pltpu.make_async_remote_copy(
                src_ref=stage, dst_ref=buf.at[pid], send_sem=ssem.at[k], recv_sem=rsem.at[k],
                device_id=peer, device_id_type=MESH).wait_recv()
        for cp in sends:
            cp.wait_send()
        acc = buf[0]
        for k in range(1, 8):
            acc = acc + buf[k]
        out_ref[...] = acc

    return pl.pallas_call(
        body, name="small_all_reduce", out_shape=_sds((SMALL_ROWS, 1024), F32),
        in_specs=[VMEM_WHOLE] * len(names), out_specs=VMEM_WHOLE,
        scratch_shapes=[pltpu.VMEM((SMALL_ROWS, 1024), F32), pltpu.VMEM((8, SMALL_ROWS, 1024), F32),
                        pltpu.SemaphoreType.DMA((7,)), pltpu.SemaphoreType.DMA((7,))],
    )(*[parts[k] for k in names])


def _dot(a, b):
    return jnp.dot(a, b, preferred_element_type=F32)


def _dot_nt(a, b):
    return lax.dot_general(a, b, (((1,), (1,)), ((), ())), preferred_element_type=F32)


def _dot_tn(a, b):
    return lax.dot_general(a, b, (((0,), (0,)), ((), ())), preferred_element_type=F32)


def _sigmoid(v):
    return 1.0 / (1.0 + jnp.exp(-v))


def _rms_scale(v):
    return lax.rsqrt(jnp.mean(v * v, axis=-1, keepdims=True) + EPS)


def _rms_bwd(v, r, g, dy):
    vh = v * r
    dh = dy * g
    return r * (dh - vh * jnp.mean(dh * vh, axis=-1, keepdims=True)), vh


def _head_sum(a, ones_bd):
    hi = a.astype(BF16)
    lo = (a - hi.astype(F32)).astype(BF16)
    return _dot(hi, ones_bd) + _dot(lo, ones_bd)


def _head_rms_scale(v, ones_bd):
    return lax.rsqrt(_head_sum(v * v, ones_bd) * (1.0 / HEAD_DIM) + EPS)


def _norm_matmul(name, x, g, ws, tm, tn, swiglu, out_dtype=F32):
    t, d = x.shape
    n = ws[0].shape[1]
    nw = len(ws)

    def body(x_ref, g_ref, *refs):
        w_refs, h_ref, o_refs = refs[:nw], refs[nw], refs[nw + 1:2 * nw + 1]
        hs = refs[-1]

        @pl.when(pl.program_id(1) == 0)
        def _():
            xv = x_ref[...]
            h = (xv * _rms_scale(xv) * g_ref[...]).astype(BF16)
            hs[...] = h
            h_ref[...] = h

        h = hs[...]
        outs = [_dot(h, w[...]) for w in w_refs]
        for o_ref, o in zip(o_refs, outs):
            o_ref[...] = o.astype(out_dtype)
        if swiglu:
            refs[2 * nw + 1][...] = (outs[0] * _sigmoid(outs[0]) * outs[1]).astype(BF16)

    row = pl.BlockSpec((tm, d), lambda i, j: (i, 0))
    col = pl.BlockSpec((tm, tn), lambda i, j: (i, j))
    out_shape = [_sds((t, d), BF16)] + [_sds((t, n), out_dtype)] * nw
    out_specs = [row] + [col] * nw
    if swiglu:
        out_shape.append(_sds((t, n), BF16))
        out_specs.append(col)
    return pl.pallas_call(
        body, name=name, grid=(t // tm, n // tn), out_shape=out_shape,
        in_specs=[row, pl.BlockSpec((1, d), lambda i, j: (0, 0))]
        + [pl.BlockSpec((d, tn), lambda i, j: (0, j))] * nw,
        out_specs=out_specs, scratch_shapes=[pltpu.VMEM((tm, d), BF16)],
        compiler_params=_params("parallel", "arbitrary"),
    )(x, g, *ws)


def _matmul(name, a, w, extras, out_dtypes, epilogue, tm, tn, transposed_w=False, loss=False):
    t, k = a.shape
    n = w.shape[0] if transposed_w else w.shape[1]
    ne, no = len(extras), len(out_dtypes)

    def body(a_ref, w_ref, *refs):
        e_refs, o_refs = refs[:ne], refs[ne:]
        acc = _dot_nt(a_ref[...], w_ref[...]) if transposed_w else _dot(a_ref[...], w_ref[...])
        res = epilogue(acc, *[e[...] for e in e_refs])
        for o_ref, r in zip(o_refs[:no], res[:no]):
            o_ref[...] = r.astype(o_ref.dtype)
        if loss:
            first = jnp.logical_and(pl.program_id(0) == 0, pl.program_id(1) == 0)

            @pl.when(first)
            def _():
                o_refs[no][...] = jnp.zeros_like(o_refs[no])

            o_refs[no][...] += res[no]

    col = pl.BlockSpec((tm, tn), lambda i, j: (i, j))
    w_spec = (pl.BlockSpec((tn, k), lambda i, j: (j, 0)) if transposed_w
              else pl.BlockSpec((k, tn), lambda i, j: (0, j)))
    out_shape = [_sds((t, n), dt) for dt in out_dtypes]
    out_specs = [col] * no
    if loss:
        out_shape.append(_sds((8, 128), F32))
        out_specs.append(pl.BlockSpec((8, 128), lambda i, j: (0, 0)))
    return pl.pallas_call(
        body, name=name, grid=(t // tm, n // tn), out_shape=out_shape,
        in_specs=[pl.BlockSpec((tm, k), lambda i, j: (i, 0)), w_spec] + [col] * ne,
        out_specs=out_specs,
        compiler_params=_params(*(("arbitrary", "arbitrary") if loss else ("parallel", "parallel"))),
    )(a, w, *extras)


def _matmul_norm_bwd(name, pairs, x, dres, g, tm, carry=None):
    t, d = x.shape
    npairs = len(pairs)

    def body(*refs):
        a_refs, w_refs = refs[:npairs], refs[npairs:2 * npairs]
        x_ref, r_ref, g_ref, dx_ref, dxb_ref, dg_ref = refs[2 * npairs:]
        dy = _dot_nt(a_refs[0][...], w_refs[0][...])
        for a_ref, w_ref in zip(a_refs[1:], w_refs[1:]):
            dy = dy + _dot_nt(a_ref[...], w_ref[...])
        xv = x_ref[...]
        dx, xh = _rms_bwd(xv, _rms_scale(xv), g_ref[...], dy)
        dx = dx + r_ref[...]
        dx_ref[...] = dx
        dxb_ref[...] = dx.astype(BF16)

        @pl.when(pl.program_id(0) == 0)
        def _():
            dg_ref[...] = jnp.zeros_like(dg_ref)

        dg_ref[...] += jnp.sum(dy * xh, axis=0, keepdims=True)

    row = pl.BlockSpec((tm, d), lambda i: (i, 0))
    vec = pl.BlockSpec((1, d), lambda i: (0, 0))
    return _call(
        body, [a for a, _ in pairs] + [w for _, w in pairs] + [x, dres, g], name=name,
        grid=(t // tm,), out_shape=[_sds((t, d), F32), _sds((t, d), BF16), _sds((1, d), F32)],
        in_specs=[pl.BlockSpec((tm, a.shape[1]), lambda i: (i, 0)) for a, _ in pairs]
        + [pl.BlockSpec(w.shape, lambda i: (0, 0)) for _, w in pairs] + [row, row, vec],
        out_specs=[row, row, vec], carry=carry)


def _matmul_tn(name, a, g, tn, tk):
    t, ka = a.shape
    n = g.shape[1]

    def body(a_ref, g_ref, o_ref):
        @pl.when(pl.program_id(1) == 0)
        def _():
            o_ref[...] = jnp.zeros_like(o_ref)

        o_ref[...] += _dot_tn(a_ref[...], g_ref[...])

    return pl.pallas_call(
        body, name=name, grid=(n // tn, t // tk), out_shape=_sds((ka, n), F32),
        in_specs=[pl.BlockSpec((tk, ka), lambda j, s: (s, 0)),
                  pl.BlockSpec((tk, tn), lambda j, s: (s, j))],
        out_specs=pl.BlockSpec((ka, tn), lambda j, s: (0, j)),
        compiler_params=_params("parallel", "arbitrary"),
    )(a, g)


def _elementwise(name, fn, ins, out_dtypes, tr):
    r, n = ins[0].shape
    tr = _row_block(r, tr)
    ni = len(ins)

    def body(*refs):
        res = fn(*[ref[...] for ref in refs[:ni]])
        for o_ref, v in zip(refs[ni:], res):
            o_ref[...] = v.astype(o_ref.dtype)

    blk = pl.BlockSpec((tr, n), lambda i: (i, 0))
    return pl.pallas_call(
        body, name=name, grid=(r // tr,), out_shape=[_sds((r, n), dt) for dt in out_dtypes],
        in_specs=[blk] * ni, out_specs=[blk] * len(out_dtypes),
        compiler_params=_params("parallel"),
    )(*ins)


def _adamw_update(w, g, m, v):
    m = ADAM_B1 * m + (1.0 - ADAM_B1) * g
    v = ADAM_B2 * v + (1.0 - ADAM_B2) * (g * g)
    m_hat = m / (1.0 - ADAM_B1 ** ADAM_STEP)
    v_hat = v / (1.0 - ADAM_B2 ** ADAM_STEP)
    return -ADAM_LR * (m_hat / (jnp.sqrt(v_hat) + ADAM_EPS) + ADAM_WD * w), m, v


def _adamw(name, w, g, m, v):
    return _elementwise(name, _adamw_update, [w, g, m, v], [F32] * 3, 256)


def _adamw_shard(name, w, m, v, mine, theirs, where):
    r, n = w.shape
    h = r // 2
    tr = _row_block(h, 256)
    nb = h // tr

    def body(w_ref, p_ref, m_ref, v_ref, a_ref, b_ref, g_ref, d_ref, nm_ref, nv_ref):
        g = jnp.where(pl.program_id(0) == w_ref[0], a_ref[...], b_ref[...])
        g_ref[...] = g
        d_ref[...], nm_ref[...], nv_ref[...] = _adamw_update(p_ref[...], g, m_ref[...], v_ref[...])

    whole = pl.BlockSpec((tr, n), lambda s, i, c: (s * nb + i, 0))
    half = pl.BlockSpec((tr, n), lambda s, i, c: (i, 0))
    return pl.pallas_call(
        body, name=name, out_shape=[_sds((r, n), F32)] * 4,
        grid_spec=pltpu.PrefetchScalarGridSpec(
            num_scalar_prefetch=1, grid=(2, nb), in_specs=[whole] * 3 + [half] * 2,
            out_specs=[whole] * 4),
        compiler_params=_params("parallel", "parallel"),
    )(where, w, m, v, mine, theirs)


def _qkv_prepare(z, gq, gk, ones_bd, tm):
    t = z.shape[0]

    def body(zq_ref, zk_ref, gq_ref, gk_ref, bd_ref, q_ref, k_ref):
        bd = bd_ref[...]
        q = zq_ref[...]
        k = zk_ref[...]
        q_ref[...] = (q * _head_rms_scale(q, bd) * gq_ref[...]) * HEAD_DIM ** -0.5
        k_ref[...] = k * _head_rms_scale(k, bd) * gk_ref[...]

    vec = pl.BlockSpec((1, 512), lambda i: (0, 0))
    out = pl.BlockSpec((tm, 512), lambda i: (i, 0))
    return pl.pallas_call(
        body, name="qkv_prepare", grid=(t // tm,), out_shape=[_sds((t, 512), F32)] * 2,
        in_specs=[pl.BlockSpec((tm, 512), lambda i: (i, 3)), pl.BlockSpec((tm, 512), lambda i: (i, 4)),
                  vec, vec, pl.BlockSpec((512, 512), lambda i: (0, 0))],
        out_specs=[out] * 2, compiler_params=_params("parallel"),
    )(z, z, gq, gk, ones_bd)


TOK = 2048
UNITS = TOK // BAND


def _stack_masks():
    row = lax.broadcasted_iota(jnp.int32, (2 * BAND, 2 * BAND), 0) & (BAND - 1)
    col = lax.broadcasted_iota(jnp.int32, (2 * BAND, 2 * BAND), 1)
    lane = lax.broadcasted_iota(jnp.int32, (BAND, BAND), 1)
    head0 = lane < HEAD_DIM
    ones = [jnp.where(head0, 1.0, 0.0).astype(BF16), jnp.where(head0, 0.0, 1.0).astype(BF16)]
    return col - row, col, head0, ones


def _gather(srcs, dst, d, last_rows=None):
    per = TOK // d
    counts = [per] * len(srcs)
    if last_rows is not None:
        counts[-1] = last_rows
    at = 0
    for r in range(d):
        for src, n in zip(srcs, counts):
            rows = src[pl.ds(r, n, stride=d), :] if d > 1 else src[pl.ds(0, n), :]
            dst[pl.ds(at, n), :] = rows.astype(dst.dtype)
            at += n


def _scatter_add(out_ref, src, d, per_src, offset, first):
    per = TOK // d
    if d == 1:
        val = src[pl.ds(offset, per), :]
        out_ref[...] = val if first else out_ref[...] + val
        return
    for r in range(d):
        val = src[pl.ds(r * per_src + offset, per), :]
        idx = pl.ds(r, per, stride=d)
        out_ref[idx, :] = val if first else out_ref[idx, :] + val


def _attn_fwd(q, k, v, v_col, carry=None):
    t = q.shape[0]
    nblk = t // TOK

    def body(q_ref, kp_ref, k_ref, vp_ref, v_ref, y_ref, l_ref, qs, ks, vs, ob, lb, on, ln):
        i = pl.program_id(1)
        diff, col, head0, hm = _stack_masks()
        band_ok = jnp.logical_and(diff >= 0, diff <= BAND)
        for g, d in enumerate(DILATIONS):
            per = TOK // d
            nb = per // BAND
            _gather([q_ref], qs, d)
            _gather([kp_ref, k_ref], ks, d)
            _gather([vp_ref, v_ref], vs, d)

            def unit(u, carry):
                r, b = u // nb, u % nb
                qu = qs[pl.ds(pl.multiple_of(u * BAND, BAND), BAND), :]
                start = pl.multiple_of(r * 2 * per + per + (b - 1) * BAND, BAND)
                kw = ks[pl.ds(start, 2 * BAND), :]
                vw = vs[pl.ds(start, 2 * BAND), :]
                lo = jnp.where(jnp.logical_and(i == 0, b == 0), BAND, 0)
                s = _dot_nt(jnp.concatenate([qu * hm[0], qu * hm[1]], axis=0), kw)
                s = jnp.where(jnp.logical_and(band_ok, col >= lo), s, NEG)
                mx = jnp.max(s, axis=-1, keepdims=True)
                e = jnp.exp(s - mx)
                den = jnp.sum(e, axis=-1, keepdims=True)
                o2 = _dot(e.astype(BF16), vw) / den
                l2 = jnp.broadcast_to(mx + jnp.log(den), (2 * BAND, BAND))
                rows = pl.ds(pl.multiple_of(u * BAND, BAND), BAND)
                ob[rows, :] = jnp.where(head0, o2[:BAND], o2[BAND:])
                lb[rows, :] = jnp.where(head0, l2[:BAND], l2[BAND:])
                return carry

            lax.fori_loop(0, UNITS, unit, 0, unroll=8)
            _scatter_add(on.at[g], ob, d, per, 0, True)
            _scatter_add(ln.at[g], lb, d, per, 0, True)
        ls = [ln[0], ln[1], ln[2]]
        mx = jnp.maximum(jnp.maximum(ls[0], ls[1]), ls[2])
        es = [jnp.exp(l - mx) for l in ls]
        tot = es[0] + es[1] + es[2]
        y_ref[...] = (es[0] * on[0] + es[1] * on[1] + es[2] * on[2]) / tot
        l_ref[...] = mx + jnp.log(tot)

    main = pl.BlockSpec((TOK, BAND), lambda j, i: (i, j))
    prev = pl.BlockSpec((TOK, BAND), lambda j, i: (jnp.maximum(i - 1, 0), j))
    vmain = pl.BlockSpec((TOK, BAND), lambda j, i: (i, j + v_col))
    vprev = pl.BlockSpec((TOK, BAND), lambda j, i: (jnp.maximum(i - 1, 0), j + v_col))
    return _call(
        body, [q, k, k, v, v], name="attn_fwd", grid=(D_ATTN // BAND, nblk),
        out_shape=[_sds((t, D_ATTN), F32)] * 2,
        in_specs=[main, prev, main, vprev, vmain], out_specs=[main, main],
        scratch_shapes=[pltpu.VMEM((TOK, BAND), BF16), pltpu.VMEM((2 * TOK, BAND), BF16),
                        pltpu.VMEM((2 * TOK, BAND), BF16), pltpu.VMEM((TOK, BAND), F32),
                        pltpu.VMEM((TOK, BAND), F32), pltpu.VMEM((3, TOK, BAND), F32),
                        pltpu.VMEM((3, TOK, BAND), F32)],
        semantics=("parallel", "parallel"), carry=carry)


def _attn_bwd(q, k, v, v_col, do, lse, dd, carry=None):
    t = q.shape[0]
    nblk = t // TOK

    def body(q_ref, qn_ref, kp_ref, k_ref, vp_ref, v_ref, do_ref, don_ref, l_ref, ln_ref, d_ref,
             dn_ref, dq_ref, dk_ref, dv_ref, qs, dos, ks, vs, lsc, dsc, dqb, dkb, dvb):
        i = pl.program_id(1)
        diff, col, head0, hm = _stack_masks()
        band_ok = jnp.logical_and(diff >= 0, diff <= BAND)
        for g, d in enumerate(DILATIONS):
            per = TOK // d
            nb = per // BAND
            pad = per + BAND
            _gather([q_ref, qn_ref], qs, d, BAND)
            _gather([do_ref, don_ref], dos, d, BAND)
            _gather([l_ref, ln_ref], lsc, d, BAND)
            _gather([d_ref, dn_ref], dsc, d, BAND)
            _gather([kp_ref, k_ref], ks, d)
            _gather([vp_ref, v_ref], vs, d)
            dkb[...] = jnp.zeros_like(dkb)
            dvb[...] = jnp.zeros_like(dvb)

            def stacked(qrow):
                rows = pl.ds(qrow, BAND)
                qu, dou, lu, du = qs[rows, :], dos[rows, :], lsc[rows, :], dsc[rows, :]
                q2 = jnp.concatenate([qu * hm[0], qu * hm[1]], axis=0)
                do2 = jnp.concatenate([dou * hm[0], dou * hm[1]], axis=0)
                l2 = jnp.concatenate([lu[:, 0:1], lu[:, HEAD_DIM:HEAD_DIM + 1]], axis=0)
                d2 = jnp.concatenate([du[:, 0:1], du[:, HEAD_DIM:HEAD_DIM + 1]], axis=0)
                return q2, do2, l2, d2

            def unit(u, carry):
                r, b = u // nb, u % nb
                q2, do2, l2, d2 = stacked(pl.multiple_of(r * pad + b * BAND, BAND))
                start = pl.multiple_of(r * 2 * per + per + (b - 1) * BAND, BAND)
                kw = ks[pl.ds(start, 2 * BAND), :]
                vw = vs[pl.ds(start, 2 * BAND), :]
                lo = jnp.where(jnp.logical_and(i == 0, b == 0), BAND, 0)
                ok = jnp.logical_and(band_ok, col >= lo)
                p = jnp.where(ok, jnp.exp(_dot_nt(q2, kw) - l2), 0.0)
                ds = (p * (_dot_nt(do2, vw) - d2)).astype(BF16)
                dq2 = _dot(ds, kw)
                dqb[pl.ds(pl.multiple_of(u * BAND, BAND), BAND), :] = jnp.where(
                    head0, dq2[:BAND], dq2[BAND:])
                acc = pl.ds(pl.multiple_of(r * pad + b * BAND, BAND), 2 * BAND)
                dkb[acc, :] += _dot_tn(ds, q2)
                dvb[acc, :] += _dot_tn(p.astype(BF16), do2)
                return carry

            lax.fori_loop(0, UNITS, unit, 0, unroll=8)

            def halo(r, carry):
                q2, do2, l2, d2 = stacked(pl.multiple_of(r * pad + per, BAND))
                start = pl.multiple_of(r * 2 * per + per + (nb - 1) * BAND, BAND)
                kw = ks[pl.ds(start, BAND), :]
                vw = vs[pl.ds(start, BAND), :]
                ok = diff[:, :BAND] >= jnp.where(i < nblk - 1, 0, 2 * BAND)
                p = jnp.where(ok, jnp.exp(_dot_nt(q2, kw) - l2), 0.0)
                ds = (p * (_dot_nt(do2, vw) - d2)).astype(BF16)
                acc = pl.ds(pl.multiple_of(r * pad + nb * BAND, BAND), BAND)
                dkb[acc, :] += _dot_tn(ds, q2)
                dvb[acc, :] += _dot_tn(p.astype(BF16), do2)
                return carry

            lax.fori_loop(0, d, halo, 0, unroll=min(d, 8))
            _scatter_add(dq_ref, dqb, d, per, 0, g == 0)
            _scatter_add(dk_ref, dkb, d, pad, BAND, g == 0)
            _scatter_add(dv_ref, dvb, d, pad, BAND, g == 0)

    main = pl.BlockSpec((TOK, BAND), lambda j, i: (i, j))
    prev = pl.BlockSpec((TOK, BAND), lambda j, i: (jnp.maximum(i - 1, 0), j))
    nxt = pl.BlockSpec((TOK, BAND), lambda j, i: (jnp.minimum(i + 1, nblk - 1), j))
    vmain = pl.BlockSpec((TOK, BAND), lambda j, i: (i, j + v_col))
    vprev = pl.BlockSpec((TOK, BAND), lambda j, i: (jnp.maximum(i - 1, 0), j + v_col))
    acc_rows = max(d * (TOK // d + BAND) for d in DILATIONS)
    return _call(
        body, [q, q, k, k, v, v, do, do, lse, lse, dd, dd], name="attn_bwd",
        grid=(D_ATTN // BAND, nblk), out_shape=[_sds((t, D_ATTN), F32)] * 3,
        in_specs=[main, nxt, prev, main, vprev, vmain, main, nxt, main, nxt, main, nxt],
        out_specs=[main] * 3,
        scratch_shapes=[pltpu.VMEM((2 * TOK, BAND), BF16)] * 4
        + [pltpu.VMEM((2 * TOK, BAND), F32)] * 2 + [pltpu.VMEM((TOK, BAND), F32)]
        + [pltpu.VMEM((acc_rows, BAND), F32)] * 2,
        semantics=("parallel", "parallel"), carry=carry)


def _halo_rows(tm, t):
    per = tm // 8
    prev = lambda i: (jnp.maximum(i * per - 1, 0), 0)
    nxt = lambda i: (jnp.minimum((i + 1) * per, t // 8 - 1), 0)
    return prev, nxt


def _mixer_out(z, cw, y_attn, g_conv, g_attn, tm, carry=None):
    t = z.shape[0]
    prev, _ = _halo_rows(tm, t)

    def body(z_ref, zp_ref, cw_ref, y_ref, gc_ref, ga_ref, mix_ref):
        i = pl.program_id(0)
        keep = jnp.where(i > 0, 1.0, 0.0)
        cu = jnp.concatenate([zp_ref[:, 0:512] * zp_ref[:, 1024:1536] * keep,
                              z_ref[:, 0:512] * z_ref[:, 1024:1536]], axis=0)
        c = (cw_ref[0:1, :] * pltpu.roll(cu, 2, 0) + cw_ref[1:2, :] * pltpu.roll(cu, 1, 0)
             + cw_ref[2:3, :] * cu)[8:, :]
        yc = z_ref[:, 512:1024] * c
        mix_ref[:, 0:512] = (yc * _rms_scale(yc) * gc_ref[...]).astype(BF16)
        ya = y_ref[...]
        mix_ref[:, 512:1024] = (ya * _rms_scale(ya) * ga_ref[...]).astype(BF16)

    blk = pl.BlockSpec((tm, 512), lambda i: (i, 0))
    vec = pl.BlockSpec((1, 512), lambda i: (0, 0))
    return _call(
        body, [z, z, cw, y_attn, g_conv, g_attn], name="mixer_out", grid=(t // tm,),
        out_shape=_sds((t, 1024), BF16),
        in_specs=[pl.BlockSpec((tm, 1536), lambda i: (i, 0)), pl.BlockSpec((8, 1536), prev),
                  pl.BlockSpec((8, 512), lambda i: (0, 0)), blk, vec, vec],
        out_specs=pl.BlockSpec((tm, 1024), lambda i: (i, 0)),
        semantics=("parallel",), carry=carry)


def _mixer_bwd(z, dmix, y_attn, cw, g_conv, g_attn, ones_bd, tm, carry=None):
    t = z.shape[0]
    nblk = t // tm
    prev, nxt = _halo_rows(tm, t)
    e = tm + 16

    def body(z_ref, zp_ref, zn_ref, dm_ref, dmn_ref, y_ref, cw_ref, gc_ref, ga_ref, bd_ref,
             dz_ref, do_ref, dd_ref, dcw_ref, dgc_ref, dga_ref):
        i = pl.program_id(0)
        rows = lax.broadcasted_iota(jnp.int32, (e, 1), 0)
        lo = jnp.where(i > 0, 0, 8)
        hi = jnp.where(i < nblk - 1, e, tm + 8)
        ze = jnp.concatenate([zp_ref[...], z_ref[...], zn_ref[...]], axis=0)
        u, gb, gcv = ze[:, 0:512], ze[:, 512:1024], ze[:, 1024:1536]
        w0, w1, w2 = cw_ref[0:1, :], cw_ref[1:2, :], cw_ref[2:3, :]
        cu = jnp.where(rows >= lo, gcv * u, 0.0)
        cu1, cu2 = pltpu.roll(cu, 1, 0), pltpu.roll(cu, 2, 0)
        c = w0 * cu2 + w1 * cu1 + w2 * cu
        yc = gb * c
        dma = jnp.concatenate([jnp.zeros((8, 512), F32), dm_ref[:, 0:512], dmn_ref[...]], axis=0)
        dyc, ych = _rms_bwd(yc, _rms_scale(yc), gc_ref[...], dma)
        dc = jnp.where(jnp.logical_and(rows >= 8, rows < hi), dyc * gb, 0.0)
        dcu = w0 * pltpu.roll(dc, e - 2, 0) + w1 * pltpu.roll(dc, e - 1, 0) + w2 * dc
        mid = slice(8, 8 + tm)
        dz_ref[:, 0:512] = (dcu * gcv)[mid, :].astype(BF16)
        dz_ref[:, 512:1024] = (dyc * c)[mid, :].astype(BF16)
        dz_ref[:, 1024:1536] = (dcu * u)[mid, :].astype(BF16)

        ya = y_ref[...]
        dmb = dm_ref[:, 512:1024]
        dya, yah = _rms_bwd(ya, _rms_scale(ya), ga_ref[...], dmb)
        do_ref[...] = dya
        dd_ref[...] = _head_sum(dya * ya, bd_ref[...])

        @pl.when(i == 0)
        def _():
            dcw_ref[...] = jnp.zeros_like(dcw_ref)
            dgc_ref[...] = jnp.zeros_like(dgc_ref)
            dga_ref[...] = jnp.zeros_like(dga_ref)

        dcm = jnp.where(rows < tm + 8, dc, 0.0)
        dcw_ref[0:1, :] += jnp.sum(dcm * cu2, axis=0, keepdims=True)
        dcw_ref[1:2, :] += jnp.sum(dcm * cu1, axis=0, keepdims=True)
        dcw_ref[2:3, :] += jnp.sum(dcm * cu, axis=0, keepdims=True)
        dgc_ref[...] += jnp.sum((dma * ych)[mid, :], axis=0, keepdims=True)
        dga_ref[...] += jnp.sum(dmb * yah, axis=0, keepdims=True)

    blk = pl.BlockSpec((tm, 512), lambda i: (i, 0))
    vec = pl.BlockSpec((1, 512), lambda i: (0, 0))
    cwb = pl.BlockSpec((8, 512), lambda i: (0, 0))
    return _call(
        body, [z, z, z, dmix, dmix, y_attn, cw, g_conv, g_attn, ones_bd], name="mixer_bwd",
        grid=(nblk,),
        out_shape=[_sds((t, 1536), BF16), _sds((t, 512), F32), _sds((t, 512), F32),
                   _sds((8, 512), F32), _sds((1, 512), F32), _sds((1, 512), F32)],
        in_specs=[pl.BlockSpec((tm, 1536), lambda i: (i, 0)), pl.BlockSpec((8, 1536), prev),
                  pl.BlockSpec((8, 1536), nxt), pl.BlockSpec((tm, 1024), lambda i: (i, 0)),
                  pl.BlockSpec((8, 512), nxt), blk, cwb, vec, vec,
                  pl.BlockSpec((512, 512), lambda i: (0, 0))],
        out_specs=[pl.BlockSpec((tm, 1536), lambda i: (i, 0)), blk, blk, cwb, vec, vec],
        carry=carry)


def _qkv_bwd(z, dzc, dqn, dkn, dv, gq, gk, ones_bd, tm, carry=None):
    t = z.shape[0]

    def body(zq_ref, zk_ref, dzc_ref, dqn_ref, dkn_ref, dv_ref, gq_ref, gk_ref, bd_ref,
             dz_ref, dgq_ref, dgk_ref):
        bd = bd_ref[...]

        @pl.when(pl.program_id(0) == 0)
        def _():
            dgq_ref[...] = jnp.zeros_like(dgq_ref)
            dgk_ref[...] = jnp.zeros_like(dgk_ref)

        def back(v, dn, g, scale):
            r = _head_rms_scale(v, bd)
            vh = v * r
            dh = dn * (g * scale)
            dv = r * (dh - vh * (_head_sum(dh * vh, bd) * (1.0 / HEAD_DIM)))
            return dv, jnp.sum(dn * scale * vh, axis=0, keepdims=True)

        dq, dgq = back(zq_ref[...], dqn_ref[...], gq_ref[...], HEAD_DIM ** -0.5)
        dk, dgk = back(zk_ref[...], dkn_ref[...], gk_ref[...], 1.0)
        dgq_ref[...] += dgq
        dgk_ref[...] += dgk
        dz_ref[:, 0:1536] = dzc_ref[...]
        dz_ref[:, 1536:2048] = dq.astype(BF16)
        dz_ref[:, 2048:2560] = dk.astype(BF16)
        dz_ref[:, 2560:3072] = dv_ref[...].astype(BF16)

    blk = pl.BlockSpec((tm, 512), lambda i: (i, 0))
    vec = pl.BlockSpec((1, 512), lambda i: (0, 0))
    return _call(
        body, [z, z, dzc, dqn, dkn, dv, gq, gk, ones_bd], name="qkv_bwd", grid=(t // tm,),
        out_shape=[_sds((t, D_IN), BF16), _sds((1, 512), F32), _sds((1, 512), F32)],
        in_specs=[pl.BlockSpec((tm, 512), lambda i: (i, 3)), pl.BlockSpec((tm, 512), lambda i: (i, 4)),
                  pl.BlockSpec((tm, 1536), lambda i: (i, 0))] + [blk] * 3
        + [vec, vec, pl.BlockSpec((512, 512), lambda i: (0, 0))],
        out_specs=[pl.BlockSpec((tm, D_IN), lambda i: (i, 0)), vec, vec],
        carry=carry)


def _columns_from_chips(g):
    return g.transpose(1, 0, 2).reshape(g.shape[1], N_CHIPS * g.shape[2])


def _columns_to_chips(w):
    k, n4 = w.shape
    return w.reshape(k, N_CHIPS, n4 // N_CHIPS).transpose(1, 0, 2)


def kernel(x, g_mix, w_in, conv_w, g_q, g_k, g_conv_out, g_attn_out, w_out, g_ffn, w_gate, w_up, w_down, loss_target, m_g_mix, m_w_in, m_conv_w, m_g_q, m_g_k, m_g_conv_out, m_g_attn_out, m_w_out, m_g_ffn, m_w_gate, m_w_up, m_w_down, v_g_mix, v_w_in, v_conv_w, v_g_q, v_g_k, v_g_conv_out, v_g_attn_out, v_w_out, v_g_ffn, v_w_gate, v_w_up, v_w_down):
    t = x.shape[1]
    xs = x[0]
    target = loss_target[0]
    tm = min(512, t)
    tmm = min(1024, t)

    cw_pad = jnp.pad(conv_w[0], ((0, 13), (0, 0)))
    gathered = _all_gather([w_in[0].astype(BF16), cw_pad])
    win = _columns_from_chips(gathered[0])
    cw = jnp.pad(gathered[1][:, 0:3, :].transpose(1, 0, 2).reshape(3, D_CONV), ((0, 5), (0, 0)))
    later = [w_out[0].astype(BF16), w_gate[0].astype(BF16), w_up[0].astype(BF16),
             w_down[0].astype(BF16)]

    head_id = jnp.arange(D_ATTN) // HEAD_DIM
    ones_bd = (head_id[:, None] == head_id[None, :]).astype(BF16)
    gq_t = jnp.tile(g_q, (1, D_ATTN // HEAD_DIM))
    gk_t = jnp.tile(g_k, (1, D_ATTN // HEAD_DIM))

    h1, z = _norm_matmul("in_proj", xs, g_mix, [win], tmm, 768, False)
    q, k = _qkv_prepare(z, gq_t, gk_t, ones_bd, tm)
    v_col = (3 * D_CONV + 2 * D_ATTN) // BAND
    (y_attn, lse), gathered = _attn_fwd(q, k, z, v_col, carry=_x_gather_chips(later))
    mix, gathered = _mixer_out(z, cw, y_attn, g_conv_out, g_attn_out, tm,
                               carry=_x_gather_sibling(gathered))
    wout = gathered[0].reshape(D_MODEL, D_MODEL)
    wgate = _columns_from_chips(gathered[1])
    wup = _columns_from_chips(gathered[2])
    wdown = gathered[3].reshape(D_FF, D_MODEL)
    (x1,) = _matmul("out_proj", mix, wout, [xs], [F32], lambda acc, r: (r + acc,), tmm, 512)
    h2, gate, up, act = _norm_matmul("ffn_up", x1, g_ffn, [wgate, wup], tmm, 1408, True, BF16)

    def loss_epilogue(acc, r, tgt):
        err = r + acc - tgt
        dy = err * (1.0 / D_MODEL)
        return dy, dy, jnp.sum(err * err)

    dx2, dx2b, loss_sum = _matmul("ffn_down_loss", act, wdown, [x1, target], [F32, BF16],
                                  loss_epilogue, tmm, 512, loss=True)

    def swiglu_bwd(da, gt, u):
        gt, u = gt.astype(F32), u.astype(F32)
        s = _sigmoid(gt)
        return da * u * (s * (1.0 + gt * (1.0 - s))), da * (gt * s)

    dgate, dup = _matmul("ffn_down_bwd", dx2b, wdown, [gate, up], [BF16, BF16], swiglu_bwd,
                         tmm, 1408, transposed_w=True)
    gw_down = _matmul_tn("grad_w_down", act, dx2b, 512, tmm)
    gw_gate = _matmul_tn("grad_w_gate", h2, dgate, 1408, tmm)
    gw_up = _matmul_tn("grad_w_up", h2, dup, 1408, tmm)

    me = 2 * lax.axis_index("x") + lax.axis_index("y")
    where = jnp.stack([lax.axis_index("c"), me]).astype(jnp.int32)

    def pair_sums(names, full, got):
        return [_pair_sum(f"pair_sum_{nme}", a, b, where) for nme, a, b in zip(names, full, got)]

    def chip_sums(names, pair, got):
        return [_chip_sum(f"chip_sum_{nme}", own, b) for nme, (_, own), b in zip(names, pair, got)]

    ffn = ["w_gate", "w_up", "w_down"]
    full = [_columns_to_chips(gw_gate), _columns_to_chips(gw_up),
            gw_down.reshape(N_CHIPS, D_FF // N_CHIPS, D_MODEL)]
    (dx1, dx1b, gg_ffn), got = _matmul_norm_bwd("ffn_up_bwd", [(dgate, wgate), (dup, wup)], x1, dx2,
                                                g_ffn, min(256, t), carry=_x_pair(full))
    pair = pair_sums(ffn, full, got)
    (dmix,) = _matmul("out_proj_bwd", dx1b, wout, [], [F32], lambda acc: (acc,), tmm, 512,
                      transposed_w=True)
    gw_out = _matmul_tn("grad_w_out", mix, dx1b, 512, tmm)
    full = [gw_out.reshape(N_CHIPS, D_MODEL // N_CHIPS, D_MODEL)]
    (dzc, do, dd, gcw, gg_conv, gg_attn), got = _mixer_bwd(
        z, dmix, y_attn, cw, g_conv_out, g_attn_out, ones_bd, tm, carry=_x_pair(full))
    pair += pair_sums(["w_out"], full, got)
    early = ffn + ["w_out"]
    (dqn, dkn, dv), got = _attn_bwd(q, k, z, v_col, do, lse, dd,
                                    carry=_x_chips([p for p, _ in pair]))
    mine = chip_sums(early, pair, got)
    (dz, gg_q, gg_k), theirs = _qkv_bwd(z, dzc, dqn, dkn, dv, gq_t, gk_t, ones_bd, tm,
                                        carry=_x_share(mine))
    gw_in = _matmul_tn("grad_w_in", h1, dz, 768, tmm)
    full = [_columns_to_chips(gw_in)]
    (grad_x, _, gg_mix), got = _matmul_norm_bwd("in_proj_bwd", [(dz, win)], xs, dx1, g_mix,
                                                min(256, t), carry=_x_pair(full))
    pair = pair_sums(["w_in"], full, got)
    got = _exchange_alone("grad_chip_exchange_w_in", _x_chips([pair[0][0]]))
    mine += chip_sums(["w_in"], pair, got)
    theirs = list(theirs) + list(_exchange_alone("grad_pair_share_w_in", _x_share(mine[-1:])))
    big = early + ["w_in"]

    small = _small_all_reduce({
        "g_mix": gg_mix, "g_ffn": gg_ffn, "g_conv_out": gg_conv, "g_attn_out": gg_attn,
        "g_q": gg_q, "g_k": gg_k, "loss": loss_sum, "conv_w": gcw})
    heads = D_ATTN // HEAD_DIM
    grads = {
        "g_mix": small[0:1, :], "g_ffn": small[1:2, :],
        "g_conv_out": small[2:3, 0:512], "g_attn_out": small[2:3, 512:1024],
        "g_q": small[3, 0:512].reshape(heads, HEAD_DIM).sum(axis=0)[None, :],
        "g_k": small[3, 512:1024].reshape(heads, HEAD_DIM).sum(axis=0)[None, :],
        "conv_w": lax.dynamic_slice(small[8:11, 0:512], (0, me * (D_CONV // N_CHIPS)),
                                    (3, D_CONV // N_CHIPS)),
    }
    halves = dict(zip(big, zip(mine, theirs)))
    loss = small[4, 0] * 0.5 * (1.0 / D_MODEL)

    weights = dict(g_mix=g_mix, w_in=w_in, conv_w=conv_w, g_q=g_q, g_k=g_k, g_conv_out=g_conv_out,
                   g_attn_out=g_attn_out, w_out=w_out, g_ffn=g_ffn, w_gate=w_gate, w_up=w_up,
                   w_down=w_down)
    moments_m = dict(g_mix=m_g_mix, w_in=m_w_in, conv_w=m_conv_w, g_q=m_g_q, g_k=m_g_k,
                     g_conv_out=m_g_conv_out, g_attn_out=m_g_attn_out, w_out=m_w_out, g_ffn=m_g_ffn,
                     w_gate=m_w_gate, w_up=m_w_up, w_down=m_w_down)
    moments_v = dict(g_mix=v_g_mix, w_in=v_w_in, conv_w=v_conv_w, g_q=v_g_q, g_k=v_g_k,
                     g_conv_out=v_g_conv_out, g_attn_out=v_g_attn_out, w_out=v_w_out, g_ffn=v_g_ffn,
                     w_gate=v_w_gate, w_up=v_w_up, w_down=v_w_down)
    names = list(weights)
    out_g, out_d, out_m, out_v = [], [], [], []
    for nme in names:
        wgt = weights[nme]
        shape2 = wgt.shape[-2:] if wgt.ndim == 3 else wgt.shape
        state = (wgt.reshape(shape2), moments_m[nme].reshape(shape2), moments_v[nme].reshape(shape2))
        if nme in halves:
            g2, dlt, nm, nv = _adamw_shard(f"adamw_{nme}", *state, *halves[nme], where)
        else:
            g2 = grads[nme].reshape(shape2)
            dlt, nm, nv = _adamw(f"adamw_{nme}", state[0], g2, state[1], state[2])
        out_g.append(g2.reshape(wgt.shape))
        out_d.append(dlt.reshape(wgt.shape))
        out_m.append(nm.reshape(wgt.shape))
        out_v.append(nv.reshape(wgt.shape))
    return (loss, grad_x[None], *out_g, *out_d, *out_m, *out_v)
```

```python
import functools
from typing import Any, Callable, NamedTuple, Sequence

import jax
import jax.numpy as jnp
from jax import lax
from jax.experimental import pallas as pl
from jax.experimental.pallas import tpu as pltpu

F32 = jnp.float32
BF16 = jnp.bfloat16
MESH = pl.DeviceIdType.MESH

D_MODEL = 1024
D_CONV = 512
D_ATTN = 512
HEAD_DIM = 64
D_FF = 2816
D_IN = 3 * D_CONV + 3 * D_ATTN
DILATIONS = (1, 4, 16)
BAND = 128
EPS = 1e-6
NEG = -1e30
N_CHIPS = 4

ADAM_LR = 0.001
ADAM_B1 = 0.9
ADAM_B2 = 0.999
ADAM_EPS = 1e-08
ADAM_WD = 0.01
ADAM_STEP = 10

V7X_VMEM_BYTES = 64 * 1024 * 1024
VMEM_LIMIT = V7X_VMEM_BYTES - 8 * 1024 * 1024
ANY = pl.BlockSpec(memory_space=pl.ANY)
VMEM_WHOLE = pl.BlockSpec(memory_space=pltpu.VMEM)


def _params(*sem):
    return pltpu.CompilerParams(dimension_semantics=sem, vmem_limit_bytes=VMEM_LIMIT)


def _sds(shape, dtype):
    return jax.ShapeDtypeStruct(shape, dtype)


def _place():
    x, y, c = lax.axis_index("x"), lax.axis_index("y"), lax.axis_index("c")
    chips = [(1 - x, y), (x, 1 - y), (1 - x, 1 - y)]
    return x, y, c, 2 * x + y, chips, [2 * cx + cy for cx, cy in chips]


def _all_gather(shards):
    n = len(shards)

    def body(*refs):
        ins, outs, stage = refs[:n], refs[n:2 * n], refs[2 * n:3 * n]
        ssem, rsem, fsem, gsem, lsem, osem = refs[3 * n:]
        x, y, c, me, chips, cids = _place()
        sib = (x, y, 1 - c)

        def half(w, which):
            h = shards[w].shape[0] // 2
            return pl.ds(pl.multiple_of(which * h, 8), h)

        loads = [pltpu.make_async_copy(ins[w], stage[w], lsem.at[w]) for w in range(n)]
        local = [pltpu.make_async_copy(stage[w], outs[w].at[me], osem.at[w]) for w in range(n)]
        for cp in loads:
            cp.start()

        def chip_copy(w, j, src_slot):
            rows = half(w, c)
            return pltpu.make_async_remote_copy(
                src_ref=ins[w].at[rows], dst_ref=outs[w].at[src_slot, rows],
                send_sem=ssem.at[3 * w + j], recv_sem=rsem.at[3 * w + j],
                device_id=(*chips[j], c), device_id_type=MESH)

        def sib_copy(w, j, which):
            rows = half(w, which)
            return pltpu.make_async_remote_copy(
                src_ref=outs[w].at[cids[j], rows], dst_ref=outs[w].at[cids[j], rows],
                send_sem=fsem.at[3 * w + j], recv_sem=gsem.at[3 * w + j],
                device_id=sib, device_id_type=MESH)

        sends = [chip_copy(w, j, me) for w in range(n) for j in range(3)]
        for cp in sends:
            cp.start()
        for w in range(n):
            loads[w].wait()
            local[w].start()
        passed = []
        for w in range(n):
            for j in range(3):
                chip_copy(w, j, cids[j]).wait_recv()
                cp = sib_copy(w, j, c)
                cp.start()
                passed.append(cp)
        for w in range(n):
            for j in range(3):
                sib_copy(w, j, 1 - c).wait_recv()
        for cp in sends + passed:
            cp.wait_send()
        for cp in local:
            cp.wait()

    return pl.pallas_call(
        body, name="all_gather_weights",
        out_shape=[_sds((N_CHIPS,) + s.shape, s.dtype) for s in shards],
        in_specs=[ANY] * n, out_specs=[ANY] * n,
        scratch_shapes=[pltpu.VMEM(s.shape, s.dtype) for s in shards]
        + [pltpu.SemaphoreType.DMA((3 * n,))] * 4 + [pltpu.SemaphoreType.DMA((n,))] * 2,
        compiler_params=pltpu.CompilerParams(vmem_limit_bytes=VMEM_LIMIT),
    )(*shards)


class _Exchange(NamedTuple):
    srcs: Sequence[Any]
    lands: Sequence[Any]
    outs: Sequence[Any]
    n_sems: int
    copies: Callable


def _remote(src, dst, ssem, rsem, k, to):
    return pltpu.make_async_remote_copy(src_ref=src, dst_ref=dst, send_sem=ssem.at[k],
                                        recv_sem=rsem.at[k], device_id=to, device_id_type=MESH)


def _x_gather_chips(shards):
    def copies(srcs, lands, outs, ssem, rsem):
        _, _, c, me, chips, cids = _place()
        go, arrive = [], []
        for w, s in enumerate(shards):
            h = s.shape[0] // 2
            rows = pl.ds(pl.multiple_of(c * h, 8), h)
            for j in range(3):
                to = (*chips[j], c)
                go.append(_remote(srcs[w].at[rows], lands[w].at[me, rows], ssem, rsem, 3 * w + j, to))
                arrive.append(_remote(srcs[w].at[rows], lands[w].at[cids[j], rows], ssem, rsem,
                                      3 * w + j, to))
        return go, arrive

    lands = [jnp.broadcast_to(s[None], (N_CHIPS,) + s.shape) for s in shards]
    return _Exchange(shards, lands, [], 3 * len(shards), copies)


def _x_gather_sibling(gathered):
    def copies(srcs, lands, outs, ssem, rsem):
        x, y, c, _, _, cids = _place()
        go, arrive = [], []
        for w, g in enumerate(gathered):
            h = g.shape[1] // 2
            mine = pl.ds(pl.multiple_of(c * h, 8), h)
            theirs = pl.ds(pl.multiple_of((1 - c) * h, 8), h)
            for j in range(3):
                slab = lands[w].at[cids[j]]
                go.append(_remote(slab.at[mine], slab.at[mine], ssem, rsem, 3 * w + j, (x, y, 1 - c)))
                arrive.append(_remote(slab.at[theirs], slab.at[theirs], ssem, rsem, 3 * w + j,
                                      (x, y, 1 - c)))
        return go, arrive

    return _Exchange([], gathered, [], 3 * len(gathered), copies)


def _x_pair(grads):
    def copies(srcs, lands, outs, ssem, rsem):
        x, y, c, _, _, _ = _place()
        go = []
        for w, g in enumerate(grads):
            h = g.shape[1] // 2
            theirs = pl.ds(pl.multiple_of((1 - c) * h, 8), h)
            go.append(_remote(srcs[w].at[:, theirs, :], outs[w], ssem, rsem, w, (x, y, 1 - c)))
        return go, go

    outs = [_sds((N_CHIPS, g.shape[1] // 2, g.shape[2]), g.dtype) for g in grads]
    return _Exchange(grads, [], outs, len(grads), copies)


def _x_chips(parts):
    def copies(srcs, lands, outs, ssem, rsem):
        _, _, c, _, chips, cids = _place()
        go = [_remote(srcs[w].at[cids[j]], outs[w].at[j], ssem, rsem, 3 * w + j, (*chips[j], c))
              for w in range(len(parts)) for j in range(3)]
        return go, go

    outs = [_sds((3,) + p.shape[1:], p.dtype) for p in parts]
    return _Exchange(parts, [], outs, 3 * len(parts), copies)


def _x_share(halves):
    def copies(srcs, lands, outs, ssem, rsem):
        x, y, c, _, _, _ = _place()
        go = [_remote(srcs[w], outs[w], ssem, rsem, w, (x, y, 1 - c)) for w in range(len(halves))]
        return go, go

    return _Exchange(halves, [], [_sds(h.shape, h.dtype) for h in halves], len(halves), copies)


def _call(body, args, *, name, grid, in_specs, out_specs, out_shape, scratch_shapes=(),
          semantics=None, carry=None):
    single = not isinstance(out_shape, (list, tuple))
    out_shape = [out_shape] if single else list(out_shape)
    out_specs = [out_specs] if single else list(out_specs)
    if carry is None:
        res = pl.pallas_call(
            body, name=name, grid=grid, in_specs=list(in_specs), out_specs=out_specs,
            out_shape=out_shape, scratch_shapes=list(scratch_shapes),
            compiler_params=_params(*(semantics or ("arbitrary",) * len(grid))))(*args)
        return res[0] if single else res
    n_in, n_out, n_scr = len(args), len(out_shape), len(scratch_shapes)
    n_src, n_land, n_new = len(carry.srcs), len(carry.lands), len(carry.outs)

    def carrying(*refs):
        at = 0
        parts = []
        for n in (n_in, n_src, n_land, n_out, n_land, n_new, n_scr, 2):
            parts.append(refs[at:at + n])
            at += n
        ins, srcs, _, outs, lands, news, scratch, (ssem, rsem) = parts
        ids = [pl.program_id(a) for a in range(len(grid))]
        first = functools.reduce(jnp.logical_and, [i == 0 for i in ids])
        last = functools.reduce(jnp.logical_and, [i == g - 1 for i, g in zip(ids, grid)])
        go, arrive = carry.copies(srcs, lands, news, ssem, rsem)

        @pl.when(first)
        def _():
            for cp in go:
                cp.start()

        body(*ins, *outs, *scratch)

        @pl.when(last)
        def _():
            for cp in go:
                cp.wait_send()
            for cp in arrive:
                cp.wait_recv()

    res = pl.pallas_call(
        carrying, name=name, grid=grid,
        in_specs=list(in_specs) + [ANY] * (n_src + n_land),
        out_specs=out_specs + [ANY] * (n_land + n_new),
        out_shape=out_shape + [_sds(a.shape, a.dtype) for a in carry.lands] + list(carry.outs),
        input_output_aliases={n_in + n_src + i: n_out + i for i in range(n_land)},
        scratch_shapes=list(scratch_shapes) + [pltpu.SemaphoreType.DMA((carry.n_sems,))] * 2,
        compiler_params=_params(*(("arbitrary",) * len(grid))))(*args, *carry.srcs, *carry.lands)
    own = res[:n_out]
    return (own[0] if single else own), res[n_out:]


def _exchange_alone(name, exchange):
    def body(x_ref, o_ref):
        o_ref[...] = x_ref[...]

    blk = pl.BlockSpec((8, 128), lambda i: (0, 0))
    _, res = _call(body, [jnp.zeros((8, 128), F32)], name=name, grid=(1,), in_specs=[blk],
                   out_specs=blk, out_shape=_sds((8, 128), F32), carry=exchange)
    return res


def _row_block(r, want):
    return max(d for d in range(1, min(want, r) + 1) if r % d == 0 and (d % 8 == 0 or d == r))


def _pair_sum(name, full, got, where):
    _, r, n = full.shape
    h = r // 2
    tr = _row_block(h, 256)
    nb = h // tr

    def body(w_ref, a_ref, b_ref, o_ref, own_ref):
        total = a_ref[...] + b_ref[...]
        o_ref[...] = total.astype(BF16)

        @pl.when(pl.program_id(1) == w_ref[1])
        def _():
            own_ref[...] = total[0]

    blk = pl.BlockSpec((1, tr, n), lambda i, s, w: (s, i, 0))
    return pl.pallas_call(
        body, name=name, out_shape=[_sds(got.shape, BF16), _sds((h, n), F32)],
        grid_spec=pltpu.PrefetchScalarGridSpec(
            num_scalar_prefetch=1, grid=(nb, N_CHIPS),
            in_specs=[pl.BlockSpec((1, tr, n), lambda i, s, w: (s, w[0] * nb + i, 0)), blk],
            out_specs=[blk, pl.BlockSpec((tr, n), lambda i, s, w: (i, 0))]),
        compiler_params=_params("parallel", "arbitrary"),
    )(where, full, got)


def _chip_sum(name, own, got):
    h, n = own.shape
    tr = _row_block(h, 256)

    def body(a_ref, b0, b1, b2, o_ref):
        o_ref[...] = ((a_ref[...] + b0[0].astype(F32)) + b1[0].astype(F32)) + b2[0].astype(F32)

    def slot(j):
        return pl.BlockSpec((1, tr, n), lambda i: (j, i, 0))

    blk = pl.BlockSpec((tr, n), lambda i: (i, 0))
    return pl.pallas_call(
        body, name=name, grid=(h // tr,), out_shape=_sds((h, n), F32),
        in_specs=[blk, slot(0), slot(1), slot(2)], out_specs=blk,
        compiler_params=_params("parallel"),
    )(own, got, got, got)


SMALL_ROWS = 16
SMALL_LAYOUT = (
    ("g_mix", 0, 0, 1, 1024), ("g_ffn", 1, 0, 1, 1024), ("g_conv_out", 2, 0, 1, 512),
    ("g_attn_out", 2, 512, 1, 512), ("g_q", 3, 0, 1, 512), ("g_k", 3, 512, 1, 512),
    ("loss", 4, 0, 1, 128), ("conv_w", 8, 0, 8, 512))


def _small_all_reduce(parts):
    names = [s[0] for s in SMALL_LAYOUT]

    def body(*refs):
        ins = refs[:len(names)]
        out_ref, stage, buf, ssem, rsem = refs[len(names):]
        x, y, c, _, _, _ = _place()
        me = 4 * x + 2 * y + c
        stage[...] = jnp.zeros_like(stage)
        for ref, (_, r0, c0, nr, nc) in zip(ins, SMALL_LAYOUT):
            stage[r0:r0 + nr, c0:c0 + nc] = ref[0:nr, :]
        buf[me] = stage[...]
        peers = []
        for d in range(1, 8):
            px = 1 - x if d & 4 else x
            py = 1 - y if d & 2 else y
            pc = 1 - c if d & 1 else c
            peers.append(((px, py, pc), 4 * px + 2 * py + pc))
        sends = [pltpu.make_async_remote_copy(
            src_ref=stage, dst_ref=buf.at[me], send_sem=ssem.at[k], recv_sem=rsem.at[k],
            device_id=peer, device_id_type=MESH) for k, (peer, _) in enumerate(peers)]
        for cp in sends:
            cp.start()
        for k, (peer, pid) in enumerate(peers):
            pltpu.make_async_remote_copy(
                src_ref=stage, dst_ref=buf.at[pid], send_sem=ssem.at[k], recv_sem=rsem.at[k],
                device_id=peer, device_id_type=MESH).wait_recv()
        for cp in sends:
            cp.wait_send()
        acc = buf[0]
        for k in range(1, 8):
            acc = acc + buf[k]
        out_ref[...] = acc

    return pl.pallas_call(
        body, name="small_all_reduce", out_shape=_sds((SMALL_ROWS, 1024), F32),
        in_specs=[VMEM_WHOLE] * len(names), out_specs=VMEM_WHOLE,
        scratch_shapes=[pltpu.VMEM((SMALL_ROWS, 1024), F32), pltpu.VMEM((8, SMALL_ROWS, 1024), F32),
                        pltpu.SemaphoreType.DMA((7,)), pltpu.SemaphoreType.DMA((7,))],
    )(*[parts[k] for k in names])


def _dot(a, b):
    return jnp.dot(a, b, preferred_element_type=F32)


def _dot_nt(a, b):
    return lax.dot_general(a, b, (((1,), (1,)), ((), ())), preferred_element_type=F32)


def _dot_tn(a, b):
    return lax.dot_general(a, b, (((0,), (0,)), ((), ())), preferred_element_type=F32)


def _sigmoid(v):
    return 1.0 / (1.0 + jnp.exp(-v))


def _rms_scale(v):
    return lax.rsqrt(jnp.mean(v * v, axis=-1, keepdims=True) + EPS)


def _rms_bwd(v, r, g, dy):
    vh = v * r
    dh = dy * g
    return r * (dh - vh * jnp.mean(dh * vh, axis=-1, keepdims=True)), vh


def _head_sum(a, ones_bd):
    hi = a.astype(BF16)
    lo = (a - hi.astype(F32)).astype(BF16)
    return _dot(hi, ones_bd) + _dot(lo, ones_bd)


def _head_rms_scale(v, ones_bd):
    return lax.rsqrt(_head_sum(v * v, ones_bd) * (1.0 / HEAD_DIM) + EPS)


def _norm_matmul(name, x, g, ws, tm, tn, swiglu, out_dtype=F32):
    t, d = x.shape
    n = ws[0].shape[1]
    nw = len(ws)

    def body(x_ref, g_ref, *refs):
        w_refs, h_ref, o_refs = refs[:nw], refs[nw], refs[nw + 1:2 * nw + 1]
        hs = refs[-1]

        @pl.when(pl.program_id(1) == 0)
        def _():
            xv = x_ref[...]
            h = (xv * _rms_scale(xv) * g_ref[...]).astype(BF16)
            hs[...] = h
            h_ref[...] = h

        h = hs[...]
        outs = [_dot(h, w[...]) for w in w_refs]
        for o_ref, o in zip(o_refs, outs):
            o_ref[...] = o.astype(out_dtype)
        if swiglu:
            refs[2 * nw + 1][...] = (outs[0] * _sigmoid(outs[0]) * outs[1]).astype(BF16)

    row = pl.BlockSpec((tm, d), lambda i, j: (i, 0))
    col = pl.BlockSpec((tm, tn), lambda i, j: (i, j))
    out_shape = [_sds((t, d), BF16)] + [_sds((t, n), out_dtype)] * nw
    out_specs = [row] + [col] * nw
    if swiglu:
        out_shape.append(_sds((t, n), BF16))
        out_specs.append(col)
    return pl.pallas_call(
        body, name=name, grid=(t // tm, n // tn), out_shape=out_shape,
        in_specs=[row, pl.BlockSpec((1, d), lambda i, j: (0, 0))]
        + [pl.BlockSpec((d, tn), lambda i, j: (0, j))] * nw,
        out_specs=out_specs, scratch_shapes=[pltpu.VMEM((tm, d), BF16)],
        compiler_params=_params("parallel", "arbitrary"),
    )(x, g, *ws)


def _matmul(name, a, w, extras, out_dtypes, epilogue, tm, tn, transposed_w=False, loss=False):
    t, k = a.shape
    n = w.shape[0] if transposed_w else w.shape[1]
    ne, no = len(extras), len(out_dtypes)

    def body(a_ref, w_ref, *refs):
        e_refs, o_refs = refs[:ne], refs[ne:]
        acc = _dot_nt(a_ref[...], w_ref[...]) if transposed_w else _dot(a_ref[...], w_ref[...])
        res = epilogue(acc, *[e[...] for e in e_refs])
        for o_ref, r in zip(o_refs[:no], res[:no]):
            o_ref[...] = r.astype(o_ref.dtype)
        if loss:
            first = jnp.logical_and(pl.program_id(0) == 0, pl.program_id(1) == 0)

            @pl.when(first)
            def _():
                o_refs[no][...] = jnp.zeros_like(o_refs[no])

            o_refs[no][...] += res[no]

    col = pl.BlockSpec((tm, tn), lambda i, j: (i, j))
    w_spec = (pl.BlockSpec((tn, k), lambda i, j: (j, 0)) if transposed_w
              else pl.BlockSpec((k, tn), lambda i, j: (0, j)))
    out_shape = [_sds((t, n), dt) for dt in out_dtypes]
    out_specs = [col] * no
    if loss:
        out_shape.append(_sds((8, 128), F32))
        out_specs.append(pl.BlockSpec((8, 128), lambda i, j: (0, 0)))
    return pl.pallas_call(
        body, name=name, grid=(t // tm, n // tn), out_shape=out_shape,
        in_specs=[pl.BlockSpec((tm, k), lambda i, j: (i, 0)), w_spec] + [col] * ne,
        out_specs=out_specs,
        compiler_params=_params(*(("arbitrary", "arbitrary") if loss else ("parallel", "parallel"))),
    )(a, w, *extras)


def _matmul_norm_bwd(name, pairs, x, dres, g, tm, carry=None):
    t, d = x.shape
    npairs = len(pairs)

    def body(*refs):
        a_refs, w_refs = refs[:npairs], refs[npairs:2 * npairs]
        x_ref, r_ref, g_ref, dx_ref, dxb_ref, dg_ref = refs[2 * npairs:]
        dy = _dot_nt(a_refs[0][...], w_refs[0][...])
        for a_ref, w_ref in zip(a_refs[1:], w_refs[1:]):
            dy = dy + _dot_nt(a_ref[...], w_ref[...])
        xv = x_ref[...]
        dx, xh = _rms_bwd(xv, _rms_scale(xv), g_ref[...], dy)
        dx = dx + r_ref[...]
        dx_ref[...] = dx
        dxb_ref[...] = dx.astype(BF16)

        @pl.when(pl.program_id(0) == 0)
        def _():
            dg_ref[...] = jnp.zeros_like(dg_ref)

        dg_ref[...] += jnp.sum(dy * xh, axis=0, keepdims=True)

    row = pl.BlockSpec((tm, d), lambda i: (i, 0))
    vec = pl.BlockSpec((1, d), lambda i: (0, 0))
    return _call(
        body, [a for a, _ in pairs] + [w for _, w in pairs] + [x, dres, g], name=name,
        grid=(t // tm,), out_shape=[_sds((t, d), F32), _sds((t, d), BF16), _sds((1, d), F32)],
        in_specs=[pl.BlockSpec((tm, a.shape[1]), lambda i: (i, 0)) for a, _ in pairs]
        + [pl.BlockSpec(w.shape, lambda i: (0, 0), pipeline_mode=pl.Buffered(1)) for _, w in pairs]
        + [row, row, vec],
        out_specs=[row, row, vec], carry=carry)


def _matmul_tn(name, a, g, tn, tk):
    t, ka = a.shape
    n = g.shape[1]

    def body(a_ref, g_ref, o_ref):
        @pl.when(pl.program_id(1) == 0)
        def _():
            o_ref[...] = jnp.zeros_like(o_ref)

        o_ref[...] += _dot_tn(a_ref[...], g_ref[...])

    return pl.pallas_call(
        body, name=name, grid=(n // tn, t // tk), out_shape=_sds((ka, n), F32),
        in_specs=[pl.BlockSpec((tk, ka), lambda j, s: (s, 0)),
                  pl.BlockSpec((tk, tn), lambda j, s: (s, j))],
        out_specs=pl.BlockSpec((ka, tn), lambda j, s: (0, j)),
        compiler_params=_params("parallel", "arbitrary"),
    )(a, g)


def _elementwise(name, fn, ins, out_dtypes, tr):
    r, n = ins[0].shape
    tr = _row_block(r, tr)
    ni = len(ins)

    def body(*refs):
        res = fn(*[ref[...] for ref in refs[:ni]])
        for o_ref, v in zip(refs[ni:], res):
            o_ref[...] = v.astype(o_ref.dtype)

    blk = pl.BlockSpec((tr, n), lambda i: (i, 0))
    return pl.pallas_call(
        body, name=name, grid=(r // tr,), out_shape=[_sds((r, n), dt) for dt in out_dtypes],
        in_specs=[blk] * ni, out_specs=[blk] * len(out_dtypes),
        compiler_params=_params("parallel"),
    )(*ins)


def _adamw_update(w, g, m, v):
    m = ADAM_B1 * m + (1.0 - ADAM_B1) * g
    v = ADAM_B2 * v + (1.0 - ADAM_B2) * (g * g)
    m_hat = m / (1.0 - ADAM_B1 ** ADAM_STEP)
    v_hat = v / (1.0 - ADAM_B2 ** ADAM_STEP)
    return -ADAM_LR * (m_hat / (jnp.sqrt(v_hat) + ADAM_EPS) + ADAM_WD * w), m, v


def _adamw(name, w, g, m, v):
    return _elementwise(name, _adamw_update, [w, g, m, v], [F32] * 3, 256)


def _adamw_shard(name, w, m, v, mine, theirs, where):
    r, n = w.shape
    h = r // 2
    tr = _row_block(h, 256)
    nb = h // tr

    def body(w_ref, p_ref, m_ref, v_ref, a_ref, b_ref, g_ref, d_ref, nm_ref, nv_ref):
        g = jnp.where(pl.program_id(0) == w_ref[0], a_ref[...], b_ref[...])
        g_ref[...] = g
        d_ref[...], nm_ref[...], nv_ref[...] = _adamw_update(p_ref[...], g, m_ref[...], v_ref[...])

    whole = pl.BlockSpec((tr, n), lambda s, i, c: (s * nb + i, 0))
    half = pl.BlockSpec((tr, n), lambda s, i, c: (i, 0))
    return pl.pallas_call(
        body, name=name, out_shape=[_sds((r, n), F32)] * 4,
        grid_spec=pltpu.PrefetchScalarGridSpec(
            num_scalar_prefetch=1, grid=(2, nb), in_specs=[whole] * 3 + [half] * 2,
            out_specs=[whole] * 4),
        compiler_params=_params("parallel", "parallel"),
    )(where, w, m, v, mine, theirs)


def _qkv_prepare(z, gq, gk, ones_bd, tm):
    t = z.shape[0]

    def body(zq_ref, zk_ref, gq_ref, gk_ref, bd_ref, q_ref, k_ref):
        bd = bd_ref[...]
        q = zq_ref[...]
        k = zk_ref[...]
        q_ref[...] = (q * _head_rms_scale(q, bd) * gq_ref[...]) * HEAD_DIM ** -0.5
        k_ref[...] = k * _head_rms_scale(k, bd) * gk_ref[...]

    vec = pl.BlockSpec((1, 512), lambda i: (0, 0))
    out = pl.BlockSpec((tm, 512), lambda i: (i, 0))
    return pl.pallas_call(
        body, name="qkv_prepare", grid=(t // tm,), out_shape=[_sds((t, 512), F32)] * 2,
        in_specs=[pl.BlockSpec((tm, 512), lambda i: (i, 3)), pl.BlockSpec((tm, 512), lambda i: (i, 4)),
                  vec, vec, pl.BlockSpec((512, 512), lambda i: (0, 0))],
        out_specs=[out] * 2, compiler_params=_params("parallel"),
    )(z, z, gq, gk, ones_bd)


TOK = 2048
UNITS = TOK // BAND


def _stack_masks():
    row = lax.broadcasted_iota(jnp.int32, (2 * BAND, 2 * BAND), 0) & (BAND - 1)
    col = lax.broadcasted_iota(jnp.int32, (2 * BAND, 2 * BAND), 1)
    lane = lax.broadcasted_iota(jnp.int32, (BAND, BAND), 1)
    head0 = lane < HEAD_DIM
    ones = [jnp.where(head0, 1.0, 0.0).astype(BF16), jnp.where(head0, 0.0, 1.0).astype(BF16)]
    return col - row, col, head0, ones


def _split3(x):
    hi = x.astype(BF16).astype(F32)
    mid = (x - hi).astype(BF16).astype(F32)
    return hi, mid, x - hi - mid


def _gather(srcs, dst, d):
    per = TOK // d
    at = 0
    for r in range(d):
        for src in srcs:
            rows = src[pl.ds(r, per, stride=d), :] if d > 1 else src[...]
            dst[pl.ds(at, per), :] = rows.astype(dst.dtype)
            at += per


def _scatter_add(out_ref, src, d, per_src, offset, first):
    per = TOK // d
    if d == 1:
        val = src[pl.ds(offset, per), :]
        out_ref[...] = val if first else out_ref[...] + val
        return
    for r in range(d):
        val = src[pl.ds(r * per_src + offset, per), :]
        idx = pl.ds(r, per, stride=d)
        out_ref[idx, :] = val if first else out_ref[idx, :] + val


def _attn_fwd(q, k, v, v_col, carry=None):
    t = q.shape[0]
    nblk = t // TOK

    def body(q_ref, kp_ref, k_ref, vp_ref, v_ref, y_ref, l_ref, qs, ks, vs, ob, lb, on, ln):
        i = pl.program_id(1)
        diff, col, head0, hm = _stack_masks()
        band_ok = jnp.logical_and(diff >= 0, diff <= BAND)
        for g, d in enumerate(DILATIONS):
            per = TOK // d
            nb = per // BAND
            _gather([q_ref], qs, d)
            _gather([kp_ref, k_ref], ks, d)
            _gather([vp_ref, v_ref], vs, d)

            def unit(u, carry):
                r, b = u // nb, u % nb
                qu = qs[pl.ds(pl.multiple_of(u * BAND, BAND), BAND), :]
                start = pl.multiple_of(r * 2 * per + per + (b - 1) * BAND, BAND)
                kw = ks[pl.ds(start, 2 * BAND), :]
                vw = vs[pl.ds(start, 2 * BAND), :]
                lo = jnp.where(jnp.logical_and(i == 0, b == 0), BAND, 0)
                s = _dot_nt(jnp.concatenate([qu * hm[0], qu * hm[1]], axis=0), kw)
                s = jnp.where(jnp.logical_and(band_ok, col >= lo), s, NEG)
                mx = jnp.max(s, axis=-1, keepdims=True)
                e = jnp.exp(s - mx)
                den = jnp.sum(e, axis=-1, keepdims=True)
                o2 = _dot(e.astype(BF16), vw) / den
                l2 = jnp.broadcast_to(mx + jnp.log(den), (2 * BAND, BAND))
                rows = pl.ds(pl.multiple_of(u * BAND, BAND), BAND)
                ob[rows, :] = jnp.where(head0, o2[:BAND], o2[BAND:])
                lb[rows, :] = jnp.where(head0, l2[:BAND], l2[BAND:])
                return carry

            lax.fori_loop(0, UNITS, unit, 0, unroll=8)
            _scatter_add(on.at[g], ob, d, per, 0, True)
            _scatter_add(ln.at[g], lb, d, per, 0, True)
        ls = [ln[0], ln[1], ln[2]]
        mx = jnp.maximum(jnp.maximum(ls[0], ls[1]), ls[2])
        es = [jnp.exp(l - mx) for l in ls]
        tot = es[0] + es[1] + es[2]
        y_ref[...] = (es[0] * on[0] + es[1] * on[1] + es[2] * on[2]) / tot
        l_ref[...] = mx + jnp.log(tot)

    main = pl.BlockSpec((TOK, BAND), lambda j, i: (i, j))
    prev = pl.BlockSpec((TOK, BAND), lambda j, i: (jnp.maximum(i - 1, 0), j))
    vmain = pl.BlockSpec((TOK, BAND), lambda j, i: (i, j + v_col))
    vprev = pl.BlockSpec((TOK, BAND), lambda j, i: (jnp.maximum(i - 1, 0), j + v_col))
    return _call(
        body, [q, k, k, v, v], name="attn_fwd", grid=(D_ATTN // BAND, nblk),
        out_shape=[_sds((t, D_ATTN), F32)] * 2,
        in_specs=[main, prev, main, vprev, vmain], out_specs=[main, main],
        scratch_shapes=[pltpu.VMEM((TOK, BAND), BF16), pltpu.VMEM((2 * TOK, BAND), BF16),
                        pltpu.VMEM((2 * TOK, BAND), BF16), pltpu.VMEM((TOK, BAND), F32),
                        pltpu.VMEM((TOK, BAND), F32), pltpu.VMEM((3, TOK, BAND), F32),
                        pltpu.VMEM((3, TOK, BAND), F32)],
        semantics=("parallel", "parallel"), carry=carry)


def _attn_bwd(q, k, v, v_col, do, lse, dd, carry=None):
    t = q.shape[0]
    nblk = t // TOK
    offs = [sum(DILATIONS[:g]) * BAND for g in range(len(DILATIONS))]

    def body(q_ref, kp_ref, k_ref, vp_ref, v_ref, do_ref, l_ref, d_ref, dq_ref, dk_ref, dv_ref,
             qs, dos, ks, vs, lsc, dsc, dqb, dkb, dvb, ckb, cvb):
        step = pl.program_id(1)
        i = nblk - 1 - step
        key = lax.broadcasted_iota(jnp.int32, (2 * BAND, 2 * BAND), 0)
        qry = lax.broadcasted_iota(jnp.int32, (2 * BAND, 2 * BAND), 1) & (BAND - 1)
        off = key - qry
        band_ok = jnp.logical_and(off >= 0, off <= BAND)
        lane = lax.broadcasted_iota(jnp.int32, (BAND, BAND), 1)
        head0 = lane < HEAD_DIM
        hm = [jnp.where(head0, 1.0, 0.0).astype(BF16), jnp.where(head0, 0.0, 1.0).astype(BF16)]
        piece = lane & (HEAD_DIM - 1)
        lane2 = lax.broadcasted_iota(jnp.int32, (2 * BAND, BAND), 1) & (HEAD_DIM - 1)
        ones = jnp.where(lane2 < 3, 1.0, 0.0).astype(BF16)

        def pieces(x):
            hi, mid, lo = _split3(-x)
            a = jnp.where(piece == 0, hi, jnp.where(piece == 1, mid, jnp.where(piece == 2, lo, 0.0)))
            return a.astype(BF16)

        for g, d in enumerate(DILATIONS):
            per = TOK // d
            nb = per // BAND
            pad = per + BAND
            _gather([q_ref], qs, d)
            _gather([do_ref], dos, d)
            _gather([l_ref], lsc, d)
            _gather([d_ref], dsc, d)
            _gather([kp_ref, k_ref], ks, d)
            _gather([vp_ref, v_ref], vs, d)
            dkb[...] = jnp.zeros_like(dkb)
            dvb[...] = jnp.zeros_like(dvb)

            def unit(u, c_):
                r, b = u // nb, u % nb
                rows = pl.ds(pl.multiple_of(u * BAND, BAND), BAND)
                qu, dou = qs[rows, :], dos[rows, :]
                la, da = pieces(lsc[rows, :]), pieces(dsc[rows, :])
                q2 = jnp.concatenate([qu * hm[0], qu * hm[1]], axis=0)
                do2 = jnp.concatenate([dou * hm[0], dou * hm[1]], axis=0)
                l2 = jnp.concatenate([la * hm[0], la * hm[1]], axis=0)
                d2 = jnp.concatenate([da * hm[0], da * hm[1]], axis=0)
                start = pl.multiple_of(r * 2 * per + per + (b - 1) * BAND, BAND)
                kw = ks[pl.ds(start, 2 * BAND), :]
                vw = vs[pl.ds(start, 2 * BAND), :]
                lo = jnp.where(jnp.logical_and(i == 0, b == 0), BAND, 0)
                ok = jnp.logical_and(band_ok, key >= lo)
                st = _dot_nt(jnp.concatenate([kw, ones], axis=1), jnp.concatenate([q2, l2], axis=1))
                dpt = _dot_nt(jnp.concatenate([vw, ones], axis=1), jnp.concatenate([do2, d2], axis=1))
                pt = jnp.where(ok, jnp.exp(st), 0.0)
                dst = (pt * dpt).astype(BF16)
                acc = pl.ds(pl.multiple_of(r * pad + b * BAND, BAND), 2 * BAND)
                dkb[acc, :] += _dot(dst, q2)
                dvb[acc, :] += _dot(pt.astype(BF16), do2)
                dq2 = _dot_tn(dst, kw)
                dqb[rows, :] = jnp.where(head0, dq2[:BAND], dq2[BAND:])
                return c_

            lax.fori_loop(0, UNITS, unit, 0, unroll=8)

            for r in range(d):
                last = pl.ds(r * pad + per, BAND)
                kept = pl.ds(offs[g] + r * BAND, BAND)

                @pl.when(step > 0)
                def _():
                    dkb[last, :] += ckb[kept, :]
                    dvb[last, :] += cvb[kept, :]

                ckb[kept, :] = dkb[pl.ds(r * pad, BAND), :]
                cvb[kept, :] = dvb[pl.ds(r * pad, BAND), :]
            _scatter_add(dq_ref, dqb, d, per, 0, g == 0)
            _scatter_add(dk_ref, dkb, d, pad, BAND, g == 0)
            _scatter_add(dv_ref, dvb, d, pad, BAND, g == 0)

    main = pl.BlockSpec((TOK, BAND), lambda j, s: (nblk - 1 - s, j))
    prev = pl.BlockSpec((TOK, BAND), lambda j, s: (jnp.maximum(nblk - 2 - s, 0), j))
    vmain = pl.BlockSpec((TOK, BAND), lambda j, s: (nblk - 1 - s, j + v_col))
    vprev = pl.BlockSpec((TOK, BAND), lambda j, s: (jnp.maximum(nblk - 2 - s, 0), j + v_col))
    acc_rows = max(d * (TOK // d + BAND) for d in DILATIONS)
    kept_rows = sum(DILATIONS) * BAND
    return _call(
        body, [q, k, k, v, v, do, lse, dd], name="attn_bwd",
        grid=(D_ATTN // BAND, nblk), out_shape=[_sds((t, D_ATTN), F32)] * 3,
        in_specs=[main, prev, main, vprev, vmain, main, main, main], out_specs=[main] * 3,
        scratch_shapes=[pltpu.VMEM((TOK, BAND), BF16)] * 2 + [pltpu.VMEM((2 * TOK, BAND), BF16)] * 2
        + [pltpu.VMEM((TOK, BAND), F32)] * 3 + [pltpu.VMEM((acc_rows, BAND), F32)] * 2
        + [pltpu.VMEM((kept_rows, BAND), F32)] * 2,
        semantics=("parallel", "arbitrary"), carry=carry)


def _halo_rows(tm, t):
    per = tm // 8
    prev = lambda i: (jnp.maximum(i * per - 1, 0), 0)
    nxt = lambda i: (jnp.minimum((i + 1) * per, t // 8 - 1), 0)
    return prev, nxt


def _mixer_out(z, cw, y_attn, g_conv, g_attn, tm, carry=None):
    t = z.shape[0]
    prev, _ = _halo_rows(tm, t)

    def body(z_ref, zp_ref, cw_ref, y_ref, gc_ref, ga_ref, mix_ref):
        i = pl.program_id(0)
        keep = jnp.where(i > 0, 1.0, 0.0)
        cu = jnp.concatenate([zp_ref[:, 0:512] * zp_ref[:, 1024:1536] * keep,
                              z_ref[:, 0:512] * z_ref[:, 1024:1536]], axis=0)
        c = (cw_ref[0:1, :] * pltpu.roll(cu, 2, 0) + cw_ref[1:2, :] * pltpu.roll(cu, 1, 0)
             + cw_ref[2:3, :] * cu)[8:, :]
        yc = z_ref[:, 512:1024] * c
        mix_ref[:, 0:512] = (yc * _rms_scale(yc) * gc_ref[...]).astype(BF16)
        ya = y_ref[...]
        mix_ref[:, 512:1024] = (ya * _rms_scale(ya) * ga_ref[...]).astype(BF16)

    blk = pl.BlockSpec((tm, 512), lambda i: (i, 0))
    vec = pl.BlockSpec((1, 512), lambda i: (0, 0))
    return _call(
        body, [z, z, cw, y_attn, g_conv, g_attn], name="mixer_out", grid=(t // tm,),
        out_shape=_sds((t, 1024), BF16),
        in_specs=[pl.BlockSpec((tm, 1536), lambda i: (i, 0)), pl.BlockSpec((8, 1536), prev),
                  pl.BlockSpec((8, 512), lambda i: (0, 0)), blk, vec, vec],
        out_specs=pl.BlockSpec((tm, 1024), lambda i: (i, 0)),
        semantics=("parallel",), carry=carry)


def _mixer_bwd(z, dmix, y_attn, cw, g_conv, g_attn, ones_bd, tm, carry=None):
    t = z.shape[0]
    nblk = t // tm
    prev, nxt = _halo_rows(tm, t)
    e = tm + 16

    def body(z_ref, zp_ref, zn_ref, dm_ref, dmn_ref, y_ref, cw_ref, gc_ref, ga_ref, bd_ref,
             dz_ref, do_ref, dd_ref, dcw_ref, dgc_ref, dga_ref):
        i = pl.program_id(0)
        rows = lax.broadcasted_iota(jnp.int32, (e, 1), 0)
        lo = jnp.where(i > 0, 0, 8)
        hi = jnp.where(i < nblk - 1, e, tm + 8)
        ze = jnp.concatenate([zp_ref[...], z_ref[...], zn_ref[...]], axis=0)
        u, gb, gcv = ze[:, 0:512], ze[:, 512:1024], ze[:, 1024:1536]
        w0, w1, w2 = cw_ref[0:1, :], cw_ref[1:2, :], cw_ref[2:3, :]
        cu = jnp.where(rows >= lo, gcv * u, 0.0)
        cu1, cu2 = pltpu.roll(cu, 1, 0), pltpu.roll(cu, 2, 0)
        c = w0 * cu2 + w1 * cu1 + w2 * cu
        yc = gb * c
        dma = jnp.concatenate([jnp.zeros((8, 512), F32), dm_ref[:, 0:512], dmn_ref[...]], axis=0)
        dyc, ych = _rms_bwd(yc, _rms_scale(yc), gc_ref[...], dma)
        dc = jnp.where(jnp.logical_and(rows >= 8, rows < hi), dyc * gb, 0.0)
        dcu = w0 * pltpu.roll(dc, e - 2, 0) + w1 * pltpu.roll(dc, e - 1, 0) + w2 * dc
        mid = slice(8, 8 + tm)
        dz_ref[:, 0:512] = (dcu * gcv)[mid, :].astype(BF16)
        dz_ref[:, 512:1024] = (dyc * c)[mid, :].astype(BF16)
        dz_ref[:, 1024:1536] = (dcu * u)[mid, :].astype(BF16)

        ya = y_ref[...]
        dmb = dm_ref[:, 512:1024]
        dya, yah = _rms_bwd(ya, _rms_scale(ya), ga_ref[...], dmb)
        do_ref[...] = dya
        dd_ref[...] = _head_sum(dya * ya, bd_ref[...])

        @pl.when(i == 0)
        def _():
            dcw_ref[...] = jnp.zeros_like(dcw_ref)
            dgc_ref[...] = jnp.zeros_like(dgc_ref)
            dga_ref[...] = jnp.zeros_like(dga_ref)

        dcm = jnp.where(rows < tm + 8, dc, 0.0)
        dcw_ref[0:1, :] += jnp.sum(dcm * cu2, axis=0, keepdims=True)
        dcw_ref[1:2, :] += jnp.sum(dcm * cu1, axis=0, keepdims=True)
        dcw_ref[2:3, :] += jnp.sum(dcm * cu, axis=0, keepdims=True)
        dgc_ref[...] += jnp.sum((dma * ych)[mid, :], axis=0, keepdims=True)
        dga_ref[...] += jnp.sum(dmb * yah, axis=0, keepdims=True)

    blk = pl.BlockSpec((tm, 512), lambda i: (i, 0))
    vec = pl.BlockSpec((1, 512), lambda i: (0, 0))
    cwb = pl.BlockSpec((8, 512), lambda i: (0, 0))
    return _call(
        body, [z, z, z, dmix, dmix, y_attn, cw, g_conv, g_attn, ones_bd], name="mixer_bwd",
        grid=(nblk,),
        out_shape=[_sds((t, 1536), BF16), _sds((t, 512), F32), _sds((t, 512), F32),
                   _sds((8, 512), F32), _sds((1, 512), F32), _sds((1, 512), F32)],
        in_specs=[pl.BlockSpec((tm, 1536), lambda i: (i, 0)), pl.BlockSpec((8, 1536), prev),
                  pl.BlockSpec((8, 1536), nxt), pl.BlockSpec((tm, 1024), lambda i: (i, 0)),
                  pl.BlockSpec((8, 512), nxt), blk, cwb, vec, vec,
                  pl.BlockSpec((512, 512), lambda i: (0, 0))],
        out_specs=[pl.BlockSpec((tm, 1536), lambda i: (i, 0)), blk, blk, cwb, vec, vec],
        carry=carry)


def _qkv_bwd(z, dzc, dqn, dkn, dv, gq, gk, ones_bd, tm, carry=None):
    t = z.shape[0]

    def body(zq_ref, zk_ref, dzc_ref, dqn_ref, dkn_ref, dv_ref, gq_ref, gk_ref, bd_ref,
             dz_ref, dgq_ref, dgk_ref):
        bd = bd_ref[...]

        @pl.when(pl.program_id(0) == 0)
        def _():
            dgq_ref[...] = jnp.zeros_like(dgq_ref)
            dgk_ref[...] = jnp.zeros_like(dgk_ref)

        def back(v, dn, g, scale):
            r = _head_rms_scale(v, bd)
            vh = v * r
            dh = dn * (g * scale)
            dv = r * (dh - vh * (_head_sum(dh * vh, bd) * (1.0 / HEAD_DIM)))
            return dv, jnp.sum(dn * scale * vh, axis=0, keepdims=True)

        dq, dgq = back(zq_ref[...], dqn_ref[...], gq_ref[...], HEAD_DIM ** -0.5)
        dk, dgk = back(zk_ref[...], dkn_ref[...], gk_ref[...], 1.0)
        dgq_ref[...] += dgq
        dgk_ref[...] += dgk
        dz_ref[:, 0:1536] = dzc_ref[...]
        dz_ref[:, 1536:2048] = dq.astype(BF16)
        dz_ref[:, 2048:2560] = dk.astype(BF16)
        dz_ref[:, 2560:3072] = dv_ref[...].astype(BF16)

    blk = pl.BlockSpec((tm, 512), lambda i: (i, 0))
    vec = pl.BlockSpec((1, 512), lambda i: (0, 0))
    return _call(
        body, [z, z, dzc, dqn, dkn, dv, gq, gk, ones_bd], name="qkv_bwd", grid=(t // tm,),
        out_shape=[_sds((t, D_IN), BF16), _sds((1, 512), F32), _sds((1, 512), F32)],
        in_specs=[pl.BlockSpec((tm, 512), lambda i: (i, 3)), pl.BlockSpec((tm, 512), lambda i: (i, 4)),
                  pl.BlockSpec((tm, 1536), lambda i: (i, 0))] + [blk] * 3
        + [vec, vec, pl.BlockSpec((512, 512), lambda i: (0, 0))],
        out_specs=[pl.BlockSpec((tm, D_IN), lambda i: (i, 0)), vec, vec],
        carry=carry)


def _columns_from_chips(g):
    return g.transpose(1, 0, 2).reshape(g.shape[1], N_CHIPS * g.shape[2])


def _columns_to_chips(w):
    k, n4 = w.shape
    return w.reshape(k, N_CHIPS, n4 // N_CHIPS).transpose(1, 0, 2)


def kernel(x, g_mix, w_in, conv_w, g_q, g_k, g_conv_out, g_attn_out, w_out, g_ffn, w_gate, w_up, w_down, loss_target, m_g_mix, m_w_in, m_conv_w, m_g_q, m_g_k, m_g_conv_out, m_g_attn_out, m_w_out, m_g_ffn, m_w_gate, m_w_up, m_w_down, v_g_mix, v_w_in, v_conv_w, v_g_q, v_g_k, v_g_conv_out, v_g_attn_out, v_w_out, v_g_ffn, v_w_gate, v_w_up, v_w_down):
    t = x.shape[1]
    xs = x[0]
    target = loss_target[0]
    tm = min(512, t)
    tmm = min(1024, t)

    cw_pad = jnp.pad(conv_w[0], ((0, 13), (0, 0)))
    gathered = _all_gather([w_in[0].astype(BF16), cw_pad])
    win = _columns_from_chips(gathered[0])
    cw = jnp.pad(gathered[1][:, 0:3, :].transpose(1, 0, 2).reshape(3, D_CONV), ((0, 5), (0, 0)))
    later = [w_out[0].astype(BF16), w_gate[0].astype(BF16), w_up[0].astype(BF16),
             w_down[0].astype(BF16)]

    head_id = jnp.arange(D_ATTN) // HEAD_DIM
    ones_bd = (head_id[:, None] == head_id[None, :]).astype(BF16)
    gq_t = jnp.tile(g_q, (1, D_ATTN // HEAD_DIM))
    gk_t = jnp.tile(g_k, (1, D_ATTN // HEAD_DIM))

    h1, z = _norm_matmul("in_proj", xs, g_mix, [win], tmm, 768, False)
    q, k = _qkv_prepare(z, gq_t, gk_t, ones_bd, tm)
    v_col = (3 * D_CONV + 2 * D_ATTN) // BAND
    (y_attn, lse), gathered = _attn_fwd(q, k, z, v_col, carry=_x_gather_chips(later))
    mix, gathered = _mixer_out(z, cw, y_attn, g_conv_out, g_attn_out, tm,
                               carry=_x_gather_sibling(gathered))
    wout = gathered[0].reshape(D_MODEL, D_MODEL)
    wgate = _columns_from_chips(gathered[1])
    wup = _columns_from_chips(gathered[2])
    wdown = gathered[3].reshape(D_FF, D_MODEL)
    (x1,) = _matmul("out_proj", mix, wout, [xs], [F32], lambda acc, r: (r + acc,), tmm, 512)
    h2, gate, up, act = _norm_matmul("ffn_up", x1, g_ffn, [wgate, wup], tmm, 1408, True, BF16)

    def loss_epilogue(acc, r, tgt):
        err = r + acc - tgt
        dy = err * (1.0 / D_MODEL)
        return dy, dy, jnp.sum(err * err)

    dx2, dx2b, loss_sum = _matmul("ffn_down_loss", act, wdown, [x1, target], [F32, BF16],
                                  loss_epilogue, tmm, 512, loss=True)

    def swiglu_bwd(da, gt, u):
        gt, u = gt.astype(F32), u.astype(F32)
        s = _sigmoid(gt)
        return da * u * (s * (1.0 + gt * (1.0 - s))), da * (gt * s)

    dgate, dup = _matmul("ffn_down_bwd", dx2b, wdown, [gate, up], [BF16, BF16], swiglu_bwd,
                         tmm, 1408, transposed_w=True)
    gw_down = _matmul_tn("grad_w_down", act, dx2b, 512, tmm)
    gw_gate = _matmul_tn("grad_w_gate", h2, dgate, 1408, tmm)
    gw_up = _matmul_tn("grad_w_up", h2, dup, 1408, tmm)

    me = 2 * lax.axis_index("x") + lax.axis_index("y")
    where = jnp.stack([lax.axis_index("c"), me]).astype(jnp.int32)

    def pair_sums(names, full, got):
        return [_pair_sum(f"pair_sum_{nme}", a, b, where) for nme, a, b in zip(names, full, got)]

    def chip_sums(names, pair, got):
        return [_chip_sum(f"chip_sum_{nme}", own, b) for nme, (_, own), b in zip(names, pair, got)]

    ffn = ["w_gate", "w_up", "w_down"]
    full = [_columns_to_chips(gw_gate), _columns_to_chips(gw_up),
            gw_down.reshape(N_CHIPS, D_FF // N_CHIPS, D_MODEL)]
    (dx1, dx1b, gg_ffn), got = _matmul_norm_bwd("ffn_up_bwd", [(dgate, wgate), (dup, wup)], x1, dx2,
                                                g_ffn, tm, carry=_x_pair(full))
    pair = pair_sums(ffn, full, got)
    (dmix,) = _matmul("out_proj_bwd", dx1b, wout, [], [F32], lambda acc: (acc,), tmm, 512,
                      transposed_w=True)
    gw_out = _matmul_tn("grad_w_out", mix, dx1b, 512, tmm)
    full = [gw_out.reshape(N_CHIPS, D_MODEL // N_CHIPS, D_MODEL)]
    (dzc, do, dd, gcw, gg_conv, gg_attn), got = _mixer_bwd(
        z, dmix, y_attn, cw, g_conv_out, g_attn_out, ones_bd, tm, carry=_x_pair(full))
    pair += pair_sums(["w_out"], full, got)
    early = ffn + ["w_out"]
    (dqn, dkn, dv), got = _attn_bwd(q, k, z, v_col, do, lse, dd,
                                    carry=_x_chips([p for p, _ in pair]))
    mine = chip_sums(early, pair, got)
    (dz, gg_q, gg_k), theirs = _qkv_bwd(z, dzc, dqn, dkn, dv, gq_t, gk_t, ones_bd, tm,
                                        carry=_x_share(mine))
    gw_in = _matmul_tn("grad_w_in", h1, dz, 768, tmm)
    full = [_columns_to_chips(gw_in)]
    (grad_x, _, gg_mix), got = _matmul_norm_bwd("in_proj_bwd", [(dz, win)], xs, dx1, g_mix,
                                                tm, carry=_x_pair(full))
    pair = pair_sums(["w_in"], full, got)
    got = _exchange_alone("grad_chip_exchange_w_in", _x_chips([pair[0][0]]))
    mine += chip_sums(["w_in"], pair, got)
    theirs = list(theirs) + list(_exchange_alone("grad_pair_share_w_in", _x_share(mine[-1:])))
    big = early + ["w_in"]

    small = _small_all_reduce({
        "g_mix": gg_mix, "g_ffn": gg_ffn, "g_conv_out": gg_conv, "g_attn_out": gg_attn,
        "g_q": gg_q, "g_k": gg_k, "loss": loss_sum, "conv_w": gcw})
    heads = D_ATTN // HEAD_DIM
    grads = {
        "g_mix": small[0:1, :], "g_ffn": small[1:2, :],
        "g_conv_out": small[2:3, 0:512], "g_attn_out": small[2:3, 512:1024],
        "g_q": small[3, 0:512].reshape(heads, HEAD_DIM).sum(axis=0)[None, :],
        "g_k": small[3, 512:1024].reshape(heads, HEAD_DIM).sum(axis=0)[None, :],
        "conv_w": lax.dynamic_slice(small[8:11, 0:512], (0, me * (D_CONV // N_CHIPS)),
                                    (3, D_CONV // N_CHIPS)),
    }
    halves = dict(zip(big, zip(mine, theirs)))
    loss = small[4, 0] * 0.5 * (1.0 / D_MODEL)

    weights = dict(g_mix=g_mix, w_in=w_in, conv_w=conv_w, g_q=g_q, g_k=g_k, g_conv_out=g_conv_out,
                   g_attn_out=g_attn_out, w_out=w_out, g_ffn=g_ffn, w_gate=w_gate, w_up=w_up,
                   w_down=w_down)
    moments_m = dict(g_mix=m_g_mix, w_in=m_w_in, conv_w=m_conv_w, g_q=m_g_q, g_k=m_g_k,
                     g_conv_out=m_g_conv_out, g_attn_out=m_g_attn_out, w_out=m_w_out, g_ffn=m_g_ffn,
                     w_gate=m_w_gate, w_up=m_w_up, w_down=m_w_down)
    moments_v = dict(g_mix=v_g_mix, w_in=v_w_in, conv_w=v_conv_w, g_q=v_g_q, g_k=v_g_k,
                     g_conv_out=v_g_conv_out, g_attn_out=v_g_attn_out, w_out=v_w_out, g_ffn=v_g_ffn,
                     w_gate=v_w_gate, w_up=v_w_up, w_down=v_w_down)
    names = list(weights)
    out_g, out_d, out_m, out_v = [], [], [], []
    for nme in names:
        wgt = weights[nme]
        shape2 = wgt.shape[-2:] if wgt.ndim == 3 else wgt.shape
        state = (wgt.reshape(shape2), moments_m[nme].reshape(shape2), moments_v[nme].reshape(shape2))
        if nme in halves:
            g2, dlt, nm, nv = _adamw_shard(f"adamw_{nme}", *state, *halves[nme], where)
        else:
            g2 = grads[nme].reshape(shape2)
            dlt, nm, nv = _adamw(f"adamw_{nme}", state[0], g2, state[1], state[2])
        out_g.append(g2.reshape(wgt.shape))
        out_d.append(dlt.reshape(wgt.shape))
        out_m.append(nm.reshape(wgt.shape))
        out_v.append(nv.reshape(wgt.shape))
    return (loss, grad_x[None], *out_g, *out_d, *out_m, *out_v)
```

```python
import functools
from typing import Any, Callable, NamedTuple, Sequence

import jax
import jax.numpy as jnp
from jax import lax
from jax.experimental import pallas as pl
from jax.experimental.pallas import tpu as pltpu

F32 = jnp.float32
BF16 = jnp.bfloat16
MESH = pl.DeviceIdType.MESH

D_MODEL = 1024
D_CONV = 512
D_ATTN = 512
HEAD_DIM = 64
D_FF = 2816
D_IN = 3 * D_CONV + 3 * D_ATTN
DILATIONS = (1, 4, 16)
BAND = 128
EPS = 1e-6
NEG = -1e30
N_CHIPS = 4

ADAM_LR = 0.001
ADAM_B1 = 0.9
ADAM_B2 = 0.999
ADAM_EPS = 1e-08
ADAM_WD = 0.01
ADAM_STEP = 10

V7X_VMEM_BYTES = 64 * 1024 * 1024
VMEM_LIMIT = V7X_VMEM_BYTES - 8 * 1024 * 1024
ANY = pl.BlockSpec(memory_space=pl.ANY)
VMEM_WHOLE = pl.BlockSpec(memory_space=pltpu.VMEM)


def _params(*sem):
    return pltpu.CompilerParams(dimension_semantics=sem, vmem_limit_bytes=VMEM_LIMIT)


def _sds(shape, dtype):
    return jax.ShapeDtypeStruct(shape, dtype)


def _resident(whole):
    return pl.Buffered(1) if whole else None


def _place():
    x, y, c = lax.axis_index("x"), lax.axis_index("y"), lax.axis_index("c")
    chips = [(1 - x, y), (x, 1 - y), (1 - x, 1 - y)]
    return x, y, c, 2 * x + y, chips, [2 * cx + cy for cx, cy in chips]


def _all_gather(shards):
    n = len(shards)

    def body(*refs):
        ins, outs, stage = refs[:n], refs[n:2 * n], refs[2 * n:3 * n]
        ssem, rsem, fsem, gsem, lsem, osem = refs[3 * n:]
        x, y, c, me, chips, cids = _place()
        sib = (x, y, 1 - c)

        def half(w, which):
            h = shards[w].shape[0] // 2
            return pl.ds(pl.multiple_of(which * h, 8), h)

        loads = [pltpu.make_async_copy(ins[w], stage[w], lsem.at[w]) for w in range(n)]
        local = [pltpu.make_async_copy(stage[w], outs[w].at[me], osem.at[w]) for w in range(n)]
        for cp in loads:
            cp.start()

        def chip_copy(w, j, src_slot):
            rows = half(w, c)
            return pltpu.make_async_remote_copy(
                src_ref=ins[w].at[rows], dst_ref=outs[w].at[src_slot, rows],
                send_sem=ssem.at[3 * w + j], recv_sem=rsem.at[3 * w + j],
                device_id=(*chips[j], c), device_id_type=MESH)

        def sib_copy(w, j, which):
            rows = half(w, which)
            return pltpu.make_async_remote_copy(
                src_ref=outs[w].at[cids[j], rows], dst_ref=outs[w].at[cids[j], rows],
                send_sem=fsem.at[3 * w + j], recv_sem=gsem.at[3 * w + j],
                device_id=sib, device_id_type=MESH)

        sends = [chip_copy(w, j, me) for w in range(n) for j in range(3)]
        for cp in sends:
            cp.start()
        for w in range(n):
            loads[w].wait()
            local[w].start()
        passed = []
        for w in range(n):
            for j in range(3):
                chip_copy(w, j, cids[j]).wait_recv()
                cp = sib_copy(w, j, c)
                cp.start()
                passed.append(cp)
        for w in range(n):
            for j in range(3):
                sib_copy(w, j, 1 - c).wait_recv()
        for cp in sends + passed:
            cp.wait_send()
        for cp in local:
            cp.wait()

    return pl.pallas_call(
        body, name="all_gather_weights",
        out_shape=[_sds((N_CHIPS,) + s.shape, s.dtype) for s in shards],
        in_specs=[ANY] * n, out_specs=[ANY] * n,
        scratch_shapes=[pltpu.VMEM(s.shape, s.dtype) for s in shards]
        + [pltpu.SemaphoreType.DMA((3 * n,))] * 4 + [pltpu.SemaphoreType.DMA((n,))] * 2,
        compiler_params=pltpu.CompilerParams(vmem_limit_bytes=VMEM_LIMIT),
    )(*shards)


class _Exchange(NamedTuple):
    srcs: Sequence[Any]
    lands: Sequence[Any]
    outs: Sequence[Any]
    n_sems: int
    copies: Callable


def _remote(src, dst, ssem, rsem, k, to):
    return pltpu.make_async_remote_copy(src_ref=src, dst_ref=dst, send_sem=ssem.at[k],
                                        recv_sem=rsem.at[k], device_id=to, device_id_type=MESH)


def _x_gather_chips(shards):
    def copies(srcs, lands, outs, ssem, rsem):
        _, _, c, me, chips, cids = _place()
        go, arrive = [], []
        for w, s in enumerate(shards):
            h = s.shape[0] // 2
            rows = pl.ds(pl.multiple_of(c * h, 8), h)
            for j in range(3):
                to = (*chips[j], c)
                go.append(_remote(srcs[w].at[rows], lands[w].at[me, rows], ssem, rsem, 3 * w + j, to))
                arrive.append(_remote(srcs[w].at[rows], lands[w].at[cids[j], rows], ssem, rsem,
                                      3 * w + j, to))
        return go, arrive

    lands = [jnp.broadcast_to(s[None], (N_CHIPS,) + s.shape) for s in shards]
    return _Exchange(shards, lands, [], 3 * len(shards), copies)


def _x_gather_sibling(gathered):
    def copies(srcs, lands, outs, ssem, rsem):
        x, y, c, _, _, cids = _place()
        go, arrive = [], []
        for w, g in enumerate(gathered):
            h = g.shape[1] // 2
            mine = pl.ds(pl.multiple_of(c * h, 8), h)
            theirs = pl.ds(pl.multiple_of((1 - c) * h, 8), h)
            for j in range(3):
                slab = lands[w].at[cids[j]]
                go.append(_remote(slab.at[mine], slab.at[mine], ssem, rsem, 3 * w + j, (x, y, 1 - c)))
                arrive.append(_remote(slab.at[theirs], slab.at[theirs], ssem, rsem, 3 * w + j,
                                      (x, y, 1 - c)))
        return go, arrive

    return _Exchange([], gathered, [], 3 * len(gathered), copies)


def _x_pair(grads):
    def copies(srcs, lands, outs, ssem, rsem):
        x, y, c, _, _, _ = _place()
        go = []
        for w, g in enumerate(grads):
            h = g.shape[1] // 2
            theirs = pl.ds(pl.multiple_of((1 - c) * h, 8), h)
            go.append(_remote(srcs[w].at[:, theirs, :], outs[w], ssem, rsem, w, (x, y, 1 - c)))
        return go, go

    outs = [_sds((N_CHIPS, g.shape[1] // 2, g.shape[2]), g.dtype) for g in grads]
    return _Exchange(grads, [], outs, len(grads), copies)


def _x_chips(parts):
    def copies(srcs, lands, outs, ssem, rsem):
        _, _, c, _, chips, cids = _place()
        go = [_remote(srcs[w].at[cids[j]], outs[w].at[j], ssem, rsem, 3 * w + j, (*chips[j], c))
              for w in range(len(parts)) for j in range(3)]
        return go, go

    outs = [_sds((3,) + p.shape[1:], p.dtype) for p in parts]
    return _Exchange(parts, [], outs, 3 * len(parts), copies)


def _x_share(halves):
    def copies(srcs, lands, outs, ssem, rsem):
        x, y, c, _, _, _ = _place()
        go = [_remote(srcs[w], outs[w], ssem, rsem, w, (x, y, 1 - c)) for w in range(len(halves))]
        return go, go

    return _Exchange(halves, [], [_sds(h.shape, h.dtype) for h in halves], len(halves), copies)


def _call(body, args, *, name, grid, in_specs, out_specs, out_shape, scratch_shapes=(),
          semantics=None, carry=None):
    single = not isinstance(out_shape, (list, tuple))
    out_shape = [out_shape] if single else list(out_shape)
    out_specs = [out_specs] if single else list(out_specs)
    if carry is None:
        res = pl.pallas_call(
            body, name=name, grid=grid, in_specs=list(in_specs), out_specs=out_specs,
            out_shape=out_shape, scratch_shapes=list(scratch_shapes),
            compiler_params=_params(*(semantics or ("arbitrary",) * len(grid))))(*args)
        return res[0] if single else res
    n_in, n_out, n_scr = len(args), len(out_shape), len(scratch_shapes)
    n_src, n_land, n_new = len(carry.srcs), len(carry.lands), len(carry.outs)

    def carrying(*refs):
        at = 0
        parts = []
        for n in (n_in, n_src, n_land, n_out, n_land, n_new, n_scr, 2):
            parts.append(refs[at:at + n])
            at += n
        ins, srcs, _, outs, lands, news, scratch, (ssem, rsem) = parts
        ids = [pl.program_id(a) for a in range(len(grid))]
        first = functools.reduce(jnp.logical_and, [i == 0 for i in ids])
        last = functools.reduce(jnp.logical_and, [i == g - 1 for i, g in zip(ids, grid)])
        go, arrive = carry.copies(srcs, lands, news, ssem, rsem)

        @pl.when(first)
        def _():
            for cp in go:
                cp.start()

        body(*ins, *outs, *scratch)

        @pl.when(last)
        def _():
            for cp in go:
                cp.wait_send()
            for cp in arrive:
                cp.wait_recv()

    res = pl.pallas_call(
        carrying, name=name, grid=grid,
        in_specs=list(in_specs) + [ANY] * (n_src + n_land),
        out_specs=out_specs + [ANY] * (n_land + n_new),
        out_shape=out_shape + [_sds(a.shape, a.dtype) for a in carry.lands] + list(carry.outs),
        input_output_aliases={n_in + n_src + i: n_out + i for i in range(n_land)},
        scratch_shapes=list(scratch_shapes) + [pltpu.SemaphoreType.DMA((carry.n_sems,))] * 2,
        compiler_params=_params(*(("arbitrary",) * len(grid))))(*args, *carry.srcs, *carry.lands)
    own = res[:n_out]
    return (own[0] if single else own), res[n_out:]


def _exchange_alone(name, exchange):
    def body(x_ref, o_ref):
        o_ref[...] = x_ref[...]

    blk = pl.BlockSpec((8, 128), lambda i: (0, 0))
    _, res = _call(body, [jnp.zeros((8, 128), F32)], name=name, grid=(1,), in_specs=[blk],
                   out_specs=blk, out_shape=_sds((8, 128), F32), carry=exchange)
    return res


def _row_block(r, want):
    return max(d for d in range(1, min(want, r) + 1) if r % d == 0 and (d % 8 == 0 or d == r))


def _pair_sum(name, full, got, where):
    _, r, n = full.shape
    h = r // 2
    tr = _row_block(h, 256)
    nb = h // tr

    def body(w_ref, a_ref, b_ref, o_ref, own_ref):
        total = a_ref[...] + b_ref[...]
        o_ref[...] = total.astype(BF16)

        @pl.when(pl.program_id(1) == w_ref[1])
        def _():
            own_ref[...] = total[0]

    blk = pl.BlockSpec((1, tr, n), lambda i, s, w: (s, i, 0))
    return pl.pallas_call(
        body, name=name, out_shape=[_sds(got.shape, BF16), _sds((h, n), F32)],
        grid_spec=pltpu.PrefetchScalarGridSpec(
            num_scalar_prefetch=1, grid=(nb, N_CHIPS),
            in_specs=[pl.BlockSpec((1, tr, n), lambda i, s, w: (s, w[0] * nb + i, 0)), blk],
            out_specs=[blk, pl.BlockSpec((tr, n), lambda i, s, w: (i, 0))]),
        compiler_params=_params("parallel", "arbitrary"),
    )(where, full, got)


def _chip_sum(name, own, got):
    h, n = own.shape
    tr = _row_block(h, 256)

    def body(a_ref, b0, b1, b2, o_ref):
        o_ref[...] = ((a_ref[...] + b0[0].astype(F32)) + b1[0].astype(F32)) + b2[0].astype(F32)

    def slot(j):
        return pl.BlockSpec((1, tr, n), lambda i: (j, i, 0))

    blk = pl.BlockSpec((tr, n), lambda i: (i, 0))
    return pl.pallas_call(
        body, name=name, grid=(h // tr,), out_shape=_sds((h, n), F32),
        in_specs=[blk, slot(0), slot(1), slot(2)], out_specs=blk,
        compiler_params=_params("parallel"),
    )(own, got, got, got)


SMALL_ROWS = 16
SMALL_LAYOUT = (
    ("g_mix", 0, 0, 1, 1024), ("g_ffn", 1, 0, 1, 1024), ("g_conv_out", 2, 0, 1, 512),
    ("g_attn_out", 2, 512, 1, 512), ("g_q", 3, 0, 1, 512), ("g_k", 3, 512, 1, 512),
    ("loss", 4, 0, 1, 128), ("conv_w", 8, 0, 8, 512))


def _small_all_reduce(parts):
    names = [s[0] for s in SMALL_LAYOUT]

    def body(*refs):
        ins = refs[:len(names)]
        out_ref, stage, buf, ssem, rsem = refs[len(names):]
        x, y, c, _, _, _ = _place()
        me = 4 * x + 2 * y + c
        stage[...] = jnp.zeros_like(stage)
        for ref, (_, r0, c0, nr, nc) in zip(ins, SMALL_LAYOUT):
            stage[r0:r0 + nr, c0:c0 + nc] = ref[0:nr, :]
        buf[me] = stage[...]
        peers = []
        for d in range(1, 8):
            px = 1 - x if d & 4 else x
            py = 1 - y if d & 2 else y
            pc = 1 - c if d & 1 else c
            peers.append(((px, py, pc), 4 * px + 2 * py + pc))
        sends = [pltpu.make_async_remote_copy(
            src_ref=stage, dst_ref=buf.at[me], send_sem=ssem.at[k], recv_sem=rsem.at[k],
            device_id=peer, device_id_type=MESH) for k, (peer, _) in enumerate(peers)]
        for cp in sends:
            cp.start()
        for k, (peer, pid) in enumerate(peers):
            pltpu.make_async_remote_copy(
                src_ref=stage, dst_ref=buf.at[pid], send_sem=ssem.at[k], recv_sem=rsem.at[k],
                device_id=peer, device_id_type=MESH).wait_recv()
        for cp in sends:
            cp.wait_send()
        acc = buf[0]
        for k in range(1, 8):
            acc = acc + buf[k]
        out_ref[...] = acc

    return pl.pallas_call(
        body, name="small_all_reduce", out_shape=_sds((SMALL_ROWS, 1024), F32),
        in_specs=[VMEM_WHOLE] * len(names), out_specs=VMEM_WHOLE,
        scratch_shapes=[pltpu.VMEM((SMALL_ROWS, 1024), F32), pltpu.VMEM((8, SMALL_ROWS, 1024), F32),
                        pltpu.SemaphoreType.DMA((7,)), pltpu.SemaphoreType.DMA((7,))],
    )(*[parts[k] for k in names])


def _dot(a, b):
    return jnp.dot(a, b, preferred_element_type=F32)


def _dot_nt(a, b):
    return lax.dot_general(a, b, (((1,), (1,)), ((), ())), preferred_element_type=F32)


def _dot_tn(a, b):
    return lax.dot_general(a, b, (((0,), (0,)), ((), ())), preferred_element_type=F32)


def _sigmoid(v):
    return 1.0 / (1.0 + jnp.exp(-v))


def _rms_scale(v):
    return lax.rsqrt(jnp.mean(v * v, axis=-1, keepdims=True) + EPS)


def _rms_bwd(v, r, g, dy):
    vh = v * r
    dh = dy * g
    return r * (dh - vh * jnp.mean(dh * vh, axis=-1, keepdims=True)), vh


def _head_sum(a, ones_bd):
    hi = a.astype(BF16)
    lo = (a - hi.astype(F32)).astype(BF16)
    return _dot(hi, ones_bd) + _dot(lo, ones_bd)


def _head_rms_scale(v, ones_bd):
    return lax.rsqrt(_head_sum(v * v, ones_bd) * (1.0 / HEAD_DIM) + EPS)


def _norm_matmul(name, x, g, ws, tm, tn, swiglu, out_dtype=F32):
    t, d = x.shape
    n = ws[0].shape[1]
    nw = len(ws)

    def body(x_ref, g_ref, *refs):
        w_refs, h_ref, o_refs = refs[:nw], refs[nw], refs[nw + 1:2 * nw + 1]
        hs = refs[-1]

        @pl.when(pl.program_id(1) == 0)
        def _():
            xv = x_ref[...]
            h = (xv * _rms_scale(xv) * g_ref[...]).astype(BF16)
            hs[...] = h
            h_ref[...] = h

        h = hs[...]
        outs = [_dot(h, w[...]) for w in w_refs]
        for o_ref, o in zip(o_refs, outs):
            o_ref[...] = o.astype(out_dtype)
        if swiglu:
            refs[2 * nw + 1][...] = (outs[0] * _sigmoid(outs[0]) * outs[1]).astype(BF16)

    row = pl.BlockSpec((tm, d), lambda i, j: (i, 0))
    col = pl.BlockSpec((tm, tn), lambda i, j: (i, j))
    out_shape = [_sds((t, d), BF16)] + [_sds((t, n), out_dtype)] * nw
    out_specs = [row] + [col] * nw
    if swiglu:
        out_shape.append(_sds((t, n), BF16))
        out_specs.append(col)
    return pl.pallas_call(
        body, name=name, grid=(t // tm, n // tn), out_shape=out_shape,
        in_specs=[row, pl.BlockSpec((1, d), lambda i, j: (0, 0))]
        + [pl.BlockSpec((d, tn), lambda i, j: (0, j), pipeline_mode=_resident(tn == n))] * nw,
        out_specs=out_specs, scratch_shapes=[pltpu.VMEM((tm, d), BF16)],
        compiler_params=_params("parallel", "arbitrary"),
    )(x, g, *ws)


def _matmul(name, a, w, extras, out_dtypes, epilogue, tm, tn, transposed_w=False, loss=False):
    t, k = a.shape
    n = w.shape[0] if transposed_w else w.shape[1]
    ne, no = len(extras), len(out_dtypes)

    def body(a_ref, w_ref, *refs):
        e_refs, o_refs = refs[:ne], refs[ne:]
        acc = _dot_nt(a_ref[...], w_ref[...]) if transposed_w else _dot(a_ref[...], w_ref[...])
        res = epilogue(acc, *[e[...] for e in e_refs])
        for o_ref, r in zip(o_refs[:no], res[:no]):
            o_ref[...] = r.astype(o_ref.dtype)
        if loss:
            first = jnp.logical_and(pl.program_id(0) == 0, pl.program_id(1) == 0)

            @pl.when(first)
            def _():
                o_refs[no][...] = jnp.zeros_like(o_refs[no])

            o_refs[no][...] += res[no]

    col = pl.BlockSpec((tm, tn), lambda i, j: (i, j))
    w_spec = (pl.BlockSpec((tn, k), lambda i, j: (j, 0), pipeline_mode=_resident(tn == n))
              if transposed_w
              else pl.BlockSpec((k, tn), lambda i, j: (0, j), pipeline_mode=_resident(tn == n)))
    out_shape = [_sds((t, n), dt) for dt in out_dtypes]
    out_specs = [col] * no
    if loss:
        out_shape.append(_sds((8, 128), F32))
        out_specs.append(pl.BlockSpec((8, 128), lambda i, j: (0, 0)))
    return pl.pallas_call(
        body, name=name, grid=(t // tm, n // tn), out_shape=out_shape,
        in_specs=[pl.BlockSpec((tm, k), lambda i, j: (i, 0)), w_spec] + [col] * ne,
        out_specs=out_specs,
        compiler_params=_params(*(("arbitrary", "arbitrary") if loss else ("parallel", "parallel"))),
    )(a, w, *extras)


def _matmul_norm_bwd(name, pairs, x, dres, g, tm, carry=None):
    t, d = x.shape
    npairs = len(pairs)

    def body(*refs):
        a_refs, w_refs = refs[:npairs], refs[npairs:2 * npairs]
        x_ref, r_ref, g_ref, dx_ref, dxb_ref, dg_ref = refs[2 * npairs:]
        dy = _dot_nt(a_refs[0][...], w_refs[0][...])
        for a_ref, w_ref in zip(a_refs[1:], w_refs[1:]):
            dy = dy + _dot_nt(a_ref[...], w_ref[...])
        xv = x_ref[...]
        dx, xh = _rms_bwd(xv, _rms_scale(xv), g_ref[...], dy)
        dx = dx + r_ref[...]
        dx_ref[...] = dx
        dxb_ref[...] = dx.astype(BF16)

        @pl.when(pl.program_id(0) == 0)
        def _():
            dg_ref[...] = jnp.zeros_like(dg_ref)

        dg_ref[...] += jnp.sum(dy * xh, axis=0, keepdims=True)

    row = pl.BlockSpec((tm, d), lambda i: (i, 0))
    vec = pl.BlockSpec((1, d), lambda i: (0, 0))
    return _call(
        body, [a for a, _ in pairs] + [w for _, w in pairs] + [x, dres, g], name=name,
        grid=(t // tm,), out_shape=[_sds((t, d), F32), _sds((t, d), BF16), _sds((1, d), F32)],
        in_specs=[pl.BlockSpec((tm, a.shape[1]), lambda i: (i, 0)) for a, _ in pairs]
        + [pl.BlockSpec(w.shape, lambda i: (0, 0), pipeline_mode=pl.Buffered(1)) for _, w in pairs]
        + [row, row, vec],
        out_specs=[row, row, vec], carry=carry)


def _matmul_tn(name, a, g, tn, tk, by_chip=False):
    t, ka = a.shape
    n = g.shape[1]

    def body(a_ref, g_ref, o_ref):
        @pl.when(pl.program_id(1) == 0)
        def _():
            o_ref[...] = jnp.zeros_like(o_ref)

        acc = _dot_tn(a_ref[...], g_ref[...])
        o_ref[...] += acc[None] if by_chip else acc

    return pl.pallas_call(
        body, name=name, grid=(n // tn, t // tk),
        out_shape=_sds((n // tn, ka, tn) if by_chip else (ka, n), F32),
        in_specs=[pl.BlockSpec((tk, ka), lambda j, s: (s, 0)),
                  pl.BlockSpec((tk, tn), lambda j, s: (s, j))],
        out_specs=(pl.BlockSpec((1, ka, tn), lambda j, s: (j, 0, 0)) if by_chip
                   else pl.BlockSpec((ka, tn), lambda j, s: (0, j))),
        compiler_params=_params("parallel", "arbitrary"),
    )(a, g)


def _elementwise(name, fn, ins, out_dtypes, tr):
    r, n = ins[0].shape
    tr = _row_block(r, tr)
    ni = len(ins)

    def body(*refs):
        res = fn(*[ref[...] for ref in refs[:ni]])
        for o_ref, v in zip(refs[ni:], res):
            o_ref[...] = v.astype(o_ref.dtype)

    blk = pl.BlockSpec((tr, n), lambda i: (i, 0))
    return pl.pallas_call(
        body, name=name, grid=(r // tr,), out_shape=[_sds((r, n), dt) for dt in out_dtypes],
        in_specs=[blk] * ni, out_specs=[blk] * len(out_dtypes),
        compiler_params=_params("parallel"),
    )(*ins)


def _adamw_update(w, g, m, v):
    m = ADAM_B1 * m + (1.0 - ADAM_B1) * g
    v = ADAM_B2 * v + (1.0 - ADAM_B2) * (g * g)
    m_hat = m / (1.0 - ADAM_B1 ** ADAM_STEP)
    v_hat = v / (1.0 - ADAM_B2 ** ADAM_STEP)
    return -ADAM_LR * (m_hat / (jnp.sqrt(v_hat) + ADAM_EPS) + ADAM_WD * w), m, v


def _adamw(name, w, g, m, v):
    return _elementwise(name, _adamw_update, [w, g, m, v], [F32] * 3, 256)


def _adamw_shard(name, w, m, v, mine, theirs, where):
    r, n = w.shape
    h = r // 2
    tr = _row_block(h, 256)
    nb = h // tr

    def body(w_ref, p_ref, m_ref, v_ref, a_ref, b_ref, g_ref, d_ref, nm_ref, nv_ref):
        g = jnp.where(pl.program_id(0) == w_ref[0], a_ref[...], b_ref[...])
        g_ref[...] = g
        d_ref[...], nm_ref[...], nv_ref[...] = _adamw_update(p_ref[...], g, m_ref[...], v_ref[...])

    whole = pl.BlockSpec((tr, n), lambda s, i, c: (s * nb + i, 0))
    half = pl.BlockSpec((tr, n), lambda s, i, c: (i, 0))
    return pl.pallas_call(
        body, name=name, out_shape=[_sds((r, n), F32)] * 4,
        grid_spec=pltpu.PrefetchScalarGridSpec(
            num_scalar_prefetch=1, grid=(2, nb), in_specs=[whole] * 3 + [half] * 2,
            out_specs=[whole] * 4),
        compiler_params=_params("parallel", "parallel"),
    )(where, w, m, v, mine, theirs)


def _qkv_prepare(z, gq, gk, ones_bd, tm):
    t = z.shape[0]

    def body(zq_ref, zk_ref, gq_ref, gk_ref, bd_ref, q_ref, k_ref):
        bd = bd_ref[...]
        q = zq_ref[...]
        k = zk_ref[...]
        q_ref[...] = (q * _head_rms_scale(q, bd) * gq_ref[...]) * HEAD_DIM ** -0.5
        k_ref[...] = k * _head_rms_scale(k, bd) * gk_ref[...]

    vec = pl.BlockSpec((1, 512), lambda i: (0, 0))
    out = pl.BlockSpec((tm, 512), lambda i: (i, 0))
    return pl.pallas_call(
        body, name="qkv_prepare", grid=(t // tm,), out_shape=[_sds((t, 512), F32)] * 2,
        in_specs=[pl.BlockSpec((tm, 512), lambda i: (i, 3)), pl.BlockSpec((tm, 512), lambda i: (i, 4)),
                  vec, vec, pl.BlockSpec((512, 512), lambda i: (0, 0))],
        out_specs=[out] * 2, compiler_params=_params("parallel"),
    )(z, z, gq, gk, ones_bd)


TOK = 2048
UNITS = TOK // BAND


def _stack_masks():
    row = lax.broadcasted_iota(jnp.int32, (2 * BAND, 2 * BAND), 0) & (BAND - 1)
    col = lax.broadcasted_iota(jnp.int32, (2 * BAND, 2 * BAND), 1)
    lane = lax.broadcasted_iota(jnp.int32, (BAND, BAND), 1)
    head0 = lane < HEAD_DIM
    ones = [jnp.where(head0, 1.0, 0.0).astype(BF16), jnp.where(head0, 0.0, 1.0).astype(BF16)]
    return col - row, col, head0, ones


def _split3(x):
    hi = x.astype(BF16).astype(F32)
    mid = (x - hi).astype(BF16).astype(F32)
    return hi, mid, x - hi - mid


def _gather(srcs, dst, d):
    per = TOK // d
    at = 0
    for r in range(d):
        for src in srcs:
            rows = src[pl.ds(r, per, stride=d), :] if d > 1 else src[...]
            dst[pl.ds(at, per), :] = rows.astype(dst.dtype)
            at += per


def _scatter_add(out_ref, src, d, per_src, offset, first):
    per = TOK // d
    if d == 1:
        val = src[pl.ds(offset, per), :]
        out_ref[...] = val if first else out_ref[...] + val
        return
    for r in range(d):
        val = src[pl.ds(r * per_src + offset, per), :]
        idx = pl.ds(r, per, stride=d)
        out_ref[idx, :] = val if first else out_ref[idx, :] + val


def _attn_fwd(q, k, v, v_col, carry=None):
    t = q.shape[0]
    nblk = t // TOK

    def body(q_ref, kp_ref, k_ref, vp_ref, v_ref, y_ref, l_ref, qs, ks, vs, ob, lb, on, ln):
        i = pl.program_id(1)
        diff, col, head0, hm = _stack_masks()
        band_ok = jnp.logical_and(diff >= 0, diff <= BAND)
        for g, d in enumerate(DILATIONS):
            per = TOK // d
            nb = per // BAND
            _gather([q_ref], qs, d)
            _gather([kp_ref, k_ref], ks, d)
            _gather([vp_ref, v_ref], vs, d)

            def unit(u, carry):
                r, b = u // nb, u % nb
                qu = qs[pl.ds(pl.multiple_of(u * BAND, BAND), BAND), :]
                start = pl.multiple_of(r * 2 * per + per + (b - 1) * BAND, BAND)
                kw = ks[pl.ds(start, 2 * BAND), :]
                vw = vs[pl.ds(start, 2 * BAND), :]
                lo = jnp.where(jnp.logical_and(i == 0, b == 0), BAND, 0)
                s = _dot_nt(jnp.concatenate([qu * hm[0], qu * hm[1]], axis=0), kw)
                s = jnp.where(jnp.logical_and(band_ok, col >= lo), s, NEG)
                mx = jnp.max(s, axis=-1, keepdims=True)
                e = jnp.exp(s - mx)
                den = jnp.sum(e, axis=-1, keepdims=True)
                o2 = _dot(e.astype(BF16), vw) / den
                l2 = jnp.broadcast_to(mx + jnp.log(den), (2 * BAND, BAND))
                rows = pl.ds(pl.multiple_of(u * BAND, BAND), BAND)
                ob[rows, :] = jnp.where(head0, o2[:BAND], o2[BAND:])
                lb[rows, :] = jnp.where(head0, l2[:BAND], l2[BAND:])
                return carry

            lax.fori_loop(0, UNITS, unit, 0, unroll=8)
            _scatter_add(on.at[g], ob, d, per, 0, True)
            _scatter_add(ln.at[g], lb, d, per, 0, True)
        ls = [ln[0], ln[1], ln[2]]
        mx = jnp.maximum(jnp.maximum(ls[0], ls[1]), ls[2])
        es = [jnp.exp(l - mx) for l in ls]
        tot = es[0] + es[1] + es[2]
        y_ref[...] = (es[0] * on[0] + es[1] * on[1] + es[2] * on[2]) / tot
        l_ref[...] = mx + jnp.log(tot)

    main = pl.BlockSpec((TOK, BAND), lambda j, i: (i, j))
    prev = pl.BlockSpec((TOK, BAND), lambda j, i: (jnp.maximum(i - 1, 0), j))
    vmain = pl.BlockSpec((TOK, BAND), lambda j, i: (i, j + v_col))
    vprev = pl.BlockSpec((TOK, BAND), lambda j, i: (jnp.maximum(i - 1, 0), j + v_col))
    return _call(
        body, [q, k, k, v, v], name="attn_fwd", grid=(D_ATTN // BAND, nblk),
        out_shape=[_sds((t, D_ATTN), F32)] * 2,
        in_specs=[main, prev, main, vprev, vmain], out_specs=[main, main],
        scratch_shapes=[pltpu.VMEM((TOK, BAND), BF16), pltpu.VMEM((2 * TOK, BAND), BF16),
                        pltpu.VMEM((2 * TOK, BAND), BF16), pltpu.VMEM((TOK, BAND), F32),
                        pltpu.VMEM((TOK, BAND), F32), pltpu.VMEM((3, TOK, BAND), F32),
                        pltpu.VMEM((3, TOK, BAND), F32)],
        semantics=("parallel", "parallel"), carry=carry)


def _attn_bwd(q, k, v, v_col, do, lse, dd, carry=None):
    t = q.shape[0]
    nblk = t // TOK
    offs = [sum(DILATIONS[:g]) * BAND for g in range(len(DILATIONS))]

    def body(q_ref, kp_ref, k_ref, vp_ref, v_ref, do_ref, l_ref, d_ref, dq_ref, dk_ref, dv_ref,
             qs, dos, ks, vs, lsc, dsc, dqb, dkb, dvb, ckb, cvb):
        step = pl.program_id(1)
        i = nblk - 1 - step
        key = lax.broadcasted_iota(jnp.int32, (2 * BAND, 2 * BAND), 0)
        qry = lax.broadcasted_iota(jnp.int32, (2 * BAND, 2 * BAND), 1) & (BAND - 1)
        off = key - qry
        band_ok = jnp.logical_and(off >= 0, off <= BAND)
        lane = lax.broadcasted_iota(jnp.int32, (BAND, BAND), 1)
        head0 = lane < HEAD_DIM
        hm = [jnp.where(head0, 1.0, 0.0).astype(BF16), jnp.where(head0, 0.0, 1.0).astype(BF16)]
        piece = lane & (HEAD_DIM - 1)
        lane2 = lax.broadcasted_iota(jnp.int32, (2 * BAND, BAND), 1) & (HEAD_DIM - 1)
        ones = jnp.where(lane2 < 3, 1.0, 0.0).astype(BF16)

        def pieces(x):
            hi, mid, lo = _split3(-x)
            a = jnp.where(piece == 0, hi, jnp.where(piece == 1, mid, jnp.where(piece == 2, lo, 0.0)))
            return a.astype(BF16)

        for g, d in enumerate(DILATIONS):
            per = TOK // d
            nb = per // BAND
            pad = per + BAND
            _gather([q_ref], qs, d)
            _gather([do_ref], dos, d)
            _gather([l_ref], lsc, d)
            _gather([d_ref], dsc, d)
            _gather([kp_ref, k_ref], ks, d)
            _gather([vp_ref, v_ref], vs, d)
            dkb[...] = jnp.zeros_like(dkb)
            dvb[...] = jnp.zeros_like(dvb)

            def unit(u, c_):
                r, b = u // nb, u % nb
                rows = pl.ds(pl.multiple_of(u * BAND, BAND), BAND)
                qu, dou = qs[rows, :], dos[rows, :]
                la, da = pieces(lsc[rows, :]), pieces(dsc[rows, :])
                q2 = jnp.concatenate([qu * hm[0], qu * hm[1]], axis=0)
                do2 = jnp.concatenate([dou * hm[0], dou * hm[1]], axis=0)
                l2 = jnp.concatenate([la * hm[0], la * hm[1]], axis=0)
                d2 = jnp.concatenate([da * hm[0], da * hm[1]], axis=0)
                start = pl.multiple_of(r * 2 * per + per + (b - 1) * BAND, BAND)
                kw = ks[pl.ds(start, 2 * BAND), :]
                vw = vs[pl.ds(start, 2 * BAND), :]
                lo = jnp.where(jnp.logical_and(i == 0, b == 0), BAND, 0)
                ok = jnp.logical_and(band_ok, key >= lo)
                st = _dot_nt(jnp.concatenate([kw, ones], axis=1), jnp.concatenate([q2, l2], axis=1))
                dpt = _dot_nt(jnp.concatenate([vw, ones], axis=1), jnp.concatenate([do2, d2], axis=1))
                pt = jnp.where(ok, jnp.exp(st), 0.0)
                dst = (pt * dpt).astype(BF16)
                acc = pl.ds(pl.multiple_of(r * pad + b * BAND, BAND), 2 * BAND)
                dkb[acc, :] += _dot(dst, q2)
                dvb[acc, :] += _dot(pt.astype(BF16), do2)
                dq2 = _dot_tn(dst, kw)
                dqb[rows, :] = jnp.where(head0, dq2[:BAND], dq2[BAND:])
                return c_

            lax.fori_loop(0, UNITS, unit, 0, unroll=8)

            for r in range(d):
                last = pl.ds(r * pad + per, BAND)
                kept = pl.ds(offs[g] + r * BAND, BAND)

                @pl.when(step > 0)
                def _():
                    dkb[last, :] += ckb[kept, :]
                    dvb[last, :] += cvb[kept, :]

                ckb[kept, :] = dkb[pl.ds(r * pad, BAND), :]
                cvb[kept, :] = dvb[pl.ds(r * pad, BAND), :]
            _scatter_add(dq_ref, dqb, d, per, 0, g == 0)
            _scatter_add(dk_ref, dkb, d, pad, BAND, g == 0)
            _scatter_add(dv_ref, dvb, d, pad, BAND, g == 0)

    main = pl.BlockSpec((TOK, BAND), lambda j, s: (nblk - 1 - s, j))
    prev = pl.BlockSpec((TOK, BAND), lambda j, s: (jnp.maximum(nblk - 2 - s, 0), j))
    vmain = pl.BlockSpec((TOK, BAND), lambda j, s: (nblk - 1 - s, j + v_col))
    vprev = pl.BlockSpec((TOK, BAND), lambda j, s: (jnp.maximum(nblk - 2 - s, 0), j + v_col))
    acc_rows = max(d * (TOK // d + BAND) for d in DILATIONS)
    kept_rows = sum(DILATIONS) * BAND
    return _call(
        body, [q, k, k, v, v, do, lse, dd], name="attn_bwd",
        grid=(D_ATTN // BAND, nblk), out_shape=[_sds((t, D_ATTN), F32)] * 3,
        in_specs=[main, prev, main, vprev, vmain, main, main, main], out_specs=[main] * 3,
        scratch_shapes=[pltpu.VMEM((TOK, BAND), BF16)] * 2 + [pltpu.VMEM((2 * TOK, BAND), BF16)] * 2
        + [pltpu.VMEM((TOK, BAND), F32)] * 3 + [pltpu.VMEM((acc_rows, BAND), F32)] * 2
        + [pltpu.VMEM((kept_rows, BAND), F32)] * 2,
        semantics=("parallel", "arbitrary"), carry=carry)


def _halo_rows(tm, t):
    per = tm // 8
    prev = lambda i: (jnp.maximum(i * per - 1, 0), 0)
    nxt = lambda i: (jnp.minimum((i + 1) * per, t // 8 - 1), 0)
    return prev, nxt


def _mixer_out(z, cw, y_attn, g_conv, g_attn, tm, carry=None):
    t = z.shape[0]
    prev, _ = _halo_rows(tm, t)

    def body(z_ref, zp_ref, cw_ref, y_ref, gc_ref, ga_ref, mix_ref):
        i = pl.program_id(0)
        keep = jnp.where(i > 0, 1.0, 0.0)
        cu = jnp.concatenate([zp_ref[:, 0:512] * zp_ref[:, 1024:1536] * keep,
                              z_ref[:, 0:512] * z_ref[:, 1024:1536]], axis=0)
        c = (cw_ref[0:1, :] * pltpu.roll(cu, 2, 0) + cw_ref[1:2, :] * pltpu.roll(cu, 1, 0)
             + cw_ref[2:3, :] * cu)[8:, :]
        yc = z_ref[:, 512:1024] * c
        mix_ref[:, 0:512] = (yc * _rms_scale(yc) * gc_ref[...]).astype(BF16)
        ya = y_ref[...]
        mix_ref[:, 512:1024] = (ya * _rms_scale(ya) * ga_ref[...]).astype(BF16)

    blk = pl.BlockSpec((tm, 512), lambda i: (i, 0))
    vec = pl.BlockSpec((1, 512), lambda i: (0, 0))
    return _call(
        body, [z, z, cw, y_attn, g_conv, g_attn], name="mixer_out", grid=(t // tm,),
        out_shape=_sds((t, 1024), BF16),
        in_specs=[pl.BlockSpec((tm, 1536), lambda i: (i, 0)), pl.BlockSpec((8, 1536), prev),
                  pl.BlockSpec((8, 512), lambda i: (0, 0)), blk, vec, vec],
        out_specs=pl.BlockSpec((tm, 1024), lambda i: (i, 0)),
        semantics=("parallel",), carry=carry)


def _mixer_bwd(z, dmix, y_attn, cw, g_conv, g_attn, ones_bd, tm, carry=None):
    t = z.shape[0]
    nblk = t // tm
    prev, nxt = _halo_rows(tm, t)
    e = tm + 16

    def body(z_ref, zp_ref, zn_ref, dm_ref, dmn_ref, y_ref, cw_ref, gc_ref, ga_ref, bd_ref,
             dz_ref, do_ref, dd_ref, dcw_ref, dgc_ref, dga_ref):
        i = pl.program_id(0)
        rows = lax.broadcasted_iota(jnp.int32, (e, 1), 0)
        lo = jnp.where(i > 0, 0, 8)
        hi = jnp.where(i < nblk - 1, e, tm + 8)
        ze = jnp.concatenate([zp_ref[...], z_ref[...], zn_ref[...]], axis=0)
        u, gb, gcv = ze[:, 0:512], ze[:, 512:1024], ze[:, 1024:1536]
        w0, w1, w2 = cw_ref[0:1, :], cw_ref[1:2, :], cw_ref[2:3, :]
        cu = jnp.where(rows >= lo, gcv * u, 0.0)
        cu1, cu2 = pltpu.roll(cu, 1, 0), pltpu.roll(cu, 2, 0)
        c = w0 * cu2 + w1 * cu1 + w2 * cu
        yc = gb * c
        dma = jnp.concatenate([jnp.zeros((8, 512), F32), dm_ref[:, 0:512], dmn_ref[...]], axis=0)
        dyc, ych = _rms_bwd(yc, _rms_scale(yc), gc_ref[...], dma)
        dc = jnp.where(jnp.logical_and(rows >= 8, rows < hi), dyc * gb, 0.0)
        dcu = w0 * pltpu.roll(dc, e - 2, 0) + w1 * pltpu.roll(dc, e - 1, 0) + w2 * dc
        mid = slice(8, 8 + tm)
        dz_ref[:, 0:512] = (dcu * gcv)[mid, :].astype(BF16)
        dz_ref[:, 512:1024] = (dyc * c)[mid, :].astype(BF16)
        dz_ref[:, 1024:1536] = (dcu * u)[mid, :].astype(BF16)

        ya = y_ref[...]
        dmb = dm_ref[:, 512:1024]
        dya, yah = _rms_bwd(ya, _rms_scale(ya), ga_ref[...], dmb)
        do_ref[...] = dya
        dd_ref[...] = _head_sum(dya * ya, bd_ref[...])

        @pl.when(i == 0)
        def _():
            dcw_ref[...] = jnp.zeros_like(dcw_ref)
            dgc_ref[...] = jnp.zeros_like(dgc_ref)
            dga_ref[...] = jnp.zeros_like(dga_ref)

        dcm = jnp.where(rows < tm + 8, dc, 0.0)
        dcw_ref[0:1, :] += jnp.sum(dcm * cu2, axis=0, keepdims=True)
        dcw_ref[1:2, :] += jnp.sum(dcm * cu1, axis=0, keepdims=True)
        dcw_ref[2:3, :] += jnp.sum(dcm * cu, axis=0, keepdims=True)
        dgc_ref[...] += jnp.sum((dma * ych)[mid, :], axis=0, keepdims=True)
        dga_ref[...] += jnp.sum(dmb * yah, axis=0, keepdims=True)

    blk = pl.BlockSpec((tm, 512), lambda i: (i, 0))
    vec = pl.BlockSpec((1, 512), lambda i: (0, 0))
    cwb = pl.BlockSpec((8, 512), lambda i: (0, 0))
    return _call(
        body, [z, z, z, dmix, dmix, y_attn, cw, g_conv, g_attn, ones_bd], name="mixer_bwd",
        grid=(nblk,),
        out_shape=[_sds((t, 1536), BF16), _sds((t, 512), F32), _sds((t, 512), F32),
                   _sds((8, 512), F32), _sds((1, 512), F32), _sds((1, 512), F32)],
        in_specs=[pl.BlockSpec((tm, 1536), lambda i: (i, 0)), pl.BlockSpec((8, 1536), prev),
                  pl.BlockSpec((8, 1536), nxt), pl.BlockSpec((tm, 1024), lambda i: (i, 0)),
                  pl.BlockSpec((8, 512), nxt), blk, cwb, vec, vec,
                  pl.BlockSpec((512, 512), lambda i: (0, 0))],
        out_specs=[pl.BlockSpec((tm, 1536), lambda i: (i, 0)), blk, blk, cwb, vec, vec],
        carry=carry)


def _qkv_bwd(z, dzc, dqn, dkn, dv, gq, gk, ones_bd, tm, carry=None):
    t = z.shape[0]

    def body(zq_ref, zk_ref, dzc_ref, dqn_ref, dkn_ref, dv_ref, gq_ref, gk_ref, bd_ref,
             dz_ref, dgq_ref, dgk_ref):
        bd = bd_ref[...]

        @pl.when(pl.program_id(0) == 0)
        def _():
            dgq_ref[...] = jnp.zeros_like(dgq_ref)
            dgk_ref[...] = jnp.zeros_like(dgk_ref)

        def back(v, dn, g, scale):
            r = _head_rms_scale(v, bd)
            vh = v * r
            dh = dn * (g * scale)
            dv = r * (dh - vh * (_head_sum(dh * vh, bd) * (1.0 / HEAD_DIM)))
            return dv, jnp.sum(dn * scale * vh, axis=0, keepdims=True)

        dq, dgq = back(zq_ref[...], dqn_ref[...], gq_ref[...], HEAD_DIM ** -0.5)
        dk, dgk = back(zk_ref[...], dkn_ref[...], gk_ref[...], 1.0)
        dgq_ref[...] += dgq
        dgk_ref[...] += dgk
        dz_ref[:, 0:1536] = dzc_ref[...]
        dz_ref[:, 1536:2048] = dq.astype(BF16)
        dz_ref[:, 2048:2560] = dk.astype(BF16)
        dz_ref[:, 2560:3072] = dv_ref[...].astype(BF16)

    blk = pl.BlockSpec((tm, 512), lambda i: (i, 0))
    vec = pl.BlockSpec((1, 512), lambda i: (0, 0))
    return _call(
        body, [z, z, dzc, dqn, dkn, dv, gq, gk, ones_bd], name="qkv_bwd", grid=(t // tm,),
        out_shape=[_sds((t, D_IN), BF16), _sds((1, 512), F32), _sds((1, 512), F32)],
        in_specs=[pl.BlockSpec((tm, 512), lambda i: (i, 3)), pl.BlockSpec((tm, 512), lambda i: (i, 4)),
                  pl.BlockSpec((tm, 1536), lambda i: (i, 0))] + [blk] * 3
        + [vec, vec, pl.BlockSpec((512, 512), lambda i: (0, 0))],
        out_specs=[pl.BlockSpec((tm, D_IN), lambda i: (i, 0)), vec, vec],
        carry=carry)


def _columns_from_chips(g):
    return g.transpose(1, 0, 2).reshape(g.shape[1], N_CHIPS * g.shape[2])


def _columns_to_chips(w):
    k, n4 = w.shape
    return w.reshape(k, N_CHIPS, n4 // N_CHIPS).transpose(1, 0, 2)


def kernel(x, g_mix, w_in, conv_w, g_q, g_k, g_conv_out, g_attn_out, w_out, g_ffn, w_gate, w_up, w_down, loss_target, m_g_mix, m_w_in, m_conv_w, m_g_q, m_g_k, m_g_conv_out, m_g_attn_out, m_w_out, m_g_ffn, m_w_gate, m_w_up, m_w_down, v_g_mix, v_w_in, v_conv_w, v_g_q, v_g_k, v_g_conv_out, v_g_attn_out, v_w_out, v_g_ffn, v_w_gate, v_w_up, v_w_down):
    t = x.shape[1]
    xs = x[0]
    target = loss_target[0]
    tm = min(512, t)
    tmm = min(1024, t)

    cw_pad = jnp.pad(conv_w[0], ((0, 13), (0, 0)))
    gathered = _all_gather([w_in[0].astype(BF16), cw_pad])
    win = _columns_from_chips(gathered[0])
    cw = jnp.pad(gathered[1][:, 0:3, :].transpose(1, 0, 2).reshape(3, D_CONV), ((0, 5), (0, 0)))
    later = [w_out[0].astype(BF16), w_gate[0].astype(BF16), w_up[0].astype(BF16),
             w_down[0].astype(BF16)]

    head_id = jnp.arange(D_ATTN) // HEAD_DIM
    ones_bd = (head_id[:, None] == head_id[None, :]).astype(BF16)
    gq_t = jnp.tile(g_q, (1, D_ATTN // HEAD_DIM))
    gk_t = jnp.tile(g_k, (1, D_ATTN // HEAD_DIM))

    h1, z = _norm_matmul("in_proj", xs, g_mix, [win], tm, D_IN, False)
    q, k = _qkv_prepare(z, gq_t, gk_t, ones_bd, tm)
    v_col = (3 * D_CONV + 2 * D_ATTN) // BAND
    (y_attn, lse), gathered = _attn_fwd(q, k, z, v_col, carry=_x_gather_chips(later))
    mix, gathered = _mixer_out(z, cw, y_attn, g_conv_out, g_attn_out, tm,
                               carry=_x_gather_sibling(gathered))
    wout = gathered[0].reshape(D_MODEL, D_MODEL)
    wgate = _columns_from_chips(gathered[1])
    wup = _columns_from_chips(gathered[2])
    wdown = gathered[3].reshape(D_FF, D_MODEL)
    (x1,) = _matmul("out_proj", mix, wout, [xs], [F32], lambda acc, r: (r + acc,), tm, D_MODEL)
    h2, gate, up, act = _norm_matmul("ffn_up", x1, g_ffn, [wgate, wup], tm, D_FF, True, BF16)

    def loss_epilogue(acc, r, tgt):
        err = r + acc - tgt
        dy = err * (1.0 / D_MODEL)
        return dy, dy, jnp.sum(err * err)

    dx2, dx2b, loss_sum = _matmul("ffn_down_loss", act, wdown, [x1, target], [F32, BF16],
                                  loss_epilogue, tm, D_MODEL, loss=True)

    def swiglu_bwd(da, gt, u):
        gt, u = gt.astype(F32), u.astype(F32)
        s = _sigmoid(gt)
        return da * u * (s * (1.0 + gt * (1.0 - s))), da * (gt * s)

    dgate, dup = _matmul("ffn_down_bwd", dx2b, wdown, [gate, up], [BF16, BF16], swiglu_bwd,
                         tm, D_FF, transposed_w=True)
    gw_down = _matmul_tn("grad_w_down", act, dx2b, 512, tmm)
    gw_gate = _matmul_tn("grad_w_gate", h2, dgate, 1408, tmm)
    gw_up = _matmul_tn("grad_w_up", h2, dup, 1408, tmm)

    me = 2 * lax.axis_index("x") + lax.axis_index("y")
    where = jnp.stack([lax.axis_index("c"), me]).astype(jnp.int32)

    def pair_sums(names, full, got):
        return [_pair_sum(f"pair_sum_{nme}", a, b, where) for nme, a, b in zip(names, full, got)]

    def chip_sums(names, pair, got):
        return [_chip_sum(f"chip_sum_{nme}", own, b) for nme, (_, own), b in zip(names, pair, got)]

    ffn = ["w_gate", "w_up", "w_down"]
    full = [_columns_to_chips(gw_gate), _columns_to_chips(gw_up),
            gw_down.reshape(N_CHIPS, D_FF // N_CHIPS, D_MODEL)]
    (dx1, dx1b, gg_ffn), got = _matmul_norm_bwd("ffn_up_bwd", [(dgate, wgate), (dup, wup)], x1, dx2,
                                                g_ffn, tm, carry=_x_pair(full))
    pair = pair_sums(ffn, full, got)
    (dmix,) = _matmul("out_proj_bwd", dx1b, wout, [], [F32], lambda acc: (acc,), tm, D_MODEL,
                      transposed_w=True)
    gw_out = _matmul_tn("grad_w_out", mix, dx1b, 512, tmm)
    full = [gw_out.reshape(N_CHIPS, D_MODEL // N_CHIPS, D_MODEL)]
    (dzc, do, dd, gcw, gg_conv, gg_attn), got = _mixer_bwd(
        z, dmix, y_attn, cw, g_conv_out, g_attn_out, ones_bd, tm, carry=_x_pair(full))
    pair += pair_sums(["w_out"], full, got)
    early = ffn + ["w_out"]
    (dqn, dkn, dv), got = _attn_bwd(q, k, z, v_col, do, lse, dd,
                                    carry=_x_chips([p for p, _ in pair]))
    mine = chip_sums(early, pair, got)
    (dz, gg_q, gg_k), theirs = _qkv_bwd(z, dzc, dqn, dkn, dv, gq_t, gk_t, ones_bd, tm,
                                        carry=_x_share(mine))
    full = [_matmul_tn("grad_w_in", h1, dz, D_IN // N_CHIPS, tmm, by_chip=True)]
    grad_x, _, gg_mix = _matmul_norm_bwd("in_proj_bwd", [(dz, win)], xs, dx1, g_mix, tm)
    got = _exchange_alone("grad_pair_exchange_w_in", _x_pair(full))
    pair = pair_sums(["w_in"], full, got)
    got = _exchange_alone("grad_chip_exchange_w_in", _x_chips([pair[0][0]]))
    mine += chip_sums(["w_in"], pair, got)
    theirs = list(theirs) + list(_exchange_alone("grad_pair_share_w_in", _x_share(mine[-1:])))
    big = early + ["w_in"]

    small = _small_all_reduce({
        "g_mix": gg_mix, "g_ffn": gg_ffn, "g_conv_out": gg_conv, "g_attn_out": gg_attn,
        "g_q": gg_q, "g_k": gg_k, "loss": loss_sum, "conv_w": gcw})
    heads = D_ATTN // HEAD_DIM
    grads = {
        "g_mix": small[0:1, :], "g_ffn": small[1:2, :],
        "g_conv_out": small[2:3, 0:512], "g_attn_out": small[2:3, 512:1024],
        "g_q": small[3, 0:512].reshape(heads, HEAD_DIM).sum(axis=0)[None, :],
        "g_k": small[3, 512:1024].reshape(heads, HEAD_DIM).sum(axis=0)[None, :],
        "conv_w": lax.dynamic_slice(small[8:11, 0:512], (0, me * (D_CONV // N_CHIPS)),
                                    (3, D_CONV // N_CHIPS)),
    }
    halves = dict(zip(big, zip(mine, theirs)))
    loss = small[4, 0] * 0.5 * (1.0 / D_MODEL)

    weights = dict(g_mix=g_mix, w_in=w_in, conv_w=conv_w, g_q=g_q, g_k=g_k, g_conv_out=g_conv_out,
                   g_attn_out=g_attn_out, w_out=w_out, g_ffn=g_ffn, w_gate=w_gate, w_up=w_up,
                   w_down=w_down)
    moments_m = dict(g_mix=m_g_mix, w_in=m_w_in, conv_w=m_conv_w, g_q=m_g_q, g_k=m_g_k,
                     g_conv_out=m_g_conv_out, g_attn_out=m_g_attn_out, w_out=m_w_out, g_ffn=m_g_ffn,
                     w_gate=m_w_gate, w_up=m_w_up, w_down=m_w_down)
    moments_v = dict(g_mix=v_g_mix, w_in=v_w_in, conv_w=v_conv_w, g_q=v_g_q, g_k=v_g_k,
                     g_conv_out=v_g_conv_out, g_attn_out=v_g_attn_out, w_out=v_w_out, g_ffn=v_g_ffn,
                     w_gate=v_w_gate, w_up=v_w_up, w_down=v_w_down)
    names = list(weights)
    out_g, out_d, out_m, out_v = [], [], [], []
    for nme in names:
        wgt = weights[nme]
        shape2 = wgt.shape[-2:] if wgt.ndim == 3 else wgt.shape
        state = (wgt.reshape(shape2), moments_m[nme].reshape(shape2), moments_v[nme].reshape(shape2))
        if nme in halves:
            g2, dlt, nm, nv = _adamw_shard(f"adamw_{nme}", *state, *halves[nme], where)
        else:
            g2 = grads[nme].reshape(shape2)
            dlt, nm, nv = _adamw(f"adamw_{nme}", state[0], g2, state[1], state[2])
        out_g.append(g2.reshape(wgt.shape))
        out_d.append(dlt.reshape(wgt.shape))
        out_m.append(nm.reshape(wgt.shape))
        out_v.append(nv.reshape(wgt.shape))
    return (loss, grad_x[None], *out_g, *out_d, *out_m, *out_v)
```

```python
import functools
from typing import Any, Callable, NamedTuple, Sequence

import jax
import jax.numpy as jnp
from jax import lax
from jax.experimental import pallas as pl
from jax.experimental.pallas import tpu as pltpu

F32 = jnp.float32
BF16 = jnp.bfloat16
MESH = pl.DeviceIdType.MESH

D_MODEL = 1024
D_CONV = 512
D_ATTN = 512
HEAD_DIM = 64
D_FF = 2816
D_IN = 3 * D_CONV + 3 * D_ATTN
DILATIONS = (1, 4, 16)
BAND = 128
EPS = 1e-6
NEG = -1e30
N_CHIPS = 4

ADAM_LR = 0.001
ADAM_B1 = 0.9
ADAM_B2 = 0.999
ADAM_EPS = 1e-08
ADAM_WD = 0.01
ADAM_STEP = 10

V7X_VMEM_BYTES = 64 * 1024 * 1024
VMEM_LIMIT = V7X_VMEM_BYTES - 8 * 1024 * 1024
ANY = pl.BlockSpec(memory_space=pl.ANY)
VMEM_WHOLE = pl.BlockSpec(memory_space=pltpu.VMEM)


def _params(*sem):
    return pltpu.CompilerParams(dimension_semantics=sem, vmem_limit_bytes=VMEM_LIMIT)


def _sds(shape, dtype):
    return jax.ShapeDtypeStruct(shape, dtype)


def _resident(whole):
    return pl.Buffered(1) if whole else None


def _place():
    x, y, c = lax.axis_index("x"), lax.axis_index("y"), lax.axis_index("c")
    chips = [(1 - x, y), (x, 1 - y), (1 - x, 1 - y)]
    return x, y, c, 2 * x + y, chips, [2 * cx + cy for cx, cy in chips]


def _all_gather(shards):
    n = len(shards)

    def body(*refs):
        ins, outs, stage = refs[:n], refs[n:2 * n], refs[2 * n:3 * n]
        ssem, rsem, fsem, gsem, lsem, osem = refs[3 * n:]
        x, y, c, me, chips, cids = _place()
        sib = (x, y, 1 - c)

        def half(w, which):
            h = shards[w].shape[0] // 2
            return pl.ds(pl.multiple_of(which * h, 8), h)

        loads = [pltpu.make_async_copy(ins[w], stage[w], lsem.at[w]) for w in range(n)]
        local = [pltpu.make_async_copy(stage[w], outs[w].at[me], osem.at[w]) for w in range(n)]
        for cp in loads:
            cp.start()

        def chip_copy(w, j, src_slot):
            rows = half(w, c)
            return pltpu.make_async_remote_copy(
                src_ref=ins[w].at[rows], dst_ref=outs[w].at[src_slot, rows],
                send_sem=ssem.at[3 * w + j], recv_sem=rsem.at[3 * w + j],
                device_id=(*chips[j], c), device_id_type=MESH)

        def sib_copy(w, j, which):
            rows = half(w, which)
            return pltpu.make_async_remote_copy(
                src_ref=outs[w].at[cids[j], rows], dst_ref=outs[w].at[cids[j], rows],
                send_sem=fsem.at[3 * w + j], recv_sem=gsem.at[3 * w + j],
                device_id=sib, device_id_type=MESH)

        sends = [chip_copy(w, j, me) for w in range(n) for j in range(3)]
        for cp in sends:
            cp.start()
        for w in range(n):
            loads[w].wait()
            local[w].start()
        passed = []
        for w in range(n):
            for j in range(3):
                chip_copy(w, j, cids[j]).wait_recv()
                cp = sib_copy(w, j, c)
                cp.start()
                passed.append(cp)
        for w in range(n):
            for j in range(3):
                sib_copy(w, j, 1 - c).wait_recv()
        for cp in sends + passed:
            cp.wait_send()
        for cp in local:
            cp.wait()

    return pl.pallas_call(
        body, name="all_gather_weights",
        out_shape=[_sds((N_CHIPS,) + s.shape, s.dtype) for s in shards],
        in_specs=[ANY] * n, out_specs=[ANY] * n,
        scratch_shapes=[pltpu.VMEM(s.shape, s.dtype) for s in shards]
        + [pltpu.SemaphoreType.DMA((3 * n,))] * 4 + [pltpu.SemaphoreType.DMA((n,))] * 2,
        compiler_params=pltpu.CompilerParams(vmem_limit_bytes=VMEM_LIMIT),
    )(*shards)


class _Exchange(NamedTuple):
    srcs: Sequence[Any]
    lands: Sequence[Any]
    outs: Sequence[Any]
    n_sems: int
    copies: Callable


def _remote(src, dst, ssem, rsem, k, to):
    return pltpu.make_async_remote_copy(src_ref=src, dst_ref=dst, send_sem=ssem.at[k],
                                        recv_sem=rsem.at[k], device_id=to, device_id_type=MESH)


def _x_gather_chips(shards):
    def copies(srcs, lands, outs, ssem, rsem):
        _, _, c, me, chips, cids = _place()
        go, arrive = [], []
        for w, s in enumerate(shards):
            h = s.shape[0] // 2
            rows = pl.ds(pl.multiple_of(c * h, 8), h)
            for j in range(3):
                to = (*chips[j], c)
                go.append(_remote(srcs[w].at[rows], lands[w].at[me, rows], ssem, rsem, 3 * w + j, to))
                arrive.append(_remote(srcs[w].at[rows], lands[w].at[cids[j], rows], ssem, rsem,
                                      3 * w + j, to))
        return go, arrive

    lands = [jnp.broadcast_to(s[None], (N_CHIPS,) + s.shape) for s in shards]
    return _Exchange(shards, lands, [], 3 * len(shards), copies)


def _x_gather_sibling(gathered):
    def copies(srcs, lands, outs, ssem, rsem):
        x, y, c, _, _, cids = _place()
        go, arrive = [], []
        for w, g in enumerate(gathered):
            h = g.shape[1] // 2
            mine = pl.ds(pl.multiple_of(c * h, 8), h)
            theirs = pl.ds(pl.multiple_of((1 - c) * h, 8), h)
            for j in range(3):
                slab = lands[w].at[cids[j]]
                go.append(_remote(slab.at[mine], slab.at[mine], ssem, rsem, 3 * w + j, (x, y, 1 - c)))
                arrive.append(_remote(slab.at[theirs], slab.at[theirs], ssem, rsem, 3 * w + j,
                                      (x, y, 1 - c)))
        return go, arrive

    return _Exchange([], gathered, [], 3 * len(gathered), copies)


def _x_pair(grads):
    def copies(srcs, lands, outs, ssem, rsem):
        x, y, c, _, _, _ = _place()
        go = []
        for w, g in enumerate(grads):
            h = g.shape[1] // 2
            theirs = pl.ds(pl.multiple_of((1 - c) * h, 8), h)
            go.append(_remote(srcs[w].at[:, theirs, :], outs[w], ssem, rsem, w, (x, y, 1 - c)))
        return go, go

    outs = [_sds((N_CHIPS, g.shape[1] // 2, g.shape[2]), g.dtype) for g in grads]
    return _Exchange(grads, [], outs, len(grads), copies)


def _x_chips(parts):
    def copies(srcs, lands, outs, ssem, rsem):
        _, _, c, _, chips, cids = _place()
        go = [_remote(srcs[w].at[cids[j]], outs[w].at[j], ssem, rsem, 3 * w + j, (*chips[j], c))
              for w in range(len(parts)) for j in range(3)]
        return go, go

    outs = [_sds((3,) + p.shape[1:], p.dtype) for p in parts]
    return _Exchange(parts, [], outs, 3 * len(parts), copies)


def _x_share(halves):
    def copies(srcs, lands, outs, ssem, rsem):
        x, y, c, _, _, _ = _place()
        go = [_remote(srcs[w], outs[w], ssem, rsem, w, (x, y, 1 - c)) for w in range(len(halves))]
        return go, go

    return _Exchange(halves, [], [_sds(h.shape, h.dtype) for h in halves], len(halves), copies)


def _call(body, args, *, name, grid, in_specs, out_specs, out_shape, scratch_shapes=(),
          semantics=None, carry=None):
    single = not isinstance(out_shape, (list, tuple))
    out_shape = [out_shape] if single else list(out_shape)
    out_specs = [out_specs] if single else list(out_specs)
    if carry is None:
        res = pl.pallas_call(
            body, name=name, grid=grid, in_specs=list(in_specs), out_specs=out_specs,
            out_shape=out_shape, scratch_shapes=list(scratch_shapes),
            compiler_params=_params(*(semantics or ("arbitrary",) * len(grid))))(*args)
        return res[0] if single else res
    n_in, n_out, n_scr = len(args), len(out_shape), len(scratch_shapes)
    n_src, n_land, n_new = len(carry.srcs), len(carry.lands), len(carry.outs)

    def carrying(*refs):
        at = 0
        parts = []
        for n in (n_in, n_src, n_land, n_out, n_land, n_new, n_scr, 2):
            parts.append(refs[at:at + n])
            at += n
        ins, srcs, _, outs, lands, news, scratch, (ssem, rsem) = parts
        ids = [pl.program_id(a) for a in range(len(grid))]
        first = functools.reduce(jnp.logical_and, [i == 0 for i in ids])
        last = functools.reduce(jnp.logical_and, [i == g - 1 for i, g in zip(ids, grid)])
        go, arrive = carry.copies(srcs, lands, news, ssem, rsem)

        @pl.when(first)
        def _():
            for cp in go:
                cp.start()

        body(*ins, *outs, *scratch)

        @pl.when(last)
        def _():
            for cp in go:
                cp.wait_send()
            for cp in arrive:
                cp.wait_recv()

    res = pl.pallas_call(
        carrying, name=name, grid=grid,
        in_specs=list(in_specs) + [ANY] * (n_src + n_land),
        out_specs=out_specs + [ANY] * (n_land + n_new),
        out_shape=out_shape + [_sds(a.shape, a.dtype) for a in carry.lands] + list(carry.outs),
        input_output_aliases={n_in + n_src + i: n_out + i for i in range(n_land)},
        scratch_shapes=list(scratch_shapes) + [pltpu.SemaphoreType.DMA((carry.n_sems,))] * 2,
        compiler_params=_params(*(("arbitrary",) * len(grid))))(*args, *carry.srcs, *carry.lands)
    own = res[:n_out]
    return (own[0] if single else own), res[n_out:]


def _exchange_alone(name, exchange):
    def body(x_ref, o_ref):
        o_ref[...] = x_ref[...]

    blk = pl.BlockSpec((8, 128), lambda i: (0, 0))
    _, res = _call(body, [jnp.zeros((8, 128), F32)], name=name, grid=(1,), in_specs=[blk],
                   out_specs=blk, out_shape=_sds((8, 128), F32), carry=exchange)
    return res


def _row_block(r, want):
    return max(d for d in range(1, min(want, r) + 1) if r % d == 0 and (d % 8 == 0 or d == r))


def _pair_sum(name, full, got, where):
    _, r, n = full.shape
    h = r // 2
    tr = _row_block(h, 256)
    nb = h // tr

    def body(w_ref, a_ref, b_ref, o_ref, own_ref):
        total = a_ref[...] + b_ref[...]
        o_ref[...] = total.astype(BF16)

        @pl.when(pl.program_id(1) == w_ref[1])
        def _():
            own_ref[...] = total[0]

    blk = pl.BlockSpec((1, tr, n), lambda i, s, w: (s, i, 0))
    return pl.pallas_call(
        body, name=name, out_shape=[_sds(got.shape, BF16), _sds((h, n), F32)],
        grid_spec=pltpu.PrefetchScalarGridSpec(
            num_scalar_prefetch=1, grid=(nb, N_CHIPS),
            in_specs=[pl.BlockSpec((1, tr, n), lambda i, s, w: (s, w[0] * nb + i, 0)), blk],
            out_specs=[blk, pl.BlockSpec((tr, n), lambda i, s, w: (i, 0))]),
        compiler_params=_params("parallel", "arbitrary"),
    )(where, full, got)


def _chip_sum(name, own, got):
    h, n = own.shape
    tr = _row_block(h, 256)

    def body(a_ref, b0, b1, b2, o_ref):
        o_ref[...] = ((a_ref[...] + b0[0].astype(F32)) + b1[0].astype(F32)) + b2[0].astype(F32)

    def slot(j):
        return pl.BlockSpec((1, tr, n), lambda i: (j, i, 0))

    blk = pl.BlockSpec((tr, n), lambda i: (i, 0))
    return pl.pallas_call(
        body, name=name, grid=(h // tr,), out_shape=_sds((h, n), F32),
        in_specs=[blk, slot(0), slot(1), slot(2)], out_specs=blk,
        compiler_params=_params("parallel"),
    )(own, got, got, got)


SMALL_ROWS = 16
SMALL_LAYOUT = (
    ("g_mix", 0, 0, 1, 1024), ("g_ffn", 1, 0, 1, 1024), ("g_conv_out", 2, 0, 1, 512),
    ("g_attn_out", 2, 512, 1, 512), ("g_q", 3, 0, 1, 512), ("g_k", 3, 512, 1, 512),
    ("loss", 4, 0, 1, 128), ("conv_w", 8, 0, 8, 512))


def _small_all_reduce(parts):
    names = [s[0] for s in SMALL_LAYOUT]

    def body(*refs):
        ins = refs[:len(names)]
        out_ref, stage, buf, ssem, rsem = refs[len(names):]
        x, y, c, _, _, _ = _place()
        me = 4 * x + 2 * y + c
        stage[...] = jnp.zeros_like(stage)
        for ref, (_, r0, c0, nr, nc) in zip(ins, SMALL_LAYOUT):
            stage[r0:r0 + nr, c0:c0 + nc] = ref[0:nr, :]
        buf[me] = stage[...]
        peers = []
        for d in range(1, 8):
            px = 1 - x if d & 4 else x
            py = 1 - y if d & 2 else y
            pc = 1 - c if d & 1 else c
            peers.append(((px, py, pc), 4 * px + 2 * py + pc))
        sends = [pltpu.make_async_remote_copy(
            src_ref=stage, dst_ref=buf.at[me], send_sem=ssem.at[k], recv_sem=rsem.at[k],
            device_id=peer, device_id_type=MESH) for k, (peer, _) in enumerate(peers)]
        for cp in sends:
            cp.start()
        for k, (peer, pid) in enumerate(peers):
            pltpu.make_async_remote_copy(
                src_ref=stage, dst_ref=buf.at[pid], send_sem=ssem.at[k], recv_sem=rsem.at[k],
                device_id=peer, device_id_type=MESH).wait_recv()
        for cp in sends:
            cp.wait_send()
        acc = buf[0]
        for k in range(1, 8):
            acc = acc + buf[k]
        out_ref[...] = acc

    return pl.pallas_call(
        body, name="small_all_reduce", out_shape=_sds((SMALL_ROWS, 1024), F32),
        in_specs=[VMEM_WHOLE] * len(names), out_specs=VMEM_WHOLE,
        scratch_shapes=[pltpu.VMEM((SMALL_ROWS, 1024), F32), pltpu.VMEM((8, SMALL_ROWS, 1024), F32),
                        pltpu.SemaphoreType.DMA((7,)), pltpu.SemaphoreType.DMA((7,))],
    )(*[parts[k] for k in names])


def _dot(a, b):
    return jnp.dot(a, b, preferred_element_type=F32)


def _dot_nt(a, b):
    return lax.dot_general(a, b, (((1,), (1,)), ((), ())), preferred_element_type=F32)


def _dot_tn(a, b):
    return lax.dot_general(a, b, (((0,), (0,)), ((), ())), preferred_element_type=F32)


def _sigmoid(v):
    return 1.0 / (1.0 + jnp.exp(-v))


def _rms_scale(v):
    return lax.rsqrt(jnp.mean(v * v, axis=-1, keepdims=True) + EPS)


def _rms_bwd(v, r, g, dy):
    vh = v * r
    dh = dy * g
    return r * (dh - vh * jnp.mean(dh * vh, axis=-1, keepdims=True)), vh


def _head_sum(a, ones_bd):
    hi = a.astype(BF16)
    lo = (a - hi.astype(F32)).astype(BF16)
    return _dot(hi, ones_bd) + _dot(lo, ones_bd)


def _head_rms_scale(v, ones_bd):
    return lax.rsqrt(_head_sum(v * v, ones_bd) * (1.0 / HEAD_DIM) + EPS)


MXU_COLUMNS = 256


def _column_chunks(n):
    width = MXU_COLUMNS if n % MXU_COLUMNS == 0 else n
    return [slice(c, c + width) for c in range(0, n, width)]


def _norm_matmul(name, x, g, ws, tm, tn, swiglu, out_dtype=F32):
    t, d = x.shape
    n = ws[0].shape[1]
    nw = len(ws)

    def body(x_ref, g_ref, *refs):
        w_refs, h_ref, o_refs = refs[:nw], refs[nw], refs[nw + 1:2 * nw + 1]
        hs = refs[-1]

        @pl.when(pl.program_id(1) == 0)
        def _():
            xv = x_ref[...]
            h = (xv * _rms_scale(xv) * g_ref[...]).astype(BF16)
            hs[...] = h
            h_ref[...] = h

        h = hs[...]
        for cols in _column_chunks(tn):
            outs = [_dot(h, w[:, cols]) for w in w_refs]
            for o_ref, o in zip(o_refs, outs):
                o_ref[:, cols] = o.astype(out_dtype)
            if swiglu:
                refs[2 * nw + 1][:, cols] = (outs[0] * _sigmoid(outs[0]) * outs[1]).astype(BF16)

    row = pl.BlockSpec((tm, d), lambda i, j: (i, 0))
    col = pl.BlockSpec((tm, tn), lambda i, j: (i, j))
    out_shape = [_sds((t, d), BF16)] + [_sds((t, n), out_dtype)] * nw
    out_specs = [row] + [col] * nw
    if swiglu:
        out_shape.append(_sds((t, n), BF16))
        out_specs.append(col)
    return pl.pallas_call(
        body, name=name, grid=(t // tm, n // tn), out_shape=out_shape,
        in_specs=[row, pl.BlockSpec((1, d), lambda i, j: (0, 0))]
        + [pl.BlockSpec((d, tn), lambda i, j: (0, j), pipeline_mode=_resident(tn == n))] * nw,
        out_specs=out_specs, scratch_shapes=[pltpu.VMEM((tm, d), BF16)],
        compiler_params=_params("parallel", "arbitrary"),
    )(x, g, *ws)


def _matmul(name, a, w, extras, out_dtypes, epilogue, tm, tn, transposed_w=False, loss=False):
    t, k = a.shape
    n = w.shape[0] if transposed_w else w.shape[1]
    ne, no = len(extras), len(out_dtypes)

    def body(a_ref, w_ref, *refs):
        e_refs, o_refs = refs[:ne], refs[ne:]
        a = a_ref[...]
        total = 0.0
        for cols in _column_chunks(tn):
            acc = _dot_nt(a, w_ref[cols, :]) if transposed_w else _dot(a, w_ref[:, cols])
            res = epilogue(acc, *[e[:, cols] for e in e_refs])
            for o_ref, r in zip(o_refs[:no], res[:no]):
                o_ref[:, cols] = r.astype(o_ref.dtype)
            if loss:
                total = total + res[no]
        if loss:
            first = jnp.logical_and(pl.program_id(0) == 0, pl.program_id(1) == 0)

            @pl.when(first)
            def _():
                o_refs[no][...] = jnp.zeros_like(o_refs[no])

            o_refs[no][...] += total

    col = pl.BlockSpec((tm, tn), lambda i, j: (i, j))
    w_spec = (pl.BlockSpec((tn, k), lambda i, j: (j, 0), pipeline_mode=_resident(tn == n))
              if transposed_w
              else pl.BlockSpec((k, tn), lambda i, j: (0, j), pipeline_mode=_resident(tn == n)))
    out_shape = [_sds((t, n), dt) for dt in out_dtypes]
    out_specs = [col] * no
    if loss:
        out_shape.append(_sds((8, 128), F32))
        out_specs.append(pl.BlockSpec((8, 128), lambda i, j: (0, 0)))
    return pl.pallas_call(
        body, name=name, grid=(t // tm, n // tn), out_shape=out_shape,
        in_specs=[pl.BlockSpec((tm, k), lambda i, j: (i, 0)), w_spec] + [col] * ne,
        out_specs=out_specs,
        compiler_params=_params(*(("arbitrary", "arbitrary") if loss else ("parallel", "parallel"))),
    )(a, w, *extras)


def _matmul_norm_bwd(name, pairs, x, dres, g, tm, carry=None):
    t, d = x.shape
    npairs = len(pairs)

    def body(*refs):
        a_refs, w_refs = refs[:npairs], refs[npairs:2 * npairs]
        x_ref, r_ref, g_ref, dx_ref, dxb_ref, dg_ref = refs[2 * npairs:]
        dy = _dot_nt(a_refs[0][...], w_refs[0][...])
        for a_ref, w_ref in zip(a_refs[1:], w_refs[1:]):
            dy = dy + _dot_nt(a_ref[...], w_ref[...])
        xv = x_ref[...]
        dx, xh = _rms_bwd(xv, _rms_scale(xv), g_ref[...], dy)
        dx = dx + r_ref[...]
        dx_ref[...] = dx
        dxb_ref[...] = dx.astype(BF16)

        @pl.when(pl.program_id(0) == 0)
        def _():
            dg_ref[...] = jnp.zeros_like(dg_ref)

        dg_ref[...] += jnp.sum(dy * xh, axis=0, keepdims=True)

    row = pl.BlockSpec((tm, d), lambda i: (i, 0))
    vec = pl.BlockSpec((1, d), lambda i: (0, 0))
    return _call(
        body, [a for a, _ in pairs] + [w for _, w in pairs] + [x, dres, g], name=name,
        grid=(t // tm,), out_shape=[_sds((t, d), F32), _sds((t, d), BF16), _sds((1, d), F32)],
        in_specs=[pl.BlockSpec((tm, a.shape[1]), lambda i: (i, 0)) for a, _ in pairs]
        + [pl.BlockSpec(w.shape, lambda i: (0, 0), pipeline_mode=pl.Buffered(1)) for _, w in pairs]
        + [row, row, vec],
        out_specs=[row, row, vec], carry=carry)


def _matmul_tn(name, a, g, tn, tk, by_chip=False):
    t, ka = a.shape
    n = g.shape[1]

    def body(a_ref, g_ref, o_ref):
        @pl.when(pl.program_id(1) == 0)
        def _():
            o_ref[...] = jnp.zeros_like(o_ref)

        acc = _dot_tn(a_ref[...], g_ref[...])
        o_ref[...] += acc[None] if by_chip else acc

    return pl.pallas_call(
        body, name=name, grid=(n // tn, t // tk),
        out_shape=_sds((n // tn, ka, tn) if by_chip else (ka, n), F32),
        in_specs=[pl.BlockSpec((tk, ka), lambda j, s: (s, 0)),
                  pl.BlockSpec((tk, tn), lambda j, s: (s, j))],
        out_specs=(pl.BlockSpec((1, ka, tn), lambda j, s: (j, 0, 0)) if by_chip
                   else pl.BlockSpec((ka, tn), lambda j, s: (0, j))),
        compiler_params=_params("parallel", "arbitrary"),
    )(a, g)


def _elementwise(name, fn, ins, out_dtypes, tr):
    r, n = ins[0].shape
    tr = _row_block(r, tr)
    ni = len(ins)

    def body(*refs):
        res = fn(*[ref[...] for ref in refs[:ni]])
        for o_ref, v in zip(refs[ni:], res):
            o_ref[...] = v.astype(o_ref.dtype)

    blk = pl.BlockSpec((tr, n), lambda i: (i, 0))
    return pl.pallas_call(
        body, name=name, grid=(r // tr,), out_shape=[_sds((r, n), dt) for dt in out_dtypes],
        in_specs=[blk] * ni, out_specs=[blk] * len(out_dtypes),
        compiler_params=_params("parallel"),
    )(*ins)


def _adamw_update(w, g, m, v):
    m = ADAM_B1 * m + (1.0 - ADAM_B1) * g
    v = ADAM_B2 * v + (1.0 - ADAM_B2) * (g * g)
    m_hat = m / (1.0 - ADAM_B1 ** ADAM_STEP)
    v_hat = v / (1.0 - ADAM_B2 ** ADAM_STEP)
    return -ADAM_LR * (m_hat / (jnp.sqrt(v_hat) + ADAM_EPS) + ADAM_WD * w), m, v


def _adamw(name, w, g, m, v):
    return _elementwise(name, _adamw_update, [w, g, m, v], [F32] * 3, 256)


def _adamw_shard(name, w, m, v, mine, theirs, where):
    r, n = w.shape
    h = r // 2
    tr = _row_block(h, 256)
    nb = h // tr

    def body(w_ref, p_ref, m_ref, v_ref, a_ref, b_ref, g_ref, d_ref, nm_ref, nv_ref):
        g = jnp.where(pl.program_id(0) == w_ref[0], a_ref[...], b_ref[...])
        g_ref[...] = g
        d_ref[...], nm_ref[...], nv_ref[...] = _adamw_update(p_ref[...], g, m_ref[...], v_ref[...])

    whole = pl.BlockSpec((tr, n), lambda s, i, c: (s * nb + i, 0))
    half = pl.BlockSpec((tr, n), lambda s, i, c: (i, 0))
    return pl.pallas_call(
        body, name=name, out_shape=[_sds((r, n), F32)] * 4,
        grid_spec=pltpu.PrefetchScalarGridSpec(
            num_scalar_prefetch=1, grid=(2, nb), in_specs=[whole] * 3 + [half] * 2,
            out_specs=[whole] * 4),
        compiler_params=_params("parallel", "parallel"),
    )(where, w, m, v, mine, theirs)


def _qkv_prepare(z, gq, gk, ones_bd, tm):
    t = z.shape[0]

    def body(zq_ref, zk_ref, gq_ref, gk_ref, bd_ref, q_ref, k_ref):
        bd = bd_ref[...]
        q = zq_ref[...]
        k = zk_ref[...]
        q_ref[...] = (q * _head_rms_scale(q, bd) * gq_ref[...]) * HEAD_DIM ** -0.5
        k_ref[...] = k * _head_rms_scale(k, bd) * gk_ref[...]

    vec = pl.BlockSpec((1, 512), lambda i: (0, 0))
    out = pl.BlockSpec((tm, 512), lambda i: (i, 0))
    return pl.pallas_call(
        body, name="qkv_prepare", grid=(t // tm,), out_shape=[_sds((t, 512), F32)] * 2,
        in_specs=[pl.BlockSpec((tm, 512), lambda i: (i, 3)), pl.BlockSpec((tm, 512), lambda i: (i, 4)),
                  vec, vec, pl.BlockSpec((512, 512), lambda i: (0, 0))],
        out_specs=[out] * 2, compiler_params=_params("parallel"),
    )(z, z, gq, gk, ones_bd)


TOK = 2048
UNITS = TOK // BAND


def _stack_masks():
    row = lax.broadcasted_iota(jnp.int32, (2 * BAND, 2 * BAND), 0) & (BAND - 1)
    col = lax.broadcasted_iota(jnp.int32, (2 * BAND, 2 * BAND), 1)
    lane = lax.broadcasted_iota(jnp.int32, (BAND, BAND), 1)
    head0 = lane < HEAD_DIM
    ones = [jnp.where(head0, 1.0, 0.0).astype(BF16), jnp.where(head0, 0.0, 1.0).astype(BF16)]
    return col - row, col, head0, ones


def _split3(x):
    hi = x.astype(BF16).astype(F32)
    mid = (x - hi).astype(BF16).astype(F32)
    return hi, mid, x - hi - mid


def _gather(srcs, dst, d):
    per = TOK // d
    at = 0
    for r in range(d):
        for src in srcs:
            rows = src[pl.ds(r, per, stride=d), :] if d > 1 else src[...]
            dst[pl.ds(at, per), :] = rows.astype(dst.dtype)
            at += per


def _scatter_add(out_ref, src, d, per_src, offset, first):
    per = TOK // d
    if d == 1:
        val = src[pl.ds(offset, per), :]
        out_ref[...] = val if first else out_ref[...] + val
        return
    for r in range(d):
        val = src[pl.ds(r * per_src + offset, per), :]
        idx = pl.ds(r, per, stride=d)
        out_ref[idx, :] = val if first else out_ref[idx, :] + val


def _attn_fwd(q, k, v, v_col, carry=None):
    t = q.shape[0]
    nblk = t // TOK

    def body(q_ref, kp_ref, k_ref, vp_ref, v_ref, y_ref, l_ref, qs, ks, vs, ob, lb, on, ln):
        i = pl.program_id(1)
        diff, col, head0, hm = _stack_masks()
        band_ok = jnp.logical_and(diff >= 0, diff <= BAND)
        for g, d in enumerate(DILATIONS):
            per = TOK // d
            nb = per // BAND
            _gather([q_ref], qs, d)
            _gather([kp_ref, k_ref], ks, d)
            _gather([vp_ref, v_ref], vs, d)

            def unit(u, carry):
                r, b = u // nb, u % nb
                qu = qs[pl.ds(pl.multiple_of(u * BAND, BAND), BAND), :]
                start = pl.multiple_of(r * 2 * per + per + (b - 1) * BAND, BAND)
                kw = ks[pl.ds(start, 2 * BAND), :]
                vw = vs[pl.ds(start, 2 * BAND), :]
                lo = jnp.where(jnp.logical_and(i == 0, b == 0), BAND, 0)
                s = _dot_nt(jnp.concatenate([qu * hm[0], qu * hm[1]], axis=0), kw)
                s = jnp.where(jnp.logical_and(band_ok, col >= lo), s, NEG)
                mx = jnp.max(s, axis=-1, keepdims=True)
                e = jnp.exp(s - mx)
                den = jnp.sum(e, axis=-1, keepdims=True)
                o2 = _dot(e.astype(BF16), vw) / den
                l2 = jnp.broadcast_to(mx + jnp.log(den), (2 * BAND, BAND))
                rows = pl.ds(pl.multiple_of(u * BAND, BAND), BAND)
                ob[rows, :] = jnp.where(head0, o2[:BAND], o2[BAND:])
                lb[rows, :] = jnp.where(head0, l2[:BAND], l2[BAND:])
                return carry

            lax.fori_loop(0, UNITS, unit, 0, unroll=8)
            _scatter_add(on.at[g], ob, d, per, 0, True)
            _scatter_add(ln.at[g], lb, d, per, 0, True)
        ls = [ln[0], ln[1], ln[2]]
        mx = jnp.maximum(jnp.maximum(ls[0], ls[1]), ls[2])
        es = [jnp.exp(l - mx) for l in ls]
        tot = es[0] + es[1] + es[2]
        y_ref[...] = (es[0] * on[0] + es[1] * on[1] + es[2] * on[2]) / tot
        l_ref[...] = mx + jnp.log(tot)

    main = pl.BlockSpec((TOK, BAND), lambda j, i: (i, j))
    prev = pl.BlockSpec((TOK, BAND), lambda j, i: (jnp.maximum(i - 1, 0), j))
    vmain = pl.BlockSpec((TOK, BAND), lambda j, i: (i, j + v_col))
    vprev = pl.BlockSpec((TOK, BAND), lambda j, i: (jnp.maximum(i - 1, 0), j + v_col))
    return _call(
        body, [q, k, k, v, v], name="attn_fwd", grid=(D_ATTN // BAND, nblk),
        out_shape=[_sds((t, D_ATTN), F32)] * 2,
        in_specs=[main, prev, main, vprev, vmain], out_specs=[main, main],
        scratch_shapes=[pltpu.VMEM((TOK, BAND), BF16), pltpu.VMEM((2 * TOK, BAND), BF16),
                        pltpu.VMEM((2 * TOK, BAND), BF16), pltpu.VMEM((TOK, BAND), F32),
                        pltpu.VMEM((TOK, BAND), F32), pltpu.VMEM((3, TOK, BAND), F32),
                        pltpu.VMEM((3, TOK, BAND), F32)],
        semantics=("parallel", "parallel"), carry=carry)


def _attn_bwd(q, k, v, v_col, do, lse, dd, carry=None):
    t = q.shape[0]
    nblk = t // TOK
    offs = [sum(DILATIONS[:g]) * BAND for g in range(len(DILATIONS))]

    def body(q_ref, kp_ref, k_ref, vp_ref, v_ref, do_ref, l_ref, d_ref, dq_ref, dk_ref, dv_ref,
             qs, dos, ks, vs, lsc, dsc, dqb, dkb, dvb, ckb, cvb):
        step = pl.program_id(1)
        i = nblk - 1 - step
        key = lax.broadcasted_iota(jnp.int32, (2 * BAND, 2 * BAND), 0)
        qry = lax.broadcasted_iota(jnp.int32, (2 * BAND, 2 * BAND), 1) & (BAND - 1)
        off = key - qry
        band_ok = jnp.logical_and(off >= 0, off <= BAND)
        lane = lax.broadcasted_iota(jnp.int32, (BAND, BAND), 1)
        head0 = lane < HEAD_DIM
        hm = [jnp.where(head0, 1.0, 0.0).astype(BF16), jnp.where(head0, 0.0, 1.0).astype(BF16)]
        piece = lane & (HEAD_DIM - 1)
        lane2 = lax.broadcasted_iota(jnp.int32, (2 * BAND, BAND), 1) & (HEAD_DIM - 1)
        ones = jnp.where(lane2 < 3, 1.0, 0.0).astype(BF16)

        def pieces(x):
            hi, mid, lo = _split3(-x)
            a = jnp.where(piece == 0, hi, jnp.where(piece == 1, mid, jnp.where(piece == 2, lo, 0.0)))
            return a.astype(BF16)

        for g, d in enumerate(DILATIONS):
            per = TOK // d
            nb = per // BAND
            pad = per + BAND
            _gather([q_ref], qs, d)
            _gather([do_ref], dos, d)
            _gather([l_ref], lsc, d)
            _gather([d_ref], dsc, d)
            _gather([kp_ref, k_ref], ks, d)
            _gather([vp_ref, v_ref], vs, d)
            dkb[...] = jnp.zeros_like(dkb)
            dvb[...] = jnp.zeros_like(dvb)

            def unit(u, c_):
                r, b = u // nb, u % nb
                rows = pl.ds(pl.multiple_of(u * BAND, BAND), BAND)
                qu, dou = qs[rows, :], dos[rows, :]
                la, da = pieces(lsc[rows, :]), pieces(dsc[rows, :])
                q2 = jnp.concatenate([qu * hm[0], qu * hm[1]], axis=0)
                do2 = jnp.concatenate([dou * hm[0], dou * hm[1]], axis=0)
                l2 = jnp.concatenate([la * hm[0], la * hm[1]], axis=0)
                d2 = jnp.concatenate([da * hm[0], da * hm[1]], axis=0)
                start = pl.multiple_of(r * 2 * per + per + (b - 1) * BAND, BAND)
                kw = ks[pl.ds(start, 2 * BAND), :]
                vw = vs[pl.ds(start, 2 * BAND), :]
                lo = jnp.where(jnp.logical_and(i == 0, b == 0), BAND, 0)
                ok = jnp.logical_and(band_ok, key >= lo)
                st = _dot_nt(jnp.concatenate([kw, ones], axis=1), jnp.concatenate([q2, l2], axis=1))
                dpt = _dot_nt(jnp.concatenate([vw, ones], axis=1), jnp.concatenate([do2, d2], axis=1))
                pt = jnp.where(ok, jnp.exp(st), 0.0)
                dst = (pt * dpt).astype(BF16)
                acc = pl.ds(pl.multiple_of(r * pad + b * BAND, BAND), 2 * BAND)
                dkb[acc, :] += _dot(dst, q2)
                dvb[acc, :] += _dot(pt.astype(BF16), do2)
                dq2 = _dot_tn(dst, kw)
                dqb[rows, :] = jnp.where(head0, dq2[:BAND], dq2[BAND:])
                return c_

            lax.fori_loop(0, UNITS, unit, 0, unroll=8)

            for r in range(d):
                last = pl.ds(r * pad + per, BAND)
                kept = pl.ds(offs[g] + r * BAND, BAND)

                @pl.when(step > 0)
                def _():
                    dkb[last, :] += ckb[kept, :]
                    dvb[last, :] += cvb[kept, :]

                ckb[kept, :] = dkb[pl.ds(r * pad, BAND), :]
                cvb[kept, :] = dvb[pl.ds(r * pad, BAND), :]
            _scatter_add(dq_ref, dqb, d, per, 0, g == 0)
            _scatter_add(dk_ref, dkb, d, pad, BAND, g == 0)
            _scatter_add(dv_ref, dvb, d, pad, BAND, g == 0)

    main = pl.BlockSpec((TOK, BAND), lambda j, s: (nblk - 1 - s, j))
    prev = pl.BlockSpec((TOK, BAND), lambda j, s: (jnp.maximum(nblk - 2 - s, 0), j))
    vmain = pl.BlockSpec((TOK, BAND), lambda j, s: (nblk - 1 - s, j + v_col))
    vprev = pl.BlockSpec((TOK, BAND), lambda j, s: (jnp.maximum(nblk - 2 - s, 0), j + v_col))
    acc_rows = max(d * (TOK // d + BAND) for d in DILATIONS)
    kept_rows = sum(DILATIONS) * BAND
    return _call(
        body, [q, k, k, v, v, do, lse, dd], name="attn_bwd",
        grid=(D_ATTN // BAND, nblk), out_shape=[_sds((t, D_ATTN), F32)] * 3,
        in_specs=[main, prev, main, vprev, vmain, main, main, main], out_specs=[main] * 3,
        scratch_shapes=[pltpu.VMEM((TOK, BAND), BF16)] * 2 + [pltpu.VMEM((2 * TOK, BAND), BF16)] * 2
        + [pltpu.VMEM((TOK, BAND), F32)] * 3 + [pltpu.VMEM((acc_rows, BAND), F32)] * 2
        + [pltpu.VMEM((kept_rows, BAND), F32)] * 2,
        semantics=("parallel", "arbitrary"), carry=carry)


def _halo_rows(tm, t):
    per = tm // 8
    prev = lambda i: (jnp.maximum(i * per - 1, 0), 0)
    nxt = lambda i: (jnp.minimum((i + 1) * per, t // 8 - 1), 0)
    return prev, nxt


def _mixer_out(z, cw, y_attn, g_conv, g_attn, tm, carry=None):
    t = z.shape[0]
    prev, _ = _halo_rows(tm, t)

    def body(z_ref, zp_ref, cw_ref, y_ref, gc_ref, ga_ref, mix_ref):
        i = pl.program_id(0)
        keep = jnp.where(i > 0, 1.0, 0.0)
        cu = jnp.concatenate([zp_ref[:, 0:512] * zp_ref[:, 1024:1536] * keep,
                              z_ref[:, 0:512] * z_ref[:, 1024:1536]], axis=0)
        c = (cw_ref[0:1, :] * pltpu.roll(cu, 2, 0) + cw_ref[1:2, :] * pltpu.roll(cu, 1, 0)
             + cw_ref[2:3, :] * cu)[8:, :]
        yc = z_ref[:, 512:1024] * c
        mix_ref[:, 0:512] = (yc * _rms_scale(yc) * gc_ref[...]).astype(BF16)
        ya = y_ref[...]
        mix_ref[:, 512:1024] = (ya * _rms_scale(ya) * ga_ref[...]).astype(BF16)

    blk = pl.BlockSpec((tm, 512), lambda i: (i, 0))
    vec = pl.BlockSpec((1, 512), lambda i: (0, 0))
    return _call(
        body, [z, z, cw, y_attn, g_conv, g_attn], name="mixer_out", grid=(t // tm,),
        out_shape=_sds((t, 1024), BF16),
        in_specs=[pl.BlockSpec((tm, 1536), lambda i: (i, 0)), pl.BlockSpec((8, 1536), prev),
                  pl.BlockSpec((8, 512), lambda i: (0, 0)), blk, vec, vec],
        out_specs=pl.BlockSpec((tm, 1024), lambda i: (i, 0)),
        semantics=("parallel",), carry=carry)


def _mixer_bwd(z, dmix, y_attn, cw, g_conv, g_attn, ones_bd, tm, carry=None):
    t = z.shape[0]
    nblk = t // tm
    prev, nxt = _halo_rows(tm, t)
    e = tm + 16

    def body(z_ref, zp_ref, zn_ref, dm_ref, dmn_ref, y_ref, cw_ref, gc_ref, ga_ref, bd_ref,
             dz_ref, do_ref, dd_ref, dcw_ref, dgc_ref, dga_ref):
        i = pl.program_id(0)
        rows = lax.broadcasted_iota(jnp.int32, (e, 1), 0)
        lo = jnp.where(i > 0, 0, 8)
        hi = jnp.where(i < nblk - 1, e, tm + 8)
        ze = jnp.concatenate([zp_ref[...], z_ref[...], zn_ref[...]], axis=0)
        u, gb, gcv = ze[:, 0:512], ze[:, 512:1024], ze[:, 1024:1536]
        w0, w1, w2 = cw_ref[0:1, :], cw_ref[1:2, :], cw_ref[2:3, :]
        cu = jnp.where(rows >= lo, gcv * u, 0.0)
        cu1, cu2 = pltpu.roll(cu, 1, 0), pltpu.roll(cu, 2, 0)
        c = w0 * cu2 + w1 * cu1 + w2 * cu
        yc = gb * c
        dma = jnp.concatenate([jnp.zeros((8, 512), F32), dm_ref[:, 0:512], dmn_ref[...]], axis=0)
        dyc, ych = _rms_bwd(yc, _rms_scale(yc), gc_ref[...], dma)
        dc = jnp.where(jnp.logical_and(rows >= 8, rows < hi), dyc * gb, 0.0)
        dcu = w0 * pltpu.roll(dc, e - 2, 0) + w1 * pltpu.roll(dc, e - 1, 0) + w2 * dc
        mid = slice(8, 8 + tm)
        dz_ref[:, 0:512] = (dcu * gcv)[mid, :].astype(BF16)
        dz_ref[:, 512:1024] = (dyc * c)[mid, :].astype(BF16)
        dz_ref[:, 1024:1536] = (dcu * u)[mid, :].astype(BF16)

        ya = y_ref[...]
        dmb = dm_ref[:, 512:1024]
        dya, yah = _rms_bwd(ya, _rms_scale(ya), ga_ref[...], dmb)
        do_ref[...] = dya
        dd_ref[...] = _head_sum(dya * ya, bd_ref[...])

        @pl.when(i == 0)
        def _():
            dcw_ref[...] = jnp.zeros_like(dcw_ref)
            dgc_ref[...] = jnp.zeros_like(dgc_ref)
            dga_ref[...] = jnp.zeros_like(dga_ref)

        dcm = jnp.where(rows < tm + 8, dc, 0.0)
        dcw_ref[0:1, :] += jnp.sum(dcm * cu2, axis=0, keepdims=True)
        dcw_ref[1:2, :] += jnp.sum(dcm * cu1, axis=0, keepdims=True)
        dcw_ref[2:3, :] += jnp.sum(dcm * cu, axis=0, keepdims=True)
        dgc_ref[...] += jnp.sum((dma * ych)[mid, :], axis=0, keepdims=True)
        dga_ref[...] += jnp.sum(dmb * yah, axis=0, keepdims=True)

    blk = pl.BlockSpec((tm, 512), lambda i: (i, 0))
    vec = pl.BlockSpec((1, 512), lambda i: (0, 0))
    cwb = pl.BlockSpec((8, 512), lambda i: (0, 0))
    return _call(
        body, [z, z, z, dmix, dmix, y_attn, cw, g_conv, g_attn, ones_bd], name="mixer_bwd",
        grid=(nblk,),
        out_shape=[_sds((t, 1536), BF16), _sds((t, 512), F32), _sds((t, 512), F32),
                   _sds((8, 512), F32), _sds((1, 512), F32), _sds((1, 512), F32)],
        in_specs=[pl.BlockSpec((tm, 1536), lambda i: (i, 0)), pl.BlockSpec((8, 1536), prev),
                  pl.BlockSpec((8, 1536), nxt), pl.BlockSpec((tm, 1024), lambda i: (i, 0)),
                  pl.BlockSpec((8, 512), nxt), blk, cwb, vec, vec,
                  pl.BlockSpec((512, 512), lambda i: (0, 0))],
        out_specs=[pl.BlockSpec((tm, 1536), lambda i: (i, 0)), blk, blk, cwb, vec, vec],
        carry=carry)


def _qkv_bwd(z, dzc, dqn, dkn, dv, gq, gk, ones_bd, tm, carry=None):
    t = z.shape[0]

    def body(zq_ref, zk_ref, dzc_ref, dqn_ref, dkn_ref, dv_ref, gq_ref, gk_ref, bd_ref,
             dz_ref, dgq_ref, dgk_ref):
        bd = bd_ref[...]

        @pl.when(pl.program_id(0) == 0)
        def _():
            dgq_ref[...] = jnp.zeros_like(dgq_ref)
            dgk_ref[...] = jnp.zeros_like(dgk_ref)

        def back(v, dn, g, scale):
            r = _head_rms_scale(v, bd)
            vh = v * r
            dh = dn * (g * scale)
            dv = r * (dh - vh * (_head_sum(dh * vh, bd) * (1.0 / HEAD_DIM)))
            return dv, jnp.sum(dn * scale * vh, axis=0, keepdims=True)

        dq, dgq = back(zq_ref[...], dqn_ref[...], gq_ref[...], HEAD_DIM ** -0.5)
        dk, dgk = back(zk_ref[...], dkn_ref[...], gk_ref[...], 1.0)
        dgq_ref[...] += dgq
        dgk_ref[...] += dgk
        dz_ref[:, 0:1536] = dzc_ref[...]
        dz_ref[:, 1536:2048] = dq.astype(BF16)
        dz_ref[:, 2048:2560] = dk.astype(BF16)
        dz_ref[:, 2560:3072] = dv_ref[...].astype(BF16)

    blk = pl.BlockSpec((tm, 512), lambda i: (i, 0))
    vec = pl.BlockSpec((1, 512), lambda i: (0, 0))
    return _call(
        body, [z, z, dzc, dqn, dkn, dv, gq, gk, ones_bd], name="qkv_bwd", grid=(t // tm,),
        out_shape=[_sds((t, D_IN), BF16), _sds((1, 512), F32), _sds((1, 512), F32)],
        in_specs=[pl.BlockSpec((tm, 512), lambda i: (i, 3)), pl.BlockSpec((tm, 512), lambda i: (i, 4)),
                  pl.BlockSpec((tm, 1536), lambda i: (i, 0))] + [blk] * 3
        + [vec, vec, pl.BlockSpec((512, 512), lambda i: (0, 0))],
        out_specs=[pl.BlockSpec((tm, D_IN), lambda i: (i, 0)), vec, vec],
        carry=carry)


def _columns_from_chips(g):
    return g.transpose(1, 0, 2).reshape(g.shape[1], N_CHIPS * g.shape[2])


def _columns_to_chips(w):
    k, n4 = w.shape
    return w.reshape(k, N_CHIPS, n4 // N_CHIPS).transpose(1, 0, 2)


def kernel(x, g_mix, w_in, conv_w, g_q, g_k, g_conv_out, g_attn_out, w_out, g_ffn, w_gate, w_up, w_down, loss_target, m_g_mix, m_w_in, m_conv_w, m_g_q, m_g_k, m_g_conv_out, m_g_attn_out, m_w_out, m_g_ffn, m_w_gate, m_w_up, m_w_down, v_g_mix, v_w_in, v_conv_w, v_g_q, v_g_k, v_g_conv_out, v_g_attn_out, v_w_out, v_g_ffn, v_w_gate, v_w_up, v_w_down):
    t = x.shape[1]
    xs = x[0]
    target = loss_target[0]
    tm = min(512, t)
    tmm = min(1024, t)

    cw_pad = jnp.pad(conv_w[0], ((0, 13), (0, 0)))
    gathered = _all_gather([w_in[0].astype(BF16), cw_pad])
    win = _columns_from_chips(gathered[0])
    cw = jnp.pad(gathered[1][:, 0:3, :].transpose(1, 0, 2).reshape(3, D_CONV), ((0, 5), (0, 0)))
    later = [w_out[0].astype(BF16), w_gate[0].astype(BF16), w_up[0].astype(BF16),
             w_down[0].astype(BF16)]

    head_id = jnp.arange(D_ATTN) // HEAD_DIM
    ones_bd = (head_id[:, None] == head_id[None, :]).astype(BF16)
    gq_t = jnp.tile(g_q, (1, D_ATTN // HEAD_DIM))
    gk_t = jnp.tile(g_k, (1, D_ATTN // HEAD_DIM))

    h1, z = _norm_matmul("in_proj", xs, g_mix, [win], tm, D_IN, False)
    q, k = _qkv_prepare(z, gq_t, gk_t, ones_bd, tm)
    v_col = (3 * D_CONV + 2 * D_ATTN) // BAND
    (y_attn, lse), gathered = _attn_fwd(q, k, z, v_col, carry=_x_gather_chips(later))
    mix, gathered = _mixer_out(z, cw, y_attn, g_conv_out, g_attn_out, tm,
                               carry=_x_gather_sibling(gathered))
    wout = gathered[0].reshape(D_MODEL, D_MODEL)
    wgate = _columns_from_chips(gathered[1])
    wup = _columns_from_chips(gathered[2])
    wdown = gathered[3].reshape(D_FF, D_MODEL)
    (x1,) = _matmul("out_proj", mix, wout, [xs], [F32], lambda acc, r: (r + acc,), tm, D_MODEL)
    h2, gate, up, act = _norm_matmul("ffn_up", x1, g_ffn, [wgate, wup], tm, D_FF, True, BF16)

    def loss_epilogue(acc, r, tgt):
        err = r + acc - tgt
        dy = err * (1.0 / D_MODEL)
        return dy, dy, jnp.sum(err * err)

    dx2, dx2b, loss_sum = _matmul("ffn_down_loss", act, wdown, [x1, target], [F32, BF16],
                                  loss_epilogue, tm, D_MODEL, loss=True)

    def swiglu_bwd(da, gt, u):
        gt, u = gt.astype(F32), u.astype(F32)
        s = _sigmoid(gt)
        return da * u * (s * (1.0 + gt * (1.0 - s))), da * (gt * s)

    dgate, dup = _matmul("ffn_down_bwd", dx2b, wdown, [gate, up], [BF16, BF16], swiglu_bwd,
                         tm, D_FF, transposed_w=True)
    gw_down = _matmul_tn("grad_w_down", act, dx2b, 512, tmm)
    gw_gate = _matmul_tn("grad_w_gate", h2, dgate, 1408, tmm)
    gw_up = _matmul_tn("grad_w_up", h2, dup, 1408, tmm)

    me = 2 * lax.axis_index("x") + lax.axis_index("y")
    where = jnp.stack([lax.axis_index("c"), me]).astype(jnp.int32)

    def pair_sums(names, full, got):
        return [_pair_sum(f"pair_sum_{nme}", a, b, where) for nme, a, b in zip(names, full, got)]

    def chip_sums(names, pair, got):
        return [_chip_sum(f"chip_sum_{nme}", own, b) for nme, (_, own), b in zip(names, pair, got)]

    ffn = ["w_gate", "w_up", "w_down"]
    full = [_columns_to_chips(gw_gate), _columns_to_chips(gw_up),
            gw_down.reshape(N_CHIPS, D_FF // N_CHIPS, D_MODEL)]
    (dx1, dx1b, gg_ffn), got = _matmul_norm_bwd("ffn_up_bwd", [(dgate, wgate), (dup, wup)], x1, dx2,
                                                g_ffn, tm, carry=_x_pair(full))
    pair = pair_sums(ffn, full, got)
    (dmix,) = _matmul("out_proj_bwd", dx1b, wout, [], [F32], lambda acc: (acc,), tm, D_MODEL,
                      transposed_w=True)
    gw_out = _matmul_tn("grad_w_out", mix, dx1b, 512, tmm)
    full = [gw_out.reshape(N_CHIPS, D_MODEL // N_CHIPS, D_MODEL)]
    (dzc, do, dd, gcw, gg_conv, gg_attn), got = _mixer_bwd(
        z, dmix, y_attn, cw, g_conv_out, g_attn_out, ones_bd, tm, carry=_x_pair(full))
    pair += pair_sums(["w_out"], full, got)
    early = ffn + ["w_out"]
    (dqn, dkn, dv), got = _attn_bwd(q, k, z, v_col, do, lse, dd,
                                    carry=_x_chips([p for p, _ in pair]))
    mine = chip_sums(early, pair, got)
    (dz, gg_q, gg_k), theirs = _qkv_bwd(z, dzc, dqn, dkn, dv, gq_t, gk_t, ones_bd, tm,
                                        carry=_x_share(mine))
    full = [_matmul_tn("grad_w_in", h1, dz, D_IN // N_CHIPS, tmm, by_chip=True)]
    grad_x, _, gg_mix = _matmul_norm_bwd("in_proj_bwd", [(dz, win)], xs, dx1, g_mix, tm)
    got = _exchange_alone("grad_pair_exchange_w_in", _x_pair(full))
    pair = pair_sums(["w_in"], full, got)
    got = _exchange_alone("grad_chip_exchange_w_in", _x_chips([pair[0][0]]))
    mine += chip_sums(["w_in"], pair, got)
    theirs = list(theirs) + list(_exchange_alone("grad_pair_share_w_in", _x_share(mine[-1:])))
    big = early + ["w_in"]

    small = _small_all_reduce({
        "g_mix": gg_mix, "g_ffn": gg_ffn, "g_conv_out": gg_conv, "g_attn_out": gg_attn,
        "g_q": gg_q, "g_k": gg_k, "loss": loss_sum, "conv_w": gcw})
    heads = D_ATTN // HEAD_DIM
    grads = {
        "g_mix": small[0:1, :], "g_ffn": small[1:2, :],
        "g_conv_out": small[2:3, 0:512], "g_attn_out": small[2:3, 512:1024],
        "g_q": small[3, 0:512].reshape(heads, HEAD_DIM).sum(axis=0)[None, :],
        "g_k": small[3, 512:1024].reshape(heads, HEAD_DIM).sum(axis=0)[None, :],
        "conv_w": lax.dynamic_slice(small[8:11, 0:512], (0, me * (D_CONV // N_CHIPS)),
                                    (3, D_CONV // N_CHIPS)),
    }
    halves = dict(zip(big, zip(mine, theirs)))
    loss = small[4, 0] * 0.5 * (1.0 / D_MODEL)

    weights = dict(g_mix=g_mix, w_in=w_in, conv_w=conv_w, g_q=g_q, g_k=g_k, g_conv_out=g_conv_out,
                   g_attn_out=g_attn_out, w_out=w_out, g_ffn=g_ffn, w_gate=w_gate, w_up=w_up,
                   w_down=w_down)
    moments_m = dict(g_mix=m_g_mix, w_in=m_w_in, conv_w=m_conv_w, g_q=m_g_q, g_k=m_g_k,
                     g_conv_out=m_g_conv_out, g_attn_out=m_g_attn_out, w_out=m_w_out, g_ffn=m_g_ffn,
                     w_gate=m_w_gate, w_up=m_w_up, w_down=m_w_down)
    moments_v = dict(g_mix=v_g_mix, w_in=v_w_in, conv_w=v_conv_w, g_q=v_g_q, g_k=v_g_k,
                     g_conv_out=v_g_conv_out, g_attn_out=v_g_attn_out, w_out=v_w_out, g_ffn=v_g_ffn,
                     w_gate=v_w_gate, w_up=v_w_up, w_down=v_w_down)
    names = list(weights)
    out_g, out_d, out_m, out_v = [], [], [], []
    for nme in names:
        wgt = weights[nme]
        shape2 = wgt.shape[-2:] if wgt.ndim == 3 else wgt.shape
        state = (wgt.reshape(shape2), moments_m[nme].reshape(shape2), moments_v[nme].reshape(shape2))
        if nme in halves:
            g2, dlt, nm, nv = _adamw_shard(f"adamw_{nme}", *state, *halves[nme], where)
        else:
            g2 = grads[nme].reshape(shape2)
            dlt, nm, nv = _adamw(f"adamw_{nme}", state[0], g2, state[1], state[2])
        out_g.append(g2.reshape(wgt.shape))
        out_d.append(dlt.reshape(wgt.shape))
        out_m.append(nm.reshape(wgt.shape))
        out_v.append(nv.reshape(wgt.shape))
    return (loss, grad_x[None], *out_g, *out_d, *out_m, *out_v)
```

```python
import functools
from typing import Any, Callable, NamedTuple, Sequence

import jax
import jax.numpy as jnp
from jax import lax
from jax.experimental import pallas as pl
from jax.experimental.pallas import tpu as pltpu

F32 = jnp.float32
BF16 = jnp.bfloat16
MESH = pl.DeviceIdType.MESH

D_MODEL = 1024
D_CONV = 512
D_ATTN = 512
HEAD_DIM = 64
D_FF = 2816
D_IN = 3 * D_CONV + 3 * D_ATTN
DILATIONS = (1, 4, 16)
BAND = 128
EPS = 1e-6
NEG = -1e30
N_CHIPS = 4

ADAM_LR = 0.001
ADAM_B1 = 0.9
ADAM_B2 = 0.999
ADAM_EPS = 1e-08
ADAM_WD = 0.01
ADAM_STEP = 10

V7X_VMEM_BYTES = 64 * 1024 * 1024
VMEM_LIMIT = V7X_VMEM_BYTES - 8 * 1024 * 1024
ANY = pl.BlockSpec(memory_space=pl.ANY)
VMEM_WHOLE = pl.BlockSpec(memory_space=pltpu.VMEM)


def _params(*sem):
    return pltpu.CompilerParams(dimension_semantics=sem, vmem_limit_bytes=VMEM_LIMIT)


def _sds(shape, dtype):
    return jax.ShapeDtypeStruct(shape, dtype)


def _resident(whole):
    return pl.Buffered(1) if whole else None


def _place():
    x, y, c = lax.axis_index("x"), lax.axis_index("y"), lax.axis_index("c")
    chips = [(1 - x, y), (x, 1 - y), (1 - x, 1 - y)]
    return x, y, c, 2 * x + y, chips, [2 * cx + cy for cx, cy in chips]


def _all_gather(shards):
    n = len(shards)

    def body(*refs):
        ins, outs, stage = refs[:n], refs[n:2 * n], refs[2 * n:3 * n]
        ssem, rsem, fsem, gsem, lsem, osem = refs[3 * n:]
        x, y, c, me, chips, cids = _place()
        sib = (x, y, 1 - c)

        def half(w, which):
            h = shards[w].shape[0] // 2
            return pl.ds(pl.multiple_of(which * h, 8), h)

        loads = [pltpu.make_async_copy(ins[w], stage[w], lsem.at[w]) for w in range(n)]
        local = [pltpu.make_async_copy(stage[w], outs[w].at[me], osem.at[w]) for w in range(n)]
        for cp in loads:
            cp.start()

        def chip_copy(w, j, src_slot):
            rows = half(w, c)
            return pltpu.make_async_remote_copy(
                src_ref=ins[w].at[rows], dst_ref=outs[w].at[src_slot, rows],
                send_sem=ssem.at[3 * w + j], recv_sem=rsem.at[3 * w + j],
                device_id=(*chips[j], c), device_id_type=MESH)

        def sib_copy(w, j, which):
            rows = half(w, which)
            return pltpu.make_async_remote_copy(
                src_ref=outs[w].at[cids[j], rows], dst_ref=outs[w].at[cids[j], rows],
                send_sem=fsem.at[3 * w + j], recv_sem=gsem.at[3 * w + j],
                device_id=sib, device_id_type=MESH)

        sends = [chip_copy(w, j, me) for w in range(n) for j in range(3)]
        for cp in sends:
            cp.start()
        for w in range(n):
            loads[w].wait()
            local[w].start()
        passed = []
        for w in range(n):
            for j in range(3):
                chip_copy(w, j, cids[j]).wait_recv()
                cp = sib_copy(w, j, c)
                cp.start()
                passed.append(cp)
        for w in range(n):
            for j in range(3):
                sib_copy(w, j, 1 - c).wait_recv()
        for cp in sends + passed:
            cp.wait_send()
        for cp in local:
            cp.wait()

    return pl.pallas_call(
        body, name="all_gather_weights",
        out_shape=[_sds((N_CHIPS,) + s.shape, s.dtype) for s in shards],
        in_specs=[ANY] * n, out_specs=[ANY] * n,
        scratch_shapes=[pltpu.VMEM(s.shape, s.dtype) for s in shards]
        + [pltpu.SemaphoreType.DMA((3 * n,))] * 4 + [pltpu.SemaphoreType.DMA((n,))] * 2,
        compiler_params=pltpu.CompilerParams(vmem_limit_bytes=VMEM_LIMIT),
    )(*shards)


class _Exchange(NamedTuple):
    srcs: Sequence[Any]
    lands: Sequence[Any]
    outs: Sequence[Any]
    n_sems: int
    copies: Callable


def _remote(src, dst, ssem, rsem, k, to):
    return pltpu.make_async_remote_copy(src_ref=src, dst_ref=dst, send_sem=ssem.at[k],
                                        recv_sem=rsem.at[k], device_id=to, device_id_type=MESH)


def _x_gather_chips(shards):
    def copies(srcs, lands, outs, ssem, rsem):
        _, _, c, me, chips, cids = _place()
        go, arrive = [], []
        for w, s in enumerate(shards):
            h = s.shape[0] // 2
            rows = pl.ds(pl.multiple_of(c * h, 8), h)
            for j in range(3):
                to = (*chips[j], c)
                go.append(_remote(srcs[w].at[rows], lands[w].at[me, rows], ssem, rsem, 3 * w + j, to))
                arrive.append(_remote(srcs[w].at[rows], lands[w].at[cids[j], rows], ssem, rsem,
                                      3 * w + j, to))
        return go, arrive

    lands = [jnp.broadcast_to(s[None], (N_CHIPS,) + s.shape) for s in shards]
    return _Exchange(shards, lands, [], 3 * len(shards), copies)


def _x_gather_sibling(gathered):
    def copies(srcs, lands, outs, ssem, rsem):
        x, y, c, _, _, cids = _place()
        go, arrive = [], []
        for w, g in enumerate(gathered):
            h = g.shape[1] // 2
            mine = pl.ds(pl.multiple_of(c * h, 8), h)
            theirs = pl.ds(pl.multiple_of((1 - c) * h, 8), h)
            for j in range(3):
                slab = lands[w].at[cids[j]]
                go.append(_remote(slab.at[mine], slab.at[mine], ssem, rsem, 3 * w + j, (x, y, 1 - c)))
                arrive.append(_remote(slab.at[theirs], slab.at[theirs], ssem, rsem, 3 * w + j,
                                      (x, y, 1 - c)))
        return go, arrive

    return _Exchange([], gathered, [], 3 * len(gathered), copies)


def _x_pair(grads):
    def copies(srcs, lands, outs, ssem, rsem):
        x, y, c, _, _, _ = _place()
        go = []
        for w, g in enumerate(grads):
            h = g.shape[1] // 2
            theirs = pl.ds(pl.multiple_of((1 - c) * h, 8), h)
            go.append(_remote(srcs[w].at[:, theirs, :], outs[w], ssem, rsem, w, (x, y, 1 - c)))
        return go, go

    outs = [_sds((N_CHIPS, g.shape[1] // 2, g.shape[2]), g.dtype) for g in grads]
    return _Exchange(grads, [], outs, len(grads), copies)


def _x_chips(parts):
    def copies(srcs, lands, outs, ssem, rsem):
        _, _, c, _, chips, cids = _place()
        go = [_remote(srcs[w].at[cids[j]], outs[w].at[j], ssem, rsem, 3 * w + j, (*chips[j], c))
              for w in range(len(parts)) for j in range(3)]
        return go, go

    outs = [_sds((3,) + p.shape[1:], p.dtype) for p in parts]
    return _Exchange(parts, [], outs, 3 * len(parts), copies)


def _x_share(halves):
    def copies(srcs, lands, outs, ssem, rsem):
        x, y, c, _, _, _ = _place()
        go = [_remote(srcs[w], outs[w], ssem, rsem, w, (x, y, 1 - c)) for w in range(len(halves))]
        return go, go

    return _Exchange(halves, [], [_sds(h.shape, h.dtype) for h in halves], len(halves), copies)


def _call(body, args, *, name, grid, in_specs, out_specs, out_shape, scratch_shapes=(),
          semantics=None, carry=None):
    single = not isinstance(out_shape, (list, tuple))
    out_shape = [out_shape] if single else list(out_shape)
    out_specs = [out_specs] if single else list(out_specs)
    if carry is None:
        res = pl.pallas_call(
            body, name=name, grid=grid, in_specs=list(in_specs), out_specs=out_specs,
            out_shape=out_shape, scratch_shapes=list(scratch_shapes),
            compiler_params=_params(*(semantics or ("arbitrary",) * len(grid))))(*args)
        return res[0] if single else res
    n_in, n_out, n_scr = len(args), len(out_shape), len(scratch_shapes)
    n_src, n_land, n_new = len(carry.srcs), len(carry.lands), len(carry.outs)

    def carrying(*refs):
        at = 0
        parts = []
        for n in (n_in, n_src, n_land, n_out, n_land, n_new, n_scr, 2):
            parts.append(refs[at:at + n])
            at += n
        ins, srcs, _, outs, lands, news, scratch, (ssem, rsem) = parts
        ids = [pl.program_id(a) for a in range(len(grid))]
        first = functools.reduce(jnp.logical_and, [i == 0 for i in ids])
        last = functools.reduce(jnp.logical_and, [i == g - 1 for i, g in zip(ids, grid)])
        go, arrive = carry.copies(srcs, lands, news, ssem, rsem)

        @pl.when(first)
        def _():
            for cp in go:
                cp.start()

        body(*ins, *outs, *scratch)

        @pl.when(last)
        def _():
            for cp in go:
                cp.wait_send()
            for cp in arrive:
                cp.wait_recv()

    res = pl.pallas_call(
        carrying, name=name, grid=grid,
        in_specs=list(in_specs) + [ANY] * (n_src + n_land),
        out_specs=out_specs + [ANY] * (n_land + n_new),
        out_shape=out_shape + [_sds(a.shape, a.dtype) for a in carry.lands] + list(carry.outs),
        input_output_aliases={n_in + n_src + i: n_out + i for i in range(n_land)},
        scratch_shapes=list(scratch_shapes) + [pltpu.SemaphoreType.DMA((carry.n_sems,))] * 2,
        compiler_params=_params(*(("arbitrary",) * len(grid))))(*args, *carry.srcs, *carry.lands)
    own = res[:n_out]
    return (own[0] if single else own), res[n_out:]


def _exchange_alone(name, exchange):
    def body(x_ref, o_ref):
        o_ref[...] = x_ref[...]

    blk = pl.BlockSpec((8, 128), lambda i: (0, 0))
    _, res = _call(body, [jnp.zeros((8, 128), F32)], name=name, grid=(1,), in_specs=[blk],
                   out_specs=blk, out_shape=_sds((8, 128), F32), carry=exchange)
    return res


def _row_block(r, want):
    return max(d for d in range(1, min(want, r) + 1) if r % d == 0 and (d % 8 == 0 or d == r))


def _pair_sum(name, full, got, where):
    _, r, n = full.shape
    h = r // 2
    tr = _row_block(h, 256)
    nb = h // tr

    def body(w_ref, a_ref, b_ref, o_ref, own_ref):
        total = a_ref[...] + b_ref[...]
        o_ref[...] = total.astype(BF16)

        @pl.when(pl.program_id(1) == w_ref[1])
        def _():
            own_ref[...] = total[0]

    blk = pl.BlockSpec((1, tr, n), lambda i, s, w: (s, i, 0))
    return pl.pallas_call(
        body, name=name, out_shape=[_sds(got.shape, BF16), _sds((h, n), F32)],
        grid_spec=pltpu.PrefetchScalarGridSpec(
            num_scalar_prefetch=1, grid=(nb, N_CHIPS),
            in_specs=[pl.BlockSpec((1, tr, n), lambda i, s, w: (s, w[0] * nb + i, 0)), blk],
            out_specs=[blk, pl.BlockSpec((tr, n), lambda i, s, w: (i, 0))]),
        compiler_params=_params("parallel", "arbitrary"),
    )(where, full, got)


def _chip_sum(name, own, got):
    h, n = own.shape
    tr = _row_block(h, 256)

    def body(a_ref, b0, b1, b2, o_ref):
        o_ref[...] = ((a_ref[...] + b0[0].astype(F32)) + b1[0].astype(F32)) + b2[0].astype(F32)

    def slot(j):
        return pl.BlockSpec((1, tr, n), lambda i: (j, i, 0))

    blk = pl.BlockSpec((tr, n), lambda i: (i, 0))
    return pl.pallas_call(
        body, name=name, grid=(h // tr,), out_shape=_sds((h, n), F32),
        in_specs=[blk, slot(0), slot(1), slot(2)], out_specs=blk,
        compiler_params=_params("parallel"),
    )(own, got, got, got)


SMALL_ROWS = 16
SMALL_LAYOUT = (
    ("g_mix", 0, 0, 1, 1024), ("g_ffn", 1, 0, 1, 1024), ("g_conv_out", 2, 0, 1, 512),
    ("g_attn_out", 2, 512, 1, 512), ("g_q", 3, 0, 1, 512), ("g_k", 3, 512, 1, 512),
    ("loss", 4, 0, 1, 128), ("conv_w", 8, 0, 8, 512))


def _small_all_reduce(parts):
    names = [s[0] for s in SMALL_LAYOUT]

    def body(*refs):
        ins = refs[:len(names)]
        out_ref, stage, buf, ssem, rsem = refs[len(names):]
        x, y, c, _, _, _ = _place()
        me = 4 * x + 2 * y + c
        stage[...] = jnp.zeros_like(stage)
        for ref, (_, r0, c0, nr, nc) in zip(ins, SMALL_LAYOUT):
            stage[r0:r0 + nr, c0:c0 + nc] = ref[0:nr, :]
        buf[me] = stage[...]
        peers = []
        for d in range(1, 8):
            px = 1 - x if d & 4 else x
            py = 1 - y if d & 2 else y
            pc = 1 - c if d & 1 else c
            peers.append(((px, py, pc), 4 * px + 2 * py + pc))
        sends = [pltpu.make_async_remote_copy(
            src_ref=stage, dst_ref=buf.at[me], send_sem=ssem.at[k], recv_sem=rsem.at[k],
            device_id=peer, device_id_type=MESH) for k, (peer, _) in enumerate(peers)]
        for cp in sends:
            cp.start()
        for k, (peer, pid) in enumerate(peers):
            pltpu.make_async_remote_copy(
                src_ref=stage, dst_ref=buf.at[pid], send_sem=ssem.at[k], recv_sem=rsem.at[k],
                device_id=peer, device_id_type=MESH).wait_recv()
        for cp in sends:
            cp.wait_send()
        acc = buf[0]
        for k in range(1, 8):
            acc = acc + buf[k]
        out_ref[...] = acc

    return pl.pallas_call(
        body, name="small_all_reduce", out_shape=_sds((SMALL_ROWS, 1024), F32),
        in_specs=[VMEM_WHOLE] * len(names), out_specs=VMEM_WHOLE,
        scratch_shapes=[pltpu.VMEM((SMALL_ROWS, 1024), F32), pltpu.VMEM((8, SMALL_ROWS, 1024), F32),
                        pltpu.SemaphoreType.DMA((7,)), pltpu.SemaphoreType.DMA((7,))],
    )(*[parts[k] for k in names])


def _dot(a, b):
    return jnp.dot(a, b, preferred_element_type=F32)


def _dot_nt(a, b):
    return lax.dot_general(a, b, (((1,), (1,)), ((), ())), preferred_element_type=F32)


def _dot_tn(a, b):
    return lax.dot_general(a, b, (((0,), (0,)), ((), ())), preferred_element_type=F32)


def _sigmoid(v):
    return 1.0 / (1.0 + jnp.exp(-v))


def _rms_scale(v):
    return lax.rsqrt(jnp.mean(v * v, axis=-1, keepdims=True) + EPS)


def _rms_bwd(v, r, g, dy):
    vh = v * r
    dh = dy * g
    return r * (dh - vh * jnp.mean(dh * vh, axis=-1, keepdims=True)), vh


def _head_sum(a, ones_bd):
    hi = a.astype(BF16)
    lo = (a - hi.astype(F32)).astype(BF16)
    return _dot(hi, ones_bd) + _dot(lo, ones_bd)


def _head_rms_scale(v, ones_bd):
    return lax.rsqrt(_head_sum(v * v, ones_bd) * (1.0 / HEAD_DIM) + EPS)


MXU_COLUMNS = 256


def _column_chunks(n):
    width = MXU_COLUMNS if n % MXU_COLUMNS == 0 else n
    return [slice(c, c + width) for c in range(0, n, width)]


def _norm_matmul(name, x, g, ws, tm, tn, swiglu, out_dtype=F32):
    t, d = x.shape
    n = ws[0].shape[1]
    nw = len(ws)

    def body(x_ref, g_ref, *refs):
        w_refs, h_ref, o_refs = refs[:nw], refs[nw], refs[nw + 1:2 * nw + 1]
        hs = refs[-1]

        @pl.when(pl.program_id(1) == 0)
        def _():
            xv = x_ref[...]
            h = (xv * _rms_scale(xv) * g_ref[...]).astype(BF16)
            hs[...] = h
            h_ref[...] = h

        h = hs[...]
        for cols in _column_chunks(tn):
            outs = [_dot(h, w[:, cols]) for w in w_refs]
            for o_ref, o in zip(o_refs, outs):
                o_ref[:, cols] = o.astype(out_dtype)
            if swiglu:
                refs[2 * nw + 1][:, cols] = (outs[0] * _sigmoid(outs[0]) * outs[1]).astype(BF16)

    row = pl.BlockSpec((tm, d), lambda i, j: (i, 0))
    col = pl.BlockSpec((tm, tn), lambda i, j: (i, j))
    out_shape = [_sds((t, d), BF16)] + [_sds((t, n), out_dtype)] * nw
    out_specs = [row] + [col] * nw
    if swiglu:
        out_shape.append(_sds((t, n), BF16))
        out_specs.append(col)
    return pl.pallas_call(
        body, name=name, grid=(t // tm, n // tn), out_shape=out_shape,
        in_specs=[row, pl.BlockSpec((1, d), lambda i, j: (0, 0))]
        + [pl.BlockSpec((d, tn), lambda i, j: (0, j), pipeline_mode=_resident(tn == n))] * nw,
        out_specs=out_specs, scratch_shapes=[pltpu.VMEM((tm, d), BF16)],
        compiler_params=_params("parallel", "arbitrary"),
    )(x, g, *ws)


def _matmul(name, a, w, extras, out_dtypes, epilogue, tm, tn, transposed_w=False, loss=False):
    t, k = a.shape
    n = w.shape[0] if transposed_w else w.shape[1]
    ne, no = len(extras), len(out_dtypes)

    def body(a_ref, w_ref, *refs):
        e_refs, o_refs = refs[:ne], refs[ne:]
        a = a_ref[...]
        total = 0.0
        for cols in _column_chunks(tn):
            acc = _dot_nt(a, w_ref[cols, :]) if transposed_w else _dot(a, w_ref[:, cols])
            res = epilogue(acc, *[e[:, cols] for e in e_refs])
            for o_ref, r in zip(o_refs[:no], res[:no]):
                o_ref[:, cols] = r.astype(o_ref.dtype)
            if loss:
                total = total + res[no]
        if loss:
            first = jnp.logical_and(pl.program_id(0) == 0, pl.program_id(1) == 0)

            @pl.when(first)
            def _():
                o_refs[no][...] = jnp.zeros_like(o_refs[no])

            o_refs[no][...] += total

    col = pl.BlockSpec((tm, tn), lambda i, j: (i, j))
    w_spec = (pl.BlockSpec((tn, k), lambda i, j: (j, 0), pipeline_mode=_resident(tn == n))
              if transposed_w
              else pl.BlockSpec((k, tn), lambda i, j: (0, j), pipeline_mode=_resident(tn == n)))
    out_shape = [_sds((t, n), dt) for dt in out_dtypes]
    out_specs = [col] * no
    if loss:
        out_shape.append(_sds((8, 128), F32))
        out_specs.append(pl.BlockSpec((8, 128), lambda i, j: (0, 0)))
    return pl.pallas_call(
        body, name=name, grid=(t // tm, n // tn), out_shape=out_shape,
        in_specs=[pl.BlockSpec((tm, k), lambda i, j: (i, 0)), w_spec] + [col] * ne,
        out_specs=out_specs,
        compiler_params=_params(*(("arbitrary", "arbitrary") if loss else ("parallel", "parallel"))),
    )(a, w, *extras)


def _matmul_norm_bwd(name, pairs, x, dres, g, tm, carry=None):
    t, d = x.shape
    npairs = len(pairs)

    def body(*refs):
        a_refs, w_refs = refs[:npairs], refs[npairs:2 * npairs]
        x_ref, r_ref, g_ref, dx_ref, dxb_ref, dg_ref = refs[2 * npairs:]
        dy = _dot_nt(a_refs[0][...], w_refs[0][...])
        for a_ref, w_ref in zip(a_refs[1:], w_refs[1:]):
            dy = dy + _dot_nt(a_ref[...], w_ref[...])
        xv = x_ref[...]
        dx, xh = _rms_bwd(xv, _rms_scale(xv), g_ref[...], dy)
        dx = dx + r_ref[...]
        dx_ref[...] = dx
        dxb_ref[...] = dx.astype(BF16)

        @pl.when(pl.program_id(0) == 0)
        def _():
            dg_ref[...] = jnp.zeros_like(dg_ref)

        dg_ref[...] += jnp.sum(dy * xh, axis=0, keepdims=True)

    row = pl.BlockSpec((tm, d), lambda i: (i, 0))
    vec = pl.BlockSpec((1, d), lambda i: (0, 0))
    return _call(
        body, [a for a, _ in pairs] + [w for _, w in pairs] + [x, dres, g], name=name,
        grid=(t // tm,), out_shape=[_sds((t, d), F32), _sds((t, d), BF16), _sds((1, d), F32)],
        in_specs=[pl.BlockSpec((tm, a.shape[1]), lambda i: (i, 0)) for a, _ in pairs]
        + [pl.BlockSpec(w.shape, lambda i: (0, 0), pipeline_mode=pl.Buffered(1)) for _, w in pairs]
        + [row, row, vec],
        out_specs=[row, row, vec], carry=carry)


def _matmul_tn(name, a, g, tn, tk, by_chip=False):
    t, ka = a.shape
    n = g.shape[1]

    def body(a_ref, g_ref, o_ref):
        @pl.when(pl.program_id(1) == 0)
        def _():
            o_ref[...] = jnp.zeros_like(o_ref)

        acc = _dot_tn(a_ref[...], g_ref[...])
        o_ref[...] += acc[None] if by_chip else acc

    return pl.pallas_call(
        body, name=name, grid=(n // tn, t // tk),
        out_shape=_sds((n // tn, ka, tn) if by_chip else (ka, n), F32),
        in_specs=[pl.BlockSpec((tk, ka), lambda j, s: (s, 0)),
                  pl.BlockSpec((tk, tn), lambda j, s: (s, j))],
        out_specs=(pl.BlockSpec((1, ka, tn), lambda j, s: (j, 0, 0)) if by_chip
                   else pl.BlockSpec((ka, tn), lambda j, s: (0, j))),
        compiler_params=_params("parallel", "arbitrary"),
    )(a, g)


def _elementwise(name, fn, ins, out_dtypes, tr):
    r, n = ins[0].shape
    tr = _row_block(r, tr)
    ni = len(ins)

    def body(*refs):
        res = fn(*[ref[...] for ref in refs[:ni]])
        for o_ref, v in zip(refs[ni:], res):
            o_ref[...] = v.astype(o_ref.dtype)

    blk = pl.BlockSpec((tr, n), lambda i: (i, 0))
    return pl.pallas_call(
        body, name=name, grid=(r // tr,), out_shape=[_sds((r, n), dt) for dt in out_dtypes],
        in_specs=[blk] * ni, out_specs=[blk] * len(out_dtypes),
        compiler_params=_params("parallel"),
    )(*ins)


def _adamw_update(w, g, m, v):
    m = ADAM_B1 * m + (1.0 - ADAM_B1) * g
    v = ADAM_B2 * v + (1.0 - ADAM_B2) * (g * g)
    m_hat = m / (1.0 - ADAM_B1 ** ADAM_STEP)
    v_hat = v / (1.0 - ADAM_B2 ** ADAM_STEP)
    return -ADAM_LR * (m_hat / (jnp.sqrt(v_hat) + ADAM_EPS) + ADAM_WD * w), m, v


def _adamw(name, w, g, m, v):
    return _elementwise(name, _adamw_update, [w, g, m, v], [F32] * 3, 256)


def _adamw_shard(name, w, m, v, mine, theirs, where):
    r, n = w.shape
    h = r // 2
    tr = _row_block(h, 256)
    nb = h // tr

    def body(w_ref, p_ref, m_ref, v_ref, a_ref, b_ref, g_ref, d_ref, nm_ref, nv_ref):
        g = jnp.where(pl.program_id(0) == w_ref[0], a_ref[...], b_ref[...])
        g_ref[...] = g
        d_ref[...], nm_ref[...], nv_ref[...] = _adamw_update(p_ref[...], g, m_ref[...], v_ref[...])

    whole = pl.BlockSpec((tr, n), lambda s, i, c: (s * nb + i, 0))
    half = pl.BlockSpec((tr, n), lambda s, i, c: (i, 0))
    return pl.pallas_call(
        body, name=name, out_shape=[_sds((r, n), F32)] * 4,
        grid_spec=pltpu.PrefetchScalarGridSpec(
            num_scalar_prefetch=1, grid=(2, nb), in_specs=[whole] * 3 + [half] * 2,
            out_specs=[whole] * 4),
        compiler_params=_params("parallel", "parallel"),
    )(where, w, m, v, mine, theirs)


def _qkv_prepare(z, gq, gk, ones_bd, tm):
    t = z.shape[0]

    def body(zq_ref, zk_ref, gq_ref, gk_ref, bd_ref, q_ref, k_ref):
        bd = bd_ref[...]
        q = zq_ref[...]
        k = zk_ref[...]
        q_ref[...] = (q * _head_rms_scale(q, bd) * gq_ref[...]) * HEAD_DIM ** -0.5
        k_ref[...] = k * _head_rms_scale(k, bd) * gk_ref[...]

    vec = pl.BlockSpec((1, 512), lambda i: (0, 0))
    out = pl.BlockSpec((tm, 512), lambda i: (i, 0))
    return pl.pallas_call(
        body, name="qkv_prepare", grid=(t // tm,), out_shape=[_sds((t, 512), F32)] * 2,
        in_specs=[pl.BlockSpec((tm, 512), lambda i: (i, 3)), pl.BlockSpec((tm, 512), lambda i: (i, 4)),
                  vec, vec, pl.BlockSpec((512, 512), lambda i: (0, 0))],
        out_specs=[out] * 2, compiler_params=_params("parallel"),
    )(z, z, gq, gk, ones_bd)


TOK = 2048
UNITS = TOK // BAND


def _stack_masks():
    row = lax.broadcasted_iota(jnp.int32, (2 * BAND, 2 * BAND), 0) & (BAND - 1)
    col = lax.broadcasted_iota(jnp.int32, (2 * BAND, 2 * BAND), 1)
    lane = lax.broadcasted_iota(jnp.int32, (BAND, BAND), 1)
    head0 = lane < HEAD_DIM
    ones = [jnp.where(head0, 1.0, 0.0).astype(BF16), jnp.where(head0, 0.0, 1.0).astype(BF16)]
    return col - row, col, head0, ones


def _split3(x):
    hi = x.astype(BF16).astype(F32)
    mid = (x - hi).astype(BF16).astype(F32)
    return hi, mid, x - hi - mid


def _gather(srcs, dst, d):
    per = TOK // d
    at = 0
    for r in range(d):
        for src in srcs:
            rows = src[pl.ds(r, per, stride=d), :] if d > 1 else src[...]
            dst[pl.ds(at, per), :] = rows.astype(dst.dtype)
            at += per


def _scatter_add(out_ref, src, d, per_src, offset, first):
    per = TOK // d
    if d == 1:
        val = src[pl.ds(offset, per), :]
        out_ref[...] = val if first else out_ref[...] + val
        return
    for r in range(d):
        val = src[pl.ds(r * per_src + offset, per), :]
        idx = pl.ds(r, per, stride=d)
        out_ref[idx, :] = val if first else out_ref[idx, :] + val


def _attn_fwd(q, k, v, v_col, carry=None):
    t = q.shape[0]
    nblk = t // TOK

    def body(q_ref, kp_ref, k_ref, vp_ref, v_ref, y_ref, l_ref, qs, ks, vs, ob, lb, on, ln):
        i = pl.program_id(1)
        diff, col, head0, hm = _stack_masks()
        band_ok = jnp.logical_and(diff >= 0, diff <= BAND)
        for g, d in enumerate(DILATIONS):
            per = TOK // d
            nb = per // BAND
            _gather([q_ref], qs, d)
            _gather([kp_ref, k_ref], ks, d)
            _gather([vp_ref, v_ref], vs, d)

            def unit(u, carry):
                r, b = u // nb, u % nb
                qu = qs[pl.ds(pl.multiple_of(u * BAND, BAND), BAND), :]
                start = pl.multiple_of(r * 2 * per + per + (b - 1) * BAND, BAND)
                kw = ks[pl.ds(start, 2 * BAND), :]
                vw = vs[pl.ds(start, 2 * BAND), :]
                lo = jnp.where(jnp.logical_and(i == 0, b == 0), BAND, 0)
                s = _dot_nt(jnp.concatenate([qu * hm[0], qu * hm[1]], axis=0), kw)
                s = jnp.where(jnp.logical_and(band_ok, col >= lo), s, NEG)
                mx = jnp.max(s, axis=-1, keepdims=True)
                e = jnp.exp(s - mx)
                den = jnp.sum(e, axis=-1, keepdims=True)
                o2 = _dot(e.astype(BF16), vw) / den
                l2 = jnp.broadcast_to(mx + jnp.log(den), (2 * BAND, BAND))
                rows = pl.ds(pl.multiple_of(u * BAND, BAND), BAND)
                ob[rows, :] = jnp.where(head0, o2[:BAND], o2[BAND:])
                lb[rows, :] = jnp.where(head0, l2[:BAND], l2[BAND:])
                return carry

            lax.fori_loop(0, UNITS, unit, 0, unroll=16)
            _scatter_add(on.at[g], ob, d, per, 0, True)
            _scatter_add(ln.at[g], lb, d, per, 0, True)
        ls = [ln[0], ln[1], ln[2]]
        mx = jnp.maximum(jnp.maximum(ls[0], ls[1]), ls[2])
        es = [jnp.exp(l - mx) for l in ls]
        tot = es[0] + es[1] + es[2]
        y_ref[...] = (es[0] * on[0] + es[1] * on[1] + es[2] * on[2]) / tot
        l_ref[...] = mx + jnp.log(tot)

    main = pl.BlockSpec((TOK, BAND), lambda j, i: (i, j))
    prev = pl.BlockSpec((TOK, BAND), lambda j, i: (jnp.maximum(i - 1, 0), j))
    vmain = pl.BlockSpec((TOK, BAND), lambda j, i: (i, j + v_col))
    vprev = pl.BlockSpec((TOK, BAND), lambda j, i: (jnp.maximum(i - 1, 0), j + v_col))
    return _call(
        body, [q, k, k, v, v], name="attn_fwd", grid=(D_ATTN // BAND, nblk),
        out_shape=[_sds((t, D_ATTN), F32)] * 2,
        in_specs=[main, prev, main, vprev, vmain], out_specs=[main, main],
        scratch_shapes=[pltpu.VMEM((TOK, BAND), BF16), pltpu.VMEM((2 * TOK, BAND), BF16),
                        pltpu.VMEM((2 * TOK, BAND), BF16), pltpu.VMEM((TOK, BAND), F32),
                        pltpu.VMEM((TOK, BAND), F32), pltpu.VMEM((3, TOK, BAND), F32),
                        pltpu.VMEM((3, TOK, BAND), F32)],
        semantics=("parallel", "parallel"), carry=carry)


def _attn_bwd(q, k, v, v_col, do, lse, dd, carry=None):
    t = q.shape[0]
    nblk = t // TOK
    offs = [sum(DILATIONS[:g]) * BAND for g in range(len(DILATIONS))]

    def body(q_ref, kp_ref, k_ref, vp_ref, v_ref, do_ref, l_ref, d_ref, dq_ref, dk_ref, dv_ref,
             qs, dos, ks, vs, lsc, dsc, dqb, dkb, dvb, ckb, cvb):
        step = pl.program_id(1)
        i = nblk - 1 - step
        key = lax.broadcasted_iota(jnp.int32, (2 * BAND, 2 * BAND), 0)
        qry = lax.broadcasted_iota(jnp.int32, (2 * BAND, 2 * BAND), 1) & (BAND - 1)
        off = key - qry
        band_ok = jnp.logical_and(off >= 0, off <= BAND)
        lane = lax.broadcasted_iota(jnp.int32, (BAND, BAND), 1)
        head0 = lane < HEAD_DIM
        hm = [jnp.where(head0, 1.0, 0.0).astype(BF16), jnp.where(head0, 0.0, 1.0).astype(BF16)]
        piece = lane & (HEAD_DIM - 1)
        lane2 = lax.broadcasted_iota(jnp.int32, (2 * BAND, BAND), 1) & (HEAD_DIM - 1)
        ones = jnp.where(lane2 < 3, 1.0, 0.0).astype(BF16)

        def pieces(x):
            hi, mid, lo = _split3(-x)
            a = jnp.where(piece == 0, hi, jnp.where(piece == 1, mid, jnp.where(piece == 2, lo, 0.0)))
            return a.astype(BF16)

        for g, d in enumerate(DILATIONS):
            per = TOK // d
            nb = per // BAND
            pad = per + BAND
            _gather([q_ref], qs, d)
            _gather([do_ref], dos, d)
            _gather([l_ref], lsc, d)
            _gather([d_ref], dsc, d)
            _gather([kp_ref, k_ref], ks, d)
            _gather([vp_ref, v_ref], vs, d)
            dkb[...] = jnp.zeros_like(dkb)
            dvb[...] = jnp.zeros_like(dvb)

            def unit(u, c_):
                r, b = u // nb, u % nb
                rows = pl.ds(pl.multiple_of(u * BAND, BAND), BAND)
                qu, dou = qs[rows, :], dos[rows, :]
                la, da = pieces(lsc[rows, :]), pieces(dsc[rows, :])
                q2 = jnp.concatenate([qu * hm[0], qu * hm[1]], axis=0)
                do2 = jnp.concatenate([dou * hm[0], dou * hm[1]], axis=0)
                l2 = jnp.concatenate([la * hm[0], la * hm[1]], axis=0)
                d2 = jnp.concatenate([da * hm[0], da * hm[1]], axis=0)
                start = pl.multiple_of(r * 2 * per + per + (b - 1) * BAND, BAND)
                kw = ks[pl.ds(start, 2 * BAND), :]
                vw = vs[pl.ds(start, 2 * BAND), :]
                lo = jnp.where(jnp.logical_and(i == 0, b == 0), BAND, 0)
                ok = jnp.logical_and(band_ok, key >= lo)
                st = _dot_nt(jnp.concatenate([kw, ones], axis=1), jnp.concatenate([q2, l2], axis=1))
                dpt = _dot_nt(jnp.concatenate([vw, ones], axis=1), jnp.concatenate([do2, d2], axis=1))
                pt = jnp.where(ok, jnp.exp(st), 0.0)
                dst = (pt * dpt).astype(BF16)
                acc = pl.ds(pl.multiple_of(r * pad + b * BAND, BAND), 2 * BAND)
                dkb[acc, :] += _dot(dst, q2)
                dvb[acc, :] += _dot(pt.astype(BF16), do2)
                dq2 = _dot_tn(dst, kw)
                dqb[rows, :] = jnp.where(head0, dq2[:BAND], dq2[BAND:])
                return c_

            lax.fori_loop(0, UNITS, unit, 0, unroll=16)

            for r in range(d):
                last = pl.ds(r * pad + per, BAND)
                kept = pl.ds(offs[g] + r * BAND, BAND)

                @pl.when(step > 0)
                def _():
                    dkb[last, :] += ckb[kept, :]
                    dvb[last, :] += cvb[kept, :]

                ckb[kept, :] = dkb[pl.ds(r * pad, BAND), :]
                cvb[kept, :] = dvb[pl.ds(r * pad, BAND), :]
            _scatter_add(dq_ref, dqb, d, per, 0, g == 0)
            _scatter_add(dk_ref, dkb, d, pad, BAND, g == 0)
            _scatter_add(dv_ref, dvb, d, pad, BAND, g == 0)

    main = pl.BlockSpec((TOK, BAND), lambda j, s: (nblk - 1 - s, j))
    prev = pl.BlockSpec((TOK, BAND), lambda j, s: (jnp.maximum(nblk - 2 - s, 0), j))
    vmain = pl.BlockSpec((TOK, BAND), lambda j, s: (nblk - 1 - s, j + v_col))
    vprev = pl.BlockSpec((TOK, BAND), lambda j, s: (jnp.maximum(nblk - 2 - s, 0), j + v_col))
    acc_rows = max(d * (TOK // d + BAND) for d in DILATIONS)
    kept_rows = sum(DILATIONS) * BAND
    return _call(
        body, [q, k, k, v, v, do, lse, dd], name="attn_bwd",
        grid=(D_ATTN // BAND, nblk), out_shape=[_sds((t, D_ATTN), F32)] * 3,
        in_specs=[main, prev, main, vprev, vmain, main, main, main], out_specs=[main] * 3,
        scratch_shapes=[pltpu.VMEM((TOK, BAND), BF16)] * 2 + [pltpu.VMEM((2 * TOK, BAND), BF16)] * 2
        + [pltpu.VMEM((TOK, BAND), F32)] * 3 + [pltpu.VMEM((acc_rows, BAND), F32)] * 2
        + [pltpu.VMEM((kept_rows, BAND), F32)] * 2,
        semantics=("parallel", "arbitrary"), carry=carry)


def _halo_rows(tm, t):
    per = tm // 8
    prev = lambda i: (jnp.maximum(i * per - 1, 0), 0)
    nxt = lambda i: (jnp.minimum((i + 1) * per, t // 8 - 1), 0)
    return prev, nxt


def _mixer_out(z, cw, y_attn, g_conv, g_attn, tm, carry=None):
    t = z.shape[0]
    prev, _ = _halo_rows(tm, t)

    def body(z_ref, zp_ref, cw_ref, y_ref, gc_ref, ga_ref, mix_ref):
        i = pl.program_id(0)
        keep = jnp.where(i > 0, 1.0, 0.0)
        cu = jnp.concatenate([zp_ref[:, 0:512] * zp_ref[:, 1024:1536] * keep,
                              z_ref[:, 0:512] * z_ref[:, 1024:1536]], axis=0)
        c = (cw_ref[0:1, :] * pltpu.roll(cu, 2, 0) + cw_ref[1:2, :] * pltpu.roll(cu, 1, 0)
             + cw_ref[2:3, :] * cu)[8:, :]
        yc = z_ref[:, 512:1024] * c
        mix_ref[:, 0:512] = (yc * _rms_scale(yc) * gc_ref[...]).astype(BF16)
        ya = y_ref[...]
        mix_ref[:, 512:1024] = (ya * _rms_scale(ya) * ga_ref[...]).astype(BF16)

    blk = pl.BlockSpec((tm, 512), lambda i: (i, 0))
    vec = pl.BlockSpec((1, 512), lambda i: (0, 0))
    return _call(
        body, [z, z, cw, y_attn, g_conv, g_attn], name="mixer_out", grid=(t // tm,),
        out_shape=_sds((t, 1024), BF16),
        in_specs=[pl.BlockSpec((tm, 1536), lambda i: (i, 0)), pl.BlockSpec((8, 1536), prev),
                  pl.BlockSpec((8, 512), lambda i: (0, 0)), blk, vec, vec],
        out_specs=pl.BlockSpec((tm, 1024), lambda i: (i, 0)),
        semantics=("parallel",), carry=carry)


def _mixer_bwd(z, dmix, y_attn, cw, g_conv, g_attn, ones_bd, tm, carry=None):
    t = z.shape[0]
    nblk = t // tm
    prev, nxt = _halo_rows(tm, t)
    e = tm + 16

    def body(z_ref, zp_ref, zn_ref, dm_ref, dmn_ref, y_ref, cw_ref, gc_ref, ga_ref, bd_ref,
             dz_ref, do_ref, dd_ref, dcw_ref, dgc_ref, dga_ref):
        i = pl.program_id(0)
        rows = lax.broadcasted_iota(jnp.int32, (e, 1), 0)
        lo = jnp.where(i > 0, 0, 8)
        hi = jnp.where(i < nblk - 1, e, tm + 8)
        ze = jnp.concatenate([zp_ref[...], z_ref[...], zn_ref[...]], axis=0)
        u, gb, gcv = ze[:, 0:512], ze[:, 512:1024], ze[:, 1024:1536]
        w0, w1, w2 = cw_ref[0:1, :], cw_ref[1:2, :], cw_ref[2:3, :]
        cu = jnp.where(rows >= lo, gcv * u, 0.0)
        cu1, cu2 = pltpu.roll(cu, 1, 0), pltpu.roll(cu, 2, 0)
        c = w0 * cu2 + w1 * cu1 + w2 * cu
        yc = gb * c
        dma = jnp.concatenate([jnp.zeros((8, 512), F32), dm_ref[:, 0:512], dmn_ref[...]], axis=0)
        dyc, ych = _rms_bwd(yc, _rms_scale(yc), gc_ref[...], dma)
        dc = jnp.where(jnp.logical_and(rows >= 8, rows < hi), dyc * gb, 0.0)
        dcu = w0 * pltpu.roll(dc, e - 2, 0) + w1 * pltpu.roll(dc, e - 1, 0) + w2 * dc
        mid = slice(8, 8 + tm)
        dz_ref[:, 0:512] = (dcu * gcv)[mid, :].astype(BF16)
        dz_ref[:, 512:1024] = (dyc * c)[mid, :].astype(BF16)
        dz_ref[:, 1024:1536] = (dcu * u)[mid, :].astype(BF16)

        ya = y_ref[...]
        dmb = dm_ref[:, 512:1024]
        dya, yah = _rms_bwd(ya, _rms_scale(ya), ga_ref[...], dmb)
        do_ref[...] = dya
        dd_ref[...] = _head_sum(dya * ya, bd_ref[...])

        @pl.when(i == 0)
        def _():
            dcw_ref[...] = jnp.zeros_like(dcw_ref)
            dgc_ref[...] = jnp.zeros_like(dgc_ref)
            dga_ref[...] = jnp.zeros_like(dga_ref)

        dcm = jnp.where(rows < tm + 8, dc, 0.0)
        dcw_ref[0:1, :] += jnp.sum(dcm * cu2, axis=0, keepdims=True)
        dcw_ref[1:2, :] += jnp.sum(dcm * cu1, axis=0, keepdims=True)
        dcw_ref[2:3, :] += jnp.sum(dcm * cu, axis=0, keepdims=True)
        dgc_ref[...] += jnp.sum((dma * ych)[mid, :], axis=0, keepdims=True)
        dga_ref[...] += jnp.sum(dmb * yah, axis=0, keepdims=True)

    blk = pl.BlockSpec((tm, 512), lambda i: (i, 0))
    vec = pl.BlockSpec((1, 512), lambda i: (0, 0))
    cwb = pl.BlockSpec((8, 512), lambda i: (0, 0))
    return _call(
        body, [z, z, z, dmix, dmix, y_attn, cw, g_conv, g_attn, ones_bd], name="mixer_bwd",
        grid=(nblk,),
        out_shape=[_sds((t, 1536), BF16), _sds((t, 512), F32), _sds((t, 512), F32),
                   _sds((8, 512), F32), _sds((1, 512), F32), _sds((1, 512), F32)],
        in_specs=[pl.BlockSpec((tm, 1536), lambda i: (i, 0)), pl.BlockSpec((8, 1536), prev),
                  pl.BlockSpec((8, 1536), nxt), pl.BlockSpec((tm, 1024), lambda i: (i, 0)),
                  pl.BlockSpec((8, 512), nxt), blk, cwb, vec, vec,
                  pl.BlockSpec((512, 512), lambda i: (0, 0))],
        out_specs=[pl.BlockSpec((tm, 1536), lambda i: (i, 0)), blk, blk, cwb, vec, vec],
        carry=carry)


def _qkv_bwd(z, dzc, dqn, dkn, dv, gq, gk, ones_bd, tm, carry=None):
    t = z.shape[0]

    def body(zq_ref, zk_ref, dzc_ref, dqn_ref, dkn_ref, dv_ref, gq_ref, gk_ref, bd_ref,
             dz_ref, dgq_ref, dgk_ref):
        bd = bd_ref[...]

        @pl.when(pl.program_id(0) == 0)
        def _():
            dgq_ref[...] = jnp.zeros_like(dgq_ref)
            dgk_ref[...] = jnp.zeros_like(dgk_ref)

        def back(v, dn, g, scale):
            r = _head_rms_scale(v, bd)
            vh = v * r
            dh = dn * (g * scale)
            dv = r * (dh - vh * (_head_sum(dh * vh, bd) * (1.0 / HEAD_DIM)))
            return dv, jnp.sum(dn * scale * vh, axis=0, keepdims=True)

        dq, dgq = back(zq_ref[...], dqn_ref[...], gq_ref[...], HEAD_DIM ** -0.5)
        dk, dgk = back(zk_ref[...], dkn_ref[...], gk_ref[...], 1.0)
        dgq_ref[...] += dgq
        dgk_ref[...] += dgk
        dz_ref[:, 0:1536] = dzc_ref[...]
        dz_ref[:, 1536:2048] = dq.astype(BF16)
        dz_ref[:, 2048:2560] = dk.astype(BF16)
        dz_ref[:, 2560:3072] = dv_ref[...].astype(BF16)

    blk = pl.BlockSpec((tm, 512), lambda i: (i, 0))
    vec = pl.BlockSpec((1, 512), lambda i: (0, 0))
    return _call(
        body, [z, z, dzc, dqn, dkn, dv, gq, gk, ones_bd], name="qkv_bwd", grid=(t // tm,),
        out_shape=[_sds((t, D_IN), BF16), _sds((1, 512), F32), _sds((1, 512), F32)],
        in_specs=[pl.BlockSpec((tm, 512), lambda i: (i, 3)), pl.BlockSpec((tm, 512), lambda i: (i, 4)),
                  pl.BlockSpec((tm, 1536), lambda i: (i, 0))] + [blk] * 3
        + [vec, vec, pl.BlockSpec((512, 512), lambda i: (0, 0))],
        out_specs=[pl.BlockSpec((tm, D_IN), lambda i: (i, 0)), vec, vec],
        carry=carry)


def _columns_from_chips(g):
    return g.transpose(1, 0, 2).reshape(g.shape[1], N_CHIPS * g.shape[2])


def _columns_to_chips(w):
    k, n4 = w.shape
    return w.reshape(k, N_CHIPS, n4 // N_CHIPS).transpose(1, 0, 2)


def kernel(x, g_mix, w_in, conv_w, g_q, g_k, g_conv_out, g_attn_out, w_out, g_ffn, w_gate, w_up, w_down, loss_target, m_g_mix, m_w_in, m_conv_w, m_g_q, m_g_k, m_g_conv_out, m_g_attn_out, m_w_out, m_g_ffn, m_w_gate, m_w_up, m_w_down, v_g_mix, v_w_in, v_conv_w, v_g_q, v_g_k, v_g_conv_out, v_g_attn_out, v_w_out, v_g_ffn, v_w_gate, v_w_up, v_w_down):
    t = x.shape[1]
    xs = x[0]
    target = loss_target[0]
    tm = min(512, t)
    tmm = min(1024, t)

    cw_pad = jnp.pad(conv_w[0], ((0, 13), (0, 0)))
    gathered = _all_gather([w_in[0].astype(BF16), cw_pad])
    win = _columns_from_chips(gathered[0])
    cw = jnp.pad(gathered[1][:, 0:3, :].transpose(1, 0, 2).reshape(3, D_CONV), ((0, 5), (0, 0)))
    later = [w_out[0].astype(BF16), w_gate[0].astype(BF16), w_up[0].astype(BF16),
             w_down[0].astype(BF16)]

    head_id = jnp.arange(D_ATTN) // HEAD_DIM
    ones_bd = (head_id[:, None] == head_id[None, :]).astype(BF16)
    gq_t = jnp.tile(g_q, (1, D_ATTN // HEAD_DIM))
    gk_t = jnp.tile(g_k, (1, D_ATTN // HEAD_DIM))

    h1, z = _norm_matmul("in_proj", xs, g_mix, [win], tm, D_IN, False)
    q, k = _qkv_prepare(z, gq_t, gk_t, ones_bd, tm)
    v_col = (3 * D_CONV + 2 * D_ATTN) // BAND
    (y_attn, lse), gathered = _attn_fwd(q, k, z, v_col, carry=_x_gather_chips(later))
    mix, gathered = _mixer_out(z, cw, y_attn, g_conv_out, g_attn_out, tm,
                               carry=_x_gather_sibling(gathered))
    wout = gathered[0].reshape(D_MODEL, D_MODEL)
    wgate = _columns_from_chips(gathered[1])
    wup = _columns_from_chips(gathered[2])
    wdown = gathered[3].reshape(D_FF, D_MODEL)
    (x1,) = _matmul("out_proj", mix, wout, [xs], [F32], lambda acc, r: (r + acc,), tm, D_MODEL)
    h2, gate, up, act = _norm_matmul("ffn_up", x1, g_ffn, [wgate, wup], tm, D_FF, True, BF16)

    def loss_epilogue(acc, r, tgt):
        err = r + acc - tgt
        dy = err * (1.0 / D_MODEL)
        return dy, dy, jnp.sum(err * err)

    dx2, dx2b, loss_sum = _matmul("ffn_down_loss", act, wdown, [x1, target], [F32, BF16],
                                  loss_epilogue, tm, D_MODEL, loss=True)

    def swiglu_bwd(da, gt, u):
        gt, u = gt.astype(F32), u.astype(F32)
        s = _sigmoid(gt)
        return da * u * (s * (1.0 + gt * (1.0 - s))), da * (gt * s)

    dgate, dup = _matmul("ffn_down_bwd", dx2b, wdown, [gate, up], [BF16, BF16], swiglu_bwd,
                         tm, D_FF, transposed_w=True)
    gw_down = _matmul_tn("grad_w_down", act, dx2b, 512, tmm)
    gw_gate = _matmul_tn("grad_w_gate", h2, dgate, 1408, tmm)
    gw_up = _matmul_tn("grad_w_up", h2, dup, 1408, tmm)

    me = 2 * lax.axis_index("x") + lax.axis_index("y")
    where = jnp.stack([lax.axis_index("c"), me]).astype(jnp.int32)

    def pair_sums(names, full, got):
        return [_pair_sum(f"pair_sum_{nme}", a, b, where) for nme, a, b in zip(names, full, got)]

    def chip_sums(names, pair, got):
        return [_chip_sum(f"chip_sum_{nme}", own, b) for nme, (_, own), b in zip(names, pair, got)]

    ffn = ["w_gate", "w_up", "w_down"]
    full = [_columns_to_chips(gw_gate), _columns_to_chips(gw_up),
            gw_down.reshape(N_CHIPS, D_FF // N_CHIPS, D_MODEL)]
    (dx1, dx1b, gg_ffn), got = _matmul_norm_bwd("ffn_up_bwd", [(dgate, wgate), (dup, wup)], x1, dx2,
                                                g_ffn, tm, carry=_x_pair(full))
    pair = pair_sums(ffn, full, got)
    (dmix,) = _matmul("out_proj_bwd", dx1b, wout, [], [F32], lambda acc: (acc,), tm, D_MODEL,
                      transposed_w=True)
    gw_out = _matmul_tn("grad_w_out", mix, dx1b, 512, tmm)
    full = [gw_out.reshape(N_CHIPS, D_MODEL // N_CHIPS, D_MODEL)]
    (dzc, do, dd, gcw, gg_conv, gg_attn), got = _mixer_bwd(
        z, dmix, y_attn, cw, g_conv_out, g_attn_out, ones_bd, tm, carry=_x_pair(full))
    pair += pair_sums(["w_out"], full, got)
    early = ffn + ["w_out"]
    (dqn, dkn, dv), got = _attn_bwd(q, k, z, v_col, do, lse, dd,
                                    carry=_x_chips([p for p, _ in pair]))
    mine = chip_sums(early, pair, got)
    (dz, gg_q, gg_k), theirs = _qkv_bwd(z, dzc, dqn, dkn, dv, gq_t, gk_t, ones_bd, tm,
                                        carry=_x_share(mine))
    full = [_matmul_tn("grad_w_in", h1, dz, D_IN // N_CHIPS, tmm, by_chip=True)]
    grad_x, _, gg_mix = _matmul_norm_bwd("in_proj_bwd", [(dz, win)], xs, dx1, g_mix, tm)
    got = _exchange_alone("grad_pair_exchange_w_in", _x_pair(full))
    pair = pair_sums(["w_in"], full, got)
    got = _exchange_alone("grad_chip_exchange_w_in", _x_chips([pair[0][0]]))
    mine += chip_sums(["w_in"], pair, got)
    theirs = list(theirs) + list(_exchange_alone("grad_pair_share_w_in", _x_share(mine[-1:])))
    big = early + ["w_in"]

    small = _small_all_reduce({
        "g_mix": gg_mix, "g_ffn": gg_ffn, "g_conv_out": gg_conv, "g_attn_out": gg_attn,
        "g_q": gg_q, "g_k": gg_k, "loss": loss_sum, "conv_w": gcw})
    heads = D_ATTN // HEAD_DIM
    grads = {
        "g_mix": small[0:1, :], "g_ffn": small[1:2, :],
        "g_conv_out": small[2:3, 0:512], "g_attn_out": small[2:3, 512:1024],
        "g_q": small[3, 0:512].reshape(heads, HEAD_DIM).sum(axis=0)[None, :],
        "g_k": small[3, 512:1024].reshape(heads, HEAD_DIM).sum(axis=0)[None, :],
        "conv_w": lax.dynamic_slice(small[8:11, 0:512], (0, me * (D_CONV // N_CHIPS)),
                                    (3, D_CONV // N_CHIPS)),
    }
    halves = dict(zip(big, zip(mine, theirs)))
    loss = small[4, 0] * 0.5 * (1.0 / D_MODEL)

    weights = dict(g_mix=g_mix, w_in=w_in, conv_w=conv_w, g_q=g_q, g_k=g_k, g_conv_out=g_conv_out,
                   g_attn_out=g_attn_out, w_out=w_out, g_ffn=g_ffn, w_gate=w_gate, w_up=w_up,
                   w_down=w_down)
    moments_m = dict(g_mix=m_g_mix, w_in=m_w_in, conv_w=m_conv_w, g_q=m_g_q, g_k=m_g_k,
                     g_conv_out=m_g_conv_out, g_attn_out=m_g_attn_out, w_out=m_w_out, g_ffn=m_g_ffn,
                     w_gate=m_w_gate, w_up=m_w_up, w_down=m_w_down)
    moments_v = dict(g_mix=v_g_mix, w_in=v_w_in, conv_w=v_conv_w, g_q=v_g_q, g_k=v_g_k,
                     g_conv_out=v_g_conv_out, g_attn_out=v_g_attn_out, w_out=v_w_out, g_ffn=v_g_ffn,
                     w_gate=v_w_gate, w_up=v_w_up, w_down=v_w_down)
    names = list(weights)
    out_g, out_d, out_m, out_v = [], [], [], []
    for nme in names:
        wgt = weights[nme]
        shape2 = wgt.shape[-2:] if wgt.ndim == 3 else wgt.shape
        state = (wgt.reshape(shape2), moments_m[nme].reshape(shape2), moments_v[nme].reshape(shape2))
        if nme in halves:
            g2, dlt, nm, nv = _adamw_shard(f"adamw_{nme}", *state, *halves[nme], where)
        else:
            g2 = grads[nme].reshape(shape2)
            dlt, nm, nv = _adamw(f"adamw_{nme}", state[0], g2, state[1], state[2])
        out_g.append(g2.reshape(wgt.shape))
        out_d.append(dlt.reshape(wgt.shape))
        out_m.append(nm.reshape(wgt.shape))
        out_v.append(nv.reshape(wgt.shape))
    return (loss, grad_x[None], *out_g, *out_d, *out_m, *out_v)
```

```python
import functools
from typing import Any, Callable, NamedTuple, Sequence

import jax
import jax.numpy as jnp
from jax import lax
from jax.experimental import pallas as pl
from jax.experimental.pallas import tpu as pltpu

F32 = jnp.float32
BF16 = jnp.bfloat16
MESH = pl.DeviceIdType.MESH

D_MODEL = 1024
D_CONV = 512
D_ATTN = 512
HEAD_DIM = 64
D_FF = 2816
D_IN = 3 * D_CONV + 3 * D_ATTN
DILATIONS = (1, 4, 16)
BAND = 128
EPS = 1e-6
NEG = -1e30
N_CHIPS = 4

ADAM_LR = 0.001
ADAM_B1 = 0.9
ADAM_B2 = 0.999
ADAM_EPS = 1e-08
ADAM_WD = 0.01
ADAM_STEP = 10

V7X_VMEM_BYTES = 64 * 1024 * 1024
VMEM_LIMIT = V7X_VMEM_BYTES - 8 * 1024 * 1024
ANY = pl.BlockSpec(memory_space=pl.ANY)
VMEM_WHOLE = pl.BlockSpec(memory_space=pltpu.VMEM)


def _params(*sem):
    return pltpu.CompilerParams(dimension_semantics=sem, vmem_limit_bytes=VMEM_LIMIT)


def _sds(shape, dtype):
    return jax.ShapeDtypeStruct(shape, dtype)


def _resident(whole):
    return pl.Buffered(1) if whole else None


def _place():
    x, y, c = lax.axis_index("x"), lax.axis_index("y"), lax.axis_index("c")
    chips = [(1 - x, y), (x, 1 - y), (1 - x, 1 - y)]
    return x, y, c, 2 * x + y, chips, [2 * cx + cy for cx, cy in chips]


def _all_gather(shards):
    n = len(shards)

    def body(*refs):
        ins, outs, stage = refs[:n], refs[n:2 * n], refs[2 * n:3 * n]
        ssem, rsem, fsem, gsem, lsem, osem = refs[3 * n:]
        x, y, c, me, chips, cids = _place()
        sib = (x, y, 1 - c)

        def half(w, which):
            h = shards[w].shape[0] // 2
            return pl.ds(pl.multiple_of(which * h, 8), h)

        loads = [pltpu.make_async_copy(ins[w], stage[w], lsem.at[w]) for w in range(n)]
        local = [pltpu.make_async_copy(stage[w], outs[w].at[me], osem.at[w]) for w in range(n)]
        for cp in loads:
            cp.start()

        def chip_copy(w, j, src_slot):
            rows = half(w, c)
            return pltpu.make_async_remote_copy(
                src_ref=ins[w].at[rows], dst_ref=outs[w].at[src_slot, rows],
                send_sem=ssem.at[3 * w + j], recv_sem=rsem.at[3 * w + j],
                device_id=(*chips[j], c), device_id_type=MESH)

        def sib_copy(w, j, which):
            rows = half(w, which)
            return pltpu.make_async_remote_copy(
                src_ref=outs[w].at[cids[j], rows], dst_ref=outs[w].at[cids[j], rows],
                send_sem=fsem.at[3 * w + j], recv_sem=gsem.at[3 * w + j],
                device_id=sib, device_id_type=MESH)

        sends = [chip_copy(w, j, me) for w in range(n) for j in range(3)]
        for cp in sends:
            cp.start()
        for w in range(n):
            loads[w].wait()
            local[w].start()
        passed = []
        for w in range(n):
            for j in range(3):
                chip_copy(w, j, cids[j]).wait_recv()
                cp = sib_copy(w, j, c)
                cp.start()
                passed.append(cp)
        for w in range(n):
            for j in range(3):
                sib_copy(w, j, 1 - c).wait_recv()
        for cp in sends + passed:
            cp.wait_send()
        for cp in local:
            cp.wait()

    return pl.pallas_call(
        body, name="all_gather_weights",
        out_shape=[_sds((N_CHIPS,) + s.shape, s.dtype) for s in shards],
        in_specs=[ANY] * n, out_specs=[ANY] * n,
        scratch_shapes=[pltpu.VMEM(s.shape, s.dtype) for s in shards]
        + [pltpu.SemaphoreType.DMA((3 * n,))] * 4 + [pltpu.SemaphoreType.DMA((n,))] * 2,
        compiler_params=pltpu.CompilerParams(vmem_limit_bytes=VMEM_LIMIT),
    )(*shards)


class _Exchange(NamedTuple):
    srcs: Sequence[Any]
    lands: Sequence[Any]
    outs: Sequence[Any]
    n_sems: int
    copies: Callable


def _remote(src, dst, ssem, rsem, k, to):
    return pltpu.make_async_remote_copy(src_ref=src, dst_ref=dst, send_sem=ssem.at[k],
                                        recv_sem=rsem.at[k], device_id=to, device_id_type=MESH)


def _x_gather_chips(shards):
    def copies(srcs, lands, outs, ssem, rsem):
        _, _, c, me, chips, cids = _place()
        go, arrive = [], []
        for w, s in enumerate(shards):
            h = s.shape[0] // 2
            rows = pl.ds(pl.multiple_of(c * h, 8), h)
            for j in range(3):
                to = (*chips[j], c)
                go.append(_remote(srcs[w].at[rows], lands[w].at[me, rows], ssem, rsem, 3 * w + j, to))
                arrive.append(_remote(srcs[w].at[rows], lands[w].at[cids[j], rows], ssem, rsem,
                                      3 * w + j, to))
        return go, arrive

    lands = [jnp.broadcast_to(s[None], (N_CHIPS,) + s.shape) for s in shards]
    return _Exchange(shards, lands, [], 3 * len(shards), copies)


def _x_gather_sibling(gathered):
    def copies(srcs, lands, outs, ssem, rsem):
        x, y, c, _, _, cids = _place()
        go, arrive = [], []
        for w, g in enumerate(gathered):
            h = g.shape[1] // 2
            mine = pl.ds(pl.multiple_of(c * h, 8), h)
            theirs = pl.ds(pl.multiple_of((1 - c) * h, 8), h)
            for j in range(3):
                slab = lands[w].at[cids[j]]
                go.append(_remote(slab.at[mine], slab.at[mine], ssem, rsem, 3 * w + j, (x, y, 1 - c)))
                arrive.append(_remote(slab.at[theirs], slab.at[theirs], ssem, rsem, 3 * w + j,
                                      (x, y, 1 - c)))
        return go, arrive

    return _Exchange([], gathered, [], 3 * len(gathered), copies)


def _x_pair(grads):
    def copies(srcs, lands, outs, ssem, rsem):
        x, y, c, _, _, _ = _place()
        go = []
        for w, g in enumerate(grads):
            h = g.shape[1] // 2
            theirs = pl.ds(pl.multiple_of((1 - c) * h, 8), h)
            go.append(_remote(srcs[w].at[:, theirs, :], outs[w], ssem, rsem, w, (x, y, 1 - c)))
        return go, go

    outs = [_sds((N_CHIPS, g.shape[1] // 2, g.shape[2]), g.dtype) for g in grads]
    return _Exchange(grads, [], outs, len(grads), copies)


def _x_chips(parts):
    def copies(srcs, lands, outs, ssem, rsem):
        _, _, c, _, chips, cids = _place()
        go = [_remote(srcs[w].at[cids[j]], outs[w].at[j], ssem, rsem, 3 * w + j, (*chips[j], c))
              for w in range(len(parts)) for j in range(3)]
        return go, go

    outs = [_sds((3,) + p.shape[1:], p.dtype) for p in parts]
    return _Exchange(parts, [], outs, 3 * len(parts), copies)


def _x_share(halves):
    def copies(srcs, lands, outs, ssem, rsem):
        x, y, c, _, _, _ = _place()
        go = [_remote(srcs[w], outs[w], ssem, rsem, w, (x, y, 1 - c)) for w in range(len(halves))]
        return go, go

    return _Exchange(halves, [], [_sds(h.shape, h.dtype) for h in halves], len(halves), copies)


def _call(body, args, *, name, grid, in_specs, out_specs, out_shape, scratch_shapes=(),
          semantics=None, carry=None):
    single = not isinstance(out_shape, (list, tuple))
    out_shape = [out_shape] if single else list(out_shape)
    out_specs = [out_specs] if single else list(out_specs)
    if carry is None:
        res = pl.pallas_call(
            body, name=name, grid=grid, in_specs=list(in_specs), out_specs=out_specs,
            out_shape=out_shape, scratch_shapes=list(scratch_shapes),
            compiler_params=_params(*(semantics or ("arbitrary",) * len(grid))))(*args)
        return res[0] if single else res
    n_in, n_out, n_scr = len(args), len(out_shape), len(scratch_shapes)
    n_src, n_land, n_new = len(carry.srcs), len(carry.lands), len(carry.outs)

    def carrying(*refs):
        at = 0
        parts = []
        for n in (n_in, n_src, n_land, n_out, n_land, n_new, n_scr, 2):
            parts.append(refs[at:at + n])
            at += n
        ins, srcs, _, outs, lands, news, scratch, (ssem, rsem) = parts
        ids = [pl.program_id(a) for a in range(len(grid))]
        first = functools.reduce(jnp.logical_and, [i == 0 for i in ids])
        last = functools.reduce(jnp.logical_and, [i == g - 1 for i, g in zip(ids, grid)])
        go, arrive = carry.copies(srcs, lands, news, ssem, rsem)

        @pl.when(first)
        def _():
            for cp in go:
                cp.start()

        body(*ins, *outs, *scratch)

        @pl.when(last)
        def _():
            for cp in go:
                cp.wait_send()
            for cp in arrive:
                cp.wait_recv()

    res = pl.pallas_call(
        carrying, name=name, grid=grid,
        in_specs=list(in_specs) + [ANY] * (n_src + n_land),
        out_specs=out_specs + [ANY] * (n_land + n_new),
        out_shape=out_shape + [_sds(a.shape, a.dtype) for a in carry.lands] + list(carry.outs),
        input_output_aliases={n_in + n_src + i: n_out + i for i in range(n_land)},
        scratch_shapes=list(scratch_shapes) + [pltpu.SemaphoreType.DMA((carry.n_sems,))] * 2,
        compiler_params=_params(*(("arbitrary",) * len(grid))))(*args, *carry.srcs, *carry.lands)
    own = res[:n_out]
    return (own[0] if single else own), res[n_out:]


def _exchange_alone(name, exchange):
    def body(x_ref, o_ref):
        o_ref[...] = x_ref[...]

    blk = pl.BlockSpec((8, 128), lambda i: (0, 0))
    _, res = _call(body, [jnp.zeros((8, 128), F32)], name=name, grid=(1,), in_specs=[blk],
                   out_specs=blk, out_shape=_sds((8, 128), F32), carry=exchange)
    return res


def _row_block(r, want):
    return max(d for d in range(1, min(want, r) + 1) if r % d == 0 and (d % 8 == 0 or d == r))


def _pair_sum(name, full, got, where):
    _, r, n = full.shape
    h = r // 2
    tr = _row_block(h, 256)
    nb = h // tr

    def body(w_ref, a_ref, b_ref, o_ref, own_ref):
        total = a_ref[...] + b_ref[...]
        o_ref[...] = total.astype(BF16)

        @pl.when(pl.program_id(1) == w_ref[1])
        def _():
            own_ref[...] = total[0]

    blk = pl.BlockSpec((1, tr, n), lambda i, s, w: (s, i, 0))
    return pl.pallas_call(
        body, name=name, out_shape=[_sds(got.shape, BF16), _sds((h, n), F32)],
        grid_spec=pltpu.PrefetchScalarGridSpec(
            num_scalar_prefetch=1, grid=(nb, N_CHIPS),
            in_specs=[pl.BlockSpec((1, tr, n), lambda i, s, w: (s, w[0] * nb + i, 0)), blk],
            out_specs=[blk, pl.BlockSpec((tr, n), lambda i, s, w: (i, 0))]),
        compiler_params=_params("parallel", "arbitrary"),
    )(where, full, got)


def _chip_sum(name, own, got):
    h, n = own.shape
    tr = _row_block(h, 256)

    def body(a_ref, b0, b1, b2, o_ref):
        o_ref[...] = ((a_ref[...] + b0[0].astype(F32)) + b1[0].astype(F32)) + b2[0].astype(F32)

    def slot(j):
        return pl.BlockSpec((1, tr, n), lambda i: (j, i, 0))

    blk = pl.BlockSpec((tr, n), lambda i: (i, 0))
    return pl.pallas_call(
        body, name=name, grid=(h // tr,), out_shape=_sds((h, n), F32),
        in_specs=[blk, slot(0), slot(1), slot(2)], out_specs=blk,
        compiler_params=_params("parallel"),
    )(own, got, got, got)


SMALL_ROWS = 16
SMALL_LAYOUT = (
    ("g_mix", 0, 0, 1, 1024), ("g_ffn", 1, 0, 1, 1024), ("g_conv_out", 2, 0, 1, 512),
    ("g_attn_out", 2, 512, 1, 512), ("g_q", 3, 0, 1, 512), ("g_k", 3, 512, 1, 512),
    ("loss", 4, 0, 1, 128), ("conv_w", 8, 0, 8, 512))


def _small_all_reduce(parts):
    names = [s[0] for s in SMALL_LAYOUT]

    def body(*refs):
        ins = refs[:len(names)]
        out_ref, stage, buf, ssem, rsem = refs[len(names):]
        x, y, c, _, _, _ = _place()
        me = 4 * x + 2 * y + c
        stage[...] = jnp.zeros_like(stage)
        for ref, (_, r0, c0, nr, nc) in zip(ins, SMALL_LAYOUT):
            stage[r0:r0 + nr, c0:c0 + nc] = ref[0:nr, :]
        buf[me] = stage[...]
        peers = []
        for d in range(1, 8):
            px = 1 - x if d & 4 else x
            py = 1 - y if d & 2 else y
            pc = 1 - c if d & 1 else c
            peers.append(((px, py, pc), 4 * px + 2 * py + pc))
        sends = [pltpu.make_async_remote_copy(
            src_ref=stage, dst_ref=buf.at[me], send_sem=ssem.at[k], recv_sem=rsem.at[k],
            device_id=peer, device_id_type=MESH) for k, (peer, _) in enumerate(peers)]
        for cp in sends:
            cp.start()
        for k, (peer, pid) in enumerate(peers):
            pltpu.make_async_remote_copy(
                src_ref=stage, dst_ref=buf.at[pid], send_sem=ssem.at[k], recv_sem=rsem.at[k],
                device_id=peer, device_id_type=MESH).wait_recv()
        for cp in sends:
            cp.wait_send()
        acc = buf[0]
        for k in range(1, 8):
            acc = acc + buf[k]
        out_ref[...] = acc

    return pl.pallas_call(
        body, name="small_all_reduce", out_shape=_sds((SMALL_ROWS, 1024), F32),
        in_specs=[VMEM_WHOLE] * len(names), out_specs=VMEM_WHOLE,
        scratch_shapes=[pltpu.VMEM((SMALL_ROWS, 1024), F32), pltpu.VMEM((8, SMALL_ROWS, 1024), F32),
                        pltpu.SemaphoreType.DMA((7,)), pltpu.SemaphoreType.DMA((7,))],
    )(*[parts[k] for k in names])


def _dot(a, b):
    return jnp.dot(a, b, preferred_element_type=F32)


def _dot_nt(a, b):
    return lax.dot_general(a, b, (((1,), (1,)), ((), ())), preferred_element_type=F32)


def _dot_tn(a, b):
    return lax.dot_general(a, b, (((0,), (0,)), ((), ())), preferred_element_type=F32)


def _sigmoid(v):
    return 1.0 / (1.0 + jnp.exp(-v))


def _rms_scale(v):
    return lax.rsqrt(jnp.mean(v * v, axis=-1, keepdims=True) + EPS)


def _rms_bwd(v, r, g, dy):
    vh = v * r
    dh = dy * g
    return r * (dh - vh * jnp.mean(dh * vh, axis=-1, keepdims=True)), vh


def _head_sum(a, ones_bd):
    hi = a.astype(BF16)
    lo = (a - hi.astype(F32)).astype(BF16)
    return _dot(hi, ones_bd) + _dot(lo, ones_bd)


def _head_rms_scale(v, ones_bd):
    return lax.rsqrt(_head_sum(v * v, ones_bd) * (1.0 / HEAD_DIM) + EPS)


MXU_COLUMNS = 256


def _column_chunks(n):
    width = MXU_COLUMNS if n % MXU_COLUMNS == 0 else n
    return [slice(c, c + width) for c in range(0, n, width)]


def _norm_matmul(name, x, g, ws, tm, tn, swiglu, out_dtype=F32, transposed_w=False):
    t, d = x.shape
    n = ws[0].shape[0] if transposed_w else ws[0].shape[1]
    nw = len(ws)

    def body(x_ref, g_ref, *refs):
        w_refs, h_ref, o_refs = refs[:nw], refs[nw], refs[nw + 1:2 * nw + 1]
        hs = refs[-1]

        @pl.when(pl.program_id(1) == 0)
        def _():
            xv = x_ref[...]
            h = (xv * _rms_scale(xv) * g_ref[...]).astype(BF16)
            hs[...] = h
            h_ref[...] = h

        h = hs[...]
        for cols in _column_chunks(tn):
            outs = [_dot_nt(h, w[cols, :]) if transposed_w else _dot(h, w[:, cols]) for w in w_refs]
            for o_ref, o in zip(o_refs, outs):
                o_ref[:, cols] = o.astype(out_dtype)
            if swiglu:
                refs[2 * nw + 1][:, cols] = (outs[0] * _sigmoid(outs[0]) * outs[1]).astype(BF16)

    row = pl.BlockSpec((tm, d), lambda i, j: (i, 0))
    col = pl.BlockSpec((tm, tn), lambda i, j: (i, j))
    out_shape = [_sds((t, d), BF16)] + [_sds((t, n), out_dtype)] * nw
    out_specs = [row] + [col] * nw
    if swiglu:
        out_shape.append(_sds((t, n), BF16))
        out_specs.append(col)
    return pl.pallas_call(
        body, name=name, grid=(t // tm, n // tn), out_shape=out_shape,
        in_specs=[row, pl.BlockSpec((1, d), lambda i, j: (0, 0))]
        + [pl.BlockSpec((tn, d), lambda i, j: (j, 0), pipeline_mode=_resident(tn == n))
           if transposed_w
           else pl.BlockSpec((d, tn), lambda i, j: (0, j), pipeline_mode=_resident(tn == n))] * nw,
        out_specs=out_specs, scratch_shapes=[pltpu.VMEM((tm, d), BF16)],
        compiler_params=_params("parallel", "arbitrary"),
    )(x, g, *ws)


def _matmul(name, a, w, extras, out_dtypes, epilogue, tm, tn, transposed_w=False, loss=False):
    t, k = a.shape
    n = w.shape[0] if transposed_w else w.shape[1]
    ne, no = len(extras), len(out_dtypes)

    def body(a_ref, w_ref, *refs):
        e_refs, o_refs = refs[:ne], refs[ne:]
        a = a_ref[...]
        total = 0.0
        for cols in _column_chunks(tn):
            acc = _dot_nt(a, w_ref[cols, :]) if transposed_w else _dot(a, w_ref[:, cols])
            res = epilogue(acc, *[e[:, cols] for e in e_refs])
            for o_ref, r in zip(o_refs[:no], res[:no]):
                o_ref[:, cols] = r.astype(o_ref.dtype)
            if loss:
                total = total + res[no]
        if loss:
            first = jnp.logical_and(pl.program_id(0) == 0, pl.program_id(1) == 0)

            @pl.when(first)
            def _():
                o_refs[no][...] = jnp.zeros_like(o_refs[no])

            o_refs[no][...] += total

    col = pl.BlockSpec((tm, tn), lambda i, j: (i, j))
    w_spec = (pl.BlockSpec((tn, k), lambda i, j: (j, 0), pipeline_mode=_resident(tn == n))
              if transposed_w
              else pl.BlockSpec((k, tn), lambda i, j: (0, j), pipeline_mode=_resident(tn == n)))
    out_shape = [_sds((t, n), dt) for dt in out_dtypes]
    out_specs = [col] * no
    if loss:
        out_shape.append(_sds((8, 128), F32))
        out_specs.append(pl.BlockSpec((8, 128), lambda i, j: (0, 0)))
    return pl.pallas_call(
        body, name=name, grid=(t // tm, n // tn), out_shape=out_shape,
        in_specs=[pl.BlockSpec((tm, k), lambda i, j: (i, 0)), w_spec] + [col] * ne,
        out_specs=out_specs,
        compiler_params=_params(*(("arbitrary", "arbitrary") if loss else ("parallel", "parallel"))),
    )(a, w, *extras)


def _matmul_norm_bwd(name, pairs, x, dres, g, tm, carry=None, transposed_w=True):
    t, d = x.shape
    npairs = len(pairs)
    product = _dot_nt if transposed_w else _dot

    def body(*refs):
        a_refs, w_refs = refs[:npairs], refs[npairs:2 * npairs]
        x_ref, r_ref, g_ref, dx_ref, dxb_ref, dg_ref = refs[2 * npairs:]
        dy = product(a_refs[0][...], w_refs[0][...])
        for a_ref, w_ref in zip(a_refs[1:], w_refs[1:]):
            dy = dy + product(a_ref[...], w_ref[...])
        xv = x_ref[...]
        dx, xh = _rms_bwd(xv, _rms_scale(xv), g_ref[...], dy)
        dx = dx + r_ref[...]
        dx_ref[...] = dx
        dxb_ref[...] = dx.astype(BF16)

        @pl.when(pl.program_id(0) == 0)
        def _():
            dg_ref[...] = jnp.zeros_like(dg_ref)

        dg_ref[...] += jnp.sum(dy * xh, axis=0, keepdims=True)

    row = pl.BlockSpec((tm, d), lambda i: (i, 0))
    vec = pl.BlockSpec((1, d), lambda i: (0, 0))
    return _call(
        body, [a for a, _ in pairs] + [w for _, w in pairs] + [x, dres, g], name=name,
        grid=(t // tm,), out_shape=[_sds((t, d), F32), _sds((t, d), BF16), _sds((1, d), F32)],
        in_specs=[pl.BlockSpec((tm, a.shape[1]), lambda i: (i, 0)) for a, _ in pairs]
        + [pl.BlockSpec(w.shape, lambda i: (0, 0), pipeline_mode=pl.Buffered(1)) for _, w in pairs]
        + [row, row, vec],
        out_specs=[row, row, vec], carry=carry)


def _matmul_tn(name, a, g, tn, tk, by_chip=False):
    t, ka = a.shape
    n = g.shape[1]

    def body(a_ref, g_ref, o_ref):
        @pl.when(pl.program_id(1) == 0)
        def _():
            o_ref[...] = jnp.zeros_like(o_ref)

        acc = _dot_tn(a_ref[...], g_ref[...])
        o_ref[...] += acc[None] if by_chip else acc

    return pl.pallas_call(
        body, name=name, grid=(n // tn, t // tk),
        out_shape=_sds((n // tn, ka, tn) if by_chip else (ka, n), F32),
        in_specs=[pl.BlockSpec((tk, ka), lambda j, s: (s, 0)),
                  pl.BlockSpec((tk, tn), lambda j, s: (s, j))],
        out_specs=(pl.BlockSpec((1, ka, tn), lambda j, s: (j, 0, 0)) if by_chip
                   else pl.BlockSpec((ka, tn), lambda j, s: (0, j))),
        compiler_params=_params("parallel", "arbitrary"),
    )(a, g)


def _elementwise(name, fn, ins, out_dtypes, tr):
    r, n = ins[0].shape
    tr = _row_block(r, tr)
    ni = len(ins)

    def body(*refs):
        res = fn(*[ref[...] for ref in refs[:ni]])
        for o_ref, v in zip(refs[ni:], res):
            o_ref[...] = v.astype(o_ref.dtype)

    blk = pl.BlockSpec((tr, n), lambda i: (i, 0))
    return pl.pallas_call(
        body, name=name, grid=(r // tr,), out_shape=[_sds((r, n), dt) for dt in out_dtypes],
        in_specs=[blk] * ni, out_specs=[blk] * len(out_dtypes),
        compiler_params=_params("parallel"),
    )(*ins)


def _adamw_update(w, g, m, v):
    m = ADAM_B1 * m + (1.0 - ADAM_B1) * g
    v = ADAM_B2 * v + (1.0 - ADAM_B2) * (g * g)
    m_hat = m / (1.0 - ADAM_B1 ** ADAM_STEP)
    v_hat = v / (1.0 - ADAM_B2 ** ADAM_STEP)
    return -ADAM_LR * (m_hat / (jnp.sqrt(v_hat) + ADAM_EPS) + ADAM_WD * w), m, v


def _adamw(name, w, g, m, v):
    return _elementwise(name, _adamw_update, [w, g, m, v], [F32] * 3, 256)


def _adamw_shard(name, w, m, v, mine, theirs, where):
    r, n = w.shape
    h = r // 2
    tr = _row_block(h, 256)
    nb = h // tr

    def body(w_ref, p_ref, m_ref, v_ref, a_ref, b_ref, g_ref, d_ref, nm_ref, nv_ref):
        g = jnp.where(pl.program_id(0) == w_ref[0], a_ref[...], b_ref[...])
        g_ref[...] = g
        d_ref[...], nm_ref[...], nv_ref[...] = _adamw_update(p_ref[...], g, m_ref[...], v_ref[...])

    whole = pl.BlockSpec((tr, n), lambda s, i, c: (s * nb + i, 0))
    half = pl.BlockSpec((tr, n), lambda s, i, c: (i, 0))
    return pl.pallas_call(
        body, name=name, out_shape=[_sds((r, n), F32)] * 4,
        grid_spec=pltpu.PrefetchScalarGridSpec(
            num_scalar_prefetch=1, grid=(2, nb), in_specs=[whole] * 3 + [half] * 2,
            out_specs=[whole] * 4),
        compiler_params=_params("parallel", "parallel"),
    )(where, w, m, v, mine, theirs)


def _qkv_prepare(z, gq, gk, ones_bd, tm):
    t = z.shape[0]

    def body(zq_ref, zk_ref, gq_ref, gk_ref, bd_ref, q_ref, k_ref):
        bd = bd_ref[...]
        q = zq_ref[...]
        k = zk_ref[...]
        q_ref[...] = (q * _head_rms_scale(q, bd) * gq_ref[...]) * HEAD_DIM ** -0.5
        k_ref[...] = k * _head_rms_scale(k, bd) * gk_ref[...]

    vec = pl.BlockSpec((1, 512), lambda i: (0, 0))
    out = pl.BlockSpec((tm, 512), lambda i: (i, 0))
    return pl.pallas_call(
        body, name="qkv_prepare", grid=(t // tm,), out_shape=[_sds((t, 512), F32)] * 2,
        in_specs=[pl.BlockSpec((tm, 512), lambda i: (i, 3)), pl.BlockSpec((tm, 512), lambda i: (i, 4)),
                  vec, vec, pl.BlockSpec((512, 512), lambda i: (0, 0))],
        out_specs=[out] * 2, compiler_params=_params("parallel"),
    )(z, z, gq, gk, ones_bd)


TOK = 2048
UNITS = TOK // BAND


def _stack_masks():
    row = lax.broadcasted_iota(jnp.int32, (2 * BAND, 2 * BAND), 0) & (BAND - 1)
    col = lax.broadcasted_iota(jnp.int32, (2 * BAND, 2 * BAND), 1)
    lane = lax.broadcasted_iota(jnp.int32, (BAND, BAND), 1)
    head0 = lane < HEAD_DIM
    ones = [jnp.where(head0, 1.0, 0.0).astype(BF16), jnp.where(head0, 0.0, 1.0).astype(BF16)]
    return col - row, col, head0, ones


def _split3(x):
    hi = x.astype(BF16).astype(F32)
    mid = (x - hi).astype(BF16).astype(F32)
    return hi, mid, x - hi - mid


def _gather(srcs, dst, d):
    per = TOK // d
    at = 0
    for r in range(d):
        for src in srcs:
            rows = src[pl.ds(r, per, stride=d), :] if d > 1 else src[...]
            dst[pl.ds(at, per), :] = rows.astype(dst.dtype)
            at += per


def _scatter_add(out_ref, src, d, per_src, offset, first):
    per = TOK // d
    if d == 1:
        val = src[pl.ds(offset, per), :]
        out_ref[...] = val if first else out_ref[...] + val
        return
    for r in range(d):
        val = src[pl.ds(r * per_src + offset, per), :]
        idx = pl.ds(r, per, stride=d)
        out_ref[idx, :] = val if first else out_ref[idx, :] + val


def _attn_fwd(q, k, v, v_col, carry=None):
    t = q.shape[0]
    nblk = t // TOK

    def body(q_ref, kp_ref, k_ref, vp_ref, v_ref, y_ref, l_ref, qs, ks, vs, ob, lb, on, ln):
        i = pl.program_id(1)
        diff, col, head0, hm = _stack_masks()
        band_ok = jnp.logical_and(diff >= 0, diff <= BAND)
        for g, d in enumerate(DILATIONS):
            per = TOK // d
            nb = per // BAND
            _gather([q_ref], qs, d)
            _gather([kp_ref, k_ref], ks, d)
            _gather([vp_ref, v_ref], vs, d)

            def unit(u, carry):
                r, b = u // nb, u % nb
                qu = qs[pl.ds(pl.multiple_of(u * BAND, BAND), BAND), :]
                start = pl.multiple_of(r * 2 * per + per + (b - 1) * BAND, BAND)
                kw = ks[pl.ds(start, 2 * BAND), :]
                vw = vs[pl.ds(start, 2 * BAND), :]
                lo = jnp.where(jnp.logical_and(i == 0, b == 0), BAND, 0)
                s = _dot_nt(jnp.concatenate([qu * hm[0], qu * hm[1]], axis=0), kw)
                s = jnp.where(jnp.logical_and(band_ok, col >= lo), s, NEG)
                mx = jnp.max(s, axis=-1, keepdims=True)
                e = jnp.exp(s - mx)
                den = jnp.sum(e, axis=-1, keepdims=True)
                o2 = _dot(e.astype(BF16), vw) / den
                l2 = jnp.broadcast_to(mx + jnp.log(den), (2 * BAND, BAND))
                rows = pl.ds(pl.multiple_of(u * BAND, BAND), BAND)
                ob[rows, :] = jnp.where(head0, o2[:BAND], o2[BAND:])
                lb[rows, :] = jnp.where(head0, l2[:BAND], l2[BAND:])
                return carry

            lax.fori_loop(0, UNITS, unit, 0, unroll=16)
            _scatter_add(on.at[g], ob, d, per, 0, True)
            _scatter_add(ln.at[g], lb, d, per, 0, True)
        ls = [ln[0], ln[1], ln[2]]
        mx = jnp.maximum(jnp.maximum(ls[0], ls[1]), ls[2])
        es = [jnp.exp(l - mx) for l in ls]
        tot = es[0] + es[1] + es[2]
        y_ref[...] = (es[0] * on[0] + es[1] * on[1] + es[2] * on[2]) / tot
        l_ref[...] = mx + jnp.log(tot)

    main = pl.BlockSpec((TOK, BAND), lambda j, i: (i, j))
    prev = pl.BlockSpec((TOK, BAND), lambda j, i: (jnp.maximum(i - 1, 0), j))
    vmain = pl.BlockSpec((TOK, BAND), lambda j, i: (i, j + v_col))
    vprev = pl.BlockSpec((TOK, BAND), lambda j, i: (jnp.maximum(i - 1, 0), j + v_col))
    return _call(
        body, [q, k, k, v, v], name="attn_fwd", grid=(D_ATTN // BAND, nblk),
        out_shape=[_sds((t, D_ATTN), F32)] * 2,
        in_specs=[main, prev, main, vprev, vmain], out_specs=[main, main],
        scratch_shapes=[pltpu.VMEM((TOK, BAND), BF16), pltpu.VMEM((2 * TOK, BAND), BF16),
                        pltpu.VMEM((2 * TOK, BAND), BF16), pltpu.VMEM((TOK, BAND), F32),
                        pltpu.VMEM((TOK, BAND), F32), pltpu.VMEM((3, TOK, BAND), F32),
                        pltpu.VMEM((3, TOK, BAND), F32)],
        semantics=("parallel", "parallel"), carry=carry)


def _attn_bwd(q, k, v, v_col, do, lse, dd, carry=None):
    t = q.shape[0]
    nblk = t // TOK
    offs = [sum(DILATIONS[:g]) * BAND for g in range(len(DILATIONS))]

    def body(q_ref, kp_ref, k_ref, vp_ref, v_ref, do_ref, l_ref, d_ref, dq_ref, dk_ref, dv_ref,
             qs, dos, ks, vs, lsc, dsc, dqb, dkb, dvb, ckb, cvb):
        step = pl.program_id(1)
        i = nblk - 1 - step
        key = lax.broadcasted_iota(jnp.int32, (2 * BAND, 2 * BAND), 0)
        qry = lax.broadcasted_iota(jnp.int32, (2 * BAND, 2 * BAND), 1) & (BAND - 1)
        off = key - qry
        band_ok = jnp.logical_and(off >= 0, off <= BAND)
        lane = lax.broadcasted_iota(jnp.int32, (BAND, BAND), 1)
        head0 = lane < HEAD_DIM
        hm = [jnp.where(head0, 1.0, 0.0).astype(BF16), jnp.where(head0, 0.0, 1.0).astype(BF16)]
        piece = lane & (HEAD_DIM - 1)
        lane2 = lax.broadcasted_iota(jnp.int32, (2 * BAND, BAND), 1) & (HEAD_DIM - 1)
        ones = jnp.where(lane2 < 3, 1.0, 0.0).astype(BF16)

        def pieces(x):
            hi, mid, lo = _split3(-x)
            a = jnp.where(piece == 0, hi, jnp.where(piece == 1, mid, jnp.where(piece == 2, lo, 0.0)))
            return a.astype(BF16)

        for g, d in enumerate(DILATIONS):
            per = TOK // d
            nb = per // BAND
            pad = per + BAND
            _gather([q_ref], qs, d)
            _gather([do_ref], dos, d)
            _gather([l_ref], lsc, d)
            _gather([d_ref], dsc, d)
            _gather([kp_ref, k_ref], ks, d)
            _gather([vp_ref, v_ref], vs, d)
            dkb[...] = jnp.zeros_like(dkb)
            dvb[...] = jnp.zeros_like(dvb)

            def unit(u, c_):
                r, b = u // nb, u % nb
                rows = pl.ds(pl.multiple_of(u * BAND, BAND), BAND)
                qu, dou = qs[rows, :], dos[rows, :]
                la, da = pieces(lsc[rows, :]), pieces(dsc[rows, :])
                q2 = jnp.concatenate([qu * hm[0], qu * hm[1]], axis=0)
                do2 = jnp.concatenate([dou * hm[0], dou * hm[1]], axis=0)
                l2 = jnp.concatenate([la * hm[0], la * hm[1]], axis=0)
                d2 = jnp.concatenate([da * hm[0], da * hm[1]], axis=0)
                start = pl.multiple_of(r * 2 * per + per + (b - 1) * BAND, BAND)
                kw = ks[pl.ds(start, 2 * BAND), :]
                vw = vs[pl.ds(start, 2 * BAND), :]
                lo = jnp.where(jnp.logical_and(i == 0, b == 0), BAND, 0)
                ok = jnp.logical_and(band_ok, key >= lo)
                st = _dot_nt(jnp.concatenate([kw, ones], axis=1), jnp.concatenate([q2, l2], axis=1))
                dpt = _dot_nt(jnp.concatenate([vw, ones], axis=1), jnp.concatenate([do2, d2], axis=1))
                pt = jnp.where(ok, jnp.exp(st), 0.0)
                dst = (pt * dpt).astype(BF16)
                acc = pl.ds(pl.multiple_of(r * pad + b * BAND, BAND), 2 * BAND)
                dkb[acc, :] += _dot(dst, q2)
                dvb[acc, :] += _dot(pt.astype(BF16), do2)
                dq2 = _dot_tn(dst, kw)
                dqb[rows, :] = jnp.where(head0, dq2[:BAND], dq2[BAND:])
                return c_

            lax.fori_loop(0, UNITS, unit, 0, unroll=16)

            for r in range(d):
                last = pl.ds(r * pad + per, BAND)
                kept = pl.ds(offs[g] + r * BAND, BAND)

                @pl.when(step > 0)
                def _():
                    dkb[last, :] += ckb[kept, :]
                    dvb[last, :] += cvb[kept, :]

                ckb[kept, :] = dkb[pl.ds(r * pad, BAND), :]
                cvb[kept, :] = dvb[pl.ds(r * pad, BAND), :]
            _scatter_add(dq_ref, dqb, d, per, 0, g == 0)
            _scatter_add(dk_ref, dkb, d, pad, BAND, g == 0)
            _scatter_add(dv_ref, dvb, d, pad, BAND, g == 0)

    main = pl.BlockSpec((TOK, BAND), lambda j, s: (nblk - 1 - s, j))
    prev = pl.BlockSpec((TOK, BAND), lambda j, s: (jnp.maximum(nblk - 2 - s, 0), j))
    vmain = pl.BlockSpec((TOK, BAND), lambda j, s: (nblk - 1 - s, j + v_col))
    vprev = pl.BlockSpec((TOK, BAND), lambda j, s: (jnp.maximum(nblk - 2 - s, 0), j + v_col))
    acc_rows = max(d * (TOK // d + BAND) for d in DILATIONS)
    kept_rows = sum(DILATIONS) * BAND
    return _call(
        body, [q, k, k, v, v, do, lse, dd], name="attn_bwd",
        grid=(D_ATTN // BAND, nblk), out_shape=[_sds((t, D_ATTN), F32)] * 3,
        in_specs=[main, prev, main, vprev, vmain, main, main, main], out_specs=[main] * 3,
        scratch_shapes=[pltpu.VMEM((TOK, BAND), BF16)] * 2 + [pltpu.VMEM((2 * TOK, BAND), BF16)] * 2
        + [pltpu.VMEM((TOK, BAND), F32)] * 3 + [pltpu.VMEM((acc_rows, BAND), F32)] * 2
        + [pltpu.VMEM((kept_rows, BAND), F32)] * 2,
        semantics=("parallel", "arbitrary"), carry=carry)


def _halo_rows(tm, t):
    per = tm // 8
    prev = lambda i: (jnp.maximum(i * per - 1, 0), 0)
    nxt = lambda i: (jnp.minimum((i + 1) * per, t // 8 - 1), 0)
    return prev, nxt


def _mixer_out(z, cw, y_attn, g_conv, g_attn, tm, carry=None):
    t = z.shape[0]
    prev, _ = _halo_rows(tm, t)

    def body(z_ref, zp_ref, cw_ref, y_ref, gc_ref, ga_ref, mix_ref):
        i = pl.program_id(0)
        keep = jnp.where(i > 0, 1.0, 0.0)
        cu = jnp.concatenate([zp_ref[:, 0:512] * zp_ref[:, 1024:1536] * keep,
                              z_ref[:, 0:512] * z_ref[:, 1024:1536]], axis=0)
        c = (cw_ref[0:1, :] * pltpu.roll(cu, 2, 0) + cw_ref[1:2, :] * pltpu.roll(cu, 1, 0)
             + cw_ref[2:3, :] * cu)[8:, :]
        yc = z_ref[:, 512:1024] * c
        mix_ref[:, 0:512] = (yc * _rms_scale(yc) * gc_ref[...]).astype(BF16)
        ya = y_ref[...]
        mix_ref[:, 512:1024] = (ya * _rms_scale(ya) * ga_ref[...]).astype(BF16)

    blk = pl.BlockSpec((tm, 512), lambda i: (i, 0))
    vec = pl.BlockSpec((1, 512), lambda i: (0, 0))
    return _call(
        body, [z, z, cw, y_attn, g_conv, g_attn], name="mixer_out", grid=(t // tm,),
        out_shape=_sds((t, 1024), BF16),
        in_specs=[pl.BlockSpec((tm, 1536), lambda i: (i, 0)), pl.BlockSpec((8, 1536), prev),
                  pl.BlockSpec((8, 512), lambda i: (0, 0)), blk, vec, vec],
        out_specs=pl.BlockSpec((tm, 1024), lambda i: (i, 0)),
        semantics=("parallel",), carry=carry)


def _mixer_bwd(z, dmix, y_attn, cw, g_conv, g_attn, ones_bd, tm, carry=None):
    t = z.shape[0]
    nblk = t // tm
    prev, nxt = _halo_rows(tm, t)
    e = tm + 16

    def body(z_ref, zp_ref, zn_ref, dm_ref, dmn_ref, y_ref, cw_ref, gc_ref, ga_ref, bd_ref,
             dz_ref, do_ref, dd_ref, dcw_ref, dgc_ref, dga_ref):
        i = pl.program_id(0)
        rows = lax.broadcasted_iota(jnp.int32, (e, 1), 0)
        lo = jnp.where(i > 0, 0, 8)
        hi = jnp.where(i < nblk - 1, e, tm + 8)
        ze = jnp.concatenate([zp_ref[...], z_ref[...], zn_ref[...]], axis=0)
        u, gb, gcv = ze[:, 0:512], ze[:, 512:1024], ze[:, 1024:1536]
        w0, w1, w2 = cw_ref[0:1, :], cw_ref[1:2, :], cw_ref[2:3, :]
        cu = jnp.where(rows >= lo, gcv * u, 0.0)
        cu1, cu2 = pltpu.roll(cu, 1, 0), pltpu.roll(cu, 2, 0)
        c = w0 * cu2 + w1 * cu1 + w2 * cu
        yc = gb * c
        dma = jnp.concatenate([jnp.zeros((8, 512), F32), dm_ref[:, 0:512], dmn_ref[...]], axis=0)
        dyc, ych = _rms_bwd(yc, _rms_scale(yc), gc_ref[...], dma)
        dc = jnp.where(jnp.logical_and(rows >= 8, rows < hi), dyc * gb, 0.0)
        dcu = w0 * pltpu.roll(dc, e - 2, 0) + w1 * pltpu.roll(dc, e - 1, 0) + w2 * dc
        mid = slice(8, 8 + tm)
        dz_ref[:, 0:512] = (dcu * gcv)[mid, :].astype(BF16)
        dz_ref[:, 512:1024] = (dyc * c)[mid, :].astype(BF16)
        dz_ref[:, 1024:1536] = (dcu * u)[mid, :].astype(BF16)

        ya = y_ref[...]
        dmb = dm_ref[:, 512:1024]
        dya, yah = _rms_bwd(ya, _rms_scale(ya), ga_ref[...], dmb)
        do_ref[...] = dya
        dd_ref[...] = _head_sum(dya * ya, bd_ref[...])

        @pl.when(i == 0)
        def _():
            dcw_ref[...] = jnp.zeros_like(dcw_ref)
            dgc_ref[...] = jnp.zeros_like(dgc_ref)
            dga_ref[...] = jnp.zeros_like(dga_ref)

        dcm = jnp.where(rows < tm + 8, dc, 0.0)
        dcw_ref[0:1, :] += jnp.sum(dcm * cu2, axis=0, keepdims=True)
        dcw_ref[1:2, :] += jnp.sum(dcm * cu1, axis=0, keepdims=True)
        dcw_ref[2:3, :] += jnp.sum(dcm * cu, axis=0, keepdims=True)
        dgc_ref[...] += jnp.sum((dma * ych)[mid, :], axis=0, keepdims=True)
        dga_ref[...] += jnp.sum(dmb * yah, axis=0, keepdims=True)

    blk = pl.BlockSpec((tm, 512), lambda i: (i, 0))
    vec = pl.BlockSpec((1, 512), lambda i: (0, 0))
    cwb = pl.BlockSpec((8, 512), lambda i: (0, 0))
    return _call(
        body, [z, z, z, dmix, dmix, y_attn, cw, g_conv, g_attn, ones_bd], name="mixer_bwd",
        grid=(nblk,),
        out_shape=[_sds((t, 1536), BF16), _sds((t, 512), F32), _sds((t, 512), F32),
                   _sds((8, 512), F32), _sds((1, 512), F32), _sds((1, 512), F32)],
        in_specs=[pl.BlockSpec((tm, 1536), lambda i: (i, 0)), pl.BlockSpec((8, 1536), prev),
                  pl.BlockSpec((8, 1536), nxt), pl.BlockSpec((tm, 1024), lambda i: (i, 0)),
                  pl.BlockSpec((8, 512), nxt), blk, cwb, vec, vec,
                  pl.BlockSpec((512, 512), lambda i: (0, 0))],
        out_specs=[pl.BlockSpec((tm, 1536), lambda i: (i, 0)), blk, blk, cwb, vec, vec],
        carry=carry)


def _qkv_bwd(z, dzc, dqn, dkn, dv, gq, gk, ones_bd, tm, carry=None):
    t = z.shape[0]

    def body(zq_ref, zk_ref, dzc_ref, dqn_ref, dkn_ref, dv_ref, gq_ref, gk_ref, bd_ref,
             dz_ref, dgq_ref, dgk_ref):
        bd = bd_ref[...]

        @pl.when(pl.program_id(0) == 0)
        def _():
            dgq_ref[...] = jnp.zeros_like(dgq_ref)
            dgk_ref[...] = jnp.zeros_like(dgk_ref)

        def back(v, dn, g, scale):
            r = _head_rms_scale(v, bd)
            vh = v * r
            dh = dn * (g * scale)
            dv = r * (dh - vh * (_head_sum(dh * vh, bd) * (1.0 / HEAD_DIM)))
            return dv, jnp.sum(dn * scale * vh, axis=0, keepdims=True)

        dq, dgq = back(zq_ref[...], dqn_ref[...], gq_ref[...], HEAD_DIM ** -0.5)
        dk, dgk = back(zk_ref[...], dkn_ref[...], gk_ref[...], 1.0)
        dgq_ref[...] += dgq
        dgk_ref[...] += dgk
        dz_ref[:, 0:1536] = dzc_ref[...]
        dz_ref[:, 1536:2048] = dq.astype(BF16)
        dz_ref[:, 2048:2560] = dk.astype(BF16)
        dz_ref[:, 2560:3072] = dv_ref[...].astype(BF16)

    blk = pl.BlockSpec((tm, 512), lambda i: (i, 0))
    vec = pl.BlockSpec((1, 512), lambda i: (0, 0))
    return _call(
        body, [z, z, dzc, dqn, dkn, dv, gq, gk, ones_bd], name="qkv_bwd", grid=(t // tm,),
        out_shape=[_sds((t, D_IN), BF16), _sds((1, 512), F32), _sds((1, 512), F32)],
        in_specs=[pl.BlockSpec((tm, 512), lambda i: (i, 3)), pl.BlockSpec((tm, 512), lambda i: (i, 4)),
                  pl.BlockSpec((tm, 1536), lambda i: (i, 0))] + [blk] * 3
        + [vec, vec, pl.BlockSpec((512, 512), lambda i: (0, 0))],
        out_specs=[pl.BlockSpec((tm, D_IN), lambda i: (i, 0)), vec, vec],
        carry=carry)


def _columns_from_chips(g):
    return g.transpose(1, 0, 2).reshape(g.shape[1], N_CHIPS * g.shape[2])


def kernel(x, g_mix, w_in, conv_w, g_q, g_k, g_conv_out, g_attn_out, w_out, g_ffn, w_gate, w_up, w_down, loss_target, m_g_mix, m_w_in, m_conv_w, m_g_q, m_g_k, m_g_conv_out, m_g_attn_out, m_w_out, m_g_ffn, m_w_gate, m_w_up, m_w_down, v_g_mix, v_w_in, v_conv_w, v_g_q, v_g_k, v_g_conv_out, v_g_attn_out, v_w_out, v_g_ffn, v_w_gate, v_w_up, v_w_down):
    t = x.shape[1]
    xs = x[0]
    target = loss_target[0]
    tm = min(512, t)
    tmm = min(1024, t)

    cw_pad = jnp.pad(conv_w[0], ((0, 13), (0, 0)))
    gathered = _all_gather([w_in[0].astype(BF16), cw_pad])
    win = _columns_from_chips(gathered[0])
    cw = jnp.pad(gathered[1][:, 0:3, :].transpose(1, 0, 2).reshape(3, D_CONV), ((0, 5), (0, 0)))
    later = [w_out[0].astype(BF16), w_gate[0].T.astype(BF16), w_up[0].T.astype(BF16),
             w_down[0].astype(BF16)]

    head_id = jnp.arange(D_ATTN) // HEAD_DIM
    ones_bd = (head_id[:, None] == head_id[None, :]).astype(BF16)
    gq_t = jnp.tile(g_q, (1, D_ATTN // HEAD_DIM))
    gk_t = jnp.tile(g_k, (1, D_ATTN // HEAD_DIM))

    h1, z = _norm_matmul("in_proj", xs, g_mix, [win], tm, D_IN, False)
    q, k = _qkv_prepare(z, gq_t, gk_t, ones_bd, tm)
    v_col = (3 * D_CONV + 2 * D_ATTN) // BAND
    (y_attn, lse), gathered = _attn_fwd(q, k, z, v_col, carry=_x_gather_chips(later))
    mix, gathered = _mixer_out(z, cw, y_attn, g_conv_out, g_attn_out, tm,
                               carry=_x_gather_sibling(gathered))
    wout = gathered[0].reshape(D_MODEL, D_MODEL)
    wgate_t = gathered[1].reshape(D_FF, D_MODEL)
    wup_t = gathered[2].reshape(D_FF, D_MODEL)
    wdown = gathered[3].reshape(D_FF, D_MODEL)
    (x1,) = _matmul("out_proj", mix, wout, [xs], [F32], lambda acc, r: (r + acc,), tm, D_MODEL)
    h2, gate, up, act = _norm_matmul("ffn_up", x1, g_ffn, [wgate_t, wup_t], tm, D_FF, True, BF16,
                                     transposed_w=True)

    def loss_epilogue(acc, r, tgt):
        err = r + acc - tgt
        dy = err * (1.0 / D_MODEL)
        return dy, dy, jnp.sum(err * err)

    dx2, dx2b, loss_sum = _matmul("ffn_down_loss", act, wdown, [x1, target], [F32, BF16],
                                  loss_epilogue, tm, D_MODEL, loss=True)

    def swiglu_bwd(da, gt, u):
        gt, u = gt.astype(F32), u.astype(F32)
        s = _sigmoid(gt)
        return da * u * (s * (1.0 + gt * (1.0 - s))), da * (gt * s)

    dgate, dup = _matmul("ffn_down_bwd", dx2b, wdown, [gate, up], [BF16, BF16], swiglu_bwd,
                         tm, D_FF, transposed_w=True)
    gw_down = _matmul_tn("grad_w_down", act, dx2b, 512, tmm)
    gw_gate_t = _matmul_tn("grad_w_gate", dgate, h2, 512, tmm)
    gw_up_t = _matmul_tn("grad_w_up", dup, h2, 512, tmm)

    me = 2 * lax.axis_index("x") + lax.axis_index("y")
    where = jnp.stack([lax.axis_index("c"), me]).astype(jnp.int32)

    def pair_sums(names, full, got):
        return [_pair_sum(f"pair_sum_{nme}", a, b, where) for nme, a, b in zip(names, full, got)]

    def chip_sums(names, pair, got):
        return [_chip_sum(f"chip_sum_{nme}", own, b) for nme, (_, own), b in zip(names, pair, got)]

    ffn = ["w_gate", "w_up", "w_down"]
    full = [g.reshape(N_CHIPS, D_FF // N_CHIPS, D_MODEL) for g in (gw_gate_t, gw_up_t, gw_down)]
    (dx1, dx1b, gg_ffn), got = _matmul_norm_bwd(
        "ffn_up_bwd", [(dgate, wgate_t), (dup, wup_t)], x1, dx2, g_ffn, tm, carry=_x_pair(full),
        transposed_w=False)
    pair = pair_sums(ffn, full, got)
    (dmix,) = _matmul("out_proj_bwd", dx1b, wout, [], [F32], lambda acc: (acc,), tm, D_MODEL,
                      transposed_w=True)
    gw_out = _matmul_tn("grad_w_out", mix, dx1b, 512, tmm)
    full = [gw_out.reshape(N_CHIPS, D_MODEL // N_CHIPS, D_MODEL)]
    (dzc, do, dd, gcw, gg_conv, gg_attn), got = _mixer_bwd(
        z, dmix, y_attn, cw, g_conv_out, g_attn_out, ones_bd, tm, carry=_x_pair(full))
    pair += pair_sums(["w_out"], full, got)
    early = ffn + ["w_out"]
    (dqn, dkn, dv), got = _attn_bwd(q, k, z, v_col, do, lse, dd,
                                    carry=_x_chips([p for p, _ in pair]))
    mine = chip_sums(early, pair, got)
    (dz, gg_q, gg_k), theirs = _qkv_bwd(z, dzc, dqn, dkn, dv, gq_t, gk_t, ones_bd, tm,
                                        carry=_x_share(mine))
    full = [_matmul_tn("grad_w_in", h1, dz, D_IN // N_CHIPS, tmm, by_chip=True)]
    grad_x, _, gg_mix = _matmul_norm_bwd("in_proj_bwd", [(dz, win)], xs, dx1, g_mix, tm)
    got = _exchange_alone("grad_pair_exchange_w_in", _x_pair(full))
    pair = pair_sums(["w_in"], full, got)
    got = _exchange_alone("grad_chip_exchange_w_in", _x_chips([pair[0][0]]))
    mine += chip_sums(["w_in"], pair, got)
    theirs = list(theirs) + list(_exchange_alone("grad_pair_share_w_in", _x_share(mine[-1:])))
    big = early + ["w_in"]

    small = _small_all_reduce({
        "g_mix": gg_mix, "g_ffn": gg_ffn, "g_conv_out": gg_conv, "g_attn_out": gg_attn,
        "g_q": gg_q, "g_k": gg_k, "loss": loss_sum, "conv_w": gcw})
    heads = D_ATTN // HEAD_DIM
    grads = {
        "g_mix": small[0:1, :], "g_ffn": small[1:2, :],
        "g_conv_out": small[2:3, 0:512], "g_attn_out": small[2:3, 512:1024],
        "g_q": small[3, 0:512].reshape(heads, HEAD_DIM).sum(axis=0)[None, :],
        "g_k": small[3, 512:1024].reshape(heads, HEAD_DIM).sum(axis=0)[None, :],
        "conv_w": lax.dynamic_slice(small[8:11, 0:512], (0, me * (D_CONV // N_CHIPS)),
                                    (3, D_CONV // N_CHIPS)),
    }
    halves = dict(zip(big, zip(mine, theirs)))
    loss = small[4, 0] * 0.5 * (1.0 / D_MODEL)

    weights = dict(g_mix=g_mix, w_in=w_in, conv_w=conv_w, g_q=g_q, g_k=g_k, g_conv_out=g_conv_out,
                   g_attn_out=g_attn_out, w_out=w_out, g_ffn=g_ffn, w_gate=w_gate, w_up=w_up,
                   w_down=w_down)
    moments_m = dict(g_mix=m_g_mix, w_in=m_w_in, conv_w=m_conv_w, g_q=m_g_q, g_k=m_g_k,
                     g_conv_out=m_g_conv_out, g_attn_out=m_g_attn_out, w_out=m_w_out, g_ffn=m_g_ffn,
                     w_gate=m_w_gate, w_up=m_w_up, w_down=m_w_down)
    moments_v = dict(g_mix=v_g_mix, w_in=v_w_in, conv_w=v_conv_w, g_q=v_g_q, g_k=v_g_k,
                     g_conv_out=v_g_conv_out, g_attn_out=v_g_attn_out, w_out=v_w_out, g_ffn=v_g_ffn,
                     w_gate=v_w_gate, w_up=v_w_up, w_down=v_w_down)
    names = list(weights)
    out_g, out_d, out_m, out_v = [], [], [], []
    for nme in names:
        wgt = weights[nme]
        shape2 = wgt.shape[-2:] if wgt.ndim == 3 else wgt.shape
        flip = nme in ("w_gate", "w_up")

        def to2d(a):
            return a.reshape(shape2).T if flip else a.reshape(shape2)

        def back(a):
            return (a.T if flip else a).reshape(wgt.shape)

        state = (to2d(wgt), to2d(moments_m[nme]), to2d(moments_v[nme]))
        if nme in halves:
            g2, dlt, nm, nv = _adamw_shard(f"adamw_{nme}", *state, *halves[nme], where)
        else:
            g2 = grads[nme].reshape(shape2)
            dlt, nm, nv = _adamw(f"adamw_{nme}", state[0], g2, state[1], state[2])
        out_g.append(back(g2))
        out_d.append(back(dlt))
        out_m.append(back(nm))
        out_v.append(back(nv))
    return (loss, grad_x[None], *out_g, *out_d, *out_m, *out_v)
```

```python
import functools
from typing import Any, Callable, NamedTuple, Sequence

import jax
import jax.numpy as jnp
from jax import lax
from jax.experimental import pallas as pl
from jax.experimental.pallas import tpu as pltpu

F32 = jnp.float32
BF16 = jnp.bfloat16
MESH = pl.DeviceIdType.MESH

D_MODEL = 1024
D_CONV = 512
D_ATTN = 512
HEAD_DIM = 64
D_FF = 2816
D_IN = 3 * D_CONV + 3 * D_ATTN
DILATIONS = (1, 4, 16)
BAND = 128
EPS = 1e-6
NEG = -1e30
N_CHIPS = 4

ADAM_LR = 0.001
ADAM_B1 = 0.9
ADAM_B2 = 0.999
ADAM_EPS = 1e-08
ADAM_WD = 0.01
ADAM_STEP = 10

V7X_VMEM_BYTES = 64 * 1024 * 1024
VMEM_LIMIT = V7X_VMEM_BYTES - 8 * 1024 * 1024
ANY = pl.BlockSpec(memory_space=pl.ANY)
VMEM_WHOLE = pl.BlockSpec(memory_space=pltpu.VMEM)


def _params(*sem):
    return pltpu.CompilerParams(dimension_semantics=sem, vmem_limit_bytes=VMEM_LIMIT)


def _sds(shape, dtype):
    return jax.ShapeDtypeStruct(shape, dtype)


def _resident(whole):
    return pl.Buffered(1) if whole else None


def _place():
    x, y, c = lax.axis_index("x"), lax.axis_index("y"), lax.axis_index("c")
    chips = [(1 - x, y), (x, 1 - y), (1 - x, 1 - y)]
    return x, y, c, 2 * x + y, chips, [2 * cx + cy for cx, cy in chips]


def _all_gather(shards):
    n = len(shards)

    def body(*refs):
        ins, outs, stage = refs[:n], refs[n:2 * n], refs[2 * n:3 * n]
        ssem, rsem, fsem, gsem, lsem, osem = refs[3 * n:]
        x, y, c, me, chips, cids = _place()
        sib = (x, y, 1 - c)

        def half(w, which):
            h = shards[w].shape[0] // 2
            return pl.ds(pl.multiple_of(which * h, 8), h)

        loads = [pltpu.make_async_copy(ins[w], stage[w], lsem.at[w]) for w in range(n)]
        local = [pltpu.make_async_copy(stage[w], outs[w].at[me], osem.at[w]) for w in range(n)]
        for cp in loads:
            cp.start()

        def chip_copy(w, j, src_slot):
            rows = half(w, c)
            return pltpu.make_async_remote_copy(
                src_ref=ins[w].at[rows], dst_ref=outs[w].at[src_slot, rows],
                send_sem=ssem.at[3 * w + j], recv_sem=rsem.at[3 * w + j],
                device_id=(*chips[j], c), device_id_type=MESH)

        def sib_copy(w, j, which):
            rows = half(w, which)
            return pltpu.make_async_remote_copy(
                src_ref=outs[w].at[cids[j], rows], dst_ref=outs[w].at[cids[j], rows],
                send_sem=fsem.at[3 * w + j], recv_sem=gsem.at[3 * w + j],
                device_id=sib, device_id_type=MESH)

        sends = [chip_copy(w, j, me) for w in range(n) for j in range(3)]
        for cp in sends:
            cp.start()
        for w in range(n):
            loads[w].wait()
            local[w].start()
        passed = []
        for w in range(n):
            for j in range(3):
                chip_copy(w, j, cids[j]).wait_recv()
                cp = sib_copy(w, j, c)
                cp.start()
                passed.append(cp)
        for w in range(n):
            for j in range(3):
                sib_copy(w, j, 1 - c).wait_recv()
        for cp in sends + passed:
            cp.wait_send()
        for cp in local:
            cp.wait()

    return pl.pallas_call(
        body, name="all_gather_weights",
        out_shape=[_sds((N_CHIPS,) + s.shape, s.dtype) for s in shards],
        in_specs=[ANY] * n, out_specs=[ANY] * n,
        scratch_shapes=[pltpu.VMEM(s.shape, s.dtype) for s in shards]
        + [pltpu.SemaphoreType.DMA((3 * n,))] * 4 + [pltpu.SemaphoreType.DMA((n,))] * 2,
        compiler_params=pltpu.CompilerParams(vmem_limit_bytes=VMEM_LIMIT),
    )(*shards)


class _Exchange(NamedTuple):
    srcs: Sequence[Any]
    lands: Sequence[Any]
    outs: Sequence[Any]
    n_sems: int
    copies: Callable


def _remote(src, dst, ssem, rsem, k, to):
    return pltpu.make_async_remote_copy(src_ref=src, dst_ref=dst, send_sem=ssem.at[k],
                                        recv_sem=rsem.at[k], device_id=to, device_id_type=MESH)


def _x_gather_chips(shards):
    def copies(srcs, lands, outs, ssem, rsem):
        _, _, c, me, chips, cids = _place()
        go, arrive = [], []
        for w, s in enumerate(shards):
            h = s.shape[0] // 2
            rows = pl.ds(pl.multiple_of(c * h, 8), h)
            for j in range(3):
                to = (*chips[j], c)
                go.append(_remote(srcs[w].at[rows], lands[w].at[me, rows], ssem, rsem, 3 * w + j, to))
                arrive.append(_remote(srcs[w].at[rows], lands[w].at[cids[j], rows], ssem, rsem,
                                      3 * w + j, to))
        return go, arrive

    lands = [jnp.broadcast_to(s[None], (N_CHIPS,) + s.shape) for s in shards]
    return _Exchange(shards, lands, [], 3 * len(shards), copies)


def _x_gather_sibling(gathered):
    def copies(srcs, lands, outs, ssem, rsem):
        x, y, c, _, _, cids = _place()
        go, arrive = [], []
        for w, g in enumerate(gathered):
            h = g.shape[1] // 2
            mine = pl.ds(pl.multiple_of(c * h, 8), h)
            theirs = pl.ds(pl.multiple_of((1 - c) * h, 8), h)
            for j in range(3):
                slab = lands[w].at[cids[j]]
                go.append(_remote(slab.at[mine], slab.at[mine], ssem, rsem, 3 * w + j, (x, y, 1 - c)))
                arrive.append(_remote(slab.at[theirs], slab.at[theirs], ssem, rsem, 3 * w + j,
                                      (x, y, 1 - c)))
        return go, arrive

    return _Exchange([], gathered, [], 3 * len(gathered), copies)


def _x_pair(grads):
    def copies(srcs, lands, outs, ssem, rsem):
        x, y, c, _, _, _ = _place()
        go = []
        for w, g in enumerate(grads):
            h = g.shape[1] // 2
            theirs = pl.ds(pl.multiple_of((1 - c) * h, 8), h)
            go.append(_remote(srcs[w].at[:, theirs, :], outs[w], ssem, rsem, w, (x, y, 1 - c)))
        return go, go

    outs = [_sds((N_CHIPS, g.shape[1] // 2, g.shape[2]), g.dtype) for g in grads]
    return _Exchange(grads, [], outs, len(grads), copies)


def _x_chips(parts):
    def copies(srcs, lands, outs, ssem, rsem):
        _, _, c, _, chips, cids = _place()
        go = [_remote(srcs[w].at[cids[j]], outs[w].at[j], ssem, rsem, 3 * w + j, (*chips[j], c))
              for w in range(len(parts)) for j in range(3)]
        return go, go

    outs = [_sds((3,) + p.shape[1:], p.dtype) for p in parts]
    return _Exchange(parts, [], outs, 3 * len(parts), copies)


def _x_share(halves):
    def copies(srcs, lands, outs, ssem, rsem):
        x, y, c, _, _, _ = _place()
        go = [_remote(srcs[w], outs[w], ssem, rsem, w, (x, y, 1 - c)) for w in range(len(halves))]
        return go, go

    return _Exchange(halves, [], [_sds(h.shape, h.dtype) for h in halves], len(halves), copies)


def _call(body, args, *, name, grid, in_specs, out_specs, out_shape, scratch_shapes=(),
          semantics=None, carry=None, aliases=None):
    single = not isinstance(out_shape, (list, tuple))
    out_shape = [out_shape] if single else list(out_shape)
    out_specs = [out_specs] if single else list(out_specs)
    aliases = dict(aliases or {})
    if carry is None:
        res = pl.pallas_call(
            body, name=name, grid=grid, in_specs=list(in_specs), out_specs=out_specs,
            out_shape=out_shape, scratch_shapes=list(scratch_shapes), input_output_aliases=aliases,
            compiler_params=_params(*(semantics or ("arbitrary",) * len(grid))))(*args)
        return res[0] if single else res
    n_in, n_out, n_scr = len(args), len(out_shape), len(scratch_shapes)
    n_src, n_land, n_new = len(carry.srcs), len(carry.lands), len(carry.outs)

    def carrying(*refs):
        at = 0
        parts = []
        for n in (n_in, n_src, n_land, n_out, n_land, n_new, n_scr, 2):
            parts.append(refs[at:at + n])
            at += n
        ins, srcs, _, outs, lands, news, scratch, (ssem, rsem) = parts
        ids = [pl.program_id(a) for a in range(len(grid))]
        first = functools.reduce(jnp.logical_and, [i == 0 for i in ids])
        last = functools.reduce(jnp.logical_and, [i == g - 1 for i, g in zip(ids, grid)])
        go, arrive = carry.copies(srcs, lands, news, ssem, rsem)

        @pl.when(first)
        def _():
            for cp in go:
                cp.start()

        body(*ins, *outs, *scratch)

        @pl.when(last)
        def _():
            for cp in go:
                cp.wait_send()
            for cp in arrive:
                cp.wait_recv()

    res = pl.pallas_call(
        carrying, name=name, grid=grid,
        in_specs=list(in_specs) + [ANY] * (n_src + n_land),
        out_specs=out_specs + [ANY] * (n_land + n_new),
        out_shape=out_shape + [_sds(a.shape, a.dtype) for a in carry.lands] + list(carry.outs),
        input_output_aliases={**aliases, **{n_in + n_src + i: n_out + i for i in range(n_land)}},
        scratch_shapes=list(scratch_shapes) + [pltpu.SemaphoreType.DMA((carry.n_sems,))] * 2,
        compiler_params=_params(*(("arbitrary",) * len(grid))))(*args, *carry.srcs, *carry.lands)
    own = res[:n_out]
    return (own[0] if single else own), res[n_out:]


def _exchange_alone(name, exchange):
    def body(x_ref, o_ref):
        o_ref[...] = x_ref[...]

    blk = pl.BlockSpec((8, 128), lambda i: (0, 0))
    _, res = _call(body, [jnp.zeros((8, 128), F32)], name=name, grid=(1,), in_specs=[blk],
                   out_specs=blk, out_shape=_sds((8, 128), F32), carry=exchange)
    return res


def _row_block(r, want):
    return max(d for d in range(1, min(want, r) + 1) if r % d == 0 and (d % 8 == 0 or d == r))


def _pair_sum(name, full, got, where):
    _, r, n = full.shape
    h = r // 2
    tr = _row_block(h, 256)
    nb = h // tr

    def body(w_ref, a_ref, b_ref, o_ref, own_ref):
        total = a_ref[...] + b_ref[...]
        o_ref[...] = total.astype(BF16)

        @pl.when(pl.program_id(1) == w_ref[1])
        def _():
            own_ref[...] = total[0]

    blk = pl.BlockSpec((1, tr, n), lambda i, s, w: (s, i, 0))
    return pl.pallas_call(
        body, name=name, out_shape=[_sds(got.shape, BF16), _sds((h, n), F32)],
        grid_spec=pltpu.PrefetchScalarGridSpec(
            num_scalar_prefetch=1, grid=(nb, N_CHIPS),
            in_specs=[pl.BlockSpec((1, tr, n), lambda i, s, w: (s, w[0] * nb + i, 0)), blk],
            out_specs=[blk, pl.BlockSpec((tr, n), lambda i, s, w: (i, 0))]),
        compiler_params=_params("parallel", "arbitrary"),
    )(where, full, got)


def _chip_sum(name, own, got):
    h, n = own.shape
    tr = _row_block(h, 256)

    def body(a_ref, b0, b1, b2, o_ref):
        o_ref[...] = ((a_ref[...] + b0[0].astype(F32)) + b1[0].astype(F32)) + b2[0].astype(F32)

    def slot(j):
        return pl.BlockSpec((1, tr, n), lambda i: (j, i, 0))

    blk = pl.BlockSpec((tr, n), lambda i: (i, 0))
    return pl.pallas_call(
        body, name=name, grid=(h // tr,), out_shape=_sds((h, n), F32),
        in_specs=[blk, slot(0), slot(1), slot(2)], out_specs=blk,
        compiler_params=_params("parallel"),
    )(own, got, got, got)


SMALL_ROWS = 16
SMALL_LAYOUT = (
    ("g_mix", 0, 0, 1, 1024), ("g_ffn", 1, 0, 1, 1024), ("g_conv_out", 2, 0, 1, 512),
    ("g_attn_out", 2, 512, 1, 512), ("g_q", 3, 0, 1, 512), ("g_k", 3, 512, 1, 512),
    ("loss", 4, 0, 1, 128), ("conv_w", 8, 0, 8, 512))


def _small_all_reduce(parts):
    names = [s[0] for s in SMALL_LAYOUT]

    def body(*refs):
        ins = refs[:len(names)]
        out_ref, stage, buf, ssem, rsem = refs[len(names):]
        x, y, c, _, _, _ = _place()
        me = 4 * x + 2 * y + c
        stage[...] = jnp.zeros_like(stage)
        for ref, (_, r0, c0, nr, nc) in zip(ins, SMALL_LAYOUT):
            stage[r0:r0 + nr, c0:c0 + nc] = ref[0:nr, :]
        buf[me] = stage[...]
        peers = []
        for d in range(1, 8):
            px = 1 - x if d & 4 else x
            py = 1 - y if d & 2 else y
            pc = 1 - c if d & 1 else c
            peers.append(((px, py, pc), 4 * px + 2 * py + pc))
        sends = [pltpu.make_async_remote_copy(
            src_ref=stage, dst_ref=buf.at[me], send_sem=ssem.at[k], recv_sem=rsem.at[k],
            device_id=peer, device_id_type=MESH) for k, (peer, _) in enumerate(peers)]
        for cp in sends:
            cp.start()
        for k, (peer, pid) in enumerate(peers):
            pltpu.make_async_remote_copy(
                src_ref=stage, dst_ref=buf.at[pid], send_sem=ssem.at[k], recv_sem=rsem.at[k],
                device_id=peer, device_id_type=MESH).wait_recv()
        for cp in sends:
            cp.wait_send()
        acc = buf[0]
        for k in range(1, 8):
            acc = acc + buf[k]
        out_ref[...] = acc

    return pl.pallas_call(
        body, name="small_all_reduce", out_shape=_sds((SMALL_ROWS, 1024), F32),
        in_specs=[VMEM_WHOLE] * len(names), out_specs=VMEM_WHOLE,
        scratch_shapes=[pltpu.VMEM((SMALL_ROWS, 1024), F32), pltpu.VMEM((8, SMALL_ROWS, 1024), F32),
                        pltpu.SemaphoreType.DMA((7,)), pltpu.SemaphoreType.DMA((7,))],
    )(*[parts[k] for k in names])


def _dot(a, b):
    return jnp.dot(a, b, preferred_element_type=F32)


def _dot_nt(a, b):
    return lax.dot_general(a, b, (((1,), (1,)), ((), ())), preferred_element_type=F32)


def _dot_tn(a, b):
    return lax.dot_general(a, b, (((0,), (0,)), ((), ())), preferred_element_type=F32)


def _sigmoid(v):
    return 1.0 / (1.0 + jnp.exp(-v))


def _rms_scale(v):
    return lax.rsqrt(jnp.mean(v * v, axis=-1, keepdims=True) + EPS)


def _rms_bwd(v, r, g, dy):
    vh = v * r
    dh = dy * g
    return r * (dh - vh * jnp.mean(dh * vh, axis=-1, keepdims=True)), vh


def _head_sum(a, ones_bd):
    hi = a.astype(BF16)
    lo = (a - hi.astype(F32)).astype(BF16)
    return _dot(hi, ones_bd) + _dot(lo, ones_bd)


def _head_rms_scale(v, ones_bd):
    return lax.rsqrt(_head_sum(v * v, ones_bd) * (1.0 / HEAD_DIM) + EPS)


MXU_COLUMNS = 256


def _column_chunks(n):
    width = MXU_COLUMNS if n % MXU_COLUMNS == 0 else n
    return [slice(c, c + width) for c in range(0, n, width)]


def _norm_matmul(name, x, g, ws, tm, tn, swiglu, out_dtype=F32, transposed_w=False):
    t, d = x.shape
    n = ws[0].shape[0] if transposed_w else ws[0].shape[1]
    nw = len(ws)

    def body(x_ref, g_ref, *refs):
        w_refs, h_ref, o_refs = refs[:nw], refs[nw], refs[nw + 1:2 * nw + 1]
        hs = refs[-1]

        @pl.when(pl.program_id(1) == 0)
        def _():
            xv = x_ref[...]
            h = (xv * _rms_scale(xv) * g_ref[...]).astype(BF16)
            hs[...] = h
            h_ref[...] = h

        h = hs[...]
        for cols in _column_chunks(tn):
            outs = [_dot_nt(h, w[cols, :]) if transposed_w else _dot(h, w[:, cols]) for w in w_refs]
            for o_ref, o in zip(o_refs, outs):
                o_ref[:, cols] = o.astype(out_dtype)
            if swiglu:
                refs[2 * nw + 1][:, cols] = (outs[0] * _sigmoid(outs[0]) * outs[1]).astype(BF16)

    row = pl.BlockSpec((tm, d), lambda i, j: (i, 0))
    col = pl.BlockSpec((tm, tn), lambda i, j: (i, j))
    out_shape = [_sds((t, d), BF16)] + [_sds((t, n), out_dtype)] * nw
    out_specs = [row] + [col] * nw
    if swiglu:
        out_shape.append(_sds((t, n), BF16))
        out_specs.append(col)
    return pl.pallas_call(
        body, name=name, grid=(t // tm, n // tn), out_shape=out_shape,
        in_specs=[row, pl.BlockSpec((1, d), lambda i, j: (0, 0))]
        + [pl.BlockSpec((tn, d), lambda i, j: (j, 0), pipeline_mode=_resident(tn == n))
           if transposed_w
           else pl.BlockSpec((d, tn), lambda i, j: (0, j), pipeline_mode=_resident(tn == n))] * nw,
        out_specs=out_specs, scratch_shapes=[pltpu.VMEM((tm, d), BF16)],
        compiler_params=_params("parallel", "arbitrary"),
    )(x, g, *ws)


def _matmul(name, a, w, extras, out_dtypes, epilogue, tm, tn, transposed_w=False, loss=False):
    t, k = a.shape
    n = w.shape[0] if transposed_w else w.shape[1]
    ne, no = len(extras), len(out_dtypes)

    def body(a_ref, w_ref, *refs):
        e_refs, o_refs = refs[:ne], refs[ne:]
        a = a_ref[...]
        total = 0.0
        for cols in _column_chunks(tn):
            acc = _dot_nt(a, w_ref[cols, :]) if transposed_w else _dot(a, w_ref[:, cols])
            res = epilogue(acc, *[e[:, cols] for e in e_refs])
            for o_ref, r in zip(o_refs[:no], res[:no]):
                o_ref[:, cols] = r.astype(o_ref.dtype)
            if loss:
                total = total + res[no]
        if loss:
            first = jnp.logical_and(pl.program_id(0) == 0, pl.program_id(1) == 0)

            @pl.when(first)
            def _():
                o_refs[no][...] = jnp.zeros_like(o_refs[no])

            o_refs[no][...] += total

    col = pl.BlockSpec((tm, tn), lambda i, j: (i, j))
    w_spec = (pl.BlockSpec((tn, k), lambda i, j: (j, 0), pipeline_mode=_resident(tn == n))
              if transposed_w
              else pl.BlockSpec((k, tn), lambda i, j: (0, j), pipeline_mode=_resident(tn == n)))
    out_shape = [_sds((t, n), dt) for dt in out_dtypes]
    out_specs = [col] * no
    if loss:
        out_shape.append(_sds((8, 128), F32))
        out_specs.append(pl.BlockSpec((8, 128), lambda i, j: (0, 0)))
    return pl.pallas_call(
        body, name=name, grid=(t // tm, n // tn), out_shape=out_shape,
        in_specs=[pl.BlockSpec((tm, k), lambda i, j: (i, 0)), w_spec] + [col] * ne,
        out_specs=out_specs,
        compiler_params=_params(*(("arbitrary", "arbitrary") if loss else ("parallel", "parallel"))),
    )(a, w, *extras)


def _matmul_norm_bwd(name, pairs, x, dres, g, tm, carry=None, transposed_w=True, blocks=None,
                     into=None):
    t, d = x.shape
    npairs = len(pairs)
    product = _dot_nt if transposed_w else _dot
    first, count = blocks or (0, t // tm)
    into = list(into or [])

    def body(*refs):
        a_refs, w_refs = refs[:npairs], refs[npairs:2 * npairs]
        x_ref, r_ref, g_ref = refs[2 * npairs:2 * npairs + 3]
        dx_ref, dxb_ref, dg_ref = refs[-3:]
        dy = product(a_refs[0][...], w_refs[0][...])
        for a_ref, w_ref in zip(a_refs[1:], w_refs[1:]):
            dy = dy + product(a_ref[...], w_ref[...])
        xv = x_ref[...]
        dx, xh = _rms_bwd(xv, _rms_scale(xv), g_ref[...], dy)
        dx = dx + r_ref[...]
        dx_ref[...] = dx
        dxb_ref[...] = dx.astype(BF16)

        @pl.when(pl.program_id(0) == 0)
        def _():
            dg_ref[...] = jnp.zeros_like(dg_ref)

        dg_ref[...] += jnp.sum(dy * xh, axis=0, keepdims=True)

    row = pl.BlockSpec((tm, d), lambda i: (first + i, 0))
    vec = pl.BlockSpec((1, d), lambda i: (0, 0))
    args = [a for a, _ in pairs] + [w for _, w in pairs] + [x, dres, g]
    return _call(
        body, args + into, name=name, grid=(count,),
        out_shape=[_sds((t, d), F32), _sds((t, d), BF16), _sds((1, d), F32)],
        in_specs=[pl.BlockSpec((tm, a.shape[1]), lambda i: (first + i, 0)) for a, _ in pairs]
        + [pl.BlockSpec(w.shape, lambda i: (0, 0), pipeline_mode=pl.Buffered(1)) for _, w in pairs]
        + [row, row, vec] + [ANY] * len(into),
        out_specs=[row, row, vec], carry=carry,
        aliases={len(args) + n: n for n in range(len(into))})


def _matmul_tn(name, a, g, tn, tk, by_chip=False):
    t, ka = a.shape
    n = g.shape[1]

    def body(a_ref, g_ref, o_ref):
        @pl.when(pl.program_id(1) == 0)
        def _():
            o_ref[...] = jnp.zeros_like(o_ref)

        acc = _dot_tn(a_ref[...], g_ref[...])
        o_ref[...] += acc[None] if by_chip else acc

    return pl.pallas_call(
        body, name=name, grid=(n // tn, t // tk),
        out_shape=_sds((n // tn, ka, tn) if by_chip else (ka, n), F32),
        in_specs=[pl.BlockSpec((tk, ka), lambda j, s: (s, 0)),
                  pl.BlockSpec((tk, tn), lambda j, s: (s, j))],
        out_specs=(pl.BlockSpec((1, ka, tn), lambda j, s: (j, 0, 0)) if by_chip
                   else pl.BlockSpec((ka, tn), lambda j, s: (0, j))),
        compiler_params=_params("parallel", "arbitrary"),
    )(a, g)


def _elementwise(name, fn, ins, out_dtypes, tr):
    r, n = ins[0].shape
    tr = _row_block(r, tr)
    ni = len(ins)

    def body(*refs):
        res = fn(*[ref[...] for ref in refs[:ni]])
        for o_ref, v in zip(refs[ni:], res):
            o_ref[...] = v.astype(o_ref.dtype)

    blk = pl.BlockSpec((tr, n), lambda i: (i, 0))
    return pl.pallas_call(
        body, name=name, grid=(r // tr,), out_shape=[_sds((r, n), dt) for dt in out_dtypes],
        in_specs=[blk] * ni, out_specs=[blk] * len(out_dtypes),
        compiler_params=_params("parallel"),
    )(*ins)


def _adamw_update(w, g, m, v):
    m = ADAM_B1 * m + (1.0 - ADAM_B1) * g
    v = ADAM_B2 * v + (1.0 - ADAM_B2) * (g * g)
    m_hat = m / (1.0 - ADAM_B1 ** ADAM_STEP)
    v_hat = v / (1.0 - ADAM_B2 ** ADAM_STEP)
    return -ADAM_LR * (m_hat / (jnp.sqrt(v_hat) + ADAM_EPS) + ADAM_WD * w), m, v


def _adamw(name, w, g, m, v):
    return _elementwise(name, _adamw_update, [w, g, m, v], [F32] * 3, 256)


def _adamw_shard(name, w, m, v, mine, theirs, where):
    r, n = w.shape
    h = r // 2
    tr = _row_block(h, 256)
    nb = h // tr

    def body(w_ref, p_ref, m_ref, v_ref, a_ref, b_ref, g_ref, d_ref, nm_ref, nv_ref):
        g = jnp.where(pl.program_id(0) == w_ref[0], a_ref[...], b_ref[...])
        g_ref[...] = g
        d_ref[...], nm_ref[...], nv_ref[...] = _adamw_update(p_ref[...], g, m_ref[...], v_ref[...])

    whole = pl.BlockSpec((tr, n), lambda s, i, c: (s * nb + i, 0))
    half = pl.BlockSpec((tr, n), lambda s, i, c: (i, 0))
    return pl.pallas_call(
        body, name=name, out_shape=[_sds((r, n), F32)] * 4,
        grid_spec=pltpu.PrefetchScalarGridSpec(
            num_scalar_prefetch=1, grid=(2, nb), in_specs=[whole] * 3 + [half] * 2,
            out_specs=[whole] * 4),
        compiler_params=_params("parallel", "parallel"),
    )(where, w, m, v, mine, theirs)


def _qkv_prepare(z, gq, gk, ones_bd, tm):
    t = z.shape[0]

    def body(zq_ref, zk_ref, gq_ref, gk_ref, bd_ref, q_ref, k_ref):
        bd = bd_ref[...]
        q = zq_ref[...]
        k = zk_ref[...]
        q_ref[...] = (q * _head_rms_scale(q, bd) * gq_ref[...]) * HEAD_DIM ** -0.5
        k_ref[...] = k * _head_rms_scale(k, bd) * gk_ref[...]

    vec = pl.BlockSpec((1, 512), lambda i: (0, 0))
    out = pl.BlockSpec((tm, 512), lambda i: (i, 0))
    return pl.pallas_call(
        body, name="qkv_prepare", grid=(t // tm,), out_shape=[_sds((t, 512), F32)] * 2,
        in_specs=[pl.BlockSpec((tm, 512), lambda i: (i, 3)), pl.BlockSpec((tm, 512), lambda i: (i, 4)),
                  vec, vec, pl.BlockSpec((512, 512), lambda i: (0, 0))],
        out_specs=[out] * 2, compiler_params=_params("parallel"),
    )(z, z, gq, gk, ones_bd)


TOK = 2048
UNITS = TOK // BAND


def _stack_masks():
    row = lax.broadcasted_iota(jnp.int32, (2 * BAND, 2 * BAND), 0) & (BAND - 1)
    col = lax.broadcasted_iota(jnp.int32, (2 * BAND, 2 * BAND), 1)
    lane = lax.broadcasted_iota(jnp.int32, (BAND, BAND), 1)
    head0 = lane < HEAD_DIM
    ones = [jnp.where(head0, 1.0, 0.0).astype(BF16), jnp.where(head0, 0.0, 1.0).astype(BF16)]
    return col - row, col, head0, ones


def _split3(x):
    hi = x.astype(BF16).astype(F32)
    mid = (x - hi).astype(BF16).astype(F32)
    return hi, mid, x - hi - mid


def _gather(srcs, dst, d):
    per = TOK // d
    at = 0
    for r in range(d):
        for src in srcs:
            rows = src[pl.ds(r, per, stride=d), :] if d > 1 else src[...]
            dst[pl.ds(at, per), :] = rows.astype(dst.dtype)
            at += per


def _scatter_add(out_ref, src, d, per_src, offset, first):
    per = TOK // d
    if d == 1:
        val = src[pl.ds(offset, per), :]
        out_ref[...] = val if first else out_ref[...] + val
        return
    for r in range(d):
        val = src[pl.ds(r * per_src + offset, per), :]
        idx = pl.ds(r, per, stride=d)
        out_ref[idx, :] = val if first else out_ref[idx, :] + val


def _attn_fwd(q, k, v, v_col, carry=None):
    t = q.shape[0]
    nblk = t // TOK

    def body(q_ref, kp_ref, k_ref, vp_ref, v_ref, y_ref, l_ref, qs, ks, vs, ob, lb, on, ln):
        i = pl.program_id(1)
        diff, col, head0, hm = _stack_masks()
        band_ok = jnp.logical_and(diff >= 0, diff <= BAND)
        for g, d in enumerate(DILATIONS):
            per = TOK // d
            nb = per // BAND
            _gather([q_ref], qs, d)
            _gather([kp_ref, k_ref], ks, d)
            _gather([vp_ref, v_ref], vs, d)

            def unit(u, carry):
                r, b = u // nb, u % nb
                qu = qs[pl.ds(pl.multiple_of(u * BAND, BAND), BAND), :]
                start = pl.multiple_of(r * 2 * per + per + (b - 1) * BAND, BAND)
                kw = ks[pl.ds(start, 2 * BAND), :]
                vw = vs[pl.ds(start, 2 * BAND), :]
                lo = jnp.where(jnp.logical_and(i == 0, b == 0), BAND, 0)
                s = _dot_nt(jnp.concatenate([qu * hm[0], qu * hm[1]], axis=0), kw)
                s = jnp.where(jnp.logical_and(band_ok, col >= lo), s, NEG)
                mx = jnp.max(s, axis=-1, keepdims=True)
                e = jnp.exp(s - mx)
                den = jnp.sum(e, axis=-1, keepdims=True)
                o2 = _dot(e.astype(BF16), vw) / den
                l2 = jnp.broadcast_to(mx + jnp.log(den), (2 * BAND, BAND))
                rows = pl.ds(pl.multiple_of(u * BAND, BAND), BAND)
                ob[rows, :] = jnp.where(head0, o2[:BAND], o2[BAND:])
                lb[rows, :] = jnp.where(head0, l2[:BAND], l2[BAND:])
                return carry

            lax.fori_loop(0, UNITS, unit, 0, unroll=16)
            _scatter_add(on.at[g], ob, d, per, 0, True)
            _scatter_add(ln.at[g], lb, d, per, 0, True)
        ls = [ln[0], ln[1], ln[2]]
        mx = jnp.maximum(jnp.maximum(ls[0], ls[1]), ls[2])
        es = [jnp.exp(l - mx) for l in ls]
        tot = es[0] + es[1] + es[2]
        y_ref[...] = (es[0] * on[0] + es[1] * on[1] + es[2] * on[2]) / tot
        l_ref[...] = mx + jnp.log(tot)

    main = pl.BlockSpec((TOK, BAND), lambda j, i: (i, j))
    prev = pl.BlockSpec((TOK, BAND), lambda j, i: (jnp.maximum(i - 1, 0), j))
    vmain = pl.BlockSpec((TOK, BAND), lambda j, i: (i, j + v_col))
    vprev = pl.BlockSpec((TOK, BAND), lambda j, i: (jnp.maximum(i - 1, 0), j + v_col))
    return _call(
        body, [q, k, k, v, v], name="attn_fwd", grid=(D_ATTN // BAND, nblk),
        out_shape=[_sds((t, D_ATTN), F32)] * 2,
        in_specs=[main, prev, main, vprev, vmain], out_specs=[main, main],
        scratch_shapes=[pltpu.VMEM((TOK, BAND), BF16), pltpu.VMEM((2 * TOK, BAND), BF16),
                        pltpu.VMEM((2 * TOK, BAND), BF16), pltpu.VMEM((TOK, BAND), F32),
                        pltpu.VMEM((TOK, BAND), F32), pltpu.VMEM((3, TOK, BAND), F32),
                        pltpu.VMEM((3, TOK, BAND), F32)],
        semantics=("parallel", "parallel"), carry=carry)


def _attn_bwd(q, k, v, v_col, do, lse, dd, carry=None):
    t = q.shape[0]
    nblk = t // TOK
    offs = [sum(DILATIONS[:g]) * BAND for g in range(len(DILATIONS))]

    def body(q_ref, kp_ref, k_ref, vp_ref, v_ref, do_ref, l_ref, d_ref, dq_ref, dk_ref, dv_ref,
             qs, dos, ks, vs, lsc, dsc, dqb, dkb, dvb, ckb, cvb):
        step = pl.program_id(1)
        i = nblk - 1 - step
        key = lax.broadcasted_iota(jnp.int32, (2 * BAND, 2 * BAND), 0)
        qry = lax.broadcasted_iota(jnp.int32, (2 * BAND, 2 * BAND), 1) & (BAND - 1)
        off = key - qry
        band_ok = jnp.logical_and(off >= 0, off <= BAND)
        lane = lax.broadcasted_iota(jnp.int32, (BAND, BAND), 1)
        head0 = lane < HEAD_DIM
        hm = [jnp.where(head0, 1.0, 0.0).astype(BF16), jnp.where(head0, 0.0, 1.0).astype(BF16)]
        piece = lane & (HEAD_DIM - 1)
        lane2 = lax.broadcasted_iota(jnp.int32, (2 * BAND, BAND), 1) & (HEAD_DIM - 1)
        ones = jnp.where(lane2 < 3, 1.0, 0.0).astype(BF16)

        def pieces(x):
            hi, mid, lo = _split3(-x)
            a = jnp.where(piece == 0, hi, jnp.where(piece == 1, mid, jnp.where(piece == 2, lo, 0.0)))
            return a.astype(BF16)

        for g, d in enumerate(DILATIONS):
            per = TOK // d
            nb = per // BAND
            pad = per + BAND
            _gather([q_ref], qs, d)
            _gather([do_ref], dos, d)
            _gather([l_ref], lsc, d)
            _gather([d_ref], dsc, d)
            _gather([kp_ref, k_ref], ks, d)
            _gather([vp_ref, v_ref], vs, d)
            dkb[...] = jnp.zeros_like(dkb)
            dvb[...] = jnp.zeros_like(dvb)

            def unit(u, c_):
                r, b = u // nb, u % nb
                rows = pl.ds(pl.multiple_of(u * BAND, BAND), BAND)
                qu, dou = qs[rows, :], dos[rows, :]
                la, da = pieces(lsc[rows, :]), pieces(dsc[rows, :])
                q2 = jnp.concatenate([qu * hm[0], qu * hm[1]], axis=0)
                do2 = jnp.concatenate([dou * hm[0], dou * hm[1]], axis=0)
                l2 = jnp.concatenate([la * hm[0], la * hm[1]], axis=0)
                d2 = jnp.concatenate([da * hm[0], da * hm[1]], axis=0)
                start = pl.multiple_of(r * 2 * per + per + (b - 1) * BAND, BAND)
                kw = ks[pl.ds(start, 2 * BAND), :]
                vw = vs[pl.ds(start, 2 * BAND), :]
                lo = jnp.where(jnp.logical_and(i == 0, b == 0), BAND, 0)
                ok = jnp.logical_and(band_ok, key >= lo)
                st = _dot_nt(jnp.concatenate([kw, ones], axis=1), jnp.concatenate([q2, l2], axis=1))
                dpt = _dot_nt(jnp.concatenate([vw, ones], axis=1), jnp.concatenate([do2, d2], axis=1))
                pt = jnp.where(ok, jnp.exp(st), 0.0)
                dst = (pt * dpt).astype(BF16)
                acc = pl.ds(pl.multiple_of(r * pad + b * BAND, BAND), 2 * BAND)
                dkb[acc, :] += _dot(dst, q2)
                dvb[acc, :] += _dot(pt.astype(BF16), do2)
                dq2 = _dot_tn(dst, kw)
                dqb[rows, :] = jnp.where(head0, dq2[:BAND], dq2[BAND:])
                return c_

            lax.fori_loop(0, UNITS, unit, 0, unroll=16)

            for r in range(d):
                last = pl.ds(r * pad + per, BAND)
                kept = pl.ds(offs[g] + r * BAND, BAND)

                @pl.when(step > 0)
                def _():
                    dkb[last, :] += ckb[kept, :]
                    dvb[last, :] += cvb[kept, :]

                ckb[kept, :] = dkb[pl.ds(r * pad, BAND), :]
                cvb[kept, :] = dvb[pl.ds(r * pad, BAND), :]
            _scatter_add(dq_ref, dqb, d, per, 0, g == 0)
            _scatter_add(dk_ref, dkb, d, pad, BAND, g == 0)
            _scatter_add(dv_ref, dvb, d, pad, BAND, g == 0)

    main = pl.BlockSpec((TOK, BAND), lambda j, s: (nblk - 1 - s, j))
    prev = pl.BlockSpec((TOK, BAND), lambda j, s: (jnp.maximum(nblk - 2 - s, 0), j))
    vmain = pl.BlockSpec((TOK, BAND), lambda j, s: (nblk - 1 - s, j + v_col))
    vprev = pl.BlockSpec((TOK, BAND), lambda j, s: (jnp.maximum(nblk - 2 - s, 0), j + v_col))
    acc_rows = max(d * (TOK // d + BAND) for d in DILATIONS)
    kept_rows = sum(DILATIONS) * BAND
    return _call(
        body, [q, k, k, v, v, do, lse, dd], name="attn_bwd",
        grid=(D_ATTN // BAND, nblk), out_shape=[_sds((t, D_ATTN), F32)] * 3,
        in_specs=[main, prev, main, vprev, vmain, main, main, main], out_specs=[main] * 3,
        scratch_shapes=[pltpu.VMEM((TOK, BAND), BF16)] * 2 + [pltpu.VMEM((2 * TOK, BAND), BF16)] * 2
        + [pltpu.VMEM((TOK, BAND), F32)] * 3 + [pltpu.VMEM((acc_rows, BAND), F32)] * 2
        + [pltpu.VMEM((kept_rows, BAND), F32)] * 2,
        semantics=("parallel", "arbitrary"), carry=carry)


def _halo_rows(tm, t):
    per = tm // 8
    prev = lambda i: (jnp.maximum(i * per - 1, 0), 0)
    nxt = lambda i: (jnp.minimum((i + 1) * per, t // 8 - 1), 0)
    return prev, nxt


def _mixer_out(z, cw, y_attn, g_conv, g_attn, tm, carry=None):
    t = z.shape[0]
    prev, _ = _halo_rows(tm, t)

    def body(z_ref, zp_ref, cw_ref, y_ref, gc_ref, ga_ref, mix_ref):
        i = pl.program_id(0)
        keep = jnp.where(i > 0, 1.0, 0.0)
        cu = jnp.concatenate([zp_ref[:, 0:512] * zp_ref[:, 1024:1536] * keep,
                              z_ref[:, 0:512] * z_ref[:, 1024:1536]], axis=0)
        c = (cw_ref[0:1, :] * pltpu.roll(cu, 2, 0) + cw_ref[1:2, :] * pltpu.roll(cu, 1, 0)
             + cw_ref[2:3, :] * cu)[8:, :]
        yc = z_ref[:, 512:1024] * c
        mix_ref[:, 0:512] = (yc * _rms_scale(yc) * gc_ref[...]).astype(BF16)
        ya = y_ref[...]
        mix_ref[:, 512:1024] = (ya * _rms_scale(ya) * ga_ref[...]).astype(BF16)

    blk = pl.BlockSpec((tm, 512), lambda i: (i, 0))
    vec = pl.BlockSpec((1, 512), lambda i: (0, 0))
    return _call(
        body, [z, z, cw, y_attn, g_conv, g_attn], name="mixer_out", grid=(t // tm,),
        out_shape=_sds((t, 1024), BF16),
        in_specs=[pl.BlockSpec((tm, 1536), lambda i: (i, 0)), pl.BlockSpec((8, 1536), prev),
                  pl.BlockSpec((8, 512), lambda i: (0, 0)), blk, vec, vec],
        out_specs=pl.BlockSpec((tm, 1024), lambda i: (i, 0)),
        semantics=("parallel",), carry=carry)


def _mixer_bwd(z, dmix, y_attn, cw, g_conv, g_attn, ones_bd, tm, carry=None):
    t = z.shape[0]
    nblk = t // tm
    prev, nxt = _halo_rows(tm, t)
    e = tm + 16

    def body(z_ref, zp_ref, zn_ref, dm_ref, dmn_ref, y_ref, cw_ref, gc_ref, ga_ref, bd_ref,
             dz_ref, do_ref, dd_ref, dcw_ref, dgc_ref, dga_ref):
        i = pl.program_id(0)
        rows = lax.broadcasted_iota(jnp.int32, (e, 1), 0)
        lo = jnp.where(i > 0, 0, 8)
        hi = jnp.where(i < nblk - 1, e, tm + 8)
        ze = jnp.concatenate([zp_ref[...], z_ref[...], zn_ref[...]], axis=0)
        u, gb, gcv = ze[:, 0:512], ze[:, 512:1024], ze[:, 1024:1536]
        w0, w1, w2 = cw_ref[0:1, :], cw_ref[1:2, :], cw_ref[2:3, :]
        cu = jnp.where(rows >= lo, gcv * u, 0.0)
        cu1, cu2 = pltpu.roll(cu, 1, 0), pltpu.roll(cu, 2, 0)
        c = w0 * cu2 + w1 * cu1 + w2 * cu
        yc = gb * c
        dma = jnp.concatenate([jnp.zeros((8, 512), F32), dm_ref[:, 0:512], dmn_ref[...]], axis=0)
        dyc, ych = _rms_bwd(yc, _rms_scale(yc), gc_ref[...], dma)
        dc = jnp.where(jnp.logical_and(rows >= 8, rows < hi), dyc * gb, 0.0)
        dcu = w0 * pltpu.roll(dc, e - 2, 0) + w1 * pltpu.roll(dc, e - 1, 0) + w2 * dc
        mid = slice(8, 8 + tm)
        dz_ref[:, 0:512] = (dcu * gcv)[mid, :].astype(BF16)
        dz_ref[:, 512:1024] = (dyc * c)[mid, :].astype(BF16)
        dz_ref[:, 1024:1536] = (dcu * u)[mid, :].astype(BF16)

        ya = y_ref[...]
        dmb = dm_ref[:, 512:1024]
        dya, yah = _rms_bwd(ya, _rms_scale(ya), ga_ref[...], dmb)
        do_ref[...] = dya
        dd_ref[...] = _head_sum(dya * ya, bd_ref[...])

        @pl.when(i == 0)
        def _():
            dcw_ref[...] = jnp.zeros_like(dcw_ref)
            dgc_ref[...] = jnp.zeros_like(dgc_ref)
            dga_ref[...] = jnp.zeros_like(dga_ref)

        dcm = jnp.where(rows < tm + 8, dc, 0.0)
        dcw_ref[0:1, :] += jnp.sum(dcm * cu2, axis=0, keepdims=True)
        dcw_ref[1:2, :] += jnp.sum(dcm * cu1, axis=0, keepdims=True)
        dcw_ref[2:3, :] += jnp.sum(dcm * cu, axis=0, keepdims=True)
        dgc_ref[...] += jnp.sum((dma * ych)[mid, :], axis=0, keepdims=True)
        dga_ref[...] += jnp.sum(dmb * yah, axis=0, keepdims=True)

    blk = pl.BlockSpec((tm, 512), lambda i: (i, 0))
    vec = pl.BlockSpec((1, 512), lambda i: (0, 0))
    cwb = pl.BlockSpec((8, 512), lambda i: (0, 0))
    return _call(
        body, [z, z, z, dmix, dmix, y_attn, cw, g_conv, g_attn, ones_bd], name="mixer_bwd",
        grid=(nblk,),
        out_shape=[_sds((t, 1536), BF16), _sds((t, 512), F32), _sds((t, 512), F32),
                   _sds((8, 512), F32), _sds((1, 512), F32), _sds((1, 512), F32)],
        in_specs=[pl.BlockSpec((tm, 1536), lambda i: (i, 0)), pl.BlockSpec((8, 1536), prev),
                  pl.BlockSpec((8, 1536), nxt), pl.BlockSpec((tm, 1024), lambda i: (i, 0)),
                  pl.BlockSpec((8, 512), nxt), blk, cwb, vec, vec,
                  pl.BlockSpec((512, 512), lambda i: (0, 0))],
        out_specs=[pl.BlockSpec((tm, 1536), lambda i: (i, 0)), blk, blk, cwb, vec, vec],
        carry=carry)


def _qkv_bwd(z, dzc, dqn, dkn, dv, gq, gk, ones_bd, tm, carry=None):
    t = z.shape[0]

    def body(zq_ref, zk_ref, dzc_ref, dqn_ref, dkn_ref, dv_ref, gq_ref, gk_ref, bd_ref,
             dz_ref, dgq_ref, dgk_ref):
        bd = bd_ref[...]

        @pl.when(pl.program_id(0) == 0)
        def _():
            dgq_ref[...] = jnp.zeros_like(dgq_ref)
            dgk_ref[...] = jnp.zeros_like(dgk_ref)

        def back(v, dn, g, scale):
            r = _head_rms_scale(v, bd)
            vh = v * r
            dh = dn * (g * scale)
            dv = r * (dh - vh * (_head_sum(dh * vh, bd) * (1.0 / HEAD_DIM)))
            return dv, jnp.sum(dn * scale * vh, axis=0, keepdims=True)

        dq, dgq = back(zq_ref[...], dqn_ref[...], gq_ref[...], HEAD_DIM ** -0.5)
        dk, dgk = back(zk_ref[...], dkn_ref[...], gk_ref[...], 1.0)
        dgq_ref[...] += dgq
        dgk_ref[...] += dgk
        dz_ref[:, 0:1536] = dzc_ref[...]
        dz_ref[:, 1536:2048] = dq.astype(BF16)
        dz_ref[:, 2048:2560] = dk.astype(BF16)
        dz_ref[:, 2560:3072] = dv_ref[...].astype(BF16)

    blk = pl.BlockSpec((tm, 512), lambda i: (i, 0))
    vec = pl.BlockSpec((1, 512), lambda i: (0, 0))
    return _call(
        body, [z, z, dzc, dqn, dkn, dv, gq, gk, ones_bd], name="qkv_bwd", grid=(t // tm,),
        out_shape=[_sds((t, D_IN), BF16), _sds((1, 512), F32), _sds((1, 512), F32)],
        in_specs=[pl.BlockSpec((tm, 512), lambda i: (i, 3)), pl.BlockSpec((tm, 512), lambda i: (i, 4)),
                  pl.BlockSpec((tm, 1536), lambda i: (i, 0))] + [blk] * 3
        + [vec, vec, pl.BlockSpec((512, 512), lambda i: (0, 0))],
        out_specs=[pl.BlockSpec((tm, D_IN), lambda i: (i, 0)), vec, vec],
        carry=carry)


def _columns_from_chips(g):
    return g.transpose(1, 0, 2).reshape(g.shape[1], N_CHIPS * g.shape[2])


def kernel(x, g_mix, w_in, conv_w, g_q, g_k, g_conv_out, g_attn_out, w_out, g_ffn, w_gate, w_up, w_down, loss_target, m_g_mix, m_w_in, m_conv_w, m_g_q, m_g_k, m_g_conv_out, m_g_attn_out, m_w_out, m_g_ffn, m_w_gate, m_w_up, m_w_down, v_g_mix, v_w_in, v_conv_w, v_g_q, v_g_k, v_g_conv_out, v_g_attn_out, v_w_out, v_g_ffn, v_w_gate, v_w_up, v_w_down):
    t = x.shape[1]
    xs = x[0]
    target = loss_target[0]
    tm = min(512, t)
    tmm = min(1024, t)

    cw_pad = jnp.pad(conv_w[0], ((0, 13), (0, 0)))
    gathered = _all_gather([w_in[0].astype(BF16), cw_pad])
    win = _columns_from_chips(gathered[0])
    cw = jnp.pad(gathered[1][:, 0:3, :].transpose(1, 0, 2).reshape(3, D_CONV), ((0, 5), (0, 0)))
    later = [w_out[0].astype(BF16), w_gate[0].T.astype(BF16), w_up[0].T.astype(BF16),
             w_down[0].astype(BF16)]

    head_id = jnp.arange(D_ATTN) // HEAD_DIM
    ones_bd = (head_id[:, None] == head_id[None, :]).astype(BF16)
    gq_t = jnp.tile(g_q, (1, D_ATTN // HEAD_DIM))
    gk_t = jnp.tile(g_k, (1, D_ATTN // HEAD_DIM))

    h1, z = _norm_matmul("in_proj", xs, g_mix, [win], tm, D_IN, False)
    q, k = _qkv_prepare(z, gq_t, gk_t, ones_bd, tm)
    v_col = (3 * D_CONV + 2 * D_ATTN) // BAND
    (y_attn, lse), gathered = _attn_fwd(q, k, z, v_col, carry=_x_gather_chips(later))
    mix, gathered = _mixer_out(z, cw, y_attn, g_conv_out, g_attn_out, tm,
                               carry=_x_gather_sibling(gathered))
    wout = gathered[0].reshape(D_MODEL, D_MODEL)
    wgate_t = gathered[1].reshape(D_FF, D_MODEL)
    wup_t = gathered[2].reshape(D_FF, D_MODEL)
    wdown = gathered[3].reshape(D_FF, D_MODEL)
    (x1,) = _matmul("out_proj", mix, wout, [xs], [F32], lambda acc, r: (r + acc,), tm, D_MODEL)
    h2, gate, up, act = _norm_matmul("ffn_up", x1, g_ffn, [wgate_t, wup_t], tm, D_FF, True, BF16,
                                     transposed_w=True)

    def loss_epilogue(acc, r, tgt):
        err = r + acc - tgt
        dy = err * (1.0 / D_MODEL)
        return dy, dy, jnp.sum(err * err)

    dx2, dx2b, loss_sum = _matmul("ffn_down_loss", act, wdown, [x1, target], [F32, BF16],
                                  loss_epilogue, tm, D_MODEL, loss=True)

    def swiglu_bwd(da, gt, u):
        gt, u = gt.astype(F32), u.astype(F32)
        s = _sigmoid(gt)
        return da * u * (s * (1.0 + gt * (1.0 - s))), da * (gt * s)

    dgate, dup = _matmul("ffn_down_bwd", dx2b, wdown, [gate, up], [BF16, BF16], swiglu_bwd,
                         tm, D_FF, transposed_w=True)
    gw_down = _matmul_tn("grad_w_down", act, dx2b, 512, tmm)
    gw_gate_t = _matmul_tn("grad_w_gate", dgate, h2, 512, tmm)
    gw_up_t = _matmul_tn("grad_w_up", dup, h2, 512, tmm)

    me = 2 * lax.axis_index("x") + lax.axis_index("y")
    where = jnp.stack([lax.axis_index("c"), me]).astype(jnp.int32)

    def pair_sums(names, full, got):
        return [_pair_sum(f"pair_sum_{nme}", a, b, where) for nme, a, b in zip(names, full, got)]

    def chip_sums(names, pair, got):
        return [_chip_sum(f"chip_sum_{nme}", own, b) for nme, (_, own), b in zip(names, pair, got)]

    ffn = ["w_gate", "w_up", "w_down"]
    full = [g.reshape(N_CHIPS, D_FF // N_CHIPS, D_MODEL) for g in (gw_gate_t, gw_up_t, gw_down)]
    (dx1, dx1b, gg_ffn), got = _matmul_norm_bwd(
        "ffn_up_bwd", [(dgate, wgate_t), (dup, wup_t)], x1, dx2, g_ffn, tm, carry=_x_pair(full),
        transposed_w=False)
    pair = pair_sums(ffn, full, got)
    (dmix,) = _matmul("out_proj_bwd", dx1b, wout, [], [F32], lambda acc: (acc,), tm, D_MODEL,
                      transposed_w=True)
    gw_out = _matmul_tn("grad_w_out", mix, dx1b, 512, tmm)
    full = [gw_out.reshape(N_CHIPS, D_MODEL // N_CHIPS, D_MODEL)]
    (dzc, do, dd, gcw, gg_conv, gg_attn), got = _mixer_bwd(
        z, dmix, y_attn, cw, g_conv_out, g_attn_out, ones_bd, tm, carry=_x_pair(full))
    pair += pair_sums(["w_out"], full, got)
    early = ffn + ["w_out"]
    (dqn, dkn, dv), got = _attn_bwd(q, k, z, v_col, do, lse, dd,
                                    carry=_x_chips([p for p, _ in pair]))
    mine = chip_sums(early, pair, got)
    (dz, gg_q, gg_k), theirs = _qkv_bwd(z, dzc, dqn, dkn, dv, gq_t, gk_t, ones_bd, tm,
                                        carry=_x_share(mine))
    full = [_matmul_tn("grad_w_in", h1, dz, D_IN // N_CHIPS, tmm, by_chip=True)]
    nb = t // tm
    cuts = [0, nb // 4, 3 * nb // 4, nb]

    def in_proj_bwd(part, into, carry):
        return _matmul_norm_bwd(f"in_proj_bwd_{part}", [(dz, win)], xs, dx1, g_mix, tm, carry=carry,
                                blocks=(cuts[part], cuts[part + 1] - cuts[part]), into=into)

    (dxa, dxb, gg_a), got = in_proj_bwd(0, None, _x_pair(full))
    pair = pair_sums(["w_in"], full, got)
    (dxa, dxb, gg_b), got = in_proj_bwd(1, [dxa, dxb], _x_chips([pair[0][0]]))
    mine += chip_sums(["w_in"], pair, got)
    grad_x, _, gg_c = in_proj_bwd(2, [dxa, dxb], None)
    gg_mix = (gg_a + gg_b) + gg_c
    theirs = list(theirs) + list(_exchange_alone("grad_pair_share_w_in", _x_share(mine[-1:])))
    big = early + ["w_in"]

    small = _small_all_reduce({
        "g_mix": gg_mix, "g_ffn": gg_ffn, "g_conv_out": gg_conv, "g_attn_out": gg_attn,
        "g_q": gg_q, "g_k": gg_k, "loss": loss_sum, "conv_w": gcw})
    heads = D_ATTN // HEAD_DIM
    grads = {
        "g_mix": small[0:1, :], "g_ffn": small[1:2, :],
        "g_conv_out": small[2:3, 0:512], "g_attn_out": small[2:3, 512:1024],
        "g_q": small[3, 0:512].reshape(heads, HEAD_DIM).sum(axis=0)[None, :],
        "g_k": small[3, 512:1024].reshape(heads, HEAD_DIM).sum(axis=0)[None, :],
        "conv_w": lax.dynamic_slice(small[8:11, 0:512], (0, me * (D_CONV // N_CHIPS)),
                                    (3, D_CONV // N_CHIPS)),
    }
    halves = dict(zip(big, zip(mine, theirs)))
    loss = small[4, 0] * 0.5 * (1.0 / D_MODEL)

    weights = dict(g_mix=g_mix, w_in=w_in, conv_w=conv_w, g_q=g_q, g_k=g_k, g_conv_out=g_conv_out,
                   g_attn_out=g_attn_out, w_out=w_out, g_ffn=g_ffn, w_gate=w_gate, w_up=w_up,
                   w_down=w_down)
    moments_m = dict(g_mix=m_g_mix, w_in=m_w_in, conv_w=m_conv_w, g_q=m_g_q, g_k=m_g_k,
                     g_conv_out=m_g_conv_out, g_attn_out=m_g_attn_out, w_out=m_w_out, g_ffn=m_g_ffn,
                     w_gate=m_w_gate, w_up=m_w_up, w_down=m_w_down)
    moments_v = dict(g_mix=v_g_mix, w_in=v_w_in, conv_w=v_conv_w, g_q=v_g_q, g_k=v_g_k,
                     g_conv_out=v_g_conv_out, g_attn_out=v_g_attn_out, w_out=v_w_out, g_ffn=v_g_ffn,
                     w_gate=v_w_gate, w_up=v_w_up, w_down=v_w_down)
    names = list(weights)
    out_g, out_d, out_m, out_v = [], [], [], []
    for nme in names:
        wgt = weights[nme]
        shape2 = wgt.shape[-2:] if wgt.ndim == 3 else wgt.shape
        flip = nme in ("w_gate", "w_up")

        def to2d(a):
            return a.reshape(shape2).T if flip else a.reshape(shape2)

        def back(a):
            return (a.T if flip else a).reshape(wgt.shape)

        state = (to2d(wgt), to2d(moments_m[nme]), to2d(moments_v[nme]))
        if nme in halves:
            g2, dlt, nm, nv = _adamw_shard(f"adamw_{nme}", *state, *halves[nme], where)
        else:
            g2 = grads[nme].reshape(shape2)
            dlt, nm, nv = _adamw(f"adamw_{nme}", state[0], g2, state[1], state[2])
        out_g.append(back(g2))
        out_d.append(back(dlt))
        out_m.append(back(nm))
        out_v.append(back(nv))
    return (loss, grad_x[None], *out_g, *out_d, *out_m, *out_v)
```

```python
import functools
from typing import Any, Callable, NamedTuple, Sequence

import jax
import jax.numpy as jnp
from jax import lax
from jax.experimental import pallas as pl
from jax.experimental.pallas import tpu as pltpu

F32 = jnp.float32
BF16 = jnp.bfloat16
MESH = pl.DeviceIdType.MESH

D_MODEL = 1024
D_CONV = 512
D_ATTN = 512
HEAD_DIM = 64
D_FF = 2816
D_IN = 3 * D_CONV + 3 * D_ATTN
DILATIONS = (1, 4, 16)
BAND = 128
EPS = 1e-6
NEG = -1e30
N_CHIPS = 4

ADAM_LR = 0.001
ADAM_B1 = 0.9
ADAM_B2 = 0.999
ADAM_EPS = 1e-08
ADAM_WD = 0.01
ADAM_STEP = 10

V7X_VMEM_BYTES = 64 * 1024 * 1024
VMEM_LIMIT = V7X_VMEM_BYTES - 8 * 1024 * 1024
ANY = pl.BlockSpec(memory_space=pl.ANY)
VMEM_WHOLE = pl.BlockSpec(memory_space=pltpu.VMEM)


def _params(*sem):
    return pltpu.CompilerParams(dimension_semantics=sem, vmem_limit_bytes=VMEM_LIMIT)


def _sds(shape, dtype):
    return jax.ShapeDtypeStruct(shape, dtype)


def _resident(whole):
    return pl.Buffered(1) if whole else None


def _place():
    x, y, c = lax.axis_index("x"), lax.axis_index("y"), lax.axis_index("c")
    chips = [(1 - x, y), (x, 1 - y), (1 - x, 1 - y)]
    return x, y, c, 2 * x + y, chips, [2 * cx + cy for cx, cy in chips]


def _all_gather(shards):
    n = len(shards)

    def body(*refs):
        ins, outs, stage = refs[:n], refs[n:2 * n], refs[2 * n:3 * n]
        ssem, rsem, fsem, gsem, lsem, osem = refs[3 * n:]
        x, y, c, me, chips, cids = _place()
        sib = (x, y, 1 - c)

        def half(w, which):
            h = shards[w].shape[0] // 2
            return pl.ds(pl.multiple_of(which * h, 8), h)

        loads = [pltpu.make_async_copy(ins[w], stage[w], lsem.at[w]) for w in range(n)]
        local = [pltpu.make_async_copy(stage[w], outs[w].at[me], osem.at[w]) for w in range(n)]
        for cp in loads:
            cp.start()

        def chip_copy(w, j, src_slot):
            rows = half(w, c)
            return pltpu.make_async_remote_copy(
                src_ref=ins[w].at[rows], dst_ref=outs[w].at[src_slot, rows],
                send_sem=ssem.at[3 * w + j], recv_sem=rsem.at[3 * w + j],
                device_id=(*chips[j], c), device_id_type=MESH)

        def sib_copy(w, j, which):
            rows = half(w, which)
            return pltpu.make_async_remote_copy(
                src_ref=outs[w].at[cids[j], rows], dst_ref=outs[w].at[cids[j], rows],
                send_sem=fsem.at[3 * w + j], recv_sem=gsem.at[3 * w + j],
                device_id=sib, device_id_type=MESH)

        sends = [chip_copy(w, j, me) for w in range(n) for j in range(3)]
        for cp in sends:
            cp.start()
        for w in range(n):
            loads[w].wait()
            local[w].start()
        passed = []
        for w in range(n):
            for j in range(3):
                chip_copy(w, j, cids[j]).wait_recv()
                cp = sib_copy(w, j, c)
                cp.start()
                passed.append(cp)
        for w in range(n):
            for j in range(3):
                sib_copy(w, j, 1 - c).wait_recv()
        for cp in sends + passed:
            cp.wait_send()
        for cp in local:
            cp.wait()

    return pl.pallas_call(
        body, name="all_gather_weights",
        out_shape=[_sds((N_CHIPS,) + s.shape, s.dtype) for s in shards],
        in_specs=[ANY] * n, out_specs=[ANY] * n,
        scratch_shapes=[pltpu.VMEM(s.shape, s.dtype) for s in shards]
        + [pltpu.SemaphoreType.DMA((3 * n,))] * 4 + [pltpu.SemaphoreType.DMA((n,))] * 2,
        compiler_params=pltpu.CompilerParams(vmem_limit_bytes=VMEM_LIMIT),
    )(*shards)


class _Exchange(NamedTuple):
    srcs: Sequence[Any]
    lands: Sequence[Any]
    outs: Sequence[Any]
    n_sems: int
    copies: Callable


def _remote(src, dst, ssem, rsem, k, to):
    return pltpu.make_async_remote_copy(src_ref=src, dst_ref=dst, send_sem=ssem.at[k],
                                        recv_sem=rsem.at[k], device_id=to, device_id_type=MESH)


def _x_gather_chips(shards):
    def copies(srcs, lands, outs, ssem, rsem):
        _, _, c, me, chips, cids = _place()
        go, arrive = [], []
        for w, s in enumerate(shards):
            h = s.shape[0] // 2
            rows = pl.ds(pl.multiple_of(c * h, 8), h)
            for j in range(3):
                to = (*chips[j], c)
                go.append(_remote(srcs[w].at[rows], lands[w].at[me, rows], ssem, rsem, 3 * w + j, to))
                arrive.append(_remote(srcs[w].at[rows], lands[w].at[cids[j], rows], ssem, rsem,
                                      3 * w + j, to))
        return go, arrive

    lands = [jnp.broadcast_to(s[None], (N_CHIPS,) + s.shape) for s in shards]
    return _Exchange(shards, lands, [], 3 * len(shards), copies)


def _x_gather_sibling(gathered):
    def copies(srcs, lands, outs, ssem, rsem):
        x, y, c, _, _, cids = _place()
        go, arrive = [], []
        for w, g in enumerate(gathered):
            h = g.shape[1] // 2
            mine = pl.ds(pl.multiple_of(c * h, 8), h)
            theirs = pl.ds(pl.multiple_of((1 - c) * h, 8), h)
            for j in range(3):
                slab = lands[w].at[cids[j]]
                go.append(_remote(slab.at[mine], slab.at[mine], ssem, rsem, 3 * w + j, (x, y, 1 - c)))
                arrive.append(_remote(slab.at[theirs], slab.at[theirs], ssem, rsem, 3 * w + j,
                                      (x, y, 1 - c)))
        return go, arrive

    return _Exchange([], gathered, [], 3 * len(gathered), copies)


def _x_pair(grads):
    def copies(srcs, lands, outs, ssem, rsem):
        x, y, c, _, _, _ = _place()
        go = []
        for w, g in enumerate(grads):
            h = g.shape[1] // 2
            theirs = pl.ds(pl.multiple_of((1 - c) * h, 8), h)
            go.append(_remote(srcs[w].at[:, theirs, :], outs[w], ssem, rsem, w, (x, y, 1 - c)))
        return go, go

    outs = [_sds((N_CHIPS, g.shape[1] // 2, g.shape[2]), g.dtype) for g in grads]
    return _Exchange(grads, [], outs, len(grads), copies)


def _x_chips(parts):
    def copies(srcs, lands, outs, ssem, rsem):
        _, _, c, _, chips, cids = _place()
        go = [_remote(srcs[w].at[cids[j]], outs[w].at[j], ssem, rsem, 3 * w + j, (*chips[j], c))
              for w in range(len(parts)) for j in range(3)]
        return go, go

    outs = [_sds((3,) + p.shape[1:], p.dtype) for p in parts]
    return _Exchange(parts, [], outs, 3 * len(parts), copies)


def _x_share(halves):
    def copies(srcs, lands, outs, ssem, rsem):
        x, y, c, _, _, _ = _place()
        go = [_remote(srcs[w], outs[w], ssem, rsem, w, (x, y, 1 - c)) for w in range(len(halves))]
        return go, go

    return _Exchange(halves, [], [_sds(h.shape, h.dtype) for h in halves], len(halves), copies)


def _call(body, args, *, name, grid, in_specs, out_specs, out_shape, scratch_shapes=(),
          semantics=None, carry=None):
    single = not isinstance(out_shape, (list, tuple))
    out_shape = [out_shape] if single else list(out_shape)
    out_specs = [out_specs] if single else list(out_specs)
    if carry is None:
        res = pl.pallas_call(
            body, name=name, grid=grid, in_specs=list(in_specs), out_specs=out_specs,
            out_shape=out_shape, scratch_shapes=list(scratch_shapes),
            compiler_params=_params(*(semantics or ("arbitrary",) * len(grid))))(*args)
        return res[0] if single else res
    n_in, n_out, n_scr = len(args), len(out_shape), len(scratch_shapes)
    n_src, n_land, n_new = len(carry.srcs), len(carry.lands), len(carry.outs)

    def carrying(*refs):
        at = 0
        parts = []
        for n in (n_in, n_src, n_land, n_out, n_land, n_new, n_scr, 2):
            parts.append(refs[at:at + n])
            at += n
        ins, srcs, _, outs, lands, news, scratch, (ssem, rsem) = parts
        ids = [pl.program_id(a) for a in range(len(grid))]
        first = functools.reduce(jnp.logical_and, [i == 0 for i in ids])
        last = functools.reduce(jnp.logical_and, [i == g - 1 for i, g in zip(ids, grid)])
        go, arrive = carry.copies(srcs, lands, news, ssem, rsem)

        @pl.when(first)
        def _():
            for cp in go:
                cp.start()

        body(*ins, *outs, *scratch)

        @pl.when(last)
        def _():
            for cp in go:
                cp.wait_send()
            for cp in arrive:
                cp.wait_recv()

    res = pl.pallas_call(
        carrying, name=name, grid=grid,
        in_specs=list(in_specs) + [ANY] * (n_src + n_land),
        out_specs=out_specs + [ANY] * (n_land + n_new),
        out_shape=out_shape + [_sds(a.shape, a.dtype) for a in carry.lands] + list(carry.outs),
        input_output_aliases={n_in + n_src + i: n_out + i for i in range(n_land)},
        scratch_shapes=list(scratch_shapes) + [pltpu.SemaphoreType.DMA((carry.n_sems,))] * 2,
        compiler_params=_params(*(("arbitrary",) * len(grid))))(*args, *carry.srcs, *carry.lands)
    own = res[:n_out]
    return (own[0] if single else own), res[n_out:]


def _exchange_alone(name, exchange):
    def body(x_ref, o_ref):
        o_ref[...] = x_ref[...]

    blk = pl.BlockSpec((8, 128), lambda i: (0, 0))
    _, res = _call(body, [jnp.zeros((8, 128), F32)], name=name, grid=(1,), in_specs=[blk],
                   out_specs=blk, out_shape=_sds((8, 128), F32), carry=exchange)
    return res


def _row_block(r, want):
    return max(d for d in range(1, min(want, r) + 1) if r % d == 0 and (d % 8 == 0 or d == r))


def _pair_sum(name, full, got, where):
    _, r, n = full.shape
    h = r // 2
    tr = _row_block(h, 256)
    nb = h // tr

    def body(w_ref, a_ref, b_ref, o_ref, own_ref):
        total = a_ref[...] + b_ref[...]
        o_ref[...] = total.astype(BF16)

        @pl.when(pl.program_id(1) == w_ref[1])
        def _():
            own_ref[...] = total[0]

    blk = pl.BlockSpec((1, tr, n), lambda i, s, w: (s, i, 0))
    return pl.pallas_call(
        body, name=name, out_shape=[_sds(got.shape, BF16), _sds((h, n), F32)],
        grid_spec=pltpu.PrefetchScalarGridSpec(
            num_scalar_prefetch=1, grid=(nb, N_CHIPS),
            in_specs=[pl.BlockSpec((1, tr, n), lambda i, s, w: (s, w[0] * nb + i, 0)), blk],
            out_specs=[blk, pl.BlockSpec((tr, n), lambda i, s, w: (i, 0))]),
        compiler_params=_params("parallel", "arbitrary"),
    )(where, full, got)


def _chip_sum(name, own, got):
    h, n = own.shape
    tr = _row_block(h, 256)

    def body(a_ref, b0, b1, b2, o_ref):
        o_ref[...] = ((a_ref[...] + b0[0].astype(F32)) + b1[0].astype(F32)) + b2[0].astype(F32)

    def slot(j):
        return pl.BlockSpec((1, tr, n), lambda i: (j, i, 0))

    blk = pl.BlockSpec((tr, n), lambda i: (i, 0))
    return pl.pallas_call(
        body, name=name, grid=(h // tr,), out_shape=_sds((h, n), F32),
        in_specs=[blk, slot(0), slot(1), slot(2)], out_specs=blk,
        compiler_params=_params("parallel"),
    )(own, got, got, got)


SMALL_ROWS = 16
SMALL_LAYOUT = (
    ("g_mix", 0, 0, 1, 1024), ("g_ffn", 1, 0, 1, 1024), ("g_conv_out", 2, 0, 1, 512),
    ("g_attn_out", 2, 512, 1, 512), ("g_q", 3, 0, 1, 512), ("g_k", 3, 512, 1, 512),
    ("loss", 4, 0, 1, 128), ("conv_w", 8, 0, 8, 512))


def _small_all_reduce(parts):
    names = [s[0] for s in SMALL_LAYOUT]

    def body(*refs):
        ins = refs[:len(names)]
        out_ref, stage, buf, ssem, rsem = refs[len(names):]
        x, y, c, _, _, _ = _place()
        me = 4 * x + 2 * y + c
        stage[...] = jnp.zeros_like(stage)
        for ref, (_, r0, c0, nr, nc) in zip(ins, SMALL_LAYOUT):
            stage[r0:r0 + nr, c0:c0 + nc] = ref[0:nr, :]
        buf[me] = stage[...]
        peers = []
        for d in range(1, 8):
            px = 1 - x if d & 4 else x
            py = 1 - y if d & 2 else y
            pc = 1 - c if d & 1 else c
            peers.append(((px, py, pc), 4 * px + 2 * py + pc))
        sends = [pltpu.make_async_remote_copy(
            src_ref=stage, dst_ref=buf.at[me], send_sem=ssem.at[k], recv_sem=rsem.at[k],
            device_id=peer, device_id_type=MESH) for k, (peer, _) in enumerate(peers)]
        for cp in sends:
            cp.start()
        for k, (peer, pid) in enumerate(peers):
            pltpu.make_async_remote_copy(
                src_ref=stage, dst_ref=buf.at[pid], send_sem=ssem.at[k], recv_sem=rsem.at[k],
                device_id=peer, device_id_type=MESH).wait_recv()
        for cp in sends:
            cp.wait_send()
        acc = buf[0]
        for k in range(1, 8):
            acc = acc + buf[k]
        out_ref[...] = acc

    return pl.pallas_call(
        body, name="small_all_reduce", out_shape=_sds((SMALL_ROWS, 1024), F32),
        in_specs=[VMEM_WHOLE] * len(names), out_specs=VMEM_WHOLE,
        scratch_shapes=[pltpu.VMEM((SMALL_ROWS, 1024), F32), pltpu.VMEM((8, SMALL_ROWS, 1024), F32),
                        pltpu.SemaphoreType.DMA((7,)), pltpu.SemaphoreType.DMA((7,))],
    )(*[parts[k] for k in names])


def _dot(a, b):
    return jnp.dot(a, b, preferred_element_type=F32)


def _dot_nt(a, b):
    return lax.dot_general(a, b, (((1,), (1,)), ((), ())), preferred_element_type=F32)


def _dot_tn(a, b):
    return lax.dot_general(a, b, (((0,), (0,)), ((), ())), preferred_element_type=F32)


def _sigmoid(v):
    return 1.0 / (1.0 + jnp.exp(-v))


def _rms_scale(v):
    return lax.rsqrt(jnp.mean(v * v, axis=-1, keepdims=True) + EPS)


def _rms_bwd(v, r, g, dy):
    vh = v * r
    dh = dy * g
    return r * (dh - vh * jnp.mean(dh * vh, axis=-1, keepdims=True)), vh


def _head_sum(a, ones_bd):
    hi = a.astype(BF16)
    lo = (a - hi.astype(F32)).astype(BF16)
    return _dot(hi, ones_bd) + _dot(lo, ones_bd)


def _head_rms_scale(v, ones_bd):
    return lax.rsqrt(_head_sum(v * v, ones_bd) * (1.0 / HEAD_DIM) + EPS)


MXU_COLUMNS = 256


def _column_chunks(n):
    width = MXU_COLUMNS if n % MXU_COLUMNS == 0 else n
    return [slice(c, c + width) for c in range(0, n, width)]


def _norm_matmul(name, x, g, ws, tm, tn, swiglu, out_dtype=F32, transposed_w=False):
    t, d = x.shape
    n = ws[0].shape[0] if transposed_w else ws[0].shape[1]
    nw = len(ws)

    def body(x_ref, g_ref, *refs):
        w_refs, h_ref, o_refs = refs[:nw], refs[nw], refs[nw + 1:2 * nw + 1]
        hs = refs[-1]

        @pl.when(pl.program_id(1) == 0)
        def _():
            xv = x_ref[...]
            h = (xv * _rms_scale(xv) * g_ref[...]).astype(BF16)
            hs[...] = h
            h_ref[...] = h

        h = hs[...]
        for cols in _column_chunks(tn):
            outs = [_dot_nt(h, w[cols, :]) if transposed_w else _dot(h, w[:, cols]) for w in w_refs]
            for o_ref, o in zip(o_refs, outs):
                o_ref[:, cols] = o.astype(out_dtype)
            if swiglu:
                refs[2 * nw + 1][:, cols] = (outs[0] * _sigmoid(outs[0]) * outs[1]).astype(BF16)

    row = pl.BlockSpec((tm, d), lambda i, j: (i, 0))
    col = pl.BlockSpec((tm, tn), lambda i, j: (i, j))
    out_shape = [_sds((t, d), BF16)] + [_sds((t, n), out_dtype)] * nw
    out_specs = [row] + [col] * nw
    if swiglu:
        out_shape.append(_sds((t, n), BF16))
        out_specs.append(col)
    return pl.pallas_call(
        body, name=name, grid=(t // tm, n // tn), out_shape=out_shape,
        in_specs=[row, pl.BlockSpec((1, d), lambda i, j: (0, 0))]
        + [pl.BlockSpec((tn, d), lambda i, j: (j, 0), pipeline_mode=_resident(tn == n))
           if transposed_w
           else pl.BlockSpec((d, tn), lambda i, j: (0, j), pipeline_mode=_resident(tn == n))] * nw,
        out_specs=out_specs, scratch_shapes=[pltpu.VMEM((tm, d), BF16)],
        compiler_params=_params("parallel", "arbitrary"),
    )(x, g, *ws)


def _matmul(name, a, w, extras, out_dtypes, epilogue, tm, tn, transposed_w=False, loss=False):
    t, k = a.shape
    n = w.shape[0] if transposed_w else w.shape[1]
    ne, no = len(extras), len(out_dtypes)

    def body(a_ref, w_ref, *refs):
        e_refs, o_refs = refs[:ne], refs[ne:]
        a = a_ref[...]
        total = 0.0
        for cols in _column_chunks(tn):
            acc = _dot_nt(a, w_ref[cols, :]) if transposed_w else _dot(a, w_ref[:, cols])
            res = epilogue(acc, *[e[:, cols] for e in e_refs])
            for o_ref, r in zip(o_refs[:no], res[:no]):
                o_ref[:, cols] = r.astype(o_ref.dtype)
            if loss:
                total = total + res[no]
        if loss:
            first = jnp.logical_and(pl.program_id(0) == 0, pl.program_id(1) == 0)

            @pl.when(first)
            def _():
                o_refs[no][...] = jnp.zeros_like(o_refs[no])

            o_refs[no][...] += total

    col = pl.BlockSpec((tm, tn), lambda i, j: (i, j))
    w_spec = (pl.BlockSpec((tn, k), lambda i, j: (j, 0), pipeline_mode=_resident(tn == n))
              if transposed_w
              else pl.BlockSpec((k, tn), lambda i, j: (0, j), pipeline_mode=_resident(tn == n)))
    out_shape = [_sds((t, n), dt) for dt in out_dtypes]
    out_specs = [col] * no
    if loss:
        out_shape.append(_sds((8, 128), F32))
        out_specs.append(pl.BlockSpec((8, 128), lambda i, j: (0, 0)))
    return pl.pallas_call(
        body, name=name, grid=(t // tm, n // tn), out_shape=out_shape,
        in_specs=[pl.BlockSpec((tm, k), lambda i, j: (i, 0)), w_spec] + [col] * ne,
        out_specs=out_specs,
        compiler_params=_params(*(("arbitrary", "arbitrary") if loss else ("parallel", "parallel"))),
    )(a, w, *extras)


def _matmul_norm_bwd(name, pairs, x, dres, g, tm, carry=None, transposed_w=True):
    t, d = x.shape
    npairs = len(pairs)
    product = _dot_nt if transposed_w else _dot

    def body(*refs):
        a_refs, w_refs = refs[:npairs], refs[npairs:2 * npairs]
        x_ref, r_ref, g_ref, dx_ref, dxb_ref, dg_ref = refs[2 * npairs:]
        dy = product(a_refs[0][...], w_refs[0][...])
        for a_ref, w_ref in zip(a_refs[1:], w_refs[1:]):
            dy = dy + product(a_ref[...], w_ref[...])
        xv = x_ref[...]
        dx, xh = _rms_bwd(xv, _rms_scale(xv), g_ref[...], dy)
        dx = dx + r_ref[...]
        dx_ref[...] = dx
        dxb_ref[...] = dx.astype(BF16)

        @pl.when(pl.program_id(0) == 0)
        def _():
            dg_ref[...] = jnp.zeros_like(dg_ref)

        dg_ref[...] += jnp.sum(dy * xh, axis=0, keepdims=True)

    row = pl.BlockSpec((tm, d), lambda i: (i, 0))
    vec = pl.BlockSpec((1, d), lambda i: (0, 0))
    return _call(
        body, [a for a, _ in pairs] + [w for _, w in pairs] + [x, dres, g], name=name,
        grid=(t // tm,), out_shape=[_sds((t, d), F32), _sds((t, d), BF16), _sds((1, d), F32)],
        in_specs=[pl.BlockSpec((tm, a.shape[1]), lambda i: (i, 0)) for a, _ in pairs]
        + [pl.BlockSpec(w.shape, lambda i: (0, 0), pipeline_mode=pl.Buffered(1)) for _, w in pairs]
        + [row, row, vec],
        out_specs=[row, row, vec], carry=carry)


def _matmul_tn(name, a, g, tn, tk, by_chip=False):
    t, ka = a.shape
    n = g.shape[1]

    def body(a_ref, g_ref, o_ref):
        @pl.when(pl.program_id(1) == 0)
        def _():
            o_ref[...] = jnp.zeros_like(o_ref)

        acc = _dot_tn(a_ref[...], g_ref[...])
        o_ref[...] += acc[None] if by_chip else acc

    return pl.pallas_call(
        body, name=name, grid=(n // tn, t // tk),
        out_shape=_sds((n // tn, ka, tn) if by_chip else (ka, n), F32),
        in_specs=[pl.BlockSpec((tk, ka), lambda j, s: (s, 0)),
                  pl.BlockSpec((tk, tn), lambda j, s: (s, j))],
        out_specs=(pl.BlockSpec((1, ka, tn), lambda j, s: (j, 0, 0)) if by_chip
                   else pl.BlockSpec((ka, tn), lambda j, s: (0, j))),
        compiler_params=_params("parallel", "arbitrary"),
    )(a, g)


def _elementwise(name, fn, ins, out_dtypes, tr):
    r, n = ins[0].shape
    tr = _row_block(r, tr)
    ni = len(ins)

    def body(*refs):
        res = fn(*[ref[...] for ref in refs[:ni]])
        for o_ref, v in zip(refs[ni:], res):
            o_ref[...] = v.astype(o_ref.dtype)

    blk = pl.BlockSpec((tr, n), lambda i: (i, 0))
    return pl.pallas_call(
        body, name=name, grid=(r // tr,), out_shape=[_sds((r, n), dt) for dt in out_dtypes],
        in_specs=[blk] * ni, out_specs=[blk] * len(out_dtypes),
        compiler_params=_params("parallel"),
    )(*ins)


def _adamw_update(w, g, m, v):
    m = ADAM_B1 * m + (1.0 - ADAM_B1) * g
    v = ADAM_B2 * v + (1.0 - ADAM_B2) * (g * g)
    m_hat = m / (1.0 - ADAM_B1 ** ADAM_STEP)
    v_hat = v / (1.0 - ADAM_B2 ** ADAM_STEP)
    return -ADAM_LR * (m_hat / (jnp.sqrt(v_hat) + ADAM_EPS) + ADAM_WD * w), m, v


def _adamw(name, w, g, m, v):
    return _elementwise(name, _adamw_update, [w, g, m, v], [F32] * 3, 256)


def _adamw_shard(name, w, m, v, mine, theirs, where):
    r, n = w.shape
    h = r // 2
    tr = _row_block(h, 256)
    nb = h // tr

    def body(w_ref, p_ref, m_ref, v_ref, a_ref, b_ref, g_ref, d_ref, nm_ref, nv_ref):
        g = jnp.where(pl.program_id(0) == w_ref[0], a_ref[...], b_ref[...])
        g_ref[...] = g
        d_ref[...], nm_ref[...], nv_ref[...] = _adamw_update(p_ref[...], g, m_ref[...], v_ref[...])

    whole = pl.BlockSpec((tr, n), lambda s, i, c: (s * nb + i, 0))
    half = pl.BlockSpec((tr, n), lambda s, i, c: (i, 0))
    return pl.pallas_call(
        body, name=name, out_shape=[_sds((r, n), F32)] * 4,
        grid_spec=pltpu.PrefetchScalarGridSpec(
            num_scalar_prefetch=1, grid=(2, nb), in_specs=[whole] * 3 + [half] * 2,
            out_specs=[whole] * 4),
        compiler_params=_params("parallel", "parallel"),
    )(where, w, m, v, mine, theirs)


PAIRS = D_ATTN // BAND


def _qkv_prepare(z, gq, gk, ones_bd, tm):
    t = z.shape[0]
    nd = len(DILATIONS)

    def body(zq_ref, zk_ref, zv_ref, gq_ref, gk_ref, bd_ref, *refs):
        outs, slab = refs[:3 * nd], refs[3 * nd]
        bd = bd_ref[...]
        q = zq_ref[...]
        k = zk_ref[...]
        vals = [(q * _head_rms_scale(q, bd) * gq_ref[...]) * HEAD_DIM ** -0.5,
                k * _head_rms_scale(k, bd) * gk_ref[...], zv_ref[...]]
        for n, val in enumerate(vals):
            for c in range(PAIRS):
                slab[c] = val[:, c * BAND:(c + 1) * BAND]
            for g, d in enumerate(DILATIONS):
                o_ref = outs[n * nd + g]
                for c in range(PAIRS):
                    for r in range(d):
                        rows = slab.at[c][pl.ds(r, tm // d, stride=d), :] if d > 1 else slab[c]
                        o_ref[c, r] = rows.astype(BF16)

    vec = pl.BlockSpec((1, 512), lambda i: (0, 0))
    blk = lambda col: pl.BlockSpec((tm, 512), lambda i: (i, col))
    return pl.pallas_call(
        body, name="qkv_prepare", grid=(t // tm,),
        out_shape=[_sds((PAIRS, d, t // d, BAND), BF16) for _ in range(3) for d in DILATIONS],
        in_specs=[blk(3), blk(4), blk(5), vec, vec, pl.BlockSpec((512, 512), lambda i: (0, 0))],
        out_specs=[pl.BlockSpec((PAIRS, d, tm // d, BAND), lambda i: (0, 0, i, 0))
                   for _ in range(3) for d in DILATIONS],
        scratch_shapes=[pltpu.VMEM((PAIRS, tm, BAND), F32)],
        compiler_params=_params("parallel"),
    )(z, z, z, gq, gk, ones_bd)


TOK = 2048
UNITS = TOK // BAND


def _stack_masks():
    row = lax.broadcasted_iota(jnp.int32, (2 * BAND, 2 * BAND), 0) & (BAND - 1)
    col = lax.broadcasted_iota(jnp.int32, (2 * BAND, 2 * BAND), 1)
    lane = lax.broadcasted_iota(jnp.int32, (BAND, BAND), 1)
    head0 = lane < HEAD_DIM
    ones = [jnp.where(head0, 1.0, 0.0).astype(BF16), jnp.where(head0, 0.0, 1.0).astype(BF16)]
    return col - row, col, head0, ones


def _split3(x):
    hi = x.astype(BF16).astype(F32)
    mid = (x - hi).astype(BF16).astype(F32)
    return hi, mid, x - hi - mid


def _gather(srcs, dst, d):
    per = TOK // d
    at = 0
    for r in range(d):
        for src in srcs:
            rows = src[pl.ds(r, per, stride=d), :] if d > 1 else src[...]
            dst[pl.ds(at, per), :] = rows.astype(dst.dtype)
            at += per


def _scatter_add(out_ref, src, d, per_src, offset, first):
    per = TOK // d
    if d == 1:
        val = src[pl.ds(offset, per), :]
        out_ref[...] = val if first else out_ref[...] + val
        return
    for r in range(d):
        val = src[pl.ds(r * per_src + offset, per), :]
        idx = pl.ds(r, per, stride=d)
        out_ref[idx, :] = val if first else out_ref[idx, :] + val


def _dilated_specs(nblk, reverse):
    def at(s):
        return (nblk - 1 - s) if reverse else s
    main = [pl.BlockSpec((1, d, TOK // d, BAND), lambda j, s: (j, 0, at(s), 0)) for d in DILATIONS]
    prev = [pl.BlockSpec((1, d, TOK // d, BAND), lambda j, s: (j, 0, jnp.maximum(at(s) - 1, 0), 0))
            for d in DILATIONS]
    return main, prev


def _window_rows(prev_ref, main_ref, dst, d):
    per = TOK // d
    for r in range(d):
        dst[pl.ds(r * (per + BAND), BAND), :] = prev_ref[0, r, pl.ds(per - BAND, BAND), :]
        dst[pl.ds(r * (per + BAND) + BAND, per), :] = main_ref[0, r]


def _attn_fwd(qs, ks, vs, carry=None):
    t = qs[0].shape[2]
    nblk = t // TOK
    nd = len(DILATIONS)

    def body(*refs):
        q_refs, kp_refs, k_refs = refs[:nd], refs[nd:2 * nd], refs[2 * nd:3 * nd]
        vp_refs, v_refs = refs[3 * nd:4 * nd], refs[4 * nd:5 * nd]
        y_ref, l_ref, kw_s, vw_s, ob, lb, on, ln = refs[5 * nd:]
        i = pl.program_id(1)
        diff, col, head0, hm = _stack_masks()
        band_ok = jnp.logical_and(diff >= 0, diff <= BAND)
        for g, d in enumerate(DILATIONS):
            per = TOK // d
            nb = per // BAND
            pad = per + BAND
            _window_rows(kp_refs[g], k_refs[g], kw_s, d)
            _window_rows(vp_refs[g], v_refs[g], vw_s, d)
            q_ref = q_refs[g]

            def unit(u, carry):
                r, b = u // nb, u % nb
                qu = q_ref[0, r, pl.ds(pl.multiple_of(b * BAND, BAND), BAND), :]
                start = pl.multiple_of(r * pad + b * BAND, BAND)
                kw = kw_s[pl.ds(start, 2 * BAND), :]
                vw = vw_s[pl.ds(start, 2 * BAND), :]
                lo = jnp.where(jnp.logical_and(i == 0, b == 0), BAND, 0)
                s = _dot_nt(jnp.concatenate([qu * hm[0], qu * hm[1]], axis=0), kw)
                s = jnp.where(jnp.logical_and(band_ok, col >= lo), s, NEG)
                mx = jnp.max(s, axis=-1, keepdims=True)
                e = jnp.exp(s - mx)
                den = jnp.sum(e, axis=-1, keepdims=True)
                o2 = _dot(e.astype(BF16), vw) / den
                l2 = jnp.broadcast_to(mx + jnp.log(den), (2 * BAND, BAND))
                rows = pl.ds(pl.multiple_of(u * BAND, BAND), BAND)
                ob[rows, :] = jnp.where(head0, o2[:BAND], o2[BAND:])
                lb[rows, :] = jnp.where(head0, l2[:BAND], l2[BAND:])
                return carry

            lax.fori_loop(0, UNITS, unit, 0, unroll=16)
            _scatter_add(on.at[g], ob, d, per, 0, True)
            _scatter_add(ln.at[g], lb, d, per, 0, True)
        ls = [ln[0], ln[1], ln[2]]
        mx = jnp.maximum(jnp.maximum(ls[0], ls[1]), ls[2])
        es = [jnp.exp(l - mx) for l in ls]
        tot = es[0] + es[1] + es[2]
        y_ref[...] = (es[0] * on[0] + es[1] * on[1] + es[2] * on[2]) / tot
        l_ref[...] = mx + jnp.log(tot)

    main, prev = _dilated_specs(nblk, False)
    out = pl.BlockSpec((TOK, BAND), lambda j, i: (i, j))
    win_rows = max(d * (TOK // d + BAND) for d in DILATIONS)
    return _call(
        body, list(qs) + list(ks) + list(ks) + list(vs) + list(vs), name="attn_fwd",
        grid=(PAIRS, nblk), out_shape=[_sds((t, D_ATTN), F32)] * 2,
        in_specs=main + prev + main + prev + main, out_specs=[out, out],
        scratch_shapes=[pltpu.VMEM((win_rows, BAND), BF16)] * 2 + [pltpu.VMEM((TOK, BAND), F32)] * 2
        + [pltpu.VMEM((nd, TOK, BAND), F32)] * 2,
        semantics=("parallel", "parallel"), carry=carry)


def _attn_bwd(qs, ks, vs, do, lse, dd, carry=None):
    t = qs[0].shape[2]
    nblk = t // TOK
    nd = len(DILATIONS)
    offs = [sum(DILATIONS[:g]) * BAND for g in range(nd)]

    def body(*refs):
        q_refs, kp_refs, k_refs = refs[:nd], refs[nd:2 * nd], refs[2 * nd:3 * nd]
        vp_refs, v_refs = refs[3 * nd:4 * nd], refs[4 * nd:5 * nd]
        (do_ref, l_ref, d_ref, dq_ref, dk_ref, dv_ref, kw_s, vw_s, dos, lsc, dsc, dqb, dkb, dvb, ckb,
         cvb) = refs[5 * nd:]
        step = pl.program_id(1)
        i = nblk - 1 - step
        key = lax.broadcasted_iota(jnp.int32, (2 * BAND, 2 * BAND), 0)
        qry = lax.broadcasted_iota(jnp.int32, (2 * BAND, 2 * BAND), 1) & (BAND - 1)
        off = key - qry
        band_ok = jnp.logical_and(off >= 0, off <= BAND)
        lane = lax.broadcasted_iota(jnp.int32, (BAND, BAND), 1)
        head0 = lane < HEAD_DIM
        hm = [jnp.where(head0, 1.0, 0.0).astype(BF16), jnp.where(head0, 0.0, 1.0).astype(BF16)]
        piece = lane & (HEAD_DIM - 1)
        lane2 = lax.broadcasted_iota(jnp.int32, (2 * BAND, BAND), 1) & (HEAD_DIM - 1)
        ones = jnp.where(lane2 < 3, 1.0, 0.0).astype(BF16)

        def pieces(x):
            hi, mid, lo = _split3(-x)
            a = jnp.where(piece == 0, hi, jnp.where(piece == 1, mid, jnp.where(piece == 2, lo, 0.0)))
            return a.astype(BF16)

        for g, d in enumerate(DILATIONS):
            per = TOK // d
            nb = per // BAND
            pad = per + BAND
            _window_rows(kp_refs[g], k_refs[g], kw_s, d)
            _window_rows(vp_refs[g], v_refs[g], vw_s, d)
            _gather([do_ref], dos, d)
            _gather([l_ref], lsc, d)
            _gather([d_ref], dsc, d)
            dkb[...] = jnp.zeros_like(dkb)
            dvb[...] = jnp.zeros_like(dvb)
            q_ref = q_refs[g]

            def unit(u, c_):
                r, b = u // nb, u % nb
                rows = pl.ds(pl.multiple_of(u * BAND, BAND), BAND)
                qu = q_ref[0, r, pl.ds(pl.multiple_of(b * BAND, BAND), BAND), :]
                dou = dos[rows, :]
                la, da = pieces(lsc[rows, :]), pieces(dsc[rows, :])
                q2 = jnp.concatenate([qu * hm[0], qu * hm[1]], axis=0)
                do2 = jnp.concatenate([dou * hm[0], dou * hm[1]], axis=0)
                l2 = jnp.concatenate([la * hm[0], la * hm[1]], axis=0)
                d2 = jnp.concatenate([da * hm[0], da * hm[1]], axis=0)
                acc = pl.ds(pl.multiple_of(r * pad + b * BAND, BAND), 2 * BAND)
                kw = kw_s[acc, :]
                vw = vw_s[acc, :]
                lo = jnp.where(jnp.logical_and(i == 0, b == 0), BAND, 0)
                ok = jnp.logical_and(band_ok, key >= lo)
                st = _dot_nt(jnp.concatenate([kw, ones], axis=1), jnp.concatenate([q2, l2], axis=1))
                dpt = _dot_nt(jnp.concatenate([vw, ones], axis=1), jnp.concatenate([do2, d2], axis=1))
                pt = jnp.where(ok, jnp.exp(st), 0.0)
                dst = (pt * dpt).astype(BF16)
                dkb[acc, :] += _dot(dst, q2)
                dvb[acc, :] += _dot(pt.astype(BF16), do2)
                dq2 = _dot_tn(dst, kw)
                dqb[rows, :] = jnp.where(head0, dq2[:BAND], dq2[BAND:])
                return c_

            lax.fori_loop(0, UNITS, unit, 0, unroll=16)

            for r in range(d):
                last = pl.ds(r * pad + per, BAND)
                kept = pl.ds(offs[g] + r * BAND, BAND)

                @pl.when(step > 0)
                def _():
                    dkb[last, :] += ckb[kept, :]
                    dvb[last, :] += cvb[kept, :]

                ckb[kept, :] = dkb[pl.ds(r * pad, BAND), :]
                cvb[kept, :] = dvb[pl.ds(r * pad, BAND), :]
            _scatter_add(dq_ref, dqb, d, per, 0, g == 0)
            _scatter_add(dk_ref, dkb, d, pad, BAND, g == 0)
            _scatter_add(dv_ref, dvb, d, pad, BAND, g == 0)

    main, prev = _dilated_specs(nblk, True)
    tok = pl.BlockSpec((TOK, BAND), lambda j, s: (nblk - 1 - s, j))
    acc_rows = max(d * (TOK // d + BAND) for d in DILATIONS)
    kept_rows = sum(DILATIONS) * BAND
    return _call(
        body, list(qs) + list(ks) + list(ks) + list(vs) + list(vs) + [do, lse, dd], name="attn_bwd",
        grid=(PAIRS, nblk), out_shape=[_sds((t, D_ATTN), F32)] * 3,
        in_specs=main + prev + main + prev + main + [tok] * 3, out_specs=[tok] * 3,
        scratch_shapes=[pltpu.VMEM((acc_rows, BAND), BF16)] * 2 + [pltpu.VMEM((TOK, BAND), BF16)]
        + [pltpu.VMEM((TOK, BAND), F32)] * 3 + [pltpu.VMEM((acc_rows, BAND), F32)] * 2
        + [pltpu.VMEM((kept_rows, BAND), F32)] * 2,
        semantics=("parallel", "arbitrary"), carry=carry)


def _halo_rows(tm, t):
    per = tm // 8
    prev = lambda i: (jnp.maximum(i * per - 1, 0), 0)
    nxt = lambda i: (jnp.minimum((i + 1) * per, t // 8 - 1), 0)
    return prev, nxt


def _mixer_out(z, cw, y_attn, g_conv, g_attn, tm, carry=None):
    t = z.shape[0]
    prev, _ = _halo_rows(tm, t)

    def body(z_ref, zp_ref, cw_ref, y_ref, gc_ref, ga_ref, mix_ref):
        i = pl.program_id(0)
        keep = jnp.where(i > 0, 1.0, 0.0)
        cu = jnp.concatenate([zp_ref[:, 0:512] * zp_ref[:, 1024:1536] * keep,
                              z_ref[:, 0:512] * z_ref[:, 1024:1536]], axis=0)
        c = (cw_ref[0:1, :] * pltpu.roll(cu, 2, 0) + cw_ref[1:2, :] * pltpu.roll(cu, 1, 0)
             + cw_ref[2:3, :] * cu)[8:, :]
        yc = z_ref[:, 512:1024] * c
        mix_ref[:, 0:512] = (yc * _rms_scale(yc) * gc_ref[...]).astype(BF16)
        ya = y_ref[...]
        mix_ref[:, 512:1024] = (ya * _rms_scale(ya) * ga_ref[...]).astype(BF16)

    blk = pl.BlockSpec((tm, 512), lambda i: (i, 0))
    vec = pl.BlockSpec((1, 512), lambda i: (0, 0))
    return _call(
        body, [z, z, cw, y_attn, g_conv, g_attn], name="mixer_out", grid=(t // tm,),
        out_shape=_sds((t, 1024), BF16),
        in_specs=[pl.BlockSpec((tm, 1536), lambda i: (i, 0)), pl.BlockSpec((8, 1536), prev),
                  pl.BlockSpec((8, 512), lambda i: (0, 0)), blk, vec, vec],
        out_specs=pl.BlockSpec((tm, 1024), lambda i: (i, 0)),
        semantics=("parallel",), carry=carry)


def _mixer_bwd(z, dmix, y_attn, cw, g_conv, g_attn, ones_bd, tm, carry=None):
    t = z.shape[0]
    nblk = t // tm
    prev, nxt = _halo_rows(tm, t)
    e = tm + 16

    def body(z_ref, zp_ref, zn_ref, dm_ref, dmn_ref, y_ref, cw_ref, gc_ref, ga_ref, bd_ref,
             dz_ref, do_ref, dd_ref, dcw_ref, dgc_ref, dga_ref):
        i = pl.program_id(0)
        rows = lax.broadcasted_iota(jnp.int32, (e, 1), 0)
        lo = jnp.where(i > 0, 0, 8)
        hi = jnp.where(i < nblk - 1, e, tm + 8)
        ze = jnp.concatenate([zp_ref[...], z_ref[...], zn_ref[...]], axis=0)
        u, gb, gcv = ze[:, 0:512], ze[:, 512:1024], ze[:, 1024:1536]
        w0, w1, w2 = cw_ref[0:1, :], cw_ref[1:2, :], cw_ref[2:3, :]
        cu = jnp.where(rows >= lo, gcv * u, 0.0)
        cu1, cu2 = pltpu.roll(cu, 1, 0), pltpu.roll(cu, 2, 0)
        c = w0 * cu2 + w1 * cu1 + w2 * cu
        yc = gb * c
        dma = jnp.concatenate([jnp.zeros((8, 512), F32), dm_ref[:, 0:512], dmn_ref[...]], axis=0)
        dyc, ych = _rms_bwd(yc, _rms_scale(yc), gc_ref[...], dma)
        dc = jnp.where(jnp.logical_and(rows >= 8, rows < hi), dyc * gb, 0.0)
        dcu = w0 * pltpu.roll(dc, e - 2, 0) + w1 * pltpu.roll(dc, e - 1, 0) + w2 * dc
        mid = slice(8, 8 + tm)
        dz_ref[:, 0:512] = (dcu * gcv)[mid, :].astype(BF16)
        dz_ref[:, 512:1024] = (dyc * c)[mid, :].astype(BF16)
        dz_ref[:, 1024:1536] = (dcu * u)[mid, :].astype(BF16)

        ya = y_ref[...]
        dmb = dm_ref[:, 512:1024]
        dya, yah = _rms_bwd(ya, _rms_scale(ya), ga_ref[...], dmb)
        do_ref[...] = dya
        dd_ref[...] = _head_sum(dya * ya, bd_ref[...])

        @pl.when(i == 0)
        def _():
            dcw_ref[...] = jnp.zeros_like(dcw_ref)
            dgc_ref[...] = jnp.zeros_like(dgc_ref)
            dga_ref[...] = jnp.zeros_like(dga_ref)

        dcm = jnp.where(rows < tm + 8, dc, 0.0)
        dcw_ref[0:1, :] += jnp.sum(dcm * cu2, axis=0, keepdims=True)
        dcw_ref[1:2, :] += jnp.sum(dcm * cu1, axis=0, keepdims=True)
        dcw_ref[2:3, :] += jnp.sum(dcm * cu, axis=0, keepdims=True)
        dgc_ref[...] += jnp.sum((dma * ych)[mid, :], axis=0, keepdims=True)
        dga_ref[...] += jnp.sum(dmb * yah, axis=0, keepdims=True)

    blk = pl.BlockSpec((tm, 512), lambda i: (i, 0))
    vec = pl.BlockSpec((1, 512), lambda i: (0, 0))
    cwb = pl.BlockSpec((8, 512), lambda i: (0, 0))
    return _call(
        body, [z, z, z, dmix, dmix, y_attn, cw, g_conv, g_attn, ones_bd], name="mixer_bwd",
        grid=(nblk,),
        out_shape=[_sds((t, 1536), BF16), _sds((t, 512), F32), _sds((t, 512), F32),
                   _sds((8, 512), F32), _sds((1, 512), F32), _sds((1, 512), F32)],
        in_specs=[pl.BlockSpec((tm, 1536), lambda i: (i, 0)), pl.BlockSpec((8, 1536), prev),
                  pl.BlockSpec((8, 1536), nxt), pl.BlockSpec((tm, 1024), lambda i: (i, 0)),
                  pl.BlockSpec((8, 512), nxt), blk, cwb, vec, vec,
                  pl.BlockSpec((512, 512), lambda i: (0, 0))],
        out_specs=[pl.BlockSpec((tm, 1536), lambda i: (i, 0)), blk, blk, cwb, vec, vec],
        carry=carry)


def _qkv_bwd(z, dzc, dqn, dkn, dv, gq, gk, ones_bd, tm, carry=None):
    t = z.shape[0]

    def body(zq_ref, zk_ref, dzc_ref, dqn_ref, dkn_ref, dv_ref, gq_ref, gk_ref, bd_ref,
             dz_ref, dgq_ref, dgk_ref):
        bd = bd_ref[...]

        @pl.when(pl.program_id(0) == 0)
        def _():
            dgq_ref[...] = jnp.zeros_like(dgq_ref)
            dgk_ref[...] = jnp.zeros_like(dgk_ref)

        def back(v, dn, g, scale):
            r = _head_rms_scale(v, bd)
            vh = v * r
            dh = dn * (g * scale)
            dv = r * (dh - vh * (_head_sum(dh * vh, bd) * (1.0 / HEAD_DIM)))
            return dv, jnp.sum(dn * scale * vh, axis=0, keepdims=True)

        dq, dgq = back(zq_ref[...], dqn_ref[...], gq_ref[...], HEAD_DIM ** -0.5)
        dk, dgk = back(zk_ref[...], dkn_ref[...], gk_ref[...], 1.0)
        dgq_ref[...] += dgq
        dgk_ref[...] += dgk
        dz_ref[:, 0:1536] = dzc_ref[...]
        dz_ref[:, 1536:2048] = dq.astype(BF16)
        dz_ref[:, 2048:2560] = dk.astype(BF16)
        dz_ref[:, 2560:3072] = dv_ref[...].astype(BF16)

    blk = pl.BlockSpec((tm, 512), lambda i: (i, 0))
    vec = pl.BlockSpec((1, 512), lambda i: (0, 0))
    return _call(
        body, [z, z, dzc, dqn, dkn, dv, gq, gk, ones_bd], name="qkv_bwd", grid=(t // tm,),
        out_shape=[_sds((t, D_IN), BF16), _sds((1, 512), F32), _sds((1, 512), F32)],
        in_specs=[pl.BlockSpec((tm, 512), lambda i: (i, 3)), pl.BlockSpec((tm, 512), lambda i: (i, 4)),
                  pl.BlockSpec((tm, 1536), lambda i: (i, 0))] + [blk] * 3
        + [vec, vec, pl.BlockSpec((512, 512), lambda i: (0, 0))],
        out_specs=[pl.BlockSpec((tm, D_IN), lambda i: (i, 0)), vec, vec],
        carry=carry)


def _columns_from_chips(g):
    return g.transpose(1, 0, 2).reshape(g.shape[1], N_CHIPS * g.shape[2])


def kernel(x, g_mix, w_in, conv_w, g_q, g_k, g_conv_out, g_attn_out, w_out, g_ffn, w_gate, w_up, w_down, loss_target, m_g_mix, m_w_in, m_conv_w, m_g_q, m_g_k, m_g_conv_out, m_g_attn_out, m_w_out, m_g_ffn, m_w_gate, m_w_up, m_w_down, v_g_mix, v_w_in, v_conv_w, v_g_q, v_g_k, v_g_conv_out, v_g_attn_out, v_w_out, v_g_ffn, v_w_gate, v_w_up, v_w_down):
    t = x.shape[1]
    xs = x[0]
    target = loss_target[0]
    tm = min(512, t)
    tmm = min(1024, t)

    cw_pad = jnp.pad(conv_w[0], ((0, 13), (0, 0)))
    gathered = _all_gather([w_in[0].astype(BF16), cw_pad])
    win = _columns_from_chips(gathered[0])
    cw = jnp.pad(gathered[1][:, 0:3, :].transpose(1, 0, 2).reshape(3, D_CONV), ((0, 5), (0, 0)))
    later = [w_out[0].astype(BF16), w_gate[0].T.astype(BF16), w_up[0].T.astype(BF16),
             w_down[0].astype(BF16)]

    head_id = jnp.arange(D_ATTN) // HEAD_DIM
    ones_bd = (head_id[:, None] == head_id[None, :]).astype(BF16)
    gq_t = jnp.tile(g_q, (1, D_ATTN // HEAD_DIM))
    gk_t = jnp.tile(g_k, (1, D_ATTN // HEAD_DIM))

    h1, z = _norm_matmul("in_proj", xs, g_mix, [win], tm, D_IN, False)
    dilated = _qkv_prepare(z, gq_t, gk_t, ones_bd, tm)
    nd = len(DILATIONS)
    qs, ks, vs = dilated[:nd], dilated[nd:2 * nd], dilated[2 * nd:]
    (y_attn, lse), gathered = _attn_fwd(qs, ks, vs, carry=_x_gather_chips(later))
    mix, gathered = _mixer_out(z, cw, y_attn, g_conv_out, g_attn_out, tm,
                               carry=_x_gather_sibling(gathered))
    wout = gathered[0].reshape(D_MODEL, D_MODEL)
    wgate_t = gathered[1].reshape(D_FF, D_MODEL)
    wup_t = gathered[2].reshape(D_FF, D_MODEL)
    wdown = gathered[3].reshape(D_FF, D_MODEL)
    (x1,) = _matmul("out_proj", mix, wout, [xs], [F32], lambda acc, r: (r + acc,), tm, D_MODEL)
    h2, gate, up, act = _norm_matmul("ffn_up", x1, g_ffn, [wgate_t, wup_t], tm, D_FF, True, BF16,
                                     transposed_w=True)

    def loss_epilogue(acc, r, tgt):
        err = r + acc - tgt
        dy = err * (1.0 / D_MODEL)
        return dy, dy, jnp.sum(err * err)

    dx2, dx2b, loss_sum = _matmul("ffn_down_loss", act, wdown, [x1, target], [F32, BF16],
                                  loss_epilogue, tm, D_MODEL, loss=True)

    def swiglu_bwd(da, gt, u):
        gt, u = gt.astype(F32), u.astype(F32)
        s = _sigmoid(gt)
        return da * u * (s * (1.0 + gt * (1.0 - s))), da * (gt * s)

    dgate, dup = _matmul("ffn_down_bwd", dx2b, wdown, [gate, up], [BF16, BF16], swiglu_bwd,
                         tm, D_FF, transposed_w=True)
    gw_down = _matmul_tn("grad_w_down", act, dx2b, 512, tmm)
    gw_gate_t = _matmul_tn("grad_w_gate", dgate, h2, 512, tmm)
    gw_up_t = _matmul_tn("grad_w_up", dup, h2, 512, tmm)

    me = 2 * lax.axis_index("x") + lax.axis_index("y")
    where = jnp.stack([lax.axis_index("c"), me]).astype(jnp.int32)

    def pair_sums(names, full, got):
        return [_pair_sum(f"pair_sum_{nme}", a, b, where) for nme, a, b in zip(names, full, got)]

    def chip_sums(names, pair, got):
        return [_chip_sum(f"chip_sum_{nme}", own, b) for nme, (_, own), b in zip(names, pair, got)]

    ffn = ["w_gate", "w_up", "w_down"]
    full = [g.reshape(N_CHIPS, D_FF // N_CHIPS, D_MODEL) for g in (gw_gate_t, gw_up_t, gw_down)]
    (dx1, dx1b, gg_ffn), got = _matmul_norm_bwd(
        "ffn_up_bwd", [(dgate, wgate_t), (dup, wup_t)], x1, dx2, g_ffn, tm, carry=_x_pair(full),
        transposed_w=False)
    pair = pair_sums(ffn, full, got)
    (dmix,) = _matmul("out_proj_bwd", dx1b, wout, [], [F32], lambda acc: (acc,), tm, D_MODEL,
                      transposed_w=True)
    gw_out = _matmul_tn("grad_w_out", mix, dx1b, 512, tmm)
    full = [gw_out.reshape(N_CHIPS, D_MODEL // N_CHIPS, D_MODEL)]
    (dzc, do, dd, gcw, gg_conv, gg_attn), got = _mixer_bwd(
        z, dmix, y_attn, cw, g_conv_out, g_attn_out, ones_bd, tm, carry=_x_pair(full))
    pair += pair_sums(["w_out"], full, got)
    early = ffn + ["w_out"]
    (dqn, dkn, dv), got = _attn_bwd(qs, ks, vs, do, lse, dd, carry=_x_chips([p for p, _ in pair]))
    mine = chip_sums(early, pair, got)
    (dz, gg_q, gg_k), theirs = _qkv_bwd(z, dzc, dqn, dkn, dv, gq_t, gk_t, ones_bd, tm,
                                        carry=_x_share(mine))
    full = [_matmul_tn("grad_w_in", h1, dz, D_IN // N_CHIPS, tmm, by_chip=True)]
    grad_x, _, gg_mix = _matmul_norm_bwd("in_proj_bwd", [(dz, win)], xs, dx1, g_mix, tm)
    got = _exchange_alone("grad_pair_exchange_w_in", _x_pair(full))
    pair = pair_sums(["w_in"], full, got)
    got = _exchange_alone("grad_chip_exchange_w_in", _x_chips([pair[0][0]]))
    mine += chip_sums(["w_in"], pair, got)
    theirs = list(theirs) + list(_exchange_alone("grad_pair_share_w_in", _x_share(mine[-1:])))
    big = early + ["w_in"]

    small = _small_all_reduce({
        "g_mix": gg_mix, "g_ffn": gg_ffn, "g_conv_out": gg_conv, "g_attn_out": gg_attn,
        "g_q": gg_q, "g_k": gg_k, "loss": loss_sum, "conv_w": gcw})
    heads = D_ATTN // HEAD_DIM
    grads = {
        "g_mix": small[0:1, :], "g_ffn": small[1:2, :],
        "g_conv_out": small[2:3, 0:512], "g_attn_out": small[2:3, 512:1024],
        "g_q": small[3, 0:512].reshape(heads, HEAD_DIM).sum(axis=0)[None, :],
        "g_k": small[3, 512:1024].reshape(heads, HEAD_DIM).sum(axis=0)[None, :],
        "conv_w": lax.dynamic_slice(small[8:11, 0:512], (0, me * (D_CONV // N_CHIPS)),
                                    (3, D_CONV // N_CHIPS)),
    }
    halves = dict(zip(big, zip(mine, theirs)))
    loss = small[4, 0] * 0.5 * (1.0 / D_MODEL)

    weights = dict(g_mix=g_mix, w_in=w_in, conv_w=conv_w, g_q=g_q, g_k=g_k, g_conv_out=g_conv_out,
                   g_attn_out=g_attn_out, w_out=w_out, g_ffn=g_ffn, w_gate=w_gate, w_up=w_up,
                   w_down=w_down)
    moments_m = dict(g_mix=m_g_mix, w_in=m_w_in, conv_w=m_conv_w, g_q=m_g_q, g_k=m_g_k,
                     g_conv_out=m_g_conv_out, g_attn_out=m_g_attn_out, w_out=m_w_out, g_ffn=m_g_ffn,
                     w_gate=m_w_gate, w_up=m_w_up, w_down=m_w_down)
    moments_v = dict(g_mix=v_g_mix, w_in=v_w_in, conv_w=v_conv_w, g_q=v_g_q, g_k=v_g_k,
                     g_conv_out=v_g_conv_out, g_attn_out=v_g_attn_out, w_out=v_w_out, g_ffn=v_g_ffn,
                     w_gate=v_w_gate, w_up=v_w_up, w_down=v_w_down)
    names = list(weights)
    out_g, out_d, out_m, out_v = [], [], [], []
    for nme in names:
        wgt = weights[nme]
        shape2 = wgt.shape[-2:] if wgt.ndim == 3 else wgt.shape
        flip = nme in ("w_gate", "w_up")

        def to2d(a):
            return a.reshape(shape2).T if flip else a.reshape(shape2)

        def back(a):
            return (a.T if flip else a).reshape(wgt.shape)

        state = (to2d(wgt), to2d(moments_m[nme]), to2d(moments_v[nme]))
        if nme in halves:
            g2, dlt, nm, nv = _adamw_shard(f"adamw_{nme}", *state, *halves[nme], where)
        else:
            g2 = grads[nme].reshape(shape2)
            dlt, nm, nv = _adamw(f"adamw_{nme}", state[0], g2, state[1], state[2])
        out_g.append(back(g2))
        out_d.append(back(dlt))
        out_m.append(back(nm))
        out_v.append(back(nv))
    return (loss, grad_x[None], *out_g, *out_d, *out_m, *out_v)
```

```python
import functools
from typing import Any, Callable, NamedTuple, Sequence

import jax
import jax.numpy as jnp
from jax import lax
from jax.experimental import pallas as pl
from jax.experimental.pallas import tpu as pltpu

F32 = jnp.float32
BF16 = jnp.bfloat16
MESH = pl.DeviceIdType.MESH

D_MODEL = 1024
D_CONV = 512
D_ATTN = 512
HEAD_DIM = 64
D_FF = 2816
D_IN = 3 * D_CONV + 3 * D_ATTN
DILATIONS = (1, 4, 16)
BAND = 128
EPS = 1e-6
NEG = -1e30
N_CHIPS = 4

ADAM_LR = 0.001
ADAM_B1 = 0.9
ADAM_B2 = 0.999
ADAM_EPS = 1e-08
ADAM_WD = 0.01
ADAM_STEP = 10

V7X_VMEM_BYTES = 64 * 1024 * 1024
VMEM_LIMIT = V7X_VMEM_BYTES - 8 * 1024 * 1024
ANY = pl.BlockSpec(memory_space=pl.ANY)
VMEM_WHOLE = pl.BlockSpec(memory_space=pltpu.VMEM)


def _params(*sem):
    return pltpu.CompilerParams(dimension_semantics=sem, vmem_limit_bytes=VMEM_LIMIT)


def _sds(shape, dtype):
    return jax.ShapeDtypeStruct(shape, dtype)


def _resident(whole):
    return pl.Buffered(1) if whole else None


def _place():
    x, y, c = lax.axis_index("x"), lax.axis_index("y"), lax.axis_index("c")
    chips = [(1 - x, y), (x, 1 - y), (1 - x, 1 - y)]
    return x, y, c, 2 * x + y, chips, [2 * cx + cy for cx, cy in chips]


def _all_gather(shards):
    n = len(shards)

    def body(*refs):
        ins, outs, stage = refs[:n], refs[n:2 * n], refs[2 * n:3 * n]
        ssem, rsem, fsem, gsem, lsem, osem = refs[3 * n:]
        x, y, c, me, chips, cids = _place()
        sib = (x, y, 1 - c)

        def half(w, which):
            h = shards[w].shape[0] // 2
            return pl.ds(pl.multiple_of(which * h, 8), h)

        loads = [pltpu.make_async_copy(ins[w], stage[w], lsem.at[w]) for w in range(n)]
        local = [pltpu.make_async_copy(stage[w], outs[w].at[me], osem.at[w]) for w in range(n)]
        for cp in loads:
            cp.start()

        def chip_copy(w, j, src_slot):
            rows = half(w, c)
            return pltpu.make_async_remote_copy(
                src_ref=ins[w].at[rows], dst_ref=outs[w].at[src_slot, rows],
                send_sem=ssem.at[3 * w + j], recv_sem=rsem.at[3 * w + j],
                device_id=(*chips[j], c), device_id_type=MESH)

        def sib_copy(w, j, which):
            rows = half(w, which)
            return pltpu.make_async_remote_copy(
                src_ref=outs[w].at[cids[j], rows], dst_ref=outs[w].at[cids[j], rows],
                send_sem=fsem.at[3 * w + j], recv_sem=gsem.at[3 * w + j],
                device_id=sib, device_id_type=MESH)

        sends = [chip_copy(w, j, me) for w in range(n) for j in range(3)]
        for cp in sends:
            cp.start()
        for w in range(n):
            loads[w].wait()
            local[w].start()
        passed = []
        for w in range(n):
            for j in range(3):
                chip_copy(w, j, cids[j]).wait_recv()
                cp = sib_copy(w, j, c)
                cp.start()
                passed.append(cp)
        for w in range(n):
            for j in range(3):
                sib_copy(w, j, 1 - c).wait_recv()
        for cp in sends + passed:
            cp.wait_send()
        for cp in local:
            cp.wait()

    return pl.pallas_call(
        body, name="all_gather_weights",
        out_shape=[_sds((N_CHIPS,) + s.shape, s.dtype) for s in shards],
        in_specs=[ANY] * n, out_specs=[ANY] * n,
        scratch_shapes=[pltpu.VMEM(s.shape, s.dtype) for s in shards]
        + [pltpu.SemaphoreType.DMA((3 * n,))] * 4 + [pltpu.SemaphoreType.DMA((n,))] * 2,
        compiler_params=pltpu.CompilerParams(vmem_limit_bytes=VMEM_LIMIT),
    )(*shards)


class _Exchange(NamedTuple):
    srcs: Sequence[Any]
    lands: Sequence[Any]
    outs: Sequence[Any]
    n_sems: int
    copies: Callable


def _remote(src, dst, ssem, rsem, k, to):
    return pltpu.make_async_remote_copy(src_ref=src, dst_ref=dst, send_sem=ssem.at[k],
                                        recv_sem=rsem.at[k], device_id=to, device_id_type=MESH)


def _x_gather_chips(shards):
    def copies(srcs, lands, outs, ssem, rsem):
        _, _, c, me, chips, cids = _place()
        go, arrive = [], []
        for w, s in enumerate(shards):
            h = s.shape[0] // 2
            rows = pl.ds(pl.multiple_of(c * h, 8), h)
            for j in range(3):
                to = (*chips[j], c)
                go.append(_remote(srcs[w].at[rows], lands[w].at[me, rows], ssem, rsem, 3 * w + j, to))
                arrive.append(_remote(srcs[w].at[rows], lands[w].at[cids[j], rows], ssem, rsem,
                                      3 * w + j, to))
        return go, arrive

    lands = [jnp.broadcast_to(s[None], (N_CHIPS,) + s.shape) for s in shards]
    return _Exchange(shards, lands, [], 3 * len(shards), copies)


def _x_gather_sibling(gathered):
    def copies(srcs, lands, outs, ssem, rsem):
        x, y, c, _, _, cids = _place()
        go, arrive = [], []
        for w, g in enumerate(gathered):
            h = g.shape[1] // 2
            mine = pl.ds(pl.multiple_of(c * h, 8), h)
            theirs = pl.ds(pl.multiple_of((1 - c) * h, 8), h)
            for j in range(3):
                slab = lands[w].at[cids[j]]
                go.append(_remote(slab.at[mine], slab.at[mine], ssem, rsem, 3 * w + j, (x, y, 1 - c)))
                arrive.append(_remote(slab.at[theirs], slab.at[theirs], ssem, rsem, 3 * w + j,
                                      (x, y, 1 - c)))
        return go, arrive

    return _Exchange([], gathered, [], 3 * len(gathered), copies)


def _x_pair(grads):
    def copies(srcs, lands, outs, ssem, rsem):
        x, y, c, _, _, _ = _place()
        go = []
        for w, g in enumerate(grads):
            h = g.shape[1] // 2
            theirs = pl.ds(pl.multiple_of((1 - c) * h, 8), h)
            go.append(_remote(srcs[w].at[:, theirs, :], outs[w], ssem, rsem, w, (x, y, 1 - c)))
        return go, go

    outs = [_sds((N_CHIPS, g.shape[1] // 2, g.shape[2]), g.dtype) for g in grads]
    return _Exchange(grads, [], outs, len(grads), copies)


def _x_chips(parts):
    def copies(srcs, lands, outs, ssem, rsem):
        _, _, c, _, chips, cids = _place()
        go = [_remote(srcs[w].at[cids[j]], outs[w].at[j], ssem, rsem, 3 * w + j, (*chips[j], c))
              for w in range(len(parts)) for j in range(3)]
        return go, go

    outs = [_sds((3,) + p.shape[1:], p.dtype) for p in parts]
    return _Exchange(parts, [], outs, 3 * len(parts), copies)


def _x_share(halves):
    def copies(srcs, lands, outs, ssem, rsem):
        x, y, c, _, _, _ = _place()
        go = [_remote(srcs[w], outs[w], ssem, rsem, w, (x, y, 1 - c)) for w in range(len(halves))]
        return go, go

    return _Exchange(halves, [], [_sds(h.shape, h.dtype) for h in halves], len(halves), copies)


def _call(body, args, *, name, grid, in_specs, out_specs, out_shape, scratch_shapes=(),
          semantics=None, carry=None):
    single = not isinstance(out_shape, (list, tuple))
    out_shape = [out_shape] if single else list(out_shape)
    out_specs = [out_specs] if single else list(out_specs)
    if carry is None:
        res = pl.pallas_call(
            body, name=name, grid=grid, in_specs=list(in_specs), out_specs=out_specs,
            out_shape=out_shape, scratch_shapes=list(scratch_shapes),
            compiler_params=_params(*(semantics or ("arbitrary",) * len(grid))))(*args)
        return res[0] if single else res
    n_in, n_out, n_scr = len(args), len(out_shape), len(scratch_shapes)
    n_src, n_land, n_new = len(carry.srcs), len(carry.lands), len(carry.outs)

    def carrying(*refs):
        at = 0
        parts = []
        for n in (n_in, n_src, n_land, n_out, n_land, n_new, n_scr, 2):
            parts.append(refs[at:at + n])
            at += n
        ins, srcs, _, outs, lands, news, scratch, (ssem, rsem) = parts
        ids = [pl.program_id(a) for a in range(len(grid))]
        first = functools.reduce(jnp.logical_and, [i == 0 for i in ids])
        last = functools.reduce(jnp.logical_and, [i == g - 1 for i, g in zip(ids, grid)])
        go, arrive = carry.copies(srcs, lands, news, ssem, rsem)

        @pl.when(first)
        def _():
            for cp in go:
                cp.start()

        body(*ins, *outs, *scratch)

        @pl.when(last)
        def _():
            for cp in go:
                cp.wait_send()
            for cp in arrive:
                cp.wait_recv()

    res = pl.pallas_call(
        carrying, name=name, grid=grid,
        in_specs=list(in_specs) + [ANY] * (n_src + n_land),
        out_specs=out_specs + [ANY] * (n_land + n_new),
        out_shape=out_shape + [_sds(a.shape, a.dtype) for a in carry.lands] + list(carry.outs),
        input_output_aliases={n_in + n_src + i: n_out + i for i in range(n_land)},
        scratch_shapes=list(scratch_shapes) + [pltpu.SemaphoreType.DMA((carry.n_sems,))] * 2,
        compiler_params=_params(*(("arbitrary",) * len(grid))))(*args, *carry.srcs, *carry.lands)
    own = res[:n_out]
    return (own[0] if single else own), res[n_out:]


def _exchange_alone(name, exchange):
    def body(x_ref, o_ref):
        o_ref[...] = x_ref[...]

    blk = pl.BlockSpec((8, 128), lambda i: (0, 0))
    _, res = _call(body, [jnp.zeros((8, 128), F32)], name=name, grid=(1,), in_specs=[blk],
                   out_specs=blk, out_shape=_sds((8, 128), F32), carry=exchange)
    return res


def _row_block(r, want):
    return max(d for d in range(1, min(want, r) + 1) if r % d == 0 and (d % 8 == 0 or d == r))


def _pair_sum(name, full, got, where):
    _, r, n = full.shape
    h = r // 2
    tr = _row_block(h, 256)
    nb = h // tr

    def body(w_ref, a_ref, b_ref, o_ref, own_ref):
        total = a_ref[...] + b_ref[...]
        o_ref[...] = total.astype(BF16)

        @pl.when(pl.program_id(1) == w_ref[1])
        def _():
            own_ref[...] = total[0]

    blk = pl.BlockSpec((1, tr, n), lambda i, s, w: (s, i, 0))
    return pl.pallas_call(
        body, name=name, out_shape=[_sds(got.shape, BF16), _sds((h, n), F32)],
        grid_spec=pltpu.PrefetchScalarGridSpec(
            num_scalar_prefetch=1, grid=(nb, N_CHIPS),
            in_specs=[pl.BlockSpec((1, tr, n), lambda i, s, w: (s, w[0] * nb + i, 0)), blk],
            out_specs=[blk, pl.BlockSpec((tr, n), lambda i, s, w: (i, 0))]),
        compiler_params=_params("parallel", "arbitrary"),
    )(where, full, got)


def _chip_sum(name, own, got):
    h, n = own.shape
    tr = _row_block(h, 256)

    def body(a_ref, b0, b1, b2, o_ref):
        o_ref[...] = ((a_ref[...] + b0[0].astype(F32)) + b1[0].astype(F32)) + b2[0].astype(F32)

    def slot(j):
        return pl.BlockSpec((1, tr, n), lambda i: (j, i, 0))

    blk = pl.BlockSpec((tr, n), lambda i: (i, 0))
    return pl.pallas_call(
        body, name=name, grid=(h // tr,), out_shape=_sds((h, n), F32),
        in_specs=[blk, slot(0), slot(1), slot(2)], out_specs=blk,
        compiler_params=_params("parallel"),
    )(own, got, got, got)


SMALL_ROWS = 16
SMALL_LAYOUT = (
    ("g_mix", 0, 0, 1, 1024), ("g_ffn", 1, 0, 1, 1024), ("g_conv_out", 2, 0, 1, 512),
    ("g_attn_out", 2, 512, 1, 512), ("g_q", 3, 0, 1, 512), ("g_k", 3, 512, 1, 512),
    ("loss", 4, 0, 1, 128), ("conv_w", 8, 0, 8, 512))


def _small_all_reduce(parts):
    names = [s[0] for s in SMALL_LAYOUT]

    def body(*refs):
        ins = refs[:len(names)]
        out_ref, stage, buf, ssem, rsem = refs[len(names):]
        x, y, c, _, _, _ = _place()
        me = 4 * x + 2 * y + c
        stage[...] = jnp.zeros_like(stage)
        for ref, (_, r0, c0, nr, nc) in zip(ins, SMALL_LAYOUT):
            stage[r0:r0 + nr, c0:c0 + nc] = ref[0:nr, :]
        buf[me] = stage[...]
        peers = []
        for d in range(1, 8):
            px = 1 - x if d & 4 else x
            py = 1 - y if d & 2 else y
            pc = 1 - c if d & 1 else c
            peers.append(((px, py, pc), 4 * px + 2 * py + pc))
        sends = [pltpu.make_async_remote_copy(
            src_ref=stage, dst_ref=buf.at[me], send_sem=ssem.at[k], recv_sem=rsem.at[k],
            device_id=peer, device_id_type=MESH) for k, (peer, _) in enumerate(peers)]
        for cp in sends:
            cp.start()
        for k, (peer, pid) in enumerate(peers):
            pltpu.make_async_remote_copy(
                src_ref=stage, dst_ref=buf.at[pid], send_sem=ssem.at[k], recv_sem=rsem.at[k],
                device_id=peer, device_id_type=MESH).wait_recv()
        for cp in sends:
            cp.wait_send()
        acc = buf[0]
        for k in range(1, 8):
            acc = acc + buf[k]
        out_ref[...] = acc

    return pl.pallas_call(
        body, name="small_all_reduce", out_shape=_sds((SMALL_ROWS, 1024), F32),
        in_specs=[VMEM_WHOLE] * len(names), out_specs=VMEM_WHOLE,
        scratch_shapes=[pltpu.VMEM((SMALL_ROWS, 1024), F32), pltpu.VMEM((8, SMALL_ROWS, 1024), F32),
                        pltpu.SemaphoreType.DMA((7,)), pltpu.SemaphoreType.DMA((7,))],
    )(*[parts[k] for k in names])


def _dot(a, b):
    return jnp.dot(a, b, preferred_element_type=F32)


def _dot_nt(a, b):
    return lax.dot_general(a, b, (((1,), (1,)), ((), ())), preferred_element_type=F32)


def _dot_tn(a, b):
    return lax.dot_general(a, b, (((0,), (0,)), ((), ())), preferred_element_type=F32)


def _sigmoid(v):
    return 1.0 / (1.0 + jnp.exp(-v))


def _rms_scale(v):
    return lax.rsqrt(jnp.mean(v * v, axis=-1, keepdims=True) + EPS)


def _rms_bwd(v, r, g, dy):
    vh = v * r
    dh = dy * g
    return r * (dh - vh * jnp.mean(dh * vh, axis=-1, keepdims=True)), vh


def _head_sum(a, ones_bd):
    hi = a.astype(BF16)
    lo = (a - hi.astype(F32)).astype(BF16)
    return _dot(hi, ones_bd) + _dot(lo, ones_bd)


def _head_rms_scale(v, ones_bd):
    return lax.rsqrt(_head_sum(v * v, ones_bd) * (1.0 / HEAD_DIM) + EPS)


MXU_COLUMNS = 256


def _column_chunks(n):
    width = MXU_COLUMNS if n % MXU_COLUMNS == 0 else n
    return [slice(c, c + width) for c in range(0, n, width)]


def _norm_matmul(name, x, g, ws, tm, tn, swiglu, out_dtype=F32, transposed_w=False):
    t, d = x.shape
    n = ws[0].shape[0] if transposed_w else ws[0].shape[1]
    nw = len(ws)

    def body(x_ref, g_ref, *refs):
        w_refs, h_ref, o_refs = refs[:nw], refs[nw], refs[nw + 1:2 * nw + 1]
        hs = refs[-1]

        @pl.when(pl.program_id(1) == 0)
        def _():
            xv = x_ref[...]
            h = (xv * _rms_scale(xv) * g_ref[...]).astype(BF16)
            hs[...] = h
            h_ref[...] = h

        h = hs[...]
        for cols in _column_chunks(tn):
            outs = [_dot_nt(h, w[cols, :]) if transposed_w else _dot(h, w[:, cols]) for w in w_refs]
            for o_ref, o in zip(o_refs, outs):
                o_ref[:, cols] = o.astype(out_dtype)
            if swiglu:
                refs[2 * nw + 1][:, cols] = (outs[0] * _sigmoid(outs[0]) * outs[1]).astype(BF16)

    row = pl.BlockSpec((tm, d), lambda i, j: (i, 0))
    col = pl.BlockSpec((tm, tn), lambda i, j: (i, j))
    out_shape = [_sds((t, d), BF16)] + [_sds((t, n), out_dtype)] * nw
    out_specs = [row] + [col] * nw
    if swiglu:
        out_shape.append(_sds((t, n), BF16))
        out_specs.append(col)
    return pl.pallas_call(
        body, name=name, grid=(t // tm, n // tn), out_shape=out_shape,
        in_specs=[row, pl.BlockSpec((1, d), lambda i, j: (0, 0))]
        + [pl.BlockSpec((tn, d), lambda i, j: (j, 0), pipeline_mode=_resident(tn == n))
           if transposed_w
           else pl.BlockSpec((d, tn), lambda i, j: (0, j), pipeline_mode=_resident(tn == n))] * nw,
        out_specs=out_specs, scratch_shapes=[pltpu.VMEM((tm, d), BF16)],
        compiler_params=_params("parallel", "arbitrary"),
    )(x, g, *ws)


def _matmul(name, a, w, extras, out_dtypes, epilogue, tm, tn, transposed_w=False, loss=False):
    t, k = a.shape
    n = w.shape[0] if transposed_w else w.shape[1]
    ne, no = len(extras), len(out_dtypes)

    def body(a_ref, w_ref, *refs):
        e_refs, o_refs = refs[:ne], refs[ne:]
        a = a_ref[...]
        total = 0.0
        for cols in _column_chunks(tn):
            acc = _dot_nt(a, w_ref[cols, :]) if transposed_w else _dot(a, w_ref[:, cols])
            res = epilogue(acc, *[e[:, cols] for e in e_refs])
            for o_ref, r in zip(o_refs[:no], res[:no]):
                o_ref[:, cols] = r.astype(o_ref.dtype)
            if loss:
                total = total + res[no]
        if loss:
            first = jnp.logical_and(pl.program_id(0) == 0, pl.program_id(1) == 0)

            @pl.when(first)
            def _():
                o_refs[no][...] = jnp.zeros_like(o_refs[no])

            o_refs[no][...] += total

    col = pl.BlockSpec((tm, tn), lambda i, j: (i, j))
    w_spec = (pl.BlockSpec((tn, k), lambda i, j: (j, 0), pipeline_mode=_resident(tn == n))
              if transposed_w
              else pl.BlockSpec((k, tn), lambda i, j: (0, j), pipeline_mode=_resident(tn == n)))
    out_shape = [_sds((t, n), dt) for dt in out_dtypes]
    out_specs = [col] * no
    if loss:
        out_shape.append(_sds((8, 128), F32))
        out_specs.append(pl.BlockSpec((8, 128), lambda i, j: (0, 0)))
    return pl.pallas_call(
        body, name=name, grid=(t // tm, n // tn), out_shape=out_shape,
        in_specs=[pl.BlockSpec((tm, k), lambda i, j: (i, 0)), w_spec] + [col] * ne,
        out_specs=out_specs,
        compiler_params=_params(*(("arbitrary", "arbitrary") if loss else ("parallel", "parallel"))),
    )(a, w, *extras)


def _matmul_norm_bwd(name, pairs, x, dres, g, tm, carry=None, transposed_w=True):
    t, d = x.shape
    npairs = len(pairs)
    product = _dot_nt if transposed_w else _dot

    def body(*refs):
        a_refs, w_refs = refs[:npairs], refs[npairs:2 * npairs]
        x_ref, r_ref, g_ref, dx_ref, dxb_ref, dg_ref = refs[2 * npairs:]
        dy = product(a_refs[0][...], w_refs[0][...])
        for a_ref, w_ref in zip(a_refs[1:], w_refs[1:]):
            dy = dy + product(a_ref[...], w_ref[...])
        xv = x_ref[...]
        dx, xh = _rms_bwd(xv, _rms_scale(xv), g_ref[...], dy)
        dx = dx + r_ref[...]
        dx_ref[...] = dx
        dxb_ref[...] = dx.astype(BF16)

        @pl.when(pl.program_id(0) == 0)
        def _():
            dg_ref[...] = jnp.zeros_like(dg_ref)

        dg_ref[...] += jnp.sum(dy * xh, axis=0, keepdims=True)

    row = pl.BlockSpec((tm, d), lambda i: (i, 0))
    vec = pl.BlockSpec((1, d), lambda i: (0, 0))
    return _call(
        body, [a for a, _ in pairs] + [w for _, w in pairs] + [x, dres, g], name=name,
        grid=(t // tm,), out_shape=[_sds((t, d), F32), _sds((t, d), BF16), _sds((1, d), F32)],
        in_specs=[pl.BlockSpec((tm, a.shape[1]), lambda i: (i, 0)) for a, _ in pairs]
        + [pl.BlockSpec(w.shape, lambda i: (0, 0), pipeline_mode=pl.Buffered(1)) for _, w in pairs]
        + [row, row, vec],
        out_specs=[row, row, vec], carry=carry)


def _matmul_tn(name, a, g, tn, tk, by_chip=False):
    t, ka = a.shape
    n = g.shape[1]

    def body(a_ref, g_ref, o_ref):
        @pl.when(pl.program_id(1) == 0)
        def _():
            o_ref[...] = jnp.zeros_like(o_ref)

        acc = _dot_tn(a_ref[...], g_ref[...])
        o_ref[...] += acc[None] if by_chip else acc

    return pl.pallas_call(
        body, name=name, grid=(n // tn, t // tk),
        out_shape=_sds((n // tn, ka, tn) if by_chip else (ka, n), F32),
        in_specs=[pl.BlockSpec((tk, ka), lambda j, s: (s, 0)),
                  pl.BlockSpec((tk, tn), lambda j, s: (s, j))],
        out_specs=(pl.BlockSpec((1, ka, tn), lambda j, s: (j, 0, 0)) if by_chip
                   else pl.BlockSpec((ka, tn), lambda j, s: (0, j))),
        compiler_params=_params("parallel", "arbitrary"),
    )(a, g)


def _elementwise(name, fn, ins, out_dtypes, tr):
    r, n = ins[0].shape
    tr = _row_block(r, tr)
    ni = len(ins)

    def body(*refs):
        res = fn(*[ref[...] for ref in refs[:ni]])
        for o_ref, v in zip(refs[ni:], res):
            o_ref[...] = v.astype(o_ref.dtype)

    blk = pl.BlockSpec((tr, n), lambda i: (i, 0))
    return pl.pallas_call(
        body, name=name, grid=(r // tr,), out_shape=[_sds((r, n), dt) for dt in out_dtypes],
        in_specs=[blk] * ni, out_specs=[blk] * len(out_dtypes),
        compiler_params=_params("parallel"),
    )(*ins)


def _adamw_update(w, g, m, v):
    m = ADAM_B1 * m + (1.0 - ADAM_B1) * g
    v = ADAM_B2 * v + (1.0 - ADAM_B2) * (g * g)
    m_hat = m / (1.0 - ADAM_B1 ** ADAM_STEP)
    v_hat = v / (1.0 - ADAM_B2 ** ADAM_STEP)
    return -ADAM_LR * (m_hat / (jnp.sqrt(v_hat) + ADAM_EPS) + ADAM_WD * w), m, v


def _adamw(name, w, g, m, v):
    return _elementwise(name, _adamw_update, [w, g, m, v], [F32] * 3, 256)


def _adamw_shard(name, w, m, v, mine, theirs, where):
    r, n = w.shape
    h = r // 2
    tr = _row_block(h, 256)
    nb = h // tr

    def body(w_ref, p_ref, m_ref, v_ref, a_ref, b_ref, g_ref, d_ref, nm_ref, nv_ref):
        g = jnp.where(pl.program_id(0) == w_ref[0], a_ref[...], b_ref[...])
        g_ref[...] = g
        d_ref[...], nm_ref[...], nv_ref[...] = _adamw_update(p_ref[...], g, m_ref[...], v_ref[...])

    whole = pl.BlockSpec((tr, n), lambda s, i, c: (s * nb + i, 0))
    half = pl.BlockSpec((tr, n), lambda s, i, c: (i, 0))
    return pl.pallas_call(
        body, name=name, out_shape=[_sds((r, n), F32)] * 4,
        grid_spec=pltpu.PrefetchScalarGridSpec(
            num_scalar_prefetch=1, grid=(2, nb), in_specs=[whole] * 3 + [half] * 2,
            out_specs=[whole] * 4),
        compiler_params=_params("parallel", "parallel"),
    )(where, w, m, v, mine, theirs)


PAIRS = D_ATTN // BAND


def _in_proj(x, g, w, gq, gk, ones_bd, tm):
    t, dm = x.shape
    n = w.shape[1]
    nd = len(DILATIONS)
    first = 3 * D_CONV

    def body(x_ref, g_ref, w_ref, gq_ref, gk_ref, bd_ref, h_ref, z_ref, *refs):
        outs, slab = refs[:3 * nd], refs[3 * nd]
        xv = x_ref[...]
        h = (xv * _rms_scale(xv) * g_ref[...]).astype(BF16)
        h_ref[...] = h
        for cols in _column_chunks(n):
            z_ref[:, cols] = _dot(h, w_ref[:, cols])
        bd = bd_ref[...]
        q = z_ref[:, first:first + D_ATTN]
        k = z_ref[:, first + D_ATTN:first + 2 * D_ATTN]
        vals = [(q * _head_rms_scale(q, bd) * gq_ref[...]) * HEAD_DIM ** -0.5,
                k * _head_rms_scale(k, bd) * gk_ref[...], z_ref[:, first + 2 * D_ATTN:n]]
        for m, val in enumerate(vals):
            for c in range(PAIRS):
                slab[c] = val[:, c * BAND:(c + 1) * BAND]
            for a, d in enumerate(DILATIONS):
                o_ref = outs[m * nd + a]
                for c in range(PAIRS):
                    for r in range(d):
                        rows = slab.at[c][pl.ds(r, tm // d, stride=d), :] if d > 1 else slab[c]
                        o_ref[c, r] = rows.astype(BF16)

    row = pl.BlockSpec((tm, dm), lambda i: (i, 0))
    vec = pl.BlockSpec((1, D_ATTN), lambda i: (0, 0))
    return pl.pallas_call(
        body, name="in_proj", grid=(t // tm,),
        out_shape=[_sds((t, dm), BF16), _sds((t, n), F32)]
        + [_sds((PAIRS, d, t // d, BAND), BF16) for _ in range(3) for d in DILATIONS],
        in_specs=[row, pl.BlockSpec((1, dm), lambda i: (0, 0)),
                  pl.BlockSpec((dm, n), lambda i: (0, 0), pipeline_mode=_resident(True)), vec, vec,
                  pl.BlockSpec((D_ATTN, D_ATTN), lambda i: (0, 0), pipeline_mode=_resident(True))],
        out_specs=[row, pl.BlockSpec((tm, n), lambda i: (i, 0))]
        + [pl.BlockSpec((PAIRS, d, tm // d, BAND), lambda i: (0, 0, i, 0))
           for _ in range(3) for d in DILATIONS],
        scratch_shapes=[pltpu.VMEM((PAIRS, tm, BAND), F32)],
        compiler_params=_params("parallel"),
    )(x, g, w, gq, gk, ones_bd)


TOK = 2048
UNITS = TOK // BAND


def _stack_masks():
    row = lax.broadcasted_iota(jnp.int32, (2 * BAND, 2 * BAND), 0) & (BAND - 1)
    col = lax.broadcasted_iota(jnp.int32, (2 * BAND, 2 * BAND), 1)
    lane = lax.broadcasted_iota(jnp.int32, (BAND, BAND), 1)
    head0 = lane < HEAD_DIM
    ones = [jnp.where(head0, 1.0, 0.0).astype(BF16), jnp.where(head0, 0.0, 1.0).astype(BF16)]
    return col - row, col, head0, ones


def _split3(x):
    hi = x.astype(BF16).astype(F32)
    mid = (x - hi).astype(BF16).astype(F32)
    return hi, mid, x - hi - mid


def _gather(srcs, dst, d):
    per = TOK // d
    at = 0
    for r in range(d):
        for src in srcs:
            rows = src[pl.ds(r, per, stride=d), :] if d > 1 else src[...]
            dst[pl.ds(at, per), :] = rows.astype(dst.dtype)
            at += per


def _scatter_add(out_ref, src, d, per_src, offset, first):
    per = TOK // d
    if d == 1:
        val = src[pl.ds(offset, per), :]
        out_ref[...] = val if first else out_ref[...] + val
        return
    for r in range(d):
        val = src[pl.ds(r * per_src + offset, per), :]
        idx = pl.ds(r, per, stride=d)
        out_ref[idx, :] = val if first else out_ref[idx, :] + val


def _dilated_specs(nblk, reverse):
    def at(s):
        return (nblk - 1 - s) if reverse else s
    main = [pl.BlockSpec((1, d, TOK // d, BAND), lambda j, s: (j, 0, at(s), 0)) for d in DILATIONS]
    prev = [pl.BlockSpec((1, d, TOK // d, BAND), lambda j, s: (j, 0, jnp.maximum(at(s) - 1, 0), 0))
            for d in DILATIONS]
    return main, prev


def _window_rows(prev_ref, main_ref, dst, d):
    per = TOK // d
    for r in range(d):
        dst[pl.ds(r * (per + BAND), BAND), :] = prev_ref[0, r, pl.ds(per - BAND, BAND), :]
        dst[pl.ds(r * (per + BAND) + BAND, per), :] = main_ref[0, r]


def _attn_fwd(qs, ks, vs, carry=None):
    t = qs[0].shape[2]
    nblk = t // TOK
    nd = len(DILATIONS)

    def body(*refs):
        q_refs, kp_refs, k_refs = refs[:nd], refs[nd:2 * nd], refs[2 * nd:3 * nd]
        vp_refs, v_refs = refs[3 * nd:4 * nd], refs[4 * nd:5 * nd]
        y_ref, l_ref, kw_s, vw_s, ob, lb, on, ln = refs[5 * nd:]
        i = pl.program_id(1)
        diff, col, head0, hm = _stack_masks()
        band_ok = jnp.logical_and(diff >= 0, diff <= BAND)
        for g, d in enumerate(DILATIONS):
            per = TOK // d
            nb = per // BAND
            pad = per + BAND
            _window_rows(kp_refs[g], k_refs[g], kw_s, d)
            _window_rows(vp_refs[g], v_refs[g], vw_s, d)
            q_ref = q_refs[g]

            def unit(u, carry):
                r, b = u // nb, u % nb
                qu = q_ref[0, r, pl.ds(pl.multiple_of(b * BAND, BAND), BAND), :]
                start = pl.multiple_of(r * pad + b * BAND, BAND)
                kw = kw_s[pl.ds(start, 2 * BAND), :]
                vw = vw_s[pl.ds(start, 2 * BAND), :]
                lo = jnp.where(jnp.logical_and(i == 0, b == 0), BAND, 0)
                s = _dot_nt(jnp.concatenate([qu * hm[0], qu * hm[1]], axis=0), kw)
                s = jnp.where(jnp.logical_and(band_ok, col >= lo), s, NEG)
                mx = jnp.max(s, axis=-1, keepdims=True)
                e = jnp.exp(s - mx)
                den = jnp.sum(e, axis=-1, keepdims=True)
                o2 = _dot(e.astype(BF16), vw) / den
                l2 = jnp.broadcast_to(mx + jnp.log(den), (2 * BAND, BAND))
                rows = pl.ds(pl.multiple_of(u * BAND, BAND), BAND)
                ob[rows, :] = jnp.where(head0, o2[:BAND], o2[BAND:])
                lb[rows, :] = jnp.where(head0, l2[:BAND], l2[BAND:])
                return carry

            lax.fori_loop(0, UNITS, unit, 0, unroll=16)
            _scatter_add(on.at[g], ob, d, per, 0, True)
            _scatter_add(ln.at[g], lb, d, per, 0, True)
        ls = [ln[0], ln[1], ln[2]]
        mx = jnp.maximum(jnp.maximum(ls[0], ls[1]), ls[2])
        es = [jnp.exp(l - mx) for l in ls]
        tot = es[0] + es[1] + es[2]
        y_ref[...] = (es[0] * on[0] + es[1] * on[1] + es[2] * on[2]) / tot
        l_ref[...] = mx + jnp.log(tot)

    main, prev = _dilated_specs(nblk, False)
    out = pl.BlockSpec((TOK, BAND), lambda j, i: (i, j))
    win_rows = max(d * (TOK // d + BAND) for d in DILATIONS)
    return _call(
        body, list(qs) + list(ks) + list(ks) + list(vs) + list(vs), name="attn_fwd",
        grid=(PAIRS, nblk), out_shape=[_sds((t, D_ATTN), F32)] * 2,
        in_specs=main + prev + main + prev + main, out_specs=[out, out],
        scratch_shapes=[pltpu.VMEM((win_rows, BAND), BF16)] * 2 + [pltpu.VMEM((TOK, BAND), F32)] * 2
        + [pltpu.VMEM((nd, TOK, BAND), F32)] * 2,
        semantics=("parallel", "parallel"), carry=carry)


def _attn_bwd(qs, ks, vs, do, lse, dd, carry=None):
    t = qs[0].shape[2]
    nblk = t // TOK
    nd = len(DILATIONS)
    offs = [sum(DILATIONS[:g]) * BAND for g in range(nd)]

    def body(*refs):
        q_refs, kp_refs, k_refs = refs[:nd], refs[nd:2 * nd], refs[2 * nd:3 * nd]
        vp_refs, v_refs = refs[3 * nd:4 * nd], refs[4 * nd:5 * nd]
        (do_ref, l_ref, d_ref, dq_ref, dk_ref, dv_ref, kw_s, vw_s, dos, lsc, dsc, dqb, dkb, dvb, ckb,
         cvb) = refs[5 * nd:]
        step = pl.program_id(1)
        i = nblk - 1 - step
        key = lax.broadcasted_iota(jnp.int32, (2 * BAND, 2 * BAND), 0)
        qry = lax.broadcasted_iota(jnp.int32, (2 * BAND, 2 * BAND), 1) & (BAND - 1)
        off = key - qry
        band_ok = jnp.logical_and(off >= 0, off <= BAND)
        lane = lax.broadcasted_iota(jnp.int32, (BAND, BAND), 1)
        head0 = lane < HEAD_DIM
        hm = [jnp.where(head0, 1.0, 0.0).astype(BF16), jnp.where(head0, 0.0, 1.0).astype(BF16)]
        piece = lane & (HEAD_DIM - 1)
        lane2 = lax.broadcasted_iota(jnp.int32, (2 * BAND, BAND), 1) & (HEAD_DIM - 1)
        ones = jnp.where(lane2 < 3, 1.0, 0.0).astype(BF16)

        def pieces(x):
            hi, mid, lo = _split3(-x)
            a = jnp.where(piece == 0, hi, jnp.where(piece == 1, mid, jnp.where(piece == 2, lo, 0.0)))
            return a.astype(BF16)

        for g, d in enumerate(DILATIONS):
            per = TOK // d
            nb = per // BAND
            pad = per + BAND
            _window_rows(kp_refs[g], k_refs[g], kw_s, d)
            _window_rows(vp_refs[g], v_refs[g], vw_s, d)
            _gather([do_ref], dos, d)
            _gather([l_ref], lsc, d)
            _gather([d_ref], dsc, d)
            dkb[...] = jnp.zeros_like(dkb)
            dvb[...] = jnp.zeros_like(dvb)
            q_ref = q_refs[g]

            def unit(u, c_):
                r, b = u // nb, u % nb
                rows = pl.ds(pl.multiple_of(u * BAND, BAND), BAND)
                qu = q_ref[0, r, pl.ds(pl.multiple_of(b * BAND, BAND), BAND), :]
                dou = dos[rows, :]
                la, da = pieces(lsc[rows, :]), pieces(dsc[rows, :])
                q2 = jnp.concatenate([qu * hm[0], qu * hm[1]], axis=0)
                do2 = jnp.concatenate([dou * hm[0], dou * hm[1]], axis=0)
                l2 = jnp.concatenate([la * hm[0], la * hm[1]], axis=0)
                d2 = jnp.concatenate([da * hm[0], da * hm[1]], axis=0)
                acc = pl.ds(pl.multiple_of(r * pad + b * BAND, BAND), 2 * BAND)
                kw = kw_s[acc, :]
                vw = vw_s[acc, :]
                lo = jnp.where(jnp.logical_and(i == 0, b == 0), BAND, 0)
                ok = jnp.logical_and(band_ok, key >= lo)
                st = _dot_nt(jnp.concatenate([kw, ones], axis=1), jnp.concatenate([q2, l2], axis=1))
                dpt = _dot_nt(jnp.concatenate([vw, ones], axis=1), jnp.concatenate([do2, d2], axis=1))
                pt = jnp.where(ok, jnp.exp(st), 0.0)
                dst = (pt * dpt).astype(BF16)
                dkb[acc, :] += _dot(dst, q2)
                dvb[acc, :] += _dot(pt.astype(BF16), do2)
                dq2 = _dot_tn(dst, kw)
                dqb[rows, :] = jnp.where(head0, dq2[:BAND], dq2[BAND:])
                return c_

            lax.fori_loop(0, UNITS, unit, 0, unroll=16)

            for r in range(d):
                last = pl.ds(r * pad + per, BAND)
                kept = pl.ds(offs[g] + r * BAND, BAND)

                @pl.when(step > 0)
                def _():
                    dkb[last, :] += ckb[kept, :]
                    dvb[last, :] += cvb[kept, :]

                ckb[kept, :] = dkb[pl.ds(r * pad, BAND), :]
                cvb[kept, :] = dvb[pl.ds(r * pad, BAND), :]
            _scatter_add(dq_ref, dqb, d, per, 0, g == 0)
            _scatter_add(dk_ref, dkb, d, pad, BAND, g == 0)
            _scatter_add(dv_ref, dvb, d, pad, BAND, g == 0)

    main, prev = _dilated_specs(nblk, True)
    tok = pl.BlockSpec((TOK, BAND), lambda j, s: (nblk - 1 - s, j))
    acc_rows = max(d * (TOK // d + BAND) for d in DILATIONS)
    kept_rows = sum(DILATIONS) * BAND
    return _call(
        body, list(qs) + list(ks) + list(ks) + list(vs) + list(vs) + [do, lse, dd], name="attn_bwd",
        grid=(PAIRS, nblk), out_shape=[_sds((t, D_ATTN), F32)] * 3,
        in_specs=main + prev + main + prev + main + [tok] * 3, out_specs=[tok] * 3,
        scratch_shapes=[pltpu.VMEM((acc_rows, BAND), BF16)] * 2 + [pltpu.VMEM((TOK, BAND), BF16)]
        + [pltpu.VMEM((TOK, BAND), F32)] * 3 + [pltpu.VMEM((acc_rows, BAND), F32)] * 2
        + [pltpu.VMEM((kept_rows, BAND), F32)] * 2,
        semantics=("parallel", "arbitrary"), carry=carry)


def _halo_rows(tm, t):
    per = tm // 8
    prev = lambda i: (jnp.maximum(i * per - 1, 0), 0)
    nxt = lambda i: (jnp.minimum((i + 1) * per, t // 8 - 1), 0)
    return prev, nxt


def _mixer_out(z, cw, y_attn, g_conv, g_attn, tm, carry=None):
    t = z.shape[0]
    prev, _ = _halo_rows(tm, t)

    def body(z_ref, zp_ref, cw_ref, y_ref, gc_ref, ga_ref, mix_ref):
        i = pl.program_id(0)
        keep = jnp.where(i > 0, 1.0, 0.0)
        cu = jnp.concatenate([zp_ref[:, 0:512] * zp_ref[:, 1024:1536] * keep,
                              z_ref[:, 0:512] * z_ref[:, 1024:1536]], axis=0)
        c = (cw_ref[0:1, :] * pltpu.roll(cu, 2, 0) + cw_ref[1:2, :] * pltpu.roll(cu, 1, 0)
             + cw_ref[2:3, :] * cu)[8:, :]
        yc = z_ref[:, 512:1024] * c
        mix_ref[:, 0:512] = (yc * _rms_scale(yc) * gc_ref[...]).astype(BF16)
        ya = y_ref[...]
        mix_ref[:, 512:1024] = (ya * _rms_scale(ya) * ga_ref[...]).astype(BF16)

    blk = pl.BlockSpec((tm, 512), lambda i: (i, 0))
    vec = pl.BlockSpec((1, 512), lambda i: (0, 0))
    return _call(
        body, [z, z, cw, y_attn, g_conv, g_attn], name="mixer_out", grid=(t // tm,),
        out_shape=_sds((t, 1024), BF16),
        in_specs=[pl.BlockSpec((tm, 1536), lambda i: (i, 0)), pl.BlockSpec((8, 1536), prev),
                  pl.BlockSpec((8, 512), lambda i: (0, 0)), blk, vec, vec],
        out_specs=pl.BlockSpec((tm, 1024), lambda i: (i, 0)),
        semantics=("parallel",), carry=carry)


def _mixer_bwd(z, dmix, y_attn, cw, g_conv, g_attn, ones_bd, tm, carry=None):
    t = z.shape[0]
    nblk = t // tm
    prev, nxt = _halo_rows(tm, t)
    e = tm + 16

    def body(z_ref, zp_ref, zn_ref, dm_ref, dmn_ref, y_ref, cw_ref, gc_ref, ga_ref, bd_ref,
             dz_ref, do_ref, dd_ref, dcw_ref, dgc_ref, dga_ref):
        i = pl.program_id(0)
        rows = lax.broadcasted_iota(jnp.int32, (e, 1), 0)
        lo = jnp.where(i > 0, 0, 8)
        hi = jnp.where(i < nblk - 1, e, tm + 8)
        ze = jnp.concatenate([zp_ref[...], z_ref[...], zn_ref[...]], axis=0)
        u, gb, gcv = ze[:, 0:512], ze[:, 512:1024], ze[:, 1024:1536]
        w0, w1, w2 = cw_ref[0:1, :], cw_ref[1:2, :], cw_ref[2:3, :]
        cu = jnp.where(rows >= lo, gcv * u, 0.0)
        cu1, cu2 = pltpu.roll(cu, 1, 0), pltpu.roll(cu, 2, 0)
        c = w0 * cu2 + w1 * cu1 + w2 * cu
        yc = gb * c
        dma = jnp.concatenate([jnp.zeros((8, 512), F32), dm_ref[:, 0:512], dmn_ref[...]], axis=0)
        dyc, ych = _rms_bwd(yc, _rms_scale(yc), gc_ref[...], dma)
        dc = jnp.where(jnp.logical_and(rows >= 8, rows < hi), dyc * gb, 0.0)
        dcu = w0 * pltpu.roll(dc, e - 2, 0) + w1 * pltpu.roll(dc, e - 1, 0) + w2 * dc
        mid = slice(8, 8 + tm)
        dz_ref[:, 0:512] = (dcu * gcv)[mid, :].astype(BF16)
        dz_ref[:, 512:1024] = (dyc * c)[mid, :].astype(BF16)
        dz_ref[:, 1024:1536] = (dcu * u)[mid, :].astype(BF16)

        ya = y_ref[...]
        dmb = dm_ref[:, 512:1024]
        dya, yah = _rms_bwd(ya, _rms_scale(ya), ga_ref[...], dmb)
        do_ref[...] = dya
        dd_ref[...] = _head_sum(dya * ya, bd_ref[...])

        @pl.when(i == 0)
        def _():
            dcw_ref[...] = jnp.zeros_like(dcw_ref)
            dgc_ref[...] = jnp.zeros_like(dgc_ref)
            dga_ref[...] = jnp.zeros_like(dga_ref)

        dcm = jnp.where(rows < tm + 8, dc, 0.0)
        dcw_ref[0:1, :] += jnp.sum(dcm * cu2, axis=0, keepdims=True)
        dcw_ref[1:2, :] += jnp.sum(dcm * cu1, axis=0, keepdims=True)
        dcw_ref[2:3, :] += jnp.sum(dcm * cu, axis=0, keepdims=True)
        dgc_ref[...] += jnp.sum((dma * ych)[mid, :], axis=0, keepdims=True)
        dga_ref[...] += jnp.sum(dmb * yah, axis=0, keepdims=True)

    blk = pl.BlockSpec((tm, 512), lambda i: (i, 0))
    vec = pl.BlockSpec((1, 512), lambda i: (0, 0))
    cwb = pl.BlockSpec((8, 512), lambda i: (0, 0))
    return _call(
        body, [z, z, z, dmix, dmix, y_attn, cw, g_conv, g_attn, ones_bd], name="mixer_bwd",
        grid=(nblk,),
        out_shape=[_sds((t, 1536), BF16), _sds((t, 512), F32), _sds((t, 512), F32),
                   _sds((8, 512), F32), _sds((1, 512), F32), _sds((1, 512), F32)],
        in_specs=[pl.BlockSpec((tm, 1536), lambda i: (i, 0)), pl.BlockSpec((8, 1536), prev),
                  pl.BlockSpec((8, 1536), nxt), pl.BlockSpec((tm, 1024), lambda i: (i, 0)),
                  pl.BlockSpec((8, 512), nxt), blk, cwb, vec, vec,
                  pl.BlockSpec((512, 512), lambda i: (0, 0))],
        out_specs=[pl.BlockSpec((tm, 1536), lambda i: (i, 0)), blk, blk, cwb, vec, vec],
        carry=carry)


def _qkv_bwd(z, dzc, dqn, dkn, dv, gq, gk, ones_bd, tm, carry=None):
    t = z.shape[0]

    def body(zq_ref, zk_ref, dzc_ref, dqn_ref, dkn_ref, dv_ref, gq_ref, gk_ref, bd_ref,
             dz_ref, dgq_ref, dgk_ref):
        bd = bd_ref[...]

        @pl.when(pl.program_id(0) == 0)
        def _():
            dgq_ref[...] = jnp.zeros_like(dgq_ref)
            dgk_ref[...] = jnp.zeros_like(dgk_ref)

        def back(v, dn, g, scale):
            r = _head_rms_scale(v, bd)
            vh = v * r
            dh = dn * (g * scale)
            dv = r * (dh - vh * (_head_sum(dh * vh, bd) * (1.0 / HEAD_DIM)))
            return dv, jnp.sum(dn * scale * vh, axis=0, keepdims=True)

        dq, dgq = back(zq_ref[...], dqn_ref[...], gq_ref[...], HEAD_DIM ** -0.5)
        dk, dgk = back(zk_ref[...], dkn_ref[...], gk_ref[...], 1.0)
        dgq_ref[...] += dgq
        dgk_ref[...] += dgk
        dz_ref[:, 0:1536] = dzc_ref[...]
        dz_ref[:, 1536:2048] = dq.astype(BF16)
        dz_ref[:, 2048:2560] = dk.astype(BF16)
        dz_ref[:, 2560:3072] = dv_ref[...].astype(BF16)

    blk = pl.BlockSpec((tm, 512), lambda i: (i, 0))
    vec = pl.BlockSpec((1, 512), lambda i: (0, 0))
    return _call(
        body, [z, z, dzc, dqn, dkn, dv, gq, gk, ones_bd], name="qkv_bwd", grid=(t // tm,),
        out_shape=[_sds((t, D_IN), BF16), _sds((1, 512), F32), _sds((1, 512), F32)],
        in_specs=[pl.BlockSpec((tm, 512), lambda i: (i, 3)), pl.BlockSpec((tm, 512), lambda i: (i, 4)),
                  pl.BlockSpec((tm, 1536), lambda i: (i, 0))] + [blk] * 3
        + [vec, vec, pl.BlockSpec((512, 512), lambda i: (0, 0))],
        out_specs=[pl.BlockSpec((tm, D_IN), lambda i: (i, 0)), vec, vec],
        carry=carry)


def _columns_from_chips(g):
    return g.transpose(1, 0, 2).reshape(g.shape[1], N_CHIPS * g.shape[2])


def kernel(x, g_mix, w_in, conv_w, g_q, g_k, g_conv_out, g_attn_out, w_out, g_ffn, w_gate, w_up, w_down, loss_target, m_g_mix, m_w_in, m_conv_w, m_g_q, m_g_k, m_g_conv_out, m_g_attn_out, m_w_out, m_g_ffn, m_w_gate, m_w_up, m_w_down, v_g_mix, v_w_in, v_conv_w, v_g_q, v_g_k, v_g_conv_out, v_g_attn_out, v_w_out, v_g_ffn, v_w_gate, v_w_up, v_w_down):
    t = x.shape[1]
    xs = x[0]
    target = loss_target[0]
    tm = min(512, t)
    tmm = min(1024, t)

    cw_pad = jnp.pad(conv_w[0], ((0, 13), (0, 0)))
    gathered = _all_gather([w_in[0].astype(BF16), cw_pad])
    win = _columns_from_chips(gathered[0])
    cw = jnp.pad(gathered[1][:, 0:3, :].transpose(1, 0, 2).reshape(3, D_CONV), ((0, 5), (0, 0)))
    later = [w_out[0].astype(BF16), w_gate[0].T.astype(BF16), w_up[0].T.astype(BF16),
             w_down[0].astype(BF16)]

    head_id = jnp.arange(D_ATTN) // HEAD_DIM
    ones_bd = (head_id[:, None] == head_id[None, :]).astype(BF16)
    gq_t = jnp.tile(g_q, (1, D_ATTN // HEAD_DIM))
    gk_t = jnp.tile(g_k, (1, D_ATTN // HEAD_DIM))

    h1, z, *dilated = _in_proj(xs, g_mix, win, gq_t, gk_t, ones_bd, tm)
    nd = len(DILATIONS)
    qs, ks, vs = dilated[:nd], dilated[nd:2 * nd], dilated[2 * nd:]
    (y_attn, lse), gathered = _attn_fwd(qs, ks, vs, carry=_x_gather_chips(later))
    mix, gathered = _mixer_out(z, cw, y_attn, g_conv_out, g_attn_out, tm,
                               carry=_x_gather_sibling(gathered))
    wout = gathered[0].reshape(D_MODEL, D_MODEL)
    wgate_t = gathered[1].reshape(D_FF, D_MODEL)
    wup_t = gathered[2].reshape(D_FF, D_MODEL)
    wdown = gathered[3].reshape(D_FF, D_MODEL)
    (x1,) = _matmul("out_proj", mix, wout, [xs], [F32], lambda acc, r: (r + acc,), tm, D_MODEL)
    h2, gate, up, act = _norm_matmul("ffn_up", x1, g_ffn, [wgate_t, wup_t], tm, D_FF, True, BF16,
                                     transposed_w=True)

    def loss_epilogue(acc, r, tgt):
        err = r + acc - tgt
        dy = err * (1.0 / D_MODEL)
        return dy, dy, jnp.sum(err * err)

    dx2, dx2b, loss_sum = _matmul("ffn_down_loss", act, wdown, [x1, target], [F32, BF16],
                                  loss_epilogue, tm, D_MODEL, loss=True)

    def swiglu_bwd(da, gt, u):
        gt, u = gt.astype(F32), u.astype(F32)
        s = _sigmoid(gt)
        return da * u * (s * (1.0 + gt * (1.0 - s))), da * (gt * s)

    dgate, dup = _matmul("ffn_down_bwd", dx2b, wdown, [gate, up], [BF16, BF16], swiglu_bwd,
                         tm, D_FF, transposed_w=True)
    gw_down = _matmul_tn("grad_w_down", act, dx2b, 512, tmm)
    gw_gate_t = _matmul_tn("grad_w_gate", dgate, h2, 512, tmm)
    gw_up_t = _matmul_tn("grad_w_up", dup, h2, 512, tmm)

    me = 2 * lax.axis_index("x") + lax.axis_index("y")
    where = jnp.stack([lax.axis_index("c"), me]).astype(jnp.int32)

    def pair_sums(names, full, got):
        return [_pair_sum(f"pair_sum_{nme}", a, b, where) for nme, a, b in zip(names, full, got)]

    def chip_sums(names, pair, got):
        return [_chip_sum(f"chip_sum_{nme}", own, b) for nme, (_, own), b in zip(names, pair, got)]

    ffn = ["w_gate", "w_up", "w_down"]
    full = [g.reshape(N_CHIPS, D_FF // N_CHIPS, D_MODEL) for g in (gw_gate_t, gw_up_t, gw_down)]
    (dx1, dx1b, gg_ffn), got = _matmul_norm_bwd(
        "ffn_up_bwd", [(dgate, wgate_t), (dup, wup_t)], x1, dx2, g_ffn, tm, carry=_x_pair(full),
        transposed_w=False)
    pair = pair_sums(ffn, full, got)
    (dmix,) = _matmul("out_proj_bwd", dx1b, wout, [], [F32], lambda acc: (acc,), tm, D_MODEL,
                      transposed_w=True)
    gw_out = _matmul_tn("grad_w_out", mix, dx1b, 512, tmm)
    full = [gw_out.reshape(N_CHIPS, D_MODEL // N_CHIPS, D_MODEL)]
    (dzc, do, dd, gcw, gg_conv, gg_attn), got = _mixer_bwd(
        z, dmix, y_attn, cw, g_conv_out, g_attn_out, ones_bd, tm, carry=_x_pair(full))
    pair += pair_sums(["w_out"], full, got)
    early = ffn + ["w_out"]
    (dqn, dkn, dv), got = _attn_bwd(qs, ks, vs, do, lse, dd, carry=_x_chips([p for p, _ in pair]))
    mine = chip_sums(early, pair, got)
    (dz, gg_q, gg_k), theirs = _qkv_bwd(z, dzc, dqn, dkn, dv, gq_t, gk_t, ones_bd, tm,
                                        carry=_x_share(mine))
    full = [_matmul_tn("grad_w_in", h1, dz, D_IN // N_CHIPS, tmm, by_chip=True)]
    grad_x, _, gg_mix = _matmul_norm_bwd("in_proj_bwd", [(dz, win)], xs, dx1, g_mix, tm)
    got = _exchange_alone("grad_pair_exchange_w_in", _x_pair(full))
    pair = pair_sums(["w_in"], full, got)
    got = _exchange_alone("grad_chip_exchange_w_in", _x_chips([pair[0][0]]))
    mine += chip_sums(["w_in"], pair, got)
    theirs = list(theirs) + list(_exchange_alone("grad_pair_share_w_in", _x_share(mine[-1:])))
    big = early + ["w_in"]

    small = _small_all_reduce({
        "g_mix": gg_mix, "g_ffn": gg_ffn, "g_conv_out": gg_conv, "g_attn_out": gg_attn,
        "g_q": gg_q, "g_k": gg_k, "loss": loss_sum, "conv_w": gcw})
    heads = D_ATTN // HEAD_DIM
    grads = {
        "g_mix": small[0:1, :], "g_ffn": small[1:2, :],
        "g_conv_out": small[2:3, 0:512], "g_attn_out": small[2:3, 512:1024],
        "g_q": small[3, 0:512].reshape(heads, HEAD_DIM).sum(axis=0)[None, :],
        "g_k": small[3, 512:1024].reshape(heads, HEAD_DIM).sum(axis=0)[None, :],
        "conv_w": lax.dynamic_slice(small[8:11, 0:512], (0, me * (D_CONV // N_CHIPS)),
                                    (3, D_CONV // N_CHIPS)),
    }
    halves = dict(zip(big, zip(mine, theirs)))
    loss = small[4, 0] * 0.5 * (1.0 / D_MODEL)

    weights = dict(g_mix=g_mix, w_in=w_in, conv_w=conv_w, g_q=g_q, g_k=g_k, g_conv_out=g_conv_out,
                   g_attn_out=g_attn_out, w_out=w_out, g_ffn=g_ffn, w_gate=w_gate, w_up=w_up,
                   w_down=w_down)
    moments_m = dict(g_mix=m_g_mix, w_in=m_w_in, conv_w=m_conv_w, g_q=m_g_q, g_k=m_g_k,
                     g_conv_out=m_g_conv_out, g_attn_out=m_g_attn_out, w_out=m_w_out, g_ffn=m_g_ffn,
                     w_gate=m_w_gate, w_up=m_w_up, w_down=m_w_down)
    moments_v = dict(g_mix=v_g_mix, w_in=v_w_in, conv_w=v_conv_w, g_q=v_g_q, g_k=v_g_k,
                     g_conv_out=v_g_conv_out, g_attn_out=v_g_attn_out, w_out=v_w_out, g_ffn=v_g_ffn,
                     w_gate=v_w_gate, w_up=v_w_up, w_down=v_w_down)
    names = list(weights)
    out_g, out_d, out_m, out_v = [], [], [], []
    for nme in names:
        wgt = weights[nme]
        shape2 = wgt.shape[-2:] if wgt.ndim == 3 else wgt.shape
        flip = nme in ("w_gate", "w_up")

        def to2d(a):
            return a.reshape(shape2).T if flip else a.reshape(shape2)

        def back(a):
            return (a.T if flip else a).reshape(wgt.shape)

        state = (to2d(wgt), to2d(moments_m[nme]), to2d(moments_v[nme]))
        if nme in halves:
            g2, dlt, nm, nv = _adamw_shard(f"adamw_{nme}", *state, *halves[nme], where)
        else:
            g2 = grads[nme].reshape(shape2)
            dlt, nm, nv = _adamw(f"adamw_{nme}", state[0], g2, state[1], state[2])
        out_g.append(back(g2))
        out_d.append(back(dlt))
        out_m.append(back(nm))
        out_v.append(back(nv))
    return (loss, grad_x[None], *out_g, *out_d, *out_m, *out_v)
```

```python
import functools
from typing import Any, Callable, NamedTuple, Sequence

import jax
import jax.numpy as jnp
from jax import lax
from jax.experimental import pallas as pl
from jax.experimental.pallas import tpu as pltpu

F32 = jnp.float32
BF16 = jnp.bfloat16
MESH = pl.DeviceIdType.MESH

D_MODEL = 1024
D_CONV = 512
D_ATTN = 512
HEAD_DIM = 64
D_FF = 2816
D_IN = 3 * D_CONV + 3 * D_ATTN
DILATIONS = (1, 4, 16)
BAND = 128
EPS = 1e-6
NEG = -1e30
N_CHIPS = 4

ADAM_LR = 0.001
ADAM_B1 = 0.9
ADAM_B2 = 0.999
ADAM_EPS = 1e-08
ADAM_WD = 0.01
ADAM_STEP = 10

V7X_VMEM_BYTES = 64 * 1024 * 1024
VMEM_LIMIT = V7X_VMEM_BYTES - 8 * 1024 * 1024
ANY = pl.BlockSpec(memory_space=pl.ANY)
VMEM_WHOLE = pl.BlockSpec(memory_space=pltpu.VMEM)


def _params(*sem):
    return pltpu.CompilerParams(dimension_semantics=sem, vmem_limit_bytes=VMEM_LIMIT)


def _sds(shape, dtype):
    return jax.ShapeDtypeStruct(shape, dtype)


def _resident(whole):
    return pl.Buffered(1) if whole else None


def _place():
    x, y, c = lax.axis_index("x"), lax.axis_index("y"), lax.axis_index("c")
    chips = [(1 - x, y), (x, 1 - y), (1 - x, 1 - y)]
    return x, y, c, 2 * x + y, chips, [2 * cx + cy for cx, cy in chips]


def _all_gather(shards):
    n = len(shards)

    def body(*refs):
        ins, outs, stage = refs[:n], refs[n:2 * n], refs[2 * n:3 * n]
        ssem, rsem, fsem, gsem, lsem, osem = refs[3 * n:]
        x, y, c, me, chips, cids = _place()
        sib = (x, y, 1 - c)

        def half(w, which):
            h = shards[w].shape[0] // 2
            return pl.ds(pl.multiple_of(which * h, 8), h)

        loads = [pltpu.make_async_copy(ins[w], stage[w], lsem.at[w]) for w in range(n)]
        local = [pltpu.make_async_copy(stage[w], outs[w].at[me], osem.at[w]) for w in range(n)]
        for cp in loads:
            cp.start()

        def chip_copy(w, j, src_slot):
            rows = half(w, c)
            return pltpu.make_async_remote_copy(
                src_ref=ins[w].at[rows], dst_ref=outs[w].at[src_slot, rows],
                send_sem=ssem.at[3 * w + j], recv_sem=rsem.at[3 * w + j],
                device_id=(*chips[j], c), device_id_type=MESH)

        def sib_copy(w, j, which):
            rows = half(w, which)
            return pltpu.make_async_remote_copy(
                src_ref=outs[w].at[cids[j], rows], dst_ref=outs[w].at[cids[j], rows],
                send_sem=fsem.at[3 * w + j], recv_sem=gsem.at[3 * w + j],
                device_id=sib, device_id_type=MESH)

        sends = [chip_copy(w, j, me) for w in range(n) for j in range(3)]
        for cp in sends:
            cp.start()
        for w in range(n):
            loads[w].wait()
            local[w].start()
        passed = []
        for w in range(n):
            for j in range(3):
                chip_copy(w, j, cids[j]).wait_recv()
                cp = sib_copy(w, j, c)
                cp.start()
                passed.append(cp)
        for w in range(n):
            for j in range(3):
                sib_copy(w, j, 1 - c).wait_recv()
        for cp in sends + passed:
            cp.wait_send()
        for cp in local:
            cp.wait()

    return pl.pallas_call(
        body, name="all_gather_weights",
        out_shape=[_sds((N_CHIPS,) + s.shape, s.dtype) for s in shards],
        in_specs=[ANY] * n, out_specs=[ANY] * n,
        scratch_shapes=[pltpu.VMEM(s.shape, s.dtype) for s in shards]
        + [pltpu.SemaphoreType.DMA((3 * n,))] * 4 + [pltpu.SemaphoreType.DMA((n,))] * 2,
        compiler_params=pltpu.CompilerParams(vmem_limit_bytes=VMEM_LIMIT),
    )(*shards)


class _Exchange(NamedTuple):
    srcs: Sequence[Any]
    lands: Sequence[Any]
    outs: Sequence[Any]
    n_sems: int
    copies: Callable


def _remote(src, dst, ssem, rsem, k, to):
    return pltpu.make_async_remote_copy(src_ref=src, dst_ref=dst, send_sem=ssem.at[k],
                                        recv_sem=rsem.at[k], device_id=to, device_id_type=MESH)


def _x_gather_chips(shards):
    def copies(srcs, lands, outs, ssem, rsem):
        _, _, c, me, chips, cids = _place()
        go, arrive = [], []
        for w, s in enumerate(shards):
            h = s.shape[0] // 2
            rows = pl.ds(pl.multiple_of(c * h, 8), h)
            for j in range(3):
                to = (*chips[j], c)
                go.append(_remote(srcs[w].at[rows], lands[w].at[me, rows], ssem, rsem, 3 * w + j, to))
                arrive.append(_remote(srcs[w].at[rows], lands[w].at[cids[j], rows], ssem, rsem,
                                      3 * w + j, to))
        return go, arrive

    lands = [jnp.broadcast_to(s[None], (N_CHIPS,) + s.shape) for s in shards]
    return _Exchange(shards, lands, [], 3 * len(shards), copies)


def _x_gather_sibling(gathered):
    def copies(srcs, lands, outs, ssem, rsem):
        x, y, c, _, _, cids = _place()
        go, arrive = [], []
        for w, g in enumerate(gathered):
            h = g.shape[1] // 2
            mine = pl.ds(pl.multiple_of(c * h, 8), h)
            theirs = pl.ds(pl.multiple_of((1 - c) * h, 8), h)
            for j in range(3):
                slab = lands[w].at[cids[j]]
                go.append(_remote(slab.at[mine], slab.at[mine], ssem, rsem, 3 * w + j, (x, y, 1 - c)))
                arrive.append(_remote(slab.at[theirs], slab.at[theirs], ssem, rsem, 3 * w + j,
                                      (x, y, 1 - c)))
        return go, arrive

    return _Exchange([], gathered, [], 3 * len(gathered), copies)


def _x_pair(grads):
    def copies(srcs, lands, outs, ssem, rsem):
        x, y, c, _, _, _ = _place()
        go = []
        for w, g in enumerate(grads):
            h = g.shape[1] // 2
            theirs = pl.ds(pl.multiple_of((1 - c) * h, 8), h)
            go.append(_remote(srcs[w].at[:, theirs, :], outs[w], ssem, rsem, w, (x, y, 1 - c)))
        return go, go

    outs = [_sds((N_CHIPS, g.shape[1] // 2, g.shape[2]), g.dtype) for g in grads]
    return _Exchange(grads, [], outs, len(grads), copies)


def _x_chips(parts):
    def copies(srcs, lands, outs, ssem, rsem):
        _, _, c, _, chips, cids = _place()
        go = [_remote(srcs[w].at[cids[j]], outs[w].at[j], ssem, rsem, 3 * w + j, (*chips[j], c))
              for w in range(len(parts)) for j in range(3)]
        return go, go

    outs = [_sds((3,) + p.shape[1:], p.dtype) for p in parts]
    return _Exchange(parts, [], outs, 3 * len(parts), copies)


def _x_share(halves):
    def copies(srcs, lands, outs, ssem, rsem):
        x, y, c, _, _, _ = _place()
        go = [_remote(srcs[w], outs[w], ssem, rsem, w, (x, y, 1 - c)) for w in range(len(halves))]
        return go, go

    return _Exchange(halves, [], [_sds(h.shape, h.dtype) for h in halves], len(halves), copies)


def _call(body, args, *, name, grid, in_specs, out_specs, out_shape, scratch_shapes=(),
          semantics=None, carry=None):
    single = not isinstance(out_shape, (list, tuple))
    out_shape = [out_shape] if single else list(out_shape)
    out_specs = [out_specs] if single else list(out_specs)
    if carry is None:
        res = pl.pallas_call(
            body, name=name, grid=grid, in_specs=list(in_specs), out_specs=out_specs,
            out_shape=out_shape, scratch_shapes=list(scratch_shapes),
            compiler_params=_params(*(semantics or ("arbitrary",) * len(grid))))(*args)
        return res[0] if single else res
    n_in, n_out, n_scr = len(args), len(out_shape), len(scratch_shapes)
    n_src, n_land, n_new = len(carry.srcs), len(carry.lands), len(carry.outs)

    def carrying(*refs):
        at = 0
        parts = []
        for n in (n_in, n_src, n_land, n_out, n_land, n_new, n_scr, 2):
            parts.append(refs[at:at + n])
            at += n
        ins, srcs, _, outs, lands, news, scratch, (ssem, rsem) = parts
        ids = [pl.program_id(a) for a in range(len(grid))]
        first = functools.reduce(jnp.logical_and, [i == 0 for i in ids])
        last = functools.reduce(jnp.logical_and, [i == g - 1 for i, g in zip(ids, grid)])
        go, arrive = carry.copies(srcs, lands, news, ssem, rsem)

        @pl.when(first)
        def _():
            for cp in go:
                cp.start()

        body(*ins, *outs, *scratch)

        @pl.when(last)
        def _():
            for cp in go:
                cp.wait_send()
            for cp in arrive:
                cp.wait_recv()

    res = pl.pallas_call(
        carrying, name=name, grid=grid,
        in_specs=list(in_specs) + [ANY] * (n_src + n_land),
        out_specs=out_specs + [ANY] * (n_land + n_new),
        out_shape=out_shape + [_sds(a.shape, a.dtype) for a in carry.lands] + list(carry.outs),
        input_output_aliases={n_in + n_src + i: n_out + i for i in range(n_land)},
        scratch_shapes=list(scratch_shapes) + [pltpu.SemaphoreType.DMA((carry.n_sems,))] * 2,
        compiler_params=_params(*(("arbitrary",) * len(grid))))(*args, *carry.srcs, *carry.lands)
    own = res[:n_out]
    return (own[0] if single else own), res[n_out:]


def _exchange_alone(name, exchange):
    def body(x_ref, o_ref):
        o_ref[...] = x_ref[...]

    blk = pl.BlockSpec((8, 128), lambda i: (0, 0))
    _, res = _call(body, [jnp.zeros((8, 128), F32)], name=name, grid=(1,), in_specs=[blk],
                   out_specs=blk, out_shape=_sds((8, 128), F32), carry=exchange)
    return res


def _row_block(r, want):
    return max(d for d in range(1, min(want, r) + 1) if r % d == 0 and (d % 8 == 0 or d == r))


def _pair_sum(name, full, got, where):
    _, r, n = full.shape
    h = r // 2
    tr = _row_block(h, 256)
    nb = h // tr

    def body(w_ref, a_ref, b_ref, o_ref, own_ref):
        total = a_ref[...] + b_ref[...]
        o_ref[...] = total.astype(BF16)

        @pl.when(pl.program_id(1) == w_ref[1])
        def _():
            own_ref[...] = total[0]

    blk = pl.BlockSpec((1, tr, n), lambda i, s, w: (s, i, 0))
    return pl.pallas_call(
        body, name=name, out_shape=[_sds(got.shape, BF16), _sds((h, n), F32)],
        grid_spec=pltpu.PrefetchScalarGridSpec(
            num_scalar_prefetch=1, grid=(nb, N_CHIPS),
            in_specs=[pl.BlockSpec((1, tr, n), lambda i, s, w: (s, w[0] * nb + i, 0)), blk],
            out_specs=[blk, pl.BlockSpec((tr, n), lambda i, s, w: (i, 0))]),
        compiler_params=_params("parallel", "arbitrary"),
    )(where, full, got)


def _chip_sum(name, own, got):
    h, n = own.shape
    tr = _row_block(h, 256)

    def body(a_ref, b0, b1, b2, o_ref):
        o_ref[...] = ((a_ref[...] + b0[0].astype(F32)) + b1[0].astype(F32)) + b2[0].astype(F32)

    def slot(j):
        return pl.BlockSpec((1, tr, n), lambda i: (j, i, 0))

    blk = pl.BlockSpec((tr, n), lambda i: (i, 0))
    return pl.pallas_call(
        body, name=name, grid=(h // tr,), out_shape=_sds((h, n), F32),
        in_specs=[blk, slot(0), slot(1), slot(2)], out_specs=blk,
        compiler_params=_params("parallel"),
    )(own, got, got, got)


SMALL_ROWS = 16
SMALL_LAYOUT = (
    ("g_mix", 0, 0, 1, 1024), ("g_ffn", 1, 0, 1, 1024), ("g_conv_out", 2, 0, 1, 512),
    ("g_attn_out", 2, 512, 1, 512), ("g_q", 3, 0, 1, 512), ("g_k", 3, 512, 1, 512),
    ("loss", 4, 0, 1, 128), ("conv_w", 8, 0, 8, 512))


def _small_all_reduce(parts):
    names = [s[0] for s in SMALL_LAYOUT]

    def body(*refs):
        ins = refs[:len(names)]
        out_ref, stage, buf, ssem, rsem = refs[len(names):]
        x, y, c, _, _, _ = _place()
        me = 4 * x + 2 * y + c
        stage[...] = jnp.zeros_like(stage)
        for ref, (_, r0, c0, nr, nc) in zip(ins, SMALL_LAYOUT):
            stage[r0:r0 + nr, c0:c0 + nc] = ref[0:nr, :]
        buf[me] = stage[...]
        peers = []
        for d in range(1, 8):
            px = 1 - x if d & 4 else x
            py = 1 - y if d & 2 else y
            pc = 1 - c if d & 1 else c
            peers.append(((px, py, pc), 4 * px + 2 * py + pc))
        sends = [pltpu.make_async_remote_copy(
            src_ref=stage, dst_ref=buf.at[me], send_sem=ssem.at[k], recv_sem=rsem.at[k],
            device_id=peer, device_id_type=MESH) for k, (peer, _) in enumerate(peers)]
        for cp in sends:
            cp.start()
        for k, (peer, pid) in enumerate(peers):
            pltpu.make_async_remote_copy(
                src_ref=stage, dst_ref=buf.at[pid], send_sem=ssem.at[k], recv_sem=rsem.at[k],
                device_id=peer, device_id_type=MESH).wait_recv()
        for cp in sends:
            cp.wait_send()
        acc = buf[0]
        for k in range(1, 8):
            acc = acc + buf[k]
        out_ref[...] = acc

    return pl.pallas_call(
        body, name="small_all_reduce", out_shape=_sds((SMALL_ROWS, 1024), F32),
        in_specs=[VMEM_WHOLE] * len(names), out_specs=VMEM_WHOLE,
        scratch_shapes=[pltpu.VMEM((SMALL_ROWS, 1024), F32), pltpu.VMEM((8, SMALL_ROWS, 1024), F32),
                        pltpu.SemaphoreType.DMA((7,)), pltpu.SemaphoreType.DMA((7,))],
    )(*[parts[k] for k in names])


def _dot(a, b):
    return jnp.dot(a, b, preferred_element_type=F32)


def _dot_nt(a, b):
    return lax.dot_general(a, b, (((1,), (1,)), ((), ())), preferred_element_type=F32)


def _dot_tn(a, b):
    return lax.dot_general(a, b, (((0,), (0,)), ((), ())), preferred_element_type=F32)


def _sigmoid(v):
    return 1.0 / (1.0 + jnp.exp(-v))


def _rms_scale(v):
    return lax.rsqrt(jnp.mean(v * v, axis=-1, keepdims=True) + EPS)


def _rms_bwd(v, r, g, dy):
    vh = v * r
    dh = dy * g
    return r * (dh - vh * jnp.mean(dh * vh, axis=-1, keepdims=True)), vh


def _head_sum(a, ones_bd):
    hi = a.astype(BF16)
    lo = (a - hi.astype(F32)).astype(BF16)
    return _dot(hi, ones_bd) + _dot(lo, ones_bd)


def _head_rms_scale(v, ones_bd):
    return lax.rsqrt(_head_sum(v * v, ones_bd) * (1.0 / HEAD_DIM) + EPS)


MXU_COLUMNS = 256


def _column_chunks(n):
    width = MXU_COLUMNS if n % MXU_COLUMNS == 0 else n
    return [slice(c, c + width) for c in range(0, n, width)]


def _norm_matmul(name, x, g, ws, tm, tn, swiglu, out_dtype=F32, transposed_w=False):
    t, d = x.shape
    n = ws[0].shape[0] if transposed_w else ws[0].shape[1]
    nw = len(ws)

    def body(x_ref, g_ref, *refs):
        w_refs, h_ref, o_refs = refs[:nw], refs[nw], refs[nw + 1:2 * nw + 1]
        hs = refs[-1]

        @pl.when(pl.program_id(1) == 0)
        def _():
            xv = x_ref[...]
            h = (xv * _rms_scale(xv) * g_ref[...]).astype(BF16)
            hs[...] = h
            h_ref[...] = h

        h = hs[...]
        for cols in _column_chunks(tn):
            outs = [_dot_nt(h, w[cols, :]) if transposed_w else _dot(h, w[:, cols]) for w in w_refs]
            for o_ref, o in zip(o_refs, outs):
                o_ref[:, cols] = o.astype(out_dtype)
            if swiglu:
                refs[2 * nw + 1][:, cols] = (outs[0] * _sigmoid(outs[0]) * outs[1]).astype(BF16)

    row = pl.BlockSpec((tm, d), lambda i, j: (i, 0))
    col = pl.BlockSpec((tm, tn), lambda i, j: (i, j))
    out_shape = [_sds((t, d), BF16)] + [_sds((t, n), out_dtype)] * nw
    out_specs = [row] + [col] * nw
    if swiglu:
        out_shape.append(_sds((t, n), BF16))
        out_specs.append(col)
    return pl.pallas_call(
        body, name=name, grid=(t // tm, n // tn), out_shape=out_shape,
        in_specs=[row, pl.BlockSpec((1, d), lambda i, j: (0, 0))]
        + [pl.BlockSpec((tn, d), lambda i, j: (j, 0), pipeline_mode=_resident(tn == n))
           if transposed_w
           else pl.BlockSpec((d, tn), lambda i, j: (0, j), pipeline_mode=_resident(tn == n))] * nw,
        out_specs=out_specs, scratch_shapes=[pltpu.VMEM((tm, d), BF16)],
        compiler_params=_params("parallel", "arbitrary"),
    )(x, g, *ws)


def _matmul(name, a, w, extras, out_dtypes, epilogue, tm, tn, transposed_w=False, loss=False):
    t, k = a.shape
    n = w.shape[0] if transposed_w else w.shape[1]
    ne, no = len(extras), len(out_dtypes)

    def body(a_ref, w_ref, *refs):
        e_refs, o_refs = refs[:ne], refs[ne:]
        a = a_ref[...]
        total = 0.0
        for cols in _column_chunks(tn):
            acc = _dot_nt(a, w_ref[cols, :]) if transposed_w else _dot(a, w_ref[:, cols])
            res = epilogue(acc, *[e[:, cols] for e in e_refs])
            for o_ref, r in zip(o_refs[:no], res[:no]):
                o_ref[:, cols] = r.astype(o_ref.dtype)
            if loss:
                total = total + res[no]
        if loss:
            first = jnp.logical_and(pl.program_id(0) == 0, pl.program_id(1) == 0)

            @pl.when(first)
            def _():
                o_refs[no][...] = jnp.zeros_like(o_refs[no])

            o_refs[no][...] += total

    col = pl.BlockSpec((tm, tn), lambda i, j: (i, j))
    w_spec = (pl.BlockSpec((tn, k), lambda i, j: (j, 0), pipeline_mode=_resident(tn == n))
              if transposed_w
              else pl.BlockSpec((k, tn), lambda i, j: (0, j), pipeline_mode=_resident(tn == n)))
    out_shape = [_sds((t, n), dt) for dt in out_dtypes]
    out_specs = [col] * no
    if loss:
        out_shape.append(_sds((8, 128), F32))
        out_specs.append(pl.BlockSpec((8, 128), lambda i, j: (0, 0)))
    return pl.pallas_call(
        body, name=name, grid=(t // tm, n // tn), out_shape=out_shape,
        in_specs=[pl.BlockSpec((tm, k), lambda i, j: (i, 0)), w_spec] + [col] * ne,
        out_specs=out_specs,
        compiler_params=_params(*(("arbitrary", "arbitrary") if loss else ("parallel", "parallel"))),
    )(a, w, *extras)


def _matmul_norm_bwd(name, pairs, x, dres, g, tm, carry=None, transposed_w=True):
    t, d = x.shape
    npairs = len(pairs)
    product = _dot_nt if transposed_w else _dot

    def body(*refs):
        a_refs, w_refs = refs[:npairs], refs[npairs:2 * npairs]
        x_ref, r_ref, g_ref, dx_ref, dxb_ref, dg_ref = refs[2 * npairs:]
        dy = product(a_refs[0][...], w_refs[0][...])
        for a_ref, w_ref in zip(a_refs[1:], w_refs[1:]):
            dy = dy + product(a_ref[...], w_ref[...])
        xv = x_ref[...]
        dx, xh = _rms_bwd(xv, _rms_scale(xv), g_ref[...], dy)
        dx = dx + r_ref[...]
        dx_ref[...] = dx
        dxb_ref[...] = dx.astype(BF16)

        @pl.when(pl.program_id(0) == 0)
        def _():
            dg_ref[...] = jnp.zeros_like(dg_ref)

        dg_ref[...] += jnp.sum(dy * xh, axis=0, keepdims=True)

    row = pl.BlockSpec((tm, d), lambda i: (i, 0))
    vec = pl.BlockSpec((1, d), lambda i: (0, 0))
    return _call(
        body, [a for a, _ in pairs] + [w for _, w in pairs] + [x, dres, g], name=name,
        grid=(t // tm,), out_shape=[_sds((t, d), F32), _sds((t, d), BF16), _sds((1, d), F32)],
        in_specs=[pl.BlockSpec((tm, a.shape[1]), lambda i: (i, 0)) for a, _ in pairs]
        + [pl.BlockSpec(w.shape, lambda i: (0, 0), pipeline_mode=pl.Buffered(1)) for _, w in pairs]
        + [row, row, vec],
        out_specs=[row, row, vec], carry=carry)


def _matmul_tn(name, a, g, tn, tk, by_chip=False):
    t, ka = a.shape
    n = g.shape[1]

    def body(a_ref, g_ref, o_ref):
        @pl.when(pl.program_id(1) == 0)
        def _():
            o_ref[...] = jnp.zeros_like(o_ref)

        acc = _dot_tn(a_ref[...], g_ref[...])
        o_ref[...] += acc[None] if by_chip else acc

    return pl.pallas_call(
        body, name=name, grid=(n // tn, t // tk),
        out_shape=_sds((n // tn, ka, tn) if by_chip else (ka, n), F32),
        in_specs=[pl.BlockSpec((tk, ka), lambda j, s: (s, 0)),
                  pl.BlockSpec((tk, tn), lambda j, s: (s, j))],
        out_specs=(pl.BlockSpec((1, ka, tn), lambda j, s: (j, 0, 0)) if by_chip
                   else pl.BlockSpec((ka, tn), lambda j, s: (0, j))),
        compiler_params=_params("parallel", "arbitrary"),
    )(a, g)


def _elementwise(name, fn, ins, out_dtypes, tr):
    r, n = ins[0].shape
    tr = _row_block(r, tr)
    ni = len(ins)

    def body(*refs):
        res = fn(*[ref[...] for ref in refs[:ni]])
        for o_ref, v in zip(refs[ni:], res):
            o_ref[...] = v.astype(o_ref.dtype)

    blk = pl.BlockSpec((tr, n), lambda i: (i, 0))
    return pl.pallas_call(
        body, name=name, grid=(r // tr,), out_shape=[_sds((r, n), dt) for dt in out_dtypes],
        in_specs=[blk] * ni, out_specs=[blk] * len(out_dtypes),
        compiler_params=_params("parallel"),
    )(*ins)


def _adamw_update(w, g, m, v):
    m = ADAM_B1 * m + (1.0 - ADAM_B1) * g
    v = ADAM_B2 * v + (1.0 - ADAM_B2) * (g * g)
    m_hat = m / (1.0 - ADAM_B1 ** ADAM_STEP)
    v_hat = v / (1.0 - ADAM_B2 ** ADAM_STEP)
    return -ADAM_LR * (m_hat / (jnp.sqrt(v_hat) + ADAM_EPS) + ADAM_WD * w), m, v


def _adamw(name, w, g, m, v):
    return _elementwise(name, _adamw_update, [w, g, m, v], [F32] * 3, 256)


def _adamw_shard(name, w, m, v, mine, theirs, where):
    r, n = w.shape
    h = r // 2
    tr = _row_block(h, 256)
    nb = h // tr

    def body(w_ref, p_ref, m_ref, v_ref, a_ref, b_ref, g_ref, d_ref, nm_ref, nv_ref):
        g = jnp.where(pl.program_id(0) == w_ref[0], a_ref[...], b_ref[...])
        g_ref[...] = g
        d_ref[...], nm_ref[...], nv_ref[...] = _adamw_update(p_ref[...], g, m_ref[...], v_ref[...])

    whole = pl.BlockSpec((tr, n), lambda s, i, c: (s * nb + i, 0))
    half = pl.BlockSpec((tr, n), lambda s, i, c: (i, 0))
    return pl.pallas_call(
        body, name=name, out_shape=[_sds((r, n), F32)] * 4,
        grid_spec=pltpu.PrefetchScalarGridSpec(
            num_scalar_prefetch=1, grid=(2, nb), in_specs=[whole] * 3 + [half] * 2,
            out_specs=[whole] * 4),
        compiler_params=_params("parallel", "parallel"),
    )(where, w, m, v, mine, theirs)


PAIRS = D_ATTN // BAND


def _in_proj(x, g, w, gq, gk, ones_bd, tm):
    t, dm = x.shape
    n = w.shape[1]
    nd = len(DILATIONS)
    first = 3 * D_CONV

    def body(x_ref, g_ref, w_ref, gq_ref, gk_ref, bd_ref, h_ref, z_ref, *refs):
        outs, slab = refs[:3 * nd], refs[3 * nd]
        xv = x_ref[...]
        h = (xv * _rms_scale(xv) * g_ref[...]).astype(BF16)
        h_ref[...] = h
        for cols in _column_chunks(n):
            z_ref[:, cols] = _dot(h, w_ref[:, cols])
        bd = bd_ref[...]
        q = z_ref[:, first:first + D_ATTN]
        k = z_ref[:, first + D_ATTN:first + 2 * D_ATTN]
        vals = [(q * _head_rms_scale(q, bd) * gq_ref[...]) * HEAD_DIM ** -0.5,
                k * _head_rms_scale(k, bd) * gk_ref[...], z_ref[:, first + 2 * D_ATTN:n]]
        for m, val in enumerate(vals):
            for c in range(PAIRS):
                slab[c] = val[:, c * BAND:(c + 1) * BAND]
            for a, d in enumerate(DILATIONS):
                o_ref = outs[m * nd + a]
                for c in range(PAIRS):
                    for r in range(d):
                        rows = slab.at[c][pl.ds(r, tm // d, stride=d), :] if d > 1 else slab[c]
                        o_ref[c, r] = rows.astype(BF16)

    row = pl.BlockSpec((tm, dm), lambda i: (i, 0))
    vec = pl.BlockSpec((1, D_ATTN), lambda i: (0, 0))
    return pl.pallas_call(
        body, name="in_proj", grid=(t // tm,),
        out_shape=[_sds((t, dm), BF16), _sds((t, n), F32)]
        + [_sds((PAIRS, d, t // d, BAND), BF16) for _ in range(3) for d in DILATIONS],
        in_specs=[row, pl.BlockSpec((1, dm), lambda i: (0, 0)),
                  pl.BlockSpec((dm, n), lambda i: (0, 0), pipeline_mode=_resident(True)), vec, vec,
                  pl.BlockSpec((D_ATTN, D_ATTN), lambda i: (0, 0), pipeline_mode=_resident(True))],
        out_specs=[row, pl.BlockSpec((tm, n), lambda i: (i, 0))]
        + [pl.BlockSpec((PAIRS, d, tm // d, BAND), lambda i: (0, 0, i, 0))
           for _ in range(3) for d in DILATIONS],
        scratch_shapes=[pltpu.VMEM((PAIRS, tm, BAND), F32)],
        compiler_params=_params("parallel"),
    )(x, g, w, gq, gk, ones_bd)


TOK = 2048
UNITS = TOK // BAND


def _stack_masks():
    row = lax.broadcasted_iota(jnp.int32, (2 * BAND, 2 * BAND), 0) & (BAND - 1)
    col = lax.broadcasted_iota(jnp.int32, (2 * BAND, 2 * BAND), 1)
    lane = lax.broadcasted_iota(jnp.int32, (BAND, BAND), 1)
    head0 = lane < HEAD_DIM
    ones = [jnp.where(head0, 1.0, 0.0).astype(BF16), jnp.where(head0, 0.0, 1.0).astype(BF16)]
    return col - row, col, head0, ones


def _split3(x):
    hi = x.astype(BF16).astype(F32)
    mid = (x - hi).astype(BF16).astype(F32)
    return hi, mid, x - hi - mid


def _gather(srcs, dst, d):
    per = TOK // d
    at = 0
    for r in range(d):
        for src in srcs:
            rows = src[pl.ds(r, per, stride=d), :] if d > 1 else src[...]
            dst[pl.ds(at, per), :] = rows.astype(dst.dtype)
            at += per


def _scatter_add(out_ref, src, d, per_src, offset, first):
    per = TOK // d
    if d == 1:
        val = src[pl.ds(offset, per), :]
        out_ref[...] = val if first else out_ref[...] + val
        return
    for r in range(d):
        val = src[pl.ds(r * per_src + offset, per), :]
        idx = pl.ds(r, per, stride=d)
        out_ref[idx, :] = val if first else out_ref[idx, :] + val


def _dilated_specs(nblk, reverse):
    def at(s):
        return (nblk - 1 - s) if reverse else s
    main = [pl.BlockSpec((1, d, TOK // d, BAND), lambda j, s: (j, 0, at(s), 0)) for d in DILATIONS]
    prev = [pl.BlockSpec((1, d, TOK // d, BAND), lambda j, s: (j, 0, jnp.maximum(at(s) - 1, 0), 0))
            for d in DILATIONS]
    return main, prev


def _window_rows(prev_ref, main_ref, dst, d):
    per = TOK // d
    for r in range(d):
        dst[pl.ds(r * (per + BAND), BAND), :] = prev_ref[0, r, pl.ds(per - BAND, BAND), :]
        dst[pl.ds(r * (per + BAND) + BAND, per), :] = main_ref[0, r]


def _attn_fwd(qs, ks, vs, carry=None):
    t = qs[0].shape[2]
    nblk = t // TOK
    nd = len(DILATIONS)

    def body(*refs):
        q_refs, kp_refs, k_refs = refs[:nd], refs[nd:2 * nd], refs[2 * nd:3 * nd]
        vp_refs, v_refs = refs[3 * nd:4 * nd], refs[4 * nd:5 * nd]
        y_ref, l_ref, kw_s, vw_s, ob, lb, on, ln = refs[5 * nd:]
        i = pl.program_id(1)
        diff, col, head0, hm = _stack_masks()
        band_ok = jnp.logical_and(diff >= 0, diff <= BAND)
        for g, d in enumerate(DILATIONS):
            per = TOK // d
            nb = per // BAND
            pad = per + BAND
            _window_rows(kp_refs[g], k_refs[g], kw_s, d)
            _window_rows(vp_refs[g], v_refs[g], vw_s, d)
            q_ref = q_refs[g]

            def unit(u, carry):
                r, b = u // nb, u % nb
                qu = q_ref[0, r, pl.ds(pl.multiple_of(b * BAND, BAND), BAND), :]
                start = pl.multiple_of(r * pad + b * BAND, BAND)
                kw = kw_s[pl.ds(start, 2 * BAND), :]
                vw = vw_s[pl.ds(start, 2 * BAND), :]
                lo = jnp.where(jnp.logical_and(i == 0, b == 0), BAND, 0)
                s = _dot_nt(jnp.concatenate([qu * hm[0], qu * hm[1]], axis=0), kw)
                s = jnp.where(jnp.logical_and(band_ok, col >= lo), s, NEG)
                mx = jnp.max(s, axis=-1, keepdims=True)
                e = jnp.exp(s - mx)
                den = jnp.sum(e, axis=-1, keepdims=True)
                o2 = _dot(e.astype(BF16), vw) / den
                l2 = jnp.broadcast_to(mx + jnp.log(den), (2 * BAND, BAND))
                rows = pl.ds(pl.multiple_of(u * BAND, BAND), BAND)
                ob[rows, :] = jnp.where(head0, o2[:BAND], o2[BAND:])
                lb[rows, :] = jnp.where(head0, l2[:BAND], l2[BAND:])
                return carry

            lax.fori_loop(0, UNITS, unit, 0, unroll=16)
            _scatter_add(on.at[g], ob, d, per, 0, True)
            _scatter_add(ln.at[g], lb, d, per, 0, True)
        ls = [ln[0], ln[1], ln[2]]
        mx = jnp.maximum(jnp.maximum(ls[0], ls[1]), ls[2])
        es = [jnp.exp(l - mx) for l in ls]
        tot = es[0] + es[1] + es[2]
        y_ref[...] = (es[0] * on[0] + es[1] * on[1] + es[2] * on[2]) / tot
        l_ref[...] = mx + jnp.log(tot)

    main, prev = _dilated_specs(nblk, False)
    out = pl.BlockSpec((TOK, BAND), lambda j, i: (i, j))
    win_rows = max(d * (TOK // d + BAND) for d in DILATIONS)
    return _call(
        body, list(qs) + list(ks) + list(ks) + list(vs) + list(vs), name="attn_fwd",
        grid=(PAIRS, nblk), out_shape=[_sds((t, D_ATTN), F32)] * 2,
        in_specs=main + prev + main + prev + main, out_specs=[out, out],
        scratch_shapes=[pltpu.VMEM((win_rows, BAND), BF16)] * 2 + [pltpu.VMEM((TOK, BAND), F32)] * 2
        + [pltpu.VMEM((nd, TOK, BAND), F32)] * 2,
        semantics=("parallel", "parallel"), carry=carry)


def _attn_bwd(qs, ks, vs, do, lse, dd, carry=None):
    t = qs[0].shape[2]
    nblk = t // TOK
    nd = len(DILATIONS)
    offs = [sum(DILATIONS[:g]) * BAND for g in range(nd)]

    def body(*refs):
        q_refs, kp_refs, k_refs = refs[:nd], refs[nd:2 * nd], refs[2 * nd:3 * nd]
        vp_refs, v_refs = refs[3 * nd:4 * nd], refs[4 * nd:5 * nd]
        (do_ref, l_ref, d_ref, dq_ref, dk_ref, dv_ref, kw_s, vw_s, dos, lds, pn, dqb, dkb, dvb, ckb,
         cvb) = refs[5 * nd:]
        step = pl.program_id(1)
        i = nblk - 1 - step
        key = lax.broadcasted_iota(jnp.int32, (2 * BAND, 2 * BAND), 0)
        qry = lax.broadcasted_iota(jnp.int32, (2 * BAND, 2 * BAND), 1) & (BAND - 1)
        off = key - qry
        band_ok = jnp.logical_and(off >= 0, off <= BAND)
        lane = lax.broadcasted_iota(jnp.int32, (BAND, BAND), 1)
        head0 = lane < HEAD_DIM
        hm = [jnp.where(head0, 1.0, 0.0).astype(BF16), jnp.where(head0, 0.0, 1.0).astype(BF16)]
        lane2 = lax.broadcasted_iota(jnp.int32, (2 * BAND, BAND), 1) & (HEAD_DIM - 1)
        ones_l = jnp.where(lane2 < 3, 1.0, 0.0).astype(BF16)
        ones_d = jnp.where(jnp.logical_and(lane2 >= 3, lane2 < 6), 1.0, 0.0).astype(BF16)
        piece = lax.broadcasted_iota(jnp.int32, (TOK, BAND), 1) & (HEAD_DIM - 1)

        def pieces(x, at):
            hi, mid, lo = _split3(-x)
            return jnp.where(piece == at, hi,
                             jnp.where(piece == at + 1, mid, jnp.where(piece == at + 2, lo, 0.0)))

        pn[...] = pieces(l_ref[...], 0) + pieces(d_ref[...], 3)
        for g, d in enumerate(DILATIONS):
            per = TOK // d
            nb = per // BAND
            pad = per + BAND
            _window_rows(kp_refs[g], k_refs[g], kw_s, d)
            _window_rows(vp_refs[g], v_refs[g], vw_s, d)
            _gather([do_ref], dos, d)
            _gather([pn], lds, d)
            dkb[...] = jnp.zeros_like(dkb)
            dvb[...] = jnp.zeros_like(dvb)
            q_ref = q_refs[g]

            def unit(u, c_):
                r, b = u // nb, u % nb
                rows = pl.ds(pl.multiple_of(u * BAND, BAND), BAND)
                qu = q_ref[0, r, pl.ds(pl.multiple_of(b * BAND, BAND), BAND), :]
                dou, ldu = dos[rows, :], lds[rows, :]
                q2 = jnp.concatenate([qu * hm[0], qu * hm[1]], axis=0)
                do2 = jnp.concatenate([dou * hm[0], dou * hm[1]], axis=0)
                ld2 = jnp.concatenate([ldu * hm[0], ldu * hm[1]], axis=0)
                acc = pl.ds(pl.multiple_of(r * pad + b * BAND, BAND), 2 * BAND)
                kw = kw_s[acc, :]
                vw = vw_s[acc, :]
                lo = jnp.where(jnp.logical_and(i == 0, b == 0), BAND, 0)
                ok = jnp.logical_and(band_ok, key >= lo)
                st = _dot_nt(jnp.concatenate([kw, ones_l], axis=1), jnp.concatenate([q2, ld2], axis=1))
                dpt = _dot_nt(jnp.concatenate([vw, ones_d], axis=1), jnp.concatenate([do2, ld2], axis=1))
                pt = jnp.where(ok, jnp.exp(st), 0.0)
                dst = (pt * dpt).astype(BF16)
                dkb[acc, :] += _dot(dst, q2)
                dvb[acc, :] += _dot(pt.astype(BF16), do2)
                dq2 = _dot_tn(dst, kw)
                dqb[rows, :] = jnp.where(head0, dq2[:BAND], dq2[BAND:])
                return c_

            lax.fori_loop(0, UNITS, unit, 0, unroll=16)

            for r in range(d):
                last = pl.ds(r * pad + per, BAND)
                kept = pl.ds(offs[g] + r * BAND, BAND)

                @pl.when(step > 0)
                def _():
                    dkb[last, :] += ckb[kept, :]
                    dvb[last, :] += cvb[kept, :]

                ckb[kept, :] = dkb[pl.ds(r * pad, BAND), :]
                cvb[kept, :] = dvb[pl.ds(r * pad, BAND), :]
            _scatter_add(dq_ref, dqb, d, per, 0, g == 0)
            _scatter_add(dk_ref, dkb, d, pad, BAND, g == 0)
            _scatter_add(dv_ref, dvb, d, pad, BAND, g == 0)

    main, prev = _dilated_specs(nblk, True)
    tok = pl.BlockSpec((TOK, BAND), lambda j, s: (nblk - 1 - s, j))
    acc_rows = max(d * (TOK // d + BAND) for d in DILATIONS)
    kept_rows = sum(DILATIONS) * BAND
    return _call(
        body, list(qs) + list(ks) + list(ks) + list(vs) + list(vs) + [do, lse, dd], name="attn_bwd",
        grid=(PAIRS, nblk), out_shape=[_sds((t, D_ATTN), F32)] * 3,
        in_specs=main + prev + main + prev + main + [tok] * 3, out_specs=[tok] * 3,
        scratch_shapes=[pltpu.VMEM((acc_rows, BAND), BF16)] * 2 + [pltpu.VMEM((TOK, BAND), BF16)] * 2
        + [pltpu.VMEM((TOK, BAND), F32)] * 2 + [pltpu.VMEM((acc_rows, BAND), F32)] * 2
        + [pltpu.VMEM((kept_rows, BAND), F32)] * 2,
        semantics=("parallel", "arbitrary"), carry=carry)


def _halo_rows(tm, t):
    per = tm // 8
    prev = lambda i: (jnp.maximum(i * per - 1, 0), 0)
    nxt = lambda i: (jnp.minimum((i + 1) * per, t // 8 - 1), 0)
    return prev, nxt


def _mixer_out(z, cw, y_attn, g_conv, g_attn, tm, carry=None):
    t = z.shape[0]
    prev, _ = _halo_rows(tm, t)

    def body(z_ref, zp_ref, cw_ref, y_ref, gc_ref, ga_ref, mix_ref):
        i = pl.program_id(0)
        keep = jnp.where(i > 0, 1.0, 0.0)
        cu = jnp.concatenate([zp_ref[:, 0:512] * zp_ref[:, 1024:1536] * keep,
                              z_ref[:, 0:512] * z_ref[:, 1024:1536]], axis=0)
        c = (cw_ref[0:1, :] * pltpu.roll(cu, 2, 0) + cw_ref[1:2, :] * pltpu.roll(cu, 1, 0)
             + cw_ref[2:3, :] * cu)[8:, :]
        yc = z_ref[:, 512:1024] * c
        mix_ref[:, 0:512] = (yc * _rms_scale(yc) * gc_ref[...]).astype(BF16)
        ya = y_ref[...]
        mix_ref[:, 512:1024] = (ya * _rms_scale(ya) * ga_ref[...]).astype(BF16)

    blk = pl.BlockSpec((tm, 512), lambda i: (i, 0))
    vec = pl.BlockSpec((1, 512), lambda i: (0, 0))
    return _call(
        body, [z, z, cw, y_attn, g_conv, g_attn], name="mixer_out", grid=(t // tm,),
        out_shape=_sds((t, 1024), BF16),
        in_specs=[pl.BlockSpec((tm, 1536), lambda i: (i, 0)), pl.BlockSpec((8, 1536), prev),
                  pl.BlockSpec((8, 512), lambda i: (0, 0)), blk, vec, vec],
        out_specs=pl.BlockSpec((tm, 1024), lambda i: (i, 0)),
        semantics=("parallel",), carry=carry)


def _mixer_bwd(z, dx1, wout, y_attn, cw, g_conv, g_attn, ones_bd, tm, carry=None):
    t = z.shape[0]
    nblk = t // tm
    prev, nxt = _halo_rows(tm, t)
    e = tm + 16

    def body(z_ref, zp_ref, zn_ref, dx_ref, dxn_ref, w_ref, y_ref, cw_ref, gc_ref, ga_ref, bd_ref,
             dz_ref, do_ref, dd_ref, dcw_ref, dgc_ref, dga_ref):
        i = pl.program_id(0)
        dm = _dot_nt(dx_ref[...], w_ref[...])
        dmn = _dot_nt(dxn_ref[...], w_ref[0:D_CONV, :])[0:8, :]
        rows = lax.broadcasted_iota(jnp.int32, (e, 1), 0)
        lo = jnp.where(i > 0, 0, 8)
        hi = jnp.where(i < nblk - 1, e, tm + 8)
        ze = jnp.concatenate([zp_ref[...], z_ref[...], zn_ref[...]], axis=0)
        u, gb, gcv = ze[:, 0:512], ze[:, 512:1024], ze[:, 1024:1536]
        w0, w1, w2 = cw_ref[0:1, :], cw_ref[1:2, :], cw_ref[2:3, :]
        cu = jnp.where(rows >= lo, gcv * u, 0.0)
        cu1, cu2 = pltpu.roll(cu, 1, 0), pltpu.roll(cu, 2, 0)
        c = w0 * cu2 + w1 * cu1 + w2 * cu
        yc = gb * c
        dma = jnp.concatenate([jnp.zeros((8, 512), F32), dm[:, 0:512], dmn], axis=0)
        dyc, ych = _rms_bwd(yc, _rms_scale(yc), gc_ref[...], dma)
        dc = jnp.where(jnp.logical_and(rows >= 8, rows < hi), dyc * gb, 0.0)
        dcu = w0 * pltpu.roll(dc, e - 2, 0) + w1 * pltpu.roll(dc, e - 1, 0) + w2 * dc
        mid = slice(8, 8 + tm)
        dz_ref[:, 0:512] = (dcu * gcv)[mid, :].astype(BF16)
        dz_ref[:, 512:1024] = (dyc * c)[mid, :].astype(BF16)
        dz_ref[:, 1024:1536] = (dcu * u)[mid, :].astype(BF16)

        ya = y_ref[...]
        dmb = dm[:, 512:1024]
        dya, yah = _rms_bwd(ya, _rms_scale(ya), ga_ref[...], dmb)
        do_ref[...] = dya
        dd_ref[...] = _head_sum(dya * ya, bd_ref[...])

        @pl.when(i == 0)
        def _():
            dcw_ref[...] = jnp.zeros_like(dcw_ref)
            dgc_ref[...] = jnp.zeros_like(dgc_ref)
            dga_ref[...] = jnp.zeros_like(dga_ref)

        dcm = jnp.where(rows < tm + 8, dc, 0.0)
        dcw_ref[0:1, :] += jnp.sum(dcm * cu2, axis=0, keepdims=True)
        dcw_ref[1:2, :] += jnp.sum(dcm * cu1, axis=0, keepdims=True)
        dcw_ref[2:3, :] += jnp.sum(dcm * cu, axis=0, keepdims=True)
        dgc_ref[...] += jnp.sum((dma * ych)[mid, :], axis=0, keepdims=True)
        dga_ref[...] += jnp.sum(dmb * yah, axis=0, keepdims=True)

    blk = pl.BlockSpec((tm, 512), lambda i: (i, 0))
    vec = pl.BlockSpec((1, 512), lambda i: (0, 0))
    cwb = pl.BlockSpec((8, 512), lambda i: (0, 0))
    next16 = lambda i: (jnp.minimum((i + 1) * (tm // 16), t // 16 - 1), 0)
    return _call(
        body, [z, z, z, dx1, dx1, wout, y_attn, cw, g_conv, g_attn, ones_bd], name="mixer_bwd",
        grid=(nblk,),
        out_shape=[_sds((t, 1536), BF16), _sds((t, 512), F32), _sds((t, 512), F32),
                   _sds((8, 512), F32), _sds((1, 512), F32), _sds((1, 512), F32)],
        in_specs=[pl.BlockSpec((tm, 1536), lambda i: (i, 0)), pl.BlockSpec((8, 1536), prev),
                  pl.BlockSpec((8, 1536), nxt), pl.BlockSpec((tm, D_MODEL), lambda i: (i, 0)),
                  pl.BlockSpec((16, D_MODEL), next16),
                  pl.BlockSpec(wout.shape, lambda i: (0, 0), pipeline_mode=_resident(True)),
                  blk, cwb, vec, vec, pl.BlockSpec((512, 512), lambda i: (0, 0))],
        out_specs=[pl.BlockSpec((tm, 1536), lambda i: (i, 0)), blk, blk, cwb, vec, vec],
        carry=carry)


def _qkv_bwd(z, dzc, dqn, dkn, dv, gq, gk, ones_bd, tm, carry=None):
    t = z.shape[0]

    def body(zq_ref, zk_ref, dzc_ref, dqn_ref, dkn_ref, dv_ref, gq_ref, gk_ref, bd_ref,
             dz_ref, dgq_ref, dgk_ref):
        bd = bd_ref[...]

        @pl.when(pl.program_id(0) == 0)
        def _():
            dgq_ref[...] = jnp.zeros_like(dgq_ref)
            dgk_ref[...] = jnp.zeros_like(dgk_ref)

        def back(v, dn, g, scale):
            r = _head_rms_scale(v, bd)
            vh = v * r
            dh = dn * (g * scale)
            dv = r * (dh - vh * (_head_sum(dh * vh, bd) * (1.0 / HEAD_DIM)))
            return dv, jnp.sum(dn * scale * vh, axis=0, keepdims=True)

        dq, dgq = back(zq_ref[...], dqn_ref[...], gq_ref[...], HEAD_DIM ** -0.5)
        dk, dgk = back(zk_ref[...], dkn_ref[...], gk_ref[...], 1.0)
        dgq_ref[...] += dgq
        dgk_ref[...] += dgk
        dz_ref[:, 0:1536] = dzc_ref[...]
        dz_ref[:, 1536:2048] = dq.astype(BF16)
        dz_ref[:, 2048:2560] = dk.astype(BF16)
        dz_ref[:, 2560:3072] = dv_ref[...].astype(BF16)

    blk = pl.BlockSpec((tm, 512), lambda i: (i, 0))
    vec = pl.BlockSpec((1, 512), lambda i: (0, 0))
    return _call(
        body, [z, z, dzc, dqn, dkn, dv, gq, gk, ones_bd], name="qkv_bwd", grid=(t // tm,),
        out_shape=[_sds((t, D_IN), BF16), _sds((1, 512), F32), _sds((1, 512), F32)],
        in_specs=[pl.BlockSpec((tm, 512), lambda i: (i, 3)), pl.BlockSpec((tm, 512), lambda i: (i, 4)),
                  pl.BlockSpec((tm, 1536), lambda i: (i, 0))] + [blk] * 3
        + [vec, vec, pl.BlockSpec((512, 512), lambda i: (0, 0))],
        out_specs=[pl.BlockSpec((tm, D_IN), lambda i: (i, 0)), vec, vec],
        carry=carry)


def _columns_from_chips(g):
    return g.transpose(1, 0, 2).reshape(g.shape[1], N_CHIPS * g.shape[2])


def kernel(x, g_mix, w_in, conv_w, g_q, g_k, g_conv_out, g_attn_out, w_out, g_ffn, w_gate, w_up, w_down, loss_target, m_g_mix, m_w_in, m_conv_w, m_g_q, m_g_k, m_g_conv_out, m_g_attn_out, m_w_out, m_g_ffn, m_w_gate, m_w_up, m_w_down, v_g_mix, v_w_in, v_conv_w, v_g_q, v_g_k, v_g_conv_out, v_g_attn_out, v_w_out, v_g_ffn, v_w_gate, v_w_up, v_w_down):
    t = x.shape[1]
    xs = x[0]
    target = loss_target[0]
    tm = min(512, t)
    tmm = min(1024, t)

    cw_pad = jnp.pad(conv_w[0], ((0, 13), (0, 0)))
    gathered = _all_gather([w_in[0].astype(BF16), cw_pad])
    win = _columns_from_chips(gathered[0])
    cw = jnp.pad(gathered[1][:, 0:3, :].transpose(1, 0, 2).reshape(3, D_CONV), ((0, 5), (0, 0)))
    later = [w_out[0].astype(BF16), w_gate[0].T.astype(BF16), w_up[0].T.astype(BF16),
             w_down[0].astype(BF16)]

    head_id = jnp.arange(D_ATTN) // HEAD_DIM
    ones_bd = (head_id[:, None] == head_id[None, :]).astype(BF16)
    gq_t = jnp.tile(g_q, (1, D_ATTN // HEAD_DIM))
    gk_t = jnp.tile(g_k, (1, D_ATTN // HEAD_DIM))

    h1, z, *dilated = _in_proj(xs, g_mix, win, gq_t, gk_t, ones_bd, tm)
    nd = len(DILATIONS)
    qs, ks, vs = dilated[:nd], dilated[nd:2 * nd], dilated[2 * nd:]
    (y_attn, lse), gathered = _attn_fwd(qs, ks, vs, carry=_x_gather_chips(later))
    mix, gathered = _mixer_out(z, cw, y_attn, g_conv_out, g_attn_out, tm,
                               carry=_x_gather_sibling(gathered))
    wout = gathered[0].reshape(D_MODEL, D_MODEL)
    wgate_t = gathered[1].reshape(D_FF, D_MODEL)
    wup_t = gathered[2].reshape(D_FF, D_MODEL)
    wdown = gathered[3].reshape(D_FF, D_MODEL)
    (x1,) = _matmul("out_proj", mix, wout, [xs], [F32], lambda acc, r: (r + acc,), tm, D_MODEL)
    h2, gate, up, act = _norm_matmul("ffn_up", x1, g_ffn, [wgate_t, wup_t], tm, D_FF, True, BF16,
                                     transposed_w=True)

    def loss_epilogue(acc, r, tgt):
        err = r + acc - tgt
        dy = err * (1.0 / D_MODEL)
        return dy, dy, jnp.sum(err * err)

    dx2, dx2b, loss_sum = _matmul("ffn_down_loss", act, wdown, [x1, target], [F32, BF16],
                                  loss_epilogue, tm, D_MODEL, loss=True)

    def swiglu_bwd(da, gt, u):
        gt, u = gt.astype(F32), u.astype(F32)
        s = _sigmoid(gt)
        return da * u * (s * (1.0 + gt * (1.0 - s))), da * (gt * s)

    dgate, dup = _matmul("ffn_down_bwd", dx2b, wdown, [gate, up], [BF16, BF16], swiglu_bwd,
                         tm, D_FF, transposed_w=True)
    gw_down = _matmul_tn("grad_w_down", act, dx2b, 512, tmm)
    gw_gate_t = _matmul_tn("grad_w_gate", dgate, h2, 512, tmm)
    gw_up_t = _matmul_tn("grad_w_up", dup, h2, 512, tmm)

    me = 2 * lax.axis_index("x") + lax.axis_index("y")
    where = jnp.stack([lax.axis_index("c"), me]).astype(jnp.int32)

    def pair_sums(names, full, got):
        return [_pair_sum(f"pair_sum_{nme}", a, b, where) for nme, a, b in zip(names, full, got)]

    def chip_sums(names, pair, got):
        return [_chip_sum(f"chip_sum_{nme}", own, b) for nme, (_, own), b in zip(names, pair, got)]

    ffn = ["w_gate", "w_up", "w_down"]
    full = [g.reshape(N_CHIPS, D_FF // N_CHIPS, D_MODEL) for g in (gw_gate_t, gw_up_t, gw_down)]
    (dx1, dx1b, gg_ffn), got = _matmul_norm_bwd(
        "ffn_up_bwd", [(dgate, wgate_t), (dup, wup_t)], x1, dx2, g_ffn, tm, carry=_x_pair(full),
        transposed_w=False)
    pair = pair_sums(ffn, full, got)
    gw_out = _matmul_tn("grad_w_out", mix, dx1b, 512, tmm)
    full = [gw_out.reshape(N_CHIPS, D_MODEL // N_CHIPS, D_MODEL)]
    (dzc, do, dd, gcw, gg_conv, gg_attn), got = _mixer_bwd(
        z, dx1b, wout, y_attn, cw, g_conv_out, g_attn_out, ones_bd, tm, carry=_x_pair(full))
    pair += pair_sums(["w_out"], full, got)
    early = ffn + ["w_out"]
    (dqn, dkn, dv), got = _attn_bwd(qs, ks, vs, do, lse, dd, carry=_x_chips([p for p, _ in pair]))
    mine = chip_sums(early, pair, got)
    (dz, gg_q, gg_k), theirs = _qkv_bwd(z, dzc, dqn, dkn, dv, gq_t, gk_t, ones_bd, tm,
                                        carry=_x_share(mine))
    full = [_matmul_tn("grad_w_in", h1, dz, D_IN // N_CHIPS, tmm, by_chip=True)]
    grad_x, _, gg_mix = _matmul_norm_bwd("in_proj_bwd", [(dz, win)], xs, dx1, g_mix, tm)
    got = _exchange_alone("grad_pair_exchange_w_in", _x_pair(full))
    pair = pair_sums(["w_in"], full, got)
    got = _exchange_alone("grad_chip_exchange_w_in", _x_chips([pair[0][0]]))
    mine += chip_sums(["w_in"], pair, got)
    theirs = list(theirs) + list(_exchange_alone("grad_pair_share_w_in", _x_share(mine[-1:])))
    big = early + ["w_in"]

    small = _small_all_reduce({
        "g_mix": gg_mix, "g_ffn": gg_ffn, "g_conv_out": gg_conv, "g_attn_out": gg_attn,
        "g_q": gg_q, "g_k": gg_k, "loss": loss_sum, "conv_w": gcw})
    heads = D_ATTN // HEAD_DIM
    grads = {
        "g_mix": small[0:1, :], "g_ffn": small[1:2, :],
        "g_conv_out": small[2:3, 0:512], "g_attn_out": small[2:3, 512:1024],
        "g_q": small[3, 0:512].reshape(heads, HEAD_DIM).sum(axis=0)[None, :],
        "g_k": small[3, 512:1024].reshape(heads, HEAD_DIM).sum(axis=0)[None, :],
        "conv_w": lax.dynamic_slice(small[8:11, 0:512], (0, me * (D_CONV // N_CHIPS)),
                                    (3, D_CONV // N_CHIPS)),
    }
    halves = dict(zip(big, zip(mine, theirs)))
    loss = small[4, 0] * 0.5 * (1.0 / D_MODEL)

    weights = dict(g_mix=g_mix, w_in=w_in, conv_w=conv_w, g_q=g_q, g_k=g_k, g_conv_out=g_conv_out,
                   g_attn_out=g_attn_out, w_out=w_out, g_ffn=g_ffn, w_gate=w_gate, w_up=w_up,
                   w_down=w_down)
    moments_m = dict(g_mix=m_g_mix, w_in=m_w_in, conv_w=m_conv_w, g_q=m_g_q, g_k=m_g_k,
                     g_conv_out=m_g_conv_out, g_attn_out=m_g_attn_out, w_out=m_w_out, g_ffn=m_g_ffn,
                     w_gate=m_w_gate, w_up=m_w_up, w_down=m_w_down)
    moments_v = dict(g_mix=v_g_mix, w_in=v_w_in, conv_w=v_conv_w, g_q=v_g_q, g_k=v_g_k,
                     g_conv_out=v_g_conv_out, g_attn_out=v_g_attn_out, w_out=v_w_out, g_ffn=v_g_ffn,
                     w_gate=v_w_gate, w_up=v_w_up, w_down=v_w_down)
    names = list(weights)
    out_g, out_d, out_m, out_v = [], [], [], []
    for nme in names:
        wgt = weights[nme]
        shape2 = wgt.shape[-2:] if wgt.ndim == 3 else wgt.shape
        flip = nme in ("w_gate", "w_up")

        def to2d(a):
            return a.reshape(shape2).T if flip else a.reshape(shape2)

        def back(a):
            return (a.T if flip else a).reshape(wgt.shape)

        state = (to2d(wgt), to2d(moments_m[nme]), to2d(moments_v[nme]))
        if nme in halves:
            g2, dlt, nm, nv = _adamw_shard(f"adamw_{nme}", *state, *halves[nme], where)
        else:
            g2 = grads[nme].reshape(shape2)
            dlt, nm, nv = _adamw(f"adamw_{nme}", state[0], g2, state[1], state[2])
        out_g.append(back(g2))
        out_d.append(back(dlt))
        out_m.append(back(nm))
        out_v.append(back(nv))
    return (loss, grad_x[None], *out_g, *out_d, *out_m, *out_v)
```

```python
import functools
from typing import Any, Callable, NamedTuple, Sequence

import jax
import jax.numpy as jnp
from jax import lax
from jax.experimental import pallas as pl
from jax.experimental.pallas import tpu as pltpu

F32 = jnp.float32
BF16 = jnp.bfloat16
MESH = pl.DeviceIdType.MESH

D_MODEL = 1024
D_CONV = 512
D_ATTN = 512
HEAD_DIM = 64
D_FF = 2816
D_IN = 3 * D_CONV + 3 * D_ATTN
DILATIONS = (1, 4, 16)
BAND = 128
EPS = 1e-6
NEG = -1e30
N_CHIPS = 4

ADAM_LR = 0.001
ADAM_B1 = 0.9
ADAM_B2 = 0.999
ADAM_EPS = 1e-08
ADAM_WD = 0.01
ADAM_STEP = 10

V7X_VMEM_BYTES = 64 * 1024 * 1024
VMEM_LIMIT = V7X_VMEM_BYTES - 8 * 1024 * 1024
ANY = pl.BlockSpec(memory_space=pl.ANY)
VMEM_WHOLE = pl.BlockSpec(memory_space=pltpu.VMEM)


def _params(*sem):
    return pltpu.CompilerParams(dimension_semantics=sem, vmem_limit_bytes=VMEM_LIMIT)


def _sds(shape, dtype):
    return jax.ShapeDtypeStruct(shape, dtype)


def _resident(whole):
    return pl.Buffered(1) if whole else None


def _place():
    x, y, c = lax.axis_index("x"), lax.axis_index("y"), lax.axis_index("c")
    chips = [(1 - x, y), (x, 1 - y), (1 - x, 1 - y)]
    return x, y, c, 2 * x + y, chips, [2 * cx + cy for cx, cy in chips]


def _all_gather(shards):
    n = len(shards)

    def body(*refs):
        ins, outs, stage = refs[:n], refs[n:2 * n], refs[2 * n:3 * n]
        ssem, rsem, fsem, gsem, lsem, osem = refs[3 * n:]
        x, y, c, me, chips, cids = _place()
        sib = (x, y, 1 - c)

        def half(w, which):
            h = shards[w].shape[0] // 2
            return pl.ds(pl.multiple_of(which * h, 8), h)

        loads = [pltpu.make_async_copy(ins[w], stage[w], lsem.at[w]) for w in range(n)]
        local = [pltpu.make_async_copy(stage[w], outs[w].at[me], osem.at[w]) for w in range(n)]
        for cp in loads:
            cp.start()

        def chip_copy(w, j, src_slot):
            rows = half(w, c)
            return pltpu.make_async_remote_copy(
                src_ref=ins[w].at[rows], dst_ref=outs[w].at[src_slot, rows],
                send_sem=ssem.at[3 * w + j], recv_sem=rsem.at[3 * w + j],
                device_id=(*chips[j], c), device_id_type=MESH)

        def sib_copy(w, j, which):
            rows = half(w, which)
            return pltpu.make_async_remote_copy(
                src_ref=outs[w].at[cids[j], rows], dst_ref=outs[w].at[cids[j], rows],
                send_sem=fsem.at[3 * w + j], recv_sem=gsem.at[3 * w + j],
                device_id=sib, device_id_type=MESH)

        sends = [chip_copy(w, j, me) for w in range(n) for j in range(3)]
        for cp in sends:
            cp.start()
        for w in range(n):
            loads[w].wait()
            local[w].start()
        passed = []
        for w in range(n):
            for j in range(3):
                chip_copy(w, j, cids[j]).wait_recv()
                cp = sib_copy(w, j, c)
                cp.start()
                passed.append(cp)
        for w in range(n):
            for j in range(3):
                sib_copy(w, j, 1 - c).wait_recv()
        for cp in sends + passed:
            cp.wait_send()
        for cp in local:
            cp.wait()

    return pl.pallas_call(
        body, name="all_gather_weights",
        out_shape=[_sds((N_CHIPS,) + s.shape, s.dtype) for s in shards],
        in_specs=[ANY] * n, out_specs=[ANY] * n,
        scratch_shapes=[pltpu.VMEM(s.shape, s.dtype) for s in shards]
        + [pltpu.SemaphoreType.DMA((3 * n,))] * 4 + [pltpu.SemaphoreType.DMA((n,))] * 2,
        compiler_params=pltpu.CompilerParams(vmem_limit_bytes=VMEM_LIMIT),
    )(*shards)


class _Exchange(NamedTuple):
    srcs: Sequence[Any]
    lands: Sequence[Any]
    outs: Sequence[Any]
    n_sems: int
    copies: Callable


def _remote(src, dst, ssem, rsem, k, to):
    return pltpu.make_async_remote_copy(src_ref=src, dst_ref=dst, send_sem=ssem.at[k],
                                        recv_sem=rsem.at[k], device_id=to, device_id_type=MESH)


def _x_gather_chips(shards):
    def copies(srcs, lands, outs, ssem, rsem):
        _, _, c, me, chips, cids = _place()
        go, arrive = [], []
        for w, s in enumerate(shards):
            h = s.shape[0] // 2
            rows = pl.ds(pl.multiple_of(c * h, 8), h)
            for j in range(3):
                to = (*chips[j], c)
                go.append(_remote(srcs[w].at[rows], lands[w].at[me, rows], ssem, rsem, 3 * w + j, to))
                arrive.append(_remote(srcs[w].at[rows], lands[w].at[cids[j], rows], ssem, rsem,
                                      3 * w + j, to))
        return go, arrive

    lands = [jnp.broadcast_to(s[None], (N_CHIPS,) + s.shape) for s in shards]
    return _Exchange(shards, lands, [], 3 * len(shards), copies)


def _x_gather_sibling(gathered):
    def copies(srcs, lands, outs, ssem, rsem):
        x, y, c, _, _, cids = _place()
        go, arrive = [], []
        for w, g in enumerate(gathered):
            h = g.shape[1] // 2
            mine = pl.ds(pl.multiple_of(c * h, 8), h)
            theirs = pl.ds(pl.multiple_of((1 - c) * h, 8), h)
            for j in range(3):
                slab = lands[w].at[cids[j]]
                go.append(_remote(slab.at[mine], slab.at[mine], ssem, rsem, 3 * w + j, (x, y, 1 - c)))
                arrive.append(_remote(slab.at[theirs], slab.at[theirs], ssem, rsem, 3 * w + j,
                                      (x, y, 1 - c)))
        return go, arrive

    return _Exchange([], gathered, [], 3 * len(gathered), copies)


def _x_pair(grads):
    def copies(srcs, lands, outs, ssem, rsem):
        x, y, c, _, _, _ = _place()
        go = []
        for w, g in enumerate(grads):
            h = g.shape[1] // 2
            theirs = pl.ds(pl.multiple_of((1 - c) * h, 8), h)
            go.append(_remote(srcs[w].at[:, theirs, :], outs[w], ssem, rsem, w, (x, y, 1 - c)))
        return go, go

    outs = [_sds((N_CHIPS, g.shape[1] // 2, g.shape[2]), g.dtype) for g in grads]
    return _Exchange(grads, [], outs, len(grads), copies)


def _x_chips(parts):
    def copies(srcs, lands, outs, ssem, rsem):
        _, _, c, _, chips, cids = _place()
        go = [_remote(srcs[w].at[cids[j]], outs[w].at[j], ssem, rsem, 3 * w + j, (*chips[j], c))
              for w in range(len(parts)) for j in range(3)]
        return go, go

    outs = [_sds((3,) + p.shape[1:], p.dtype) for p in parts]
    return _Exchange(parts, [], outs, 3 * len(parts), copies)


def _x_share(halves):
    def copies(srcs, lands, outs, ssem, rsem):
        x, y, c, _, _, _ = _place()
        go = [_remote(srcs[w], outs[w], ssem, rsem, w, (x, y, 1 - c)) for w in range(len(halves))]
        return go, go

    return _Exchange(halves, [], [_sds(h.shape, h.dtype) for h in halves], len(halves), copies)


def _call(body, args, *, name, grid, in_specs, out_specs, out_shape, scratch_shapes=(),
          semantics=None, carry=None, aliases=None):
    single = not isinstance(out_shape, (list, tuple))
    out_shape = [out_shape] if single else list(out_shape)
    out_specs = [out_specs] if single else list(out_specs)
    aliases = dict(aliases or {})
    if carry is None:
        res = pl.pallas_call(
            body, name=name, grid=grid, in_specs=list(in_specs), out_specs=out_specs,
            out_shape=out_shape, scratch_shapes=list(scratch_shapes), input_output_aliases=aliases,
            compiler_params=_params(*(semantics or ("arbitrary",) * len(grid))))(*args)
        return res[0] if single else res
    n_in, n_out, n_scr = len(args), len(out_shape), len(scratch_shapes)
    n_src, n_land, n_new = len(carry.srcs), len(carry.lands), len(carry.outs)

    def carrying(*refs):
        at = 0
        parts = []
        for n in (n_in, n_src, n_land, n_out, n_land, n_new, n_scr, 2):
            parts.append(refs[at:at + n])
            at += n
        ins, srcs, _, outs, lands, news, scratch, (ssem, rsem) = parts
        ids = [pl.program_id(a) for a in range(len(grid))]
        first = functools.reduce(jnp.logical_and, [i == 0 for i in ids])
        last = functools.reduce(jnp.logical_and, [i == g - 1 for i, g in zip(ids, grid)])
        go, arrive = carry.copies(srcs, lands, news, ssem, rsem)

        @pl.when(first)
        def _():
            for cp in go:
                cp.start()

        body(*ins, *outs, *scratch)

        @pl.when(last)
        def _():
            for cp in go:
                cp.wait_send()
            for cp in arrive:
                cp.wait_recv()

    res = pl.pallas_call(
        carrying, name=name, grid=grid,
        in_specs=list(in_specs) + [ANY] * (n_src + n_land),
        out_specs=out_specs + [ANY] * (n_land + n_new),
        out_shape=out_shape + [_sds(a.shape, a.dtype) for a in carry.lands] + list(carry.outs),
        input_output_aliases={**aliases, **{n_in + n_src + i: n_out + i for i in range(n_land)}},
        scratch_shapes=list(scratch_shapes) + [pltpu.SemaphoreType.DMA((carry.n_sems,))] * 2,
        compiler_params=_params(*(("arbitrary",) * len(grid))))(*args, *carry.srcs, *carry.lands)
    own = res[:n_out]
    return (own[0] if single else own), res[n_out:]


def _exchange_alone(name, exchange):
    def body(x_ref, o_ref):
        o_ref[...] = x_ref[...]

    blk = pl.BlockSpec((8, 128), lambda i: (0, 0))
    _, res = _call(body, [jnp.zeros((8, 128), F32)], name=name, grid=(1,), in_specs=[blk],
                   out_specs=blk, out_shape=_sds((8, 128), F32), carry=exchange)
    return res


def _row_block(r, want):
    return max(d for d in range(1, min(want, r) + 1) if r % d == 0 and (d % 8 == 0 or d == r))


def _pair_sum(name, full, got, where):
    _, r, n = full.shape
    h = r // 2
    tr = _row_block(h, 256)
    nb = h // tr

    def body(w_ref, a_ref, b_ref, o_ref, own_ref):
        total = a_ref[...] + b_ref[...]
        o_ref[...] = total.astype(BF16)

        @pl.when(pl.program_id(1) == w_ref[1])
        def _():
            own_ref[...] = total[0]

    blk = pl.BlockSpec((1, tr, n), lambda i, s, w: (s, i, 0))
    return pl.pallas_call(
        body, name=name, out_shape=[_sds(got.shape, BF16), _sds((h, n), F32)],
        grid_spec=pltpu.PrefetchScalarGridSpec(
            num_scalar_prefetch=1, grid=(nb, N_CHIPS),
            in_specs=[pl.BlockSpec((1, tr, n), lambda i, s, w: (s, w[0] * nb + i, 0)), blk],
            out_specs=[blk, pl.BlockSpec((tr, n), lambda i, s, w: (i, 0))]),
        compiler_params=_params("parallel", "arbitrary"),
    )(where, full, got)


def _chip_sum(name, own, got):
    h, n = own.shape
    tr = _row_block(h, 256)

    def body(a_ref, b0, b1, b2, o_ref):
        o_ref[...] = ((a_ref[...] + b0[0].astype(F32)) + b1[0].astype(F32)) + b2[0].astype(F32)

    def slot(j):
        return pl.BlockSpec((1, tr, n), lambda i: (j, i, 0))

    blk = pl.BlockSpec((tr, n), lambda i: (i, 0))
    return pl.pallas_call(
        body, name=name, grid=(h // tr,), out_shape=_sds((h, n), F32),
        in_specs=[blk, slot(0), slot(1), slot(2)], out_specs=blk,
        compiler_params=_params("parallel"),
    )(own, got, got, got)


SMALL_ROWS = 16
SMALL_LAYOUT = (
    ("g_mix", 0, 0, 1, 1024), ("g_ffn", 1, 0, 1, 1024), ("g_conv_out", 2, 0, 1, 512),
    ("g_attn_out", 2, 512, 1, 512), ("g_q", 3, 0, 1, 512), ("g_k", 3, 512, 1, 512),
    ("loss", 4, 0, 1, 128), ("conv_w", 8, 0, 8, 512))


def _small_all_reduce(parts):
    names = [s[0] for s in SMALL_LAYOUT]

    def body(*refs):
        ins = refs[:len(names)]
        out_ref, stage, buf, ssem, rsem = refs[len(names):]
        x, y, c, _, _, _ = _place()
        me = 4 * x + 2 * y + c
        stage[...] = jnp.zeros_like(stage)
        for ref, (_, r0, c0, nr, nc) in zip(ins, SMALL_LAYOUT):
            stage[r0:r0 + nr, c0:c0 + nc] = ref[0:nr, :]
        buf[me] = stage[...]
        peers = []
        for d in range(1, 8):
            px = 1 - x if d & 4 else x
            py = 1 - y if d & 2 else y
            pc = 1 - c if d & 1 else c
            peers.append(((px, py, pc), 4 * px + 2 * py + pc))
        sends = [pltpu.make_async_remote_copy(
            src_ref=stage, dst_ref=buf.at[me], send_sem=ssem.at[k], recv_sem=rsem.at[k],
            device_id=peer, device_id_type=MESH) for k, (peer, _) in enumerate(peers)]
        for cp in sends:
            cp.start()
        for k, (peer, pid) in enumerate(peers):
            pltpu.make_async_remote_copy(
                src_ref=stage, dst_ref=buf.at[pid], send_sem=ssem.at[k], recv_sem=rsem.at[k],
                device_id=peer, device_id_type=MESH).wait_recv()
        for cp in sends:
            cp.wait_send()
        acc = buf[0]
        for k in range(1, 8):
            acc = acc + buf[k]
        out_ref[...] = acc

    return pl.pallas_call(
        body, name="small_all_reduce", out_shape=_sds((SMALL_ROWS, 1024), F32),
        in_specs=[VMEM_WHOLE] * len(names), out_specs=VMEM_WHOLE,
        scratch_shapes=[pltpu.VMEM((SMALL_ROWS, 1024), F32), pltpu.VMEM((8, SMALL_ROWS, 1024), F32),
                        pltpu.SemaphoreType.DMA((7,)), pltpu.SemaphoreType.DMA((7,))],
    )(*[parts[k] for k in names])


def _dot(a, b):
    return jnp.dot(a, b, preferred_element_type=F32)


def _dot_nt(a, b):
    return lax.dot_general(a, b, (((1,), (1,)), ((), ())), preferred_element_type=F32)


def _dot_tn(a, b):
    return lax.dot_general(a, b, (((0,), (0,)), ((), ())), preferred_element_type=F32)


def _sigmoid(v):
    return 1.0 / (1.0 + jnp.exp(-v))


def _rms_scale(v):
    return lax.rsqrt(jnp.mean(v * v, axis=-1, keepdims=True) + EPS)


def _rms_bwd(v, r, g, dy):
    vh = v * r
    dh = dy * g
    return r * (dh - vh * jnp.mean(dh * vh, axis=-1, keepdims=True)), vh


def _head_sum(a, ones_bd):
    hi = a.astype(BF16)
    lo = (a - hi.astype(F32)).astype(BF16)
    return _dot(hi, ones_bd) + _dot(lo, ones_bd)


def _head_rms_scale(v, ones_bd):
    return lax.rsqrt(_head_sum(v * v, ones_bd) * (1.0 / HEAD_DIM) + EPS)


MXU_COLUMNS = 256


def _column_chunks(n):
    width = MXU_COLUMNS if n % MXU_COLUMNS == 0 else n
    return [slice(c, c + width) for c in range(0, n, width)]


def _norm_matmul(name, x, g, ws, tm, tn, swiglu, out_dtype=F32, transposed_w=False):
    t, d = x.shape
    n = ws[0].shape[0] if transposed_w else ws[0].shape[1]
    nw = len(ws)

    def body(x_ref, g_ref, *refs):
        w_refs, h_ref, o_refs = refs[:nw], refs[nw], refs[nw + 1:2 * nw + 1]
        hs = refs[-1]

        @pl.when(pl.program_id(1) == 0)
        def _():
            xv = x_ref[...]
            h = (xv * _rms_scale(xv) * g_ref[...]).astype(BF16)
            hs[...] = h
            h_ref[...] = h

        h = hs[...]
        for cols in _column_chunks(tn):
            outs = [_dot_nt(h, w[cols, :]) if transposed_w else _dot(h, w[:, cols]) for w in w_refs]
            for o_ref, o in zip(o_refs, outs):
                o_ref[:, cols] = o.astype(out_dtype)
            if swiglu:
                refs[2 * nw + 1][:, cols] = (outs[0] * _sigmoid(outs[0]) * outs[1]).astype(BF16)

    row = pl.BlockSpec((tm, d), lambda i, j: (i, 0))
    col = pl.BlockSpec((tm, tn), lambda i, j: (i, j))
    out_shape = [_sds((t, d), BF16)] + [_sds((t, n), out_dtype)] * nw
    out_specs = [row] + [col] * nw
    if swiglu:
        out_shape.append(_sds((t, n), BF16))
        out_specs.append(col)
    return pl.pallas_call(
        body, name=name, grid=(t // tm, n // tn), out_shape=out_shape,
        in_specs=[row, pl.BlockSpec((1, d), lambda i, j: (0, 0))]
        + [pl.BlockSpec((tn, d), lambda i, j: (j, 0), pipeline_mode=_resident(tn == n))
           if transposed_w
           else pl.BlockSpec((d, tn), lambda i, j: (0, j), pipeline_mode=_resident(tn == n))] * nw,
        out_specs=out_specs, scratch_shapes=[pltpu.VMEM((tm, d), BF16)],
        compiler_params=_params("parallel", "arbitrary"),
    )(x, g, *ws)


def _matmul(name, a, w, extras, out_dtypes, epilogue, tm, tn, transposed_w=False, loss=False):
    t, k = a.shape
    n = w.shape[0] if transposed_w else w.shape[1]
    ne, no = len(extras), len(out_dtypes)

    def body(a_ref, w_ref, *refs):
        e_refs, o_refs = refs[:ne], refs[ne:]
        a = a_ref[...]
        total = 0.0
        for cols in _column_chunks(tn):
            acc = _dot_nt(a, w_ref[cols, :]) if transposed_w else _dot(a, w_ref[:, cols])
            res = epilogue(acc, *[e[:, cols] for e in e_refs])
            for o_ref, r in zip(o_refs[:no], res[:no]):
                o_ref[:, cols] = r.astype(o_ref.dtype)
            if loss:
                total = total + res[no]
        if loss:
            first = jnp.logical_and(pl.program_id(0) == 0, pl.program_id(1) == 0)

            @pl.when(first)
            def _():
                o_refs[no][...] = jnp.zeros_like(o_refs[no])

            o_refs[no][...] += total

    col = pl.BlockSpec((tm, tn), lambda i, j: (i, j))
    w_spec = (pl.BlockSpec((tn, k), lambda i, j: (j, 0), pipeline_mode=_resident(tn == n))
              if transposed_w
              else pl.BlockSpec((k, tn), lambda i, j: (0, j), pipeline_mode=_resident(tn == n)))
    out_shape = [_sds((t, n), dt) for dt in out_dtypes]
    out_specs = [col] * no
    if loss:
        out_shape.append(_sds((8, 128), F32))
        out_specs.append(pl.BlockSpec((8, 128), lambda i, j: (0, 0)))
    return pl.pallas_call(
        body, name=name, grid=(t // tm, n // tn), out_shape=out_shape,
        in_specs=[pl.BlockSpec((tm, k), lambda i, j: (i, 0)), w_spec] + [col] * ne,
        out_specs=out_specs,
        compiler_params=_params(*(("arbitrary", "arbitrary") if loss else ("parallel", "parallel"))),
    )(a, w, *extras)


def _matmul_norm_bwd(name, pairs, x, dres, g, tm, carry=None, transposed_w=True):
    t, d = x.shape
    npairs = len(pairs)
    product = _dot_nt if transposed_w else _dot

    def body(*refs):
        a_refs, w_refs = refs[:npairs], refs[npairs:2 * npairs]
        x_ref, r_ref, g_ref, dx_ref, dxb_ref, dg_ref = refs[2 * npairs:]
        dy = product(a_refs[0][...], w_refs[0][...])
        for a_ref, w_ref in zip(a_refs[1:], w_refs[1:]):
            dy = dy + product(a_ref[...], w_ref[...])
        xv = x_ref[...]
        dx, xh = _rms_bwd(xv, _rms_scale(xv), g_ref[...], dy)
        dx = dx + r_ref[...]
        dx_ref[...] = dx
        dxb_ref[...] = dx.astype(BF16)

        @pl.when(pl.program_id(0) == 0)
        def _():
            dg_ref[...] = jnp.zeros_like(dg_ref)

        dg_ref[...] += jnp.sum(dy * xh, axis=0, keepdims=True)

    row = pl.BlockSpec((tm, d), lambda i: (i, 0))
    vec = pl.BlockSpec((1, d), lambda i: (0, 0))
    return _call(
        body, [a for a, _ in pairs] + [w for _, w in pairs] + [x, dres, g], name=name,
        grid=(t // tm,), out_shape=[_sds((t, d), F32), _sds((t, d), BF16), _sds((1, d), F32)],
        in_specs=[pl.BlockSpec((tm, a.shape[1]), lambda i: (i, 0)) for a, _ in pairs]
        + [pl.BlockSpec(w.shape, lambda i: (0, 0), pipeline_mode=pl.Buffered(1)) for _, w in pairs]
        + [row, row, vec],
        out_specs=[row, row, vec], carry=carry)


def _matmul_tn(name, a, g, tn, tk, by_chip=False):
    t, ka = a.shape
    n = g.shape[1]

    def body(a_ref, g_ref, o_ref):
        @pl.when(pl.program_id(1) == 0)
        def _():
            o_ref[...] = jnp.zeros_like(o_ref)

        acc = _dot_tn(a_ref[...], g_ref[...])
        o_ref[...] += acc[None] if by_chip else acc

    return pl.pallas_call(
        body, name=name, grid=(n // tn, t // tk),
        out_shape=_sds((n // tn, ka, tn) if by_chip else (ka, n), F32),
        in_specs=[pl.BlockSpec((tk, ka), lambda j, s: (s, 0)),
                  pl.BlockSpec((tk, tn), lambda j, s: (s, j))],
        out_specs=(pl.BlockSpec((1, ka, tn), lambda j, s: (j, 0, 0)) if by_chip
                   else pl.BlockSpec((ka, tn), lambda j, s: (0, j))),
        compiler_params=_params("parallel", "arbitrary"),
    )(a, g)


def _elementwise(name, fn, ins, out_dtypes, tr):
    r, n = ins[0].shape
    tr = _row_block(r, tr)
    ni = len(ins)

    def body(*refs):
        res = fn(*[ref[...] for ref in refs[:ni]])
        for o_ref, v in zip(refs[ni:], res):
            o_ref[...] = v.astype(o_ref.dtype)

    blk = pl.BlockSpec((tr, n), lambda i: (i, 0))
    return pl.pallas_call(
        body, name=name, grid=(r // tr,), out_shape=[_sds((r, n), dt) for dt in out_dtypes],
        in_specs=[blk] * ni, out_specs=[blk] * len(out_dtypes),
        compiler_params=_params("parallel"),
    )(*ins)


def _adamw_update(w, g, m, v):
    m = ADAM_B1 * m + (1.0 - ADAM_B1) * g
    v = ADAM_B2 * v + (1.0 - ADAM_B2) * (g * g)
    m_hat = m / (1.0 - ADAM_B1 ** ADAM_STEP)
    v_hat = v / (1.0 - ADAM_B2 ** ADAM_STEP)
    return -ADAM_LR * (m_hat / (jnp.sqrt(v_hat) + ADAM_EPS) + ADAM_WD * w), m, v


def _adamw(name, w, g, m, v):
    return _elementwise(name, _adamw_update, [w, g, m, v], [F32] * 3, 256)


def _adamw_shard(name, w, m, v, mine, theirs, where):
    r, n = w.shape
    h = r // 2
    tr = _row_block(h, 256)
    nb = h // tr

    def body(w_ref, p_ref, m_ref, v_ref, a_ref, b_ref, g_ref, d_ref, nm_ref, nv_ref):
        g = jnp.where(pl.program_id(0) == w_ref[0], a_ref[...], b_ref[...])
        g_ref[...] = g
        d_ref[...], nm_ref[...], nv_ref[...] = _adamw_update(p_ref[...], g, m_ref[...], v_ref[...])

    whole = pl.BlockSpec((tr, n), lambda s, i, c: (s * nb + i, 0))
    used = pl.BlockSpec((tr, n), lambda s, i, c: (jnp.where(s == c[0], i, 0), 0))
    unused = pl.BlockSpec((tr, n), lambda s, i, c: (jnp.where(s == c[0], 0, i), 0))
    return pl.pallas_call(
        body, name=name, out_shape=[_sds((r, n), F32)] * 4,
        grid_spec=pltpu.PrefetchScalarGridSpec(
            num_scalar_prefetch=1, grid=(2, nb), in_specs=[whole] * 3 + [used, unused],
            out_specs=[whole] * 4),
        compiler_params=_params("arbitrary", "arbitrary"),
    )(where, w, m, v, mine, theirs)


PAIRS = D_ATTN // BAND


def _in_proj(x, g, w, gq, gk, ones_bd, tm):
    t, dm = x.shape
    n = w.shape[1]
    nd = len(DILATIONS)
    first = 3 * D_CONV

    def body(x_ref, g_ref, w_ref, gq_ref, gk_ref, bd_ref, h_ref, z_ref, *refs):
        outs, slab = refs[:3 * nd], refs[3 * nd]
        xv = x_ref[...]
        h = (xv * _rms_scale(xv) * g_ref[...]).astype(BF16)
        h_ref[...] = h
        for cols in _column_chunks(n):
            z_ref[:, cols] = _dot(h, w_ref[:, cols])
        bd = bd_ref[...]
        q = z_ref[:, first:first + D_ATTN]
        k = z_ref[:, first + D_ATTN:first + 2 * D_ATTN]
        vals = [(q * _head_rms_scale(q, bd) * gq_ref[...]) * HEAD_DIM ** -0.5,
                k * _head_rms_scale(k, bd) * gk_ref[...], z_ref[:, first + 2 * D_ATTN:n]]
        for m, val in enumerate(vals):
            for c in range(PAIRS):
                slab[c] = val[:, c * BAND:(c + 1) * BAND]
            for a, d in enumerate(DILATIONS):
                o_ref = outs[m * nd + a]
                for c in range(PAIRS):
                    for r in range(d):
                        rows = slab.at[c][pl.ds(r, tm // d, stride=d), :] if d > 1 else slab[c]
                        o_ref[c, r] = rows.astype(BF16)

    row = pl.BlockSpec((tm, dm), lambda i: (i, 0))
    vec = pl.BlockSpec((1, D_ATTN), lambda i: (0, 0))
    return pl.pallas_call(
        body, name="in_proj", grid=(t // tm,),
        out_shape=[_sds((t, dm), BF16), _sds((t, n), F32)]
        + [_sds((PAIRS, d, t // d, BAND), BF16) for _ in range(3) for d in DILATIONS],
        in_specs=[row, pl.BlockSpec((1, dm), lambda i: (0, 0)),
                  pl.BlockSpec((dm, n), lambda i: (0, 0), pipeline_mode=_resident(True)), vec, vec,
                  pl.BlockSpec((D_ATTN, D_ATTN), lambda i: (0, 0), pipeline_mode=_resident(True))],
        out_specs=[row, pl.BlockSpec((tm, n), lambda i: (i, 0))]
        + [pl.BlockSpec((PAIRS, d, tm // d, BAND), lambda i: (0, 0, i, 0))
           for _ in range(3) for d in DILATIONS],
        scratch_shapes=[pltpu.VMEM((PAIRS, tm, BAND), F32)],
        compiler_params=_params("parallel"),
    )(x, g, w, gq, gk, ones_bd)


TOK = 2048
UNITS = TOK // BAND


def _stack_masks():
    row = lax.broadcasted_iota(jnp.int32, (2 * BAND, 2 * BAND), 0) & (BAND - 1)
    col = lax.broadcasted_iota(jnp.int32, (2 * BAND, 2 * BAND), 1)
    lane = lax.broadcasted_iota(jnp.int32, (BAND, BAND), 1)
    head0 = lane < HEAD_DIM
    ones = [jnp.where(head0, 1.0, 0.0).astype(BF16), jnp.where(head0, 0.0, 1.0).astype(BF16)]
    return col - row, col, head0, ones


def _split3(x):
    hi = x.astype(BF16).astype(F32)
    mid = (x - hi).astype(BF16).astype(F32)
    return hi, mid, x - hi - mid


def _gather(srcs, dst, d):
    per = TOK // d
    at = 0
    for r in range(d):
        for src in srcs:
            rows = src[pl.ds(r, per, stride=d), :] if d > 1 else src[...]
            dst[pl.ds(at, per), :] = rows.astype(dst.dtype)
            at += per


def _scatter_add(out_ref, src, d, per_src, offset, first):
    per = TOK // d
    if d == 1:
        val = src[pl.ds(offset, per), :]
        out_ref[...] = val if first else out_ref[...] + val
        return
    for r in range(d):
        val = src[pl.ds(r * per_src + offset, per), :]
        idx = pl.ds(r, per, stride=d)
        out_ref[idx, :] = val if first else out_ref[idx, :] + val


def _dilated_specs(nblk, reverse):
    def at(s):
        return (nblk - 1 - s) if reverse else s
    main = [pl.BlockSpec((1, d, TOK // d, BAND), lambda j, s: (j, 0, at(s), 0)) for d in DILATIONS]
    prev = [pl.BlockSpec((1, d, TOK // d, BAND), lambda j, s: (j, 0, jnp.maximum(at(s) - 1, 0), 0))
            for d in DILATIONS]
    return main, prev


def _window_rows(prev_ref, main_ref, dst, d):
    per = TOK // d
    for r in range(d):
        dst[pl.ds(r * (per + BAND), BAND), :] = prev_ref[0, r, pl.ds(per - BAND, BAND), :]
        dst[pl.ds(r * (per + BAND) + BAND, per), :] = main_ref[0, r]


def _attn_fwd(qs, ks, vs, carry=None):
    t = qs[0].shape[2]
    nblk = t // TOK
    nd = len(DILATIONS)

    def body(*refs):
        q_refs, kp_refs, k_refs = refs[:nd], refs[nd:2 * nd], refs[2 * nd:3 * nd]
        vp_refs, v_refs = refs[3 * nd:4 * nd], refs[4 * nd:5 * nd]
        y_ref, l_ref, kw_s, vw_s, ob, lb, on, ln = refs[5 * nd:]
        i = pl.program_id(1)
        diff, col, head0, hm = _stack_masks()
        band_ok = jnp.logical_and(diff >= 0, diff <= BAND)
        for g, d in enumerate(DILATIONS):
            per = TOK // d
            nb = per // BAND
            pad = per + BAND
            _window_rows(kp_refs[g], k_refs[g], kw_s, d)
            _window_rows(vp_refs[g], v_refs[g], vw_s, d)
            q_ref = q_refs[g]

            def unit(u, carry):
                r, b = u // nb, u % nb
                qu = q_ref[0, r, pl.ds(pl.multiple_of(b * BAND, BAND), BAND), :]
                start = pl.multiple_of(r * pad + b * BAND, BAND)
                kw = kw_s[pl.ds(start, 2 * BAND), :]
                vw = vw_s[pl.ds(start, 2 * BAND), :]
                lo = jnp.where(jnp.logical_and(i == 0, b == 0), BAND, 0)
                s = _dot_nt(jnp.concatenate([qu * hm[0], qu * hm[1]], axis=0), kw)
                s = jnp.where(jnp.logical_and(band_ok, col >= lo), s, NEG)
                mx = jnp.max(s, axis=-1, keepdims=True)
                e = jnp.exp(s - mx)
                den = jnp.sum(e, axis=-1, keepdims=True)
                o2 = _dot(e.astype(BF16), vw) / den
                l2 = jnp.broadcast_to(mx + jnp.log(den), (2 * BAND, BAND))
                rows = pl.ds(pl.multiple_of(u * BAND, BAND), BAND)
                ob[rows, :] = jnp.where(head0, o2[:BAND], o2[BAND:])
                lb[rows, :] = jnp.where(head0, l2[:BAND], l2[BAND:])
                return carry

            lax.fori_loop(0, UNITS, unit, 0, unroll=16)
            _scatter_add(on.at[g], ob, d, per, 0, True)
            _scatter_add(ln.at[g], lb, d, per, 0, True)
        ls = [ln[0], ln[1], ln[2]]
        mx = jnp.maximum(jnp.maximum(ls[0], ls[1]), ls[2])
        es = [jnp.exp(l - mx) for l in ls]
        tot = es[0] + es[1] + es[2]
        y_ref[...] = (es[0] * on[0] + es[1] * on[1] + es[2] * on[2]) / tot
        l_ref[...] = mx + jnp.log(tot)

    main, prev = _dilated_specs(nblk, False)
    out = pl.BlockSpec((TOK, BAND), lambda j, i: (i, j))
    win_rows = max(d * (TOK // d + BAND) for d in DILATIONS)
    return _call(
        body, list(qs) + list(ks) + list(ks) + list(vs) + list(vs), name="attn_fwd",
        grid=(PAIRS, nblk), out_shape=[_sds((t, D_ATTN), F32)] * 2,
        in_specs=main + prev + main + prev + main, out_specs=[out, out],
        scratch_shapes=[pltpu.VMEM((win_rows, BAND), BF16)] * 2 + [pltpu.VMEM((TOK, BAND), F32)] * 2
        + [pltpu.VMEM((nd, TOK, BAND), F32)] * 2,
        semantics=("parallel", "parallel"), carry=carry)


def _attn_bwd(qs, ks, vs, do, lse, dd, carry=None):
    t = qs[0].shape[2]
    nblk = t // TOK
    nd = len(DILATIONS)
    offs = [sum(DILATIONS[:g]) * BAND for g in range(nd)]

    def body(*refs):
        q_refs, kp_refs, k_refs = refs[:nd], refs[nd:2 * nd], refs[2 * nd:3 * nd]
        vp_refs, v_refs = refs[3 * nd:4 * nd], refs[4 * nd:5 * nd]
        (do_ref, l_ref, d_ref, dq_ref, dk_ref, dv_ref, kw_s, vw_s, dos, lds, pn, dqb, dkb, dvb, ckb,
         cvb) = refs[5 * nd:]
        step = pl.program_id(1)
        i = nblk - 1 - step
        key = lax.broadcasted_iota(jnp.int32, (2 * BAND, 2 * BAND), 0)
        qry = lax.broadcasted_iota(jnp.int32, (2 * BAND, 2 * BAND), 1) & (BAND - 1)
        off = key - qry
        band_ok = jnp.logical_and(off >= 0, off <= BAND)
        lane = lax.broadcasted_iota(jnp.int32, (BAND, BAND), 1)
        head0 = lane < HEAD_DIM
        hm = [jnp.where(head0, 1.0, 0.0).astype(BF16), jnp.where(head0, 0.0, 1.0).astype(BF16)]
        lane2 = lax.broadcasted_iota(jnp.int32, (2 * BAND, BAND), 1) & (HEAD_DIM - 1)
        ones_l = jnp.where(lane2 < 3, 1.0, 0.0).astype(BF16)
        ones_d = jnp.where(jnp.logical_and(lane2 >= 3, lane2 < 6), 1.0, 0.0).astype(BF16)
        piece = lax.broadcasted_iota(jnp.int32, (TOK, BAND), 1) & (HEAD_DIM - 1)

        def pieces(x, at):
            hi, mid, lo = _split3(-x)
            return jnp.where(piece == at, hi,
                             jnp.where(piece == at + 1, mid, jnp.where(piece == at + 2, lo, 0.0)))

        pn[...] = pieces(l_ref[...], 0) + pieces(d_ref[...], 3)
        for g, d in enumerate(DILATIONS):
            per = TOK // d
            nb = per // BAND
            pad = per + BAND
            _window_rows(kp_refs[g], k_refs[g], kw_s, d)
            _window_rows(vp_refs[g], v_refs[g], vw_s, d)
            _gather([do_ref], dos, d)
            _gather([pn], lds, d)
            dkb[...] = jnp.zeros_like(dkb)
            dvb[...] = jnp.zeros_like(dvb)
            q_ref = q_refs[g]

            def unit(u, c_):
                r, b = u // nb, u % nb
                rows = pl.ds(pl.multiple_of(u * BAND, BAND), BAND)
                qu = q_ref[0, r, pl.ds(pl.multiple_of(b * BAND, BAND), BAND), :]
                dou, ldu = dos[rows, :], lds[rows, :]
                q2 = jnp.concatenate([qu * hm[0], qu * hm[1]], axis=0)
                do2 = jnp.concatenate([dou * hm[0], dou * hm[1]], axis=0)
                ld2 = jnp.concatenate([ldu * hm[0], ldu * hm[1]], axis=0)
                acc = pl.ds(pl.multiple_of(r * pad + b * BAND, BAND), 2 * BAND)
                kw = kw_s[acc, :]
                vw = vw_s[acc, :]
                lo = jnp.where(jnp.logical_and(i == 0, b == 0), BAND, 0)
                ok = jnp.logical_and(band_ok, key >= lo)
                st = _dot_nt(jnp.concatenate([kw, ones_l], axis=1), jnp.concatenate([q2, ld2], axis=1))
                dpt = _dot_nt(jnp.concatenate([vw, ones_d], axis=1), jnp.concatenate([do2, ld2], axis=1))
                pt = jnp.where(ok, jnp.exp(st), 0.0)
                dst = (pt * dpt).astype(BF16)
                dkb[acc, :] += _dot(dst, q2)
                dvb[acc, :] += _dot(pt.astype(BF16), do2)
                dq2 = _dot_tn(dst, kw)
                dqb[rows, :] = jnp.where(head0, dq2[:BAND], dq2[BAND:])
                return c_

            lax.fori_loop(0, UNITS, unit, 0, unroll=16)

            for r in range(d):
                last = pl.ds(r * pad + per, BAND)
                kept = pl.ds(offs[g] + r * BAND, BAND)

                @pl.when(step > 0)
                def _():
                    dkb[last, :] += ckb[kept, :]
                    dvb[last, :] += cvb[kept, :]

                ckb[kept, :] = dkb[pl.ds(r * pad, BAND), :]
                cvb[kept, :] = dvb[pl.ds(r * pad, BAND), :]
            _scatter_add(dq_ref, dqb, d, per, 0, g == 0)
            _scatter_add(dk_ref, dkb, d, pad, BAND, g == 0)
            _scatter_add(dv_ref, dvb, d, pad, BAND, g == 0)

    main, prev = _dilated_specs(nblk, True)
    tok = pl.BlockSpec((TOK, BAND), lambda j, s: (nblk - 1 - s, j))
    acc_rows = max(d * (TOK // d + BAND) for d in DILATIONS)
    kept_rows = sum(DILATIONS) * BAND
    return _call(
        body, list(qs) + list(ks) + list(ks) + list(vs) + list(vs) + [do, lse, dd], name="attn_bwd",
        grid=(PAIRS, nblk), out_shape=[_sds((t, D_ATTN), F32)] * 3,
        in_specs=main + prev + main + prev + main + [tok] * 3, out_specs=[tok] * 3,
        scratch_shapes=[pltpu.VMEM((acc_rows, BAND), BF16)] * 2 + [pltpu.VMEM((TOK, BAND), BF16)] * 2
        + [pltpu.VMEM((TOK, BAND), F32)] * 2 + [pltpu.VMEM((acc_rows, BAND), F32)] * 2
        + [pltpu.VMEM((kept_rows, BAND), F32)] * 2,
        semantics=("parallel", "arbitrary"), carry=carry)


def _halo_rows(tm, t):
    per = tm // 8
    prev = lambda i: (jnp.maximum(i * per - 1, 0), 0)
    nxt = lambda i: (jnp.minimum((i + 1) * per, t // 8 - 1), 0)
    return prev, nxt


def _mixer_out(z, cw, y_attn, g_conv, g_attn, tm, carry=None):
    t = z.shape[0]
    prev, _ = _halo_rows(tm, t)

    def body(z_ref, zp_ref, cw_ref, y_ref, gc_ref, ga_ref, mix_ref):
        i = pl.program_id(0)
        keep = jnp.where(i > 0, 1.0, 0.0)
        cu = jnp.concatenate([zp_ref[:, 0:512] * zp_ref[:, 1024:1536] * keep,
                              z_ref[:, 0:512] * z_ref[:, 1024:1536]], axis=0)
        c = (cw_ref[0:1, :] * pltpu.roll(cu, 2, 0) + cw_ref[1:2, :] * pltpu.roll(cu, 1, 0)
             + cw_ref[2:3, :] * cu)[8:, :]
        yc = z_ref[:, 512:1024] * c
        mix_ref[:, 0:512] = (yc * _rms_scale(yc) * gc_ref[...]).astype(BF16)
        ya = y_ref[...]
        mix_ref[:, 512:1024] = (ya * _rms_scale(ya) * ga_ref[...]).astype(BF16)

    blk = pl.BlockSpec((tm, 512), lambda i: (i, 0))
    vec = pl.BlockSpec((1, 512), lambda i: (0, 0))
    return _call(
        body, [z, z, cw, y_attn, g_conv, g_attn], name="mixer_out", grid=(t // tm,),
        out_shape=_sds((t, 1024), BF16),
        in_specs=[pl.BlockSpec((tm, 1536), lambda i: (i, 0)), pl.BlockSpec((8, 1536), prev),
                  pl.BlockSpec((8, 512), lambda i: (0, 0)), blk, vec, vec],
        out_specs=pl.BlockSpec((tm, 1024), lambda i: (i, 0)),
        semantics=("parallel",), carry=carry)


def _mixer_bwd(z, dx1, wout, y_attn, cw, g_conv, g_attn, ones_bd, tm, carry=None):
    t = z.shape[0]
    nblk = t // tm
    prev, nxt = _halo_rows(tm, t)
    e = tm + 16

    def body(z_ref, zp_ref, zn_ref, dx_ref, dxn_ref, w_ref, y_ref, cw_ref, gc_ref, ga_ref, bd_ref,
             dz_ref, do_ref, dd_ref, dcw_ref, dgc_ref, dga_ref):
        i = pl.program_id(0)
        dm = _dot_nt(dx_ref[...], w_ref[...])
        dmn = _dot_nt(dxn_ref[...], w_ref[0:D_CONV, :])[0:8, :]
        rows = lax.broadcasted_iota(jnp.int32, (e, 1), 0)
        lo = jnp.where(i > 0, 0, 8)
        hi = jnp.where(i < nblk - 1, e, tm + 8)
        ze = jnp.concatenate([zp_ref[...], z_ref[...], zn_ref[...]], axis=0)
        u, gb, gcv = ze[:, 0:512], ze[:, 512:1024], ze[:, 1024:1536]
        w0, w1, w2 = cw_ref[0:1, :], cw_ref[1:2, :], cw_ref[2:3, :]
        cu = jnp.where(rows >= lo, gcv * u, 0.0)
        cu1, cu2 = pltpu.roll(cu, 1, 0), pltpu.roll(cu, 2, 0)
        c = w0 * cu2 + w1 * cu1 + w2 * cu
        yc = gb * c
        dma = jnp.concatenate([jnp.zeros((8, 512), F32), dm[:, 0:512], dmn], axis=0)
        dyc, ych = _rms_bwd(yc, _rms_scale(yc), gc_ref[...], dma)
        dc = jnp.where(jnp.logical_and(rows >= 8, rows < hi), dyc * gb, 0.0)
        dcu = w0 * pltpu.roll(dc, e - 2, 0) + w1 * pltpu.roll(dc, e - 1, 0) + w2 * dc
        mid = slice(8, 8 + tm)
        dz_ref[:, 0:512] = (dcu * gcv)[mid, :].astype(BF16)
        dz_ref[:, 512:1024] = (dyc * c)[mid, :].astype(BF16)
        dz_ref[:, 1024:1536] = (dcu * u)[mid, :].astype(BF16)

        ya = y_ref[...]
        dmb = dm[:, 512:1024]
        dya, yah = _rms_bwd(ya, _rms_scale(ya), ga_ref[...], dmb)
        do_ref[...] = dya
        dd_ref[...] = _head_sum(dya * ya, bd_ref[...])

        @pl.when(i == 0)
        def _():
            dcw_ref[...] = jnp.zeros_like(dcw_ref)
            dgc_ref[...] = jnp.zeros_like(dgc_ref)
            dga_ref[...] = jnp.zeros_like(dga_ref)

        dcm = jnp.where(rows < tm + 8, dc, 0.0)
        dcw_ref[0:1, :] += jnp.sum(dcm * cu2, axis=0, keepdims=True)
        dcw_ref[1:2, :] += jnp.sum(dcm * cu1, axis=0, keepdims=True)
        dcw_ref[2:3, :] += jnp.sum(dcm * cu, axis=0, keepdims=True)
        dgc_ref[...] += jnp.sum((dma * ych)[mid, :], axis=0, keepdims=True)
        dga_ref[...] += jnp.sum(dmb * yah, axis=0, keepdims=True)

    blk = pl.BlockSpec((tm, 512), lambda i: (i, 0))
    vec = pl.BlockSpec((1, 512), lambda i: (0, 0))
    cwb = pl.BlockSpec((8, 512), lambda i: (0, 0))
    next16 = lambda i: (jnp.minimum((i + 1) * (tm // 16), t // 16 - 1), 0)
    return _call(
        body, [z, z, z, dx1, dx1, wout, y_attn, cw, g_conv, g_attn, ones_bd], name="mixer_bwd",
        grid=(nblk,),
        out_shape=[_sds((t, D_IN), BF16), _sds((t, 512), F32), _sds((t, 512), F32),
                   _sds((8, 512), F32), _sds((1, 512), F32), _sds((1, 512), F32)],
        in_specs=[pl.BlockSpec((tm, 1536), lambda i: (i, 0)), pl.BlockSpec((8, 1536), prev),
                  pl.BlockSpec((8, 1536), nxt), pl.BlockSpec((tm, D_MODEL), lambda i: (i, 0)),
                  pl.BlockSpec((16, D_MODEL), next16),
                  pl.BlockSpec(wout.shape, lambda i: (0, 0), pipeline_mode=_resident(True)),
                  blk, cwb, vec, vec, pl.BlockSpec((512, 512), lambda i: (0, 0))],
        out_specs=[pl.BlockSpec((tm, 1536), lambda i: (i, 0)), blk, blk, cwb, vec, vec],
        carry=carry)


def _qkv_bwd(z, dz, dqn, dkn, dv, gq, gk, ones_bd, tm, carry=None):
    t = z.shape[0]

    def body(zq_ref, zk_ref, _, dqn_ref, dkn_ref, dv_ref, gq_ref, gk_ref, bd_ref,
             dz_ref, dgq_ref, dgk_ref):
        bd = bd_ref[...]

        @pl.when(pl.program_id(0) == 0)
        def _():
            dgq_ref[...] = jnp.zeros_like(dgq_ref)
            dgk_ref[...] = jnp.zeros_like(dgk_ref)

        def back(v, dn, g, scale):
            r = _head_rms_scale(v, bd)
            vh = v * r
            dh = dn * (g * scale)
            dv = r * (dh - vh * (_head_sum(dh * vh, bd) * (1.0 / HEAD_DIM)))
            return dv, jnp.sum(dn * scale * vh, axis=0, keepdims=True)

        dq, dgq = back(zq_ref[...], dqn_ref[...], gq_ref[...], HEAD_DIM ** -0.5)
        dk, dgk = back(zk_ref[...], dkn_ref[...], gk_ref[...], 1.0)
        dgq_ref[...] += dgq
        dgk_ref[...] += dgk
        dz_ref[:, 0:512] = dq.astype(BF16)
        dz_ref[:, 512:1024] = dk.astype(BF16)
        dz_ref[:, 1024:1536] = dv_ref[...].astype(BF16)

    blk = pl.BlockSpec((tm, 512), lambda i: (i, 0))
    vec = pl.BlockSpec((1, 512), lambda i: (0, 0))
    return _call(
        body, [z, z, dz, dqn, dkn, dv, gq, gk, ones_bd], name="qkv_bwd", grid=(t // tm,),
        out_shape=[_sds((t, D_IN), BF16), _sds((1, 512), F32), _sds((1, 512), F32)],
        in_specs=[pl.BlockSpec((tm, 512), lambda i: (i, 3)), pl.BlockSpec((tm, 512), lambda i: (i, 4)),
                  ANY] + [blk] * 3 + [vec, vec, pl.BlockSpec((512, 512), lambda i: (0, 0))],
        out_specs=[pl.BlockSpec((tm, 1536), lambda i: (i, 1)), vec, vec],
        carry=carry, aliases={2: 0})


def _columns_from_chips(g):
    return g.transpose(1, 0, 2).reshape(g.shape[1], N_CHIPS * g.shape[2])


def kernel(x, g_mix, w_in, conv_w, g_q, g_k, g_conv_out, g_attn_out, w_out, g_ffn, w_gate, w_up, w_down, loss_target, m_g_mix, m_w_in, m_conv_w, m_g_q, m_g_k, m_g_conv_out, m_g_attn_out, m_w_out, m_g_ffn, m_w_gate, m_w_up, m_w_down, v_g_mix, v_w_in, v_conv_w, v_g_q, v_g_k, v_g_conv_out, v_g_attn_out, v_w_out, v_g_ffn, v_w_gate, v_w_up, v_w_down):
    t = x.shape[1]
    xs = x[0]
    target = loss_target[0]
    tm = min(512, t)
    tmm = min(1024, t)

    cw_pad = jnp.pad(conv_w[0], ((0, 13), (0, 0)))
    gathered = _all_gather([w_in[0].astype(BF16), cw_pad])
    win = _columns_from_chips(gathered[0])
    cw = jnp.pad(gathered[1][:, 0:3, :].transpose(1, 0, 2).reshape(3, D_CONV), ((0, 5), (0, 0)))
    later = [w_out[0].astype(BF16), w_gate[0].T.astype(BF16), w_up[0].T.astype(BF16),
             w_down[0].astype(BF16)]

    head_id = jnp.arange(D_ATTN) // HEAD_DIM
    ones_bd = (head_id[:, None] == head_id[None, :]).astype(BF16)
    gq_t = jnp.tile(g_q, (1, D_ATTN // HEAD_DIM))
    gk_t = jnp.tile(g_k, (1, D_ATTN // HEAD_DIM))

    h1, z, *dilated = _in_proj(xs, g_mix, win, gq_t, gk_t, ones_bd, tm)
    nd = len(DILATIONS)
    qs, ks, vs = dilated[:nd], dilated[nd:2 * nd], dilated[2 * nd:]
    (y_attn, lse), gathered = _attn_fwd(qs, ks, vs, carry=_x_gather_chips(later))
    mix, gathered = _mixer_out(z, cw, y_attn, g_conv_out, g_attn_out, tm,
                               carry=_x_gather_sibling(gathered))
    wout = gathered[0].reshape(D_MODEL, D_MODEL)
    wgate_t = gathered[1].reshape(D_FF, D_MODEL)
    wup_t = gathered[2].reshape(D_FF, D_MODEL)
    wdown = gathered[3].reshape(D_FF, D_MODEL)
    (x1,) = _matmul("out_proj", mix, wout, [xs], [F32], lambda acc, r: (r + acc,), tm, D_MODEL)
    h2, gate, up, act = _norm_matmul("ffn_up", x1, g_ffn, [wgate_t, wup_t], tm, D_FF, True, BF16,
                                     transposed_w=True)

    def loss_epilogue(acc, r, tgt):
        err = r + acc - tgt
        dy = err * (1.0 / D_MODEL)
        return dy, dy, jnp.sum(err * err)

    dx2, dx2b, loss_sum = _matmul("ffn_down_loss", act, wdown, [x1, target], [F32, BF16],
                                  loss_epilogue, tm, D_MODEL, loss=True)

    def swiglu_bwd(da, gt, u):
        gt, u = gt.astype(F32), u.astype(F32)
        s = _sigmoid(gt)
        return da * u * (s * (1.0 + gt * (1.0 - s))), da * (gt * s)

    dgate, dup = _matmul("ffn_down_bwd", dx2b, wdown, [gate, up], [BF16, BF16], swiglu_bwd,
                         tm, D_FF, transposed_w=True)
    gw_down = _matmul_tn("grad_w_down", act, dx2b, 512, tmm)
    gw_gate_t = _matmul_tn("grad_w_gate", dgate, h2, 512, tmm)
    gw_up_t = _matmul_tn("grad_w_up", dup, h2, 512, tmm)

    me = 2 * lax.axis_index("x") + lax.axis_index("y")
    where = jnp.stack([lax.axis_index("c"), me]).astype(jnp.int32)

    def pair_sums(names, full, got):
        return [_pair_sum(f"pair_sum_{nme}", a, b, where) for nme, a, b in zip(names, full, got)]

    def chip_sums(names, pair, got):
        return [_chip_sum(f"chip_sum_{nme}", own, b) for nme, (_, own), b in zip(names, pair, got)]

    ffn = ["w_gate", "w_up", "w_down"]
    full = [g.reshape(N_CHIPS, D_FF // N_CHIPS, D_MODEL) for g in (gw_gate_t, gw_up_t, gw_down)]
    (dx1, dx1b, gg_ffn), got = _matmul_norm_bwd(
        "ffn_up_bwd", [(dgate, wgate_t), (dup, wup_t)], x1, dx2, g_ffn, tm, carry=_x_pair(full),
        transposed_w=False)
    pair = pair_sums(ffn, full, got)
    gw_out = _matmul_tn("grad_w_out", mix, dx1b, 512, tmm)
    full = [gw_out.reshape(N_CHIPS, D_MODEL // N_CHIPS, D_MODEL)]
    (dzc, do, dd, gcw, gg_conv, gg_attn), got = _mixer_bwd(
        z, dx1b, wout, y_attn, cw, g_conv_out, g_attn_out, ones_bd, tm, carry=_x_pair(full))
    pair += pair_sums(["w_out"], full, got)
    early = ffn + ["w_out"]
    (dqn, dkn, dv), got = _attn_bwd(qs, ks, vs, do, lse, dd, carry=_x_chips([p for p, _ in pair]))
    mine = chip_sums(early, pair, got)
    (dz, gg_q, gg_k), theirs = _qkv_bwd(z, dzc, dqn, dkn, dv, gq_t, gk_t, ones_bd, tm,
                                        carry=_x_share(mine))
    full = [_matmul_tn("grad_w_in", h1, dz, D_IN // N_CHIPS, tmm, by_chip=True)]
    grad_x, _, gg_mix = _matmul_norm_bwd("in_proj_bwd", [(dz, win)], xs, dx1, g_mix, tm)
    got = _exchange_alone("grad_pair_exchange_w_in", _x_pair(full))
    pair = pair_sums(["w_in"], full, got)
    got = _exchange_alone("grad_chip_exchange_w_in", _x_chips([pair[0][0]]))
    mine += chip_sums(["w_in"], pair, got)
    theirs = list(theirs) + list(_exchange_alone("grad_pair_share_w_in", _x_share(mine[-1:])))
    big = early + ["w_in"]

    small = _small_all_reduce({
        "g_mix": gg_mix, "g_ffn": gg_ffn, "g_conv_out": gg_conv, "g_attn_out": gg_attn,
        "g_q": gg_q, "g_k": gg_k, "loss": loss_sum, "conv_w": gcw})
    heads = D_ATTN // HEAD_DIM
    grads = {
        "g_mix": small[0:1, :], "g_ffn": small[1:2, :],
        "g_conv_out": small[2:3, 0:512], "g_attn_out": small[2:3, 512:1024],
        "g_q": small[3, 0:512].reshape(heads, HEAD_DIM).sum(axis=0)[None, :],
        "g_k": small[3, 512:1024].reshape(heads, HEAD_DIM).sum(axis=0)[None, :],
        "conv_w": lax.dynamic_slice(small[8:11, 0:512], (0, me * (D_CONV // N_CHIPS)),
                                    (3, D_CONV // N_CHIPS)),
    }
    halves = dict(zip(big, zip(mine, theirs)))
    loss = small[4, 0] * 0.5 * (1.0 / D_MODEL)

    weights = dict(g_mix=g_mix, w_in=w_in, conv_w=conv_w, g_q=g_q, g_k=g_k, g_conv_out=g_conv_out,
                   g_attn_out=g_attn_out, w_out=w_out, g_ffn=g_ffn, w_gate=w_gate, w_up=w_up,
                   w_down=w_down)
    moments_m = dict(g_mix=m_g_mix, w_in=m_w_in, conv_w=m_conv_w, g_q=m_g_q, g_k=m_g_k,
                     g_conv_out=m_g_conv_out, g_attn_out=m_g_attn_out, w_out=m_w_out, g_ffn=m_g_ffn,
                     w_gate=m_w_gate, w_up=m_w_up, w_down=m_w_down)
    moments_v = dict(g_mix=v_g_mix, w_in=v_w_in, conv_w=v_conv_w, g_q=v_g_q, g_k=v_g_k,
                     g_conv_out=v_g_conv_out, g_attn_out=v_g_attn_out, w_out=v_w_out, g_ffn=v_g_ffn,
                     w_gate=v_w_gate, w_up=v_w_up, w_down=v_w_down)
    names = list(weights)
    out_g, out_d, out_m, out_v = [], [], [], []
    for nme in names:
        wgt = weights[nme]
        shape2 = wgt.shape[-2:] if wgt.ndim == 3 else wgt.shape
        flip = nme in ("w_gate", "w_up")

        def to2d(a):
            return a.reshape(shape2).T if flip else a.reshape(shape2)

        def back(a):
            return (a.T if flip else a).reshape(wgt.shape)

        state = (to2d(wgt), to2d(moments_m[nme]), to2d(moments_v[nme]))
        if nme in halves:
            g2, dlt, nm, nv = _adamw_shard(f"adamw_{nme}", *state, *halves[nme], where)
        else:
            g2 = grads[nme].reshape(shape2)
            dlt, nm, nv = _adamw(f"adamw_{nme}", state[0], g2, state[1], state[2])
        out_g.append(back(g2))
        out_d.append(back(dlt))
        out_m.append(back(nm))
        out_v.append(back(nv))
    return (loss, grad_x[None], *out_g, *out_d, *out_m, *out_v)
```

```python
import functools
from typing import Any, Callable, NamedTuple, Sequence

import jax
import jax.numpy as jnp
from jax import lax
from jax.experimental import pallas as pl
from jax.experimental.pallas import tpu as pltpu

F32 = jnp.float32
BF16 = jnp.bfloat16
MESH = pl.DeviceIdType.MESH

D_MODEL = 1024
D_CONV = 512
D_ATTN = 512
HEAD_DIM = 64
D_FF = 2816
D_IN = 3 * D_CONV + 3 * D_ATTN
DILATIONS = (1, 4, 16)
BAND = 128
EPS = 1e-6
NEG = -1e30
N_CHIPS = 4

ADAM_LR = 0.001
ADAM_B1 = 0.9
ADAM_B2 = 0.999
ADAM_EPS = 1e-08
ADAM_WD = 0.01
ADAM_STEP = 10

V7X_VMEM_BYTES = 64 * 1024 * 1024
VMEM_LIMIT = V7X_VMEM_BYTES - 8 * 1024 * 1024
ANY = pl.BlockSpec(memory_space=pl.ANY)
VMEM_WHOLE = pl.BlockSpec(memory_space=pltpu.VMEM)


def _params(*sem):
    return pltpu.CompilerParams(dimension_semantics=sem, vmem_limit_bytes=VMEM_LIMIT)


def _sds(shape, dtype):
    return jax.ShapeDtypeStruct(shape, dtype)


def _resident(whole):
    return pl.Buffered(1) if whole else None


def _place():
    x, y, c = lax.axis_index("x"), lax.axis_index("y"), lax.axis_index("c")
    chips = [(1 - x, y), (x, 1 - y), (1 - x, 1 - y)]
    return x, y, c, 2 * x + y, chips, [2 * cx + cy for cx, cy in chips]


def _all_gather(shards):
    n = len(shards)

    def body(*refs):
        ins, outs, stage = refs[:n], refs[n:2 * n], refs[2 * n:3 * n]
        ssem, rsem, fsem, gsem, lsem, osem = refs[3 * n:]
        x, y, c, me, chips, cids = _place()
        sib = (x, y, 1 - c)

        def half(w, which):
            h = shards[w].shape[0] // 2
            return pl.ds(pl.multiple_of(which * h, 8), h)

        loads = [pltpu.make_async_copy(ins[w], stage[w], lsem.at[w]) for w in range(n)]
        local = [pltpu.make_async_copy(stage[w], outs[w].at[me], osem.at[w]) for w in range(n)]
        for cp in loads:
            cp.start()

        def chip_copy(w, j, src_slot):
            rows = half(w, c)
            return pltpu.make_async_remote_copy(
                src_ref=ins[w].at[rows], dst_ref=outs[w].at[src_slot, rows],
                send_sem=ssem.at[3 * w + j], recv_sem=rsem.at[3 * w + j],
                device_id=(*chips[j], c), device_id_type=MESH)

        def sib_copy(w, j, which):
            rows = half(w, which)
            return pltpu.make_async_remote_copy(
                src_ref=outs[w].at[cids[j], rows], dst_ref=outs[w].at[cids[j], rows],
                send_sem=fsem.at[3 * w + j], recv_sem=gsem.at[3 * w + j],
                device_id=sib, device_id_type=MESH)

        sends = [chip_copy(w, j, me) for w in range(n) for j in range(3)]
        for cp in sends:
            cp.start()
        for w in range(n):
            loads[w].wait()
            local[w].start()
        passed = []
        for w in range(n):
            for j in range(3):
                chip_copy(w, j, cids[j]).wait_recv()
                cp = sib_copy(w, j, c)
                cp.start()
                passed.append(cp)
        for w in range(n):
            for j in range(3):
                sib_copy(w, j, 1 - c).wait_recv()
        for cp in sends + passed:
            cp.wait_send()
        for cp in local:
            cp.wait()

    return pl.pallas_call(
        body, name="all_gather_weights",
        out_shape=[_sds((N_CHIPS,) + s.shape, s.dtype) for s in shards],
        in_specs=[ANY] * n, out_specs=[ANY] * n,
        scratch_shapes=[pltpu.VMEM(s.shape, s.dtype) for s in shards]
        + [pltpu.SemaphoreType.DMA((3 * n,))] * 4 + [pltpu.SemaphoreType.DMA((n,))] * 2,
        compiler_params=pltpu.CompilerParams(vmem_limit_bytes=VMEM_LIMIT),
    )(*shards)


class _Exchange(NamedTuple):
    srcs: Sequence[Any]
    lands: Sequence[Any]
    outs: Sequence[Any]
    n_sems: int
    copies: Callable


def _remote(src, dst, ssem, rsem, k, to):
    return pltpu.make_async_remote_copy(src_ref=src, dst_ref=dst, send_sem=ssem.at[k],
                                        recv_sem=rsem.at[k], device_id=to, device_id_type=MESH)


def _x_gather_chips(shards):
    def copies(srcs, lands, outs, ssem, rsem):
        _, _, c, me, chips, cids = _place()
        go, arrive = [], []
        for w, s in enumerate(shards):
            h = s.shape[0] // 2
            rows = pl.ds(pl.multiple_of(c * h, 8), h)
            for j in range(3):
                to = (*chips[j], c)
                go.append(_remote(srcs[w].at[rows], lands[w].at[me, rows], ssem, rsem, 3 * w + j, to))
                arrive.append(_remote(srcs[w].at[rows], lands[w].at[cids[j], rows], ssem, rsem,
                                      3 * w + j, to))
        return go, arrive

    lands = [jnp.broadcast_to(s[None], (N_CHIPS,) + s.shape) for s in shards]
    return _Exchange(shards, lands, [], 3 * len(shards), copies)


def _x_gather_sibling(gathered):
    def copies(srcs, lands, outs, ssem, rsem):
        x, y, c, _, _, cids = _place()
        go, arrive = [], []
        for w, g in enumerate(gathered):
            h = g.shape[1] // 2
            mine = pl.ds(pl.multiple_of(c * h, 8), h)
            theirs = pl.ds(pl.multiple_of((1 - c) * h, 8), h)
            for j in range(3):
                slab = lands[w].at[cids[j]]
                go.append(_remote(slab.at[mine], slab.at[mine], ssem, rsem, 3 * w + j, (x, y, 1 - c)))
                arrive.append(_remote(slab.at[theirs], slab.at[theirs], ssem, rsem, 3 * w + j,
                                      (x, y, 1 - c)))
        return go, arrive

    return _Exchange([], gathered, [], 3 * len(gathered), copies)


def _x_pair(grads):
    def copies(srcs, lands, outs, ssem, rsem):
        x, y, c, _, _, _ = _place()
        go = []
        for w, g in enumerate(grads):
            h = g.shape[1] // 2
            theirs = pl.ds(pl.multiple_of((1 - c) * h, 8), h)
            go.append(_remote(srcs[w].at[:, theirs, :], outs[w], ssem, rsem, w, (x, y, 1 - c)))
        return go, go

    outs = [_sds((N_CHIPS, g.shape[1] // 2, g.shape[2]), g.dtype) for g in grads]
    return _Exchange(grads, [], outs, len(grads), copies)


def _x_chips(parts):
    def copies(srcs, lands, outs, ssem, rsem):
        _, _, c, _, chips, cids = _place()
        go = [_remote(srcs[w].at[cids[j]], outs[w].at[j], ssem, rsem, 3 * w + j, (*chips[j], c))
              for w in range(len(parts)) for j in range(3)]
        return go, go

    outs = [_sds((3,) + p.shape[1:], p.dtype) for p in parts]
    return _Exchange(parts, [], outs, 3 * len(parts), copies)


def _x_share(halves):
    def copies(srcs, lands, outs, ssem, rsem):
        x, y, c, _, _, _ = _place()
        go = [_remote(srcs[w], outs[w], ssem, rsem, w, (x, y, 1 - c)) for w in range(len(halves))]
        return go, go

    return _Exchange(halves, [], [_sds(h.shape, h.dtype) for h in halves], len(halves), copies)


def _call(body, args, *, name, grid, in_specs, out_specs, out_shape, scratch_shapes=(),
          semantics=None, carry=None, aliases=None):
    single = not isinstance(out_shape, (list, tuple))
    out_shape = [out_shape] if single else list(out_shape)
    out_specs = [out_specs] if single else list(out_specs)
    aliases = dict(aliases or {})
    if carry is None:
        res = pl.pallas_call(
            body, name=name, grid=grid, in_specs=list(in_specs), out_specs=out_specs,
            out_shape=out_shape, scratch_shapes=list(scratch_shapes), input_output_aliases=aliases,
            compiler_params=_params(*(semantics or ("arbitrary",) * len(grid))))(*args)
        return res[0] if single else res
    n_in, n_out, n_scr = len(args), len(out_shape), len(scratch_shapes)
    n_src, n_land, n_new = len(carry.srcs), len(carry.lands), len(carry.outs)

    def carrying(*refs):
        at = 0
        parts = []
        for n in (n_in, n_src, n_land, n_out, n_land, n_new, n_scr, 2):
            parts.append(refs[at:at + n])
            at += n
        ins, srcs, _, outs, lands, news, scratch, (ssem, rsem) = parts
        ids = [pl.program_id(a) for a in range(len(grid))]
        first = functools.reduce(jnp.logical_and, [i == 0 for i in ids])
        last = functools.reduce(jnp.logical_and, [i == g - 1 for i, g in zip(ids, grid)])
        go, arrive = carry.copies(srcs, lands, news, ssem, rsem)

        @pl.when(first)
        def _():
            for cp in go:
                cp.start()

        body(*ins, *outs, *scratch)

        @pl.when(last)
        def _():
            for cp in go:
                cp.wait_send()
            for cp in arrive:
                cp.wait_recv()

    res = pl.pallas_call(
        carrying, name=name, grid=grid,
        in_specs=list(in_specs) + [ANY] * (n_src + n_land),
        out_specs=out_specs + [ANY] * (n_land + n_new),
        out_shape=out_shape + [_sds(a.shape, a.dtype) for a in carry.lands] + list(carry.outs),
        input_output_aliases={**aliases, **{n_in + n_src + i: n_out + i for i in range(n_land)}},
        scratch_shapes=list(scratch_shapes) + [pltpu.SemaphoreType.DMA((carry.n_sems,))] * 2,
        compiler_params=_params(*(("arbitrary",) * len(grid))))(*args, *carry.srcs, *carry.lands)
    own = res[:n_out]
    return (own[0] if single else own), res[n_out:]


def _exchange_alone(name, exchange):
    def body(x_ref, o_ref):
        o_ref[...] = x_ref[...]

    blk = pl.BlockSpec((8, 128), lambda i: (0, 0))
    _, res = _call(body, [jnp.zeros((8, 128), F32)], name=name, grid=(1,), in_specs=[blk],
                   out_specs=blk, out_shape=_sds((8, 128), F32), carry=exchange)
    return res


def _row_block(r, want):
    return max(d for d in range(1, min(want, r) + 1) if r % d == 0 and (d % 8 == 0 or d == r))


def _pair_sum(name, full, got, where):
    _, r, n = full.shape
    h = r // 2
    tr = _row_block(h, 256)
    nb = h // tr

    def body(w_ref, a_ref, b_ref, o_ref, own_ref):
        total = a_ref[...] + b_ref[...]
        o_ref[...] = total.astype(BF16)

        @pl.when(pl.program_id(1) == w_ref[1])
        def _():
            own_ref[...] = total[0]

    blk = pl.BlockSpec((1, tr, n), lambda i, s, w: (s, i, 0))
    return pl.pallas_call(
        body, name=name, out_shape=[_sds(got.shape, BF16), _sds((h, n), F32)],
        grid_spec=pltpu.PrefetchScalarGridSpec(
            num_scalar_prefetch=1, grid=(nb, N_CHIPS),
            in_specs=[pl.BlockSpec((1, tr, n), lambda i, s, w: (s, w[0] * nb + i, 0)), blk],
            out_specs=[blk, pl.BlockSpec((tr, n), lambda i, s, w: (i, 0))]),
        compiler_params=_params("parallel", "arbitrary"),
    )(where, full, got)


def _chip_sum(name, own, got):
    h, n = own.shape
    tr = _row_block(h, 256)

    def body(a_ref, b0, b1, b2, o_ref):
        o_ref[...] = ((a_ref[...] + b0[0].astype(F32)) + b1[0].astype(F32)) + b2[0].astype(F32)

    def slot(j):
        return pl.BlockSpec((1, tr, n), lambda i: (j, i, 0))

    blk = pl.BlockSpec((tr, n), lambda i: (i, 0))
    return pl.pallas_call(
        body, name=name, grid=(h // tr,), out_shape=_sds((h, n), F32),
        in_specs=[blk, slot(0), slot(1), slot(2)], out_specs=blk,
        compiler_params=_params("parallel"),
    )(own, got, got, got)


SMALL_ROWS = 16
SMALL_LAYOUT = (
    ("g_mix", 0, 0, 1, 1024), ("g_ffn", 1, 0, 1, 1024), ("g_conv_out", 2, 0, 1, 512),
    ("g_attn_out", 2, 512, 1, 512), ("g_q", 3, 0, 1, 512), ("g_k", 3, 512, 1, 512),
    ("loss", 4, 0, 1, 128), ("conv_w", 8, 0, 8, 512))


def _small_all_reduce(parts):
    names = [s[0] for s in SMALL_LAYOUT]

    def body(*refs):
        ins = refs[:len(names)]
        out_ref, stage, buf, ssem, rsem = refs[len(names):]
        x, y, c, _, _, _ = _place()
        me = 4 * x + 2 * y + c
        stage[...] = jnp.zeros_like(stage)
        for ref, (_, r0, c0, nr, nc) in zip(ins, SMALL_LAYOUT):
            stage[r0:r0 + nr, c0:c0 + nc] = ref[0:nr, :]
        buf[me] = stage[...]
        peers = []
        for d in range(1, 8):
            px = 1 - x if d & 4 else x
            py = 1 - y if d & 2 else y
            pc = 1 - c if d & 1 else c
            peers.append(((px, py, pc), 4 * px + 2 * py + pc))
        sends = [pltpu.make_async_remote_copy(
            src_ref=stage, dst_ref=buf.at[me], send_sem=ssem.at[k], recv_sem=rsem.at[k],
            device_id=peer, device_id_type=MESH) for k, (peer, _) in enumerate(peers)]
        for cp in sends:
            cp.start()
        for k, (peer, pid) in enumerate(peers):
            pltpu.make_async_remote_copy(
                src_ref=stage, dst_ref=buf.at[pid], send_sem=ssem.at[k], recv_sem=rsem.at[k],
                device_id=peer, device_id_type=MESH).wait_recv()
        for cp in sends:
            cp.wait_send()
        acc = buf[0]
        for k in range(1, 8):
            acc = acc + buf[k]
        out_ref[...] = acc

    return pl.pallas_call(
        body, name="small_all_reduce", out_shape=_sds((SMALL_ROWS, 1024), F32),
        in_specs=[VMEM_WHOLE] * len(names), out_specs=VMEM_WHOLE,
        scratch_shapes=[pltpu.VMEM((SMALL_ROWS, 1024), F32), pltpu.VMEM((8, SMALL_ROWS, 1024), F32),
                        pltpu.SemaphoreType.DMA((7,)), pltpu.SemaphoreType.DMA((7,))],
    )(*[parts[k] for k in names])


def _dot(a, b):
    return jnp.dot(a, b, preferred_element_type=F32)


def _dot_nt(a, b):
    return lax.dot_general(a, b, (((1,), (1,)), ((), ())), preferred_element_type=F32)


def _dot_tn(a, b):
    return lax.dot_general(a, b, (((0,), (0,)), ((), ())), preferred_element_type=F32)


def _sigmoid(v):
    return 1.0 / (1.0 + jnp.exp(-v))


def _rms_scale(v):
    return lax.rsqrt(jnp.mean(v * v, axis=-1, keepdims=True) + EPS)


def _rms_bwd(v, r, g, dy):
    vh = v * r
    dh = dy * g
    return r * (dh - vh * jnp.mean(dh * vh, axis=-1, keepdims=True)), vh


def _head_sum(a, ones_bd):
    hi = a.astype(BF16)
    lo = (a - hi.astype(F32)).astype(BF16)
    return _dot(hi, ones_bd) + _dot(lo, ones_bd)


def _head_rms_scale(v, ones_bd):
    return lax.rsqrt(_head_sum(v * v, ones_bd) * (1.0 / HEAD_DIM) + EPS)


MXU_COLUMNS = 256


def _column_chunks(n):
    width = MXU_COLUMNS if n % MXU_COLUMNS == 0 else n
    return [slice(c, c + width) for c in range(0, n, width)]


def _norm_matmul(name, x, g, ws, tm, tn, swiglu, out_dtype=F32, transposed_w=False):
    t, d = x.shape
    n = ws[0].shape[0] if transposed_w else ws[0].shape[1]
    nw = len(ws)

    def body(x_ref, g_ref, *refs):
        w_refs, h_ref, o_refs = refs[:nw], refs[nw], refs[nw + 1:2 * nw + 1]
        hs = refs[-1]

        @pl.when(pl.program_id(1) == 0)
        def _():
            xv = x_ref[...]
            h = (xv * _rms_scale(xv) * g_ref[...]).astype(BF16)
            hs[...] = h
            h_ref[...] = h

        h = hs[...]
        for cols in _column_chunks(tn):
            outs = [_dot_nt(h, w[cols, :]) if transposed_w else _dot(h, w[:, cols]) for w in w_refs]
            for o_ref, o in zip(o_refs, outs):
                o_ref[:, cols] = o.astype(out_dtype)
            if swiglu:
                refs[2 * nw + 1][:, cols] = (outs[0] * _sigmoid(outs[0]) * outs[1]).astype(BF16)

    row = pl.BlockSpec((tm, d), lambda i, j: (i, 0))
    col = pl.BlockSpec((tm, tn), lambda i, j: (i, j))
    out_shape = [_sds((t, d), BF16)] + [_sds((t, n), out_dtype)] * nw
    out_specs = [row] + [col] * nw
    if swiglu:
        out_shape.append(_sds((t, n), BF16))
        out_specs.append(col)
    return pl.pallas_call(
        body, name=name, grid=(t // tm, n // tn), out_shape=out_shape,
        in_specs=[row, pl.BlockSpec((1, d), lambda i, j: (0, 0))]
        + [pl.BlockSpec((tn, d), lambda i, j: (j, 0), pipeline_mode=_resident(tn == n))
           if transposed_w
           else pl.BlockSpec((d, tn), lambda i, j: (0, j), pipeline_mode=_resident(tn == n))] * nw,
        out_specs=out_specs, scratch_shapes=[pltpu.VMEM((tm, d), BF16)],
        compiler_params=_params("parallel", "arbitrary"),
    )(x, g, *ws)


def _matmul(name, a, w, extras, out_dtypes, epilogue, tm, tn, transposed_w=False, loss=False):
    t, k = a.shape
    n = w.shape[0] if transposed_w else w.shape[1]
    ne, no = len(extras), len(out_dtypes)

    def body(a_ref, w_ref, *refs):
        e_refs, o_refs = refs[:ne], refs[ne:]
        a = a_ref[...]
        total = 0.0
        for cols in _column_chunks(tn):
            acc = _dot_nt(a, w_ref[cols, :]) if transposed_w else _dot(a, w_ref[:, cols])
            res = epilogue(acc, *[e[:, cols] for e in e_refs])
            for o_ref, r in zip(o_refs[:no], res[:no]):
                o_ref[:, cols] = r.astype(o_ref.dtype)
            if loss:
                total = total + res[no]
        if loss:
            first = jnp.logical_and(pl.program_id(0) == 0, pl.program_id(1) == 0)

            @pl.when(first)
            def _():
                o_refs[no][...] = jnp.zeros_like(o_refs[no])

            o_refs[no][...] += total

    col = pl.BlockSpec((tm, tn), lambda i, j: (i, j))
    w_spec = (pl.BlockSpec((tn, k), lambda i, j: (j, 0), pipeline_mode=_resident(tn == n))
              if transposed_w
              else pl.BlockSpec((k, tn), lambda i, j: (0, j), pipeline_mode=_resident(tn == n)))
    out_shape = [_sds((t, n), dt) for dt in out_dtypes]
    out_specs = [col] * no
    if loss:
        out_shape.append(_sds((8, 128), F32))
        out_specs.append(pl.BlockSpec((8, 128), lambda i, j: (0, 0)))
    return pl.pallas_call(
        body, name=name, grid=(t // tm, n // tn), out_shape=out_shape,
        in_specs=[pl.BlockSpec((tm, k), lambda i, j: (i, 0)), w_spec] + [col] * ne,
        out_specs=out_specs,
        compiler_params=_params(*(("arbitrary", "arbitrary") if loss else ("parallel", "parallel"))),
    )(a, w, *extras)


def _matmul_norm_bwd(name, pairs, x, dres, g, tm, carry=None, transposed_w=True):
    t, d = x.shape
    npairs = len(pairs)
    product = _dot_nt if transposed_w else _dot

    def body(*refs):
        a_refs, w_refs = refs[:npairs], refs[npairs:2 * npairs]
        x_ref, r_ref, g_ref, dx_ref, dxb_ref, dg_ref = refs[2 * npairs:]
        dy = product(a_refs[0][...], w_refs[0][...])
        for a_ref, w_ref in zip(a_refs[1:], w_refs[1:]):
            dy = dy + product(a_ref[...], w_ref[...])
        xv = x_ref[...]
        dx, xh = _rms_bwd(xv, _rms_scale(xv), g_ref[...], dy)
        dx = dx + r_ref[...]
        dx_ref[...] = dx
        dxb_ref[...] = dx.astype(BF16)

        @pl.when(pl.program_id(0) == 0)
        def _():
            dg_ref[...] = jnp.zeros_like(dg_ref)

        dg_ref[...] += jnp.sum(dy * xh, axis=0, keepdims=True)

    row = pl.BlockSpec((tm, d), lambda i: (i, 0))
    vec = pl.BlockSpec((1, d), lambda i: (0, 0))
    return _call(
        body, [a for a, _ in pairs] + [w for _, w in pairs] + [x, dres, g], name=name,
        grid=(t // tm,), out_shape=[_sds((t, d), F32), _sds((t, d), BF16), _sds((1, d), F32)],
        in_specs=[pl.BlockSpec((tm, a.shape[1]), lambda i: (i, 0)) for a, _ in pairs]
        + [pl.BlockSpec(w.shape, lambda i: (0, 0), pipeline_mode=pl.Buffered(1)) for _, w in pairs]
        + [row, row, vec],
        out_specs=[row, row, vec], carry=carry)


def _matmul_tn(name, a, g, tn, tk, by_chip=False):
    t, ka = a.shape
    n = g.shape[1]

    def body(a_ref, g_ref, o_ref):
        @pl.when(pl.program_id(1) == 0)
        def _():
            o_ref[...] = jnp.zeros_like(o_ref)

        acc = _dot_tn(a_ref[...], g_ref[...])
        o_ref[...] += acc[None] if by_chip else acc

    return pl.pallas_call(
        body, name=name, grid=(n // tn, t // tk),
        out_shape=_sds((n // tn, ka, tn) if by_chip else (ka, n), F32),
        in_specs=[pl.BlockSpec((tk, ka), lambda j, s: (s, 0)),
                  pl.BlockSpec((tk, tn), lambda j, s: (s, j))],
        out_specs=(pl.BlockSpec((1, ka, tn), lambda j, s: (j, 0, 0)) if by_chip
                   else pl.BlockSpec((ka, tn), lambda j, s: (0, j))),
        compiler_params=_params("parallel", "arbitrary"),
    )(a, g)


def _elementwise(name, fn, ins, out_dtypes, tr):
    r, n = ins[0].shape
    tr = _row_block(r, tr)
    ni = len(ins)

    def body(*refs):
        res = fn(*[ref[...] for ref in refs[:ni]])
        for o_ref, v in zip(refs[ni:], res):
            o_ref[...] = v.astype(o_ref.dtype)

    blk = pl.BlockSpec((tr, n), lambda i: (i, 0))
    return pl.pallas_call(
        body, name=name, grid=(r // tr,), out_shape=[_sds((r, n), dt) for dt in out_dtypes],
        in_specs=[blk] * ni, out_specs=[blk] * len(out_dtypes),
        compiler_params=_params("parallel"),
    )(*ins)


def _adamw_update(w, g, m, v):
    m = ADAM_B1 * m + (1.0 - ADAM_B1) * g
    v = ADAM_B2 * v + (1.0 - ADAM_B2) * (g * g)
    m_hat = m / (1.0 - ADAM_B1 ** ADAM_STEP)
    v_hat = v / (1.0 - ADAM_B2 ** ADAM_STEP)
    return -ADAM_LR * (m_hat / (jnp.sqrt(v_hat) + ADAM_EPS) + ADAM_WD * w), m, v


def _adamw(name, w, g, m, v):
    return _elementwise(name, _adamw_update, [w, g, m, v], [F32] * 3, 256)


def _adamw_shard(name, w, m, v, mine, theirs, where):
    r, n = w.shape
    h = r // 2
    tr = _row_block(h, 256)
    nb = h // tr

    def body(w_ref, p_ref, m_ref, v_ref, a_ref, b_ref, g_ref, d_ref, nm_ref, nv_ref):
        g = jnp.where(pl.program_id(0) == w_ref[0], a_ref[...], b_ref[...])
        g_ref[...] = g
        d_ref[...], nm_ref[...], nv_ref[...] = _adamw_update(p_ref[...], g, m_ref[...], v_ref[...])

    whole = pl.BlockSpec((tr, n), lambda s, i, c: (s * nb + i, 0))
    used = pl.BlockSpec((tr, n), lambda s, i, c: (jnp.where(s == c[0], i, 0), 0))
    unused = pl.BlockSpec((tr, n), lambda s, i, c: (jnp.where(s == c[0], 0, i), 0))
    return pl.pallas_call(
        body, name=name, out_shape=[_sds((r, n), F32)] * 4,
        grid_spec=pltpu.PrefetchScalarGridSpec(
            num_scalar_prefetch=1, grid=(2, nb), in_specs=[whole] * 3 + [used, unused],
            out_specs=[whole] * 4),
        compiler_params=_params("arbitrary", "arbitrary"),
    )(where, w, m, v, mine, theirs)


PAIRS = D_ATTN // BAND


def _in_proj(x, g, w, gq, gk, ones_bd, tm):
    t, dm = x.shape
    n = w.shape[1]
    nd = len(DILATIONS)
    first = 3 * D_CONV

    def body(x_ref, g_ref, w_ref, gq_ref, gk_ref, bd_ref, h_ref, z_ref, *refs):
        outs, slab = refs[:3 * nd], refs[3 * nd]
        xv = x_ref[...]
        h = (xv * _rms_scale(xv) * g_ref[...]).astype(BF16)
        h_ref[...] = h
        for cols in _column_chunks(n):
            z_ref[:, cols] = _dot(h, w_ref[:, cols])
        bd = bd_ref[...]
        q = z_ref[:, first:first + D_ATTN]
        k = z_ref[:, first + D_ATTN:first + 2 * D_ATTN]
        vals = [(q * _head_rms_scale(q, bd) * gq_ref[...]) * HEAD_DIM ** -0.5,
                k * _head_rms_scale(k, bd) * gk_ref[...], z_ref[:, first + 2 * D_ATTN:n]]
        for m, val in enumerate(vals):
            for c in range(PAIRS):
                slab[c] = val[:, c * BAND:(c + 1) * BAND]
            for a, d in enumerate(DILATIONS):
                o_ref = outs[m * nd + a]
                for c in range(PAIRS):
                    for r in range(d):
                        rows = slab.at[c][pl.ds(r, tm // d, stride=d), :] if d > 1 else slab[c]
                        o_ref[c, r] = rows.astype(BF16)

    row = pl.BlockSpec((tm, dm), lambda i: (i, 0))
    vec = pl.BlockSpec((1, D_ATTN), lambda i: (0, 0))
    return pl.pallas_call(
        body, name="in_proj", grid=(t // tm,),
        out_shape=[_sds((t, dm), BF16), _sds((t, n), F32)]
        + [_sds((PAIRS, d, t // d, BAND), BF16) for _ in range(3) for d in DILATIONS],
        in_specs=[row, pl.BlockSpec((1, dm), lambda i: (0, 0)),
                  pl.BlockSpec((dm, n), lambda i: (0, 0), pipeline_mode=_resident(True)), vec, vec,
                  pl.BlockSpec((D_ATTN, D_ATTN), lambda i: (0, 0), pipeline_mode=_resident(True))],
        out_specs=[row, pl.BlockSpec((tm, n), lambda i: (i, 0))]
        + [pl.BlockSpec((PAIRS, d, tm // d, BAND), lambda i: (0, 0, i, 0))
           for _ in range(3) for d in DILATIONS],
        scratch_shapes=[pltpu.VMEM((PAIRS, tm, BAND), F32)],
        compiler_params=_params("parallel"),
    )(x, g, w, gq, gk, ones_bd)


TOK = 2048
UNITS = TOK // BAND


def _stack_masks():
    row = lax.broadcasted_iota(jnp.int32, (2 * BAND, 2 * BAND), 0) & (BAND - 1)
    col = lax.broadcasted_iota(jnp.int32, (2 * BAND, 2 * BAND), 1)
    lane = lax.broadcasted_iota(jnp.int32, (BAND, BAND), 1)
    head0 = lane < HEAD_DIM
    ones = [jnp.where(head0, 1.0, 0.0).astype(BF16), jnp.where(head0, 0.0, 1.0).astype(BF16)]
    return col - row, col, head0, ones


def _split3(x):
    hi = x.astype(BF16).astype(F32)
    mid = (x - hi).astype(BF16).astype(F32)
    return hi, mid, x - hi - mid


def _gather(srcs, dst, d):
    per = TOK // d
    at = 0
    for r in range(d):
        for src in srcs:
            rows = src[pl.ds(r, per, stride=d), :] if d > 1 else src[...]
            dst[pl.ds(at, per), :] = rows.astype(dst.dtype)
            at += per


def _scatter_add(out_ref, src, d, per_src, offset, first):
    per = TOK // d
    if d == 1:
        val = src[pl.ds(offset, per), :]
        out_ref[...] = val if first else out_ref[...] + val
        return
    for r in range(d):
        val = src[pl.ds(r * per_src + offset, per), :]
        idx = pl.ds(r, per, stride=d)
        out_ref[idx, :] = val if first else out_ref[idx, :] + val


def _dilated_specs(nblk, reverse):
    def at(s):
        return (nblk - 1 - s) if reverse else s
    main = [pl.BlockSpec((1, d, TOK // d, BAND), lambda j, s: (j, 0, at(s), 0)) for d in DILATIONS]
    prev = [pl.BlockSpec((1, d, TOK // d, BAND), lambda j, s: (j, 0, jnp.maximum(at(s) - 1, 0), 0))
            for d in DILATIONS]
    return main, prev


def _window_rows(prev_ref, main_ref, dst, d):
    per = TOK // d
    for r in range(d):
        dst[pl.ds(r * (per + BAND), BAND), :] = prev_ref[0, r, pl.ds(per - BAND, BAND), :]
        dst[pl.ds(r * (per + BAND) + BAND, per), :] = main_ref[0, r]


def _attn_fwd(qs, ks, vs, carry=None):
    t = qs[0].shape[2]
    nblk = t // TOK
    nd = len(DILATIONS)

    def body(*refs):
        q_refs, kp_refs, k_refs = refs[:nd], refs[nd:2 * nd], refs[2 * nd:3 * nd]
        vp_refs, v_refs = refs[3 * nd:4 * nd], refs[4 * nd:5 * nd]
        y_ref, l_ref, kw_s, vw_s, ob, lb, on, ln = refs[5 * nd:]
        i = pl.program_id(1)
        diff, col, head0, hm = _stack_masks()
        band_ok = jnp.logical_and(diff >= 0, diff <= BAND)
        for g, d in enumerate(DILATIONS):
            per = TOK // d
            nb = per // BAND
            pad = per + BAND
            _window_rows(kp_refs[g], k_refs[g], kw_s, d)
            _window_rows(vp_refs[g], v_refs[g], vw_s, d)
            q_ref = q_refs[g]

            def unit(u, carry):
                r, b = u // nb, u % nb
                qu = q_ref[0, r, pl.ds(pl.multiple_of(b * BAND, BAND), BAND), :]
                start = pl.multiple_of(r * pad + b * BAND, BAND)
                kw = kw_s[pl.ds(start, 2 * BAND), :]
                vw = vw_s[pl.ds(start, 2 * BAND), :]
                lo = jnp.where(jnp.logical_and(i == 0, b == 0), BAND, 0)
                s = _dot_nt(jnp.concatenate([qu * hm[0], qu * hm[1]], axis=0), kw)
                s = jnp.where(jnp.logical_and(band_ok, col >= lo), s, NEG)
                mx = jnp.max(s, axis=-1, keepdims=True)
                e = jnp.exp(s - mx)
                den = jnp.sum(e, axis=-1, keepdims=True)
                o2 = _dot(e.astype(BF16), vw) / den
                l2 = jnp.broadcast_to(mx + jnp.log(den), (2 * BAND, BAND))
                rows = pl.ds(pl.multiple_of(u * BAND, BAND), BAND)
                ob[rows, :] = jnp.where(head0, o2[:BAND], o2[BAND:])
                lb[rows, :] = jnp.where(head0, l2[:BAND], l2[BAND:])
                return carry

            lax.fori_loop(0, UNITS, unit, 0, unroll=16)
            _scatter_add(on.at[g], ob, d, per, 0, True)
            _scatter_add(ln.at[g], lb, d, per, 0, True)
        ls = [ln[0], ln[1], ln[2]]
        mx = jnp.maximum(jnp.maximum(ls[0], ls[1]), ls[2])
        es = [jnp.exp(l - mx) for l in ls]
        tot = es[0] + es[1] + es[2]
        y_ref[...] = (es[0] * on[0] + es[1] * on[1] + es[2] * on[2]) / tot
        l_ref[...] = mx + jnp.log(tot)

    main, prev = _dilated_specs(nblk, False)
    out = pl.BlockSpec((TOK, BAND), lambda j, i: (i, j))
    win_rows = max(d * (TOK // d + BAND) for d in DILATIONS)
    return _call(
        body, list(qs) + list(ks) + list(ks) + list(vs) + list(vs), name="attn_fwd",
        grid=(PAIRS, nblk), out_shape=[_sds((t, D_ATTN), F32)] * 2,
        in_specs=main + prev + main + prev + main, out_specs=[out, out],
        scratch_shapes=[pltpu.VMEM((win_rows, BAND), BF16)] * 2 + [pltpu.VMEM((TOK, BAND), F32)] * 2
        + [pltpu.VMEM((nd, TOK, BAND), F32)] * 2,
        semantics=("parallel", "parallel"), carry=carry)


def _attn_bwd(qs, ks, vs, do, lse, dd, carry=None):
    t = qs[0].shape[2]
    nblk = t // TOK
    nd = len(DILATIONS)
    offs = [sum(DILATIONS[:g]) * BAND for g in range(nd)]

    def body(*refs):
        q_refs, kp_refs, k_refs = refs[:nd], refs[nd:2 * nd], refs[2 * nd:3 * nd]
        vp_refs, v_refs = refs[3 * nd:4 * nd], refs[4 * nd:5 * nd]
        (do_ref, l_ref, d_ref, dq_ref, dk_ref, dv_ref, kw_s, vw_s, dos, lds, pn, dqb, dkb, dvb, ckb,
         cvb) = refs[5 * nd:]
        step = pl.program_id(1)
        i = nblk - 1 - step
        key = lax.broadcasted_iota(jnp.int32, (2 * BAND, 2 * BAND), 0)
        qry = lax.broadcasted_iota(jnp.int32, (2 * BAND, 2 * BAND), 1) & (BAND - 1)
        off = key - qry
        band_ok = jnp.logical_and(off >= 0, off <= BAND)
        lane = lax.broadcasted_iota(jnp.int32, (BAND, BAND), 1)
        head0 = lane < HEAD_DIM
        hm = [jnp.where(head0, 1.0, 0.0).astype(BF16), jnp.where(head0, 0.0, 1.0).astype(BF16)]
        lane2 = lax.broadcasted_iota(jnp.int32, (2 * BAND, BAND), 1) & (HEAD_DIM - 1)
        ones_l = jnp.where(lane2 < 3, 1.0, 0.0).astype(BF16)
        ones_d = jnp.where(jnp.logical_and(lane2 >= 3, lane2 < 6), 1.0, 0.0).astype(BF16)
        piece = lax.broadcasted_iota(jnp.int32, (TOK, BAND), 1) & (HEAD_DIM - 1)

        def pieces(x, at):
            hi, mid, lo = _split3(-x)
            return jnp.where(piece == at, hi,
                             jnp.where(piece == at + 1, mid, jnp.where(piece == at + 2, lo, 0.0)))

        pn[...] = pieces(l_ref[...], 0) + pieces(d_ref[...], 3)
        for g, d in enumerate(DILATIONS):
            per = TOK // d
            nb = per // BAND
            pad = per + BAND
            _window_rows(kp_refs[g], k_refs[g], kw_s, d)
            _window_rows(vp_refs[g], v_refs[g], vw_s, d)
            _gather([do_ref], dos, d)
            _gather([pn], lds, d)
            for r in range(d):
                spare = pl.ds(r * pad, BAND)
                dkb[spare, :] = jnp.zeros((BAND, BAND), F32)
                dvb[spare, :] = jnp.zeros((BAND, BAND), F32)
            q_ref = q_refs[g]

            def unit(u, c_):
                r, b = u // nb, u % nb
                rows = pl.ds(pl.multiple_of(u * BAND, BAND), BAND)
                qu = q_ref[0, r, pl.ds(pl.multiple_of(b * BAND, BAND), BAND), :]
                dou, ldu = dos[rows, :], lds[rows, :]
                q2 = jnp.concatenate([qu * hm[0], qu * hm[1]], axis=0)
                do2 = jnp.concatenate([dou * hm[0], dou * hm[1]], axis=0)
                ld2 = jnp.concatenate([ldu * hm[0], ldu * hm[1]], axis=0)
                acc = pl.ds(pl.multiple_of(r * pad + b * BAND, BAND), 2 * BAND)
                kw = kw_s[acc, :]
                vw = vw_s[acc, :]
                lo = jnp.where(jnp.logical_and(i == 0, b == 0), BAND, 0)
                ok = jnp.logical_and(band_ok, key >= lo)
                st = _dot_nt(jnp.concatenate([kw, ones_l], axis=1), jnp.concatenate([q2, ld2], axis=1))
                dpt = _dot_nt(jnp.concatenate([vw, ones_d], axis=1), jnp.concatenate([do2, ld2], axis=1))
                pt = jnp.where(ok, jnp.exp(st), 0.0)
                dst = (pt * dpt).astype(BF16)
                low = pl.ds(pl.multiple_of(r * pad + b * BAND, BAND), BAND)
                high = pl.ds(pl.multiple_of(r * pad + (b + 1) * BAND, BAND), BAND)
                dkw = _dot(dst, q2)
                dvw = _dot(pt.astype(BF16), do2)
                dkb[low, :] += dkw[:BAND]
                dvb[low, :] += dvw[:BAND]
                dkb[high, :] = dkw[BAND:]
                dvb[high, :] = dvw[BAND:]
                dq2 = _dot_tn(dst, kw)
                dqb[rows, :] = jnp.where(head0, dq2[:BAND], dq2[BAND:])
                return c_

            lax.fori_loop(0, UNITS, unit, 0, unroll=16)

            for r in range(d):
                last = pl.ds(r * pad + per, BAND)
                kept = pl.ds(offs[g] + r * BAND, BAND)

                @pl.when(step > 0)
                def _():
                    dkb[last, :] += ckb[kept, :]
                    dvb[last, :] += cvb[kept, :]

                ckb[kept, :] = dkb[pl.ds(r * pad, BAND), :]
                cvb[kept, :] = dvb[pl.ds(r * pad, BAND), :]
            _scatter_add(dq_ref, dqb, d, per, 0, g == 0)
            _scatter_add(dk_ref, dkb, d, pad, BAND, g == 0)
            _scatter_add(dv_ref, dvb, d, pad, BAND, g == 0)

    main, prev = _dilated_specs(nblk, True)
    tok = pl.BlockSpec((TOK, BAND), lambda j, s: (nblk - 1 - s, j))
    acc_rows = max(d * (TOK // d + BAND) for d in DILATIONS)
    kept_rows = sum(DILATIONS) * BAND
    return _call(
        body, list(qs) + list(ks) + list(ks) + list(vs) + list(vs) + [do, lse, dd], name="attn_bwd",
        grid=(PAIRS, nblk), out_shape=[_sds((t, D_ATTN), F32)] * 3,
        in_specs=main + prev + main + prev + main + [tok] * 3, out_specs=[tok] * 3,
        scratch_shapes=[pltpu.VMEM((acc_rows, BAND), BF16)] * 2 + [pltpu.VMEM((TOK, BAND), BF16)] * 2
        + [pltpu.VMEM((TOK, BAND), F32)] * 2 + [pltpu.VMEM((acc_rows, BAND), F32)] * 2
        + [pltpu.VMEM((kept_rows, BAND), F32)] * 2,
        semantics=("parallel", "arbitrary"), carry=carry)


def _halo_rows(tm, t):
    per = tm // 8
    prev = lambda i: (jnp.maximum(i * per - 1, 0), 0)
    nxt = lambda i: (jnp.minimum((i + 1) * per, t // 8 - 1), 0)
    return prev, nxt


def _mixer_out(z, cw, y_attn, g_conv, g_attn, tm, carry=None):
    t = z.shape[0]
    prev, _ = _halo_rows(tm, t)

    def body(z_ref, zp_ref, cw_ref, y_ref, gc_ref, ga_ref, mix_ref):
        i = pl.program_id(0)
        keep = jnp.where(i > 0, 1.0, 0.0)
        cu = jnp.concatenate([zp_ref[:, 0:512] * zp_ref[:, 1024:1536] * keep,
                              z_ref[:, 0:512] * z_ref[:, 1024:1536]], axis=0)
        c = (cw_ref[0:1, :] * pltpu.roll(cu, 2, 0) + cw_ref[1:2, :] * pltpu.roll(cu, 1, 0)
             + cw_ref[2:3, :] * cu)[8:, :]
        yc = z_ref[:, 512:1024] * c
        mix_ref[:, 0:512] = (yc * _rms_scale(yc) * gc_ref[...]).astype(BF16)
        ya = y_ref[...]
        mix_ref[:, 512:1024] = (ya * _rms_scale(ya) * ga_ref[...]).astype(BF16)

    blk = pl.BlockSpec((tm, 512), lambda i: (i, 0))
    vec = pl.BlockSpec((1, 512), lambda i: (0, 0))
    return _call(
        body, [z, z, cw, y_attn, g_conv, g_attn], name="mixer_out", grid=(t // tm,),
        out_shape=_sds((t, 1024), BF16),
        in_specs=[pl.BlockSpec((tm, 1536), lambda i: (i, 0)), pl.BlockSpec((8, 1536), prev),
                  pl.BlockSpec((8, 512), lambda i: (0, 0)), blk, vec, vec],
        out_specs=pl.BlockSpec((tm, 1024), lambda i: (i, 0)),
        semantics=("parallel",), carry=carry)


def _mixer_bwd(z, dx1, wout, y_attn, cw, g_conv, g_attn, ones_bd, tm, carry=None):
    t = z.shape[0]
    nblk = t // tm
    prev, nxt = _halo_rows(tm, t)
    e = tm + 16

    def body(z_ref, zp_ref, zn_ref, dx_ref, dxn_ref, w_ref, y_ref, cw_ref, gc_ref, ga_ref, bd_ref,
             dz_ref, do_ref, dd_ref, dcw_ref, dgc_ref, dga_ref):
        i = pl.program_id(0)
        dm = _dot_nt(dx_ref[...], w_ref[...])
        dmn = _dot_nt(dxn_ref[...], w_ref[0:D_CONV, :])[0:8, :]
        rows = lax.broadcasted_iota(jnp.int32, (e, 1), 0)
        lo = jnp.where(i > 0, 0, 8)
        hi = jnp.where(i < nblk - 1, e, tm + 8)
        ze = jnp.concatenate([zp_ref[...], z_ref[...], zn_ref[...]], axis=0)
        u, gb, gcv = ze[:, 0:512], ze[:, 512:1024], ze[:, 1024:1536]
        w0, w1, w2 = cw_ref[0:1, :], cw_ref[1:2, :], cw_ref[2:3, :]
        cu = jnp.where(rows >= lo, gcv * u, 0.0)
        cu1, cu2 = pltpu.roll(cu, 1, 0), pltpu.roll(cu, 2, 0)
        c = w0 * cu2 + w1 * cu1 + w2 * cu
        yc = gb * c
        dma = jnp.concatenate([jnp.zeros((8, 512), F32), dm[:, 0:512], dmn], axis=0)
        dyc, ych = _rms_bwd(yc, _rms_scale(yc), gc_ref[...], dma)
        dc = jnp.where(jnp.logical_and(rows >= 8, rows < hi), dyc * gb, 0.0)
        dcu = w0 * pltpu.roll(dc, e - 2, 0) + w1 * pltpu.roll(dc, e - 1, 0) + w2 * dc
        mid = slice(8, 8 + tm)
        dz_ref[:, 0:512] = (dcu * gcv)[mid, :].astype(BF16)
        dz_ref[:, 512:1024] = (dyc * c)[mid, :].astype(BF16)
        dz_ref[:, 1024:1536] = (dcu * u)[mid, :].astype(BF16)

        ya = y_ref[...]
        dmb = dm[:, 512:1024]
        dya, yah = _rms_bwd(ya, _rms_scale(ya), ga_ref[...], dmb)
        do_ref[...] = dya
        dd_ref[...] = _head_sum(dya * ya, bd_ref[...])

        @pl.when(i == 0)
        def _():
            dcw_ref[...] = jnp.zeros_like(dcw_ref)
            dgc_ref[...] = jnp.zeros_like(dgc_ref)
            dga_ref[...] = jnp.zeros_like(dga_ref)

        dcm = jnp.where(rows < tm + 8, dc, 0.0)
        dcw_ref[0:1, :] += jnp.sum(dcm * cu2, axis=0, keepdims=True)
        dcw_ref[1:2, :] += jnp.sum(dcm * cu1, axis=0, keepdims=True)
        dcw_ref[2:3, :] += jnp.sum(dcm * cu, axis=0, keepdims=True)
        dgc_ref[...] += jnp.sum((dma * ych)[mid, :], axis=0, keepdims=True)
        dga_ref[...] += jnp.sum(dmb * yah, axis=0, keepdims=True)

    blk = pl.BlockSpec((tm, 512), lambda i: (i, 0))
    vec = pl.BlockSpec((1, 512), lambda i: (0, 0))
    cwb = pl.BlockSpec((8, 512), lambda i: (0, 0))
    next16 = lambda i: (jnp.minimum((i + 1) * (tm // 16), t // 16 - 1), 0)
    return _call(
        body, [z, z, z, dx1, dx1, wout, y_attn, cw, g_conv, g_attn, ones_bd], name="mixer_bwd",
        grid=(nblk,),
        out_shape=[_sds((t, D_IN), BF16), _sds((t, 512), F32), _sds((t, 512), F32),
                   _sds((8, 512), F32), _sds((1, 512), F32), _sds((1, 512), F32)],
        in_specs=[pl.BlockSpec((tm, 1536), lambda i: (i, 0)), pl.BlockSpec((8, 1536), prev),
                  pl.BlockSpec((8, 1536), nxt), pl.BlockSpec((tm, D_MODEL), lambda i: (i, 0)),
                  pl.BlockSpec((16, D_MODEL), next16),
                  pl.BlockSpec(wout.shape, lambda i: (0, 0), pipeline_mode=_resident(True)),
                  blk, cwb, vec, vec, pl.BlockSpec((512, 512), lambda i: (0, 0))],
        out_specs=[pl.BlockSpec((tm, 1536), lambda i: (i, 0)), blk, blk, cwb, vec, vec],
        carry=carry)


def _qkv_bwd(z, dz, dqn, dkn, dv, gq, gk, ones_bd, tm, carry=None):
    t = z.shape[0]

    def body(zq_ref, zk_ref, _, dqn_ref, dkn_ref, dv_ref, gq_ref, gk_ref, bd_ref,
             dz_ref, dgq_ref, dgk_ref):
        bd = bd_ref[...]

        @pl.when(pl.program_id(0) == 0)
        def _():
            dgq_ref[...] = jnp.zeros_like(dgq_ref)
            dgk_ref[...] = jnp.zeros_like(dgk_ref)

        def back(v, dn, g, scale):
            r = _head_rms_scale(v, bd)
            vh = v * r
            dh = dn * (g * scale)
            dv = r * (dh - vh * (_head_sum(dh * vh, bd) * (1.0 / HEAD_DIM)))
            return dv, jnp.sum(dn * scale * vh, axis=0, keepdims=True)

        dq, dgq = back(zq_ref[...], dqn_ref[...], gq_ref[...], HEAD_DIM ** -0.5)
        dk, dgk = back(zk_ref[...], dkn_ref[...], gk_ref[...], 1.0)
        dgq_ref[...] += dgq
        dgk_ref[...] += dgk
        dz_ref[:, 0:512] = dq.astype(BF16)
        dz_ref[:, 512:1024] = dk.astype(BF16)
        dz_ref[:, 1024:1536] = dv_ref[...].astype(BF16)

    blk = pl.BlockSpec((tm, 512), lambda i: (i, 0))
    vec = pl.BlockSpec((1, 512), lambda i: (0, 0))
    return _call(
        body, [z, z, dz, dqn, dkn, dv, gq, gk, ones_bd], name="qkv_bwd", grid=(t // tm,),
        out_shape=[_sds((t, D_IN), BF16), _sds((1, 512), F32), _sds((1, 512), F32)],
        in_specs=[pl.BlockSpec((tm, 512), lambda i: (i, 3)), pl.BlockSpec((tm, 512), lambda i: (i, 4)),
                  ANY] + [blk] * 3 + [vec, vec, pl.BlockSpec((512, 512), lambda i: (0, 0))],
        out_specs=[pl.BlockSpec((tm, 1536), lambda i: (i, 1)), vec, vec],
        carry=carry, aliases={2: 0})


def _columns_from_chips(g):
    return g.transpose(1, 0, 2).reshape(g.shape[1], N_CHIPS * g.shape[2])


def kernel(x, g_mix, w_in, conv_w, g_q, g_k, g_conv_out, g_attn_out, w_out, g_ffn, w_gate, w_up, w_down, loss_target, m_g_mix, m_w_in, m_conv_w, m_g_q, m_g_k, m_g_conv_out, m_g_attn_out, m_w_out, m_g_ffn, m_w_gate, m_w_up, m_w_down, v_g_mix, v_w_in, v_conv_w, v_g_q, v_g_k, v_g_conv_out, v_g_attn_out, v_w_out, v_g_ffn, v_w_gate, v_w_up, v_w_down):
    t = x.shape[1]
    xs = x[0]
    target = loss_target[0]
    tm = min(512, t)
    tmm = min(1024, t)

    cw_pad = jnp.pad(conv_w[0], ((0, 13), (0, 0)))
    gathered = _all_gather([w_in[0].astype(BF16), cw_pad])
    win = _columns_from_chips(gathered[0])
    cw = jnp.pad(gathered[1][:, 0:3, :].transpose(1, 0, 2).reshape(3, D_CONV), ((0, 5), (0, 0)))
    later = [w_out[0].astype(BF16), w_gate[0].T.astype(BF16), w_up[0].T.astype(BF16),
             w_down[0].astype(BF16)]

    head_id = jnp.arange(D_ATTN) // HEAD_DIM
    ones_bd = (head_id[:, None] == head_id[None, :]).astype(BF16)
    gq_t = jnp.tile(g_q, (1, D_ATTN // HEAD_DIM))
    gk_t = jnp.tile(g_k, (1, D_ATTN // HEAD_DIM))

    h1, z, *dilated = _in_proj(xs, g_mix, win, gq_t, gk_t, ones_bd, tm)
    nd = len(DILATIONS)
    qs, ks, vs = dilated[:nd], dilated[nd:2 * nd], dilated[2 * nd:]
    (y_attn, lse), gathered = _attn_fwd(qs, ks, vs, carry=_x_gather_chips(later))
    mix, gathered = _mixer_out(z, cw, y_attn, g_conv_out, g_attn_out, tm,
                               carry=_x_gather_sibling(gathered))
    wout = gathered[0].reshape(D_MODEL, D_MODEL)
    wgate_t = gathered[1].reshape(D_FF, D_MODEL)
    wup_t = gathered[2].reshape(D_FF, D_MODEL)
    wdown = gathered[3].reshape(D_FF, D_MODEL)
    (x1,) = _matmul("out_proj", mix, wout, [xs], [F32], lambda acc, r: (r + acc,), tm, D_MODEL)
    h2, gate, up, act = _norm_matmul("ffn_up", x1, g_ffn, [wgate_t, wup_t], tm, D_FF, True, BF16,
                                     transposed_w=True)

    def loss_epilogue(acc, r, tgt):
        err = r + acc - tgt
        dy = err * (1.0 / D_MODEL)
        return dy, dy, jnp.sum(err * err)

    dx2, dx2b, loss_sum = _matmul("ffn_down_loss", act, wdown, [x1, target], [F32, BF16],
                                  loss_epilogue, tm, D_MODEL, loss=True)

    def swiglu_bwd(da, gt, u):
        gt, u = gt.astype(F32), u.astype(F32)
        s = _sigmoid(gt)
        return da * u * (s * (1.0 + gt * (1.0 - s))), da * (gt * s)

    dgate, dup = _matmul("ffn_down_bwd", dx2b, wdown, [gate, up], [BF16, BF16], swiglu_bwd,
                         tm, D_FF, transposed_w=True)
    gw_down = _matmul_tn("grad_w_down", act, dx2b, 512, tmm)
    gw_gate_t = _matmul_tn("grad_w_gate", dgate, h2, 512, tmm)
    gw_up_t = _matmul_tn("grad_w_up", dup, h2, 512, tmm)

    me = 2 * lax.axis_index("x") + lax.axis_index("y")
    where = jnp.stack([lax.axis_index("c"), me]).astype(jnp.int32)

    def pair_sums(names, full, got):
        return [_pair_sum(f"pair_sum_{nme}", a, b, where) for nme, a, b in zip(names, full, got)]

    def chip_sums(names, pair, got):
        return [_chip_sum(f"chip_sum_{nme}", own, b) for nme, (_, own), b in zip(names, pair, got)]

    ffn = ["w_gate", "w_up", "w_down"]
    full = [g.reshape(N_CHIPS, D_FF // N_CHIPS, D_MODEL) for g in (gw_gate_t, gw_up_t, gw_down)]
    (dx1, dx1b, gg_ffn), got = _matmul_norm_bwd(
        "ffn_up_bwd", [(dgate, wgate_t), (dup, wup_t)], x1, dx2, g_ffn, tm, carry=_x_pair(full),
        transposed_w=False)
    pair = pair_sums(ffn, full, got)
    gw_out = _matmul_tn("grad_w_out", mix, dx1b, 512, tmm)
    full = [gw_out.reshape(N_CHIPS, D_MODEL // N_CHIPS, D_MODEL)]
    (dzc, do, dd, gcw, gg_conv, gg_attn), got = _mixer_bwd(
        z, dx1b, wout, y_attn, cw, g_conv_out, g_attn_out, ones_bd, tm, carry=_x_pair(full))
    pair += pair_sums(["w_out"], full, got)
    early = ffn + ["w_out"]
    (dqn, dkn, dv), got = _attn_bwd(qs, ks, vs, do, lse, dd, carry=_x_chips([p for p, _ in pair]))
    mine = chip_sums(early, pair, got)
    (dz, gg_q, gg_k), theirs = _qkv_bwd(z, dzc, dqn, dkn, dv, gq_t, gk_t, ones_bd, tm,
                                        carry=_x_share(mine))
    full = [_matmul_tn("grad_w_in", h1, dz, D_IN // N_CHIPS, tmm, by_chip=True)]
    got = _exchange_alone("grad_pair_exchange_w_in", _x_pair(full))
    pair = pair_sums(["w_in"], full, got)
    (grad_x, _, gg_mix), got = _matmul_norm_bwd("in_proj_bwd", [(dz, win)], xs, dx1, g_mix, tm,
                                                carry=_x_chips([pair[0][0]]))
    mine += chip_sums(["w_in"], pair, got)
    theirs = list(theirs) + list(_exchange_alone("grad_pair_share_w_in", _x_share(mine[-1:])))
    big = early + ["w_in"]

    small = _small_all_reduce({
        "g_mix": gg_mix, "g_ffn": gg_ffn, "g_conv_out": gg_conv, "g_attn_out": gg_attn,
        "g_q": gg_q, "g_k": gg_k, "loss": loss_sum, "conv_w": gcw})
    heads = D_ATTN // HEAD_DIM
    grads = {
        "g_mix": small[0:1, :], "g_ffn": small[1:2, :],
        "g_conv_out": small[2:3, 0:512], "g_attn_out": small[2:3, 512:1024],
        "g_q": small[3, 0:512].reshape(heads, HEAD_DIM).sum(axis=0)[None, :],
        "g_k": small[3, 512:1024].reshape(heads, HEAD_DIM).sum(axis=0)[None, :],
        "conv_w": lax.dynamic_slice(small[8:11, 0:512], (0, me * (D_CONV // N_CHIPS)),
                                    (3, D_CONV // N_CHIPS)),
    }
    halves = dict(zip(big, zip(mine, theirs)))
    loss = small[4, 0] * 0.5 * (1.0 / D_MODEL)

    weights = dict(g_mix=g_mix, w_in=w_in, conv_w=conv_w, g_q=g_q, g_k=g_k, g_conv_out=g_conv_out,
                   g_attn_out=g_attn_out, w_out=w_out, g_ffn=g_ffn, w_gate=w_gate, w_up=w_up,
                   w_down=w_down)
    moments_m = dict(g_mix=m_g_mix, w_in=m_w_in, conv_w=m_conv_w, g_q=m_g_q, g_k=m_g_k,
                     g_conv_out=m_g_conv_out, g_attn_out=m_g_attn_out, w_out=m_w_out, g_ffn=m_g_ffn,
                     w_gate=m_w_gate, w_up=m_w_up, w_down=m_w_down)
    moments_v = dict(g_mix=v_g_mix, w_in=v_w_in, conv_w=v_conv_w, g_q=v_g_q, g_k=v_g_k,
                     g_conv_out=v_g_conv_out, g_attn_out=v_g_attn_out, w_out=v_w_out, g_ffn=v_g_ffn,
                     w_gate=v_w_gate, w_up=v_w_up, w_down=v_w_down)
    names = list(weights)
    out_g, out_d, out_m, out_v = [], [], [], []
    for nme in names:
        wgt = weights[nme]
        shape2 = wgt.shape[-2:] if wgt.ndim == 3 else wgt.shape
        flip = nme in ("w_gate", "w_up")

        def to2d(a):
            return a.reshape(shape2).T if flip else a.reshape(shape2)

        def back(a):
            return (a.T if flip else a).reshape(wgt.shape)

        state = (to2d(wgt), to2d(moments_m[nme]), to2d(moments_v[nme]))
        if nme in halves:
            g2, dlt, nm, nv = _adamw_shard(f"adamw_{nme}", *state, *halves[nme], where)
        else:
            g2 = grads[nme].reshape(shape2)
            dlt, nm, nv = _adamw(f"adamw_{nme}", state[0], g2, state[1], state[2])
        out_g.append(back(g2))
        out_d.append(back(dlt))
        out_m.append(back(nm))
        out_v.append(back(nv))
    return (loss, grad_x[None], *out_g, *out_d, *out_m, *out_v)
```

```python
import functools
from typing import Any, Callable, NamedTuple, Sequence

import jax
import jax.numpy as jnp
from jax import lax
from jax.experimental import pallas as pl
from jax.experimental.pallas import tpu as pltpu

F32 = jnp.float32
BF16 = jnp.bfloat16
MESH = pl.DeviceIdType.MESH

D_MODEL = 1024
D_CONV = 512
D_ATTN = 512
HEAD_DIM = 64
D_FF = 2816
D_IN = 3 * D_CONV + 3 * D_ATTN
DILATIONS = (1, 4, 16)
BAND = 128
EPS = 1e-6
NEG = -1e30
N_CHIPS = 4

ADAM_LR = 0.001
ADAM_B1 = 0.9
ADAM_B2 = 0.999
ADAM_EPS = 1e-08
ADAM_WD = 0.01
ADAM_STEP = 10

V7X_VMEM_BYTES = 64 * 1024 * 1024
VMEM_LIMIT = V7X_VMEM_BYTES - 8 * 1024 * 1024
ANY = pl.BlockSpec(memory_space=pl.ANY)
VMEM_WHOLE = pl.BlockSpec(memory_space=pltpu.VMEM)


def _params(*sem):
    return pltpu.CompilerParams(dimension_semantics=sem, vmem_limit_bytes=VMEM_LIMIT)


def _sds(shape, dtype):
    return jax.ShapeDtypeStruct(shape, dtype)


def _resident(whole):
    return pl.Buffered(1) if whole else None


def _place():
    x, y, c = lax.axis_index("x"), lax.axis_index("y"), lax.axis_index("c")
    chips = [(1 - x, y), (x, 1 - y), (1 - x, 1 - y)]
    return x, y, c, 2 * x + y, chips, [2 * cx + cy for cx, cy in chips]


def _all_gather(shards):
    n = len(shards)

    def body(*refs):
        ins, outs, stage = refs[:n], refs[n:2 * n], refs[2 * n:3 * n]
        ssem, rsem, fsem, gsem, lsem, osem = refs[3 * n:]
        x, y, c, me, chips, cids = _place()
        sib = (x, y, 1 - c)

        def half(w, which):
            h = shards[w].shape[0] // 2
            return pl.ds(pl.multiple_of(which * h, 8), h)

        loads = [pltpu.make_async_copy(ins[w], stage[w], lsem.at[w]) for w in range(n)]
        local = [pltpu.make_async_copy(stage[w], outs[w].at[me], osem.at[w]) for w in range(n)]
        for cp in loads:
            cp.start()

        def chip_copy(w, j, src_slot):
            rows = half(w, c)
            return pltpu.make_async_remote_copy(
                src_ref=ins[w].at[rows], dst_ref=outs[w].at[src_slot, rows],
                send_sem=ssem.at[3 * w + j], recv_sem=rsem.at[3 * w + j],
                device_id=(*chips[j], c), device_id_type=MESH)

        def sib_copy(w, j, which):
            rows = half(w, which)
            return pltpu.make_async_remote_copy(
                src_ref=outs[w].at[cids[j], rows], dst_ref=outs[w].at[cids[j], rows],
                send_sem=fsem.at[3 * w + j], recv_sem=gsem.at[3 * w + j],
                device_id=sib, device_id_type=MESH)

        sends = [chip_copy(w, j, me) for w in range(n) for j in range(3)]
        for cp in sends:
            cp.start()
        for w in range(n):
            loads[w].wait()
            local[w].start()
        passed = []
        for w in range(n):
            for j in range(3):
                chip_copy(w, j, cids[j]).wait_recv()
                cp = sib_copy(w, j, c)
                cp.start()
                passed.append(cp)
        for w in range(n):
            for j in range(3):
                sib_copy(w, j, 1 - c).wait_recv()
        for cp in sends + passed:
            cp.wait_send()
        for cp in local:
            cp.wait()

    return pl.pallas_call(
        body, name="all_gather_weights",
        out_shape=[_sds((N_CHIPS,) + s.shape, s.dtype) for s in shards],
        in_specs=[ANY] * n, out_specs=[ANY] * n,
        scratch_shapes=[pltpu.VMEM(s.shape, s.dtype) for s in shards]
        + [pltpu.SemaphoreType.DMA((3 * n,))] * 4 + [pltpu.SemaphoreType.DMA((n,))] * 2,
        compiler_params=pltpu.CompilerParams(vmem_limit_bytes=VMEM_LIMIT),
    )(*shards)


class _Exchange(NamedTuple):
    srcs: Sequence[Any]
    lands: Sequence[Any]
    outs: Sequence[Any]
    n_sems: int
    copies: Callable


def _remote(src, dst, ssem, rsem, k, to):
    return pltpu.make_async_remote_copy(src_ref=src, dst_ref=dst, send_sem=ssem.at[k],
                                        recv_sem=rsem.at[k], device_id=to, device_id_type=MESH)


def _x_gather_chips(shards):
    def copies(srcs, lands, outs, ssem, rsem):
        _, _, c, me, chips, cids = _place()
        go, arrive = [], []
        for w, s in enumerate(shards):
            h = s.shape[0] // 2
            rows = pl.ds(pl.multiple_of(c * h, 8), h)
            for j in range(3):
                to = (*chips[j], c)
                go.append(_remote(srcs[w].at[rows], lands[w].at[me, rows], ssem, rsem, 3 * w + j, to))
                arrive.append(_remote(srcs[w].at[rows], lands[w].at[cids[j], rows], ssem, rsem,
                                      3 * w + j, to))
        return go, arrive

    lands = [jnp.broadcast_to(s[None], (N_CHIPS,) + s.shape) for s in shards]
    return _Exchange(shards, lands, [], 3 * len(shards), copies)


def _x_gather_sibling(gathered):
    def copies(srcs, lands, outs, ssem, rsem):
        x, y, c, _, _, cids = _place()
        go, arrive = [], []
        for w, g in enumerate(gathered):
            h = g.shape[1] // 2
            mine = pl.ds(pl.multiple_of(c * h, 8), h)
            theirs = pl.ds(pl.multiple_of((1 - c) * h, 8), h)
            for j in range(3):
                slab = lands[w].at[cids[j]]
                go.append(_remote(slab.at[mine], slab.at[mine], ssem, rsem, 3 * w + j, (x, y, 1 - c)))
                arrive.append(_remote(slab.at[theirs], slab.at[theirs], ssem, rsem, 3 * w + j,
                                      (x, y, 1 - c)))
        return go, arrive

    return _Exchange([], gathered, [], 3 * len(gathered), copies)


def _x_pair(grads):
    def copies(srcs, lands, outs, ssem, rsem):
        x, y, c, _, _, _ = _place()
        go = []
        for w, g in enumerate(grads):
            h = g.shape[1] // 2
            theirs = pl.ds(pl.multiple_of((1 - c) * h, 8), h)
            go.append(_remote(srcs[w].at[:, theirs, :], outs[w], ssem, rsem, w, (x, y, 1 - c)))
        return go, go

    outs = [_sds((N_CHIPS, g.shape[1] // 2, g.shape[2]), g.dtype) for g in grads]
    return _Exchange(grads, [], outs, len(grads), copies)


def _x_chips(parts):
    def copies(srcs, lands, outs, ssem, rsem):
        _, _, c, _, chips, cids = _place()
        go = [_remote(srcs[w].at[cids[j]], outs[w].at[j], ssem, rsem, 3 * w + j, (*chips[j], c))
              for w in range(len(parts)) for j in range(3)]
        return go, go

    outs = [_sds((3,) + p.shape[1:], p.dtype) for p in parts]
    return _Exchange(parts, [], outs, 3 * len(parts), copies)


def _x_share(halves):
    def copies(srcs, lands, outs, ssem, rsem):
        x, y, c, _, _, _ = _place()
        go = [_remote(srcs[w], outs[w], ssem, rsem, w, (x, y, 1 - c)) for w in range(len(halves))]
        return go, go

    return _Exchange(halves, [], [_sds(h.shape, h.dtype) for h in halves], len(halves), copies)


def _call(body, args, *, name, grid, in_specs, out_specs, out_shape, scratch_shapes=(),
          semantics=None, carry=None, aliases=None):
    single = not isinstance(out_shape, (list, tuple))
    out_shape = [out_shape] if single else list(out_shape)
    out_specs = [out_specs] if single else list(out_specs)
    aliases = dict(aliases or {})
    if carry is None:
        res = pl.pallas_call(
            body, name=name, grid=grid, in_specs=list(in_specs), out_specs=out_specs,
            out_shape=out_shape, scratch_shapes=list(scratch_shapes), input_output_aliases=aliases,
            compiler_params=_params(*(semantics or ("arbitrary",) * len(grid))))(*args)
        return res[0] if single else res
    n_in, n_out, n_scr = len(args), len(out_shape), len(scratch_shapes)
    n_src, n_land, n_new = len(carry.srcs), len(carry.lands), len(carry.outs)

    def carrying(*refs):
        at = 0
        parts = []
        for n in (n_in, n_src, n_land, n_out, n_land, n_new, n_scr, 2):
            parts.append(refs[at:at + n])
            at += n
        ins, srcs, _, outs, lands, news, scratch, (ssem, rsem) = parts
        ids = [pl.program_id(a) for a in range(len(grid))]
        first = functools.reduce(jnp.logical_and, [i == 0 for i in ids])
        last = functools.reduce(jnp.logical_and, [i == g - 1 for i, g in zip(ids, grid)])
        go, arrive = carry.copies(srcs, lands, news, ssem, rsem)

        @pl.when(first)
        def _():
            for cp in go:
                cp.start()

        body(*ins, *outs, *scratch)

        @pl.when(last)
        def _():
            for cp in go:
                cp.wait_send()
            for cp in arrive:
                cp.wait_recv()

    res = pl.pallas_call(
        carrying, name=name, grid=grid,
        in_specs=list(in_specs) + [ANY] * (n_src + n_land),
        out_specs=out_specs + [ANY] * (n_land + n_new),
        out_shape=out_shape + [_sds(a.shape, a.dtype) for a in carry.lands] + list(carry.outs),
        input_output_aliases={**aliases, **{n_in + n_src + i: n_out + i for i in range(n_land)}},
        scratch_shapes=list(scratch_shapes) + [pltpu.SemaphoreType.DMA((carry.n_sems,))] * 2,
        compiler_params=_params(*(("arbitrary",) * len(grid))))(*args, *carry.srcs, *carry.lands)
    own = res[:n_out]
    return (own[0] if single else own), res[n_out:]


def _exchange_alone(name, exchange):
    def body(x_ref, o_ref):
        o_ref[...] = x_ref[...]

    blk = pl.BlockSpec((8, 128), lambda i: (0, 0))
    _, res = _call(body, [jnp.zeros((8, 128), F32)], name=name, grid=(1,), in_specs=[blk],
                   out_specs=blk, out_shape=_sds((8, 128), F32), carry=exchange)
    return res


def _row_block(r, want):
    return max(d for d in range(1, min(want, r) + 1) if r % d == 0 and (d % 8 == 0 or d == r))


def _pair_sum(name, full, got, where):
    _, r, n = full.shape
    h = r // 2
    tr = _row_block(h, 256)
    nb = h // tr

    def body(w_ref, a_ref, b_ref, o_ref, own_ref):
        total = a_ref[...] + b_ref[...]
        o_ref[...] = total.astype(BF16)

        @pl.when(pl.program_id(1) == w_ref[1])
        def _():
            own_ref[...] = total[0]

    blk = pl.BlockSpec((1, tr, n), lambda i, s, w: (s, i, 0))
    return pl.pallas_call(
        body, name=name, out_shape=[_sds(got.shape, BF16), _sds((h, n), F32)],
        grid_spec=pltpu.PrefetchScalarGridSpec(
            num_scalar_prefetch=1, grid=(nb, N_CHIPS),
            in_specs=[pl.BlockSpec((1, tr, n), lambda i, s, w: (s, w[0] * nb + i, 0)), blk],
            out_specs=[blk, pl.BlockSpec((tr, n), lambda i, s, w: (i, 0))]),
        compiler_params=_params("parallel", "arbitrary"),
    )(where, full, got)


def _chip_sum(name, own, got):
    h, n = own.shape
    tr = _row_block(h, 256)

    def body(a_ref, b0, b1, b2, o_ref):
        o_ref[...] = ((a_ref[...] + b0[0].astype(F32)) + b1[0].astype(F32)) + b2[0].astype(F32)

    def slot(j):
        return pl.BlockSpec((1, tr, n), lambda i: (j, i, 0))

    blk = pl.BlockSpec((tr, n), lambda i: (i, 0))
    return pl.pallas_call(
        body, name=name, grid=(h // tr,), out_shape=_sds((h, n), F32),
        in_specs=[blk, slot(0), slot(1), slot(2)], out_specs=blk,
        compiler_params=_params("parallel"),
    )(own, got, got, got)


SMALL_ROWS = 16
SMALL_LAYOUT = (
    ("g_mix", 0, 0, 1, 1024), ("g_ffn", 1, 0, 1, 1024), ("g_conv_out", 2, 0, 1, 512),
    ("g_attn_out", 2, 512, 1, 512), ("g_q", 3, 0, 1, 512), ("g_k", 3, 512, 1, 512),
    ("loss", 4, 0, 1, 128), ("conv_w", 8, 0, 8, 512))


def _small_all_reduce(parts):
    names = [s[0] for s in SMALL_LAYOUT]

    def body(*refs):
        ins = refs[:len(names)]
        out_ref, stage, buf, ssem, rsem = refs[len(names):]
        x, y, c, _, _, _ = _place()
        me = 4 * x + 2 * y + c
        stage[...] = jnp.zeros_like(stage)
        for ref, (_, r0, c0, nr, nc) in zip(ins, SMALL_LAYOUT):
            stage[r0:r0 + nr, c0:c0 + nc] = ref[0:nr, :]
        buf[me] = stage[...]
        peers = []
        for d in range(1, 8):
            px = 1 - x if d & 4 else x
            py = 1 - y if d & 2 else y
            pc = 1 - c if d & 1 else c
            peers.append(((px, py, pc), 4 * px + 2 * py + pc))
        sends = [pltpu.make_async_remote_copy(
            src_ref=stage, dst_ref=buf.at[me], send_sem=ssem.at[k], recv_sem=rsem.at[k],
            device_id=peer, device_id_type=MESH) for k, (peer, _) in enumerate(peers)]
        for cp in sends:
            cp.start()
        for k, (peer, pid) in enumerate(peers):
            pltpu.make_async_remote_copy(
                src_ref=stage, dst_ref=buf.at[pid], send_sem=ssem.at[k], recv_sem=rsem.at[k],
                device_id=peer, device_id_type=MESH).wait_recv()
        for cp in sends:
            cp.wait_send()
        acc = buf[0]
        for k in range(1, 8):
            acc = acc + buf[k]
        out_ref[...] = acc

    return pl.pallas_call(
        body, name="small_all_reduce", out_shape=_sds((SMALL_ROWS, 1024), F32),
        in_specs=[VMEM_WHOLE] * len(names), out_specs=VMEM_WHOLE,
        scratch_shapes=[pltpu.VMEM((SMALL_ROWS, 1024), F32), pltpu.VMEM((8, SMALL_ROWS, 1024), F32),
                        pltpu.SemaphoreType.DMA((7,)), pltpu.SemaphoreType.DMA((7,))],
    )(*[parts[k] for k in names])


def _dot(a, b):
    return jnp.dot(a, b, preferred_element_type=F32)


def _dot_nt(a, b):
    return lax.dot_general(a, b, (((1,), (1,)), ((), ())), preferred_element_type=F32)


def _dot_tn(a, b):
    return lax.dot_general(a, b, (((0,), (0,)), ((), ())), preferred_element_type=F32)


def _sigmoid(v):
    return 1.0 / (1.0 + jnp.exp(-v))


def _rms_scale(v):
    return lax.rsqrt(jnp.mean(v * v, axis=-1, keepdims=True) + EPS)


def _rms_bwd(v, r, g, dy):
    vh = v * r
    dh = dy * g
    return r * (dh - vh * jnp.mean(dh * vh, axis=-1, keepdims=True)), vh


def _head_sum(a, ones_bd):
    hi = a.astype(BF16)
    lo = (a - hi.astype(F32)).astype(BF16)
    return _dot(hi, ones_bd) + _dot(lo, ones_bd)


def _head_rms_scale(v, ones_bd):
    return lax.rsqrt(_head_sum(v * v, ones_bd) * (1.0 / HEAD_DIM) + EPS)


MXU_COLUMNS = 256


def _column_chunks(n):
    width = MXU_COLUMNS if n % MXU_COLUMNS == 0 else n
    return [slice(c, c + width) for c in range(0, n, width)]


def _norm_matmul(name, x, g, ws, tm, tn, swiglu, out_dtype=F32, transposed_w=False):
    t, d = x.shape
    n = ws[0].shape[0] if transposed_w else ws[0].shape[1]
    nw = len(ws)

    def body(x_ref, g_ref, *refs):
        w_refs, h_ref, o_refs = refs[:nw], refs[nw], refs[nw + 1:2 * nw + 1]
        hs = refs[-1]

        @pl.when(pl.program_id(1) == 0)
        def _():
            xv = x_ref[...]
            h = (xv * _rms_scale(xv) * g_ref[...]).astype(BF16)
            hs[...] = h
            h_ref[...] = h

        h = hs[...]
        for cols in _column_chunks(tn):
            outs = [_dot_nt(h, w[cols, :]) if transposed_w else _dot(h, w[:, cols]) for w in w_refs]
            for o_ref, o in zip(o_refs, outs):
                o_ref[:, cols] = o.astype(out_dtype)
            if swiglu:
                refs[2 * nw + 1][:, cols] = (outs[0] * _sigmoid(outs[0]) * outs[1]).astype(BF16)

    row = pl.BlockSpec((tm, d), lambda i, j: (i, 0))
    col = pl.BlockSpec((tm, tn), lambda i, j: (i, j))
    out_shape = [_sds((t, d), BF16)] + [_sds((t, n), out_dtype)] * nw
    out_specs = [row] + [col] * nw
    if swiglu:
        out_shape.append(_sds((t, n), BF16))
        out_specs.append(col)
    return pl.pallas_call(
        body, name=name, grid=(t // tm, n // tn), out_shape=out_shape,
        in_specs=[row, pl.BlockSpec((1, d), lambda i, j: (0, 0))]
        + [pl.BlockSpec((tn, d), lambda i, j: (j, 0), pipeline_mode=_resident(tn == n))
           if transposed_w
           else pl.BlockSpec((d, tn), lambda i, j: (0, j), pipeline_mode=_resident(tn == n))] * nw,
        out_specs=out_specs, scratch_shapes=[pltpu.VMEM((tm, d), BF16)],
        compiler_params=_params("parallel", "arbitrary"),
    )(x, g, *ws)


def _matmul(name, a, w, extras, out_dtypes, epilogue, tm, tn, transposed_w=False, loss=False):
    t, k = a.shape
    n = w.shape[0] if transposed_w else w.shape[1]
    ne, no = len(extras), len(out_dtypes)

    def body(a_ref, w_ref, *refs):
        e_refs, o_refs = refs[:ne], refs[ne:]
        a = a_ref[...]
        total = 0.0
        for cols in _column_chunks(tn):
            acc = _dot_nt(a, w_ref[cols, :]) if transposed_w else _dot(a, w_ref[:, cols])
            res = epilogue(acc, *[e[:, cols] for e in e_refs])
            for o_ref, r in zip(o_refs[:no], res[:no]):
                o_ref[:, cols] = r.astype(o_ref.dtype)
            if loss:
                total = total + res[no]
        if loss:
            first = jnp.logical_and(pl.program_id(0) == 0, pl.program_id(1) == 0)

            @pl.when(first)
            def _():
                o_refs[no][...] = jnp.zeros_like(o_refs[no])

            o_refs[no][...] += total

    col = pl.BlockSpec((tm, tn), lambda i, j: (i, j))
    w_spec = (pl.BlockSpec((tn, k), lambda i, j: (j, 0), pipeline_mode=_resident(tn == n))
              if transposed_w
              else pl.BlockSpec((k, tn), lambda i, j: (0, j), pipeline_mode=_resident(tn == n)))
    out_shape = [_sds((t, n), dt) for dt in out_dtypes]
    out_specs = [col] * no
    if loss:
        out_shape.append(_sds((8, 128), F32))
        out_specs.append(pl.BlockSpec((8, 128), lambda i, j: (0, 0)))
    return pl.pallas_call(
        body, name=name, grid=(t // tm, n // tn), out_shape=out_shape,
        in_specs=[pl.BlockSpec((tm, k), lambda i, j: (i, 0)), w_spec] + [col] * ne,
        out_specs=out_specs,
        compiler_params=_params(*(("arbitrary", "arbitrary") if loss else ("parallel", "parallel"))),
    )(a, w, *extras)


def _matmul_norm_bwd(name, pairs, x, dres, g, tm, carry=None, transposed_w=True):
    t, d = x.shape
    npairs = len(pairs)
    product = _dot_nt if transposed_w else _dot

    def body(*refs):
        a_refs, w_refs = refs[:npairs], refs[npairs:2 * npairs]
        x_ref, r_ref, g_ref, dx_ref, dxb_ref, dg_ref = refs[2 * npairs:]
        dy = product(a_refs[0][...], w_refs[0][...])
        for a_ref, w_ref in zip(a_refs[1:], w_refs[1:]):
            dy = dy + product(a_ref[...], w_ref[...])
        xv = x_ref[...]
        dx, xh = _rms_bwd(xv, _rms_scale(xv), g_ref[...], dy)
        dx = dx + r_ref[...]
        dx_ref[...] = dx
        dxb_ref[...] = dx.astype(BF16)

        @pl.when(pl.program_id(0) == 0)
        def _():
            dg_ref[...] = jnp.zeros_like(dg_ref)

        dg_ref[...] += jnp.sum(dy * xh, axis=0, keepdims=True)

    row = pl.BlockSpec((tm, d), lambda i: (i, 0))
    vec = pl.BlockSpec((1, d), lambda i: (0, 0))
    return _call(
        body, [a for a, _ in pairs] + [w for _, w in pairs] + [x, dres, g], name=name,
        grid=(t // tm,), out_shape=[_sds((t, d), F32), _sds((t, d), BF16), _sds((1, d), F32)],
        in_specs=[pl.BlockSpec((tm, a.shape[1]), lambda i: (i, 0)) for a, _ in pairs]
        + [pl.BlockSpec(w.shape, lambda i: (0, 0), pipeline_mode=pl.Buffered(1)) for _, w in pairs]
        + [row, row, vec],
        out_specs=[row, row, vec], carry=carry)


def _matmul_tn(name, a, g, tn, tk, by_chip=False):
    t, ka = a.shape
    n = g.shape[1]

    def body(a_ref, g_ref, o_ref):
        @pl.when(pl.program_id(1) == 0)
        def _():
            o_ref[...] = jnp.zeros_like(o_ref)

        acc = _dot_tn(a_ref[...], g_ref[...])
        o_ref[...] += acc[None] if by_chip else acc

    return pl.pallas_call(
        body, name=name, grid=(n // tn, t // tk),
        out_shape=_sds((n // tn, ka, tn) if by_chip else (ka, n), F32),
        in_specs=[pl.BlockSpec((tk, ka), lambda j, s: (s, 0)),
                  pl.BlockSpec((tk, tn), lambda j, s: (s, j))],
        out_specs=(pl.BlockSpec((1, ka, tn), lambda j, s: (j, 0, 0)) if by_chip
                   else pl.BlockSpec((ka, tn), lambda j, s: (0, j))),
        compiler_params=_params("parallel", "arbitrary"),
    )(a, g)


def _elementwise(name, fn, ins, out_dtypes, tr):
    r, n = ins[0].shape
    tr = _row_block(r, tr)
    ni = len(ins)

    def body(*refs):
        res = fn(*[ref[...] for ref in refs[:ni]])
        for o_ref, v in zip(refs[ni:], res):
            o_ref[...] = v.astype(o_ref.dtype)

    blk = pl.BlockSpec((tr, n), lambda i: (i, 0))
    return pl.pallas_call(
        body, name=name, grid=(r // tr,), out_shape=[_sds((r, n), dt) for dt in out_dtypes],
        in_specs=[blk] * ni, out_specs=[blk] * len(out_dtypes),
        compiler_params=_params("parallel"),
    )(*ins)


def _adamw_update(w, g, m, v):
    m = ADAM_B1 * m + (1.0 - ADAM_B1) * g
    v = ADAM_B2 * v + (1.0 - ADAM_B2) * (g * g)
    m_hat = m / (1.0 - ADAM_B1 ** ADAM_STEP)
    v_hat = v / (1.0 - ADAM_B2 ** ADAM_STEP)
    return -ADAM_LR * (m_hat / (jnp.sqrt(v_hat) + ADAM_EPS) + ADAM_WD * w), m, v


def _adamw(name, w, g, m, v):
    return _elementwise(name, _adamw_update, [w, g, m, v], [F32] * 3, 256)


def _adamw_shard(name, w, m, v, mine, theirs, where):
    r, n = w.shape
    h = r // 2
    tr = _row_block(h, 256)
    nb = h // tr

    def body(w_ref, p_ref, m_ref, v_ref, a_ref, b_ref, g_ref, d_ref, nm_ref, nv_ref):
        g = jnp.where(pl.program_id(0) == w_ref[0], a_ref[...], b_ref[...])
        g_ref[...] = g
        d_ref[...], nm_ref[...], nv_ref[...] = _adamw_update(p_ref[...], g, m_ref[...], v_ref[...])

    whole = pl.BlockSpec((tr, n), lambda s, i, c: (s * nb + i, 0))
    used = pl.BlockSpec((tr, n), lambda s, i, c: (jnp.where(s == c[0], i, 0), 0))
    unused = pl.BlockSpec((tr, n), lambda s, i, c: (jnp.where(s == c[0], 0, i), 0))
    return pl.pallas_call(
        body, name=name, out_shape=[_sds((r, n), F32)] * 4,
        grid_spec=pltpu.PrefetchScalarGridSpec(
            num_scalar_prefetch=1, grid=(2, nb), in_specs=[whole] * 3 + [used, unused],
            out_specs=[whole] * 4),
        compiler_params=_params("arbitrary", "arbitrary"),
    )(where, w, m, v, mine, theirs)


PAIRS = D_ATTN // BAND


def _in_proj(x, g, w, gq, gk, ones_bd, tm):
    t, dm = x.shape
    n = w.shape[1]
    nd = len(DILATIONS)
    first = 3 * D_CONV

    def body(x_ref, g_ref, w_ref, gq_ref, gk_ref, bd_ref, h_ref, z_ref, *refs):
        outs, slab = refs[:3 * nd], refs[3 * nd]
        xv = x_ref[...]
        h = (xv * _rms_scale(xv) * g_ref[...]).astype(BF16)
        h_ref[...] = h
        for cols in _column_chunks(n):
            z_ref[:, cols] = _dot(h, w_ref[:, cols])
        bd = bd_ref[...]
        q = z_ref[:, first:first + D_ATTN]
        k = z_ref[:, first + D_ATTN:first + 2 * D_ATTN]
        vals = [(q * _head_rms_scale(q, bd) * gq_ref[...]) * HEAD_DIM ** -0.5,
                k * _head_rms_scale(k, bd) * gk_ref[...], z_ref[:, first + 2 * D_ATTN:n]]
        for m, val in enumerate(vals):
            for c in range(PAIRS):
                slab[c] = val[:, c * BAND:(c + 1) * BAND]
            for a, d in enumerate(DILATIONS):
                o_ref = outs[m * nd + a]
                for c in range(PAIRS):
                    for r in range(d):
                        rows = slab.at[c][pl.ds(r, tm // d, stride=d), :] if d > 1 else slab[c]
                        o_ref[c, r] = rows.astype(BF16)

    row = pl.BlockSpec((tm, dm), lambda i: (i, 0))
    vec = pl.BlockSpec((1, D_ATTN), lambda i: (0, 0))
    return pl.pallas_call(
        body, name="in_proj", grid=(t // tm,),
        out_shape=[_sds((t, dm), BF16), _sds((t, n), F32)]
        + [_sds((PAIRS, d, t // d, BAND), BF16) for _ in range(3) for d in DILATIONS],
        in_specs=[row, pl.BlockSpec((1, dm), lambda i: (0, 0)),
                  pl.BlockSpec((dm, n), lambda i: (0, 0), pipeline_mode=_resident(True)), vec, vec,
                  pl.BlockSpec((D_ATTN, D_ATTN), lambda i: (0, 0), pipeline_mode=_resident(True))],
        out_specs=[row, pl.BlockSpec((tm, n), lambda i: (i, 0))]
        + [pl.BlockSpec((PAIRS, d, tm // d, BAND), lambda i: (0, 0, i, 0))
           for _ in range(3) for d in DILATIONS],
        scratch_shapes=[pltpu.VMEM((PAIRS, tm, BAND), F32)],
        compiler_params=_params("parallel"),
    )(x, g, w, gq, gk, ones_bd)


TOK = 2048
UNITS = TOK // BAND


def _stack_masks():
    row = lax.broadcasted_iota(jnp.int32, (2 * BAND, 2 * BAND), 0) & (BAND - 1)
    col = lax.broadcasted_iota(jnp.int32, (2 * BAND, 2 * BAND), 1)
    lane = lax.broadcasted_iota(jnp.int32, (BAND, BAND), 1)
    head0 = lane < HEAD_DIM
    ones = [jnp.where(head0, 1.0, 0.0).astype(BF16), jnp.where(head0, 0.0, 1.0).astype(BF16)]
    return col - row, col, head0, ones


def _split3(x):
    hi = x.astype(BF16).astype(F32)
    mid = (x - hi).astype(BF16).astype(F32)
    return hi, mid, x - hi - mid


def _gather(srcs, dst, d):
    per = TOK // d
    at = 0
    for r in range(d):
        for src in srcs:
            rows = src[pl.ds(r, per, stride=d), :] if d > 1 else src[...]
            dst[pl.ds(at, per), :] = rows.astype(dst.dtype)
            at += per


def _scatter_add(out_ref, src, d, per_src, offset, first):
    per = TOK // d
    if d == 1:
        val = src[pl.ds(offset, per), :]
        out_ref[...] = val if first else out_ref[...] + val
        return
    for r in range(d):
        val = src[pl.ds(r * per_src + offset, per), :]
        idx = pl.ds(r, per, stride=d)
        out_ref[idx, :] = val if first else out_ref[idx, :] + val


def _dilated_specs(nblk, reverse):
    def at(s):
        return (nblk - 1 - s) if reverse else s
    main = [pl.BlockSpec((1, d, TOK // d, BAND), lambda j, s: (j, 0, at(s), 0)) for d in DILATIONS]
    prev = [pl.BlockSpec((1, d, TOK // d, BAND), lambda j, s: (j, 0, jnp.maximum(at(s) - 1, 0), 0))
            for d in DILATIONS]
    return main, prev


def _window_rows(prev_ref, main_ref, dst, d):
    per = TOK // d
    for r in range(d):
        dst[pl.ds(r * (per + BAND), BAND), :] = prev_ref[0, r, pl.ds(per - BAND, BAND), :]
        dst[pl.ds(r * (per + BAND) + BAND, per), :] = main_ref[0, r]


def _attn_fwd(qs, ks, vs, carry=None):
    t = qs[0].shape[2]
    nblk = t // TOK
    nd = len(DILATIONS)

    def body(*refs):
        q_refs, kp_refs, k_refs = refs[:nd], refs[nd:2 * nd], refs[2 * nd:3 * nd]
        vp_refs, v_refs = refs[3 * nd:4 * nd], refs[4 * nd:5 * nd]
        y_ref, l_ref, kw_s, vw_s, ob, lb, on, ln = refs[5 * nd:]
        i = pl.program_id(1)
        diff, col, head0, hm = _stack_masks()
        band_ok = jnp.logical_and(diff >= 0, diff <= BAND)
        for g, d in enumerate(DILATIONS):
            per = TOK // d
            nb = per // BAND
            pad = per + BAND
            _window_rows(kp_refs[g], k_refs[g], kw_s, d)
            _window_rows(vp_refs[g], v_refs[g], vw_s, d)
            q_ref = q_refs[g]

            def unit(u, carry):
                r, b = u // nb, u % nb
                qu = q_ref[0, r, pl.ds(pl.multiple_of(b * BAND, BAND), BAND), :]
                start = pl.multiple_of(r * pad + b * BAND, BAND)
                kw = kw_s[pl.ds(start, 2 * BAND), :]
                vw = vw_s[pl.ds(start, 2 * BAND), :]
                lo = jnp.where(jnp.logical_and(i == 0, b == 0), BAND, 0)
                s = _dot_nt(jnp.concatenate([qu * hm[0], qu * hm[1]], axis=0), kw)
                s = jnp.where(jnp.logical_and(band_ok, col >= lo), s, NEG)
                mx = jnp.max(s, axis=-1, keepdims=True)
                e = jnp.exp(s - mx)
                den = jnp.sum(e, axis=-1, keepdims=True)
                o2 = _dot(e.astype(BF16), vw) / den
                l2 = jnp.broadcast_to(mx + jnp.log(den), (2 * BAND, BAND))
                rows = pl.ds(pl.multiple_of(u * BAND, BAND), BAND)
                ob[rows, :] = jnp.where(head0, o2[:BAND], o2[BAND:])
                lb[rows, :] = jnp.where(head0, l2[:BAND], l2[BAND:])
                return carry

            lax.fori_loop(0, UNITS, unit, 0, unroll=16)
            _scatter_add(on.at[g], ob, d, per, 0, True)
            _scatter_add(ln.at[g], lb, d, per, 0, True)
        ls = [ln[0], ln[1], ln[2]]
        mx = jnp.maximum(jnp.maximum(ls[0], ls[1]), ls[2])
        es = [jnp.exp(l - mx) for l in ls]
        tot = es[0] + es[1] + es[2]
        y_ref[...] = (es[0] * on[0] + es[1] * on[1] + es[2] * on[2]) / tot
        l_ref[...] = mx + jnp.log(tot)

    main, prev = _dilated_specs(nblk, False)
    out = pl.BlockSpec((TOK, BAND), lambda j, i: (i, j))
    win_rows = max(d * (TOK // d + BAND) for d in DILATIONS)
    return _call(
        body, list(qs) + list(ks) + list(ks) + list(vs) + list(vs), name="attn_fwd",
        grid=(PAIRS, nblk), out_shape=[_sds((t, D_ATTN), F32)] * 2,
        in_specs=main + prev + main + prev + main, out_specs=[out, out],
        scratch_shapes=[pltpu.VMEM((win_rows, BAND), BF16)] * 2 + [pltpu.VMEM((TOK, BAND), F32)] * 2
        + [pltpu.VMEM((nd, TOK, BAND), F32)] * 2,
        semantics=("parallel", "parallel"), carry=carry)


def _attn_bwd(qs, ks, vs, do, lse, dd, carry=None):
    t = qs[0].shape[2]
    nblk = t // TOK
    nd = len(DILATIONS)
    offs = [sum(DILATIONS[:g]) * BAND for g in range(nd)]

    def body(*refs):
        q_refs, kp_refs, k_refs = refs[:nd], refs[nd:2 * nd], refs[2 * nd:3 * nd]
        vp_refs, v_refs = refs[3 * nd:4 * nd], refs[4 * nd:5 * nd]
        (do_ref, l_ref, d_ref, dq_ref, dk_ref, dv_ref, kw_s, vw_s, dos, lds, pn, dqb, dkb, dvb, ckb,
         cvb) = refs[5 * nd:]
        step = pl.program_id(1)
        i = nblk - 1 - step
        key = lax.broadcasted_iota(jnp.int32, (2 * BAND, 2 * BAND), 0)
        qry = lax.broadcasted_iota(jnp.int32, (2 * BAND, 2 * BAND), 1) & (BAND - 1)
        off = key - qry
        band_ok = jnp.logical_and(off >= 0, off <= BAND)
        lane = lax.broadcasted_iota(jnp.int32, (BAND, BAND), 1)
        head0 = lane < HEAD_DIM
        hm = [jnp.where(head0, 1.0, 0.0).astype(BF16), jnp.where(head0, 0.0, 1.0).astype(BF16)]
        lane2 = lax.broadcasted_iota(jnp.int32, (2 * BAND, BAND), 1) & (HEAD_DIM - 1)
        ones_l = jnp.where(lane2 < 3, 1.0, 0.0).astype(BF16)
        ones_d = jnp.where(jnp.logical_and(lane2 >= 3, lane2 < 6), 1.0, 0.0).astype(BF16)
        piece = lax.broadcasted_iota(jnp.int32, (TOK, BAND), 1) & (HEAD_DIM - 1)

        def pieces(x, at):
            hi, mid, lo = _split3(-x)
            return jnp.where(piece == at, hi,
                             jnp.where(piece == at + 1, mid, jnp.where(piece == at + 2, lo, 0.0)))

        pn[...] = pieces(l_ref[...], 0) + pieces(d_ref[...], 3)
        for g, d in enumerate(DILATIONS):
            per = TOK // d
            nb = per // BAND
            pad = per + BAND
            _window_rows(kp_refs[g], k_refs[g], kw_s, d)
            _window_rows(vp_refs[g], v_refs[g], vw_s, d)
            _gather([do_ref], dos, d)
            _gather([pn], lds, d)
            for r in range(d):
                spare = pl.ds(r * pad, BAND)
                dkb[spare, :] = jnp.zeros((BAND, BAND), F32)
                dvb[spare, :] = jnp.zeros((BAND, BAND), F32)
            q_ref = q_refs[g]

            def unit(u, c_):
                r, b = u // nb, u % nb
                rows = pl.ds(pl.multiple_of(u * BAND, BAND), BAND)
                qu = q_ref[0, r, pl.ds(pl.multiple_of(b * BAND, BAND), BAND), :]
                dou, ldu = dos[rows, :], lds[rows, :]
                q2 = jnp.concatenate([qu * hm[0], qu * hm[1]], axis=0)
                do2 = jnp.concatenate([dou * hm[0], dou * hm[1]], axis=0)
                ld2 = jnp.concatenate([ldu * hm[0], ldu * hm[1]], axis=0)
                acc = pl.ds(pl.multiple_of(r * pad + b * BAND, BAND), 2 * BAND)
                kw = kw_s[acc, :]
                vw = vw_s[acc, :]
                lo = jnp.where(jnp.logical_and(i == 0, b == 0), BAND, 0)
                ok = jnp.logical_and(band_ok, key >= lo)
                st = _dot_nt(jnp.concatenate([kw, ones_l], axis=1), jnp.concatenate([q2, ld2], axis=1))
                dpt = _dot_nt(jnp.concatenate([vw, ones_d], axis=1), jnp.concatenate([do2, ld2], axis=1))
                pt = jnp.where(ok, jnp.exp(st), 0.0)
                dst = (pt * dpt).astype(BF16)
                low = pl.ds(pl.multiple_of(r * pad + b * BAND, BAND), BAND)
                high = pl.ds(pl.multiple_of(r * pad + (b + 1) * BAND, BAND), BAND)
                dkw = _dot(dst, q2)
                dvw = _dot(pt.astype(BF16), do2)
                dkb[low, :] += dkw[:BAND]
                dvb[low, :] += dvw[:BAND]
                dkb[high, :] = dkw[BAND:]
                dvb[high, :] = dvw[BAND:]
                dq2 = _dot_tn(dst, kw)
                dqb[rows, :] = jnp.where(head0, dq2[:BAND], dq2[BAND:])
                return c_

            lax.fori_loop(0, UNITS, unit, 0, unroll=16)

            for r in range(d):
                last = pl.ds(r * pad + per, BAND)
                kept = pl.ds(offs[g] + r * BAND, BAND)

                @pl.when(step > 0)
                def _():
                    dkb[last, :] += ckb[kept, :]
                    dvb[last, :] += cvb[kept, :]

                ckb[kept, :] = dkb[pl.ds(r * pad, BAND), :]
                cvb[kept, :] = dvb[pl.ds(r * pad, BAND), :]
            _scatter_add(dq_ref, dqb, d, per, 0, g == 0)
            _scatter_add(dk_ref, dkb, d, pad, BAND, g == 0)
            _scatter_add(dv_ref, dvb, d, pad, BAND, g == 0)

    main, prev = _dilated_specs(nblk, True)
    tok = pl.BlockSpec((TOK, BAND), lambda j, s: (nblk - 1 - s, j))
    acc_rows = max(d * (TOK // d + BAND) for d in DILATIONS)
    kept_rows = sum(DILATIONS) * BAND
    return _call(
        body, list(qs) + list(ks) + list(ks) + list(vs) + list(vs) + [do, lse, dd], name="attn_bwd",
        grid=(PAIRS, nblk), out_shape=[_sds((t, D_ATTN), F32)] * 3,
        in_specs=main + prev + main + prev + main + [tok] * 3, out_specs=[tok] * 3,
        scratch_shapes=[pltpu.VMEM((acc_rows, BAND), BF16)] * 2 + [pltpu.VMEM((TOK, BAND), BF16)] * 2
        + [pltpu.VMEM((TOK, BAND), F32)] * 2 + [pltpu.VMEM((acc_rows, BAND), F32)] * 2
        + [pltpu.VMEM((kept_rows, BAND), F32)] * 2,
        semantics=("parallel", "arbitrary"), carry=carry)


def _halo_rows(tm, t):
    per = tm // 8
    prev = lambda i: (jnp.maximum(i * per - 1, 0), 0)
    nxt = lambda i: (jnp.minimum((i + 1) * per, t // 8 - 1), 0)
    return prev, nxt


def _mixer_out(z, cw, y_attn, g_conv, g_attn, tm, carry=None):
    t = z.shape[0]
    prev, _ = _halo_rows(tm, t)

    def body(z_ref, zp_ref, cw_ref, y_ref, gc_ref, ga_ref, mix_ref):
        i = pl.program_id(0)
        keep = jnp.where(i > 0, 1.0, 0.0)
        cu = jnp.concatenate([zp_ref[:, 0:512] * zp_ref[:, 1024:1536] * keep,
                              z_ref[:, 0:512] * z_ref[:, 1024:1536]], axis=0)
        c = (cw_ref[0:1, :] * pltpu.roll(cu, 2, 0) + cw_ref[1:2, :] * pltpu.roll(cu, 1, 0)
             + cw_ref[2:3, :] * cu)[8:, :]
        yc = z_ref[:, 512:1024] * c
        mix_ref[:, 0:512] = (yc * _rms_scale(yc) * gc_ref[...]).astype(BF16)
        ya = y_ref[...]
        mix_ref[:, 512:1024] = (ya * _rms_scale(ya) * ga_ref[...]).astype(BF16)

    blk = pl.BlockSpec((tm, 512), lambda i: (i, 0))
    vec = pl.BlockSpec((1, 512), lambda i: (0, 0))
    return _call(
        body, [z, z, cw, y_attn, g_conv, g_attn], name="mixer_out", grid=(t // tm,),
        out_shape=_sds((t, 1024), BF16),
        in_specs=[pl.BlockSpec((tm, 1536), lambda i: (i, 0)), pl.BlockSpec((8, 1536), prev),
                  pl.BlockSpec((8, 512), lambda i: (0, 0)), blk, vec, vec],
        out_specs=pl.BlockSpec((tm, 1024), lambda i: (i, 0)),
        semantics=("parallel",), carry=carry)


def _mixer_bwd(z, dx1, wout, y_attn, cw, g_conv, g_attn, ones_bd, tm, carry=None):
    t = z.shape[0]
    nblk = t // tm
    prev, nxt = _halo_rows(tm, t)
    e = tm + 16

    def body(z_ref, zp_ref, zn_ref, dx_ref, dxn_ref, w_ref, y_ref, cw_ref, gc_ref, ga_ref, bd_ref,
             dz_ref, do_ref, dd_ref, dcw_ref, dgc_ref, dga_ref):
        i = pl.program_id(0)
        dm = _dot_nt(dx_ref[...], w_ref[...])
        dmn = _dot_nt(dxn_ref[...], w_ref[0:D_CONV, :])[0:8, :]
        rows = lax.broadcasted_iota(jnp.int32, (e, 1), 0)
        lo = jnp.where(i > 0, 0, 8)
        hi = jnp.where(i < nblk - 1, e, tm + 8)
        ze = jnp.concatenate([zp_ref[...], z_ref[...], zn_ref[...]], axis=0)
        u, gb, gcv = ze[:, 0:512], ze[:, 512:1024], ze[:, 1024:1536]
        w0, w1, w2 = cw_ref[0:1, :], cw_ref[1:2, :], cw_ref[2:3, :]
        cu = jnp.where(rows >= lo, gcv * u, 0.0)
        cu1, cu2 = pltpu.roll(cu, 1, 0), pltpu.roll(cu, 2, 0)
        c = w0 * cu2 + w1 * cu1 + w2 * cu
        yc = gb * c
        dma = jnp.concatenate([jnp.zeros((8, 512), F32), dm[:, 0:512], dmn], axis=0)
        dyc, ych = _rms_bwd(yc, _rms_scale(yc), gc_ref[...], dma)
        dc = jnp.where(jnp.logical_and(rows >= 8, rows < hi), dyc * gb, 0.0)
        dcu = w0 * pltpu.roll(dc, e - 2, 0) + w1 * pltpu.roll(dc, e - 1, 0) + w2 * dc
        mid = slice(8, 8 + tm)
        dz_ref[:, 0:512] = (dcu * gcv)[mid, :].astype(BF16)
        dz_ref[:, 512:1024] = (dyc * c)[mid, :].astype(BF16)
        dz_ref[:, 1024:1536] = (dcu * u)[mid, :].astype(BF16)

        ya = y_ref[...]
        dmb = dm[:, 512:1024]
        dya, yah = _rms_bwd(ya, _rms_scale(ya), ga_ref[...], dmb)
        do_ref[...] = dya
        dd_ref[...] = _head_sum(dya * ya, bd_ref[...])

        @pl.when(i == 0)
        def _():
            dcw_ref[...] = jnp.zeros_like(dcw_ref)
            dgc_ref[...] = jnp.zeros_like(dgc_ref)
            dga_ref[...] = jnp.zeros_like(dga_ref)

        dcm = jnp.where(rows < tm + 8, dc, 0.0)
        dcw_ref[0:1, :] += jnp.sum(dcm * cu2, axis=0, keepdims=True)
        dcw_ref[1:2, :] += jnp.sum(dcm * cu1, axis=0, keepdims=True)
        dcw_ref[2:3, :] += jnp.sum(dcm * cu, axis=0, keepdims=True)
        dgc_ref[...] += jnp.sum((dma * ych)[mid, :], axis=0, keepdims=True)
        dga_ref[...] += jnp.sum(dmb * yah, axis=0, keepdims=True)

    blk = pl.BlockSpec((tm, 512), lambda i: (i, 0))
    vec = pl.BlockSpec((1, 512), lambda i: (0, 0))
    cwb = pl.BlockSpec((8, 512), lambda i: (0, 0))
    next16 = lambda i: (jnp.minimum((i + 1) * (tm // 16), t // 16 - 1), 0)
    return _call(
        body, [z, z, z, dx1, dx1, wout, y_attn, cw, g_conv, g_attn, ones_bd], name="mixer_bwd",
        grid=(nblk,),
        out_shape=[_sds((t, D_IN), BF16), _sds((t, 512), F32), _sds((t, 512), F32),
                   _sds((8, 512), F32), _sds((1, 512), F32), _sds((1, 512), F32)],
        in_specs=[pl.BlockSpec((tm, 1536), lambda i: (i, 0)), pl.BlockSpec((8, 1536), prev),
                  pl.BlockSpec((8, 1536), nxt), pl.BlockSpec((tm, D_MODEL), lambda i: (i, 0)),
                  pl.BlockSpec((16, D_MODEL), next16),
                  pl.BlockSpec(wout.shape, lambda i: (0, 0), pipeline_mode=_resident(True)),
                  blk, cwb, vec, vec, pl.BlockSpec((512, 512), lambda i: (0, 0))],
        out_specs=[pl.BlockSpec((tm, 1536), lambda i: (i, 0)), blk, blk, cwb, vec, vec],
        carry=carry)


def _qkv_bwd(z, dz, dqn, dkn, dv, gq, gk, ones_bd, tm, carry=None):
    t = z.shape[0]

    def body(zq_ref, zk_ref, _, dqn_ref, dkn_ref, dv_ref, gq_ref, gk_ref, bd_ref,
             dz_ref, dgq_ref, dgk_ref):
        bd = bd_ref[...]

        @pl.when(pl.program_id(0) == 0)
        def _():
            dgq_ref[...] = jnp.zeros_like(dgq_ref)
            dgk_ref[...] = jnp.zeros_like(dgk_ref)

        def back(v, dn, g, scale):
            r = _head_rms_scale(v, bd)
            vh = v * r
            dh = dn * (g * scale)
            dv = r * (dh - vh * (_head_sum(dh * vh, bd) * (1.0 / HEAD_DIM)))
            return dv, jnp.sum(dn * scale * vh, axis=0, keepdims=True)

        dq, dgq = back(zq_ref[...], dqn_ref[...], gq_ref[...], HEAD_DIM ** -0.5)
        dk, dgk = back(zk_ref[...], dkn_ref[...], gk_ref[...], 1.0)
        dgq_ref[...] += dgq
        dgk_ref[...] += dgk
        dz_ref[:, 0:512] = dq.astype(BF16)
        dz_ref[:, 512:1024] = dk.astype(BF16)
        dz_ref[:, 1024:1536] = dv_ref[...].astype(BF16)

    blk = pl.BlockSpec((tm, 512), lambda i: (i, 0))
    vec = pl.BlockSpec((1, 512), lambda i: (0, 0))
    return _call(
        body, [z, z, dz, dqn, dkn, dv, gq, gk, ones_bd], name="qkv_bwd", grid=(t // tm,),
        out_shape=[_sds((t, D_IN), BF16), _sds((1, 512), F32), _sds((1, 512), F32)],
        in_specs=[pl.BlockSpec((tm, 512), lambda i: (i, 3)), pl.BlockSpec((tm, 512), lambda i: (i, 4)),
                  ANY] + [blk] * 3 + [vec, vec, pl.BlockSpec((512, 512), lambda i: (0, 0))],
        out_specs=[pl.BlockSpec((tm, 1536), lambda i: (i, 1)), vec, vec],
        carry=carry, aliases={2: 0})


def _columns_from_chips(g):
    return g.transpose(1, 0, 2).reshape(g.shape[1], N_CHIPS * g.shape[2])


def kernel(x, g_mix, w_in, conv_w, g_q, g_k, g_conv_out, g_attn_out, w_out, g_ffn, w_gate, w_up, w_down, loss_target, m_g_mix, m_w_in, m_conv_w, m_g_q, m_g_k, m_g_conv_out, m_g_attn_out, m_w_out, m_g_ffn, m_w_gate, m_w_up, m_w_down, v_g_mix, v_w_in, v_conv_w, v_g_q, v_g_k, v_g_conv_out, v_g_attn_out, v_w_out, v_g_ffn, v_w_gate, v_w_up, v_w_down):
    t = x.shape[1]
    xs = x[0]
    target = loss_target[0]
    tm = min(512, t)
    tmm = min(2048, t)

    cw_pad = jnp.pad(conv_w[0], ((0, 13), (0, 0)))
    gathered = _all_gather([w_in[0].astype(BF16), cw_pad])
    win = _columns_from_chips(gathered[0])
    cw = jnp.pad(gathered[1][:, 0:3, :].transpose(1, 0, 2).reshape(3, D_CONV), ((0, 5), (0, 0)))
    later = [w_out[0].astype(BF16), w_gate[0].T.astype(BF16), w_up[0].T.astype(BF16),
             w_down[0].astype(BF16)]

    head_id = jnp.arange(D_ATTN) // HEAD_DIM
    ones_bd = (head_id[:, None] == head_id[None, :]).astype(BF16)
    gq_t = jnp.tile(g_q, (1, D_ATTN // HEAD_DIM))
    gk_t = jnp.tile(g_k, (1, D_ATTN // HEAD_DIM))

    h1, z, *dilated = _in_proj(xs, g_mix, win, gq_t, gk_t, ones_bd, tm)
    nd = len(DILATIONS)
    qs, ks, vs = dilated[:nd], dilated[nd:2 * nd], dilated[2 * nd:]
    (y_attn, lse), gathered = _attn_fwd(qs, ks, vs, carry=_x_gather_chips(later))
    mix, gathered = _mixer_out(z, cw, y_attn, g_conv_out, g_attn_out, tm,
                               carry=_x_gather_sibling(gathered))
    wout = gathered[0].reshape(D_MODEL, D_MODEL)
    wgate_t = gathered[1].reshape(D_FF, D_MODEL)
    wup_t = gathered[2].reshape(D_FF, D_MODEL)
    wdown = gathered[3].reshape(D_FF, D_MODEL)
    (x1,) = _matmul("out_proj", mix, wout, [xs], [F32], lambda acc, r: (r + acc,), tm, D_MODEL)
    h2, gate, up, act = _norm_matmul("ffn_up", x1, g_ffn, [wgate_t, wup_t], tm, D_FF, True, BF16,
                                     transposed_w=True)

    def loss_epilogue(acc, r, tgt):
        err = r + acc - tgt
        dy = err * (1.0 / D_MODEL)
        return dy, dy, jnp.sum(err * err)

    dx2, dx2b, loss_sum = _matmul("ffn_down_loss", act, wdown, [x1, target], [F32, BF16],
                                  loss_epilogue, tm, D_MODEL, loss=True)

    def swiglu_bwd(da, gt, u):
        gt, u = gt.astype(F32), u.astype(F32)
        s = _sigmoid(gt)
        return da * u * (s * (1.0 + gt * (1.0 - s))), da * (gt * s)

    dgate, dup = _matmul("ffn_down_bwd", dx2b, wdown, [gate, up], [BF16, BF16], swiglu_bwd,
                         tm, D_FF, transposed_w=True)
    gw_down = _matmul_tn("grad_w_down", act, dx2b, 512, tmm)
    gw_gate_t = _matmul_tn("grad_w_gate", dgate, h2, 512, tmm)
    gw_up_t = _matmul_tn("grad_w_up", dup, h2, 512, tmm)

    me = 2 * lax.axis_index("x") + lax.axis_index("y")
    where = jnp.stack([lax.axis_index("c"), me]).astype(jnp.int32)

    def pair_sums(names, full, got):
        return [_pair_sum(f"pair_sum_{nme}", a, b, where) for nme, a, b in zip(names, full, got)]

    def chip_sums(names, pair, got):
        return [_chip_sum(f"chip_sum_{nme}", own, b) for nme, (_, own), b in zip(names, pair, got)]

    ffn = ["w_gate", "w_up", "w_down"]
    full = [g.reshape(N_CHIPS, D_FF // N_CHIPS, D_MODEL) for g in (gw_gate_t, gw_up_t, gw_down)]
    (dx1, dx1b, gg_ffn), got = _matmul_norm_bwd(
        "ffn_up_bwd", [(dgate, wgate_t), (dup, wup_t)], x1, dx2, g_ffn, tm, carry=_x_pair(full),
        transposed_w=False)
    pair = pair_sums(ffn, full, got)
    gw_out = _matmul_tn("grad_w_out", mix, dx1b, 512, tmm)
    full = [gw_out.reshape(N_CHIPS, D_MODEL // N_CHIPS, D_MODEL)]
    (dzc, do, dd, gcw, gg_conv, gg_attn), got = _mixer_bwd(
        z, dx1b, wout, y_attn, cw, g_conv_out, g_attn_out, ones_bd, tm, carry=_x_pair(full))
    pair += pair_sums(["w_out"], full, got)
    early = ffn + ["w_out"]
    (dqn, dkn, dv), got = _attn_bwd(qs, ks, vs, do, lse, dd, carry=_x_chips([p for p, _ in pair]))
    mine = chip_sums(early, pair, got)
    (dz, gg_q, gg_k), theirs = _qkv_bwd(z, dzc, dqn, dkn, dv, gq_t, gk_t, ones_bd, tm,
                                        carry=_x_share(mine))
    full = [_matmul_tn("grad_w_in", h1, dz, D_IN // N_CHIPS, tmm, by_chip=True)]
    got = _exchange_alone("grad_pair_exchange_w_in", _x_pair(full))
    pair = pair_sums(["w_in"], full, got)
    (grad_x, _, gg_mix), got = _matmul_norm_bwd("in_proj_bwd", [(dz, win)], xs, dx1, g_mix, tm,
                                                carry=_x_chips([pair[0][0]]))
    mine += chip_sums(["w_in"], pair, got)
    theirs = list(theirs) + list(_exchange_alone("grad_pair_share_w_in", _x_share(mine[-1:])))
    big = early + ["w_in"]

    small = _small_all_reduce({
        "g_mix": gg_mix, "g_ffn": gg_ffn, "g_conv_out": gg_conv, "g_attn_out": gg_attn,
        "g_q": gg_q, "g_k": gg_k, "loss": loss_sum, "conv_w": gcw})
    heads = D_ATTN // HEAD_DIM
    grads = {
        "g_mix": small[0:1, :], "g_ffn": small[1:2, :],
        "g_conv_out": small[2:3, 0:512], "g_attn_out": small[2:3, 512:1024],
        "g_q": small[3, 0:512].reshape(heads, HEAD_DIM).sum(axis=0)[None, :],
        "g_k": small[3, 512:1024].reshape(heads, HEAD_DIM).sum(axis=0)[None, :],
        "conv_w": lax.dynamic_slice(small[8:11, 0:512], (0, me * (D_CONV // N_CHIPS)),
                                    (3, D_CONV // N_CHIPS)),
    }
    halves = dict(zip(big, zip(mine, theirs)))
    loss = small[4, 0] * 0.5 * (1.0 / D_MODEL)

    weights = dict(g_mix=g_mix, w_in=w_in, conv_w=conv_w, g_q=g_q, g_k=g_k, g_conv_out=g_conv_out,
                   g_attn_out=g_attn_out, w_out=w_out, g_ffn=g_ffn, w_gate=w_gate, w_up=w_up,
                   w_down=w_down)
    moments_m = dict(g_mix=m_g_mix, w_in=m_w_in, conv_w=m_conv_w, g_q=m_g_q, g_k=m_g_k,
                     g_conv_out=m_g_conv_out, g_attn_out=m_g_attn_out, w_out=m_w_out, g_ffn=m_g_ffn,
                     w_gate=m_w_gate, w_up=m_w_up, w_down=m_w_down)
    moments_v = dict(g_mix=v_g_mix, w_in=v_w_in, conv_w=v_conv_w, g_q=v_g_q, g_k=v_g_k,
                     g_conv_out=v_g_conv_out, g_attn_out=v_g_attn_out, w_out=v_w_out, g_ffn=v_g_ffn,
                     w_gate=v_w_gate, w_up=v_w_up, w_down=v_w_down)
    names = list(weights)
    out_g, out_d, out_m, out_v = [], [], [], []
    for nme in names:
        wgt = weights[nme]
        shape2 = wgt.shape[-2:] if wgt.ndim == 3 else wgt.shape
        flip = nme in ("w_gate", "w_up")

        def to2d(a):
            return a.reshape(shape2).T if flip else a.reshape(shape2)

        def back(a):
            return (a.T if flip else a).reshape(wgt.shape)

        state = (to2d(wgt), to2d(moments_m[nme]), to2d(moments_v[nme]))
        if nme in halves:
            g2, dlt, nm, nv = _adamw_shard(f"adamw_{nme}", *state, *halves[nme], where)
        else:
            g2 = grads[nme].reshape(shape2)
            dlt, nm, nv = _adamw(f"adamw_{nme}", state[0], g2, state[1], state[2])
        out_g.append(back(g2))
        out_d.append(back(dlt))
        out_m.append(back(nm))
        out_v.append(back(nv))
    return (loss, grad_x[None], *out_g, *out_d, *out_m, *out_v)
```

```python
import functools
from typing import Any, Callable, NamedTuple, Sequence

import jax
import jax.numpy as jnp
from jax import lax
from jax.experimental import pallas as pl
from jax.experimental.pallas import tpu as pltpu

F32 = jnp.float32
BF16 = jnp.bfloat16
MESH = pl.DeviceIdType.MESH

D_MODEL = 1024
D_CONV = 512
D_ATTN = 512
HEAD_DIM = 64
D_FF = 2816
D_IN = 3 * D_CONV + 3 * D_ATTN
DILATIONS = (1, 4, 16)
BAND = 128
EPS = 1e-6
NEG = -1e30
N_CHIPS = 4

ADAM_LR = 0.001
ADAM_B1 = 0.9
ADAM_B2 = 0.999
ADAM_EPS = 1e-08
ADAM_WD = 0.01
ADAM_STEP = 10

V7X_VMEM_BYTES = 64 * 1024 * 1024
VMEM_LIMIT = V7X_VMEM_BYTES - 8 * 1024 * 1024
ANY = pl.BlockSpec(memory_space=pl.ANY)
VMEM_WHOLE = pl.BlockSpec(memory_space=pltpu.VMEM)


def _params(*sem):
    return pltpu.CompilerParams(dimension_semantics=sem, vmem_limit_bytes=VMEM_LIMIT)


def _sds(shape, dtype):
    return jax.ShapeDtypeStruct(shape, dtype)


def _resident(whole):
    return pl.Buffered(1) if whole else None


def _place():
    x, y, c = lax.axis_index("x"), lax.axis_index("y"), lax.axis_index("c")
    chips = [(1 - x, y), (x, 1 - y), (1 - x, 1 - y)]
    return x, y, c, 2 * x + y, chips, [2 * cx + cy for cx, cy in chips]


def _all_gather(shards):
    n = len(shards)

    def body(*refs):
        ins, outs, stage = refs[:n], refs[n:2 * n], refs[2 * n:3 * n]
        ssem, rsem, fsem, gsem, lsem, osem = refs[3 * n:]
        x, y, c, me, chips, cids = _place()
        sib = (x, y, 1 - c)

        def half(w, which):
            h = shards[w].shape[0] // 2
            return pl.ds(pl.multiple_of(which * h, 8), h)

        loads = [pltpu.make_async_copy(ins[w], stage[w], lsem.at[w]) for w in range(n)]
        local = [pltpu.make_async_copy(stage[w], outs[w].at[me], osem.at[w]) for w in range(n)]
        for cp in loads:
            cp.start()

        def chip_copy(w, j, src_slot):
            rows = half(w, c)
            return pltpu.make_async_remote_copy(
                src_ref=ins[w].at[rows], dst_ref=outs[w].at[src_slot, rows],
                send_sem=ssem.at[3 * w + j], recv_sem=rsem.at[3 * w + j],
                device_id=(*chips[j], c), device_id_type=MESH)

        def sib_copy(w, j, which):
            rows = half(w, which)
            return pltpu.make_async_remote_copy(
                src_ref=outs[w].at[cids[j], rows], dst_ref=outs[w].at[cids[j], rows],
                send_sem=fsem.at[3 * w + j], recv_sem=gsem.at[3 * w + j],
                device_id=sib, device_id_type=MESH)

        sends = [chip_copy(w, j, me) for w in range(n) for j in range(3)]
        for cp in sends:
            cp.start()
        for w in range(n):
            loads[w].wait()
            local[w].start()
        passed = []
        for w in range(n):
            for j in range(3):
                chip_copy(w, j, cids[j]).wait_recv()
                cp = sib_copy(w, j, c)
                cp.start()
                passed.append(cp)
        for w in range(n):
            for j in range(3):
                sib_copy(w, j, 1 - c).wait_recv()
        for cp in sends + passed:
            cp.wait_send()
        for cp in local:
            cp.wait()

    return pl.pallas_call(
        body, name="all_gather_weights",
        out_shape=[_sds((N_CHIPS,) + s.shape, s.dtype) for s in shards],
        in_specs=[ANY] * n, out_specs=[ANY] * n,
        scratch_shapes=[pltpu.VMEM(s.shape, s.dtype) for s in shards]
        + [pltpu.SemaphoreType.DMA((3 * n,))] * 4 + [pltpu.SemaphoreType.DMA((n,))] * 2,
        compiler_params=pltpu.CompilerParams(vmem_limit_bytes=VMEM_LIMIT),
    )(*shards)


class _Exchange(NamedTuple):
    srcs: Sequence[Any]
    lands: Sequence[Any]
    outs: Sequence[Any]
    n_sems: int
    copies: Callable


def _remote(src, dst, ssem, rsem, k, to):
    return pltpu.make_async_remote_copy(src_ref=src, dst_ref=dst, send_sem=ssem.at[k],
                                        recv_sem=rsem.at[k], device_id=to, device_id_type=MESH)


def _x_gather_chips(shards):
    def copies(srcs, lands, outs, ssem, rsem):
        _, _, c, me, chips, cids = _place()
        go, arrive = [], []
        for w, s in enumerate(shards):
            h = s.shape[0] // 2
            rows = pl.ds(pl.multiple_of(c * h, 8), h)
            for j in range(3):
                to = (*chips[j], c)
                go.append(_remote(srcs[w].at[rows], lands[w].at[me, rows], ssem, rsem, 3 * w + j, to))
                arrive.append(_remote(srcs[w].at[rows], lands[w].at[cids[j], rows], ssem, rsem,
                                      3 * w + j, to))
        return go, arrive

    lands = [jnp.broadcast_to(s[None], (N_CHIPS,) + s.shape) for s in shards]
    return _Exchange(shards, lands, [], 3 * len(shards), copies)


def _x_gather_sibling(gathered):
    def copies(srcs, lands, outs, ssem, rsem):
        x, y, c, _, _, cids = _place()
        go, arrive = [], []
        for w, g in enumerate(gathered):
            h = g.shape[1] // 2
            mine = pl.ds(pl.multiple_of(c * h, 8), h)
            theirs = pl.ds(pl.multiple_of((1 - c) * h, 8), h)
            for j in range(3):
                slab = lands[w].at[cids[j]]
                go.append(_remote(slab.at[mine], slab.at[mine], ssem, rsem, 3 * w + j, (x, y, 1 - c)))
                arrive.append(_remote(slab.at[theirs], slab.at[theirs], ssem, rsem, 3 * w + j,
                                      (x, y, 1 - c)))
        return go, arrive

    return _Exchange([], gathered, [], 3 * len(gathered), copies)


def _x_pair(grads):
    def copies(srcs, lands, outs, ssem, rsem):
        x, y, c, _, _, _ = _place()
        go = []
        for w, g in enumerate(grads):
            h = g.shape[1] // 2
            theirs = pl.ds(pl.multiple_of((1 - c) * h, 8), h)
            go.append(_remote(srcs[w].at[:, theirs, :], outs[w], ssem, rsem, w, (x, y, 1 - c)))
        return go, go

    outs = [_sds((N_CHIPS, g.shape[1] // 2, g.shape[2]), g.dtype) for g in grads]
    return _Exchange(grads, [], outs, len(grads), copies)


def _x_chips(parts):
    def copies(srcs, lands, outs, ssem, rsem):
        _, _, c, _, chips, cids = _place()
        go = [_remote(srcs[w].at[cids[j]], outs[w].at[j], ssem, rsem, 3 * w + j, (*chips[j], c))
              for w in range(len(parts)) for j in range(3)]
        return go, go

    outs = [_sds((3,) + p.shape[1:], p.dtype) for p in parts]
    return _Exchange(parts, [], outs, 3 * len(parts), copies)


def _x_share(halves):
    def copies(srcs, lands, outs, ssem, rsem):
        x, y, c, _, _, _ = _place()
        go = [_remote(srcs[w], outs[w], ssem, rsem, w, (x, y, 1 - c)) for w in range(len(halves))]
        return go, go

    return _Exchange(halves, [], [_sds(h.shape, h.dtype) for h in halves], len(halves), copies)


def _call(body, args, *, name, grid, in_specs, out_specs, out_shape, scratch_shapes=(),
          semantics=None, carry=None, aliases=None):
    single = not isinstance(out_shape, (list, tuple))
    out_shape = [out_shape] if single else list(out_shape)
    out_specs = [out_specs] if single else list(out_specs)
    aliases = dict(aliases or {})
    if carry is None:
        res = pl.pallas_call(
            body, name=name, grid=grid, in_specs=list(in_specs), out_specs=out_specs,
            out_shape=out_shape, scratch_shapes=list(scratch_shapes), input_output_aliases=aliases,
            compiler_params=_params(*(semantics or ("arbitrary",) * len(grid))))(*args)
        return res[0] if single else res
    n_in, n_out, n_scr = len(args), len(out_shape), len(scratch_shapes)
    n_src, n_land, n_new = len(carry.srcs), len(carry.lands), len(carry.outs)

    def carrying(*refs):
        at = 0
        parts = []
        for n in (n_in, n_src, n_land, n_out, n_land, n_new, n_scr, 2):
            parts.append(refs[at:at + n])
            at += n
        ins, srcs, _, outs, lands, news, scratch, (ssem, rsem) = parts
        ids = [pl.program_id(a) for a in range(len(grid))]
        first = functools.reduce(jnp.logical_and, [i == 0 for i in ids])
        last = functools.reduce(jnp.logical_and, [i == g - 1 for i, g in zip(ids, grid)])
        go, arrive = carry.copies(srcs, lands, news, ssem, rsem)

        @pl.when(first)
        def _():
            for cp in go:
                cp.start()

        body(*ins, *outs, *scratch)

        @pl.when(last)
        def _():
            for cp in go:
                cp.wait_send()
            for cp in arrive:
                cp.wait_recv()

    res = pl.pallas_call(
        carrying, name=name, grid=grid,
        in_specs=list(in_specs) + [ANY] * (n_src + n_land),
        out_specs=out_specs + [ANY] * (n_land + n_new),
        out_shape=out_shape + [_sds(a.shape, a.dtype) for a in carry.lands] + list(carry.outs),
        input_output_aliases={**aliases, **{n_in + n_src + i: n_out + i for i in range(n_land)}},
        scratch_shapes=list(scratch_shapes) + [pltpu.SemaphoreType.DMA((carry.n_sems,))] * 2,
        compiler_params=_params(*(("arbitrary",) * len(grid))))(*args, *carry.srcs, *carry.lands)
    own = res[:n_out]
    return (own[0] if single else own), res[n_out:]


def _exchange_alone(name, exchange):
    def body(x_ref, o_ref):
        o_ref[...] = x_ref[...]

    blk = pl.BlockSpec((8, 128), lambda i: (0, 0))
    _, res = _call(body, [jnp.zeros((8, 128), F32)], name=name, grid=(1,), in_specs=[blk],
                   out_specs=blk, out_shape=_sds((8, 128), F32), carry=exchange)
    return res


def _row_block(r, want):
    return max(d for d in range(1, min(want, r) + 1) if r % d == 0 and (d % 8 == 0 or d == r))


def _pair_sum(name, full, got, where):
    _, r, n = full.shape
    h = r // 2
    tr = _row_block(h, 256)
    nb = h // tr

    def body(w_ref, a_ref, b_ref, o_ref, own_ref):
        total = a_ref[...] + b_ref[...]
        o_ref[...] = total.astype(BF16)

        @pl.when(pl.program_id(1) == w_ref[1])
        def _():
            own_ref[...] = total[0]

    blk = pl.BlockSpec((1, tr, n), lambda i, s, w: (s, i, 0))
    return pl.pallas_call(
        body, name=name, out_shape=[_sds(got.shape, BF16), _sds((h, n), F32)],
        grid_spec=pltpu.PrefetchScalarGridSpec(
            num_scalar_prefetch=1, grid=(nb, N_CHIPS),
            in_specs=[pl.BlockSpec((1, tr, n), lambda i, s, w: (s, w[0] * nb + i, 0)), blk],
            out_specs=[blk, pl.BlockSpec((tr, n), lambda i, s, w: (i, 0))]),
        compiler_params=_params("parallel", "arbitrary"),
    )(where, full, got)


def _chip_sum(name, own, got):
    h, n = own.shape
    tr = _row_block(h, 256)

    def body(a_ref, b0, b1, b2, o_ref):
        o_ref[...] = ((a_ref[...] + b0[0].astype(F32)) + b1[0].astype(F32)) + b2[0].astype(F32)

    def slot(j):
        return pl.BlockSpec((1, tr, n), lambda i: (j, i, 0))

    blk = pl.BlockSpec((tr, n), lambda i: (i, 0))
    return pl.pallas_call(
        body, name=name, grid=(h // tr,), out_shape=_sds((h, n), F32),
        in_specs=[blk, slot(0), slot(1), slot(2)], out_specs=blk,
        compiler_params=_params("parallel"),
    )(own, got, got, got)


SMALL_ROWS = 16
SMALL_LAYOUT = (
    ("g_mix", 0, 0, 1, 1024), ("g_ffn", 1, 0, 1, 1024), ("g_conv_out", 2, 0, 1, 512),
    ("g_attn_out", 2, 512, 1, 512), ("g_q", 3, 0, 1, 512), ("g_k", 3, 512, 1, 512),
    ("loss", 4, 0, 1, 128), ("conv_w", 8, 0, 8, 512))


def _small_all_reduce(parts):
    names = [s[0] for s in SMALL_LAYOUT]

    def body(*refs):
        ins = refs[:len(names)]
        out_ref, stage, buf, ssem, rsem = refs[len(names):]
        x, y, c, _, _, _ = _place()
        me = 4 * x + 2 * y + c
        stage[...] = jnp.zeros_like(stage)
        for ref, (_, r0, c0, nr, nc) in zip(ins, SMALL_LAYOUT):
            stage[r0:r0 + nr, c0:c0 + nc] = ref[0:nr, :]
        buf[me] = stage[...]
        peers = []
        for d in range(1, 8):
            px = 1 - x if d & 4 else x
            py = 1 - y if d & 2 else y
            pc = 1 - c if d & 1 else c
            peers.append(((px, py, pc), 4 * px + 2 * py + pc))
        sends = [pltpu.make_async_remote_copy(
            src_ref=stage, dst_ref=buf.at[me], send_sem=ssem.at[k], recv_sem=rsem.at[k],
            device_id=peer, device_id_type=MESH) for k, (peer, _) in enumerate(peers)]
        for cp in sends:
            cp.start()
        for k, (peer, pid) in enumerate(peers):
            pltpu.make_async_remote_copy(
                src_ref=stage, dst_ref=buf.at[pid], send_sem=ssem.at[k], recv_sem=rsem.at[k],
                device_id=peer, device_id_type=MESH).wait_recv()
        for cp in sends:
            cp.wait_send()
        acc = buf[0]
        for k in range(1, 8):
            acc = acc + buf[k]
        out_ref[...] = acc

    return pl.pallas_call(
        body, name="small_all_reduce", out_shape=_sds((SMALL_ROWS, 1024), F32),
        in_specs=[VMEM_WHOLE] * len(names), out_specs=VMEM_WHOLE,
        scratch_shapes=[pltpu.VMEM((SMALL_ROWS, 1024), F32), pltpu.VMEM((8, SMALL_ROWS, 1024), F32),
                        pltpu.SemaphoreType.DMA((7,)), pltpu.SemaphoreType.DMA((7,))],
    )(*[parts[k] for k in names])


def _dot(a, b):
    return jnp.dot(a, b, preferred_element_type=F32)


def _dot_nt(a, b):
    return lax.dot_general(a, b, (((1,), (1,)), ((), ())), preferred_element_type=F32)


def _dot_tn(a, b):
    return lax.dot_general(a, b, (((0,), (0,)), ((), ())), preferred_element_type=F32)


def _sigmoid(v):
    return 1.0 / (1.0 + jnp.exp(-v))


def _rms_scale(v):
    return lax.rsqrt(jnp.mean(v * v, axis=-1, keepdims=True) + EPS)


def _rms_bwd(v, r, g, dy):
    vh = v * r
    dh = dy * g
    return r * (dh - vh * jnp.mean(dh * vh, axis=-1, keepdims=True)), vh


def _head_sum(a, ones_bd):
    hi = a.astype(BF16)
    lo = (a - hi.astype(F32)).astype(BF16)
    return _dot(hi, ones_bd) + _dot(lo, ones_bd)


def _head_rms_scale(v, ones_bd):
    return lax.rsqrt(_head_sum(v * v, ones_bd) * (1.0 / HEAD_DIM) + EPS)


MXU_COLUMNS = 256


def _column_chunks(n):
    width = MXU_COLUMNS if n % MXU_COLUMNS == 0 else n
    return [slice(c, c + width) for c in range(0, n, width)]


def _norm_matmul(name, x, g, ws, tm, tn, swiglu, out_dtype=F32, transposed_w=False):
    t, d = x.shape
    n = ws[0].shape[0] if transposed_w else ws[0].shape[1]
    nw = len(ws)

    def body(x_ref, g_ref, *refs):
        w_refs, h_ref, o_refs = refs[:nw], refs[nw], refs[nw + 1:2 * nw + 1]
        hs = refs[-1]

        @pl.when(pl.program_id(1) == 0)
        def _():
            xv = x_ref[...]
            h = (xv * _rms_scale(xv) * g_ref[...]).astype(BF16)
            hs[...] = h
            h_ref[...] = h

        h = hs[...]
        for cols in _column_chunks(tn):
            outs = [_dot_nt(h, w[cols, :]) if transposed_w else _dot(h, w[:, cols]) for w in w_refs]
            for o_ref, o in zip(o_refs, outs):
                o_ref[:, cols] = o.astype(out_dtype)
            if swiglu:
                refs[2 * nw + 1][:, cols] = (outs[0] * _sigmoid(outs[0]) * outs[1]).astype(BF16)

    row = pl.BlockSpec((tm, d), lambda i, j: (i, 0))
    col = pl.BlockSpec((tm, tn), lambda i, j: (i, j))
    out_shape = [_sds((t, d), BF16)] + [_sds((t, n), out_dtype)] * nw
    out_specs = [row] + [col] * nw
    if swiglu:
        out_shape.append(_sds((t, n), BF16))
        out_specs.append(col)
    return pl.pallas_call(
        body, name=name, grid=(t // tm, n // tn), out_shape=out_shape,
        in_specs=[row, pl.BlockSpec((1, d), lambda i, j: (0, 0))]
        + [pl.BlockSpec((tn, d), lambda i, j: (j, 0), pipeline_mode=_resident(tn == n))
           if transposed_w
           else pl.BlockSpec((d, tn), lambda i, j: (0, j), pipeline_mode=_resident(tn == n))] * nw,
        out_specs=out_specs, scratch_shapes=[pltpu.VMEM((tm, d), BF16)],
        compiler_params=_params("parallel", "arbitrary"),
    )(x, g, *ws)


def _matmul(name, a, w, extras, out_dtypes, epilogue, tm, tn, transposed_w=False, loss=False):
    t, k = a.shape
    n = w.shape[0] if transposed_w else w.shape[1]
    ne, no = len(extras), len(out_dtypes)

    def body(a_ref, w_ref, *refs):
        e_refs, o_refs = refs[:ne], refs[ne:]
        a = a_ref[...]
        total = 0.0
        for cols in _column_chunks(tn):
            acc = _dot_nt(a, w_ref[cols, :]) if transposed_w else _dot(a, w_ref[:, cols])
            res = epilogue(acc, *[e[:, cols] for e in e_refs])
            for o_ref, r in zip(o_refs[:no], res[:no]):
                o_ref[:, cols] = r.astype(o_ref.dtype)
            if loss:
                total = total + res[no]
        if loss:
            first = jnp.logical_and(pl.program_id(0) == 0, pl.program_id(1) == 0)

            @pl.when(first)
            def _():
                o_refs[no][...] = jnp.zeros_like(o_refs[no])

            o_refs[no][...] += total

    col = pl.BlockSpec((tm, tn), lambda i, j: (i, j))
    w_spec = (pl.BlockSpec((tn, k), lambda i, j: (j, 0), pipeline_mode=_resident(tn == n))
              if transposed_w
              else pl.BlockSpec((k, tn), lambda i, j: (0, j), pipeline_mode=_resident(tn == n)))
    out_shape = [_sds((t, n), dt) for dt in out_dtypes]
    out_specs = [col] * no
    if loss:
        out_shape.append(_sds((8, 128), F32))
        out_specs.append(pl.BlockSpec((8, 128), lambda i, j: (0, 0)))
    return pl.pallas_call(
        body, name=name, grid=(t // tm, n // tn), out_shape=out_shape,
        in_specs=[pl.BlockSpec((tm, k), lambda i, j: (i, 0)), w_spec] + [col] * ne,
        out_specs=out_specs,
        compiler_params=_params(*(("arbitrary", "arbitrary") if loss else ("parallel", "parallel"))),
    )(a, w, *extras)


def _matmul_norm_bwd(name, pairs, x, dres, g, tm, carry=None, transposed_w=True):
    t, d = x.shape
    npairs = len(pairs)
    product = _dot_nt if transposed_w else _dot

    def body(*refs):
        a_refs, w_refs = refs[:npairs], refs[npairs:2 * npairs]
        x_ref, r_ref, g_ref, dx_ref, dxb_ref, dg_ref = refs[2 * npairs:]
        dy = product(a_refs[0][...], w_refs[0][...])
        for a_ref, w_ref in zip(a_refs[1:], w_refs[1:]):
            dy = dy + product(a_ref[...], w_ref[...])
        xv = x_ref[...]
        dx, xh = _rms_bwd(xv, _rms_scale(xv), g_ref[...], dy)
        dx = dx + r_ref[...]
        dx_ref[...] = dx
        dxb_ref[...] = dx.astype(BF16)

        @pl.when(pl.program_id(0) == 0)
        def _():
            dg_ref[...] = jnp.zeros_like(dg_ref)

        dg_ref[...] += jnp.sum(dy * xh, axis=0, keepdims=True)

    row = pl.BlockSpec((tm, d), lambda i: (i, 0))
    vec = pl.BlockSpec((1, d), lambda i: (0, 0))
    return _call(
        body, [a for a, _ in pairs] + [w for _, w in pairs] + [x, dres, g], name=name,
        grid=(t // tm,), out_shape=[_sds((t, d), F32), _sds((t, d), BF16), _sds((1, d), F32)],
        in_specs=[pl.BlockSpec((tm, a.shape[1]), lambda i: (i, 0)) for a, _ in pairs]
        + [pl.BlockSpec(w.shape, lambda i: (0, 0), pipeline_mode=pl.Buffered(1)) for _, w in pairs]
        + [row, row, vec],
        out_specs=[row, row, vec], carry=carry)


def _matmul_tn(name, a, g, tn, tk, by_chip=False):
    t, ka = a.shape
    n = g.shape[1]

    def body(a_ref, g_ref, o_ref):
        @pl.when(pl.program_id(1) == 0)
        def _():
            o_ref[...] = jnp.zeros_like(o_ref)

        acc = _dot_tn(a_ref[...], g_ref[...])
        o_ref[...] += acc[None] if by_chip else acc

    return pl.pallas_call(
        body, name=name, grid=(n // tn, t // tk),
        out_shape=_sds((n // tn, ka, tn) if by_chip else (ka, n), F32),
        in_specs=[pl.BlockSpec((tk, ka), lambda j, s: (s, 0)),
                  pl.BlockSpec((tk, tn), lambda j, s: (s, j))],
        out_specs=(pl.BlockSpec((1, ka, tn), lambda j, s: (j, 0, 0)) if by_chip
                   else pl.BlockSpec((ka, tn), lambda j, s: (0, j))),
        compiler_params=_params("parallel", "arbitrary"),
    )(a, g)


def _elementwise(name, fn, ins, out_dtypes, tr):
    r, n = ins[0].shape
    tr = _row_block(r, tr)
    ni = len(ins)

    def body(*refs):
        res = fn(*[ref[...] for ref in refs[:ni]])
        for o_ref, v in zip(refs[ni:], res):
            o_ref[...] = v.astype(o_ref.dtype)

    blk = pl.BlockSpec((tr, n), lambda i: (i, 0))
    return pl.pallas_call(
        body, name=name, grid=(r // tr,), out_shape=[_sds((r, n), dt) for dt in out_dtypes],
        in_specs=[blk] * ni, out_specs=[blk] * len(out_dtypes),
        compiler_params=_params("parallel"),
    )(*ins)


def _adamw_update(w, g, m, v):
    m = ADAM_B1 * m + (1.0 - ADAM_B1) * g
    v = ADAM_B2 * v + (1.0 - ADAM_B2) * (g * g)
    m_hat = m / (1.0 - ADAM_B1 ** ADAM_STEP)
    v_hat = v / (1.0 - ADAM_B2 ** ADAM_STEP)
    return -ADAM_LR * (m_hat / (jnp.sqrt(v_hat) + ADAM_EPS) + ADAM_WD * w), m, v


def _adamw(name, w, g, m, v):
    return _elementwise(name, _adamw_update, [w, g, m, v], [F32] * 3, 256)


def _adamw_shard(name, w, m, v, mine, theirs, where):
    r, n = w.shape
    h = r // 2
    tr = _row_block(h, 256)
    nb = h // tr

    def body(w_ref, p_ref, m_ref, v_ref, a_ref, b_ref, g_ref, d_ref, nm_ref, nv_ref):
        g = jnp.where(pl.program_id(0) == w_ref[0], a_ref[...], b_ref[...])
        g_ref[...] = g
        d_ref[...], nm_ref[...], nv_ref[...] = _adamw_update(p_ref[...], g, m_ref[...], v_ref[...])

    whole = pl.BlockSpec((tr, n), lambda s, i, c: (s * nb + i, 0))
    used = pl.BlockSpec((tr, n), lambda s, i, c: (jnp.where(s == c[0], i, 0), 0))
    unused = pl.BlockSpec((tr, n), lambda s, i, c: (jnp.where(s == c[0], 0, i), 0))
    return pl.pallas_call(
        body, name=name, out_shape=[_sds((r, n), F32)] * 4,
        grid_spec=pltpu.PrefetchScalarGridSpec(
            num_scalar_prefetch=1, grid=(2, nb), in_specs=[whole] * 3 + [used, unused],
            out_specs=[whole] * 4),
        compiler_params=_params("arbitrary", "arbitrary"),
    )(where, w, m, v, mine, theirs)


PAIRS = D_ATTN // BAND


def _in_proj(x, g, w, gq, gk, ones_bd, tm):
    t, dm = x.shape
    n = w.shape[1]
    nd = len(DILATIONS)
    first = 3 * D_CONV

    def body(x_ref, g_ref, w_ref, gq_ref, gk_ref, bd_ref, h_ref, z_ref, *refs):
        outs, slabs = refs[:3 * nd], refs[3 * nd:]
        xv = x_ref[...]
        h = (xv * _rms_scale(xv) * g_ref[...]).astype(BF16)
        h_ref[...] = h
        for cols in _column_chunks(n):
            z_ref[:, cols] = _dot(h, w_ref[:, cols])
        bd = bd_ref[...]
        q = z_ref[:, first:first + D_ATTN]
        k = z_ref[:, first + D_ATTN:first + 2 * D_ATTN]
        vals = [(q * _head_rms_scale(q, bd) * gq_ref[...]) * HEAD_DIM ** -0.5,
                k * _head_rms_scale(k, bd) * gk_ref[...], z_ref[:, first + 2 * D_ATTN:n]]
        for m, val in enumerate(vals):
            for c in range(PAIRS):
                slabs[0][c] = val[:, c * BAND:(c + 1) * BAND]
            cur, before = 0, 1
            for a, d in enumerate(DILATIONS):
                o_ref, src, dst = outs[m * nd + a], slabs[cur], slabs[1 - cur]
                step, count = d // before, tm // d
                keep = step > 1 and a + 1 < nd
                for c in range(PAIRS):
                    for r in range(d):
                        start = (r % before) * (tm // before) + r // before
                        rows = src.at[c][pl.ds(start, count, stride=step), :] if step > 1 else src[c]
                        o_ref[c, r] = rows.astype(BF16)
                        if keep:
                            dst.at[c][pl.ds(r * count, count), :] = rows
                if keep:
                    cur = 1 - cur
                before = d

    row = pl.BlockSpec((tm, dm), lambda i: (i, 0))
    vec = pl.BlockSpec((1, D_ATTN), lambda i: (0, 0))
    return pl.pallas_call(
        body, name="in_proj", grid=(t // tm,),
        out_shape=[_sds((t, dm), BF16), _sds((t, n), F32)]
        + [_sds((PAIRS, d, t // d, BAND), BF16) for _ in range(3) for d in DILATIONS],
        in_specs=[row, pl.BlockSpec((1, dm), lambda i: (0, 0)),
                  pl.BlockSpec((dm, n), lambda i: (0, 0), pipeline_mode=_resident(True)), vec, vec,
                  pl.BlockSpec((D_ATTN, D_ATTN), lambda i: (0, 0), pipeline_mode=_resident(True))],
        out_specs=[row, pl.BlockSpec((tm, n), lambda i: (i, 0))]
        + [pl.BlockSpec((PAIRS, d, tm // d, BAND), lambda i: (0, 0, i, 0))
           for _ in range(3) for d in DILATIONS],
        scratch_shapes=[pltpu.VMEM((PAIRS, tm, BAND), F32)] * 2,
        compiler_params=_params("parallel"),
    )(x, g, w, gq, gk, ones_bd)


TOK = 2048
UNITS = TOK // BAND


def _stack_masks():
    row = lax.broadcasted_iota(jnp.int32, (2 * BAND, 2 * BAND), 0) & (BAND - 1)
    col = lax.broadcasted_iota(jnp.int32, (2 * BAND, 2 * BAND), 1)
    lane = lax.broadcasted_iota(jnp.int32, (BAND, BAND), 1)
    head0 = lane < HEAD_DIM
    ones = [jnp.where(head0, 1.0, 0.0).astype(BF16), jnp.where(head0, 0.0, 1.0).astype(BF16)]
    return col - row, col, head0, ones


def _split3(x):
    hi = x.astype(BF16).astype(F32)
    mid = (x - hi).astype(BF16).astype(F32)
    return hi, mid, x - hi - mid


def _gather(srcs, dst, d):
    per = TOK // d
    at = 0
    for r in range(d):
        for src in srcs:
            rows = src[pl.ds(r, per, stride=d), :] if d > 1 else src[...]
            dst[pl.ds(at, per), :] = rows.astype(dst.dtype)
            at += per


def _scatter(out_ref, src, d):
    per = TOK // d
    if d == 1:
        out_ref[...] = src[...]
        return
    for r in range(d):
        out_ref[pl.ds(r, per, stride=d), :] = src[pl.ds(r * per, per), :]


def _dilated_specs(nblk, reverse):
    def at(s):
        return (nblk - 1 - s) if reverse else s
    main = [pl.BlockSpec((1, d, TOK // d, BAND), lambda j, s: (j, 0, at(s), 0)) for d in DILATIONS]
    prev = [pl.BlockSpec((1, d, TOK // d, BAND), lambda j, s: (j, 0, jnp.maximum(at(s) - 1, 0), 0))
            for d in DILATIONS]
    return main, prev


def _window_rows(prev_ref, main_ref, dst, d):
    per = TOK // d
    for r in range(d):
        dst[pl.ds(r * (per + BAND), BAND), :] = prev_ref[0, r, pl.ds(per - BAND, BAND), :]
        dst[pl.ds(r * (per + BAND) + BAND, per), :] = main_ref[0, r]


def _attn_fwd(qs, ks, vs, carry=None):
    t = qs[0].shape[2]
    nblk = t // TOK
    nd = len(DILATIONS)

    def body(*refs):
        q_refs, kp_refs, k_refs = refs[:nd], refs[nd:2 * nd], refs[2 * nd:3 * nd]
        vp_refs, v_refs = refs[3 * nd:4 * nd], refs[4 * nd:5 * nd]
        y_ref, l_ref, kw_s, vw_s, ob, lb, on, ln = refs[5 * nd:]
        i = pl.program_id(1)
        diff, col, head0, hm = _stack_masks()
        band_ok = jnp.logical_and(diff >= 0, diff <= BAND)
        for g, d in enumerate(DILATIONS):
            per = TOK // d
            nb = per // BAND
            pad = per + BAND
            _window_rows(kp_refs[g], k_refs[g], kw_s, d)
            _window_rows(vp_refs[g], v_refs[g], vw_s, d)
            q_ref = q_refs[g]

            def unit(u, carry):
                r, b = u // nb, u % nb
                qu = q_ref[0, r, pl.ds(pl.multiple_of(b * BAND, BAND), BAND), :]
                start = pl.multiple_of(r * pad + b * BAND, BAND)
                kw = kw_s[pl.ds(start, 2 * BAND), :]
                vw = vw_s[pl.ds(start, 2 * BAND), :]
                lo = jnp.where(jnp.logical_and(i == 0, b == 0), BAND, 0)
                s = _dot_nt(jnp.concatenate([qu * hm[0], qu * hm[1]], axis=0), kw)
                s = jnp.where(jnp.logical_and(band_ok, col >= lo), s, NEG)
                mx = jnp.max(s, axis=-1, keepdims=True)
                e = jnp.exp(s - mx)
                den = jnp.sum(e, axis=-1, keepdims=True)
                o2 = _dot(e.astype(BF16), vw) / den
                l2 = jnp.broadcast_to(mx + jnp.log(den), (2 * BAND, BAND))
                rows = pl.ds(pl.multiple_of(u * BAND, BAND), BAND)
                ob[rows, :] = jnp.where(head0, o2[:BAND], o2[BAND:])
                lb[rows, :] = jnp.where(head0, l2[:BAND], l2[BAND:])
                return carry

            lax.fori_loop(0, UNITS, unit, 0, unroll=16)
            _scatter(on.at[g], ob, d)
            _scatter(ln.at[g], lb, d)
        ls = [ln[0], ln[1], ln[2]]
        mx = jnp.maximum(jnp.maximum(ls[0], ls[1]), ls[2])
        es = [jnp.exp(l - mx) for l in ls]
        tot = es[0] + es[1] + es[2]
        y_ref[...] = (es[0] * on[0] + es[1] * on[1] + es[2] * on[2]) / tot
        l_ref[...] = mx + jnp.log(tot)

    main, prev = _dilated_specs(nblk, False)
    out = pl.BlockSpec((TOK, BAND), lambda j, i: (i, j))
    win_rows = max(d * (TOK // d + BAND) for d in DILATIONS)
    return _call(
        body, list(qs) + list(ks) + list(ks) + list(vs) + list(vs), name="attn_fwd",
        grid=(PAIRS, nblk), out_shape=[_sds((t, D_ATTN), F32)] * 2,
        in_specs=main + prev + main + prev + main, out_specs=[out, out],
        scratch_shapes=[pltpu.VMEM((win_rows, BAND), BF16)] * 2 + [pltpu.VMEM((TOK, BAND), F32)] * 2
        + [pltpu.VMEM((nd, TOK, BAND), F32)] * 2,
        semantics=("parallel", "parallel"), carry=carry)


def _attn_bwd(qs, ks, vs, do, lse, dd, carry=None):
    t = qs[0].shape[2]
    nblk = t // TOK
    nd = len(DILATIONS)
    offs = [sum(DILATIONS[:g]) * BAND for g in range(nd)]

    def body(*refs):
        q_refs, kp_refs, k_refs = refs[:nd], refs[nd:2 * nd], refs[2 * nd:3 * nd]
        vp_refs, v_refs = refs[3 * nd:4 * nd], refs[4 * nd:5 * nd]
        (do_ref, l_ref, d_ref, dq_ref, dk_ref, dv_ref, kw_s, vw_s, dos, lds, pn, dqb, dkb, dvb, ckb,
         cvb, *folds) = refs[5 * nd:]
        step = pl.program_id(1)
        i = nblk - 1 - step
        key = lax.broadcasted_iota(jnp.int32, (2 * BAND, 2 * BAND), 0)
        qry = lax.broadcasted_iota(jnp.int32, (2 * BAND, 2 * BAND), 1) & (BAND - 1)
        off = key - qry
        band_ok = jnp.logical_and(off >= 0, off <= BAND)
        lane = lax.broadcasted_iota(jnp.int32, (BAND, BAND), 1)
        head0 = lane < HEAD_DIM
        hm = [jnp.where(head0, 1.0, 0.0).astype(BF16), jnp.where(head0, 0.0, 1.0).astype(BF16)]
        lane2 = lax.broadcasted_iota(jnp.int32, (2 * BAND, BAND), 1) & (HEAD_DIM - 1)
        ones_l = jnp.where(lane2 < 3, 1.0, 0.0).astype(BF16)
        ones_d = jnp.where(jnp.logical_and(lane2 >= 3, lane2 < 6), 1.0, 0.0).astype(BF16)
        piece = lax.broadcasted_iota(jnp.int32, (TOK, BAND), 1) & (HEAD_DIM - 1)

        def pieces(x, at):
            hi, mid, lo = _split3(-x)
            return jnp.where(piece == at, hi,
                             jnp.where(piece == at + 1, mid, jnp.where(piece == at + 2, lo, 0.0)))

        pn[...] = pieces(l_ref[...], 0) + pieces(d_ref[...], 3)
        order = sorted(range(nd), key=lambda a: -DILATIONS[a])
        assert DILATIONS[order[-1]] == 1
        for pos, g in enumerate(order):
            d = DILATIONS[g]
            per = TOK // d
            nb = per // BAND
            pad = per + BAND
            _window_rows(kp_refs[g], k_refs[g], kw_s, d)
            _window_rows(vp_refs[g], v_refs[g], vw_s, d)
            _gather([do_ref], dos, d)
            _gather([pn], lds, d)
            for r in range(d):
                spare = pl.ds(r * pad, BAND)
                dkb[spare, :] = jnp.zeros((BAND, BAND), F32)
                dvb[spare, :] = jnp.zeros((BAND, BAND), F32)
            q_ref = q_refs[g]

            def unit(u, c_):
                r, b = u // nb, u % nb
                rows = pl.ds(pl.multiple_of(u * BAND, BAND), BAND)
                qu = q_ref[0, r, pl.ds(pl.multiple_of(b * BAND, BAND), BAND), :]
                dou, ldu = dos[rows, :], lds[rows, :]
                q2 = jnp.concatenate([qu * hm[0], qu * hm[1]], axis=0)
                do2 = jnp.concatenate([dou * hm[0], dou * hm[1]], axis=0)
                ld2 = jnp.concatenate([ldu * hm[0], ldu * hm[1]], axis=0)
                acc = pl.ds(pl.multiple_of(r * pad + b * BAND, BAND), 2 * BAND)
                kw = kw_s[acc, :]
                vw = vw_s[acc, :]
                lo = jnp.where(jnp.logical_and(i == 0, b == 0), BAND, 0)
                ok = jnp.logical_and(band_ok, key >= lo)
                st = _dot_nt(jnp.concatenate([kw, ones_l], axis=1), jnp.concatenate([q2, ld2], axis=1))
                dpt = _dot_nt(jnp.concatenate([vw, ones_d], axis=1), jnp.concatenate([do2, ld2], axis=1))
                pt = jnp.where(ok, jnp.exp(st), 0.0)
                dst = (pt * dpt).astype(BF16)
                low = pl.ds(pl.multiple_of(r * pad + b * BAND, BAND), BAND)
                high = pl.ds(pl.multiple_of(r * pad + (b + 1) * BAND, BAND), BAND)
                dkw = _dot(dst, q2)
                dvw = _dot(pt.astype(BF16), do2)
                dkb[low, :] += dkw[:BAND]
                dvb[low, :] += dvw[:BAND]
                dkb[high, :] = dkw[BAND:]
                dvb[high, :] = dvw[BAND:]
                dq2 = _dot_tn(dst, kw)
                dqb[rows, :] = jnp.where(head0, dq2[:BAND], dq2[BAND:])
                return c_

            lax.fori_loop(0, UNITS, unit, 0, unroll=16)

            for r in range(d):
                last = pl.ds(r * pad + per, BAND)
                kept = pl.ds(offs[g] + r * BAND, BAND)

                @pl.when(step > 0)
                def _():
                    dkb[last, :] += ckb[kept, :]
                    dvb[last, :] += cvb[kept, :]

                ckb[kept, :] = dkb[pl.ds(r * pad, BAND), :]
                cvb[kept, :] = dvb[pl.ds(r * pad, BAND), :]
            narrower = DILATIONS[order[pos + 1]] if pos + 1 < nd else None
            for n, (buf, out_ref, stride, at) in enumerate(
                    ((dqb, dq_ref, per, 0), (dkb, dk_ref, pad, BAND), (dvb, dv_ref, pad, BAND))):
                wider, onward = folds[2 * n + pos % 2], folds[2 * n + (pos + 1) % 2]
                for r in range(d):
                    val = buf[pl.ds(r * stride + at, per), :]
                    if pos > 0:
                        val = val + wider[pl.ds(r * per, per), :]
                    if narrower is None:
                        out_ref[...] = val
                    else:
                        start = (r % narrower) * (TOK // narrower) + r // narrower
                        onward[pl.ds(start, per, stride=d // narrower), :] = val

    main, prev = _dilated_specs(nblk, True)
    tok = pl.BlockSpec((TOK, BAND), lambda j, s: (nblk - 1 - s, j))
    acc_rows = max(d * (TOK // d + BAND) for d in DILATIONS)
    kept_rows = sum(DILATIONS) * BAND
    return _call(
        body, list(qs) + list(ks) + list(ks) + list(vs) + list(vs) + [do, lse, dd], name="attn_bwd",
        grid=(PAIRS, nblk), out_shape=[_sds((t, D_ATTN), F32)] * 3,
        in_specs=main + prev + main + prev + main + [tok] * 3, out_specs=[tok] * 3,
        scratch_shapes=[pltpu.VMEM((acc_rows, BAND), BF16)] * 2 + [pltpu.VMEM((TOK, BAND), BF16)] * 2
        + [pltpu.VMEM((TOK, BAND), F32)] * 2 + [pltpu.VMEM((acc_rows, BAND), F32)] * 2
        + [pltpu.VMEM((kept_rows, BAND), F32)] * 2 + [pltpu.VMEM((TOK, BAND), F32)] * 6,
        semantics=("parallel", "arbitrary"), carry=carry)


def _halo_rows(tm, t):
    per = tm // 8
    prev = lambda i: (jnp.maximum(i * per - 1, 0), 0)
    nxt = lambda i: (jnp.minimum((i + 1) * per, t // 8 - 1), 0)
    return prev, nxt


def _mixer_out(z, cw, y_attn, g_conv, g_attn, tm, carry=None):
    t = z.shape[0]
    prev, _ = _halo_rows(tm, t)

    def body(z_ref, zp_ref, cw_ref, y_ref, gc_ref, ga_ref, mix_ref):
        i = pl.program_id(0)
        keep = jnp.where(i > 0, 1.0, 0.0)
        cu = jnp.concatenate([zp_ref[:, 0:512] * zp_ref[:, 1024:1536] * keep,
                              z_ref[:, 0:512] * z_ref[:, 1024:1536]], axis=0)
        c = (cw_ref[0:1, :] * pltpu.roll(cu, 2, 0) + cw_ref[1:2, :] * pltpu.roll(cu, 1, 0)
             + cw_ref[2:3, :] * cu)[8:, :]
        yc = z_ref[:, 512:1024] * c
        mix_ref[:, 0:512] = (yc * _rms_scale(yc) * gc_ref[...]).astype(BF16)
        ya = y_ref[...]
        mix_ref[:, 512:1024] = (ya * _rms_scale(ya) * ga_ref[...]).astype(BF16)

    blk = pl.BlockSpec((tm, 512), lambda i: (i, 0))
    vec = pl.BlockSpec((1, 512), lambda i: (0, 0))
    return _call(
        body, [z, z, cw, y_attn, g_conv, g_attn], name="mixer_out", grid=(t // tm,),
        out_shape=_sds((t, 1024), BF16),
        in_specs=[pl.BlockSpec((tm, 1536), lambda i: (i, 0)), pl.BlockSpec((8, 1536), prev),
                  pl.BlockSpec((8, 512), lambda i: (0, 0)), blk, vec, vec],
        out_specs=pl.BlockSpec((tm, 1024), lambda i: (i, 0)),
        semantics=("parallel",), carry=carry)


def _mixer_bwd(z, dx1, wout, y_attn, cw, g_conv, g_attn, ones_bd, tm, carry=None):
    t = z.shape[0]
    nblk = t // tm
    prev, nxt = _halo_rows(tm, t)
    e = tm + 16

    def body(z_ref, zp_ref, zn_ref, dx_ref, dxn_ref, w_ref, y_ref, cw_ref, gc_ref, ga_ref, bd_ref,
             dz_ref, do_ref, dd_ref, dcw_ref, dgc_ref, dga_ref):
        i = pl.program_id(0)
        dm = _dot_nt(dx_ref[...], w_ref[...])
        dmn = _dot_nt(dxn_ref[...], w_ref[0:D_CONV, :])[0:8, :]
        rows = lax.broadcasted_iota(jnp.int32, (e, 1), 0)
        lo = jnp.where(i > 0, 0, 8)
        hi = jnp.where(i < nblk - 1, e, tm + 8)
        ze = jnp.concatenate([zp_ref[...], z_ref[...], zn_ref[...]], axis=0)
        u, gb, gcv = ze[:, 0:512], ze[:, 512:1024], ze[:, 1024:1536]
        w0, w1, w2 = cw_ref[0:1, :], cw_ref[1:2, :], cw_ref[2:3, :]
        cu = jnp.where(rows >= lo, gcv * u, 0.0)
        cu1, cu2 = pltpu.roll(cu, 1, 0), pltpu.roll(cu, 2, 0)
        c = w0 * cu2 + w1 * cu1 + w2 * cu
        yc = gb * c
        dma = jnp.concatenate([jnp.zeros((8, 512), F32), dm[:, 0:512], dmn], axis=0)
        dyc, ych = _rms_bwd(yc, _rms_scale(yc), gc_ref[...], dma)
        dc = jnp.where(jnp.logical_and(rows >= 8, rows < hi), dyc * gb, 0.0)
        dcu = w0 * pltpu.roll(dc, e - 2, 0) + w1 * pltpu.roll(dc, e - 1, 0) + w2 * dc
        mid = slice(8, 8 + tm)
        dz_ref[:, 0:512] = (dcu * gcv)[mid, :].astype(BF16)
        dz_ref[:, 512:1024] = (dyc * c)[mid, :].astype(BF16)
        dz_ref[:, 1024:1536] = (dcu * u)[mid, :].astype(BF16)

        ya = y_ref[...]
        dmb = dm[:, 512:1024]
        dya, yah = _rms_bwd(ya, _rms_scale(ya), ga_ref[...], dmb)
        do_ref[...] = dya
        dd_ref[...] = _head_sum(dya * ya, bd_ref[...])

        @pl.when(i == 0)
        def _():
            dcw_ref[...] = jnp.zeros_like(dcw_ref)
            dgc_ref[...] = jnp.zeros_like(dgc_ref)
            dga_ref[...] = jnp.zeros_like(dga_ref)

        dcm = jnp.where(rows < tm + 8, dc, 0.0)
        dcw_ref[0:1, :] += jnp.sum(dcm * cu2, axis=0, keepdims=True)
        dcw_ref[1:2, :] += jnp.sum(dcm * cu1, axis=0, keepdims=True)
        dcw_ref[2:3, :] += jnp.sum(dcm * cu, axis=0, keepdims=True)
        dgc_ref[...] += jnp.sum((dma * ych)[mid, :], axis=0, keepdims=True)
        dga_ref[...] += jnp.sum(dmb * yah, axis=0, keepdims=True)

    blk = pl.BlockSpec((tm, 512), lambda i: (i, 0))
    vec = pl.BlockSpec((1, 512), lambda i: (0, 0))
    cwb = pl.BlockSpec((8, 512), lambda i: (0, 0))
    next16 = lambda i: (jnp.minimum((i + 1) * (tm // 16), t // 16 - 1), 0)
    return _call(
        body, [z, z, z, dx1, dx1, wout, y_attn, cw, g_conv, g_attn, ones_bd], name="mixer_bwd",
        grid=(nblk,),
        out_shape=[_sds((t, D_IN), BF16), _sds((t, 512), F32), _sds((t, 512), F32),
                   _sds((8, 512), F32), _sds((1, 512), F32), _sds((1, 512), F32)],
        in_specs=[pl.BlockSpec((tm, 1536), lambda i: (i, 0)), pl.BlockSpec((8, 1536), prev),
                  pl.BlockSpec((8, 1536), nxt), pl.BlockSpec((tm, D_MODEL), lambda i: (i, 0)),
                  pl.BlockSpec((16, D_MODEL), next16),
                  pl.BlockSpec(wout.shape, lambda i: (0, 0), pipeline_mode=_resident(True)),
                  blk, cwb, vec, vec, pl.BlockSpec((512, 512), lambda i: (0, 0))],
        out_specs=[pl.BlockSpec((tm, 1536), lambda i: (i, 0)), blk, blk, cwb, vec, vec],
        carry=carry)


def _qkv_bwd(z, dz, dqn, dkn, dv, gq, gk, ones_bd, tm, carry=None):
    t = z.shape[0]

    def body(zq_ref, zk_ref, _, dqn_ref, dkn_ref, dv_ref, gq_ref, gk_ref, bd_ref,
             dz_ref, dgq_ref, dgk_ref):
        bd = bd_ref[...]

        @pl.when(pl.program_id(0) == 0)
        def _():
            dgq_ref[...] = jnp.zeros_like(dgq_ref)
            dgk_ref[...] = jnp.zeros_like(dgk_ref)

        def back(v, dn, g, scale):
            r = _head_rms_scale(v, bd)
            vh = v * r
            dh = dn * (g * scale)
            dv = r * (dh - vh * (_head_sum(dh * vh, bd) * (1.0 / HEAD_DIM)))
            return dv, jnp.sum(dn * scale * vh, axis=0, keepdims=True)

        dq, dgq = back(zq_ref[...], dqn_ref[...], gq_ref[...], HEAD_DIM ** -0.5)
        dk, dgk = back(zk_ref[...], dkn_ref[...], gk_ref[...], 1.0)
        dgq_ref[...] += dgq
        dgk_ref[...] += dgk
        dz_ref[:, 0:512] = dq.astype(BF16)
        dz_ref[:, 512:1024] = dk.astype(BF16)
        dz_ref[:, 1024:1536] = dv_ref[...].astype(BF16)

    blk = pl.BlockSpec((tm, 512), lambda i: (i, 0))
    vec = pl.BlockSpec((1, 512), lambda i: (0, 0))
    return _call(
        body, [z, z, dz, dqn, dkn, dv, gq, gk, ones_bd], name="qkv_bwd", grid=(t // tm,),
        out_shape=[_sds((t, D_IN), BF16), _sds((1, 512), F32), _sds((1, 512), F32)],
        in_specs=[pl.BlockSpec((tm, 512), lambda i: (i, 3)), pl.BlockSpec((tm, 512), lambda i: (i, 4)),
                  ANY] + [blk] * 3 + [vec, vec, pl.BlockSpec((512, 512), lambda i: (0, 0))],
        out_specs=[pl.BlockSpec((tm, 1536), lambda i: (i, 1)), vec, vec],
        carry=carry, aliases={2: 0})


def _columns_from_chips(g):
    return g.transpose(1, 0, 2).reshape(g.shape[1], N_CHIPS * g.shape[2])


def kernel(x, g_mix, w_in, conv_w, g_q, g_k, g_conv_out, g_attn_out, w_out, g_ffn, w_gate, w_up, w_down, loss_target, m_g_mix, m_w_in, m_conv_w, m_g_q, m_g_k, m_g_conv_out, m_g_attn_out, m_w_out, m_g_ffn, m_w_gate, m_w_up, m_w_down, v_g_mix, v_w_in, v_conv_w, v_g_q, v_g_k, v_g_conv_out, v_g_attn_out, v_w_out, v_g_ffn, v_w_gate, v_w_up, v_w_down):
    t = x.shape[1]
    xs = x[0]
    target = loss_target[0]
    tm = min(512, t)
    tmm = min(2048, t)

    cw_pad = jnp.pad(conv_w[0], ((0, 13), (0, 0)))
    gathered = _all_gather([w_in[0].astype(BF16), cw_pad])
    win = _columns_from_chips(gathered[0])
    cw = jnp.pad(gathered[1][:, 0:3, :].transpose(1, 0, 2).reshape(3, D_CONV), ((0, 5), (0, 0)))
    later = [w_out[0].astype(BF16), w_gate[0].T.astype(BF16), w_up[0].T.astype(BF16),
             w_down[0].astype(BF16)]

    head_id = jnp.arange(D_ATTN) // HEAD_DIM
    ones_bd = (head_id[:, None] == head_id[None, :]).astype(BF16)
    gq_t = jnp.tile(g_q, (1, D_ATTN // HEAD_DIM))
    gk_t = jnp.tile(g_k, (1, D_ATTN // HEAD_DIM))

    h1, z, *dilated = _in_proj(xs, g_mix, win, gq_t, gk_t, ones_bd, tm)
    nd = len(DILATIONS)
    qs, ks, vs = dilated[:nd], dilated[nd:2 * nd], dilated[2 * nd:]
    (y_attn, lse), gathered = _attn_fwd(qs, ks, vs, carry=_x_gather_chips(later))
    mix, gathered = _mixer_out(z, cw, y_attn, g_conv_out, g_attn_out, tm,
                               carry=_x_gather_sibling(gathered))
    wout = gathered[0].reshape(D_MODEL, D_MODEL)
    wgate_t = gathered[1].reshape(D_FF, D_MODEL)
    wup_t = gathered[2].reshape(D_FF, D_MODEL)
    wdown = gathered[3].reshape(D_FF, D_MODEL)
    (x1,) = _matmul("out_proj", mix, wout, [xs], [F32], lambda acc, r: (r + acc,), tm, D_MODEL)
    h2, gate, up, act = _norm_matmul("ffn_up", x1, g_ffn, [wgate_t, wup_t], tm, D_FF, True, BF16,
                                     transposed_w=True)

    def loss_epilogue(acc, r, tgt):
        err = r + acc - tgt
        dy = err * (1.0 / D_MODEL)
        return dy, dy, jnp.sum(err * err)

    dx2, dx2b, loss_sum = _matmul("ffn_down_loss", act, wdown, [x1, target], [F32, BF16],
                                  loss_epilogue, tm, D_MODEL, loss=True)

    def swiglu_bwd(da, gt, u):
        gt, u = gt.astype(F32), u.astype(F32)
        s = _sigmoid(gt)
        return da * u * (s * (1.0 + gt * (1.0 - s))), da * (gt * s)

    dgate, dup = _matmul("ffn_down_bwd", dx2b, wdown, [gate, up], [BF16, BF16], swiglu_bwd,
                         tm, D_FF, transposed_w=True)
    gw_down = _matmul_tn("grad_w_down", act, dx2b, 512, tmm)
    gw_gate_t = _matmul_tn("grad_w_gate", dgate, h2, 512, tmm)
    gw_up_t = _matmul_tn("grad_w_up", dup, h2, 512, tmm)

    me = 2 * lax.axis_index("x") + lax.axis_index("y")
    where = jnp.stack([lax.axis_index("c"), me]).astype(jnp.int32)

    def pair_sums(names, full, got):
        return [_pair_sum(f"pair_sum_{nme}", a, b, where) for nme, a, b in zip(names, full, got)]

    def chip_sums(names, pair, got):
        return [_chip_sum(f"chip_sum_{nme}", own, b) for nme, (_, own), b in zip(names, pair, got)]

    ffn = ["w_gate", "w_up", "w_down"]
    full = [g.reshape(N_CHIPS, D_FF // N_CHIPS, D_MODEL) for g in (gw_gate_t, gw_up_t, gw_down)]
    (dx1, dx1b, gg_ffn), got = _matmul_norm_bwd(
        "ffn_up_bwd", [(dgate, wgate_t), (dup, wup_t)], x1, dx2, g_ffn, tm, carry=_x_pair(full),
        transposed_w=False)
    pair = pair_sums(ffn, full, got)
    gw_out = _matmul_tn("grad_w_out", mix, dx1b, 512, tmm)
    full = [gw_out.reshape(N_CHIPS, D_MODEL // N_CHIPS, D_MODEL)]
    (dzc, do, dd, gcw, gg_conv, gg_attn), got = _mixer_bwd(
        z, dx1b, wout, y_attn, cw, g_conv_out, g_attn_out, ones_bd, tm, carry=_x_pair(full))
    pair += pair_sums(["w_out"], full, got)
    early = ffn + ["w_out"]
    (dqn, dkn, dv), got = _attn_bwd(qs, ks, vs, do, lse, dd, carry=_x_chips([p for p, _ in pair]))
    mine = chip_sums(early, pair, got)
    (dz, gg_q, gg_k), theirs = _qkv_bwd(z, dzc, dqn, dkn, dv, gq_t, gk_t, ones_bd, tm,
                                        carry=_x_share(mine))
    full = [_matmul_tn("grad_w_in", h1, dz, D_IN // N_CHIPS, tmm, by_chip=True)]
    got = _exchange_alone("grad_pair_exchange_w_in", _x_pair(full))
    pair = pair_sums(["w_in"], full, got)
    (grad_x, _, gg_mix), got = _matmul_norm_bwd("in_proj_bwd", [(dz, win)], xs, dx1, g_mix, tm,
                                                carry=_x_chips([pair[0][0]]))
    mine += chip_sums(["w_in"], pair, got)
    theirs = list(theirs) + list(_exchange_alone("grad_pair_share_w_in", _x_share(mine[-1:])))
    big = early + ["w_in"]

    small = _small_all_reduce({
        "g_mix": gg_mix, "g_ffn": gg_ffn, "g_conv_out": gg_conv, "g_attn_out": gg_attn,
        "g_q": gg_q, "g_k": gg_k, "loss": loss_sum, "conv_w": gcw})
    heads = D_ATTN // HEAD_DIM
    grads = {
        "g_mix": small[0:1, :], "g_ffn": small[1:2, :],
        "g_conv_out": small[2:3, 0:512], "g_attn_out": small[2:3, 512:1024],
        "g_q": small[3, 0:512].reshape(heads, HEAD_DIM).sum(axis=0)[None, :],
        "g_k": small[3, 512:1024].reshape(heads, HEAD_DIM).sum(axis=0)[None, :],
        "conv_w": lax.dynamic_slice(small[8:11, 0:512], (0, me * (D_CONV // N_CHIPS)),
                                    (3, D_CONV // N_CHIPS)),
    }
    halves = dict(zip(big, zip(mine, theirs)))
    loss = small[4, 0] * 0.5 * (1.0 / D_MODEL)

    weights = dict(g_mix=g_mix, w_in=w_in, conv_w=conv_w, g_q=g_q, g_k=g_k, g_conv_out=g_conv_out,
                   g_attn_out=g_attn_out, w_out=w_out, g_ffn=g_ffn, w_gate=w_gate, w_up=w_up,
                   w_down=w_down)
    moments_m = dict(g_mix=m_g_mix, w_in=m_w_in, conv_w=m_conv_w, g_q=m_g_q, g_k=m_g_k,
                     g_conv_out=m_g_conv_out, g_attn_out=m_g_attn_out, w_out=m_w_out, g_ffn=m_g_ffn,
                     w_gate=m_w_gate, w_up=m_w_up, w_down=m_w_down)
    moments_v = dict(g_mix=v_g_mix, w_in=v_w_in, conv_w=v_conv_w, g_q=v_g_q, g_k=v_g_k,
                     g_conv_out=v_g_conv_out, g_attn_out=v_g_attn_out, w_out=v_w_out, g_ffn=v_g_ffn,
                     w_gate=v_w_gate, w_up=v_w_up, w_down=v_w_down)
    names = list(weights)
    out_g, out_d, out_m, out_v = [], [], [], []
    for nme in names:
        wgt = weights[nme]
        shape2 = wgt.shape[-2:] if wgt.ndim == 3 else wgt.shape
        flip = nme in ("w_gate", "w_up")

        def to2d(a):
            return a.reshape(shape2).T if flip else a.reshape(shape2)

        def back(a):
            return (a.T if flip else a).reshape(wgt.shape)

        state = (to2d(wgt), to2d(moments_m[nme]), to2d(moments_v[nme]))
        if nme in halves:
            g2, dlt, nm, nv = _adamw_shard(f"adamw_{nme}", *state, *halves[nme], where)
        else:
            g2 = grads[nme].reshape(shape2)
            dlt, nm, nv = _adamw(f"adamw_{nme}", state[0], g2, state[1], state[2])
        out_g.append(back(g2))
        out_d.append(back(dlt))
        out_m.append(back(nm))
        out_v.append(back(nv))
    return (loss, grad_x[None], *out_g, *out_d, *out_m, *out_v)
```

```python
import functools
from typing import Any, Callable, NamedTuple, Sequence

import jax
import jax.numpy as jnp
from jax import lax
from jax.experimental import pallas as pl
from jax.experimental.pallas import tpu as pltpu

F32 = jnp.float32
BF16 = jnp.bfloat16
MESH = pl.DeviceIdType.MESH

D_MODEL = 1024
D_CONV = 512
D_ATTN = 512
HEAD_DIM = 64
D_FF = 2816
D_IN = 3 * D_CONV + 3 * D_ATTN
DILATIONS = (1, 4, 16)
BAND = 128
EPS = 1e-6
NEG = -1e30
N_CHIPS = 4

ADAM_LR = 0.001
ADAM_B1 = 0.9
ADAM_B2 = 0.999
ADAM_EPS = 1e-08
ADAM_WD = 0.01
ADAM_STEP = 10

V7X_VMEM_BYTES = 64 * 1024 * 1024
VMEM_LIMIT = V7X_VMEM_BYTES - 8 * 1024 * 1024
ANY = pl.BlockSpec(memory_space=pl.ANY)
VMEM_WHOLE = pl.BlockSpec(memory_space=pltpu.VMEM)


def _params(*sem):
    return pltpu.CompilerParams(dimension_semantics=sem, vmem_limit_bytes=VMEM_LIMIT)


def _sds(shape, dtype):
    return jax.ShapeDtypeStruct(shape, dtype)


def _resident(whole):
    return pl.Buffered(1) if whole else None


def _place():
    x, y, c = lax.axis_index("x"), lax.axis_index("y"), lax.axis_index("c")
    chips = [(1 - x, y), (x, 1 - y), (1 - x, 1 - y)]
    return x, y, c, 2 * x + y, chips, [2 * cx + cy for cx, cy in chips]


def _all_gather(shards):
    n = len(shards)

    def body(*refs):
        ins, outs, stage = refs[:n], refs[n:2 * n], refs[2 * n:3 * n]
        ssem, rsem, fsem, gsem, lsem, osem = refs[3 * n:]
        x, y, c, me, chips, cids = _place()
        sib = (x, y, 1 - c)

        def half(w, which):
            h = shards[w].shape[0] // 2
            return pl.ds(pl.multiple_of(which * h, 8), h)

        loads = [pltpu.make_async_copy(ins[w], stage[w], lsem.at[w]) for w in range(n)]
        local = [pltpu.make_async_copy(stage[w], outs[w].at[me], osem.at[w]) for w in range(n)]
        for cp in loads:
            cp.start()

        def chip_copy(w, j, src_slot):
            rows = half(w, c)
            return pltpu.make_async_remote_copy(
                src_ref=ins[w].at[rows], dst_ref=outs[w].at[src_slot, rows],
                send_sem=ssem.at[3 * w + j], recv_sem=rsem.at[3 * w + j],
                device_id=(*chips[j], c), device_id_type=MESH)

        def sib_copy(w, j, which):
            rows = half(w, which)
            return pltpu.make_async_remote_copy(
                src_ref=outs[w].at[cids[j], rows], dst_ref=outs[w].at[cids[j], rows],
                send_sem=fsem.at[3 * w + j], recv_sem=gsem.at[3 * w + j],
                device_id=sib, device_id_type=MESH)

        sends = [chip_copy(w, j, me) for w in range(n) for j in range(3)]
        for cp in sends:
            cp.start()
        for w in range(n):
            loads[w].wait()
            local[w].start()
        passed = []
        for w in range(n):
            for j in range(3):
                chip_copy(w, j, cids[j]).wait_recv()
                cp = sib_copy(w, j, c)
                cp.start()
                passed.append(cp)
        for w in range(n):
            for j in range(3):
                sib_copy(w, j, 1 - c).wait_recv()
        for cp in sends + passed:
            cp.wait_send()
        for cp in local:
            cp.wait()

    return pl.pallas_call(
        body, name="all_gather_weights",
        out_shape=[_sds((N_CHIPS,) + s.shape, s.dtype) for s in shards],
        in_specs=[ANY] * n, out_specs=[ANY] * n,
        scratch_shapes=[pltpu.VMEM(s.shape, s.dtype) for s in shards]
        + [pltpu.SemaphoreType.DMA((3 * n,))] * 4 + [pltpu.SemaphoreType.DMA((n,))] * 2,
        compiler_params=pltpu.CompilerParams(vmem_limit_bytes=VMEM_LIMIT),
    )(*shards)


class _Exchange(NamedTuple):
    srcs: Sequence[Any]
    lands: Sequence[Any]
    outs: Sequence[Any]
    n_sems: int
    copies: Callable


def _remote(src, dst, ssem, rsem, k, to):
    return pltpu.make_async_remote_copy(src_ref=src, dst_ref=dst, send_sem=ssem.at[k],
                                        recv_sem=rsem.at[k], device_id=to, device_id_type=MESH)


def _x_gather_chips(shards):
    def copies(srcs, lands, outs, ssem, rsem):
        _, _, c, me, chips, cids = _place()
        go, arrive = [], []
        for w, s in enumerate(shards):
            h = s.shape[0] // 2
            rows = pl.ds(pl.multiple_of(c * h, 8), h)
            for j in range(3):
                to = (*chips[j], c)
                go.append(_remote(srcs[w].at[rows], lands[w].at[me, rows], ssem, rsem, 3 * w + j, to))
                arrive.append(_remote(srcs[w].at[rows], lands[w].at[cids[j], rows], ssem, rsem,
                                      3 * w + j, to))
        return go, arrive

    lands = [jnp.broadcast_to(s[None], (N_CHIPS,) + s.shape) for s in shards]
    return _Exchange(shards, lands, [], 3 * len(shards), copies)


def _x_gather_sibling(gathered):
    def copies(srcs, lands, outs, ssem, rsem):
        x, y, c, _, _, cids = _place()
        go, arrive = [], []
        for w, g in enumerate(gathered):
            h = g.shape[1] // 2
            mine = pl.ds(pl.multiple_of(c * h, 8), h)
            theirs = pl.ds(pl.multiple_of((1 - c) * h, 8), h)
            for j in range(3):
                slab = lands[w].at[cids[j]]
                go.append(_remote(slab.at[mine], slab.at[mine], ssem, rsem, 3 * w + j, (x, y, 1 - c)))
                arrive.append(_remote(slab.at[theirs], slab.at[theirs], ssem, rsem, 3 * w + j,
                                      (x, y, 1 - c)))
        return go, arrive

    return _Exchange([], gathered, [], 3 * len(gathered), copies)


def _x_pair(grads):
    def copies(srcs, lands, outs, ssem, rsem):
        x, y, c, _, _, _ = _place()
        go = []
        for w, g in enumerate(grads):
            h = g.shape[1] // 2
            theirs = pl.ds(pl.multiple_of((1 - c) * h, 8), h)
            go.append(_remote(srcs[w].at[:, theirs, :], outs[w], ssem, rsem, w, (x, y, 1 - c)))
        return go, go

    outs = [_sds((N_CHIPS, g.shape[1] // 2, g.shape[2]), g.dtype) for g in grads]
    return _Exchange(grads, [], outs, len(grads), copies)


def _x_chips(parts):
    def copies(srcs, lands, outs, ssem, rsem):
        _, _, c, _, chips, cids = _place()
        go = [_remote(srcs[w].at[cids[j]], outs[w].at[j], ssem, rsem, 3 * w + j, (*chips[j], c))
              for w in range(len(parts)) for j in range(3)]
        return go, go

    outs = [_sds((3,) + p.shape[1:], p.dtype) for p in parts]
    return _Exchange(parts, [], outs, 3 * len(parts), copies)


def _x_share(halves):
    def copies(srcs, lands, outs, ssem, rsem):
        x, y, c, _, _, _ = _place()
        go = [_remote(srcs[w], outs[w], ssem, rsem, w, (x, y, 1 - c)) for w in range(len(halves))]
        return go, go

    return _Exchange(halves, [], [_sds(h.shape, h.dtype) for h in halves], len(halves), copies)


def _call(body, args, *, name, grid, in_specs, out_specs, out_shape, scratch_shapes=(),
          semantics=None, carry=None, aliases=None):
    single = not isinstance(out_shape, (list, tuple))
    out_shape = [out_shape] if single else list(out_shape)
    out_specs = [out_specs] if single else list(out_specs)
    aliases = dict(aliases or {})
    if carry is None:
        res = pl.pallas_call(
            body, name=name, grid=grid, in_specs=list(in_specs), out_specs=out_specs,
            out_shape=out_shape, scratch_shapes=list(scratch_shapes), input_output_aliases=aliases,
            compiler_params=_params(*(semantics or ("arbitrary",) * len(grid))))(*args)
        return res[0] if single else res
    n_in, n_out, n_scr = len(args), len(out_shape), len(scratch_shapes)
    n_src, n_land, n_new = len(carry.srcs), len(carry.lands), len(carry.outs)

    def carrying(*refs):
        at = 0
        parts = []
        for n in (n_in, n_src, n_land, n_out, n_land, n_new, n_scr, 2):
            parts.append(refs[at:at + n])
            at += n
        ins, srcs, _, outs, lands, news, scratch, (ssem, rsem) = parts
        ids = [pl.program_id(a) for a in range(len(grid))]
        first = functools.reduce(jnp.logical_and, [i == 0 for i in ids])
        last = functools.reduce(jnp.logical_and, [i == g - 1 for i, g in zip(ids, grid)])
        go, arrive = carry.copies(srcs, lands, news, ssem, rsem)

        @pl.when(first)
        def _():
            for cp in go:
                cp.start()

        body(*ins, *outs, *scratch)

        @pl.when(last)
        def _():
            for cp in go:
                cp.wait_send()
            for cp in arrive:
                cp.wait_recv()

    res = pl.pallas_call(
        carrying, name=name, grid=grid,
        in_specs=list(in_specs) + [ANY] * (n_src + n_land),
        out_specs=out_specs + [ANY] * (n_land + n_new),
        out_shape=out_shape + [_sds(a.shape, a.dtype) for a in carry.lands] + list(carry.outs),
        input_output_aliases={**aliases, **{n_in + n_src + i: n_out + i for i in range(n_land)}},
        scratch_shapes=list(scratch_shapes) + [pltpu.SemaphoreType.DMA((carry.n_sems,))] * 2,
        compiler_params=_params(*(("arbitrary",) * len(grid))))(*args, *carry.srcs, *carry.lands)
    own = res[:n_out]
    return (own[0] if single else own), res[n_out:]


def _exchange_alone(name, exchange):
    def body(x_ref, o_ref):
        o_ref[...] = x_ref[...]

    blk = pl.BlockSpec((8, 128), lambda i: (0, 0))
    _, res = _call(body, [jnp.zeros((8, 128), F32)], name=name, grid=(1,), in_specs=[blk],
                   out_specs=blk, out_shape=_sds((8, 128), F32), carry=exchange)
    return res


def _row_block(r, want):
    return max(d for d in range(1, min(want, r) + 1) if r % d == 0 and (d % 8 == 0 or d == r))


def _pair_sum(name, full, got, where):
    _, r, n = full.shape
    h = r // 2
    tr = _row_block(h, 256)
    nb = h // tr

    def body(w_ref, a_ref, b_ref, o_ref, own_ref):
        total = a_ref[...] + b_ref[...]
        o_ref[...] = total.astype(BF16)

        @pl.when(pl.program_id(1) == w_ref[1])
        def _():
            own_ref[...] = total[0]

    blk = pl.BlockSpec((1, tr, n), lambda i, s, w: (s, i, 0))
    return pl.pallas_call(
        body, name=name, out_shape=[_sds(got.shape, BF16), _sds((h, n), F32)],
        grid_spec=pltpu.PrefetchScalarGridSpec(
            num_scalar_prefetch=1, grid=(nb, N_CHIPS),
            in_specs=[pl.BlockSpec((1, tr, n), lambda i, s, w: (s, w[0] * nb + i, 0)), blk],
            out_specs=[blk, pl.BlockSpec((tr, n), lambda i, s, w: (i, 0))]),
        compiler_params=_params("parallel", "arbitrary"),
    )(where, full, got)


def _chip_sum(name, own, got):
    h, n = own.shape
    tr = _row_block(h, 256)

    def body(a_ref, b0, b1, b2, o_ref):
        o_ref[...] = ((a_ref[...] + b0[0].astype(F32)) + b1[0].astype(F32)) + b2[0].astype(F32)

    def slot(j):
        return pl.BlockSpec((1, tr, n), lambda i: (j, i, 0))

    blk = pl.BlockSpec((tr, n), lambda i: (i, 0))
    return pl.pallas_call(
        body, name=name, grid=(h // tr,), out_shape=_sds((h, n), F32),
        in_specs=[blk, slot(0), slot(1), slot(2)], out_specs=blk,
        compiler_params=_params("parallel"),
    )(own, got, got, got)


SMALL_ROWS = 16
SMALL_LAYOUT = (
    ("g_mix", 0, 0, 1, 1024), ("g_ffn", 1, 0, 1, 1024), ("g_conv_out", 2, 0, 1, 512),
    ("g_attn_out", 2, 512, 1, 512), ("g_q", 3, 0, 1, 512), ("g_k", 3, 512, 1, 512),
    ("loss", 4, 0, 1, 128), ("conv_w", 8, 0, 8, 512))


def _small_all_reduce(parts):
    names = [s[0] for s in SMALL_LAYOUT]

    def body(*refs):
        ins = refs[:len(names)]
        out_ref, stage, buf, ssem, rsem = refs[len(names):]
        x, y, c, _, _, _ = _place()
        me = 4 * x + 2 * y + c
        stage[...] = jnp.zeros_like(stage)
        for ref, (_, r0, c0, nr, nc) in zip(ins, SMALL_LAYOUT):
            stage[r0:r0 + nr, c0:c0 + nc] = ref[0:nr, :]
        buf[me] = stage[...]
        peers = []
        for d in range(1, 8):
            px = 1 - x if d & 4 else x
            py = 1 - y if d & 2 else y
            pc = 1 - c if d & 1 else c
            peers.append(((px, py, pc), 4 * px + 2 * py + pc))
        sends = [pltpu.make_async_remote_copy(
            src_ref=stage, dst_ref=buf.at[me], send_sem=ssem.at[k], recv_sem=rsem.at[k],
            device_id=peer, device_id_type=MESH) for k, (peer, _) in enumerate(peers)]
        for cp in sends:
            cp.start()
        for k, (peer, pid) in enumerate(peers):
            pltpu.make_async_remote_copy(
                src_ref=stage, dst_ref=buf.at[pid], send_sem=ssem.at[k], recv_sem=rsem.at[k],
                device_id=peer, device_id_type=MESH).wait_recv()
        for cp in sends:
            cp.wait_send()
        acc = buf[0]
        for k in range(1, 8):
            acc = acc + buf[k]
        out_ref[...] = acc

    return pl.pallas_call(
        body, name="small_all_reduce", out_shape=_sds((SMALL_ROWS, 1024), F32),
        in_specs=[VMEM_WHOLE] * len(names), out_specs=VMEM_WHOLE,
        scratch_shapes=[pltpu.VMEM((SMALL_ROWS, 1024), F32), pltpu.VMEM((8, SMALL_ROWS, 1024), F32),
                        pltpu.SemaphoreType.DMA((7,)), pltpu.SemaphoreType.DMA((7,))],
    )(*[parts[k] for k in names])


def _dot(a, b):
    return jnp.dot(a, b, preferred_element_type=F32)


def _dot_nt(a, b):
    return lax.dot_general(a, b, (((1,), (1,)), ((), ())), preferred_element_type=F32)


def _dot_tn(a, b):
    return lax.dot_general(a, b, (((0,), (0,)), ((), ())), preferred_element_type=F32)


def _sigmoid(v):
    return 1.0 / (1.0 + jnp.exp(-v))


def _rms_scale(v):
    return lax.rsqrt(jnp.mean(v * v, axis=-1, keepdims=True) + EPS)


def _rms_bwd(v, r, g, dy):
    vh = v * r
    dh = dy * g
    return r * (dh - vh * jnp.mean(dh * vh, axis=-1, keepdims=True)), vh


def _head_sum(a, ones_bd):
    hi = a.astype(BF16)
    lo = (a - hi.astype(F32)).astype(BF16)
    return _dot(hi, ones_bd) + _dot(lo, ones_bd)


def _head_rms_scale(v, ones_bd):
    return lax.rsqrt(_head_sum(v * v, ones_bd) * (1.0 / HEAD_DIM) + EPS)


MXU_COLUMNS = 256


def _column_chunks(n):
    width = MXU_COLUMNS if n % MXU_COLUMNS == 0 else n
    return [slice(c, c + width) for c in range(0, n, width)]


def _norm_matmul(name, x, g, ws, tm, tn, swiglu, out_dtype=F32, transposed_w=False):
    t, d = x.shape
    n = ws[0].shape[0] if transposed_w else ws[0].shape[1]
    nw = len(ws)

    def body(x_ref, g_ref, *refs):
        w_refs, h_ref, o_refs = refs[:nw], refs[nw], refs[nw + 1:2 * nw + 1]
        hs = refs[-1]

        @pl.when(pl.program_id(1) == 0)
        def _():
            xv = x_ref[...]
            h = (xv * _rms_scale(xv) * g_ref[...]).astype(BF16)
            hs[...] = h
            h_ref[...] = h

        h = hs[...]
        for cols in _column_chunks(tn):
            outs = [_dot_nt(h, w[cols, :]) if transposed_w else _dot(h, w[:, cols]) for w in w_refs]
            for o_ref, o in zip(o_refs, outs):
                o_ref[:, cols] = o.astype(out_dtype)
            if swiglu:
                refs[2 * nw + 1][:, cols] = (outs[0] * _sigmoid(outs[0]) * outs[1]).astype(BF16)

    row = pl.BlockSpec((tm, d), lambda i, j: (i, 0))
    col = pl.BlockSpec((tm, tn), lambda i, j: (i, j))
    out_shape = [_sds((t, d), BF16)] + [_sds((t, n), out_dtype)] * nw
    out_specs = [row] + [col] * nw
    if swiglu:
        out_shape.append(_sds((t, n), BF16))
        out_specs.append(col)
    return pl.pallas_call(
        body, name=name, grid=(t // tm, n // tn), out_shape=out_shape,
        in_specs=[row, pl.BlockSpec((1, d), lambda i, j: (0, 0))]
        + [pl.BlockSpec((tn, d), lambda i, j: (j, 0), pipeline_mode=_resident(tn == n))
           if transposed_w
           else pl.BlockSpec((d, tn), lambda i, j: (0, j), pipeline_mode=_resident(tn == n))] * nw,
        out_specs=out_specs, scratch_shapes=[pltpu.VMEM((tm, d), BF16)],
        compiler_params=_params("parallel", "arbitrary"),
    )(x, g, *ws)


def _matmul(name, a, w, extras, out_dtypes, epilogue, tm, tn, transposed_w=False, loss=False):
    t, k = a.shape
    n = w.shape[0] if transposed_w else w.shape[1]
    ne, no = len(extras), len(out_dtypes)

    def body(a_ref, w_ref, *refs):
        e_refs, o_refs = refs[:ne], refs[ne:]
        a = a_ref[...]
        total = 0.0
        for cols in _column_chunks(tn):
            acc = _dot_nt(a, w_ref[cols, :]) if transposed_w else _dot(a, w_ref[:, cols])
            res = epilogue(acc, *[e[:, cols] for e in e_refs])
            for o_ref, r in zip(o_refs[:no], res[:no]):
                o_ref[:, cols] = r.astype(o_ref.dtype)
            if loss:
                total = total + res[no]
        if loss:
            first = jnp.logical_and(pl.program_id(0) == 0, pl.program_id(1) == 0)

            @pl.when(first)
            def _():
                o_refs[no][...] = jnp.zeros_like(o_refs[no])

            o_refs[no][...] += total

    col = pl.BlockSpec((tm, tn), lambda i, j: (i, j))
    w_spec = (pl.BlockSpec((tn, k), lambda i, j: (j, 0), pipeline_mode=_resident(tn == n))
              if transposed_w
              else pl.BlockSpec((k, tn), lambda i, j: (0, j), pipeline_mode=_resident(tn == n)))
    out_shape = [_sds((t, n), dt) for dt in out_dtypes]
    out_specs = [col] * no
    if loss:
        out_shape.append(_sds((8, 128), F32))
        out_specs.append(pl.BlockSpec((8, 128), lambda i, j: (0, 0)))
    return pl.pallas_call(
        body, name=name, grid=(t // tm, n // tn), out_shape=out_shape,
        in_specs=[pl.BlockSpec((tm, k), lambda i, j: (i, 0)), w_spec] + [col] * ne,
        out_specs=out_specs,
        compiler_params=_params(*(("arbitrary", "arbitrary") if loss else ("parallel", "parallel"))),
    )(a, w, *extras)


def _matmul_norm_bwd(name, pairs, x, dres, g, tm, carry=None, transposed_w=True):
    t, d = x.shape
    npairs = len(pairs)
    product = _dot_nt if transposed_w else _dot

    def body(*refs):
        a_refs, w_refs = refs[:npairs], refs[npairs:2 * npairs]
        x_ref, r_ref, g_ref, dx_ref, dxb_ref, dg_ref = refs[2 * npairs:]
        dy = product(a_refs[0][...], w_refs[0][...])
        for a_ref, w_ref in zip(a_refs[1:], w_refs[1:]):
            dy = dy + product(a_ref[...], w_ref[...])
        xv = x_ref[...]
        dx, xh = _rms_bwd(xv, _rms_scale(xv), g_ref[...], dy)
        dx = dx + r_ref[...]
        dx_ref[...] = dx
        dxb_ref[...] = dx.astype(BF16)

        @pl.when(pl.program_id(0) == 0)
        def _():
            dg_ref[...] = jnp.zeros_like(dg_ref)

        dg_ref[...] += jnp.sum(dy * xh, axis=0, keepdims=True)

    row = pl.BlockSpec((tm, d), lambda i: (i, 0))
    vec = pl.BlockSpec((1, d), lambda i: (0, 0))
    return _call(
        body, [a for a, _ in pairs] + [w for _, w in pairs] + [x, dres, g], name=name,
        grid=(t // tm,), out_shape=[_sds((t, d), F32), _sds((t, d), BF16), _sds((1, d), F32)],
        in_specs=[pl.BlockSpec((tm, a.shape[1]), lambda i: (i, 0)) for a, _ in pairs]
        + [pl.BlockSpec(w.shape, lambda i: (0, 0), pipeline_mode=pl.Buffered(1)) for _, w in pairs]
        + [row, row, vec],
        out_specs=[row, row, vec], carry=carry)


def _matmul_tn(name, a, g, tn, tk, by_chip=False):
    t, ka = a.shape
    n = g.shape[1]

    def body(a_ref, g_ref, o_ref):
        @pl.when(pl.program_id(1) == 0)
        def _():
            o_ref[...] = jnp.zeros_like(o_ref)

        acc = _dot_tn(a_ref[...], g_ref[...])
        o_ref[...] += acc[None] if by_chip else acc

    return pl.pallas_call(
        body, name=name, grid=(n // tn, t // tk),
        out_shape=_sds((n // tn, ka, tn) if by_chip else (ka, n), F32),
        in_specs=[pl.BlockSpec((tk, ka), lambda j, s: (s, 0)),
                  pl.BlockSpec((tk, tn), lambda j, s: (s, j))],
        out_specs=(pl.BlockSpec((1, ka, tn), lambda j, s: (j, 0, 0)) if by_chip
                   else pl.BlockSpec((ka, tn), lambda j, s: (0, j))),
        compiler_params=_params("parallel", "arbitrary"),
    )(a, g)


def _elementwise(name, fn, ins, out_dtypes, tr):
    r, n = ins[0].shape
    tr = _row_block(r, tr)
    ni = len(ins)

    def body(*refs):
        res = fn(*[ref[...] for ref in refs[:ni]])
        for o_ref, v in zip(refs[ni:], res):
            o_ref[...] = v.astype(o_ref.dtype)

    blk = pl.BlockSpec((tr, n), lambda i: (i, 0))
    return pl.pallas_call(
        body, name=name, grid=(r // tr,), out_shape=[_sds((r, n), dt) for dt in out_dtypes],
        in_specs=[blk] * ni, out_specs=[blk] * len(out_dtypes),
        compiler_params=_params("parallel"),
    )(*ins)


def _adamw_update(w, g, m, v):
    m = ADAM_B1 * m + (1.0 - ADAM_B1) * g
    v = ADAM_B2 * v + (1.0 - ADAM_B2) * (g * g)
    m_hat = m / (1.0 - ADAM_B1 ** ADAM_STEP)
    v_hat = v / (1.0 - ADAM_B2 ** ADAM_STEP)
    return -ADAM_LR * (m_hat / (jnp.sqrt(v_hat) + ADAM_EPS) + ADAM_WD * w), m, v


def _adamw(name, w, g, m, v):
    return _elementwise(name, _adamw_update, [w, g, m, v], [F32] * 3, 256)


def _adamw_shard(name, w, m, v, mine, theirs, where):
    r, n = w.shape
    h = r // 2
    tr = _row_block(h, 256)
    nb = h // tr

    def body(w_ref, p_ref, m_ref, v_ref, a_ref, b_ref, g_ref, d_ref, nm_ref, nv_ref):
        g = jnp.where(pl.program_id(0) == w_ref[0], a_ref[...], b_ref[...])
        g_ref[...] = g
        d_ref[...], nm_ref[...], nv_ref[...] = _adamw_update(p_ref[...], g, m_ref[...], v_ref[...])

    whole = pl.BlockSpec((tr, n), lambda s, i, c: (s * nb + i, 0))
    used = pl.BlockSpec((tr, n), lambda s, i, c: (jnp.where(s == c[0], i, 0), 0))
    unused = pl.BlockSpec((tr, n), lambda s, i, c: (jnp.where(s == c[0], 0, i), 0))
    return pl.pallas_call(
        body, name=name, out_shape=[_sds((r, n), F32)] * 4,
        grid_spec=pltpu.PrefetchScalarGridSpec(
            num_scalar_prefetch=1, grid=(2, nb), in_specs=[whole] * 3 + [used, unused],
            out_specs=[whole] * 4),
        compiler_params=_params("arbitrary", "arbitrary"),
    )(where, w, m, v, mine, theirs)


PAIRS = D_ATTN // BAND


def _in_proj(x, g, w, gq, gk, ones_bd, tm):
    t, dm = x.shape
    n = w.shape[1]
    nd = len(DILATIONS)
    first = 3 * D_CONV

    def body(x_ref, g_ref, w_ref, gq_ref, gk_ref, bd_ref, h_ref, z_ref, *refs):
        outs, slabs = refs[:3 * nd], refs[3 * nd:]
        xv = x_ref[...]
        h = (xv * _rms_scale(xv) * g_ref[...]).astype(BF16)
        h_ref[...] = h
        for cols in _column_chunks(n):
            z_ref[:, cols] = _dot(h, w_ref[:, cols])
        bd = bd_ref[...]
        q = z_ref[:, first:first + D_ATTN]
        k = z_ref[:, first + D_ATTN:first + 2 * D_ATTN]
        vals = [(q * _head_rms_scale(q, bd) * gq_ref[...]) * HEAD_DIM ** -0.5,
                k * _head_rms_scale(k, bd) * gk_ref[...], z_ref[:, first + 2 * D_ATTN:n]]
        for m, val in enumerate(vals):
            for c in range(PAIRS):
                slabs[0][c] = val[:, c * BAND:(c + 1) * BAND]
            cur, before = 0, 1
            for a, d in enumerate(DILATIONS):
                o_ref, src, dst = outs[m * nd + a], slabs[cur], slabs[1 - cur]
                step, count = d // before, tm // d
                keep = step > 1 and a + 1 < nd
                for c in range(PAIRS):
                    for r in range(d):
                        start = (r % before) * (tm // before) + r // before
                        rows = src.at[c][pl.ds(start, count, stride=step), :] if step > 1 else src[c]
                        o_ref[c, r] = rows.astype(BF16)
                        if keep:
                            dst.at[c][pl.ds(r * count, count), :] = rows
                if keep:
                    cur = 1 - cur
                before = d

    row = pl.BlockSpec((tm, dm), lambda i: (i, 0))
    vec = pl.BlockSpec((1, D_ATTN), lambda i: (0, 0))
    return pl.pallas_call(
        body, name="in_proj", grid=(t // tm,),
        out_shape=[_sds((t, dm), BF16), _sds((t, n), F32)]
        + [_sds((PAIRS, d, t // d, BAND), BF16) for _ in range(3) for d in DILATIONS],
        in_specs=[row, pl.BlockSpec((1, dm), lambda i: (0, 0)),
                  pl.BlockSpec((dm, n), lambda i: (0, 0), pipeline_mode=_resident(True)), vec, vec,
                  pl.BlockSpec((D_ATTN, D_ATTN), lambda i: (0, 0), pipeline_mode=_resident(True))],
        out_specs=[row, pl.BlockSpec((tm, n), lambda i: (i, 0))]
        + [pl.BlockSpec((PAIRS, d, tm // d, BAND), lambda i: (0, 0, i, 0))
           for _ in range(3) for d in DILATIONS],
        scratch_shapes=[pltpu.VMEM((PAIRS, tm, BAND), F32)] * 2,
        compiler_params=_params("parallel"),
    )(x, g, w, gq, gk, ones_bd)


TOK = 2048
UNITS = TOK // BAND


def _stack_masks():
    row = lax.broadcasted_iota(jnp.int32, (2 * BAND, 2 * BAND), 0) & (BAND - 1)
    col = lax.broadcasted_iota(jnp.int32, (2 * BAND, 2 * BAND), 1)
    lane = lax.broadcasted_iota(jnp.int32, (BAND, BAND), 1)
    head0 = lane < HEAD_DIM
    ones = [jnp.where(head0, 1.0, 0.0).astype(BF16), jnp.where(head0, 0.0, 1.0).astype(BF16)]
    return col - row, col, head0, ones


def _split3(x):
    hi = x.astype(BF16).astype(F32)
    mid = (x - hi).astype(BF16).astype(F32)
    return hi, mid, x - hi - mid


def _gather(srcs, dst, d, before=1):
    per, step, span = TOK // d, d // before, TOK // before
    at = 0
    for r in range(d):
        start = (r % before) * span + r // before
        for src in srcs:
            rows = src[pl.ds(start, per, stride=step), :] if step > 1 else src[pl.ds(start, per), :]
            dst[pl.ds(at, per), :] = rows.astype(dst.dtype)
            at += per


def _scatter(out_ref, src, d):
    per = TOK // d
    if d == 1:
        out_ref[...] = src[...]
        return
    for r in range(d):
        out_ref[pl.ds(r, per, stride=d), :] = src[pl.ds(r * per, per), :]


def _dilated_specs(nblk, reverse):
    def at(s):
        return (nblk - 1 - s) if reverse else s
    main = [pl.BlockSpec((1, d, TOK // d, BAND), lambda j, s: (j, 0, at(s), 0)) for d in DILATIONS]
    prev = [pl.BlockSpec((1, d, TOK // d, BAND), lambda j, s: (j, 0, jnp.maximum(at(s) - 1, 0), 0))
            for d in DILATIONS]
    return main, prev


def _window_rows(prev_ref, main_ref, dst, d):
    per = TOK // d
    for r in range(d):
        dst[pl.ds(r * (per + BAND), BAND), :] = prev_ref[0, r, pl.ds(per - BAND, BAND), :]
        dst[pl.ds(r * (per + BAND) + BAND, per), :] = main_ref[0, r]


def _attn_fwd(qs, ks, vs, carry=None):
    t = qs[0].shape[2]
    nblk = t // TOK
    nd = len(DILATIONS)

    def body(*refs):
        q_refs, kp_refs, k_refs = refs[:nd], refs[nd:2 * nd], refs[2 * nd:3 * nd]
        vp_refs, v_refs = refs[3 * nd:4 * nd], refs[4 * nd:5 * nd]
        y_ref, l_ref, kw_s, vw_s, ob, lb, on, ln = refs[5 * nd:]
        i = pl.program_id(1)
        diff, col, head0, hm = _stack_masks()
        band_ok = jnp.logical_and(diff >= 0, diff <= BAND)
        for g, d in enumerate(DILATIONS):
            per = TOK // d
            nb = per // BAND
            pad = per + BAND
            _window_rows(kp_refs[g], k_refs[g], kw_s, d)
            _window_rows(vp_refs[g], v_refs[g], vw_s, d)
            q_ref = q_refs[g]

            def unit(u, carry):
                r, b = u // nb, u % nb
                qu = q_ref[0, r, pl.ds(pl.multiple_of(b * BAND, BAND), BAND), :]
                start = pl.multiple_of(r * pad + b * BAND, BAND)
                kw = kw_s[pl.ds(start, 2 * BAND), :]
                vw = vw_s[pl.ds(start, 2 * BAND), :]
                lo = jnp.where(jnp.logical_and(i == 0, b == 0), BAND, 0)
                s = _dot_nt(jnp.concatenate([qu * hm[0], qu * hm[1]], axis=0), kw)
                s = jnp.where(jnp.logical_and(band_ok, col >= lo), s, NEG)
                mx = jnp.max(s, axis=-1, keepdims=True)
                e = jnp.exp(s - mx)
                den = jnp.sum(e, axis=-1, keepdims=True)
                o2 = _dot(e.astype(BF16), vw) / den
                l2 = jnp.broadcast_to(mx + jnp.log(den), (2 * BAND, BAND))
                rows = pl.ds(pl.multiple_of(u * BAND, BAND), BAND)
                ob[rows, :] = jnp.where(head0, o2[:BAND], o2[BAND:])
                lb[rows, :] = jnp.where(head0, l2[:BAND], l2[BAND:])
                return carry

            lax.fori_loop(0, UNITS, unit, 0, unroll=16)
            _scatter(on.at[g], ob, d)
            _scatter(ln.at[g], lb, d)
        ls = [ln[0], ln[1], ln[2]]
        mx = jnp.maximum(jnp.maximum(ls[0], ls[1]), ls[2])
        es = [jnp.exp(l - mx) for l in ls]
        tot = es[0] + es[1] + es[2]
        y_ref[...] = (es[0] * on[0] + es[1] * on[1] + es[2] * on[2]) / tot
        l_ref[...] = mx + jnp.log(tot)

    main, prev = _dilated_specs(nblk, False)
    out = pl.BlockSpec((TOK, BAND), lambda j, i: (i, j))
    win_rows = max(d * (TOK // d + BAND) for d in DILATIONS)
    return _call(
        body, list(qs) + list(ks) + list(ks) + list(vs) + list(vs), name="attn_fwd",
        grid=(PAIRS, nblk), out_shape=[_sds((t, D_ATTN), F32)] * 2,
        in_specs=main + prev + main + prev + main, out_specs=[out, out],
        scratch_shapes=[pltpu.VMEM((win_rows, BAND), BF16)] * 2 + [pltpu.VMEM((TOK, BAND), F32)] * 2
        + [pltpu.VMEM((nd, TOK, BAND), F32)] * 2,
        semantics=("parallel", "parallel"), carry=carry)


def _attn_bwd(qs, ks, vs, do, lse, dd, carry=None):
    t = qs[0].shape[2]
    nblk = t // TOK
    nd = len(DILATIONS)
    offs = [sum(DILATIONS[:g]) * BAND for g in range(nd)]

    def body(*refs):
        q_refs, kp_refs, k_refs = refs[:nd], refs[nd:2 * nd], refs[2 * nd:3 * nd]
        vp_refs, v_refs = refs[3 * nd:4 * nd], refs[4 * nd:5 * nd]
        (do_ref, l_ref, d_ref, dq_ref, dk_ref, dv_ref, kw_s, vw_s, dos, lds, pn, dqb, dkb, dvb, ckb,
         cvb, *more) = refs[5 * nd:]
        folds, mids = more[:6], more[6:]
        step = pl.program_id(1)
        i = nblk - 1 - step
        key = lax.broadcasted_iota(jnp.int32, (2 * BAND, 2 * BAND), 0)
        qry = lax.broadcasted_iota(jnp.int32, (2 * BAND, 2 * BAND), 1) & (BAND - 1)
        off = key - qry
        band_ok = jnp.logical_and(off >= 0, off <= BAND)
        lane = lax.broadcasted_iota(jnp.int32, (BAND, BAND), 1)
        head0 = lane < HEAD_DIM
        hm = [jnp.where(head0, 1.0, 0.0).astype(BF16), jnp.where(head0, 0.0, 1.0).astype(BF16)]
        lane2 = lax.broadcasted_iota(jnp.int32, (2 * BAND, BAND), 1) & (HEAD_DIM - 1)
        ones_l = jnp.where(lane2 < 3, 1.0, 0.0).astype(BF16)
        ones_d = jnp.where(jnp.logical_and(lane2 >= 3, lane2 < 6), 1.0, 0.0).astype(BF16)
        piece = lax.broadcasted_iota(jnp.int32, (TOK, BAND), 1) & (HEAD_DIM - 1)

        def pieces(x, at):
            hi, mid, lo = _split3(-x)
            return jnp.where(piece == at, hi,
                             jnp.where(piece == at + 1, mid, jnp.where(piece == at + 2, lo, 0.0)))

        pn[...] = pieces(l_ref[...], 0) + pieces(d_ref[...], 3)
        order = sorted(range(nd), key=lambda a: -DILATIONS[a])
        assert DILATIONS[order[-1]] == 1
        levels = {1: (do_ref, pn)}
        for n, a in enumerate(reversed(order[1:-1])):
            d, before = DILATIONS[a], DILATIONS[order[-1 - n]]
            levels[d] = (mids[2 * n], mids[2 * n + 1])
            for src, dst in zip(levels[before], levels[d]):
                _gather([src], dst, d, before)
        for pos, g in enumerate(order):
            d = DILATIONS[g]
            per = TOK // d
            nb = per // BAND
            pad = per + BAND
            _window_rows(kp_refs[g], k_refs[g], kw_s, d)
            _window_rows(vp_refs[g], v_refs[g], vw_s, d)
            known = d if d in levels else DILATIONS[order[pos + 1]]
            _gather([levels[known][0]], dos, d, known)
            _gather([levels[known][1]], lds, d, known)
            for r in range(d):
                spare = pl.ds(r * pad, BAND)
                dkb[spare, :] = jnp.zeros((BAND, BAND), F32)
                dvb[spare, :] = jnp.zeros((BAND, BAND), F32)
            q_ref = q_refs[g]

            def unit(u, c_):
                r, b = u // nb, u % nb
                rows = pl.ds(pl.multiple_of(u * BAND, BAND), BAND)
                qu = q_ref[0, r, pl.ds(pl.multiple_of(b * BAND, BAND), BAND), :]
                dou, ldu = dos[rows, :], lds[rows, :]
                q2 = jnp.concatenate([qu * hm[0], qu * hm[1]], axis=0)
                do2 = jnp.concatenate([dou * hm[0], dou * hm[1]], axis=0)
                ld2 = jnp.concatenate([ldu * hm[0], ldu * hm[1]], axis=0)
                acc = pl.ds(pl.multiple_of(r * pad + b * BAND, BAND), 2 * BAND)
                kw = kw_s[acc, :]
                vw = vw_s[acc, :]
                lo = jnp.where(jnp.logical_and(i == 0, b == 0), BAND, 0)
                ok = jnp.logical_and(band_ok, key >= lo)
                st = _dot_nt(jnp.concatenate([kw, ones_l], axis=1), jnp.concatenate([q2, ld2], axis=1))
                dpt = _dot_nt(jnp.concatenate([vw, ones_d], axis=1), jnp.concatenate([do2, ld2], axis=1))
                pt = jnp.where(ok, jnp.exp(st), 0.0)
                dst = (pt * dpt).astype(BF16)
                low = pl.ds(pl.multiple_of(r * pad + b * BAND, BAND), BAND)
                high = pl.ds(pl.multiple_of(r * pad + (b + 1) * BAND, BAND), BAND)
                dkw = _dot(dst, q2)
                dvw = _dot(pt.astype(BF16), do2)
                dkb[low, :] += dkw[:BAND]
                dvb[low, :] += dvw[:BAND]
                dkb[high, :] = dkw[BAND:]
                dvb[high, :] = dvw[BAND:]
                dq2 = _dot_tn(dst, kw)
                dqb[rows, :] = jnp.where(head0, dq2[:BAND], dq2[BAND:])
                return c_

            lax.fori_loop(0, UNITS, unit, 0, unroll=16)

            for r in range(d):
                last = pl.ds(r * pad + per, BAND)
                kept = pl.ds(offs[g] + r * BAND, BAND)

                @pl.when(step > 0)
                def _():
                    dkb[last, :] += ckb[kept, :]
                    dvb[last, :] += cvb[kept, :]

                ckb[kept, :] = dkb[pl.ds(r * pad, BAND), :]
                cvb[kept, :] = dvb[pl.ds(r * pad, BAND), :]
            narrower = DILATIONS[order[pos + 1]] if pos + 1 < nd else None
            for n, (buf, out_ref, stride, at) in enumerate(
                    ((dqb, dq_ref, per, 0), (dkb, dk_ref, pad, BAND), (dvb, dv_ref, pad, BAND))):
                wider, onward = folds[2 * n + pos % 2], folds[2 * n + (pos + 1) % 2]
                for r in range(d):
                    val = buf[pl.ds(r * stride + at, per), :]
                    if pos > 0:
                        val = val + wider[pl.ds(r * per, per), :]
                    if narrower is None:
                        out_ref[...] = val
                    else:
                        start = (r % narrower) * (TOK // narrower) + r // narrower
                        onward[pl.ds(start, per, stride=d // narrower), :] = val

    main, prev = _dilated_specs(nblk, True)
    tok = pl.BlockSpec((TOK, BAND), lambda j, s: (nblk - 1 - s, j))
    acc_rows = max(d * (TOK // d + BAND) for d in DILATIONS)
    kept_rows = sum(DILATIONS) * BAND
    return _call(
        body, list(qs) + list(ks) + list(ks) + list(vs) + list(vs) + [do, lse, dd], name="attn_bwd",
        grid=(PAIRS, nblk), out_shape=[_sds((t, D_ATTN), F32)] * 3,
        in_specs=main + prev + main + prev + main + [tok] * 3, out_specs=[tok] * 3,
        scratch_shapes=[pltpu.VMEM((acc_rows, BAND), BF16)] * 2 + [pltpu.VMEM((TOK, BAND), BF16)] * 2
        + [pltpu.VMEM((TOK, BAND), F32)] * 2 + [pltpu.VMEM((acc_rows, BAND), F32)] * 2
        + [pltpu.VMEM((kept_rows, BAND), F32)] * 2
        + [pltpu.VMEM((TOK, BAND), F32)] * (6 + 2 * (nd - 2)),
        semantics=("parallel", "arbitrary"), carry=carry)


def _halo_rows(tm, t):
    per = tm // 8
    prev = lambda i: (jnp.maximum(i * per - 1, 0), 0)
    nxt = lambda i: (jnp.minimum((i + 1) * per, t // 8 - 1), 0)
    return prev, nxt


def _mixer_out(z, cw, y_attn, g_conv, g_attn, tm, carry=None):
    t = z.shape[0]
    prev, _ = _halo_rows(tm, t)

    def body(z_ref, zp_ref, cw_ref, y_ref, gc_ref, ga_ref, mix_ref):
        i = pl.program_id(0)
        keep = jnp.where(i > 0, 1.0, 0.0)
        cu = jnp.concatenate([zp_ref[:, 0:512] * zp_ref[:, 1024:1536] * keep,
                              z_ref[:, 0:512] * z_ref[:, 1024:1536]], axis=0)
        c = (cw_ref[0:1, :] * pltpu.roll(cu, 2, 0) + cw_ref[1:2, :] * pltpu.roll(cu, 1, 0)
             + cw_ref[2:3, :] * cu)[8:, :]
        yc = z_ref[:, 512:1024] * c
        mix_ref[:, 0:512] = (yc * _rms_scale(yc) * gc_ref[...]).astype(BF16)
        ya = y_ref[...]
        mix_ref[:, 512:1024] = (ya * _rms_scale(ya) * ga_ref[...]).astype(BF16)

    blk = pl.BlockSpec((tm, 512), lambda i: (i, 0))
    vec = pl.BlockSpec((1, 512), lambda i: (0, 0))
    return _call(
        body, [z, z, cw, y_attn, g_conv, g_attn], name="mixer_out", grid=(t // tm,),
        out_shape=_sds((t, 1024), BF16),
        in_specs=[pl.BlockSpec((tm, 1536), lambda i: (i, 0)), pl.BlockSpec((8, 1536), prev),
                  pl.BlockSpec((8, 512), lambda i: (0, 0)), blk, vec, vec],
        out_specs=pl.BlockSpec((tm, 1024), lambda i: (i, 0)),
        semantics=("parallel",), carry=carry)


def _mixer_bwd(z, dx1, wout, y_attn, cw, g_conv, g_attn, ones_bd, tm, carry=None):
    t = z.shape[0]
    nblk = t // tm
    prev, nxt = _halo_rows(tm, t)
    e = tm + 16

    def body(z_ref, zp_ref, zn_ref, dx_ref, dxn_ref, w_ref, y_ref, cw_ref, gc_ref, ga_ref, bd_ref,
             dz_ref, do_ref, dd_ref, dcw_ref, dgc_ref, dga_ref):
        i = pl.program_id(0)
        dm = _dot_nt(dx_ref[...], w_ref[...])
        dmn = _dot_nt(dxn_ref[...], w_ref[0:D_CONV, :])[0:8, :]
        rows = lax.broadcasted_iota(jnp.int32, (e, 1), 0)
        lo = jnp.where(i > 0, 0, 8)
        hi = jnp.where(i < nblk - 1, e, tm + 8)
        ze = jnp.concatenate([zp_ref[...], z_ref[...], zn_ref[...]], axis=0)
        u, gb, gcv = ze[:, 0:512], ze[:, 512:1024], ze[:, 1024:1536]
        w0, w1, w2 = cw_ref[0:1, :], cw_ref[1:2, :], cw_ref[2:3, :]
        cu = jnp.where(rows >= lo, gcv * u, 0.0)
        cu1, cu2 = pltpu.roll(cu, 1, 0), pltpu.roll(cu, 2, 0)
        c = w0 * cu2 + w1 * cu1 + w2 * cu
        yc = gb * c
        dma = jnp.concatenate([jnp.zeros((8, 512), F32), dm[:, 0:512], dmn], axis=0)
        dyc, ych = _rms_bwd(yc, _rms_scale(yc), gc_ref[...], dma)
        dc = jnp.where(jnp.logical_and(rows >= 8, rows < hi), dyc * gb, 0.0)
        dcu = w0 * pltpu.roll(dc, e - 2, 0) + w1 * pltpu.roll(dc, e - 1, 0) + w2 * dc
        mid = slice(8, 8 + tm)
        dz_ref[:, 0:512] = (dcu * gcv)[mid, :].astype(BF16)
        dz_ref[:, 512:1024] = (dyc * c)[mid, :].astype(BF16)
        dz_ref[:, 1024:1536] = (dcu * u)[mid, :].astype(BF16)

        ya = y_ref[...]
        dmb = dm[:, 512:1024]
        dya, yah = _rms_bwd(ya, _rms_scale(ya), ga_ref[...], dmb)
        do_ref[...] = dya
        dd_ref[...] = _head_sum(dya * ya, bd_ref[...])

        @pl.when(i == 0)
        def _():
            dcw_ref[...] = jnp.zeros_like(dcw_ref)
            dgc_ref[...] = jnp.zeros_like(dgc_ref)
            dga_ref[...] = jnp.zeros_like(dga_ref)

        dcm = jnp.where(rows < tm + 8, dc, 0.0)
        dcw_ref[0:1, :] += jnp.sum(dcm * cu2, axis=0, keepdims=True)
        dcw_ref[1:2, :] += jnp.sum(dcm * cu1, axis=0, keepdims=True)
        dcw_ref[2:3, :] += jnp.sum(dcm * cu, axis=0, keepdims=True)
        dgc_ref[...] += jnp.sum((dma * ych)[mid, :], axis=0, keepdims=True)
        dga_ref[...] += jnp.sum(dmb * yah, axis=0, keepdims=True)

    blk = pl.BlockSpec((tm, 512), lambda i: (i, 0))
    vec = pl.BlockSpec((1, 512), lambda i: (0, 0))
    cwb = pl.BlockSpec((8, 512), lambda i: (0, 0))
    next16 = lambda i: (jnp.minimum((i + 1) * (tm // 16), t // 16 - 1), 0)
    return _call(
        body, [z, z, z, dx1, dx1, wout, y_attn, cw, g_conv, g_attn, ones_bd], name="mixer_bwd",
        grid=(nblk,),
        out_shape=[_sds((t, D_IN), BF16), _sds((t, 512), F32), _sds((t, 512), F32),
                   _sds((8, 512), F32), _sds((1, 512), F32), _sds((1, 512), F32)],
        in_specs=[pl.BlockSpec((tm, 1536), lambda i: (i, 0)), pl.BlockSpec((8, 1536), prev),
                  pl.BlockSpec((8, 1536), nxt), pl.BlockSpec((tm, D_MODEL), lambda i: (i, 0)),
                  pl.BlockSpec((16, D_MODEL), next16),
                  pl.BlockSpec(wout.shape, lambda i: (0, 0), pipeline_mode=_resident(True)),
                  blk, cwb, vec, vec, pl.BlockSpec((512, 512), lambda i: (0, 0))],
        out_specs=[pl.BlockSpec((tm, 1536), lambda i: (i, 0)), blk, blk, cwb, vec, vec],
        carry=carry)


def _qkv_bwd(z, dz, dqn, dkn, dv, gq, gk, ones_bd, tm, carry=None):
    t = z.shape[0]

    def body(zq_ref, zk_ref, _, dqn_ref, dkn_ref, dv_ref, gq_ref, gk_ref, bd_ref,
             dz_ref, dgq_ref, dgk_ref):
        bd = bd_ref[...]

        @pl.when(pl.program_id(0) == 0)
        def _():
            dgq_ref[...] = jnp.zeros_like(dgq_ref)
            dgk_ref[...] = jnp.zeros_like(dgk_ref)

        def back(v, dn, g, scale):
            r = _head_rms_scale(v, bd)
            vh = v * r
            dh = dn * (g * scale)
            dv = r * (dh - vh * (_head_sum(dh * vh, bd) * (1.0 / HEAD_DIM)))
            return dv, jnp.sum(dn * scale * vh, axis=0, keepdims=True)

        dq, dgq = back(zq_ref[...], dqn_ref[...], gq_ref[...], HEAD_DIM ** -0.5)
        dk, dgk = back(zk_ref[...], dkn_ref[...], gk_ref[...], 1.0)
        dgq_ref[...] += dgq
        dgk_ref[...] += dgk
        dz_ref[:, 0:512] = dq.astype(BF16)
        dz_ref[:, 512:1024] = dk.astype(BF16)
        dz_ref[:, 1024:1536] = dv_ref[...].astype(BF16)

    blk = pl.BlockSpec((tm, 512), lambda i: (i, 0))
    vec = pl.BlockSpec((1, 512), lambda i: (0, 0))
    return _call(
        body, [z, z, dz, dqn, dkn, dv, gq, gk, ones_bd], name="qkv_bwd", grid=(t // tm,),
        out_shape=[_sds((t, D_IN), BF16), _sds((1, 512), F32), _sds((1, 512), F32)],
        in_specs=[pl.BlockSpec((tm, 512), lambda i: (i, 3)), pl.BlockSpec((tm, 512), lambda i: (i, 4)),
                  ANY] + [blk] * 3 + [vec, vec, pl.BlockSpec((512, 512), lambda i: (0, 0))],
        out_specs=[pl.BlockSpec((tm, 1536), lambda i: (i, 1)), vec, vec],
        carry=carry, aliases={2: 0})


def _columns_from_chips(g):
    return g.transpose(1, 0, 2).reshape(g.shape[1], N_CHIPS * g.shape[2])


def kernel(x, g_mix, w_in, conv_w, g_q, g_k, g_conv_out, g_attn_out, w_out, g_ffn, w_gate, w_up, w_down, loss_target, m_g_mix, m_w_in, m_conv_w, m_g_q, m_g_k, m_g_conv_out, m_g_attn_out, m_w_out, m_g_ffn, m_w_gate, m_w_up, m_w_down, v_g_mix, v_w_in, v_conv_w, v_g_q, v_g_k, v_g_conv_out, v_g_attn_out, v_w_out, v_g_ffn, v_w_gate, v_w_up, v_w_down):
    t = x.shape[1]
    xs = x[0]
    target = loss_target[0]
    tm = min(512, t)
    tmm = min(2048, t)

    cw_pad = jnp.pad(conv_w[0], ((0, 13), (0, 0)))
    gathered = _all_gather([w_in[0].astype(BF16), cw_pad])
    win = _columns_from_chips(gathered[0])
    cw = jnp.pad(gathered[1][:, 0:3, :].transpose(1, 0, 2).reshape(3, D_CONV), ((0, 5), (0, 0)))
    later = [w_out[0].astype(BF16), w_gate[0].T.astype(BF16), w_up[0].T.astype(BF16),
             w_down[0].astype(BF16)]

    head_id = jnp.arange(D_ATTN) // HEAD_DIM
    ones_bd = (head_id[:, None] == head_id[None, :]).astype(BF16)
    gq_t = jnp.tile(g_q, (1, D_ATTN // HEAD_DIM))
    gk_t = jnp.tile(g_k, (1, D_ATTN // HEAD_DIM))

    h1, z, *dilated = _in_proj(xs, g_mix, win, gq_t, gk_t, ones_bd, tm)
    nd = len(DILATIONS)
    qs, ks, vs = dilated[:nd], dilated[nd:2 * nd], dilated[2 * nd:]
    (y_attn, lse), gathered = _attn_fwd(qs, ks, vs, carry=_x_gather_chips(later))
    mix, gathered = _mixer_out(z, cw, y_attn, g_conv_out, g_attn_out, tm,
                               carry=_x_gather_sibling(gathered))
    wout = gathered[0].reshape(D_MODEL, D_MODEL)
    wgate_t = gathered[1].reshape(D_FF, D_MODEL)
    wup_t = gathered[2].reshape(D_FF, D_MODEL)
    wdown = gathered[3].reshape(D_FF, D_MODEL)
    (x1,) = _matmul("out_proj", mix, wout, [xs], [F32], lambda acc, r: (r + acc,), tm, D_MODEL)
    h2, gate, up, act = _norm_matmul("ffn_up", x1, g_ffn, [wgate_t, wup_t], tm, D_FF, True, BF16,
                                     transposed_w=True)

    def loss_epilogue(acc, r, tgt):
        err = r + acc - tgt
        dy = err * (1.0 / D_MODEL)
        return dy, dy, jnp.sum(err * err)

    dx2, dx2b, loss_sum = _matmul("ffn_down_loss", act, wdown, [x1, target], [F32, BF16],
                                  loss_epilogue, tm, D_MODEL, loss=True)

    def swiglu_bwd(da, gt, u):
        gt, u = gt.astype(F32), u.astype(F32)
        s = _sigmoid(gt)
        return da * u * (s * (1.0 + gt * (1.0 - s))), da * (gt * s)

    dgate, dup = _matmul("ffn_down_bwd", dx2b, wdown, [gate, up], [BF16, BF16], swiglu_bwd,
                         tm, D_FF, transposed_w=True)
    gw_down = _matmul_tn("grad_w_down", act, dx2b, 512, tmm)
    gw_gate_t = _matmul_tn("grad_w_gate", dgate, h2, 512, tmm)
    gw_up_t = _matmul_tn("grad_w_up", dup, h2, 512, tmm)

    me = 2 * lax.axis_index("x") + lax.axis_index("y")
    where = jnp.stack([lax.axis_index("c"), me]).astype(jnp.int32)

    def pair_sums(names, full, got):
        return [_pair_sum(f"pair_sum_{nme}", a, b, where) for nme, a, b in zip(names, full, got)]

    def chip_sums(names, pair, got):
        return [_chip_sum(f"chip_sum_{nme}", own, b) for nme, (_, own), b in zip(names, pair, got)]

    ffn = ["w_gate", "w_up", "w_down"]
    full = [g.reshape(N_CHIPS, D_FF // N_CHIPS, D_MODEL) for g in (gw_gate_t, gw_up_t, gw_down)]
    (dx1, dx1b, gg_ffn), got = _matmul_norm_bwd(
        "ffn_up_bwd", [(dgate, wgate_t), (dup, wup_t)], x1, dx2, g_ffn, tm, carry=_x_pair(full),
        transposed_w=False)
    pair = pair_sums(ffn, full, got)
    gw_out = _matmul_tn("grad_w_out", mix, dx1b, 512, tmm)
    full = [gw_out.reshape(N_CHIPS, D_MODEL // N_CHIPS, D_MODEL)]
    (dzc, do, dd, gcw, gg_conv, gg_attn), got = _mixer_bwd(
        z, dx1b, wout, y_attn, cw, g_conv_out, g_attn_out, ones_bd, tm, carry=_x_pair(full))
    pair += pair_sums(["w_out"], full, got)
    early = ffn + ["w_out"]
    (dqn, dkn, dv), got = _attn_bwd(qs, ks, vs, do, lse, dd, carry=_x_chips([p for p, _ in pair]))
    mine = chip_sums(early, pair, got)
    (dz, gg_q, gg_k), theirs = _qkv_bwd(z, dzc, dqn, dkn, dv, gq_t, gk_t, ones_bd, tm,
                                        carry=_x_share(mine))
    full = [_matmul_tn("grad_w_in", h1, dz, D_IN // N_CHIPS, tmm, by_chip=True)]
    got = _exchange_alone("grad_pair_exchange_w_in", _x_pair(full))
    pair = pair_sums(["w_in"], full, got)
    (grad_x, _, gg_mix), got = _matmul_norm_bwd("in_proj_bwd", [(dz, win)], xs, dx1, g_mix, tm,
                                                carry=_x_chips([pair[0][0]]))
    mine += chip_sums(["w_in"], pair, got)
    theirs = list(theirs) + list(_exchange_alone("grad_pair_share_w_in", _x_share(mine[-1:])))
    big = early + ["w_in"]

    small = _small_all_reduce({
        "g_mix": gg_mix, "g_ffn": gg_ffn, "g_conv_out": gg_conv, "g_attn_out": gg_attn,
        "g_q": gg_q, "g_k": gg_k, "loss": loss_sum, "conv_w": gcw})
    heads = D_ATTN // HEAD_DIM
    grads = {
        "g_mix": small[0:1, :], "g_ffn": small[1:2, :],
        "g_conv_out": small[2:3, 0:512], "g_attn_out": small[2:3, 512:1024],
        "g_q": small[3, 0:512].reshape(heads, HEAD_DIM).sum(axis=0)[None, :],
        "g_k": small[3, 512:1024].reshape(heads, HEAD_DIM).sum(axis=0)[None, :],
        "conv_w": lax.dynamic_slice(small[8:11, 0:512], (0, me * (D_CONV // N_CHIPS)),
                                    (3, D_CONV // N_CHIPS)),
    }
    halves = dict(zip(big, zip(mine, theirs)))
    loss = small[4, 0] * 0.5 * (1.0 / D_MODEL)

    weights = dict(g_mix=g_mix, w_in=w_in, conv_w=conv_w, g_q=g_q, g_k=g_k, g_conv_out=g_conv_out,
                   g_attn_out=g_attn_out, w_out=w_out, g_ffn=g_ffn, w_gate=w_gate, w_up=w_up,
                   w_down=w_down)
    moments_m = dict(g_mix=m_g_mix, w_in=m_w_in, conv_w=m_conv_w, g_q=m_g_q, g_k=m_g_k,
                     g_conv_out=m_g_conv_out, g_attn_out=m_g_attn_out, w_out=m_w_out, g_ffn=m_g_ffn,
                     w_gate=m_w_gate, w_up=m_w_up, w_down=m_w_down)
    moments_v = dict(g_mix=v_g_mix, w_in=v_w_in, conv_w=v_conv_w, g_q=v_g_q, g_k=v_g_k,
                     g_conv_out=v_g_conv_out, g_attn_out=v_g_attn_out, w_out=v_w_out, g_ffn=v_g_ffn,
                     w_gate=v_w_gate, w_up=v_w_up, w_down=v_w_down)
    names = list(weights)
    out_g, out_d, out_m, out_v = [], [], [], []
    for nme in names:
        wgt = weights[nme]
        shape2 = wgt.shape[-2:] if wgt.ndim == 3 else wgt.shape
        flip = nme in ("w_gate", "w_up")

        def to2d(a):
            return a.reshape(shape2).T if flip else a.reshape(shape2)

        def back(a):
            return (a.T if flip else a).reshape(wgt.shape)

        state = (to2d(wgt), to2d(moments_m[nme]), to2d(moments_v[nme]))
        if nme in halves:
            g2, dlt, nm, nv = _adamw_shard(f"adamw_{nme}", *state, *halves[nme], where)
        else:
            g2 = grads[nme].reshape(shape2)
            dlt, nm, nv = _adamw(f"adamw_{nme}", state[0], g2, state[1], state[2])
        out_g.append(back(g2))
        out_d.append(back(dlt))
        out_m.append(back(nm))
        out_v.append(back(nv))
    return (loss, grad_x[None], *out_g, *out_d, *out_m, *out_v)
```

```python
import functools
from typing import Any, Callable, NamedTuple, Sequence

import jax
import jax.numpy as jnp
from jax import lax
from jax.experimental import pallas as pl
from jax.experimental.pallas import tpu as pltpu

F32 = jnp.float32
BF16 = jnp.bfloat16
MESH = pl.DeviceIdType.MESH

D_MODEL = 1024
D_CONV = 512
D_ATTN = 512
HEAD_DIM = 64
D_FF = 2816
D_IN = 3 * D_CONV + 3 * D_ATTN
DILATIONS = (1, 4, 16)
BAND = 128
EPS = 1e-6
NEG = -1e30
N_CHIPS = 4

ADAM_LR = 0.001
ADAM_B1 = 0.9
ADAM_B2 = 0.999
ADAM_EPS = 1e-08
ADAM_WD = 0.01
ADAM_STEP = 10

V7X_VMEM_BYTES = 64 * 1024 * 1024
VMEM_LIMIT = V7X_VMEM_BYTES - 8 * 1024 * 1024
ANY = pl.BlockSpec(memory_space=pl.ANY)
VMEM_WHOLE = pl.BlockSpec(memory_space=pltpu.VMEM)


def _params(*sem):
    return pltpu.CompilerParams(dimension_semantics=sem, vmem_limit_bytes=VMEM_LIMIT)


def _sds(shape, dtype):
    return jax.ShapeDtypeStruct(shape, dtype)


def _resident(whole):
    return pl.Buffered(1) if whole else None


def _place():
    x, y, c = lax.axis_index("x"), lax.axis_index("y"), lax.axis_index("c")
    chips = [(1 - x, y), (x, 1 - y), (1 - x, 1 - y)]
    return x, y, c, 2 * x + y, chips, [2 * cx + cy for cx, cy in chips]


class _Exchange(NamedTuple):
    srcs: Sequence[Any]
    lands: Sequence[Any]
    outs: Sequence[Any]
    n_sems: int
    copies: Callable


def _remote(src, dst, ssem, rsem, k, to):
    return pltpu.make_async_remote_copy(src_ref=src, dst_ref=dst, send_sem=ssem.at[k],
                                        recv_sem=rsem.at[k], device_id=to, device_id_type=MESH)


def _x_gather_chips(shards):
    def copies(srcs, lands, outs, ssem, rsem):
        _, _, c, me, chips, cids = _place()
        go, arrive = [], []
        for w, s in enumerate(shards):
            h = s.shape[0] // 2
            rows = pl.ds(pl.multiple_of(c * h, 8), h)
            for j in range(3):
                to = (*chips[j], c)
                go.append(_remote(srcs[w].at[rows], lands[w].at[me, rows], ssem, rsem, 3 * w + j, to))
                arrive.append(_remote(srcs[w].at[rows], lands[w].at[cids[j], rows], ssem, rsem,
                                      3 * w + j, to))
        return go, arrive

    lands = [jnp.broadcast_to(s[None], (N_CHIPS,) + s.shape) for s in shards]
    return _Exchange(shards, lands, [], 3 * len(shards), copies)


def _x_gather_sibling(gathered):
    def copies(srcs, lands, outs, ssem, rsem):
        x, y, c, _, _, cids = _place()
        go, arrive = [], []
        for w, g in enumerate(gathered):
            h = g.shape[1] // 2
            mine = pl.ds(pl.multiple_of(c * h, 8), h)
            theirs = pl.ds(pl.multiple_of((1 - c) * h, 8), h)
            for j in range(3):
                slab = lands[w].at[cids[j]]
                go.append(_remote(slab.at[mine], slab.at[mine], ssem, rsem, 3 * w + j, (x, y, 1 - c)))
                arrive.append(_remote(slab.at[theirs], slab.at[theirs], ssem, rsem, 3 * w + j,
                                      (x, y, 1 - c)))
        return go, arrive

    return _Exchange([], gathered, [], 3 * len(gathered), copies)


def _x_pair(grads):
    def copies(srcs, lands, outs, ssem, rsem):
        x, y, c, _, _, _ = _place()
        go = []
        for w, g in enumerate(grads):
            h = g.shape[1] // 2
            theirs = pl.ds(pl.multiple_of((1 - c) * h, 8), h)
            go.append(_remote(srcs[w].at[:, theirs, :], outs[w], ssem, rsem, w, (x, y, 1 - c)))
        return go, go

    outs = [_sds((N_CHIPS, g.shape[1] // 2, g.shape[2]), g.dtype) for g in grads]
    return _Exchange(grads, [], outs, len(grads), copies)


def _x_chips(parts):
    def copies(srcs, lands, outs, ssem, rsem):
        _, _, c, _, chips, cids = _place()
        go = [_remote(srcs[w].at[cids[j]], outs[w].at[j], ssem, rsem, 3 * w + j, (*chips[j], c))
              for w in range(len(parts)) for j in range(3)]
        return go, go

    outs = [_sds((3,) + p.shape[1:], p.dtype) for p in parts]
    return _Exchange(parts, [], outs, 3 * len(parts), copies)


def _x_share(halves):
    def copies(srcs, lands, outs, ssem, rsem):
        x, y, c, _, _, _ = _place()
        go = [_remote(srcs[w], outs[w], ssem, rsem, w, (x, y, 1 - c)) for w in range(len(halves))]
        return go, go

    return _Exchange(halves, [], [_sds(h.shape, h.dtype) for h in halves], len(halves), copies)


def _call(body, args, *, name, grid, in_specs, out_specs, out_shape, scratch_shapes=(),
          semantics=None, carry=None, aliases=None):
    single = not isinstance(out_shape, (list, tuple))
    out_shape = [out_shape] if single else list(out_shape)
    out_specs = [out_specs] if single else list(out_specs)
    aliases = dict(aliases or {})
    if carry is None:
        res = pl.pallas_call(
            body, name=name, grid=grid, in_specs=list(in_specs), out_specs=out_specs,
            out_shape=out_shape, scratch_shapes=list(scratch_shapes), input_output_aliases=aliases,
            compiler_params=_params(*(semantics or ("arbitrary",) * len(grid))))(*args)
        return res[0] if single else res
    n_in, n_out, n_scr = len(args), len(out_shape), len(scratch_shapes)
    n_src, n_land, n_new = len(carry.srcs), len(carry.lands), len(carry.outs)

    def carrying(*refs):
        at = 0
        parts = []
        for n in (n_in, n_src, n_land, n_out, n_land, n_new, n_scr, 2):
            parts.append(refs[at:at + n])
            at += n
        ins, srcs, _, outs, lands, news, scratch, (ssem, rsem) = parts
        ids = [pl.program_id(a) for a in range(len(grid))]
        first = functools.reduce(jnp.logical_and, [i == 0 for i in ids])
        last = functools.reduce(jnp.logical_and, [i == g - 1 for i, g in zip(ids, grid)])
        go, arrive = carry.copies(srcs, lands, news, ssem, rsem)

        @pl.when(first)
        def _():
            for cp in go:
                cp.start()

        body(*ins, *outs, *scratch)

        @pl.when(last)
        def _():
            for cp in go:
                cp.wait_send()
            for cp in arrive:
                cp.wait_recv()

    res = pl.pallas_call(
        carrying, name=name, grid=grid,
        in_specs=list(in_specs) + [ANY] * (n_src + n_land),
        out_specs=out_specs + [ANY] * (n_land + n_new),
        out_shape=out_shape + [_sds(a.shape, a.dtype) for a in carry.lands] + list(carry.outs),
        input_output_aliases={**aliases, **{n_in + n_src + i: n_out + i for i in range(n_land)}},
        scratch_shapes=list(scratch_shapes) + [pltpu.SemaphoreType.DMA((carry.n_sems,))] * 2,
        compiler_params=_params(*(("arbitrary",) * len(grid))))(*args, *carry.srcs, *carry.lands)
    own = res[:n_out]
    return (own[0] if single else own), res[n_out:]


def _exchange_alone(name, exchange):
    def body(x_ref, o_ref):
        o_ref[...] = x_ref[...]

    blk = pl.BlockSpec((8, 128), lambda i: (0, 0))
    _, res = _call(body, [jnp.zeros((8, 128), F32)], name=name, grid=(1,), in_specs=[blk],
                   out_specs=blk, out_shape=_sds((8, 128), F32), carry=exchange)
    return res


def _row_block(r, want):
    return max(d for d in range(1, min(want, r) + 1) if r % d == 0 and (d % 8 == 0 or d == r))


def _pair_sum(name, full, got, where):
    _, r, n = full.shape
    h = r // 2
    tr = _row_block(h, 256)
    nb = h // tr

    def body(w_ref, a_ref, b_ref, o_ref, own_ref):
        total = a_ref[...] + b_ref[...]
        o_ref[...] = total.astype(BF16)

        @pl.when(pl.program_id(1) == w_ref[1])
        def _():
            own_ref[...] = total[0]

    blk = pl.BlockSpec((1, tr, n), lambda i, s, w: (s, i, 0))
    return pl.pallas_call(
        body, name=name, out_shape=[_sds(got.shape, BF16), _sds((h, n), F32)],
        grid_spec=pltpu.PrefetchScalarGridSpec(
            num_scalar_prefetch=1, grid=(nb, N_CHIPS),
            in_specs=[pl.BlockSpec((1, tr, n), lambda i, s, w: (s, w[0] * nb + i, 0)), blk],
            out_specs=[blk, pl.BlockSpec((tr, n), lambda i, s, w: (i, 0))]),
        compiler_params=_params("parallel", "arbitrary"),
    )(where, full, got)


def _chip_sum(name, own, got):
    h, n = own.shape
    tr = _row_block(h, 256)

    def body(a_ref, b0, b1, b2, o_ref):
        o_ref[...] = ((a_ref[...] + b0[0].astype(F32)) + b1[0].astype(F32)) + b2[0].astype(F32)

    def slot(j):
        return pl.BlockSpec((1, tr, n), lambda i: (j, i, 0))

    blk = pl.BlockSpec((tr, n), lambda i: (i, 0))
    return pl.pallas_call(
        body, name=name, grid=(h // tr,), out_shape=_sds((h, n), F32),
        in_specs=[blk, slot(0), slot(1), slot(2)], out_specs=blk,
        compiler_params=_params("parallel"),
    )(own, got, got, got)


SMALL_ROWS = 16
SMALL_LAYOUT = (
    ("g_mix", 0, 0, 1, 1024), ("g_ffn", 1, 0, 1, 1024), ("g_conv_out", 2, 0, 1, 512),
    ("g_attn_out", 2, 512, 1, 512), ("g_q", 3, 0, 1, 512), ("g_k", 3, 512, 1, 512),
    ("loss", 4, 0, 1, 128), ("conv_w", 8, 0, 8, 512))


def _small_all_reduce(parts):
    names = [s[0] for s in SMALL_LAYOUT]

    def body(*refs):
        ins = refs[:len(names)]
        out_ref, stage, buf, ssem, rsem = refs[len(names):]
        x, y, c, _, _, _ = _place()
        me = 4 * x + 2 * y + c
        stage[...] = jnp.zeros_like(stage)
        for ref, (_, r0, c0, nr, nc) in zip(ins, SMALL_LAYOUT):
            stage[r0:r0 + nr, c0:c0 + nc] = ref[0:nr, :]
        buf[me] = stage[...]
        peers = []
        for d in range(1, 8):
            px = 1 - x if d & 4 else x
            py = 1 - y if d & 2 else y
            pc = 1 - c if d & 1 else c
            peers.append(((px, py, pc), 4 * px + 2 * py + pc))
        sends = [pltpu.make_async_remote_copy(
            src_ref=stage, dst_ref=buf.at[me], send_sem=ssem.at[k], recv_sem=rsem.at[k],
            device_id=peer, device_id_type=MESH) for k, (peer, _) in enumerate(peers)]
        for cp in sends:
            cp.start()
        for k, (peer, pid) in enumerate(peers):
            pltpu.make_async_remote_copy(
                src_ref=stage, dst_ref=buf.at[pid], send_sem=ssem.at[k], recv_sem=rsem.at[k],
                device_id=peer, device_id_type=MESH).wait_recv()
        for cp in sends:
            cp.wait_send()
        acc = buf[0]
        for k in range(1, 8):
            acc = acc + buf[k]
        out_ref[...] = acc

    return pl.pallas_call(
        body, name="small_all_reduce", out_shape=_sds((SMALL_ROWS, 1024), F32),
        in_specs=[VMEM_WHOLE] * len(names), out_specs=VMEM_WHOLE,
        scratch_shapes=[pltpu.VMEM((SMALL_ROWS, 1024), F32), pltpu.VMEM((8, SMALL_ROWS, 1024), F32),
                        pltpu.SemaphoreType.DMA((7,)), pltpu.SemaphoreType.DMA((7,))],
    )(*[parts[k] for k in names])


def _dot(a, b):
    return jnp.dot(a, b, preferred_element_type=F32)


def _dot_nt(a, b):
    return lax.dot_general(a, b, (((1,), (1,)), ((), ())), preferred_element_type=F32)


def _dot_tn(a, b):
    return lax.dot_general(a, b, (((0,), (0,)), ((), ())), preferred_element_type=F32)


def _sigmoid(v):
    return 1.0 / (1.0 + jnp.exp(-v))


def _rms_scale(v):
    return lax.rsqrt(jnp.mean(v * v, axis=-1, keepdims=True) + EPS)


def _rms_bwd(v, r, g, dy):
    vh = v * r
    dh = dy * g
    return r * (dh - vh * jnp.mean(dh * vh, axis=-1, keepdims=True)), vh


def _head_sum(a, ones_bd):
    hi = a.astype(BF16)
    lo = (a - hi.astype(F32)).astype(BF16)
    return _dot(hi, ones_bd) + _dot(lo, ones_bd)


def _head_rms_scale(v, ones_bd):
    return lax.rsqrt(_head_sum(v * v, ones_bd) * (1.0 / HEAD_DIM) + EPS)


MXU_COLUMNS = 256


def _column_chunks(n):
    width = MXU_COLUMNS if n % MXU_COLUMNS == 0 else n
    return [slice(c, c + width) for c in range(0, n, width)]


def _norm_matmul(name, x, g, ws, tm, tn, swiglu, out_dtype=F32, transposed_w=False):
    t, d = x.shape
    n = ws[0].shape[0] if transposed_w else ws[0].shape[1]
    nw = len(ws)

    def body(x_ref, g_ref, *refs):
        w_refs, h_ref, o_refs = refs[:nw], refs[nw], refs[nw + 1:2 * nw + 1]
        hs = refs[-1]

        @pl.when(pl.program_id(1) == 0)
        def _():
            xv = x_ref[...]
            h = (xv * _rms_scale(xv) * g_ref[...]).astype(BF16)
            hs[...] = h
            h_ref[...] = h

        h = hs[...]
        for cols in _column_chunks(tn):
            outs = [_dot_nt(h, w[cols, :]) if transposed_w else _dot(h, w[:, cols]) for w in w_refs]
            for o_ref, o in zip(o_refs, outs):
                o_ref[:, cols] = o.astype(out_dtype)
            if swiglu:
                refs[2 * nw + 1][:, cols] = (outs[0] * _sigmoid(outs[0]) * outs[1]).astype(BF16)

    row = pl.BlockSpec((tm, d), lambda i, j: (i, 0))
    col = pl.BlockSpec((tm, tn), lambda i, j: (i, j))
    out_shape = [_sds((t, d), BF16)] + [_sds((t, n), out_dtype)] * nw
    out_specs = [row] + [col] * nw
    if swiglu:
        out_shape.append(_sds((t, n), BF16))
        out_specs.append(col)
    return pl.pallas_call(
        body, name=name, grid=(t // tm, n // tn), out_shape=out_shape,
        in_specs=[row, pl.BlockSpec((1, d), lambda i, j: (0, 0))]
        + [pl.BlockSpec((tn, d), lambda i, j: (j, 0), pipeline_mode=_resident(tn == n))
           if transposed_w
           else pl.BlockSpec((d, tn), lambda i, j: (0, j), pipeline_mode=_resident(tn == n))] * nw,
        out_specs=out_specs, scratch_shapes=[pltpu.VMEM((tm, d), BF16)],
        compiler_params=_params("parallel", "arbitrary"),
    )(x, g, *ws)


def _matmul(name, a, w, extras, out_dtypes, epilogue, tm, tn, transposed_w=False, loss=False):
    t, k = a.shape
    n = w.shape[0] if transposed_w else w.shape[1]
    ne, no = len(extras), len(out_dtypes)

    def body(a_ref, w_ref, *refs):
        e_refs, o_refs = refs[:ne], refs[ne:]
        a = a_ref[...]
        total = 0.0
        for cols in _column_chunks(tn):
            acc = _dot_nt(a, w_ref[cols, :]) if transposed_w else _dot(a, w_ref[:, cols])
            res = epilogue(acc, *[e[:, cols] for e in e_refs])
            for o_ref, r in zip(o_refs[:no], res[:no]):
                o_ref[:, cols] = r.astype(o_ref.dtype)
            if loss:
                total = total + res[no]
        if loss:
            first = jnp.logical_and(pl.program_id(0) == 0, pl.program_id(1) == 0)

            @pl.when(first)
            def _():
                o_refs[no][...] = jnp.zeros_like(o_refs[no])

            o_refs[no][...] += total

    col = pl.BlockSpec((tm, tn), lambda i, j: (i, j))
    w_spec = (pl.BlockSpec((tn, k), lambda i, j: (j, 0), pipeline_mode=_resident(tn == n))
              if transposed_w
              else pl.BlockSpec((k, tn), lambda i, j: (0, j), pipeline_mode=_resident(tn == n)))
    out_shape = [_sds((t, n), dt) for dt in out_dtypes]
    out_specs = [col] * no
    if loss:
        out_shape.append(_sds((8, 128), F32))
        out_specs.append(pl.BlockSpec((8, 128), lambda i, j: (0, 0)))
    return pl.pallas_call(
        body, name=name, grid=(t // tm, n // tn), out_shape=out_shape,
        in_specs=[pl.BlockSpec((tm, k), lambda i, j: (i, 0)), w_spec] + [col] * ne,
        out_specs=out_specs,
        compiler_params=_params(*(("arbitrary", "arbitrary") if loss else ("parallel", "parallel"))),
    )(a, w, *extras)


def _matmul_norm_bwd(name, pairs, x, dres, g, tm, carry=None, transposed_w=True):
    t, d = x.shape
    npairs = len(pairs)
    product = _dot_nt if transposed_w else _dot

    def body(*refs):
        a_refs, w_refs = refs[:npairs], refs[npairs:2 * npairs]
        x_ref, r_ref, g_ref, dx_ref, dxb_ref, dg_ref = refs[2 * npairs:]
        dy = product(a_refs[0][...], w_refs[0][...])
        for a_ref, w_ref in zip(a_refs[1:], w_refs[1:]):
            dy = dy + product(a_ref[...], w_ref[...])
        xv = x_ref[...]
        dx, xh = _rms_bwd(xv, _rms_scale(xv), g_ref[...], dy)
        dx = dx + r_ref[...]
        dx_ref[...] = dx
        dxb_ref[...] = dx.astype(BF16)

        @pl.when(pl.program_id(0) == 0)
        def _():
            dg_ref[...] = jnp.zeros_like(dg_ref)

        dg_ref[...] += jnp.sum(dy * xh, axis=0, keepdims=True)

    row = pl.BlockSpec((tm, d), lambda i: (i, 0))
    vec = pl.BlockSpec((1, d), lambda i: (0, 0))
    return _call(
        body, [a for a, _ in pairs] + [w for _, w in pairs] + [x, dres, g], name=name,
        grid=(t // tm,), out_shape=[_sds((t, d), F32), _sds((t, d), BF16), _sds((1, d), F32)],
        in_specs=[pl.BlockSpec((tm, a.shape[1]), lambda i: (i, 0)) for a, _ in pairs]
        + [pl.BlockSpec(w.shape, lambda i: (0, 0), pipeline_mode=pl.Buffered(1)) for _, w in pairs]
        + [row, row, vec],
        out_specs=[row, row, vec], carry=carry)


def _matmul_tn(name, a, g, tn, tk, by_chip=False):
    t, ka = a.shape
    n = g.shape[1]

    def body(a_ref, g_ref, o_ref):
        @pl.when(pl.program_id(1) == 0)
        def _():
            o_ref[...] = jnp.zeros_like(o_ref)

        acc = _dot_tn(a_ref[...], g_ref[...])
        o_ref[...] += acc[None] if by_chip else acc

    return pl.pallas_call(
        body, name=name, grid=(n // tn, t // tk),
        out_shape=_sds((n // tn, ka, tn) if by_chip else (ka, n), F32),
        in_specs=[pl.BlockSpec((tk, ka), lambda j, s: (s, 0)),
                  pl.BlockSpec((tk, tn), lambda j, s: (s, j))],
        out_specs=(pl.BlockSpec((1, ka, tn), lambda j, s: (j, 0, 0)) if by_chip
                   else pl.BlockSpec((ka, tn), lambda j, s: (0, j))),
        compiler_params=_params("parallel", "arbitrary"),
    )(a, g)


def _elementwise(name, fn, ins, out_dtypes, tr):
    r, n = ins[0].shape
    tr = _row_block(r, tr)
    ni = len(ins)

    def body(*refs):
        res = fn(*[ref[...] for ref in refs[:ni]])
        for o_ref, v in zip(refs[ni:], res):
            o_ref[...] = v.astype(o_ref.dtype)

    blk = pl.BlockSpec((tr, n), lambda i: (i, 0))
    return pl.pallas_call(
        body, name=name, grid=(r // tr,), out_shape=[_sds((r, n), dt) for dt in out_dtypes],
        in_specs=[blk] * ni, out_specs=[blk] * len(out_dtypes),
        compiler_params=_params("parallel"),
    )(*ins)


def _adamw_update(w, g, m, v):
    m = ADAM_B1 * m + (1.0 - ADAM_B1) * g
    v = ADAM_B2 * v + (1.0 - ADAM_B2) * (g * g)
    m_hat = m / (1.0 - ADAM_B1 ** ADAM_STEP)
    v_hat = v / (1.0 - ADAM_B2 ** ADAM_STEP)
    return -ADAM_LR * (m_hat / (jnp.sqrt(v_hat) + ADAM_EPS) + ADAM_WD * w), m, v


def _adamw(name, w, g, m, v):
    return _elementwise(name, _adamw_update, [w, g, m, v], [F32] * 3, 256)


def _adamw_shard(name, w, m, v, mine, theirs, where):
    r, n = w.shape
    h = r // 2
    tr = _row_block(h, 256)
    nb = h // tr

    def body(w_ref, p_ref, m_ref, v_ref, a_ref, b_ref, g_ref, d_ref, nm_ref, nv_ref):
        g = jnp.where(pl.program_id(0) == w_ref[0], a_ref[...], b_ref[...])
        g_ref[...] = g
        d_ref[...], nm_ref[...], nv_ref[...] = _adamw_update(p_ref[...], g, m_ref[...], v_ref[...])

    whole = pl.BlockSpec((tr, n), lambda s, i, c: (s * nb + i, 0))
    used = pl.BlockSpec((tr, n), lambda s, i, c: (jnp.where(s == c[0], i, 0), 0))
    unused = pl.BlockSpec((tr, n), lambda s, i, c: (jnp.where(s == c[0], 0, i), 0))
    return pl.pallas_call(
        body, name=name, out_shape=[_sds((r, n), F32)] * 4,
        grid_spec=pltpu.PrefetchScalarGridSpec(
            num_scalar_prefetch=1, grid=(2, nb), in_specs=[whole] * 3 + [used, unused],
            out_specs=[whole] * 4),
        compiler_params=_params("arbitrary", "arbitrary"),
    )(where, w, m, v, mine, theirs)


PAIRS = D_ATTN // BAND


def _rms_norm(x, g, tm, carry=None):
    t, dm = x.shape

    def body(x_ref, g_ref, h_ref):
        xv = x_ref[...]
        h_ref[...] = (xv * _rms_scale(xv) * g_ref[...]).astype(BF16)

    row = pl.BlockSpec((tm, dm), lambda i: (i, 0))
    return _call(body, [x, g], name="rms_norm", grid=(t // tm,), out_shape=_sds((t, dm), BF16),
                 in_specs=[row, pl.BlockSpec((1, dm), lambda i: (0, 0))], out_specs=row,
                 semantics=("parallel",), carry=carry)


def _in_proj(h, w, gq, gk, ones_bd, tm):
    t, dm = h.shape
    n = w.shape[1]
    nd = len(DILATIONS)
    first = 3 * D_CONV

    def body(h_ref, w_ref, gq_ref, gk_ref, bd_ref, z_ref, *refs):
        outs, slabs = refs[:3 * nd], refs[3 * nd:]
        h = h_ref[...]
        for cols in _column_chunks(n):
            z_ref[:, cols] = _dot(h, w_ref[:, cols])
        bd = bd_ref[...]
        q = z_ref[:, first:first + D_ATTN]
        k = z_ref[:, first + D_ATTN:first + 2 * D_ATTN]
        vals = [(q * _head_rms_scale(q, bd) * gq_ref[...]) * HEAD_DIM ** -0.5,
                k * _head_rms_scale(k, bd) * gk_ref[...], z_ref[:, first + 2 * D_ATTN:n]]
        for m, val in enumerate(vals):
            for c in range(PAIRS):
                slabs[0][c] = val[:, c * BAND:(c + 1) * BAND]
            cur, before = 0, 1
            for a, d in enumerate(DILATIONS):
                o_ref, src, dst = outs[m * nd + a], slabs[cur], slabs[1 - cur]
                step, count = d // before, tm // d
                keep = step > 1 and a + 1 < nd
                for c in range(PAIRS):
                    for r in range(d):
                        start = (r % before) * (tm // before) + r // before
                        rows = src.at[c][pl.ds(start, count, stride=step), :] if step > 1 else src[c]
                        o_ref[c, r] = rows.astype(BF16)
                        if keep:
                            dst.at[c][pl.ds(r * count, count), :] = rows
                if keep:
                    cur = 1 - cur
                before = d

    row = pl.BlockSpec((tm, dm), lambda i: (i, 0))
    vec = pl.BlockSpec((1, D_ATTN), lambda i: (0, 0))
    return pl.pallas_call(
        body, name="in_proj", grid=(t // tm,),
        out_shape=[_sds((t, n), F32)]
        + [_sds((PAIRS, d, t // d, BAND), BF16) for _ in range(3) for d in DILATIONS],
        in_specs=[row, pl.BlockSpec((dm, n), lambda i: (0, 0), pipeline_mode=_resident(True)), vec, vec,
                  pl.BlockSpec((D_ATTN, D_ATTN), lambda i: (0, 0), pipeline_mode=_resident(True))],
        out_specs=[pl.BlockSpec((tm, n), lambda i: (i, 0))]
        + [pl.BlockSpec((PAIRS, d, tm // d, BAND), lambda i: (0, 0, i, 0))
           for _ in range(3) for d in DILATIONS],
        scratch_shapes=[pltpu.VMEM((PAIRS, tm, BAND), F32)] * 2,
        compiler_params=_params("parallel"),
    )(h, w, gq, gk, ones_bd)


TOK = 2048
UNITS = TOK // BAND


def _stack_masks():
    row = lax.broadcasted_iota(jnp.int32, (2 * BAND, 2 * BAND), 0) & (BAND - 1)
    col = lax.broadcasted_iota(jnp.int32, (2 * BAND, 2 * BAND), 1)
    lane = lax.broadcasted_iota(jnp.int32, (BAND, BAND), 1)
    head0 = lane < HEAD_DIM
    ones = [jnp.where(head0, 1.0, 0.0).astype(BF16), jnp.where(head0, 0.0, 1.0).astype(BF16)]
    return col - row, col, head0, ones


def _split3(x):
    hi = x.astype(BF16).astype(F32)
    mid = (x - hi).astype(BF16).astype(F32)
    return hi, mid, x - hi - mid


def _gather(srcs, dst, d, before=1):
    per, step, span = TOK // d, d // before, TOK // before
    at = 0
    for r in range(d):
        start = (r % before) * span + r // before
        for src in srcs:
            rows = src[pl.ds(start, per, stride=step), :] if step > 1 else src[pl.ds(start, per), :]
            dst[pl.ds(at, per), :] = rows.astype(dst.dtype)
            at += per


def _scatter(out_ref, src, d):
    per = TOK // d
    if d == 1:
        out_ref[...] = src[...]
        return
    for r in range(d):
        out_ref[pl.ds(r, per, stride=d), :] = src[pl.ds(r * per, per), :]


def _dilated_specs(nblk, reverse):
    def at(s):
        return (nblk - 1 - s) if reverse else s
    main = [pl.BlockSpec((1, d, TOK // d, BAND), lambda j, s: (j, 0, at(s), 0)) for d in DILATIONS]
    prev = [pl.BlockSpec((1, d, TOK // d, BAND), lambda j, s: (j, 0, jnp.maximum(at(s) - 1, 0), 0))
            for d in DILATIONS]
    return main, prev


def _window_rows(prev_ref, main_ref, dst, d):
    per = TOK // d
    for r in range(d):
        dst[pl.ds(r * (per + BAND), BAND), :] = prev_ref[0, r, pl.ds(per - BAND, BAND), :]
        dst[pl.ds(r * (per + BAND) + BAND, per), :] = main_ref[0, r]


def _attn_fwd(qs, ks, vs, carry=None):
    t = qs[0].shape[2]
    nblk = t // TOK
    nd = len(DILATIONS)

    def body(*refs):
        q_refs, kp_refs, k_refs = refs[:nd], refs[nd:2 * nd], refs[2 * nd:3 * nd]
        vp_refs, v_refs = refs[3 * nd:4 * nd], refs[4 * nd:5 * nd]
        y_ref, l_ref, kw_s, vw_s, ob, lb, on, ln = refs[5 * nd:]
        i = pl.program_id(1)
        diff, col, head0, hm = _stack_masks()
        band_ok = jnp.logical_and(diff >= 0, diff <= BAND)
        for g, d in enumerate(DILATIONS):
            per = TOK // d
            nb = per // BAND
            pad = per + BAND
            _window_rows(kp_refs[g], k_refs[g], kw_s, d)
            _window_rows(vp_refs[g], v_refs[g], vw_s, d)
            q_ref = q_refs[g]

            def unit(u, carry):
                r, b = u // nb, u % nb
                qu = q_ref[0, r, pl.ds(pl.multiple_of(b * BAND, BAND), BAND), :]
                start = pl.multiple_of(r * pad + b * BAND, BAND)
                kw = kw_s[pl.ds(start, 2 * BAND), :]
                vw = vw_s[pl.ds(start, 2 * BAND), :]
                lo = jnp.where(jnp.logical_and(i == 0, b == 0), BAND, 0)
                s = _dot_nt(jnp.concatenate([qu * hm[0], qu * hm[1]], axis=0), kw)
                s = jnp.where(jnp.logical_and(band_ok, col >= lo), s, NEG)
                mx = jnp.max(s, axis=-1, keepdims=True)
                e = jnp.exp(s - mx)
                den = jnp.sum(e, axis=-1, keepdims=True)
                o2 = _dot(e.astype(BF16), vw) / den
                l2 = jnp.broadcast_to(mx + jnp.log(den), (2 * BAND, BAND))
                rows = pl.ds(pl.multiple_of(u * BAND, BAND), BAND)
                ob[rows, :] = jnp.where(head0, o2[:BAND], o2[BAND:])
                lb[rows, :] = jnp.where(head0, l2[:BAND], l2[BAND:])
                return carry

            lax.fori_loop(0, UNITS, unit, 0, unroll=16)
            _scatter(on.at[g], ob, d)
            _scatter(ln.at[g], lb, d)
        ls = [ln[0], ln[1], ln[2]]
        mx = jnp.maximum(jnp.maximum(ls[0], ls[1]), ls[2])
        es = [jnp.exp(l - mx) for l in ls]
        tot = es[0] + es[1] + es[2]
        y_ref[...] = (es[0] * on[0] + es[1] * on[1] + es[2] * on[2]) / tot
        l_ref[...] = mx + jnp.log(tot)

    main, prev = _dilated_specs(nblk, False)
    out = pl.BlockSpec((TOK, BAND), lambda j, i: (i, j))
    win_rows = max(d * (TOK // d + BAND) for d in DILATIONS)
    return _call(
        body, list(qs) + list(ks) + list(ks) + list(vs) + list(vs), name="attn_fwd",
        grid=(PAIRS, nblk), out_shape=[_sds((t, D_ATTN), F32)] * 2,
        in_specs=main + prev + main + prev + main, out_specs=[out, out],
        scratch_shapes=[pltpu.VMEM((win_rows, BAND), BF16)] * 2 + [pltpu.VMEM((TOK, BAND), F32)] * 2
        + [pltpu.VMEM((nd, TOK, BAND), F32)] * 2,
        semantics=("parallel", "parallel"), carry=carry)


def _attn_bwd(qs, ks, vs, do, lse, dd, carry=None):
    t = qs[0].shape[2]
    nblk = t // TOK
    nd = len(DILATIONS)
    offs = [sum(DILATIONS[:g]) * BAND for g in range(nd)]

    def body(*refs):
        q_refs, kp_refs, k_refs = refs[:nd], refs[nd:2 * nd], refs[2 * nd:3 * nd]
        vp_refs, v_refs = refs[3 * nd:4 * nd], refs[4 * nd:5 * nd]
        (do_ref, l_ref, d_ref, dq_ref, dk_ref, dv_ref, kw_s, vw_s, dos, lds, pn, dqb, dkb, dvb, ckb,
         cvb, *more) = refs[5 * nd:]
        folds, mids = more[:6], more[6:]
        step = pl.program_id(1)
        i = nblk - 1 - step
        key = lax.broadcasted_iota(jnp.int32, (2 * BAND, 2 * BAND), 0)
        qry = lax.broadcasted_iota(jnp.int32, (2 * BAND, 2 * BAND), 1) & (BAND - 1)
        off = key - qry
        band_ok = jnp.logical_and(off >= 0, off <= BAND)
        lane = lax.broadcasted_iota(jnp.int32, (BAND, BAND), 1)
        head0 = lane < HEAD_DIM
        hm = [jnp.where(head0, 1.0, 0.0).astype(BF16), jnp.where(head0, 0.0, 1.0).astype(BF16)]
        lane2 = lax.broadcasted_iota(jnp.int32, (2 * BAND, BAND), 1) & (HEAD_DIM - 1)
        ones_l = jnp.where(lane2 < 3, 1.0, 0.0).astype(BF16)
        ones_d = jnp.where(jnp.logical_and(lane2 >= 3, lane2 < 6), 1.0, 0.0).astype(BF16)
        piece = lax.broadcasted_iota(jnp.int32, (TOK, BAND), 1) & (HEAD_DIM - 1)

        def pieces(x, at):
            hi, mid, lo = _split3(-x)
            return jnp.where(piece == at, hi,
                             jnp.where(piece == at + 1, mid, jnp.where(piece == at + 2, lo, 0.0)))

        pn[...] = pieces(l_ref[...], 0) + pieces(d_ref[...], 3)
        order = sorted(range(nd), key=lambda a: -DILATIONS[a])
        assert DILATIONS[order[-1]] == 1
        levels = {1: (do_ref, pn)}
        for n, a in enumerate(reversed(order[1:-1])):
            d, before = DILATIONS[a], DILATIONS[order[-1 - n]]
            levels[d] = (mids[2 * n], mids[2 * n + 1])
            for src, dst in zip(levels[before], levels[d]):
                _gather([src], dst, d, before)
        for pos, g in enumerate(order):
            d = DILATIONS[g]
            per = TOK // d
            nb = per // BAND
            pad = per + BAND
            _window_rows(kp_refs[g], k_refs[g], kw_s, d)
            _window_rows(vp_refs[g], v_refs[g], vw_s, d)
            known = d if d in levels else DILATIONS[order[pos + 1]]
            _gather([levels[known][0]], dos, d, known)
            _gather([levels[known][1]], lds, d, known)
            for r in range(d):
                spare = pl.ds(r * pad, BAND)
                dkb[spare, :] = jnp.zeros((BAND, BAND), F32)
                dvb[spare, :] = jnp.zeros((BAND, BAND), F32)
            q_ref = q_refs[g]

            def unit(u, c_):
                r, b = u // nb, u % nb
                rows = pl.ds(pl.multiple_of(u * BAND, BAND), BAND)
                qu = q_ref[0, r, pl.ds(pl.multiple_of(b * BAND, BAND), BAND), :]
                dou, ldu = dos[rows, :], lds[rows, :]
                q2 = jnp.concatenate([qu * hm[0], qu * hm[1]], axis=0)
                do2 = jnp.concatenate([dou * hm[0], dou * hm[1]], axis=0)
                ld2 = jnp.concatenate([ldu * hm[0], ldu * hm[1]], axis=0)
                acc = pl.ds(pl.multiple_of(r * pad + b * BAND, BAND), 2 * BAND)
                kw = kw_s[acc, :]
                vw = vw_s[acc, :]
                lo = jnp.where(jnp.logical_and(i == 0, b == 0), BAND, 0)
                ok = jnp.logical_and(band_ok, key >= lo)
                st = _dot_nt(jnp.concatenate([kw, ones_l], axis=1), jnp.concatenate([q2, ld2], axis=1))
                dpt = _dot_nt(jnp.concatenate([vw, ones_d], axis=1), jnp.concatenate([do2, ld2], axis=1))
                pt = jnp.where(ok, jnp.exp(st), 0.0)
                dst = (pt * dpt).astype(BF16)
                low = pl.ds(pl.multiple_of(r * pad + b * BAND, BAND), BAND)
                high = pl.ds(pl.multiple_of(r * pad + (b + 1) * BAND, BAND), BAND)
                dkw = _dot(dst, q2)
                dvw = _dot(pt.astype(BF16), do2)
                dkb[low, :] += dkw[:BAND]
                dvb[low, :] += dvw[:BAND]
                dkb[high, :] = dkw[BAND:]
                dvb[high, :] = dvw[BAND:]
                dq2 = _dot_tn(dst, kw)
                dqb[rows, :] = jnp.where(head0, dq2[:BAND], dq2[BAND:])
                return c_

            lax.fori_loop(0, UNITS, unit, 0, unroll=16)

            for r in range(d):
                last = pl.ds(r * pad + per, BAND)
                kept = pl.ds(offs[g] + r * BAND, BAND)

                @pl.when(step > 0)
                def _():
                    dkb[last, :] += ckb[kept, :]
                    dvb[last, :] += cvb[kept, :]

                ckb[kept, :] = dkb[pl.ds(r * pad, BAND), :]
                cvb[kept, :] = dvb[pl.ds(r * pad, BAND), :]
            narrower = DILATIONS[order[pos + 1]] if pos + 1 < nd else None
            for n, (buf, out_ref, stride, at) in enumerate(
                    ((dqb, dq_ref, per, 0), (dkb, dk_ref, pad, BAND), (dvb, dv_ref, pad, BAND))):
                wider, onward = folds[2 * n + pos % 2], folds[2 * n + (pos + 1) % 2]
                for r in range(d):
                    val = buf[pl.ds(r * stride + at, per), :]
                    if pos > 0:
                        val = val + wider[pl.ds(r * per, per), :]
                    if narrower is None:
                        out_ref[...] = val
                    else:
                        start = (r % narrower) * (TOK // narrower) + r // narrower
                        onward[pl.ds(start, per, stride=d // narrower), :] = val

    main, prev = _dilated_specs(nblk, True)
    tok = pl.BlockSpec((TOK, BAND), lambda j, s: (nblk - 1 - s, j))
    acc_rows = max(d * (TOK // d + BAND) for d in DILATIONS)
    kept_rows = sum(DILATIONS) * BAND
    return _call(
        body, list(qs) + list(ks) + list(ks) + list(vs) + list(vs) + [do, lse, dd], name="attn_bwd",
        grid=(PAIRS, nblk), out_shape=[_sds((t, D_ATTN), F32)] * 3,
        in_specs=main + prev + main + prev + main + [tok] * 3, out_specs=[tok] * 3,
        scratch_shapes=[pltpu.VMEM((acc_rows, BAND), BF16)] * 2 + [pltpu.VMEM((TOK, BAND), BF16)] * 2
        + [pltpu.VMEM((TOK, BAND), F32)] * 2 + [pltpu.VMEM((acc_rows, BAND), F32)] * 2
        + [pltpu.VMEM((kept_rows, BAND), F32)] * 2
        + [pltpu.VMEM((TOK, BAND), F32)] * (6 + 2 * (nd - 2)),
        semantics=("parallel", "arbitrary"), carry=carry)


def _halo_rows(tm, t):
    per = tm // 8
    prev = lambda i: (jnp.maximum(i * per - 1, 0), 0)
    nxt = lambda i: (jnp.minimum((i + 1) * per, t // 8 - 1), 0)
    return prev, nxt


def _mixer_out(z, cw, y_attn, g_conv, g_attn, tm, carry=None):
    t = z.shape[0]
    prev, _ = _halo_rows(tm, t)

    def body(z_ref, zp_ref, cw_ref, y_ref, gc_ref, ga_ref, mix_ref):
        i = pl.program_id(0)
        keep = jnp.where(i > 0, 1.0, 0.0)
        cu = jnp.concatenate([zp_ref[:, 0:512] * zp_ref[:, 1024:1536] * keep,
                              z_ref[:, 0:512] * z_ref[:, 1024:1536]], axis=0)
        c = (cw_ref[0:1, :] * pltpu.roll(cu, 2, 0) + cw_ref[1:2, :] * pltpu.roll(cu, 1, 0)
             + cw_ref[2:3, :] * cu)[8:, :]
        yc = z_ref[:, 512:1024] * c
        mix_ref[:, 0:512] = (yc * _rms_scale(yc) * gc_ref[...]).astype(BF16)
        ya = y_ref[...]
        mix_ref[:, 512:1024] = (ya * _rms_scale(ya) * ga_ref[...]).astype(BF16)

    blk = pl.BlockSpec((tm, 512), lambda i: (i, 0))
    vec = pl.BlockSpec((1, 512), lambda i: (0, 0))
    return _call(
        body, [z, z, cw, y_attn, g_conv, g_attn], name="mixer_out", grid=(t // tm,),
        out_shape=_sds((t, 1024), BF16),
        in_specs=[pl.BlockSpec((tm, 1536), lambda i: (i, 0)), pl.BlockSpec((8, 1536), prev),
                  pl.BlockSpec((8, 512), lambda i: (0, 0)), blk, vec, vec],
        out_specs=pl.BlockSpec((tm, 1024), lambda i: (i, 0)),
        semantics=("parallel",), carry=carry)


def _mixer_bwd(z, dx1, wout, y_attn, cw, g_conv, g_attn, ones_bd, tm, carry=None):
    t = z.shape[0]
    nblk = t // tm
    prev, nxt = _halo_rows(tm, t)
    e = tm + 16

    def body(z_ref, zp_ref, zn_ref, dx_ref, dxn_ref, w_ref, y_ref, cw_ref, gc_ref, ga_ref, bd_ref,
             dz_ref, do_ref, dd_ref, dcw_ref, dgc_ref, dga_ref):
        i = pl.program_id(0)
        dm = _dot_nt(dx_ref[...], w_ref[...])
        dmn = _dot_nt(dxn_ref[...], w_ref[0:D_CONV, :])[0:8, :]
        rows = lax.broadcasted_iota(jnp.int32, (e, 1), 0)
        lo = jnp.where(i > 0, 0, 8)
        hi = jnp.where(i < nblk - 1, e, tm + 8)
        ze = jnp.concatenate([zp_ref[...], z_ref[...], zn_ref[...]], axis=0)
        u, gb, gcv = ze[:, 0:512], ze[:, 512:1024], ze[:, 1024:1536]
        w0, w1, w2 = cw_ref[0:1, :], cw_ref[1:2, :], cw_ref[2:3, :]
        cu = jnp.where(rows >= lo, gcv * u, 0.0)
        cu1, cu2 = pltpu.roll(cu, 1, 0), pltpu.roll(cu, 2, 0)
        c = w0 * cu2 + w1 * cu1 + w2 * cu
        yc = gb * c
        dma = jnp.concatenate([jnp.zeros((8, 512), F32), dm[:, 0:512], dmn], axis=0)
        dyc, ych = _rms_bwd(yc, _rms_scale(yc), gc_ref[...], dma)
        dc = jnp.where(jnp.logical_and(rows >= 8, rows < hi), dyc * gb, 0.0)
        dcu = w0 * pltpu.roll(dc, e - 2, 0) + w1 * pltpu.roll(dc, e - 1, 0) + w2 * dc
        mid = slice(8, 8 + tm)
        dz_ref[:, 0:512] = (dcu * gcv)[mid, :].astype(BF16)
        dz_ref[:, 512:1024] = (dyc * c)[mid, :].astype(BF16)
        dz_ref[:, 1024:1536] = (dcu * u)[mid, :].astype(BF16)

        ya = y_ref[...]
        dmb = dm[:, 512:1024]
        dya, yah = _rms_bwd(ya, _rms_scale(ya), ga_ref[...], dmb)
        do_ref[...] = dya
        dd_ref[...] = _head_sum(dya * ya, bd_ref[...])

        @pl.when(i == 0)
        def _():
            dcw_ref[...] = jnp.zeros_like(dcw_ref)
            dgc_ref[...] = jnp.zeros_like(dgc_ref)
            dga_ref[...] = jnp.zeros_like(dga_ref)

        dcm = jnp.where(rows < tm + 8, dc, 0.0)
        dcw_ref[0:1, :] += jnp.sum(dcm * cu2, axis=0, keepdims=True)
        dcw_ref[1:2, :] += jnp.sum(dcm * cu1, axis=0, keepdims=True)
        dcw_ref[2:3, :] += jnp.sum(dcm * cu, axis=0, keepdims=True)
        dgc_ref[...] += jnp.sum((dma * ych)[mid, :], axis=0, keepdims=True)
        dga_ref[...] += jnp.sum(dmb * yah, axis=0, keepdims=True)

    blk = pl.BlockSpec((tm, 512), lambda i: (i, 0))
    vec = pl.BlockSpec((1, 512), lambda i: (0, 0))
    cwb = pl.BlockSpec((8, 512), lambda i: (0, 0))
    next16 = lambda i: (jnp.minimum((i + 1) * (tm // 16), t // 16 - 1), 0)
    return _call(
        body, [z, z, z, dx1, dx1, wout, y_attn, cw, g_conv, g_attn, ones_bd], name="mixer_bwd",
        grid=(nblk,),
        out_shape=[_sds((t, D_IN), BF16), _sds((t, 512), F32), _sds((t, 512), F32),
                   _sds((8, 512), F32), _sds((1, 512), F32), _sds((1, 512), F32)],
        in_specs=[pl.BlockSpec((tm, 1536), lambda i: (i, 0)), pl.BlockSpec((8, 1536), prev),
                  pl.BlockSpec((8, 1536), nxt), pl.BlockSpec((tm, D_MODEL), lambda i: (i, 0)),
                  pl.BlockSpec((16, D_MODEL), next16),
                  pl.BlockSpec(wout.shape, lambda i: (0, 0), pipeline_mode=_resident(True)),
                  blk, cwb, vec, vec, pl.BlockSpec((512, 512), lambda i: (0, 0))],
        out_specs=[pl.BlockSpec((tm, 1536), lambda i: (i, 0)), blk, blk, cwb, vec, vec],
        carry=carry)


def _qkv_bwd(z, dz, dqn, dkn, dv, gq, gk, ones_bd, tm, carry=None):
    t = z.shape[0]

    def body(zq_ref, zk_ref, _, dqn_ref, dkn_ref, dv_ref, gq_ref, gk_ref, bd_ref,
             dz_ref, dgq_ref, dgk_ref):
        bd = bd_ref[...]

        @pl.when(pl.program_id(0) == 0)
        def _():
            dgq_ref[...] = jnp.zeros_like(dgq_ref)
            dgk_ref[...] = jnp.zeros_like(dgk_ref)

        def back(v, dn, g, scale):
            r = _head_rms_scale(v, bd)
            vh = v * r
            dh = dn * (g * scale)
            dv = r * (dh - vh * (_head_sum(dh * vh, bd) * (1.0 / HEAD_DIM)))
            return dv, jnp.sum(dn * scale * vh, axis=0, keepdims=True)

        dq, dgq = back(zq_ref[...], dqn_ref[...], gq_ref[...], HEAD_DIM ** -0.5)
        dk, dgk = back(zk_ref[...], dkn_ref[...], gk_ref[...], 1.0)
        dgq_ref[...] += dgq
        dgk_ref[...] += dgk
        dz_ref[:, 0:512] = dq.astype(BF16)
        dz_ref[:, 512:1024] = dk.astype(BF16)
        dz_ref[:, 1024:1536] = dv_ref[...].astype(BF16)

    blk = pl.BlockSpec((tm, 512), lambda i: (i, 0))
    vec = pl.BlockSpec((1, 512), lambda i: (0, 0))
    return _call(
        body, [z, z, dz, dqn, dkn, dv, gq, gk, ones_bd], name="qkv_bwd", grid=(t // tm,),
        out_shape=[_sds((t, D_IN), BF16), _sds((1, 512), F32), _sds((1, 512), F32)],
        in_specs=[pl.BlockSpec((tm, 512), lambda i: (i, 3)), pl.BlockSpec((tm, 512), lambda i: (i, 4)),
                  ANY] + [blk] * 3 + [vec, vec, pl.BlockSpec((512, 512), lambda i: (0, 0))],
        out_specs=[pl.BlockSpec((tm, 1536), lambda i: (i, 1)), vec, vec],
        carry=carry, aliases={2: 0})


def _columns_from_chips(g):
    return g.transpose(1, 0, 2).reshape(g.shape[1], N_CHIPS * g.shape[2])


def kernel(x, g_mix, w_in, conv_w, g_q, g_k, g_conv_out, g_attn_out, w_out, g_ffn, w_gate, w_up, w_down, loss_target, m_g_mix, m_w_in, m_conv_w, m_g_q, m_g_k, m_g_conv_out, m_g_attn_out, m_w_out, m_g_ffn, m_w_gate, m_w_up, m_w_down, v_g_mix, v_w_in, v_conv_w, v_g_q, v_g_k, v_g_conv_out, v_g_attn_out, v_w_out, v_g_ffn, v_w_gate, v_w_up, v_w_down):
    t = x.shape[1]
    xs = x[0]
    target = loss_target[0]
    tm = min(512, t)
    tmm = min(2048, t)

    cw_pad = jnp.pad(conv_w[0], ((0, 13), (0, 0)))
    h1, gathered = _rms_norm(xs, g_mix, tm, carry=_x_gather_chips([w_in[0].astype(BF16), cw_pad]))
    gathered = _exchange_alone("gather_sibling_w_in", _x_gather_sibling(gathered))
    win = _columns_from_chips(gathered[0])
    cw = jnp.pad(gathered[1][:, 0:3, :].transpose(1, 0, 2).reshape(3, D_CONV), ((0, 5), (0, 0)))
    later = [w_out[0].astype(BF16), w_gate[0].T.astype(BF16), w_up[0].T.astype(BF16),
             w_down[0].astype(BF16)]

    head_id = jnp.arange(D_ATTN) // HEAD_DIM
    ones_bd = (head_id[:, None] == head_id[None, :]).astype(BF16)
    gq_t = jnp.tile(g_q, (1, D_ATTN // HEAD_DIM))
    gk_t = jnp.tile(g_k, (1, D_ATTN // HEAD_DIM))

    z, *dilated = _in_proj(h1, win, gq_t, gk_t, ones_bd, tm)
    nd = len(DILATIONS)
    qs, ks, vs = dilated[:nd], dilated[nd:2 * nd], dilated[2 * nd:]
    (y_attn, lse), gathered = _attn_fwd(qs, ks, vs, carry=_x_gather_chips(later))
    mix, gathered = _mixer_out(z, cw, y_attn, g_conv_out, g_attn_out, tm,
                               carry=_x_gather_sibling(gathered))
    wout = gathered[0].reshape(D_MODEL, D_MODEL)
    wgate_t = gathered[1].reshape(D_FF, D_MODEL)
    wup_t = gathered[2].reshape(D_FF, D_MODEL)
    wdown = gathered[3].reshape(D_FF, D_MODEL)
    (x1,) = _matmul("out_proj", mix, wout, [xs], [F32], lambda acc, r: (r + acc,), tm, D_MODEL)
    h2, gate, up, act = _norm_matmul("ffn_up", x1, g_ffn, [wgate_t, wup_t], tm, D_FF, True, BF16,
                                     transposed_w=True)

    def loss_epilogue(acc, r, tgt):
        err = r + acc - tgt
        dy = err * (1.0 / D_MODEL)
        return dy, dy, jnp.sum(err * err)

    dx2, dx2b, loss_sum = _matmul("ffn_down_loss", act, wdown, [x1, target], [F32, BF16],
                                  loss_epilogue, tm, D_MODEL, loss=True)

    def swiglu_bwd(da, gt, u):
        gt, u = gt.astype(F32), u.astype(F32)
        s = _sigmoid(gt)
        return da * u * (s * (1.0 + gt * (1.0 - s))), da * (gt * s)

    dgate, dup = _matmul("ffn_down_bwd", dx2b, wdown, [gate, up], [BF16, BF16], swiglu_bwd,
                         tm, D_FF, transposed_w=True)
    gw_down = _matmul_tn("grad_w_down", act, dx2b, 512, tmm)
    gw_gate_t = _matmul_tn("grad_w_gate", dgate, h2, 512, tmm)
    gw_up_t = _matmul_tn("grad_w_up", dup, h2, 512, tmm)

    me = 2 * lax.axis_index("x") + lax.axis_index("y")
    where = jnp.stack([lax.axis_index("c"), me]).astype(jnp.int32)

    def pair_sums(names, full, got):
        return [_pair_sum(f"pair_sum_{nme}", a, b, where) for nme, a, b in zip(names, full, got)]

    def chip_sums(names, pair, got):
        return [_chip_sum(f"chip_sum_{nme}", own, b) for nme, (_, own), b in zip(names, pair, got)]

    ffn = ["w_gate", "w_up", "w_down"]
    full = [g.reshape(N_CHIPS, D_FF // N_CHIPS, D_MODEL) for g in (gw_gate_t, gw_up_t, gw_down)]
    (dx1, dx1b, gg_ffn), got = _matmul_norm_bwd(
        "ffn_up_bwd", [(dgate, wgate_t), (dup, wup_t)], x1, dx2, g_ffn, tm, carry=_x_pair(full),
        transposed_w=False)
    pair = pair_sums(ffn, full, got)
    gw_out = _matmul_tn("grad_w_out", mix, dx1b, 512, tmm)
    full = [gw_out.reshape(N_CHIPS, D_MODEL // N_CHIPS, D_MODEL)]
    (dzc, do, dd, gcw, gg_conv, gg_attn), got = _mixer_bwd(
        z, dx1b, wout, y_attn, cw, g_conv_out, g_attn_out, ones_bd, tm, carry=_x_pair(full))
    pair += pair_sums(["w_out"], full, got)
    early = ffn + ["w_out"]
    (dqn, dkn, dv), got = _attn_bwd(qs, ks, vs, do, lse, dd, carry=_x_chips([p for p, _ in pair]))
    mine = chip_sums(early, pair, got)
    (dz, gg_q, gg_k), theirs = _qkv_bwd(z, dzc, dqn, dkn, dv, gq_t, gk_t, ones_bd, tm,
                                        carry=_x_share(mine))
    full = [_matmul_tn("grad_w_in", h1, dz, D_IN // N_CHIPS, tmm, by_chip=True)]
    got = _exchange_alone("grad_pair_exchange_w_in", _x_pair(full))
    pair = pair_sums(["w_in"], full, got)
    (grad_x, _, gg_mix), got = _matmul_norm_bwd("in_proj_bwd", [(dz, win)], xs, dx1, g_mix, tm,
                                                carry=_x_chips([pair[0][0]]))
    mine += chip_sums(["w_in"], pair, got)
    theirs = list(theirs) + list(_exchange_alone("grad_pair_share_w_in", _x_share(mine[-1:])))
    big = early + ["w_in"]

    small = _small_all_reduce({
        "g_mix": gg_mix, "g_ffn": gg_ffn, "g_conv_out": gg_conv, "g_attn_out": gg_attn,
        "g_q": gg_q, "g_k": gg_k, "loss": loss_sum, "conv_w": gcw})
    heads = D_ATTN // HEAD_DIM
    grads = {
        "g_mix": small[0:1, :], "g_ffn": small[1:2, :],
        "g_conv_out": small[2:3, 0:512], "g_attn_out": small[2:3, 512:1024],
        "g_q": small[3, 0:512].reshape(heads, HEAD_DIM).sum(axis=0)[None, :],
        "g_k": small[3, 512:1024].reshape(heads, HEAD_DIM).sum(axis=0)[None, :],
        "conv_w": lax.dynamic_slice(small[8:11, 0:512], (0, me * (D_CONV // N_CHIPS)),
                                    (3, D_CONV // N_CHIPS)),
    }
    halves = dict(zip(big, zip(mine, theirs)))
    loss = small[4, 0] * 0.5 * (1.0 / D_MODEL)

    weights = dict(g_mix=g_mix, w_in=w_in, conv_w=conv_w, g_q=g_q, g_k=g_k, g_conv_out=g_conv_out,
                   g_attn_out=g_attn_out, w_out=w_out, g_ffn=g_ffn, w_gate=w_gate, w_up=w_up,
                   w_down=w_down)
    moments_m = dict(g_mix=m_g_mix, w_in=m_w_in, conv_w=m_conv_w, g_q=m_g_q, g_k=m_g_k,
                     g_conv_out=m_g_conv_out, g_attn_out=m_g_attn_out, w_out=m_w_out, g_ffn=m_g_ffn,
                     w_gate=m_w_gate, w_up=m_w_up, w_down=m_w_down)
    moments_v = dict(g_mix=v_g_mix, w_in=v_w_in, conv_w=v_conv_w, g_q=v_g_q, g_k=v_g_k,
                     g_conv_out=v_g_conv_out, g_attn_out=v_g_attn_out, w_out=v_w_out, g_ffn=v_g_ffn,
                     w_gate=v_w_gate, w_up=v_w_up, w_down=v_w_down)
    names = list(weights)
    out_g, out_d, out_m, out_v = [], [], [], []
    for nme in names:
        wgt = weights[nme]
        shape2 = wgt.shape[-2:] if wgt.ndim == 3 else wgt.shape
        flip = nme in ("w_gate", "w_up")

        def to2d(a):
            return a.reshape(shape2).T if flip else a.reshape(shape2)

        def back(a):
            return (a.T if flip else a).reshape(wgt.shape)

        state = (to2d(wgt), to2d(moments_m[nme]), to2d(moments_v[nme]))
        if nme in halves:
            g2, dlt, nm, nv = _adamw_shard(f"adamw_{nme}", *state, *halves[nme], where)
        else:
            g2 = grads[nme].reshape(shape2)
            dlt, nm, nv = _adamw(f"adamw_{nme}", state[0], g2, state[1], state[2])
        out_g.append(back(g2))
        out_d.append(back(dlt))
        out_m.append(back(nm))
        out_v.append(back(nv))
    return (loss, grad_x[None], *out_g, *out_d, *out_m, *out_v)
```

```python
import functools
from typing import Any, Callable, NamedTuple, Sequence

import jax
import jax.numpy as jnp
from jax import lax
from jax.experimental import pallas as pl
from jax.experimental.pallas import tpu as pltpu

F32 = jnp.float32
BF16 = jnp.bfloat16
MESH = pl.DeviceIdType.MESH

D_MODEL = 1024
D_CONV = 512
D_ATTN = 512
HEAD_DIM = 64
D_FF = 2816
D_IN = 3 * D_CONV + 3 * D_ATTN
DILATIONS = (1, 4, 16)
BAND = 128
EPS = 1e-6
NEG = -1e30
N_CHIPS = 4

ADAM_LR = 0.001
ADAM_B1 = 0.9
ADAM_B2 = 0.999
ADAM_EPS = 1e-08
ADAM_WD = 0.01
ADAM_STEP = 10

V7X_VMEM_BYTES = 64 * 1024 * 1024
VMEM_LIMIT = V7X_VMEM_BYTES - 8 * 1024 * 1024
ANY = pl.BlockSpec(memory_space=pl.ANY)
VMEM_WHOLE = pl.BlockSpec(memory_space=pltpu.VMEM)


def _params(*sem):
    return pltpu.CompilerParams(dimension_semantics=sem, vmem_limit_bytes=VMEM_LIMIT)


def _sds(shape, dtype):
    return jax.ShapeDtypeStruct(shape, dtype)


def _resident(whole):
    return pl.Buffered(1) if whole else None


def _place():
    x, y, c = lax.axis_index("x"), lax.axis_index("y"), lax.axis_index("c")
    chips = [(1 - x, y), (x, 1 - y), (1 - x, 1 - y)]
    return x, y, c, 2 * x + y, chips, [2 * cx + cy for cx, cy in chips]


def _all_gather(shards):
    n = len(shards)

    def body(*refs):
        ins, outs, stage = refs[:n], refs[n:2 * n], refs[2 * n:3 * n]
        ssem, rsem, fsem, gsem, lsem, osem = refs[3 * n:]
        x, y, c, me, chips, cids = _place()
        sib = (x, y, 1 - c)

        def half(w, which):
            h = shards[w].shape[0] // 2
            return pl.ds(pl.multiple_of(which * h, 8), h)

        loads = [pltpu.make_async_copy(ins[w], stage[w], lsem.at[w]) for w in range(n)]
        local = [pltpu.make_async_copy(stage[w], outs[w].at[me], osem.at[w]) for w in range(n)]
        for cp in loads:
            cp.start()

        def chip_copy(w, j, src_slot):
            rows = half(w, c)
            return pltpu.make_async_remote_copy(
                src_ref=ins[w].at[rows], dst_ref=outs[w].at[src_slot, rows],
                send_sem=ssem.at[3 * w + j], recv_sem=rsem.at[3 * w + j],
                device_id=(*chips[j], c), device_id_type=MESH)

        def sib_copy(w, j, which):
            rows = half(w, which)
            return pltpu.make_async_remote_copy(
                src_ref=outs[w].at[cids[j], rows], dst_ref=outs[w].at[cids[j], rows],
                send_sem=fsem.at[3 * w + j], recv_sem=gsem.at[3 * w + j],
                device_id=sib, device_id_type=MESH)

        sends = [chip_copy(w, j, me) for w in range(n) for j in range(3)]
        for cp in sends:
            cp.start()
        for w in range(n):
            loads[w].wait()
            local[w].start()
        passed = []
        for w in range(n):
            for j in range(3):
                chip_copy(w, j, cids[j]).wait_recv()
                cp = sib_copy(w, j, c)
                cp.start()
                passed.append(cp)
        for w in range(n):
            for j in range(3):
                sib_copy(w, j, 1 - c).wait_recv()
        for cp in sends + passed:
            cp.wait_send()
        for cp in local:
            cp.wait()

    return pl.pallas_call(
        body, name="all_gather_weights",
        out_shape=[_sds((N_CHIPS,) + s.shape, s.dtype) for s in shards],
        in_specs=[ANY] * n, out_specs=[ANY] * n,
        scratch_shapes=[pltpu.VMEM(s.shape, s.dtype) for s in shards]
        + [pltpu.SemaphoreType.DMA((3 * n,))] * 4 + [pltpu.SemaphoreType.DMA((n,))] * 2,
        compiler_params=pltpu.CompilerParams(vmem_limit_bytes=VMEM_LIMIT),
    )(*shards)


class _Exchange(NamedTuple):
    srcs: Sequence[Any]
    lands: Sequence[Any]
    outs: Sequence[Any]
    n_sems: int
    copies: Callable


def _remote(src, dst, ssem, rsem, k, to):
    return pltpu.make_async_remote_copy(src_ref=src, dst_ref=dst, send_sem=ssem.at[k],
                                        recv_sem=rsem.at[k], device_id=to, device_id_type=MESH)


def _x_gather_chips(shards):
    def copies(srcs, lands, outs, ssem, rsem):
        _, _, c, me, chips, cids = _place()
        go, arrive = [], []
        for w, s in enumerate(shards):
            h = s.shape[0] // 2
            rows = pl.ds(pl.multiple_of(c * h, 8), h)
            for j in range(3):
                to = (*chips[j], c)
                go.append(_remote(srcs[w].at[rows], lands[w].at[me, rows], ssem, rsem, 3 * w + j, to))
                arrive.append(_remote(srcs[w].at[rows], lands[w].at[cids[j], rows], ssem, rsem,
                                      3 * w + j, to))
        return go, arrive

    lands = [jnp.broadcast_to(s[None], (N_CHIPS,) + s.shape) for s in shards]
    return _Exchange(shards, lands, [], 3 * len(shards), copies)


def _x_gather_sibling(gathered):
    def copies(srcs, lands, outs, ssem, rsem):
        x, y, c, _, _, cids = _place()
        go, arrive = [], []
        for w, g in enumerate(gathered):
            h = g.shape[1] // 2
            mine = pl.ds(pl.multiple_of(c * h, 8), h)
            theirs = pl.ds(pl.multiple_of((1 - c) * h, 8), h)
            for j in range(3):
                slab = lands[w].at[cids[j]]
                go.append(_remote(slab.at[mine], slab.at[mine], ssem, rsem, 3 * w + j, (x, y, 1 - c)))
                arrive.append(_remote(slab.at[theirs], slab.at[theirs], ssem, rsem, 3 * w + j,
                                      (x, y, 1 - c)))
        return go, arrive

    return _Exchange([], gathered, [], 3 * len(gathered), copies)


def _x_pair(grads):
    def copies(srcs, lands, outs, ssem, rsem):
        x, y, c, _, _, _ = _place()
        go = []
        for w, g in enumerate(grads):
            h = g.shape[1] // 2
            theirs = pl.ds(pl.multiple_of((1 - c) * h, 8), h)
            go.append(_remote(srcs[w].at[:, theirs, :], outs[w], ssem, rsem, w, (x, y, 1 - c)))
        return go, go

    outs = [_sds((N_CHIPS, g.shape[1] // 2, g.shape[2]), g.dtype) for g in grads]
    return _Exchange(grads, [], outs, len(grads), copies)


def _x_chips(parts):
    def copies(srcs, lands, outs, ssem, rsem):
        _, _, c, _, chips, cids = _place()
        go = [_remote(srcs[w].at[cids[j]], outs[w].at[j], ssem, rsem, 3 * w + j, (*chips[j], c))
              for w in range(len(parts)) for j in range(3)]
        return go, go

    outs = [_sds((3,) + p.shape[1:], p.dtype) for p in parts]
    return _Exchange(parts, [], outs, 3 * len(parts), copies)


def _x_share(halves):
    def copies(srcs, lands, outs, ssem, rsem):
        x, y, c, _, _, _ = _place()
        go = [_remote(srcs[w], outs[w], ssem, rsem, w, (x, y, 1 - c)) for w in range(len(halves))]
        return go, go

    return _Exchange(halves, [], [_sds(h.shape, h.dtype) for h in halves], len(halves), copies)


def _call(body, args, *, name, grid, in_specs, out_specs, out_shape, scratch_shapes=(),
          semantics=None, carry=None, aliases=None):
    single = not isinstance(out_shape, (list, tuple))
    out_shape = [out_shape] if single else list(out_shape)
    out_specs = [out_specs] if single else list(out_specs)
    aliases = dict(aliases or {})
    if carry is None:
        res = pl.pallas_call(
            body, name=name, grid=grid, in_specs=list(in_specs), out_specs=out_specs,
            out_shape=out_shape, scratch_shapes=list(scratch_shapes), input_output_aliases=aliases,
            compiler_params=_params(*(semantics or ("arbitrary",) * len(grid))))(*args)
        return res[0] if single else res
    n_in, n_out, n_scr = len(args), len(out_shape), len(scratch_shapes)
    n_src, n_land, n_new = len(carry.srcs), len(carry.lands), len(carry.outs)

    def carrying(*refs):
        at = 0
        parts = []
        for n in (n_in, n_src, n_land, n_out, n_land, n_new, n_scr, 2):
            parts.append(refs[at:at + n])
            at += n
        ins, srcs, _, outs, lands, news, scratch, (ssem, rsem) = parts
        ids = [pl.program_id(a) for a in range(len(grid))]
        first = functools.reduce(jnp.logical_and, [i == 0 for i in ids])
        last = functools.reduce(jnp.logical_and, [i == g - 1 for i, g in zip(ids, grid)])
        go, arrive = carry.copies(srcs, lands, news, ssem, rsem)

        @pl.when(first)
        def _():
            for cp in go:
                cp.start()

        body(*ins, *outs, *scratch)

        @pl.when(last)
        def _():
            for cp in go:
                cp.wait_send()
            for cp in arrive:
                cp.wait_recv()

    res = pl.pallas_call(
        carrying, name=name, grid=grid,
        in_specs=list(in_specs) + [ANY] * (n_src + n_land),
        out_specs=out_specs + [ANY] * (n_land + n_new),
        out_shape=out_shape + [_sds(a.shape, a.dtype) for a in carry.lands] + list(carry.outs),
        input_output_aliases={**aliases, **{n_in + n_src + i: n_out + i for i in range(n_land)}},
        scratch_shapes=list(scratch_shapes) + [pltpu.SemaphoreType.DMA((carry.n_sems,))] * 2,
        compiler_params=_params(*(("arbitrary",) * len(grid))))(*args, *carry.srcs, *carry.lands)
    own = res[:n_out]
    return (own[0] if single else own), res[n_out:]


def _exchange_alone(name, exchange):
    def body(x_ref, o_ref):
        o_ref[...] = x_ref[...]

    blk = pl.BlockSpec((8, 128), lambda i: (0, 0))
    _, res = _call(body, [jnp.zeros((8, 128), F32)], name=name, grid=(1,), in_specs=[blk],
                   out_specs=blk, out_shape=_sds((8, 128), F32), carry=exchange)
    return res


def _row_block(r, want):
    return max(d for d in range(1, min(want, r) + 1) if r % d == 0 and (d % 8 == 0 or d == r))


def _pair_sum(name, full, got, where):
    _, r, n = full.shape
    h = r // 2
    tr = _row_block(h, 256)
    nb = h // tr

    def body(w_ref, a_ref, b_ref, o_ref, own_ref):
        total = a_ref[...] + b_ref[...]
        o_ref[...] = total.astype(BF16)

        @pl.when(pl.program_id(1) == w_ref[1])
        def _():
            own_ref[...] = total[0]

    blk = pl.BlockSpec((1, tr, n), lambda i, s, w: (s, i, 0))
    return pl.pallas_call(
        body, name=name, out_shape=[_sds(got.shape, BF16), _sds((h, n), F32)],
        grid_spec=pltpu.PrefetchScalarGridSpec(
            num_scalar_prefetch=1, grid=(nb, N_CHIPS),
            in_specs=[pl.BlockSpec((1, tr, n), lambda i, s, w: (s, w[0] * nb + i, 0)), blk],
            out_specs=[blk, pl.BlockSpec((tr, n), lambda i, s, w: (i, 0))]),
        compiler_params=_params("parallel", "arbitrary"),
    )(where, full, got)


def _chip_sum(name, own, got):
    h, n = own.shape
    tr = _row_block(h, 256)

    def body(a_ref, b0, b1, b2, o_ref):
        o_ref[...] = ((a_ref[...] + b0[0].astype(F32)) + b1[0].astype(F32)) + b2[0].astype(F32)

    def slot(j):
        return pl.BlockSpec((1, tr, n), lambda i: (j, i, 0))

    blk = pl.BlockSpec((tr, n), lambda i: (i, 0))
    return pl.pallas_call(
        body, name=name, grid=(h // tr,), out_shape=_sds((h, n), F32),
        in_specs=[blk, slot(0), slot(1), slot(2)], out_specs=blk,
        compiler_params=_params("parallel"),
    )(own, got, got, got)


SMALL_ROWS = 16
SMALL_LAYOUT = (
    ("g_mix", 0, 0, 1, 1024), ("g_ffn", 1, 0, 1, 1024), ("g_conv_out", 2, 0, 1, 512),
    ("g_attn_out", 2, 512, 1, 512), ("g_q", 3, 0, 1, 512), ("g_k", 3, 512, 1, 512),
    ("loss", 4, 0, 1, 128), ("conv_w", 8, 0, 8, 512))


def _small_all_reduce(parts):
    names = [s[0] for s in SMALL_LAYOUT]

    def body(*refs):
        ins = refs[:len(names)]
        out_ref, stage, buf, ssem, rsem = refs[len(names):]
        x, y, c, _, _, _ = _place()
        me = 4 * x + 2 * y + c
        stage[...] = jnp.zeros_like(stage)
        for ref, (_, r0, c0, nr, nc) in zip(ins, SMALL_LAYOUT):
            stage[r0:r0 + nr, c0:c0 + nc] = ref[0:nr, :]
        buf[me] = stage[...]
        peers = []
        for d in range(1, 8):
            px = 1 - x if d & 4 else x
            py = 1 - y if d & 2 else y
            pc = 1 - c if d & 1 else c
            peers.append(((px, py, pc), 4 * px + 2 * py + pc))
        sends = [pltpu.make_async_remote_copy(
            src_ref=stage, dst_ref=buf.at[me], send_sem=ssem.at[k], recv_sem=rsem.at[k],
            device_id=peer, device_id_type=MESH) for k, (peer, _) in enumerate(peers)]
        for cp in sends:
            cp.start()
        for k, (peer, pid) in enumerate(peers):
            pltpu.make_async_remote_copy(
                src_ref=stage, dst_ref=buf.at[pid], send_sem=ssem.at[k], recv_sem=rsem.at[k],
                device_id=peer, device_id_type=MESH).wait_recv()
        for cp in sends:
            cp.wait_send()
        acc = buf[0]
        for k in range(1, 8):
            acc = acc + buf[k]
        out_ref[...] = acc

    return pl.pallas_call(
        body, name="small_all_reduce", out_shape=_sds((SMALL_ROWS, 1024), F32),
        in_specs=[VMEM_WHOLE] * len(names), out_specs=VMEM_WHOLE,
        scratch_shapes=[pltpu.VMEM((SMALL_ROWS, 1024), F32), pltpu.VMEM((8, SMALL_ROWS, 1024), F32),
                        pltpu.SemaphoreType.DMA((7,)), pltpu.SemaphoreType.DMA((7,))],
    )(*[parts[k] for k in names])


def _dot(a, b):
    return jnp.dot(a, b, preferred_element_type=F32)


def _dot_nt(a, b):
    return lax.dot_general(a, b, (((1,), (1,)), ((), ())), preferred_element_type=F32)


def _dot_tn(a, b):
    return lax.dot_general(a, b, (((0,), (0,)), ((), ())), preferred_element_type=F32)


def _sigmoid(v):
    return 1.0 / (1.0 + jnp.exp(-v))


def _rms_scale(v):
    return lax.rsqrt(jnp.mean(v * v, axis=-1, keepdims=True) + EPS)


def _rms_bwd(v, r, g, dy):
    vh = v * r
    dh = dy * g
    return r * (dh - vh * jnp.mean(dh * vh, axis=-1, keepdims=True)), vh


def _head_sum(a, ones_bd):
    hi = a.astype(BF16)
    lo = (a - hi.astype(F32)).astype(BF16)
    return _dot(hi, ones_bd) + _dot(lo, ones_bd)


def _head_rms_scale(v, ones_bd):
    return lax.rsqrt(_head_sum(v * v, ones_bd) * (1.0 / HEAD_DIM) + EPS)


MXU_COLUMNS = 256


def _column_chunks(n):
    width = MXU_COLUMNS if n % MXU_COLUMNS == 0 else n
    return [slice(c, c + width) for c in range(0, n, width)]


def _norm_matmul(name, x, g, ws, tm, tn, swiglu, out_dtype=F32, transposed_w=False):
    t, d = x.shape
    n = ws[0].shape[0] if transposed_w else ws[0].shape[1]
    nw = len(ws)

    def body(x_ref, g_ref, *refs):
        w_refs, h_ref, o_refs = refs[:nw], refs[nw], refs[nw + 1:2 * nw + 1]
        hs = refs[-1]

        @pl.when(pl.program_id(1) == 0)
        def _():
            xv = x_ref[...]
            h = (xv * _rms_scale(xv) * g_ref[...]).astype(BF16)
            hs[...] = h
            h_ref[...] = h

        h = hs[...]
        for cols in _column_chunks(tn):
            outs = [_dot_nt(h, w[cols, :]) if transposed_w else _dot(h, w[:, cols]) for w in w_refs]
            for o_ref, o in zip(o_refs, outs):
                o_ref[:, cols] = o.astype(out_dtype)
            if swiglu:
                refs[2 * nw + 1][:, cols] = (outs[0] * _sigmoid(outs[0]) * outs[1]).astype(BF16)

    row = pl.BlockSpec((tm, d), lambda i, j: (i, 0))
    col = pl.BlockSpec((tm, tn), lambda i, j: (i, j))
    out_shape = [_sds((t, d), BF16)] + [_sds((t, n), out_dtype)] * nw
    out_specs = [row] + [col] * nw
    if swiglu:
        out_shape.append(_sds((t, n), BF16))
        out_specs.append(col)
    return pl.pallas_call(
        body, name=name, grid=(t // tm, n // tn), out_shape=out_shape,
        in_specs=[row, pl.BlockSpec((1, d), lambda i, j: (0, 0))]
        + [pl.BlockSpec((tn, d), lambda i, j: (j, 0), pipeline_mode=_resident(tn == n))
           if transposed_w
           else pl.BlockSpec((d, tn), lambda i, j: (0, j), pipeline_mode=_resident(tn == n))] * nw,
        out_specs=out_specs, scratch_shapes=[pltpu.VMEM((tm, d), BF16)],
        compiler_params=_params("parallel", "arbitrary"),
    )(x, g, *ws)


def _matmul(name, a, w, extras, out_dtypes, epilogue, tm, tn, transposed_w=False, loss=False):
    t, k = a.shape
    n = w.shape[0] if transposed_w else w.shape[1]
    ne, no = len(extras), len(out_dtypes)

    def body(a_ref, w_ref, *refs):
        e_refs, o_refs = refs[:ne], refs[ne:]
        a = a_ref[...]
        total = 0.0
        for cols in _column_chunks(tn):
            acc = _dot_nt(a, w_ref[cols, :]) if transposed_w else _dot(a, w_ref[:, cols])
            res = epilogue(acc, *[e[:, cols] for e in e_refs])
            for o_ref, r in zip(o_refs[:no], res[:no]):
                o_ref[:, cols] = r.astype(o_ref.dtype)
            if loss:
                total = total + res[no]
        if loss:
            first = jnp.logical_and(pl.program_id(0) == 0, pl.program_id(1) == 0)

            @pl.when(first)
            def _():
                o_refs[no][...] = jnp.zeros_like(o_refs[no])

            o_refs[no][...] += total

    col = pl.BlockSpec((tm, tn), lambda i, j: (i, j))
    w_spec = (pl.BlockSpec((tn, k), lambda i, j: (j, 0), pipeline_mode=_resident(tn == n))
              if transposed_w
              else pl.BlockSpec((k, tn), lambda i, j: (0, j), pipeline_mode=_resident(tn == n)))
    out_shape = [_sds((t, n), dt) for dt in out_dtypes]
    out_specs = [col] * no
    if loss:
        out_shape.append(_sds((8, 128), F32))
        out_specs.append(pl.BlockSpec((8, 128), lambda i, j: (0, 0)))
    return pl.pallas_call(
        body, name=name, grid=(t // tm, n // tn), out_shape=out_shape,
        in_specs=[pl.BlockSpec((tm, k), lambda i, j: (i, 0)), w_spec] + [col] * ne,
        out_specs=out_specs,
        compiler_params=_params(*(("arbitrary", "arbitrary") if loss else ("parallel", "parallel"))),
    )(a, w, *extras)


def _matmul_norm_bwd(name, pairs, x, dres, g, tm, carry=None, transposed_w=True):
    t, d = x.shape
    npairs = len(pairs)
    product = _dot_nt if transposed_w else _dot

    def body(*refs):
        a_refs, w_refs = refs[:npairs], refs[npairs:2 * npairs]
        x_ref, r_ref, g_ref, dx_ref, dxb_ref, dg_ref = refs[2 * npairs:]
        dy = product(a_refs[0][...], w_refs[0][...])
        for a_ref, w_ref in zip(a_refs[1:], w_refs[1:]):
            dy = dy + product(a_ref[...], w_ref[...])
        xv = x_ref[...]
        dx, xh = _rms_bwd(xv, _rms_scale(xv), g_ref[...], dy)
        dx = dx + r_ref[...]
        dx_ref[...] = dx
        dxb_ref[...] = dx.astype(BF16)

        @pl.when(pl.program_id(0) == 0)
        def _():
            dg_ref[...] = jnp.zeros_like(dg_ref)

        dg_ref[...] += jnp.sum(dy * xh, axis=0, keepdims=True)

    row = pl.BlockSpec((tm, d), lambda i: (i, 0))
    vec = pl.BlockSpec((1, d), lambda i: (0, 0))
    return _call(
        body, [a for a, _ in pairs] + [w for _, w in pairs] + [x, dres, g], name=name,
        grid=(t // tm,), out_shape=[_sds((t, d), F32), _sds((t, d), BF16), _sds((1, d), F32)],
        in_specs=[pl.BlockSpec((tm, a.shape[1]), lambda i: (i, 0)) for a, _ in pairs]
        + [pl.BlockSpec(w.shape, lambda i: (0, 0), pipeline_mode=pl.Buffered(1)) for _, w in pairs]
        + [row, row, vec],
        out_specs=[row, row, vec], carry=carry)


def _matmul_tn(name, a, g, tn, tk, by_chip=False):
    t, ka = a.shape
    n = g.shape[1]

    def body(a_ref, g_ref, o_ref):
        @pl.when(pl.program_id(1) == 0)
        def _():
            o_ref[...] = jnp.zeros_like(o_ref)

        acc = _dot_tn(a_ref[...], g_ref[...])
        o_ref[...] += acc[None] if by_chip else acc

    return pl.pallas_call(
        body, name=name, grid=(n // tn, t // tk),
        out_shape=_sds((n // tn, ka, tn) if by_chip else (ka, n), F32),
        in_specs=[pl.BlockSpec((tk, ka), lambda j, s: (s, 0)),
                  pl.BlockSpec((tk, tn), lambda j, s: (s, j))],
        out_specs=(pl.BlockSpec((1, ka, tn), lambda j, s: (j, 0, 0)) if by_chip
                   else pl.BlockSpec((ka, tn), lambda j, s: (0, j))),
        compiler_params=_params("parallel", "arbitrary"),
    )(a, g)


def _elementwise(name, fn, ins, out_dtypes, tr):
    r, n = ins[0].shape
    tr = _row_block(r, tr)
    ni = len(ins)

    def body(*refs):
        res = fn(*[ref[...] for ref in refs[:ni]])
        for o_ref, v in zip(refs[ni:], res):
            o_ref[...] = v.astype(o_ref.dtype)

    blk = pl.BlockSpec((tr, n), lambda i: (i, 0))
    return pl.pallas_call(
        body, name=name, grid=(r // tr,), out_shape=[_sds((r, n), dt) for dt in out_dtypes],
        in_specs=[blk] * ni, out_specs=[blk] * len(out_dtypes),
        compiler_params=_params("parallel"),
    )(*ins)


def _adamw_update(w, g, m, v):
    m = ADAM_B1 * m + (1.0 - ADAM_B1) * g
    v = ADAM_B2 * v + (1.0 - ADAM_B2) * (g * g)
    m_hat = m / (1.0 - ADAM_B1 ** ADAM_STEP)
    v_hat = v / (1.0 - ADAM_B2 ** ADAM_STEP)
    return -ADAM_LR * (m_hat / (jnp.sqrt(v_hat) + ADAM_EPS) + ADAM_WD * w), m, v


def _adamw(name, w, g, m, v):
    return _elementwise(name, _adamw_update, [w, g, m, v], [F32] * 3, 256)


def _adamw_shard(name, w, m, v, mine, theirs, where):
    r, n = w.shape
    h = r // 2
    tr = _row_block(h, 256)
    nb = h // tr

    def body(w_ref, p_ref, m_ref, v_ref, a_ref, b_ref, g_ref, d_ref, nm_ref, nv_ref):
        g = jnp.where(pl.program_id(0) == w_ref[0], a_ref[...], b_ref[...])
        g_ref[...] = g
        d_ref[...], nm_ref[...], nv_ref[...] = _adamw_update(p_ref[...], g, m_ref[...], v_ref[...])

    whole = pl.BlockSpec((tr, n), lambda s, i, c: (s * nb + i, 0))
    used = pl.BlockSpec((tr, n), lambda s, i, c: (jnp.where(s == c[0], i, 0), 0))
    unused = pl.BlockSpec((tr, n), lambda s, i, c: (jnp.where(s == c[0], 0, i), 0))
    return pl.pallas_call(
        body, name=name, out_shape=[_sds((r, n), F32)] * 4,
        grid_spec=pltpu.PrefetchScalarGridSpec(
            num_scalar_prefetch=1, grid=(2, nb), in_specs=[whole] * 3 + [used, unused],
            out_specs=[whole] * 4),
        compiler_params=_params("arbitrary", "arbitrary"),
    )(where, w, m, v, mine, theirs)


PAIRS = D_ATTN // BAND


def _in_proj(x, g, w, gq, gk, ones_bd, tm):
    t, dm = x.shape
    n = w.shape[1]
    nd = len(DILATIONS)
    first = 3 * D_CONV

    def body(x_ref, g_ref, w_ref, gq_ref, gk_ref, bd_ref, h_ref, z_ref, *refs):
        outs, slabs = refs[:3 * nd], refs[3 * nd:]
        xv = x_ref[...]
        h = (xv * _rms_scale(xv) * g_ref[...]).astype(BF16)
        h_ref[...] = h
        for cols in _column_chunks(n):
            z_ref[:, cols] = _dot(h, w_ref[:, cols])
        bd = bd_ref[...]
        q = z_ref[:, first:first + D_ATTN]
        k = z_ref[:, first + D_ATTN:first + 2 * D_ATTN]
        vals = [(q * _head_rms_scale(q, bd) * gq_ref[...]) * HEAD_DIM ** -0.5,
                k * _head_rms_scale(k, bd) * gk_ref[...], z_ref[:, first + 2 * D_ATTN:n]]
        for m, val in enumerate(vals):
            for c in range(PAIRS):
                slabs[0][c] = val[:, c * BAND:(c + 1) * BAND]
            cur, before = 0, 1
            for a, d in enumerate(DILATIONS):
                o_ref, src, dst = outs[m * nd + a], slabs[cur], slabs[1 - cur]
                step, count = d // before, tm // d
                keep = step > 1 and a + 1 < nd
                for c in range(PAIRS):
                    for r in range(d):
                        start = (r % before) * (tm // before) + r // before
                        rows = src.at[c][pl.ds(start, count, stride=step), :] if step > 1 else src[c]
                        o_ref[c, r] = rows.astype(BF16)
                        if keep:
                            dst.at[c][pl.ds(r * count, count), :] = rows
                if keep:
                    cur = 1 - cur
                before = d

    row = pl.BlockSpec((tm, dm), lambda i: (i, 0))
    vec = pl.BlockSpec((1, D_ATTN), lambda i: (0, 0))
    return pl.pallas_call(
        body, name="in_proj", grid=(t // tm,),
        out_shape=[_sds((t, dm), BF16), _sds((t, n), F32)]
        + [_sds((PAIRS, d, t // d, BAND), BF16) for _ in range(3) for d in DILATIONS],
        in_specs=[row, pl.BlockSpec((1, dm), lambda i: (0, 0)),
                  pl.BlockSpec((dm, n), lambda i: (0, 0), pipeline_mode=_resident(True)), vec, vec,
                  pl.BlockSpec((D_ATTN, D_ATTN), lambda i: (0, 0), pipeline_mode=_resident(True))],
        out_specs=[row, pl.BlockSpec((tm, n), lambda i: (i, 0))]
        + [pl.BlockSpec((PAIRS, d, tm // d, BAND), lambda i: (0, 0, i, 0))
           for _ in range(3) for d in DILATIONS],
        scratch_shapes=[pltpu.VMEM((PAIRS, tm, BAND), F32)] * 2,
        compiler_params=_params("parallel"),
    )(x, g, w, gq, gk, ones_bd)


TOK = 2048
UNITS = TOK // BAND


def _stack_masks():
    row = lax.broadcasted_iota(jnp.int32, (2 * BAND, 2 * BAND), 0) & (BAND - 1)
    col = lax.broadcasted_iota(jnp.int32, (2 * BAND, 2 * BAND), 1)
    lane = lax.broadcasted_iota(jnp.int32, (BAND, BAND), 1)
    head0 = lane < HEAD_DIM
    ones = [jnp.where(head0, 1.0, 0.0).astype(BF16), jnp.where(head0, 0.0, 1.0).astype(BF16)]
    return col - row, col, head0, ones


def _split3(x):
    hi = x.astype(BF16).astype(F32)
    mid = (x - hi).astype(BF16).astype(F32)
    return hi, mid, x - hi - mid


def _gather(srcs, dst, d, before=1):
    per, step, span = TOK // d, d // before, TOK // before
    at = 0
    for r in range(d):
        start = (r % before) * span + r // before
        for src in srcs:
            rows = src[pl.ds(start, per, stride=step), :] if step > 1 else src[pl.ds(start, per), :]
            dst[pl.ds(at, per), :] = rows.astype(dst.dtype)
            at += per


def _scatter(out_ref, src, d):
    per = TOK // d
    if d == 1:
        out_ref[...] = src[...]
        return
    for r in range(d):
        out_ref[pl.ds(r, per, stride=d), :] = src[pl.ds(r * per, per), :]


def _dilated_specs(nblk, reverse):
    def at(s):
        return (nblk - 1 - s) if reverse else s
    main = [pl.BlockSpec((1, d, TOK // d, BAND), lambda j, s: (j, 0, at(s), 0)) for d in DILATIONS]
    prev = [pl.BlockSpec((1, d, TOK // d, BAND), lambda j, s: (j, 0, jnp.maximum(at(s) - 1, 0), 0))
            for d in DILATIONS]
    return main, prev


def _window_rows(prev_ref, main_ref, dst, d):
    per = TOK // d
    for r in range(d):
        dst[pl.ds(r * (per + BAND), BAND), :] = prev_ref[0, r, pl.ds(per - BAND, BAND), :]
        dst[pl.ds(r * (per + BAND) + BAND, per), :] = main_ref[0, r]


def _attn_fwd(qs, ks, vs, carry=None):
    t = qs[0].shape[2]
    nblk = t // TOK
    nd = len(DILATIONS)

    def body(*refs):
        q_refs, kp_refs, k_refs = refs[:nd], refs[nd:2 * nd], refs[2 * nd:3 * nd]
        vp_refs, v_refs = refs[3 * nd:4 * nd], refs[4 * nd:5 * nd]
        y_ref, l_ref, kw_s, vw_s, ob, lb, on, ln = refs[5 * nd:]
        i = pl.program_id(1)
        diff, col, head0, hm = _stack_masks()
        band_ok = jnp.logical_and(diff >= 0, diff <= BAND)
        for g, d in enumerate(DILATIONS):
            per = TOK // d
            nb = per // BAND
            pad = per + BAND
            _window_rows(kp_refs[g], k_refs[g], kw_s, d)
            _window_rows(vp_refs[g], v_refs[g], vw_s, d)
            q_ref = q_refs[g]

            def unit(u, carry):
                r, b = u // nb, u % nb
                qu = q_ref[0, r, pl.ds(pl.multiple_of(b * BAND, BAND), BAND), :]
                start = pl.multiple_of(r * pad + b * BAND, BAND)
                kw = kw_s[pl.ds(start, 2 * BAND), :]
                vw = vw_s[pl.ds(start, 2 * BAND), :]
                lo = jnp.where(jnp.logical_and(i == 0, b == 0), BAND, 0)
                s = _dot_nt(jnp.concatenate([qu * hm[0], qu * hm[1]], axis=0), kw)
                s = jnp.where(jnp.logical_and(band_ok, col >= lo), s, NEG)
                mx = jnp.max(s, axis=-1, keepdims=True)
                e = jnp.exp(s - mx)
                den = jnp.sum(e, axis=-1, keepdims=True)
                o2 = _dot(e.astype(BF16), vw) / den
                l2 = jnp.broadcast_to(mx + jnp.log(den), (2 * BAND, BAND))
                rows = pl.ds(pl.multiple_of(u * BAND, BAND), BAND)
                ob[rows, :] = jnp.where(head0, o2[:BAND], o2[BAND:])
                lb[rows, :] = jnp.where(head0, l2[:BAND], l2[BAND:])
                return carry

            lax.fori_loop(0, UNITS, unit, 0, unroll=16)
            _scatter(on.at[g], ob, d)
            _scatter(ln.at[g], lb, d)
        ls = [ln[0], ln[1], ln[2]]
        mx = jnp.maximum(jnp.maximum(ls[0], ls[1]), ls[2])
        es = [jnp.exp(l - mx) for l in ls]
        tot = es[0] + es[1] + es[2]
        y_ref[...] = (es[0] * on[0] + es[1] * on[1] + es[2] * on[2]) / tot
        l_ref[...] = mx + jnp.log(tot)

    main, prev = _dilated_specs(nblk, False)
    out = pl.BlockSpec((TOK, BAND), lambda j, i: (i, j))
    win_rows = max(d * (TOK // d + BAND) for d in DILATIONS)
    return _call(
        body, list(qs) + list(ks) + list(ks) + list(vs) + list(vs), name="attn_fwd",
        grid=(PAIRS, nblk), out_shape=[_sds((t, D_ATTN), F32)] * 2,
        in_specs=main + prev + main + prev + main, out_specs=[out, out],
        scratch_shapes=[pltpu.VMEM((win_rows, BAND), BF16)] * 2 + [pltpu.VMEM((TOK, BAND), F32)] * 2
        + [pltpu.VMEM((nd, TOK, BAND), F32)] * 2,
        semantics=("parallel", "parallel"), carry=carry)


def _attn_bwd(qs, ks, vs, do, lse, dd, carry=None):
    t = qs[0].shape[2]
    nblk = t // TOK
    nd = len(DILATIONS)
    offs = [sum(DILATIONS[:g]) * BAND for g in range(nd)]

    def body(*refs):
        q_refs, kp_refs, k_refs = refs[:nd], refs[nd:2 * nd], refs[2 * nd:3 * nd]
        vp_refs, v_refs = refs[3 * nd:4 * nd], refs[4 * nd:5 * nd]
        (do_ref, l_ref, d_ref, dq_ref, dk_ref, dv_ref, kw_s, vw_s, dos, lds, pn, dqb, dkb, dvb, ckb,
         cvb, *more) = refs[5 * nd:]
        folds, mids = more[:6], more[6:]
        step = pl.program_id(1)
        i = nblk - 1 - step
        key = lax.broadcasted_iota(jnp.int32, (2 * BAND, 2 * BAND), 0)
        qry = lax.broadcasted_iota(jnp.int32, (2 * BAND, 2 * BAND), 1) & (BAND - 1)
        off = key - qry
        band_ok = jnp.logical_and(off >= 0, off <= BAND)
        lane = lax.broadcasted_iota(jnp.int32, (BAND, BAND), 1)
        head0 = lane < HEAD_DIM
        hm = [jnp.where(head0, 1.0, 0.0).astype(BF16), jnp.where(head0, 0.0, 1.0).astype(BF16)]
        lane2 = lax.broadcasted_iota(jnp.int32, (2 * BAND, BAND), 1) & (HEAD_DIM - 1)
        ones_l = jnp.where(lane2 < 3, 1.0, 0.0).astype(BF16)
        ones_d = jnp.where(jnp.logical_and(lane2 >= 3, lane2 < 6), 1.0, 0.0).astype(BF16)
        piece = lax.broadcasted_iota(jnp.int32, (TOK, BAND), 1) & (HEAD_DIM - 1)

        def pieces(x, at):
            hi, mid, lo = _split3(-x)
            return jnp.where(piece == at, hi,
                             jnp.where(piece == at + 1, mid, jnp.where(piece == at + 2, lo, 0.0)))

        pn[...] = pieces(l_ref[...], 0) + pieces(d_ref[...], 3)
        order = sorted(range(nd), key=lambda a: -DILATIONS[a])
        assert DILATIONS[order[-1]] == 1
        levels = {1: (do_ref, pn)}
        for n, a in enumerate(reversed(order[1:-1])):
            d, before = DILATIONS[a], DILATIONS[order[-1 - n]]
            levels[d] = (mids[2 * n], mids[2 * n + 1])
            for src, dst in zip(levels[before], levels[d]):
                _gather([src], dst, d, before)
        for pos, g in enumerate(order):
            d = DILATIONS[g]
            per = TOK // d
            nb = per // BAND
            pad = per + BAND
            _window_rows(kp_refs[g], k_refs[g], kw_s, d)
            _window_rows(vp_refs[g], v_refs[g], vw_s, d)
            known = d if d in levels else DILATIONS[order[pos + 1]]
            _gather([levels[known][0]], dos, d, known)
            _gather([levels[known][1]], lds, d, known)
            for r in range(d):
                spare = pl.ds(r * pad, BAND)
                dkb[spare, :] = jnp.zeros((BAND, BAND), F32)
                dvb[spare, :] = jnp.zeros((BAND, BAND), F32)
            q_ref = q_refs[g]

            def unit(u, c_):
                r, b = u // nb, u % nb
                rows = pl.ds(pl.multiple_of(u * BAND, BAND), BAND)
                qu = q_ref[0, r, pl.ds(pl.multiple_of(b * BAND, BAND), BAND), :]
                dou, ldu = dos[rows, :], lds[rows, :]
                q2 = jnp.concatenate([qu * hm[0], qu * hm[1]], axis=0)
                do2 = jnp.concatenate([dou * hm[0], dou * hm[1]], axis=0)
                ld2 = jnp.concatenate([ldu * hm[0], ldu * hm[1]], axis=0)
                acc = pl.ds(pl.multiple_of(r * pad + b * BAND, BAND), 2 * BAND)
                kw = kw_s[acc, :]
                vw = vw_s[acc, :]
                lo = jnp.where(jnp.logical_and(i == 0, b == 0), BAND, 0)
                ok = jnp.logical_and(band_ok, key >= lo)
                st = _dot_nt(jnp.concatenate([kw, ones_l], axis=1), jnp.concatenate([q2, ld2], axis=1))
                dpt = _dot_nt(jnp.concatenate([vw, ones_d], axis=1), jnp.concatenate([do2, ld2], axis=1))
                pt = jnp.where(ok, jnp.exp(st), 0.0)
                dst = (pt * dpt).astype(BF16)
                low = pl.ds(pl.multiple_of(r * pad + b * BAND, BAND), BAND)
                high = pl.ds(pl.multiple_of(r * pad + (b + 1) * BAND, BAND), BAND)
                dkw = _dot(dst, q2)
                dvw = _dot(pt.astype(BF16), do2)
                dkb[low, :] += dkw[:BAND]
                dvb[low, :] += dvw[:BAND]
                dkb[high, :] = dkw[BAND:]
                dvb[high, :] = dvw[BAND:]
                dq2 = _dot_tn(dst, kw)
                dqb[rows, :] = jnp.where(head0, dq2[:BAND], dq2[BAND:])
                return c_

            lax.fori_loop(0, UNITS, unit, 0, unroll=16)

            for r in range(d):
                last = pl.ds(r * pad + per, BAND)
                kept = pl.ds(offs[g] + r * BAND, BAND)

                @pl.when(step > 0)
                def _():
                    dkb[last, :] += ckb[kept, :]
                    dvb[last, :] += cvb[kept, :]

                ckb[kept, :] = dkb[pl.ds(r * pad, BAND), :]
                cvb[kept, :] = dvb[pl.ds(r * pad, BAND), :]
            narrower = DILATIONS[order[pos + 1]] if pos + 1 < nd else None
            for n, (buf, out_ref, stride, at) in enumerate(
                    ((dqb, dq_ref, per, 0), (dkb, dk_ref, pad, BAND), (dvb, dv_ref, pad, BAND))):
                wider, onward = folds[2 * n + pos % 2], folds[2 * n + (pos + 1) % 2]
                for r in range(d):
                    val = buf[pl.ds(r * stride + at, per), :]
                    if pos > 0:
                        val = val + wider[pl.ds(r * per, per), :]
                    if narrower is None:
                        out_ref[...] = val
                    else:
                        start = (r % narrower) * (TOK // narrower) + r // narrower
                        onward[pl.ds(start, per, stride=d // narrower), :] = val

    main, prev = _dilated_specs(nblk, True)
    tok = pl.BlockSpec((TOK, BAND), lambda j, s: (nblk - 1 - s, j))
    acc_rows = max(d * (TOK // d + BAND) for d in DILATIONS)
    kept_rows = sum(DILATIONS) * BAND
    return _call(
        body, list(qs) + list(ks) + list(ks) + list(vs) + list(vs) + [do, lse, dd], name="attn_bwd",
        grid=(PAIRS, nblk), out_shape=[_sds((t, D_ATTN), F32)] * 3,
        in_specs=main + prev + main + prev + main + [tok] * 3, out_specs=[tok] * 3,
        scratch_shapes=[pltpu.VMEM((acc_rows, BAND), BF16)] * 2 + [pltpu.VMEM((TOK, BAND), BF16)] * 2
        + [pltpu.VMEM((TOK, BAND), F32)] * 2 + [pltpu.VMEM((acc_rows, BAND), F32)] * 2
        + [pltpu.VMEM((kept_rows, BAND), F32)] * 2
        + [pltpu.VMEM((TOK, BAND), F32)] * (6 + 2 * (nd - 2)),
        semantics=("parallel", "arbitrary"), carry=carry)


def _halo_rows(tm, t):
    per = tm // 8
    prev = lambda i: (jnp.maximum(i * per - 1, 0), 0)
    nxt = lambda i: (jnp.minimum((i + 1) * per, t // 8 - 1), 0)
    return prev, nxt


def _mixer_out(z, cw, y_attn, g_conv, g_attn, tm, carry=None):
    t = z.shape[0]
    prev, _ = _halo_rows(tm, t)

    def body(z_ref, zp_ref, cw_ref, y_ref, gc_ref, ga_ref, mix_ref):
        i = pl.program_id(0)
        keep = jnp.where(i > 0, 1.0, 0.0)
        cu = jnp.concatenate([zp_ref[:, 0:512] * zp_ref[:, 1024:1536] * keep,
                              z_ref[:, 0:512] * z_ref[:, 1024:1536]], axis=0)
        c = (cw_ref[0:1, :] * pltpu.roll(cu, 2, 0) + cw_ref[1:2, :] * pltpu.roll(cu, 1, 0)
             + cw_ref[2:3, :] * cu)[8:, :]
        yc = z_ref[:, 512:1024] * c
        mix_ref[:, 0:512] = (yc * _rms_scale(yc) * gc_ref[...]).astype(BF16)
        ya = y_ref[...]
        mix_ref[:, 512:1024] = (ya * _rms_scale(ya) * ga_ref[...]).astype(BF16)

    blk = pl.BlockSpec((tm, 512), lambda i: (i, 0))
    vec = pl.BlockSpec((1, 512), lambda i: (0, 0))
    return _call(
        body, [z, z, cw, y_attn, g_conv, g_attn], name="mixer_out", grid=(t // tm,),
        out_shape=_sds((t, 1024), BF16),
        in_specs=[pl.BlockSpec((tm, 1536), lambda i: (i, 0)), pl.BlockSpec((8, 1536), prev),
                  pl.BlockSpec((8, 512), lambda i: (0, 0)), blk, vec, vec],
        out_specs=pl.BlockSpec((tm, 1024), lambda i: (i, 0)),
        semantics=("parallel",), carry=carry)


def _mixer_bwd(z, dx1, wout, y_attn, cw, g_conv, g_attn, ones_bd, tm, carry=None):
    t = z.shape[0]
    nblk = t // tm
    prev, nxt = _halo_rows(tm, t)
    e = tm + 16

    def body(z_ref, zp_ref, zn_ref, dx_ref, dxn_ref, w_ref, y_ref, cw_ref, gc_ref, ga_ref, bd_ref,
             dz_ref, do_ref, dd_ref, dcw_ref, dgc_ref, dga_ref):
        i = pl.program_id(0)
        dm = _dot_nt(dx_ref[...], w_ref[...])
        dmn = _dot_nt(dxn_ref[...], w_ref[0:D_CONV, :])[0:8, :]
        rows = lax.broadcasted_iota(jnp.int32, (e, 1), 0)
        lo = jnp.where(i > 0, 0, 8)
        hi = jnp.where(i < nblk - 1, e, tm + 8)
        ze = jnp.concatenate([zp_ref[...], z_ref[...], zn_ref[...]], axis=0)
        u, gb, gcv = ze[:, 0:512], ze[:, 512:1024], ze[:, 1024:1536]
        w0, w1, w2 = cw_ref[0:1, :], cw_ref[1:2, :], cw_ref[2:3, :]
        cu = jnp.where(rows >= lo, gcv * u, 0.0)
        cu1, cu2 = pltpu.roll(cu, 1, 0), pltpu.roll(cu, 2, 0)
        c = w0 * cu2 + w1 * cu1 + w2 * cu
        yc = gb * c
        dma = jnp.concatenate([jnp.zeros((8, 512), F32), dm[:, 0:512], dmn], axis=0)
        dyc, ych = _rms_bwd(yc, _rms_scale(yc), gc_ref[...], dma)
        dc = jnp.where(jnp.logical_and(rows >= 8, rows < hi), dyc * gb, 0.0)
        dcu = w0 * pltpu.roll(dc, e - 2, 0) + w1 * pltpu.roll(dc, e - 1, 0) + w2 * dc
        mid = slice(8, 8 + tm)
        dz_ref[:, 0:512] = (dcu * gcv)[mid, :].astype(BF16)
        dz_ref[:, 512:1024] = (dyc * c)[mid, :].astype(BF16)
        dz_ref[:, 1024:1536] = (dcu * u)[mid, :].astype(BF16)

        ya = y_ref[...]
        dmb = dm[:, 512:1024]
        dya, yah = _rms_bwd(ya, _rms_scale(ya), ga_ref[...], dmb)
        do_ref[...] = dya
        dd_ref[...] = _head_sum(dya * ya, bd_ref[...])

        @pl.when(i == 0)
        def _():
            dcw_ref[...] = jnp.zeros_like(dcw_ref)
            dgc_ref[...] = jnp.zeros_like(dgc_ref)
            dga_ref[...] = jnp.zeros_like(dga_ref)

        dcm = jnp.where(rows < tm + 8, dc, 0.0)
        dcw_ref[0:1, :] += jnp.sum(dcm * cu2, axis=0, keepdims=True)
        dcw_ref[1:2, :] += jnp.sum(dcm * cu1, axis=0, keepdims=True)
        dcw_ref[2:3, :] += jnp.sum(dcm * cu, axis=0, keepdims=True)
        dgc_ref[...] += jnp.sum((dma * ych)[mid, :], axis=0, keepdims=True)
        dga_ref[...] += jnp.sum(dmb * yah, axis=0, keepdims=True)

    blk = pl.BlockSpec((tm, 512), lambda i: (i, 0))
    vec = pl.BlockSpec((1, 512), lambda i: (0, 0))
    cwb = pl.BlockSpec((8, 512), lambda i: (0, 0))
    next16 = lambda i: (jnp.minimum((i + 1) * (tm // 16), t // 16 - 1), 0)
    return _call(
        body, [z, z, z, dx1, dx1, wout, y_attn, cw, g_conv, g_attn, ones_bd], name="mixer_bwd",
        grid=(nblk,),
        out_shape=[_sds((t, D_IN), BF16), _sds((t, 512), F32), _sds((t, 512), F32),
                   _sds((8, 512), F32), _sds((1, 512), F32), _sds((1, 512), F32)],
        in_specs=[pl.BlockSpec((tm, 1536), lambda i: (i, 0)), pl.BlockSpec((8, 1536), prev),
                  pl.BlockSpec((8, 1536), nxt), pl.BlockSpec((tm, D_MODEL), lambda i: (i, 0)),
                  pl.BlockSpec((16, D_MODEL), next16),
                  pl.BlockSpec(wout.shape, lambda i: (0, 0), pipeline_mode=_resident(True)),
                  blk, cwb, vec, vec, pl.BlockSpec((512, 512), lambda i: (0, 0))],
        out_specs=[pl.BlockSpec((tm, 1536), lambda i: (i, 0)), blk, blk, cwb, vec, vec],
        carry=carry)


def _qkv_bwd(z, dz, dqn, dkn, dv, gq, gk, ones_bd, tm, carry=None):
    t = z.shape[0]

    def body(zq_ref, zk_ref, _, dqn_ref, dkn_ref, dv_ref, gq_ref, gk_ref, bd_ref,
             dz_ref, dgq_ref, dgk_ref):
        bd = bd_ref[...]

        @pl.when(pl.program_id(0) == 0)
        def _():
            dgq_ref[...] = jnp.zeros_like(dgq_ref)
            dgk_ref[...] = jnp.zeros_like(dgk_ref)

        def back(v, dn, g, scale):
            r = _head_rms_scale(v, bd)
            vh = v * r
            dh = dn * (g * scale)
            dv = r * (dh - vh * (_head_sum(dh * vh, bd) * (1.0 / HEAD_DIM)))
            return dv, jnp.sum(dn * scale * vh, axis=0, keepdims=True)

        dq, dgq = back(zq_ref[...], dqn_ref[...], gq_ref[...], HEAD_DIM ** -0.5)
        dk, dgk = back(zk_ref[...], dkn_ref[...], gk_ref[...], 1.0)
        dgq_ref[...] += dgq
        dgk_ref[...] += dgk
        dz_ref[:, 0:512] = dq.astype(BF16)
        dz_ref[:, 512:1024] = dk.astype(BF16)
        dz_ref[:, 1024:1536] = dv_ref[...].astype(BF16)

    blk = pl.BlockSpec((tm, 512), lambda i: (i, 0))
    vec = pl.BlockSpec((1, 512), lambda i: (0, 0))
    return _call(
        body, [z, z, dz, dqn, dkn, dv, gq, gk, ones_bd], name="qkv_bwd", grid=(t // tm,),
        out_shape=[_sds((t, D_IN), BF16), _sds((1, 512), F32), _sds((1, 512), F32)],
        in_specs=[pl.BlockSpec((tm, 512), lambda i: (i, 3)), pl.BlockSpec((tm, 512), lambda i: (i, 4)),
                  ANY] + [blk] * 3 + [vec, vec, pl.BlockSpec((512, 512), lambda i: (0, 0))],
        out_specs=[pl.BlockSpec((tm, 1536), lambda i: (i, 1)), vec, vec],
        carry=carry, aliases={2: 0})


def _columns_from_chips(g):
    return g.transpose(1, 0, 2).reshape(g.shape[1], N_CHIPS * g.shape[2])


def kernel(x, g_mix, w_in, conv_w, g_q, g_k, g_conv_out, g_attn_out, w_out, g_ffn, w_gate, w_up, w_down, loss_target, m_g_mix, m_w_in, m_conv_w, m_g_q, m_g_k, m_g_conv_out, m_g_attn_out, m_w_out, m_g_ffn, m_w_gate, m_w_up, m_w_down, v_g_mix, v_w_in, v_conv_w, v_g_q, v_g_k, v_g_conv_out, v_g_attn_out, v_w_out, v_g_ffn, v_w_gate, v_w_up, v_w_down):
    t = x.shape[1]
    xs = x[0]
    target = loss_target[0]
    tm = min(512, t)
    tm_wide = min(1024, t)
    tmm = min(2048, t)

    cw_pad = jnp.pad(conv_w[0], ((0, 13), (0, 0)))
    gathered = _all_gather([w_in[0].astype(BF16), cw_pad])
    win = _columns_from_chips(gathered[0])
    cw = jnp.pad(gathered[1][:, 0:3, :].transpose(1, 0, 2).reshape(3, D_CONV), ((0, 5), (0, 0)))
    later = [w_out[0].astype(BF16), w_gate[0].T.astype(BF16), w_up[0].T.astype(BF16),
             w_down[0].astype(BF16)]

    head_id = jnp.arange(D_ATTN) // HEAD_DIM
    ones_bd = (head_id[:, None] == head_id[None, :]).astype(BF16)
    gq_t = jnp.tile(g_q, (1, D_ATTN // HEAD_DIM))
    gk_t = jnp.tile(g_k, (1, D_ATTN // HEAD_DIM))

    h1, z, *dilated = _in_proj(xs, g_mix, win, gq_t, gk_t, ones_bd, tm)
    nd = len(DILATIONS)
    qs, ks, vs = dilated[:nd], dilated[nd:2 * nd], dilated[2 * nd:]
    (y_attn, lse), gathered = _attn_fwd(qs, ks, vs, carry=_x_gather_chips(later))
    mix, gathered = _mixer_out(z, cw, y_attn, g_conv_out, g_attn_out, tm,
                               carry=_x_gather_sibling(gathered))
    wout = gathered[0].reshape(D_MODEL, D_MODEL)
    wgate_t = gathered[1].reshape(D_FF, D_MODEL)
    wup_t = gathered[2].reshape(D_FF, D_MODEL)
    wdown = gathered[3].reshape(D_FF, D_MODEL)
    (x1,) = _matmul("out_proj", mix, wout, [xs], [F32], lambda acc, r: (r + acc,), tm, D_MODEL)
    h2, gate, up, act = _norm_matmul("ffn_up", x1, g_ffn, [wgate_t, wup_t], tm, D_FF, True, BF16,
                                     transposed_w=True)

    def loss_epilogue(acc, r, tgt):
        err = r + acc - tgt
        dy = err * (1.0 / D_MODEL)
        return dy, dy, jnp.sum(err * err)

    dx2, dx2b, loss_sum = _matmul("ffn_down_loss", act, wdown, [x1, target], [F32, BF16],
                                  loss_epilogue, tm_wide, D_MODEL, loss=True)

    def swiglu_bwd(da, gt, u):
        gt, u = gt.astype(F32), u.astype(F32)
        s = _sigmoid(gt)
        return da * u * (s * (1.0 + gt * (1.0 - s))), da * (gt * s)

    dgate, dup = _matmul("ffn_down_bwd", dx2b, wdown, [gate, up], [BF16, BF16], swiglu_bwd,
                         tm, D_FF, transposed_w=True)
    gw_down = _matmul_tn("grad_w_down", act, dx2b, 512, tmm)
    gw_gate_t = _matmul_tn("grad_w_gate", dgate, h2, 512, tmm)
    gw_up_t = _matmul_tn("grad_w_up", dup, h2, 512, tmm)

    me = 2 * lax.axis_index("x") + lax.axis_index("y")
    where = jnp.stack([lax.axis_index("c"), me]).astype(jnp.int32)

    def pair_sums(names, full, got):
        return [_pair_sum(f"pair_sum_{nme}", a, b, where) for nme, a, b in zip(names, full, got)]

    def chip_sums(names, pair, got):
        return [_chip_sum(f"chip_sum_{nme}", own, b) for nme, (_, own), b in zip(names, pair, got)]

    ffn = ["w_gate", "w_up", "w_down"]
    full = [g.reshape(N_CHIPS, D_FF // N_CHIPS, D_MODEL) for g in (gw_gate_t, gw_up_t, gw_down)]
    (dx1, dx1b, gg_ffn), got = _matmul_norm_bwd(
        "ffn_up_bwd", [(dgate, wgate_t), (dup, wup_t)], x1, dx2, g_ffn, tm, carry=_x_pair(full),
        transposed_w=False)
    pair = pair_sums(ffn, full, got)
    gw_out = _matmul_tn("grad_w_out", mix, dx1b, 512, tmm)
    full = [gw_out.reshape(N_CHIPS, D_MODEL // N_CHIPS, D_MODEL)]
    (dzc, do, dd, gcw, gg_conv, gg_attn), got = _mixer_bwd(
        z, dx1b, wout, y_attn, cw, g_conv_out, g_attn_out, ones_bd, tm, carry=_x_pair(full))
    pair += pair_sums(["w_out"], full, got)
    early = ffn + ["w_out"]
    (dqn, dkn, dv), got = _attn_bwd(qs, ks, vs, do, lse, dd, carry=_x_chips([p for p, _ in pair]))
    mine = chip_sums(early, pair, got)
    (dz, gg_q, gg_k), theirs = _qkv_bwd(z, dzc, dqn, dkn, dv, gq_t, gk_t, ones_bd, tm,
                                        carry=_x_share(mine))
    full = [_matmul_tn("grad_w_in", h1, dz, D_IN // N_CHIPS, tmm, by_chip=True)]
    got = _exchange_alone("grad_pair_exchange_w_in", _x_pair(full))
    pair = pair_sums(["w_in"], full, got)
    (grad_x, _, gg_mix), got = _matmul_norm_bwd("in_proj_bwd", [(dz, win)], xs, dx1, g_mix, tm_wide,
                                                carry=_x_chips([pair[0][0]]))
    mine += chip_sums(["w_in"], pair, got)
    theirs = list(theirs) + list(_exchange_alone("grad_pair_share_w_in", _x_share(mine[-1:])))
    big = early + ["w_in"]

    small = _small_all_reduce({
        "g_mix": gg_mix, "g_ffn": gg_ffn, "g_conv_out": gg_conv, "g_attn_out": gg_attn,
        "g_q": gg_q, "g_k": gg_k, "loss": loss_sum, "conv_w": gcw})
    heads = D_ATTN // HEAD_DIM
    grads = {
        "g_mix": small[0:1, :], "g_ffn": small[1:2, :],
        "g_conv_out": small[2:3, 0:512], "g_attn_out": small[2:3, 512:1024],
        "g_q": small[3, 0:512].reshape(heads, HEAD_DIM).sum(axis=0)[None, :],
        "g_k": small[3, 512:1024].reshape(heads, HEAD_DIM).sum(axis=0)[None, :],
        "conv_w": lax.dynamic_slice(small[8:11, 0:512], (0, me * (D_CONV // N_CHIPS)),
                                    (3, D_CONV // N_CHIPS)),
    }
    halves = dict(zip(big, zip(mine, theirs)))
    loss = small[4, 0] * 0.5 * (1.0 / D_MODEL)

    weights = dict(g_mix=g_mix, w_in=w_in, conv_w=conv_w, g_q=g_q, g_k=g_k, g_conv_out=g_conv_out,
                   g_attn_out=g_attn_out, w_out=w_out, g_ffn=g_ffn, w_gate=w_gate, w_up=w_up,
                   w_down=w_down)
    moments_m = dict(g_mix=m_g_mix, w_in=m_w_in, conv_w=m_conv_w, g_q=m_g_q, g_k=m_g_k,
                     g_conv_out=m_g_conv_out, g_attn_out=m_g_attn_out, w_out=m_w_out, g_ffn=m_g_ffn,
                     w_gate=m_w_gate, w_up=m_w_up, w_down=m_w_down)
    moments_v = dict(g_mix=v_g_mix, w_in=v_w_in, conv_w=v_conv_w, g_q=v_g_q, g_k=v_g_k,
                     g_conv_out=v_g_conv_out, g_attn_out=v_g_attn_out, w_out=v_w_out, g_ffn=v_g_ffn,
                     w_gate=v_w_gate, w_up=v_w_up, w_down=v_w_down)
    names = list(weights)
    out_g, out_d, out_m, out_v = [], [], [], []
    for nme in names:
        wgt = weights[nme]
        shape2 = wgt.shape[-2:] if wgt.ndim == 3 else wgt.shape
        flip = nme in ("w_gate", "w_up")

        def to2d(a):
            return a.reshape(shape2).T if flip else a.reshape(shape2)

        def back(a):
            return (a.T if flip else a).reshape(wgt.shape)

        state = (to2d(wgt), to2d(moments_m[nme]), to2d(moments_v[nme]))
        if nme in halves:
            g2, dlt, nm, nv = _adamw_shard(f"adamw_{nme}", *state, *halves[nme], where)
        else:
            g2 = grads[nme].reshape(shape2)
            dlt, nm, nv = _adamw(f"adamw_{nme}", state[0], g2, state[1], state[2])
        out_g.append(back(g2))
        out_d.append(back(dlt))
        out_m.append(back(nm))
        out_v.append(back(nv))
    return (loss, grad_x[None], *out_g, *out_d, *out_m, *out_v)
```

```python
import functools
from typing import Any, Callable, NamedTuple, Sequence

import jax
import jax.numpy as jnp
from jax import lax
from jax.experimental import pallas as pl
from jax.experimental.pallas import tpu as pltpu

F32 = jnp.float32
BF16 = jnp.bfloat16
MESH = pl.DeviceIdType.MESH

D_MODEL = 1024
D_CONV = 512
D_ATTN = 512
HEAD_DIM = 64
D_FF = 2816
D_IN = 3 * D_CONV + 3 * D_ATTN
DILATIONS = (1, 4, 16)
BAND = 128
EPS = 1e-6
NEG = -1e30
N_CHIPS = 4

ADAM_LR = 0.001
ADAM_B1 = 0.9
ADAM_B2 = 0.999
ADAM_EPS = 1e-08
ADAM_WD = 0.01
ADAM_STEP = 10

V7X_VMEM_BYTES = 64 * 1024 * 1024
VMEM_LIMIT = V7X_VMEM_BYTES - 8 * 1024 * 1024
ANY = pl.BlockSpec(memory_space=pl.ANY)
VMEM_WHOLE = pl.BlockSpec(memory_space=pltpu.VMEM)


def _params(*sem):
    return pltpu.CompilerParams(dimension_semantics=sem, vmem_limit_bytes=VMEM_LIMIT)


def _sds(shape, dtype):
    return jax.ShapeDtypeStruct(shape, dtype)


def _resident(whole):
    return pl.Buffered(1) if whole else None


def _place():
    x, y, c = lax.axis_index("x"), lax.axis_index("y"), lax.axis_index("c")
    chips = [(1 - x, y), (x, 1 - y), (1 - x, 1 - y)]
    return x, y, c, 2 * x + y, chips, [2 * cx + cy for cx, cy in chips]


def _all_gather(shards):
    n = len(shards)

    def body(*refs):
        ins, outs, stage = refs[:n], refs[n:2 * n], refs[2 * n:3 * n]
        ssem, rsem, fsem, gsem, lsem, osem = refs[3 * n:]
        x, y, c, me, chips, cids = _place()
        sib = (x, y, 1 - c)

        def half(w, which):
            h = shards[w].shape[0] // 2
            return pl.ds(pl.multiple_of(which * h, 8), h)

        loads = [pltpu.make_async_copy(ins[w], stage[w], lsem.at[w]) for w in range(n)]
        local = [pltpu.make_async_copy(stage[w], outs[w].at[me], osem.at[w]) for w in range(n)]
        for cp in loads:
            cp.start()

        def chip_copy(w, j, src_slot):
            rows = half(w, c)
            return pltpu.make_async_remote_copy(
                src_ref=ins[w].at[rows], dst_ref=outs[w].at[src_slot, rows],
                send_sem=ssem.at[3 * w + j], recv_sem=rsem.at[3 * w + j],
                device_id=(*chips[j], c), device_id_type=MESH)

        def sib_copy(w, j, which):
            rows = half(w, which)
            return pltpu.make_async_remote_copy(
                src_ref=outs[w].at[cids[j], rows], dst_ref=outs[w].at[cids[j], rows],
                send_sem=fsem.at[3 * w + j], recv_sem=gsem.at[3 * w + j],
                device_id=sib, device_id_type=MESH)

        sends = [chip_copy(w, j, me) for w in range(n) for j in range(3)]
        for cp in sends:
            cp.start()
        for w in range(n):
            loads[w].wait()
            local[w].start()
        passed = []
        for w in range(n):
            for j in range(3):
                chip_copy(w, j, cids[j]).wait_recv()
                cp = sib_copy(w, j, c)
                cp.start()
                passed.append(cp)
        for w in range(n):
            for j in range(3):
                sib_copy(w, j, 1 - c).wait_recv()
        for cp in sends + passed:
            cp.wait_send()
        for cp in local:
            cp.wait()

    return pl.pallas_call(
        body, name="all_gather_weights",
        out_shape=[_sds((N_CHIPS,) + s.shape, s.dtype) for s in shards],
        in_specs=[ANY] * n, out_specs=[ANY] * n,
        scratch_shapes=[pltpu.VMEM(s.shape, s.dtype) for s in shards]
        + [pltpu.SemaphoreType.DMA((3 * n,))] * 4 + [pltpu.SemaphoreType.DMA((n,))] * 2,
        compiler_params=pltpu.CompilerParams(vmem_limit_bytes=VMEM_LIMIT),
    )(*shards)


class _Exchange(NamedTuple):
    srcs: Sequence[Any]
    lands: Sequence[Any]
    outs: Sequence[Any]
    n_sems: int
    copies: Callable


def _remote(src, dst, ssem, rsem, k, to):
    return pltpu.make_async_remote_copy(src_ref=src, dst_ref=dst, send_sem=ssem.at[k],
                                        recv_sem=rsem.at[k], device_id=to, device_id_type=MESH)


def _x_gather_chips(shards):
    def copies(srcs, lands, outs, ssem, rsem):
        _, _, c, me, chips, cids = _place()
        go, arrive = [], []
        for w, s in enumerate(shards):
            h = s.shape[0] // 2
            rows = pl.ds(pl.multiple_of(c * h, 8), h)
            for j in range(3):
                to = (*chips[j], c)
                go.append(_remote(srcs[w].at[rows], lands[w].at[me, rows], ssem, rsem, 3 * w + j, to))
                arrive.append(_remote(srcs[w].at[rows], lands[w].at[cids[j], rows], ssem, rsem,
                                      3 * w + j, to))
        return go, arrive

    lands = [jnp.broadcast_to(s[None], (N_CHIPS,) + s.shape) for s in shards]
    return _Exchange(shards, lands, [], 3 * len(shards), copies)


def _x_gather_sibling(gathered):
    def copies(srcs, lands, outs, ssem, rsem):
        x, y, c, _, _, cids = _place()
        go, arrive = [], []
        for w, g in enumerate(gathered):
            h = g.shape[1] // 2
            mine = pl.ds(pl.multiple_of(c * h, 8), h)
            theirs = pl.ds(pl.multiple_of((1 - c) * h, 8), h)
            for j in range(3):
                slab = lands[w].at[cids[j]]
                go.append(_remote(slab.at[mine], slab.at[mine], ssem, rsem, 3 * w + j, (x, y, 1 - c)))
                arrive.append(_remote(slab.at[theirs], slab.at[theirs], ssem, rsem, 3 * w + j,
                                      (x, y, 1 - c)))
        return go, arrive

    return _Exchange([], gathered, [], 3 * len(gathered), copies)


def _x_pair(grads):
    def copies(srcs, lands, outs, ssem, rsem):
        x, y, c, _, _, _ = _place()
        go = []
        for w, g in enumerate(grads):
            h = g.shape[1] // 2
            theirs = pl.ds(pl.multiple_of((1 - c) * h, 8), h)
            go.append(_remote(srcs[w].at[:, theirs, :], outs[w], ssem, rsem, w, (x, y, 1 - c)))
        return go, go

    outs = [_sds((N_CHIPS, g.shape[1] // 2, g.shape[2]), g.dtype) for g in grads]
    return _Exchange(grads, [], outs, len(grads), copies)


def _x_chips(parts):
    def copies(srcs, lands, outs, ssem, rsem):
        _, _, c, _, chips, cids = _place()
        go = [_remote(srcs[w].at[cids[j]], outs[w].at[j], ssem, rsem, 3 * w + j, (*chips[j], c))
              for w in range(len(parts)) for j in range(3)]
        return go, go

    outs = [_sds((3,) + p.shape[1:], p.dtype) for p in parts]
    return _Exchange(parts, [], outs, 3 * len(parts), copies)


def _x_share(halves):
    def copies(srcs, lands, outs, ssem, rsem):
        x, y, c, _, _, _ = _place()
        go = [_remote(srcs[w], outs[w], ssem, rsem, w, (x, y, 1 - c)) for w in range(len(halves))]
        return go, go

    return _Exchange(halves, [], [_sds(h.shape, h.dtype) for h in halves], len(halves), copies)


def _call(body, args, *, name, grid, in_specs, out_specs, out_shape, scratch_shapes=(),
          semantics=None, carry=None, aliases=None):
    single = not isinstance(out_shape, (list, tuple))
    out_shape = [out_shape] if single else list(out_shape)
    out_specs = [out_specs] if single else list(out_specs)
    aliases = dict(aliases or {})
    if carry is None:
        res = pl.pallas_call(
            body, name=name, grid=grid, in_specs=list(in_specs), out_specs=out_specs,
            out_shape=out_shape, scratch_shapes=list(scratch_shapes), input_output_aliases=aliases,
            compiler_params=_params(*(semantics or ("arbitrary",) * len(grid))))(*args)
        return res[0] if single else res
    n_in, n_out, n_scr = len(args), len(out_shape), len(scratch_shapes)
    n_src, n_land, n_new = len(carry.srcs), len(carry.lands), len(carry.outs)

    def carrying(*refs):
        at = 0
        parts = []
        for n in (n_in, n_src, n_land, n_out, n_land, n_new, n_scr, 2):
            parts.append(refs[at:at + n])
            at += n
        ins, srcs, _, outs, lands, news, scratch, (ssem, rsem) = parts
        ids = [pl.program_id(a) for a in range(len(grid))]
        first = functools.reduce(jnp.logical_and, [i == 0 for i in ids])
        last = functools.reduce(jnp.logical_and, [i == g - 1 for i, g in zip(ids, grid)])
        go, arrive = carry.copies(srcs, lands, news, ssem, rsem)

        @pl.when(first)
        def _():
            for cp in go:
                cp.start()

        body(*ins, *outs, *scratch)

        @pl.when(last)
        def _():
            for cp in go:
                cp.wait_send()
            for cp in arrive:
                cp.wait_recv()

    res = pl.pallas_call(
        carrying, name=name, grid=grid,
        in_specs=list(in_specs) + [ANY] * (n_src + n_land),
        out_specs=out_specs + [ANY] * (n_land + n_new),
        out_shape=out_shape + [_sds(a.shape, a.dtype) for a in carry.lands] + list(carry.outs),
        input_output_aliases={**aliases, **{n_in + n_src + i: n_out + i for i in range(n_land)}},
        scratch_shapes=list(scratch_shapes) + [pltpu.SemaphoreType.DMA((carry.n_sems,))] * 2,
        compiler_params=_params(*(("arbitrary",) * len(grid))))(*args, *carry.srcs, *carry.lands)
    own = res[:n_out]
    return (own[0] if single else own), res[n_out:]


def _exchange_alone(name, exchange):
    def body(x_ref, o_ref):
        o_ref[...] = x_ref[...]

    blk = pl.BlockSpec((8, 128), lambda i: (0, 0))
    _, res = _call(body, [jnp.zeros((8, 128), F32)], name=name, grid=(1,), in_specs=[blk],
                   out_specs=blk, out_shape=_sds((8, 128), F32), carry=exchange)
    return res


def _row_block(r, want):
    return max(d for d in range(1, min(want, r) + 1) if r % d == 0 and (d % 8 == 0 or d == r))


def _pair_sum(name, full, got, where):
    _, r, n = full.shape
    h = r // 2
    tr = _row_block(h, 256)
    nb = h // tr

    def body(w_ref, a_ref, b_ref, o_ref, own_ref):
        total = a_ref[...] + b_ref[...]
        o_ref[...] = total.astype(BF16)

        @pl.when(pl.program_id(1) == w_ref[1])
        def _():
            own_ref[...] = total[0]

    blk = pl.BlockSpec((1, tr, n), lambda i, s, w: (s, i, 0))
    return pl.pallas_call(
        body, name=name, out_shape=[_sds(got.shape, BF16), _sds((h, n), F32)],
        grid_spec=pltpu.PrefetchScalarGridSpec(
            num_scalar_prefetch=1, grid=(nb, N_CHIPS),
            in_specs=[pl.BlockSpec((1, tr, n), lambda i, s, w: (s, w[0] * nb + i, 0)), blk],
            out_specs=[blk, pl.BlockSpec((tr, n), lambda i, s, w: (i, 0))]),
        compiler_params=_params("parallel", "arbitrary"),
    )(where, full, got)


def _chip_sum(name, own, got):
    h, n = own.shape
    tr = _row_block(h, 256)

    def body(a_ref, b0, b1, b2, o_ref):
        o_ref[...] = ((a_ref[...] + b0[0].astype(F32)) + b1[0].astype(F32)) + b2[0].astype(F32)

    def slot(j):
        return pl.BlockSpec((1, tr, n), lambda i: (j, i, 0))

    blk = pl.BlockSpec((tr, n), lambda i: (i, 0))
    return pl.pallas_call(
        body, name=name, grid=(h // tr,), out_shape=_sds((h, n), F32),
        in_specs=[blk, slot(0), slot(1), slot(2)], out_specs=blk,
        compiler_params=_params("parallel"),
    )(own, got, got, got)


SMALL_ROWS = 16
SMALL_LAYOUT = (
    ("g_mix", 0, 0, 1, 1024), ("g_ffn", 1, 0, 1, 1024), ("g_conv_out", 2, 0, 1, 512),
    ("g_attn_out", 2, 512, 1, 512), ("g_q", 3, 0, 1, 512), ("g_k", 3, 512, 1, 512),
    ("loss", 4, 0, 1, 128), ("conv_w", 8, 0, 8, 512))


def _small_all_reduce(parts):
    names = [s[0] for s in SMALL_LAYOUT]

    def body(*refs):
        ins = refs[:len(names)]
        out_ref, stage, buf, ssem, rsem = refs[len(names):]
        x, y, c, _, _, _ = _place()
        me = 4 * x + 2 * y + c
        stage[...] = jnp.zeros_like(stage)
        for ref, (_, r0, c0, nr, nc) in zip(ins, SMALL_LAYOUT):
            stage[r0:r0 + nr, c0:c0 + nc] = ref[0:nr, :]
        buf[me] = stage[...]
        peers = []
        for d in range(1, 8):
            px = 1 - x if d & 4 else x
            py = 1 - y if d & 2 else y
            pc = 1 - c if d & 1 else c
            peers.append(((px, py, pc), 4 * px + 2 * py + pc))
        sends = [pltpu.make_async_remote_copy(
            src_ref=stage, dst_ref=buf.at[me], send_sem=ssem.at[k], recv_sem=rsem.at[k],
            device_id=peer, device_id_type=MESH) for k, (peer, _) in enumerate(peers)]
        for cp in sends:
            cp.start()
        for k, (peer, pid) in enumerate(peers):
            pltpu.make_async_remote_copy(
                src_ref=stage, dst_ref=buf.at[pid], send_sem=ssem.at[k], recv_sem=rsem.at[k],
                device_id=peer, device_id_type=MESH).wait_recv()
        for cp in sends:
            cp.wait_send()
        acc = buf[0]
        for k in range(1, 8):
            acc = acc + buf[k]
        out_ref[...] = acc

    return pl.pallas_call(
        body, name="small_all_reduce", out_shape=_sds((SMALL_ROWS, 1024), F32),
        in_specs=[VMEM_WHOLE] * len(names), out_specs=VMEM_WHOLE,
        scratch_shapes=[pltpu.VMEM((SMALL_ROWS, 1024), F32), pltpu.VMEM((8, SMALL_ROWS, 1024), F32),
                        pltpu.SemaphoreType.DMA((7,)), pltpu.SemaphoreType.DMA((7,))],
    )(*[parts[k] for k in names])


def _dot(a, b):
    return jnp.dot(a, b, preferred_element_type=F32)


def _dot_nt(a, b):
    return lax.dot_general(a, b, (((1,), (1,)), ((), ())), preferred_element_type=F32)


def _dot_tn(a, b):
    return lax.dot_general(a, b, (((0,), (0,)), ((), ())), preferred_element_type=F32)


def _sigmoid(v):
    return 1.0 / (1.0 + jnp.exp(-v))


def _rms_scale(v):
    return lax.rsqrt(jnp.mean(v * v, axis=-1, keepdims=True) + EPS)


def _rms_bwd(v, r, g, dy):
    vh = v * r
    dh = dy * g
    return r * (dh - vh * jnp.mean(dh * vh, axis=-1, keepdims=True)), vh


def _head_sum(a, ones_bd):
    hi = a.astype(BF16)
    lo = (a - hi.astype(F32)).astype(BF16)
    return _dot(hi, ones_bd) + _dot(lo, ones_bd)


def _head_rms_scale(v, ones_bd):
    return lax.rsqrt(_head_sum(v * v, ones_bd) * (1.0 / HEAD_DIM) + EPS)


MXU_COLUMNS = 256


def _column_chunks(n):
    width = MXU_COLUMNS if n % MXU_COLUMNS == 0 else n
    return [slice(c, c + width) for c in range(0, n, width)]


def _norm_matmul(name, x, g, ws, tm, tn, swiglu, out_dtype=F32, transposed_w=False):
    t, d = x.shape
    n = ws[0].shape[0] if transposed_w else ws[0].shape[1]
    nw = len(ws)

    def body(x_ref, g_ref, *refs):
        w_refs, h_ref, o_refs = refs[:nw], refs[nw], refs[nw + 1:2 * nw + 1]
        hs = refs[-1]

        @pl.when(pl.program_id(1) == 0)
        def _():
            xv = x_ref[...]
            h = (xv * _rms_scale(xv) * g_ref[...]).astype(BF16)
            hs[...] = h
            h_ref[...] = h

        h = hs[...]
        for cols in _column_chunks(tn):
            outs = [_dot_nt(h, w[cols, :]) if transposed_w else _dot(h, w[:, cols]) for w in w_refs]
            for o_ref, o in zip(o_refs, outs):
                o_ref[:, cols] = o.astype(out_dtype)
            if swiglu:
                refs[2 * nw + 1][:, cols] = (outs[0] * _sigmoid(outs[0]) * outs[1]).astype(BF16)

    row = pl.BlockSpec((tm, d), lambda i, j: (i, 0))
    col = pl.BlockSpec((tm, tn), lambda i, j: (i, j))
    out_shape = [_sds((t, d), BF16)] + [_sds((t, n), out_dtype)] * nw
    out_specs = [row] + [col] * nw
    if swiglu:
        out_shape.append(_sds((t, n), BF16))
        out_specs.append(col)
    return pl.pallas_call(
        body, name=name, grid=(t // tm, n // tn), out_shape=out_shape,
        in_specs=[row, pl.BlockSpec((1, d), lambda i, j: (0, 0))]
        + [pl.BlockSpec((tn, d), lambda i, j: (j, 0), pipeline_mode=_resident(tn == n))
           if transposed_w
           else pl.BlockSpec((d, tn), lambda i, j: (0, j), pipeline_mode=_resident(tn == n))] * nw,
        out_specs=out_specs, scratch_shapes=[pltpu.VMEM((tm, d), BF16)],
        compiler_params=_params("parallel", "arbitrary"),
    )(x, g, *ws)


def _matmul(name, a, w, extras, out_dtypes, epilogue, tm, tn, transposed_w=False, loss=False):
    t, k = a.shape
    n = w.shape[0] if transposed_w else w.shape[1]
    ne, no = len(extras), len(out_dtypes)

    def body(a_ref, w_ref, *refs):
        e_refs, o_refs = refs[:ne], refs[ne:]
        a = a_ref[...]
        total = 0.0
        for cols in _column_chunks(tn):
            acc = _dot_nt(a, w_ref[cols, :]) if transposed_w else _dot(a, w_ref[:, cols])
            res = epilogue(acc, *[e[:, cols] for e in e_refs])
            for o_ref, r in zip(o_refs[:no], res[:no]):
                o_ref[:, cols] = r.astype(o_ref.dtype)
            if loss:
                total = total + res[no]
        if loss:
            first = jnp.logical_and(pl.program_id(0) == 0, pl.program_id(1) == 0)

            @pl.when(first)
            def _():
                o_refs[no][...] = jnp.zeros_like(o_refs[no])

            o_refs[no][...] += total

    col = pl.BlockSpec((tm, tn), lambda i, j: (i, j))
    w_spec = (pl.BlockSpec((tn, k), lambda i, j: (j, 0), pipeline_mode=_resident(tn == n))
              if transposed_w
              else pl.BlockSpec((k, tn), lambda i, j: (0, j), pipeline_mode=_resident(tn == n)))
    out_shape = [_sds((t, n), dt) for dt in out_dtypes]
    out_specs = [col] * no
    if loss:
        out_shape.append(_sds((8, 128), F32))
        out_specs.append(pl.BlockSpec((8, 128), lambda i, j: (0, 0)))
    return pl.pallas_call(
        body, name=name, grid=(t // tm, n // tn), out_shape=out_shape,
        in_specs=[pl.BlockSpec((tm, k), lambda i, j: (i, 0)), w_spec] + [col] * ne,
        out_specs=out_specs,
        compiler_params=_params(*(("arbitrary", "arbitrary") if loss else ("parallel", "parallel"))),
    )(a, w, *extras)


def _matmul_norm_bwd(name, pairs, x, dres, g, tm, carry=None, transposed_w=True, blocks=None,
                     into=()):
    t, d = x.shape
    npairs = len(pairs)
    product = _dot_nt if transposed_w else _dot
    first, count = blocks or (0, t // tm)

    def body(*refs):
        a_refs, w_refs = refs[:npairs], refs[npairs:2 * npairs]
        x_ref, r_ref, g_ref = refs[2 * npairs:2 * npairs + 3]
        dx_ref, dxb_ref, dg_ref = refs[-3:]
        dy = product(a_refs[0][...], w_refs[0][...])
        for a_ref, w_ref in zip(a_refs[1:], w_refs[1:]):
            dy = dy + product(a_ref[...], w_ref[...])
        xv = x_ref[...]
        dx, xh = _rms_bwd(xv, _rms_scale(xv), g_ref[...], dy)
        dx = dx + r_ref[...]
        dx_ref[...] = dx
        dxb_ref[...] = dx.astype(BF16)

        @pl.when(pl.program_id(0) == 0)
        def _():
            dg_ref[...] = jnp.zeros_like(dg_ref)

        dg_ref[...] += jnp.sum(dy * xh, axis=0, keepdims=True)

    row = pl.BlockSpec((tm, d), lambda i: (i + first, 0))
    vec = pl.BlockSpec((1, d), lambda i: (0, 0))
    args = [a for a, _ in pairs] + [w for _, w in pairs] + [x, dres, g]
    return _call(
        body, args + list(into), name=name,
        grid=(count,), out_shape=[_sds((t, d), F32), _sds((t, d), BF16), _sds((1, d), F32)],
        in_specs=[pl.BlockSpec((tm, a.shape[1]), lambda i: (i + first, 0)) for a, _ in pairs]
        + [pl.BlockSpec(w.shape, lambda i: (0, 0), pipeline_mode=pl.Buffered(1)) for _, w in pairs]
        + [row, row, vec] + [ANY] * len(into),
        out_specs=[row, row, vec], carry=carry,
        aliases={len(args) + k: k for k in range(len(into))})


def _matmul_tn(name, a, g, tn, tk, by_chip=False):
    t, ka = a.shape
    n = g.shape[1]

    def body(a_ref, g_ref, o_ref):
        @pl.when(pl.program_id(1) == 0)
        def _():
            o_ref[...] = jnp.zeros_like(o_ref)

        acc = _dot_tn(a_ref[...], g_ref[...])
        o_ref[...] += acc[None] if by_chip else acc

    return pl.pallas_call(
        body, name=name, grid=(n // tn, t // tk),
        out_shape=_sds((n // tn, ka, tn) if by_chip else (ka, n), F32),
        in_specs=[pl.BlockSpec((tk, ka), lambda j, s: (s, 0)),
                  pl.BlockSpec((tk, tn), lambda j, s: (s, j))],
        out_specs=(pl.BlockSpec((1, ka, tn), lambda j, s: (j, 0, 0)) if by_chip
                   else pl.BlockSpec((ka, tn), lambda j, s: (0, j))),
        compiler_params=_params("parallel", "arbitrary"),
    )(a, g)


def _elementwise(name, fn, ins, out_dtypes, tr):
    r, n = ins[0].shape
    tr = _row_block(r, tr)
    ni = len(ins)

    def body(*refs):
        res = fn(*[ref[...] for ref in refs[:ni]])
        for o_ref, v in zip(refs[ni:], res):
            o_ref[...] = v.astype(o_ref.dtype)

    blk = pl.BlockSpec((tr, n), lambda i: (i, 0))
    return pl.pallas_call(
        body, name=name, grid=(r // tr,), out_shape=[_sds((r, n), dt) for dt in out_dtypes],
        in_specs=[blk] * ni, out_specs=[blk] * len(out_dtypes),
        compiler_params=_params("parallel"),
    )(*ins)


def _adamw_update(w, g, m, v):
    m = ADAM_B1 * m + (1.0 - ADAM_B1) * g
    v = ADAM_B2 * v + (1.0 - ADAM_B2) * (g * g)
    m_hat = m / (1.0 - ADAM_B1 ** ADAM_STEP)
    v_hat = v / (1.0 - ADAM_B2 ** ADAM_STEP)
    return -ADAM_LR * (m_hat / (jnp.sqrt(v_hat) + ADAM_EPS) + ADAM_WD * w), m, v


def _adamw(name, w, g, m, v):
    return _elementwise(name, _adamw_update, [w, g, m, v], [F32] * 3, 256)


def _adamw_shard(name, w, m, v, mine, theirs, where):
    r, n = w.shape
    h = r // 2
    tr = _row_block(h, 256)
    nb = h // tr

    def body(w_ref, p_ref, m_ref, v_ref, a_ref, b_ref, g_ref, d_ref, nm_ref, nv_ref):
        g = jnp.where(pl.program_id(0) == w_ref[0], a_ref[...], b_ref[...])
        g_ref[...] = g
        d_ref[...], nm_ref[...], nv_ref[...] = _adamw_update(p_ref[...], g, m_ref[...], v_ref[...])

    whole = pl.BlockSpec((tr, n), lambda s, i, c: (s * nb + i, 0))
    used = pl.BlockSpec((tr, n), lambda s, i, c: (jnp.where(s == c[0], i, 0), 0))
    unused = pl.BlockSpec((tr, n), lambda s, i, c: (jnp.where(s == c[0], 0, i), 0))
    return pl.pallas_call(
        body, name=name, out_shape=[_sds((r, n), F32)] * 4,
        grid_spec=pltpu.PrefetchScalarGridSpec(
            num_scalar_prefetch=1, grid=(2, nb), in_specs=[whole] * 3 + [used, unused],
            out_specs=[whole] * 4),
        compiler_params=_params("arbitrary", "arbitrary"),
    )(where, w, m, v, mine, theirs)


PAIRS = D_ATTN // BAND


def _in_proj(x, g, w, gq, gk, ones_bd, tm):
    t, dm = x.shape
    n = w.shape[1]
    nd = len(DILATIONS)
    first = 3 * D_CONV

    def body(x_ref, g_ref, w_ref, gq_ref, gk_ref, bd_ref, h_ref, z_ref, *refs):
        outs, slabs = refs[:3 * nd], refs[3 * nd:]
        xv = x_ref[...]
        h = (xv * _rms_scale(xv) * g_ref[...]).astype(BF16)
        h_ref[...] = h
        for cols in _column_chunks(n):
            z_ref[:, cols] = _dot(h, w_ref[:, cols])
        bd = bd_ref[...]
        q = z_ref[:, first:first + D_ATTN]
        k = z_ref[:, first + D_ATTN:first + 2 * D_ATTN]
        vals = [(q * _head_rms_scale(q, bd) * gq_ref[...]) * HEAD_DIM ** -0.5,
                k * _head_rms_scale(k, bd) * gk_ref[...], z_ref[:, first + 2 * D_ATTN:n]]
        for m, val in enumerate(vals):
            for c in range(PAIRS):
                slabs[0][c] = val[:, c * BAND:(c + 1) * BAND]
            cur, before = 0, 1
            for a, d in enumerate(DILATIONS):
                o_ref, src, dst = outs[m * nd + a], slabs[cur], slabs[1 - cur]
                step, count = d // before, tm // d
                keep = step > 1 and a + 1 < nd
                for c in range(PAIRS):
                    for r in range(d):
                        start = (r % before) * (tm // before) + r // before
                        rows = src.at[c][pl.ds(start, count, stride=step), :] if step > 1 else src[c]
                        o_ref[c, r] = rows.astype(BF16)
                        if keep:
                            dst.at[c][pl.ds(r * count, count), :] = rows
                if keep:
                    cur = 1 - cur
                before = d

    row = pl.BlockSpec((tm, dm), lambda i: (i, 0))
    vec = pl.BlockSpec((1, D_ATTN), lambda i: (0, 0))
    return pl.pallas_call(
        body, name="in_proj", grid=(t // tm,),
        out_shape=[_sds((t, dm), BF16), _sds((t, n), F32)]
        + [_sds((PAIRS, d, t // d, BAND), BF16) for _ in range(3) for d in DILATIONS],
        in_specs=[row, pl.BlockSpec((1, dm), lambda i: (0, 0)),
                  pl.BlockSpec((dm, n), lambda i: (0, 0), pipeline_mode=_resident(True)), vec, vec,
                  pl.BlockSpec((D_ATTN, D_ATTN), lambda i: (0, 0), pipeline_mode=_resident(True))],
        out_specs=[row, pl.BlockSpec((tm, n), lambda i: (i, 0))]
        + [pl.BlockSpec((PAIRS, d, tm // d, BAND), lambda i: (0, 0, i, 0))
           for _ in range(3) for d in DILATIONS],
        scratch_shapes=[pltpu.VMEM((PAIRS, tm, BAND), F32)] * 2,
        compiler_params=_params("parallel"),
    )(x, g, w, gq, gk, ones_bd)


TOK = 2048
UNITS = TOK // BAND


def _stack_masks():
    row = lax.broadcasted_iota(jnp.int32, (2 * BAND, 2 * BAND), 0) & (BAND - 1)
    col = lax.broadcasted_iota(jnp.int32, (2 * BAND, 2 * BAND), 1)
    lane = lax.broadcasted_iota(jnp.int32, (BAND, BAND), 1)
    head0 = lane < HEAD_DIM
    ones = [jnp.where(head0, 1.0, 0.0).astype(BF16), jnp.where(head0, 0.0, 1.0).astype(BF16)]
    return col - row, col, head0, ones


def _split3(x):
    hi = x.astype(BF16).astype(F32)
    mid = (x - hi).astype(BF16).astype(F32)
    return hi, mid, x - hi - mid


def _gather(srcs, dst, d, before=1):
    per, step, span = TOK // d, d // before, TOK // before
    at = 0
    for r in range(d):
        start = (r % before) * span + r // before
        for src in srcs:
            rows = src[pl.ds(start, per, stride=step), :] if step > 1 else src[pl.ds(start, per), :]
            dst[pl.ds(at, per), :] = rows.astype(dst.dtype)
            at += per


def _scatter(out_ref, src, d):
    per = TOK // d
    if d == 1:
        out_ref[...] = src[...]
        return
    for r in range(d):
        out_ref[pl.ds(r, per, stride=d), :] = src[pl.ds(r * per, per), :]


def _dilated_specs(nblk, reverse):
    def at(s):
        return (nblk - 1 - s) if reverse else s
    main = [pl.BlockSpec((1, d, TOK // d, BAND), lambda j, s: (j, 0, at(s), 0)) for d in DILATIONS]
    prev = [pl.BlockSpec((1, d, TOK // d, BAND), lambda j, s: (j, 0, jnp.maximum(at(s) - 1, 0), 0))
            for d in DILATIONS]
    return main, prev


def _window_rows(prev_ref, main_ref, dst, d):
    per = TOK // d
    for r in range(d):
        dst[pl.ds(r * (per + BAND), BAND), :] = prev_ref[0, r, pl.ds(per - BAND, BAND), :]
        dst[pl.ds(r * (per + BAND) + BAND, per), :] = main_ref[0, r]


def _attn_fwd(qs, ks, vs, carry=None):
    t = qs[0].shape[2]
    nblk = t // TOK
    nd = len(DILATIONS)

    def body(*refs):
        q_refs, kp_refs, k_refs = refs[:nd], refs[nd:2 * nd], refs[2 * nd:3 * nd]
        vp_refs, v_refs = refs[3 * nd:4 * nd], refs[4 * nd:5 * nd]
        y_ref, l_ref, kw_s, vw_s, ob, lb, on, ln = refs[5 * nd:]
        i = pl.program_id(1)
        diff, col, head0, hm = _stack_masks()
        band_ok = jnp.logical_and(diff >= 0, diff <= BAND)
        for g, d in enumerate(DILATIONS):
            per = TOK // d
            nb = per // BAND
            pad = per + BAND
            _window_rows(kp_refs[g], k_refs[g], kw_s, d)
            _window_rows(vp_refs[g], v_refs[g], vw_s, d)
            q_ref = q_refs[g]

            def unit(u, carry):
                r, b = u // nb, u % nb
                qu = q_ref[0, r, pl.ds(pl.multiple_of(b * BAND, BAND), BAND), :]
                start = pl.multiple_of(r * pad + b * BAND, BAND)
                kw = kw_s[pl.ds(start, 2 * BAND), :]
                vw = vw_s[pl.ds(start, 2 * BAND), :]
                lo = jnp.where(jnp.logical_and(i == 0, b == 0), BAND, 0)
                s = _dot_nt(jnp.concatenate([qu * hm[0], qu * hm[1]], axis=0), kw)
                s = jnp.where(jnp.logical_and(band_ok, col >= lo), s, NEG)
                mx = jnp.max(s, axis=-1, keepdims=True)
                e = jnp.exp(s - mx)
                den = jnp.sum(e, axis=-1, keepdims=True)
                o2 = _dot(e.astype(BF16), vw) / den
                l2 = jnp.broadcast_to(mx + jnp.log(den), (2 * BAND, BAND))
                rows = pl.ds(pl.multiple_of(u * BAND, BAND), BAND)
                ob[rows, :] = jnp.where(head0, o2[:BAND], o2[BAND:])
                lb[rows, :] = jnp.where(head0, l2[:BAND], l2[BAND:])
                return carry

            lax.fori_loop(0, UNITS, unit, 0, unroll=16)
            _scatter(on.at[g], ob, d)
            _scatter(ln.at[g], lb, d)
        ls = [ln[0], ln[1], ln[2]]
        mx = jnp.maximum(jnp.maximum(ls[0], ls[1]), ls[2])
        es = [jnp.exp(l - mx) for l in ls]
        tot = es[0] + es[1] + es[2]
        y_ref[...] = (es[0] * on[0] + es[1] * on[1] + es[2] * on[2]) / tot
        l_ref[...] = mx + jnp.log(tot)

    main, prev = _dilated_specs(nblk, False)
    out = pl.BlockSpec((TOK, BAND), lambda j, i: (i, j))
    win_rows = max(d * (TOK // d + BAND) for d in DILATIONS)
    return _call(
        body, list(qs) + list(ks) + list(ks) + list(vs) + list(vs), name="attn_fwd",
        grid=(PAIRS, nblk), out_shape=[_sds((t, D_ATTN), F32)] * 2,
        in_specs=main + prev + main + prev + main, out_specs=[out, out],
        scratch_shapes=[pltpu.VMEM((win_rows, BAND), BF16)] * 2 + [pltpu.VMEM((TOK, BAND), F32)] * 2
        + [pltpu.VMEM((nd, TOK, BAND), F32)] * 2,
        semantics=("parallel", "parallel"), carry=carry)


def _attn_bwd(qs, ks, vs, do, lse, dd, carry=None):
    t = qs[0].shape[2]
    nblk = t // TOK
    nd = len(DILATIONS)
    offs = [sum(DILATIONS[:g]) * BAND for g in range(nd)]

    def body(*refs):
        q_refs, kp_refs, k_refs = refs[:nd], refs[nd:2 * nd], refs[2 * nd:3 * nd]
        vp_refs, v_refs = refs[3 * nd:4 * nd], refs[4 * nd:5 * nd]
        (do_ref, l_ref, d_ref, dq_ref, dk_ref, dv_ref, kw_s, vw_s, dos, lds, pn, dqb, dkb, dvb, ckb,
         cvb, *more) = refs[5 * nd:]
        folds, mids = more[:6], more[6:]
        step = pl.program_id(1)
        i = nblk - 1 - step
        key = lax.broadcasted_iota(jnp.int32, (2 * BAND, 2 * BAND), 0)
        qry = lax.broadcasted_iota(jnp.int32, (2 * BAND, 2 * BAND), 1) & (BAND - 1)
        off = key - qry
        band_ok = jnp.logical_and(off >= 0, off <= BAND)
        lane = lax.broadcasted_iota(jnp.int32, (BAND, BAND), 1)
        head0 = lane < HEAD_DIM
        hm = [jnp.where(head0, 1.0, 0.0).astype(BF16), jnp.where(head0, 0.0, 1.0).astype(BF16)]
        lane2 = lax.broadcasted_iota(jnp.int32, (2 * BAND, BAND), 1) & (HEAD_DIM - 1)
        ones_l = jnp.where(lane2 < 3, 1.0, 0.0).astype(BF16)
        ones_d = jnp.where(jnp.logical_and(lane2 >= 3, lane2 < 6), 1.0, 0.0).astype(BF16)
        piece = lax.broadcasted_iota(jnp.int32, (TOK, BAND), 1) & (HEAD_DIM - 1)

        def pieces(x, at):
            hi, mid, lo = _split3(-x)
            return jnp.where(piece == at, hi,
                             jnp.where(piece == at + 1, mid, jnp.where(piece == at + 2, lo, 0.0)))

        pn[...] = pieces(l_ref[...], 0) + pieces(d_ref[...], 3)
        order = sorted(range(nd), key=lambda a: -DILATIONS[a])
        assert DILATIONS[order[-1]] == 1
        levels = {1: (do_ref, pn)}
        for n, a in enumerate(reversed(order[1:-1])):
            d, before = DILATIONS[a], DILATIONS[order[-1 - n]]
            levels[d] = (mids[2 * n], mids[2 * n + 1])
            for src, dst in zip(levels[before], levels[d]):
                _gather([src], dst, d, before)
        for pos, g in enumerate(order):
            d = DILATIONS[g]
            per = TOK // d
            nb = per // BAND
            pad = per + BAND
            _window_rows(kp_refs[g], k_refs[g], kw_s, d)
            _window_rows(vp_refs[g], v_refs[g], vw_s, d)
            known = d if d in levels else DILATIONS[order[pos + 1]]
            _gather([levels[known][0]], dos, d, known)
            _gather([levels[known][1]], lds, d, known)
            for r in range(d):
                spare = pl.ds(r * pad, BAND)
                dkb[spare, :] = jnp.zeros((BAND, BAND), F32)
                dvb[spare, :] = jnp.zeros((BAND, BAND), F32)
            q_ref = q_refs[g]

            def unit(u, c_):
                r, b = u // nb, u % nb
                rows = pl.ds(pl.multiple_of(u * BAND, BAND), BAND)
                qu = q_ref[0, r, pl.ds(pl.multiple_of(b * BAND, BAND), BAND), :]
                dou, ldu = dos[rows, :], lds[rows, :]
                q2 = jnp.concatenate([qu * hm[0], qu * hm[1]], axis=0)
                do2 = jnp.concatenate([dou * hm[0], dou * hm[1]], axis=0)
                ld2 = jnp.concatenate([ldu * hm[0], ldu * hm[1]], axis=0)
                acc = pl.ds(pl.multiple_of(r * pad + b * BAND, BAND), 2 * BAND)
                kw = kw_s[acc, :]
                vw = vw_s[acc, :]
                lo = jnp.where(jnp.logical_and(i == 0, b == 0), BAND, 0)
                ok = jnp.logical_and(band_ok, key >= lo)
                st = _dot_nt(jnp.concatenate([kw, ones_l], axis=1), jnp.concatenate([q2, ld2], axis=1))
                dpt = _dot_nt(jnp.concatenate([vw, ones_d], axis=1), jnp.concatenate([do2, ld2], axis=1))
                pt = jnp.where(ok, jnp.exp(st), 0.0)
                dst = (pt * dpt).astype(BF16)
                low = pl.ds(pl.multiple_of(r * pad + b * BAND, BAND), BAND)
                high = pl.ds(pl.multiple_of(r * pad + (b + 1) * BAND, BAND), BAND)
                dkw = _dot(dst, q2)
                dvw = _dot(pt.astype(BF16), do2)
                dkb[low, :] += dkw[:BAND]
                dvb[low, :] += dvw[:BAND]
                dkb[high, :] = dkw[BAND:]
                dvb[high, :] = dvw[BAND:]
                dq2 = _dot_tn(dst, kw)
                dqb[rows, :] = jnp.where(head0, dq2[:BAND], dq2[BAND:])
                return c_

            lax.fori_loop(0, UNITS, unit, 0, unroll=16)

            for r in range(d):
                last = pl.ds(r * pad + per, BAND)
                kept = pl.ds(offs[g] + r * BAND, BAND)

                @pl.when(step > 0)
                def _():
                    dkb[last, :] += ckb[kept, :]
                    dvb[last, :] += cvb[kept, :]

                ckb[kept, :] = dkb[pl.ds(r * pad, BAND), :]
                cvb[kept, :] = dvb[pl.ds(r * pad, BAND), :]
            narrower = DILATIONS[order[pos + 1]] if pos + 1 < nd else None
            for n, (buf, out_ref, stride, at) in enumerate(
                    ((dqb, dq_ref, per, 0), (dkb, dk_ref, pad, BAND), (dvb, dv_ref, pad, BAND))):
                wider, onward = folds[2 * n + pos % 2], folds[2 * n + (pos + 1) % 2]
                for r in range(d):
                    val = buf[pl.ds(r * stride + at, per), :]
                    if pos > 0:
                        val = val + wider[pl.ds(r * per, per), :]
                    if narrower is None:
                        out_ref[...] = val
                    else:
                        start = (r % narrower) * (TOK // narrower) + r // narrower
                        onward[pl.ds(start, per, stride=d // narrower), :] = val

    main, prev = _dilated_specs(nblk, True)
    tok = pl.BlockSpec((TOK, BAND), lambda j, s: (nblk - 1 - s, j))
    acc_rows = max(d * (TOK // d + BAND) for d in DILATIONS)
    kept_rows = sum(DILATIONS) * BAND
    return _call(
        body, list(qs) + list(ks) + list(ks) + list(vs) + list(vs) + [do, lse, dd], name="attn_bwd",
        grid=(PAIRS, nblk), out_shape=[_sds((t, D_ATTN), F32)] * 3,
        in_specs=main + prev + main + prev + main + [tok] * 3, out_specs=[tok] * 3,
        scratch_shapes=[pltpu.VMEM((acc_rows, BAND), BF16)] * 2 + [pltpu.VMEM((TOK, BAND), BF16)] * 2
        + [pltpu.VMEM((TOK, BAND), F32)] * 2 + [pltpu.VMEM((acc_rows, BAND), F32)] * 2
        + [pltpu.VMEM((kept_rows, BAND), F32)] * 2
        + [pltpu.VMEM((TOK, BAND), F32)] * (6 + 2 * (nd - 2)),
        semantics=("parallel", "arbitrary"), carry=carry)


def _halo_rows(tm, t):
    per = tm // 8
    prev = lambda i: (jnp.maximum(i * per - 1, 0), 0)
    nxt = lambda i: (jnp.minimum((i + 1) * per, t // 8 - 1), 0)
    return prev, nxt


def _mixer_out(z, cw, y_attn, g_conv, g_attn, tm, carry=None):
    t = z.shape[0]
    prev, _ = _halo_rows(tm, t)

    def body(z_ref, zp_ref, cw_ref, y_ref, gc_ref, ga_ref, mix_ref):
        i = pl.program_id(0)
        keep = jnp.where(i > 0, 1.0, 0.0)
        cu = jnp.concatenate([zp_ref[:, 0:512] * zp_ref[:, 1024:1536] * keep,
                              z_ref[:, 0:512] * z_ref[:, 1024:1536]], axis=0)
        c = (cw_ref[0:1, :] * pltpu.roll(cu, 2, 0) + cw_ref[1:2, :] * pltpu.roll(cu, 1, 0)
             + cw_ref[2:3, :] * cu)[8:, :]
        yc = z_ref[:, 512:1024] * c
        mix_ref[:, 0:512] = (yc * _rms_scale(yc) * gc_ref[...]).astype(BF16)
        ya = y_ref[...]
        mix_ref[:, 512:1024] = (ya * _rms_scale(ya) * ga_ref[...]).astype(BF16)

    blk = pl.BlockSpec((tm, 512), lambda i: (i, 0))
    vec = pl.BlockSpec((1, 512), lambda i: (0, 0))
    return _call(
        body, [z, z, cw, y_attn, g_conv, g_attn], name="mixer_out", grid=(t // tm,),
        out_shape=_sds((t, 1024), BF16),
        in_specs=[pl.BlockSpec((tm, 1536), lambda i: (i, 0)), pl.BlockSpec((8, 1536), prev),
                  pl.BlockSpec((8, 512), lambda i: (0, 0)), blk, vec, vec],
        out_specs=pl.BlockSpec((tm, 1024), lambda i: (i, 0)),
        semantics=("parallel",), carry=carry)


def _mixer_bwd(z, dx1, wout, y_attn, cw, g_conv, g_attn, ones_bd, tm, carry=None):
    t = z.shape[0]
    nblk = t // tm
    prev, nxt = _halo_rows(tm, t)
    e = tm + 16

    def body(z_ref, zp_ref, zn_ref, dx_ref, dxn_ref, w_ref, y_ref, cw_ref, gc_ref, ga_ref, bd_ref,
             dz_ref, do_ref, dd_ref, dcw_ref, dgc_ref, dga_ref):
        i = pl.program_id(0)
        dm = _dot_nt(dx_ref[...], w_ref[...])
        dmn = _dot_nt(dxn_ref[...], w_ref[0:D_CONV, :])[0:8, :]
        rows = lax.broadcasted_iota(jnp.int32, (e, 1), 0)
        lo = jnp.where(i > 0, 0, 8)
        hi = jnp.where(i < nblk - 1, e, tm + 8)
        ze = jnp.concatenate([zp_ref[...], z_ref[...], zn_ref[...]], axis=0)
        u, gb, gcv = ze[:, 0:512], ze[:, 512:1024], ze[:, 1024:1536]
        w0, w1, w2 = cw_ref[0:1, :], cw_ref[1:2, :], cw_ref[2:3, :]
        cu = jnp.where(rows >= lo, gcv * u, 0.0)
        cu1, cu2 = pltpu.roll(cu, 1, 0), pltpu.roll(cu, 2, 0)
        c = w0 * cu2 + w1 * cu1 + w2 * cu
        yc = gb * c
        dma = jnp.concatenate([jnp.zeros((8, 512), F32), dm[:, 0:512], dmn], axis=0)
        dyc, ych = _rms_bwd(yc, _rms_scale(yc), gc_ref[...], dma)
        dc = jnp.where(jnp.logical_and(rows >= 8, rows < hi), dyc * gb, 0.0)
        dcu = w0 * pltpu.roll(dc, e - 2, 0) + w1 * pltpu.roll(dc, e - 1, 0) + w2 * dc
        mid = slice(8, 8 + tm)
        dz_ref[:, 0:512] = (dcu * gcv)[mid, :].astype(BF16)
        dz_ref[:, 512:1024] = (dyc * c)[mid, :].astype(BF16)
        dz_ref[:, 1024:1536] = (dcu * u)[mid, :].astype(BF16)

        ya = y_ref[...]
        dmb = dm[:, 512:1024]
        dya, yah = _rms_bwd(ya, _rms_scale(ya), ga_ref[...], dmb)
        do_ref[...] = dya
        dd_ref[...] = _head_sum(dya * ya, bd_ref[...])

        @pl.when(i == 0)
        def _():
            dcw_ref[...] = jnp.zeros_like(dcw_ref)
            dgc_ref[...] = jnp.zeros_like(dgc_ref)
            dga_ref[...] = jnp.zeros_like(dga_ref)

        dcm = jnp.where(rows < tm + 8, dc, 0.0)
        dcw_ref[0:1, :] += jnp.sum(dcm * cu2, axis=0, keepdims=True)
        dcw_ref[1:2, :] += jnp.sum(dcm * cu1, axis=0, keepdims=True)
        dcw_ref[2:3, :] += jnp.sum(dcm * cu, axis=0, keepdims=True)
        dgc_ref[...] += jnp.sum((dma * ych)[mid, :], axis=0, keepdims=True)
        dga_ref[...] += jnp.sum(dmb * yah, axis=0, keepdims=True)

    blk = pl.BlockSpec((tm, 512), lambda i: (i, 0))
    vec = pl.BlockSpec((1, 512), lambda i: (0, 0))
    cwb = pl.BlockSpec((8, 512), lambda i: (0, 0))
    next16 = lambda i: (jnp.minimum((i + 1) * (tm // 16), t // 16 - 1), 0)
    return _call(
        body, [z, z, z, dx1, dx1, wout, y_attn, cw, g_conv, g_attn, ones_bd], name="mixer_bwd",
        grid=(nblk,),
        out_shape=[_sds((t, D_IN), BF16), _sds((t, 512), F32), _sds((t, 512), F32),
                   _sds((8, 512), F32), _sds((1, 512), F32), _sds((1, 512), F32)],
        in_specs=[pl.BlockSpec((tm, 1536), lambda i: (i, 0)), pl.BlockSpec((8, 1536), prev),
                  pl.BlockSpec((8, 1536), nxt), pl.BlockSpec((tm, D_MODEL), lambda i: (i, 0)),
                  pl.BlockSpec((16, D_MODEL), next16),
                  pl.BlockSpec(wout.shape, lambda i: (0, 0), pipeline_mode=_resident(True)),
                  blk, cwb, vec, vec, pl.BlockSpec((512, 512), lambda i: (0, 0))],
        out_specs=[pl.BlockSpec((tm, 1536), lambda i: (i, 0)), blk, blk, cwb, vec, vec],
        carry=carry)


def _qkv_bwd(z, dz, dqn, dkn, dv, gq, gk, ones_bd, tm, carry=None):
    t = z.shape[0]

    def body(zq_ref, zk_ref, _, dqn_ref, dkn_ref, dv_ref, gq_ref, gk_ref, bd_ref,
             dz_ref, dgq_ref, dgk_ref):
        bd = bd_ref[...]

        @pl.when(pl.program_id(0) == 0)
        def _():
            dgq_ref[...] = jnp.zeros_like(dgq_ref)
            dgk_ref[...] = jnp.zeros_like(dgk_ref)

        def back(v, dn, g, scale):
            r = _head_rms_scale(v, bd)
            vh = v * r
            dh = dn * (g * scale)
            dv = r * (dh - vh * (_head_sum(dh * vh, bd) * (1.0 / HEAD_DIM)))
            return dv, jnp.sum(dn * scale * vh, axis=0, keepdims=True)

        dq, dgq = back(zq_ref[...], dqn_ref[...], gq_ref[...], HEAD_DIM ** -0.5)
        dk, dgk = back(zk_ref[...], dkn_ref[...], gk_ref[...], 1.0)
        dgq_ref[...] += dgq
        dgk_ref[...] += dgk
        dz_ref[:, 0:512] = dq.astype(BF16)
        dz_ref[:, 512:1024] = dk.astype(BF16)
        dz_ref[:, 1024:1536] = dv_ref[...].astype(BF16)

    blk = pl.BlockSpec((tm, 512), lambda i: (i, 0))
    vec = pl.BlockSpec((1, 512), lambda i: (0, 0))
    return _call(
        body, [z, z, dz, dqn, dkn, dv, gq, gk, ones_bd], name="qkv_bwd", grid=(t // tm,),
        out_shape=[_sds((t, D_IN), BF16), _sds((1, 512), F32), _sds((1, 512), F32)],
        in_specs=[pl.BlockSpec((tm, 512), lambda i: (i, 3)), pl.BlockSpec((tm, 512), lambda i: (i, 4)),
                  ANY] + [blk] * 3 + [vec, vec, pl.BlockSpec((512, 512), lambda i: (0, 0))],
        out_specs=[pl.BlockSpec((tm, 1536), lambda i: (i, 1)), vec, vec],
        carry=carry, aliases={2: 0})


def _columns_from_chips(g):
    return g.transpose(1, 0, 2).reshape(g.shape[1], N_CHIPS * g.shape[2])


def kernel(x, g_mix, w_in, conv_w, g_q, g_k, g_conv_out, g_attn_out, w_out, g_ffn, w_gate, w_up, w_down, loss_target, m_g_mix, m_w_in, m_conv_w, m_g_q, m_g_k, m_g_conv_out, m_g_attn_out, m_w_out, m_g_ffn, m_w_gate, m_w_up, m_w_down, v_g_mix, v_w_in, v_conv_w, v_g_q, v_g_k, v_g_conv_out, v_g_attn_out, v_w_out, v_g_ffn, v_w_gate, v_w_up, v_w_down):
    t = x.shape[1]
    xs = x[0]
    target = loss_target[0]
    tm = min(512, t)
    tm_wide = min(1024, t)
    tmm = min(2048, t)

    cw_pad = jnp.pad(conv_w[0], ((0, 13), (0, 0)))
    gathered = _all_gather([w_in[0].astype(BF16), cw_pad])
    win = _columns_from_chips(gathered[0])
    cw = jnp.pad(gathered[1][:, 0:3, :].transpose(1, 0, 2).reshape(3, D_CONV), ((0, 5), (0, 0)))
    later = [w_out[0].astype(BF16), w_gate[0].T.astype(BF16), w_up[0].T.astype(BF16),
             w_down[0].astype(BF16)]

    head_id = jnp.arange(D_ATTN) // HEAD_DIM
    ones_bd = (head_id[:, None] == head_id[None, :]).astype(BF16)
    gq_t = jnp.tile(g_q, (1, D_ATTN // HEAD_DIM))
    gk_t = jnp.tile(g_k, (1, D_ATTN // HEAD_DIM))

    h1, z, *dilated = _in_proj(xs, g_mix, win, gq_t, gk_t, ones_bd, tm)
    nd = len(DILATIONS)
    qs, ks, vs = dilated[:nd], dilated[nd:2 * nd], dilated[2 * nd:]
    (y_attn, lse), gathered = _attn_fwd(qs, ks, vs, carry=_x_gather_chips(later))
    mix, gathered = _mixer_out(z, cw, y_attn, g_conv_out, g_attn_out, tm,
                               carry=_x_gather_sibling(gathered))
    wout = gathered[0].reshape(D_MODEL, D_MODEL)
    wgate_t = gathered[1].reshape(D_FF, D_MODEL)
    wup_t = gathered[2].reshape(D_FF, D_MODEL)
    wdown = gathered[3].reshape(D_FF, D_MODEL)
    (x1,) = _matmul("out_proj", mix, wout, [xs], [F32], lambda acc, r: (r + acc,), tm, D_MODEL)
    h2, gate, up, act = _norm_matmul("ffn_up", x1, g_ffn, [wgate_t, wup_t], tm, D_FF, True, BF16,
                                     transposed_w=True)

    def loss_epilogue(acc, r, tgt):
        err = r + acc - tgt
        dy = err * (1.0 / D_MODEL)
        return dy, dy, jnp.sum(err * err)

    dx2, dx2b, loss_sum = _matmul("ffn_down_loss", act, wdown, [x1, target], [F32, BF16],
                                  loss_epilogue, tm_wide, D_MODEL, loss=True)

    def swiglu_bwd(da, gt, u):
        gt, u = gt.astype(F32), u.astype(F32)
        s = _sigmoid(gt)
        return da * u * (s * (1.0 + gt * (1.0 - s))), da * (gt * s)

    dgate, dup = _matmul("ffn_down_bwd", dx2b, wdown, [gate, up], [BF16, BF16], swiglu_bwd,
                         tm, D_FF, transposed_w=True)
    gw_down = _matmul_tn("grad_w_down", act, dx2b, 512, tmm)
    gw_gate_t = _matmul_tn("grad_w_gate", dgate, h2, 512, tmm)
    gw_up_t = _matmul_tn("grad_w_up", dup, h2, 512, tmm)

    me = 2 * lax.axis_index("x") + lax.axis_index("y")
    where = jnp.stack([lax.axis_index("c"), me]).astype(jnp.int32)

    def pair_sums(names, full, got):
        return [_pair_sum(f"pair_sum_{nme}", a, b, where) for nme, a, b in zip(names, full, got)]

    def chip_sums(names, pair, got):
        return [_chip_sum(f"chip_sum_{nme}", own, b) for nme, (_, own), b in zip(names, pair, got)]

    ffn = ["w_gate", "w_up", "w_down"]
    full = [g.reshape(N_CHIPS, D_FF // N_CHIPS, D_MODEL) for g in (gw_gate_t, gw_up_t, gw_down)]
    (dx1, dx1b, gg_ffn), got = _matmul_norm_bwd(
        "ffn_up_bwd", [(dgate, wgate_t), (dup, wup_t)], x1, dx2, g_ffn, tm, carry=_x_pair(full),
        transposed_w=False)
    pair = pair_sums(ffn, full, got)
    gw_out = _matmul_tn("grad_w_out", mix, dx1b, 512, tmm)
    full = [gw_out.reshape(N_CHIPS, D_MODEL // N_CHIPS, D_MODEL)]
    (dzc, do, dd, gcw, gg_conv, gg_attn), got = _mixer_bwd(
        z, dx1b, wout, y_attn, cw, g_conv_out, g_attn_out, ones_bd, tm, carry=_x_pair(full))
    pair += pair_sums(["w_out"], full, got)
    early = ffn + ["w_out"]
    (dqn, dkn, dv), got = _attn_bwd(qs, ks, vs, do, lse, dd, carry=_x_chips([p for p, _ in pair]))
    mine = chip_sums(early, pair, got)
    (dz, gg_q, gg_k), theirs = _qkv_bwd(z, dzc, dqn, dkn, dv, gq_t, gk_t, ones_bd, tm,
                                        carry=_x_share(mine))
    full = [_matmul_tn("grad_w_in", h1, dz, D_IN // N_CHIPS, tmm, by_chip=True)]
    nblk = t // tm_wide
    head = max(nblk // 4, 1)
    assert nblk > head
    (*begun, gg_head), got = _matmul_norm_bwd("in_proj_bwd_head", [(dz, win)], xs, dx1, g_mix, tm_wide,
                                              carry=_x_pair(full), blocks=(0, head))
    pair = pair_sums(["w_in"], full, got)
    (grad_x, _, gg_rest), got = _matmul_norm_bwd(
        "in_proj_bwd", [(dz, win)], xs, dx1, g_mix, tm_wide, carry=_x_chips([pair[0][0]]),
        blocks=(head, nblk - head), into=begun)
    gg_mix = gg_head + gg_rest
    mine += chip_sums(["w_in"], pair, got)
    theirs = list(theirs) + list(_exchange_alone("grad_pair_share_w_in", _x_share(mine[-1:])))
    big = early + ["w_in"]

    small = _small_all_reduce({
        "g_mix": gg_mix, "g_ffn": gg_ffn, "g_conv_out": gg_conv, "g_attn_out": gg_attn,
        "g_q": gg_q, "g_k": gg_k, "loss": loss_sum, "conv_w": gcw})
    heads = D_ATTN // HEAD_DIM
    grads = {
        "g_mix": small[0:1, :], "g_ffn": small[1:2, :],
        "g_conv_out": small[2:3, 0:512], "g_attn_out": small[2:3, 512:1024],
        "g_q": small[3, 0:512].reshape(heads, HEAD_DIM).sum(axis=0)[None, :],
        "g_k": small[3, 512:1024].reshape(heads, HEAD_DIM).sum(axis=0)[None, :],
        "conv_w": lax.dynamic_slice(small[8:11, 0:512], (0, me * (D_CONV // N_CHIPS)),
                                    (3, D_CONV // N_CHIPS)),
    }
    halves = dict(zip(big, zip(mine, theirs)))
    loss = small[4, 0] * 0.5 * (1.0 / D_MODEL)

    weights = dict(g_mix=g_mix, w_in=w_in, conv_w=conv_w, g_q=g_q, g_k=g_k, g_conv_out=g_conv_out,
                   g_attn_out=g_attn_out, w_out=w_out, g_ffn=g_ffn, w_gate=w_gate, w_up=w_up,
                   w_down=w_down)
    moments_m = dict(g_mix=m_g_mix, w_in=m_w_in, conv_w=m_conv_w, g_q=m_g_q, g_k=m_g_k,
                     g_conv_out=m_g_conv_out, g_attn_out=m_g_attn_out, w_out=m_w_out, g_ffn=m_g_ffn,
                     w_gate=m_w_gate, w_up=m_w_up, w_down=m_w_down)
    moments_v = dict(g_mix=v_g_mix, w_in=v_w_in, conv_w=v_conv_w, g_q=v_g_q, g_k=v_g_k,
                     g_conv_out=v_g_conv_out, g_attn_out=v_g_attn_out, w_out=v_w_out, g_ffn=v_g_ffn,
                     w_gate=v_w_gate, w_up=v_w_up, w_down=v_w_down)
    names = list(weights)
    out_g, out_d, out_m, out_v = [], [], [], []
    for nme in names:
        wgt = weights[nme]
        shape2 = wgt.shape[-2:] if wgt.ndim == 3 else wgt.shape
        flip = nme in ("w_gate", "w_up")

        def to2d(a):
            return a.reshape(shape2).T if flip else a.reshape(shape2)

        def back(a):
            return (a.T if flip else a).reshape(wgt.shape)

        state = (to2d(wgt), to2d(moments_m[nme]), to2d(moments_v[nme]))
        if nme in halves:
            g2, dlt, nm, nv = _adamw_shard(f"adamw_{nme}", *state, *halves[nme], where)
        else:
            g2 = grads[nme].reshape(shape2)
            dlt, nm, nv = _adamw(f"adamw_{nme}", state[0], g2, state[1], state[2])
        out_g.append(back(g2))
        out_d.append(back(dlt))
        out_m.append(back(nm))
        out_v.append(back(nv))
    return (loss, grad_x[None], *out_g, *out_d, *out_m, *out_v)
```

```python
import functools
from typing import Any, Callable, NamedTuple, Sequence

import jax
import jax.numpy as jnp
from jax import lax
from jax.experimental import pallas as pl
from jax.experimental.pallas import tpu as pltpu

F32 = jnp.float32
BF16 = jnp.bfloat16
MESH = pl.DeviceIdType.MESH

D_MODEL = 1024
D_CONV = 512
D_ATTN = 512
HEAD_DIM = 64
D_FF = 2816
D_IN = 3 * D_CONV + 3 * D_ATTN
DILATIONS = (1, 4, 16)
BAND = 128
EPS = 1e-6
NEG = -1e30
N_CHIPS = 4

ADAM_LR = 0.001
ADAM_B1 = 0.9
ADAM_B2 = 0.999
ADAM_EPS = 1e-08
ADAM_WD = 0.01
ADAM_STEP = 10

V7X_VMEM_BYTES = 64 * 1024 * 1024
VMEM_LIMIT = V7X_VMEM_BYTES - 8 * 1024 * 1024
ANY = pl.BlockSpec(memory_space=pl.ANY)
VMEM_WHOLE = pl.BlockSpec(memory_space=pltpu.VMEM)


def _params(*sem):
    return pltpu.CompilerParams(dimension_semantics=sem, vmem_limit_bytes=VMEM_LIMIT)


def _sds(shape, dtype):
    return jax.ShapeDtypeStruct(shape, dtype)


def _resident(whole):
    return pl.Buffered(1) if whole else None


def _place():
    x, y, c = lax.axis_index("x"), lax.axis_index("y"), lax.axis_index("c")
    chips = [(1 - x, y), (x, 1 - y), (1 - x, 1 - y)]
    return x, y, c, 2 * x + y, chips, [2 * cx + cy for cx, cy in chips]


def _all_gather(shards):
    n = len(shards)

    def body(*refs):
        ins, outs, stage = refs[:n], refs[n:2 * n], refs[2 * n:3 * n]
        ssem, rsem, fsem, gsem, lsem, osem = refs[3 * n:]
        x, y, c, me, chips, cids = _place()
        sib = (x, y, 1 - c)

        def half(w, which):
            h = shards[w].shape[0] // 2
            return pl.ds(pl.multiple_of(which * h, 8), h)

        loads = [pltpu.make_async_copy(ins[w], stage[w], lsem.at[w]) for w in range(n)]
        local = [pltpu.make_async_copy(stage[w], outs[w].at[me], osem.at[w]) for w in range(n)]
        for cp in loads:
            cp.start()

        def chip_copy(w, j, src_slot):
            rows = half(w, c)
            return pltpu.make_async_remote_copy(
                src_ref=ins[w].at[rows], dst_ref=outs[w].at[src_slot, rows],
                send_sem=ssem.at[3 * w + j], recv_sem=rsem.at[3 * w + j],
                device_id=(*chips[j], c), device_id_type=MESH)

        def sib_copy(w, j, which):
            rows = half(w, which)
            return pltpu.make_async_remote_copy(
                src_ref=outs[w].at[cids[j], rows], dst_ref=outs[w].at[cids[j], rows],
                send_sem=fsem.at[3 * w + j], recv_sem=gsem.at[3 * w + j],
                device_id=sib, device_id_type=MESH)

        sends = [chip_copy(w, j, me) for w in range(n) for j in range(3)]
        for cp in sends:
            cp.start()
        for w in range(n):
            loads[w].wait()
            local[w].start()
        passed = []
        for w in range(n):
            for j in range(3):
                chip_copy(w, j, cids[j]).wait_recv()
                cp = sib_copy(w, j, c)
                cp.start()
                passed.append(cp)
        for w in range(n):
            for j in range(3):
                sib_copy(w, j, 1 - c).wait_recv()
        for cp in sends + passed:
            cp.wait_send()
        for cp in local:
            cp.wait()

    return pl.pallas_call(
        body, name="all_gather_weights",
        out_shape=[_sds((N_CHIPS,) + s.shape, s.dtype) for s in shards],
        in_specs=[ANY] * n, out_specs=[ANY] * n,
        scratch_shapes=[pltpu.VMEM(s.shape, s.dtype) for s in shards]
        + [pltpu.SemaphoreType.DMA((3 * n,))] * 4 + [pltpu.SemaphoreType.DMA((n,))] * 2,
        compiler_params=pltpu.CompilerParams(vmem_limit_bytes=VMEM_LIMIT),
    )(*shards)


class _Exchange(NamedTuple):
    srcs: Sequence[Any]
    lands: Sequence[Any]
    outs: Sequence[Any]
    n_sems: int
    copies: Callable


def _remote(src, dst, ssem, rsem, k, to):
    return pltpu.make_async_remote_copy(src_ref=src, dst_ref=dst, send_sem=ssem.at[k],
                                        recv_sem=rsem.at[k], device_id=to, device_id_type=MESH)


def _x_gather_chips(shards):
    def copies(srcs, lands, outs, ssem, rsem):
        _, _, c, me, chips, cids = _place()
        go, arrive = [], []
        for w, s in enumerate(shards):
            h = s.shape[0] // 2
            rows = pl.ds(pl.multiple_of(c * h, 8), h)
            for j in range(3):
                to = (*chips[j], c)
                go.append(_remote(srcs[w].at[rows], lands[w].at[me, rows], ssem, rsem, 3 * w + j, to))
                arrive.append(_remote(srcs[w].at[rows], lands[w].at[cids[j], rows], ssem, rsem,
                                      3 * w + j, to))
        return go, arrive

    lands = [jnp.broadcast_to(s[None], (N_CHIPS,) + s.shape) for s in shards]
    return _Exchange(shards, lands, [], 3 * len(shards), copies)


def _x_gather_sibling(gathered):
    def copies(srcs, lands, outs, ssem, rsem):
        x, y, c, _, _, cids = _place()
        go, arrive = [], []
        for w, g in enumerate(gathered):
            h = g.shape[1] // 2
            mine = pl.ds(pl.multiple_of(c * h, 8), h)
            theirs = pl.ds(pl.multiple_of((1 - c) * h, 8), h)
            for j in range(3):
                slab = lands[w].at[cids[j]]
                go.append(_remote(slab.at[mine], slab.at[mine], ssem, rsem, 3 * w + j, (x, y, 1 - c)))
                arrive.append(_remote(slab.at[theirs], slab.at[theirs], ssem, rsem, 3 * w + j,
                                      (x, y, 1 - c)))
        return go, arrive

    return _Exchange([], gathered, [], 3 * len(gathered), copies)


def _x_pair(grads):
    def copies(srcs, lands, outs, ssem, rsem):
        x, y, c, _, _, _ = _place()
        go = []
        for w, g in enumerate(grads):
            h = g.shape[1] // 2
            theirs = pl.ds(pl.multiple_of((1 - c) * h, 8), h)
            go.append(_remote(srcs[w].at[:, theirs, :], outs[w], ssem, rsem, w, (x, y, 1 - c)))
        return go, go

    outs = [_sds((N_CHIPS, g.shape[1] // 2, g.shape[2]), g.dtype) for g in grads]
    return _Exchange(grads, [], outs, len(grads), copies)


def _x_chips(parts):
    def copies(srcs, lands, outs, ssem, rsem):
        _, _, c, _, chips, cids = _place()
        go = [_remote(srcs[w].at[cids[j]], outs[w].at[j], ssem, rsem, 3 * w + j, (*chips[j], c))
              for w in range(len(parts)) for j in range(3)]
        return go, go

    outs = [_sds((3,) + p.shape[1:], p.dtype) for p in parts]
    return _Exchange(parts, [], outs, 3 * len(parts), copies)


def _x_share(halves):
    def copies(srcs, lands, outs, ssem, rsem):
        x, y, c, _, _, _ = _place()
        go = [_remote(srcs[w], outs[w], ssem, rsem, w, (x, y, 1 - c)) for w in range(len(halves))]
        return go, go

    return _Exchange(halves, [], [_sds(h.shape, h.dtype) for h in halves], len(halves), copies)


def _call(body, args, *, name, grid, in_specs, out_specs, out_shape, scratch_shapes=(),
          semantics=None, carry=None, aliases=None):
    single = not isinstance(out_shape, (list, tuple))
    out_shape = [out_shape] if single else list(out_shape)
    out_specs = [out_specs] if single else list(out_specs)
    aliases = dict(aliases or {})
    if carry is None:
        res = pl.pallas_call(
            body, name=name, grid=grid, in_specs=list(in_specs), out_specs=out_specs,
            out_shape=out_shape, scratch_shapes=list(scratch_shapes), input_output_aliases=aliases,
            compiler_params=_params(*(semantics or ("arbitrary",) * len(grid))))(*args)
        return res[0] if single else res
    n_in, n_out, n_scr = len(args), len(out_shape), len(scratch_shapes)
    n_src, n_land, n_new = len(carry.srcs), len(carry.lands), len(carry.outs)

    def carrying(*refs):
        at = 0
        parts = []
        for n in (n_in, n_src, n_land, n_out, n_land, n_new, n_scr, 2):
            parts.append(refs[at:at + n])
            at += n
        ins, srcs, _, outs, lands, news, scratch, (ssem, rsem) = parts
        ids = [pl.program_id(a) for a in range(len(grid))]
        first = functools.reduce(jnp.logical_and, [i == 0 for i in ids])
        last = functools.reduce(jnp.logical_and, [i == g - 1 for i, g in zip(ids, grid)])
        go, arrive = carry.copies(srcs, lands, news, ssem, rsem)

        @pl.when(first)
        def _():
            for cp in go:
                cp.start()

        body(*ins, *outs, *scratch)

        @pl.when(last)
        def _():
            for cp in go:
                cp.wait_send()
            for cp in arrive:
                cp.wait_recv()

    res = pl.pallas_call(
        carrying, name=name, grid=grid,
        in_specs=list(in_specs) + [ANY] * (n_src + n_land),
        out_specs=out_specs + [ANY] * (n_land + n_new),
        out_shape=out_shape + [_sds(a.shape, a.dtype) for a in carry.lands] + list(carry.outs),
        input_output_aliases={**aliases, **{n_in + n_src + i: n_out + i for i in range(n_land)}},
        scratch_shapes=list(scratch_shapes) + [pltpu.SemaphoreType.DMA((carry.n_sems,))] * 2,
        compiler_params=_params(*(("arbitrary",) * len(grid))))(*args, *carry.srcs, *carry.lands)
    own = res[:n_out]
    return (own[0] if single else own), res[n_out:]


def _exchange_alone(name, exchange):
    def body(x_ref, o_ref):
        o_ref[...] = x_ref[...]

    blk = pl.BlockSpec((8, 128), lambda i: (0, 0))
    _, res = _call(body, [jnp.zeros((8, 128), F32)], name=name, grid=(1,), in_specs=[blk],
                   out_specs=blk, out_shape=_sds((8, 128), F32), carry=exchange)
    return res


def _row_block(r, want):
    return max(d for d in range(1, min(want, r) + 1) if r % d == 0 and (d % 8 == 0 or d == r))


def _pair_sum(name, full, got, where):
    _, r, n = full.shape
    h = r // 2
    tr = _row_block(h, 512)
    nb = h // tr

    def body(w_ref, a_ref, b_ref, o_ref, own_ref):
        total = a_ref[...] + b_ref[...]
        o_ref[...] = total.astype(BF16)

        @pl.when(pl.program_id(1) == w_ref[1])
        def _():
            own_ref[...] = total[0]

    blk = pl.BlockSpec((1, tr, n), lambda i, s, w: (s, i, 0))
    return pl.pallas_call(
        body, name=name, out_shape=[_sds(got.shape, BF16), _sds((h, n), F32)],
        grid_spec=pltpu.PrefetchScalarGridSpec(
            num_scalar_prefetch=1, grid=(nb, N_CHIPS),
            in_specs=[pl.BlockSpec((1, tr, n), lambda i, s, w: (s, w[0] * nb + i, 0)), blk],
            out_specs=[blk, pl.BlockSpec((tr, n), lambda i, s, w: (i, 0))]),
        compiler_params=_params("parallel", "arbitrary"),
    )(where, full, got)


def _chip_sum(name, own, got):
    h, n = own.shape
    tr = _row_block(h, 256)

    def body(a_ref, b0, b1, b2, o_ref):
        o_ref[...] = ((a_ref[...] + b0[0].astype(F32)) + b1[0].astype(F32)) + b2[0].astype(F32)

    def slot(j):
        return pl.BlockSpec((1, tr, n), lambda i: (j, i, 0))

    blk = pl.BlockSpec((tr, n), lambda i: (i, 0))
    return pl.pallas_call(
        body, name=name, grid=(h // tr,), out_shape=_sds((h, n), F32),
        in_specs=[blk, slot(0), slot(1), slot(2)], out_specs=blk,
        compiler_params=_params("parallel"),
    )(own, got, got, got)


SMALL_ROWS = 16
SMALL_LAYOUT = (
    ("g_mix", 0, 0, 1, 1024), ("g_ffn", 1, 0, 1, 1024), ("g_conv_out", 2, 0, 1, 512),
    ("g_attn_out", 2, 512, 1, 512), ("g_q", 3, 0, 1, 512), ("g_k", 3, 512, 1, 512),
    ("loss", 4, 0, 1, 128), ("conv_w", 8, 0, 8, 512))


def _small_all_reduce(parts):
    names = [s[0] for s in SMALL_LAYOUT]

    def body(*refs):
        ins = refs[:len(names)]
        out_ref, stage, buf, ssem, rsem = refs[len(names):]
        x, y, c, _, _, _ = _place()
        me = 4 * x + 2 * y + c
        stage[...] = jnp.zeros_like(stage)
        for ref, (_, r0, c0, nr, nc) in zip(ins, SMALL_LAYOUT):
            stage[r0:r0 + nr, c0:c0 + nc] = ref[0:nr, :]
        buf[me] = stage[...]
        peers = []
        for d in range(1, 8):
            px = 1 - x if d & 4 else x
            py = 1 - y if d & 2 else y
            pc = 1 - c if d & 1 else c
            peers.append(((px, py, pc), 4 * px + 2 * py + pc))
        sends = [pltpu.make_async_remote_copy(
            src_ref=stage, dst_ref=buf.at[me], send_sem=ssem.at[k], recv_sem=rsem.at[k],
            device_id=peer, device_id_type=MESH) for k, (peer, _) in enumerate(peers)]
        for cp in sends:
            cp.start()
        for k, (peer, pid) in enumerate(peers):
            pltpu.make_async_remote_copy(
                src_ref=stage, dst_ref=buf.at[pid], send_sem=ssem.at[k], recv_sem=rsem.at[k],
                device_id=peer, device_id_type=MESH).wait_recv()
        for cp in sends:
            cp.wait_send()
        acc = buf[0]
        for k in range(1, 8):
            acc = acc + buf[k]
        out_ref[...] = acc

    return pl.pallas_call(
        body, name="small_all_reduce", out_shape=_sds((SMALL_ROWS, 1024), F32),
        in_specs=[VMEM_WHOLE] * len(names), out_specs=VMEM_WHOLE,
        scratch_shapes=[pltpu.VMEM((SMALL_ROWS, 1024), F32), pltpu.VMEM((8, SMALL_ROWS, 1024), F32),
                        pltpu.SemaphoreType.DMA((7,)), pltpu.SemaphoreType.DMA((7,))],
    )(*[parts[k] for k in names])


def _dot(a, b):
    return jnp.dot(a, b, preferred_element_type=F32)


def _dot_nt(a, b):
    return lax.dot_general(a, b, (((1,), (1,)), ((), ())), preferred_element_type=F32)


def _dot_tn(a, b):
    return lax.dot_general(a, b, (((0,), (0,)), ((), ())), preferred_element_type=F32)


def _sigmoid(v):
    return 1.0 / (1.0 + jnp.exp(-v))


def _rms_scale(v):
    return lax.rsqrt(jnp.mean(v * v, axis=-1, keepdims=True) + EPS)


def _rms_bwd(v, r, g, dy):
    vh = v * r
    dh = dy * g
    return r * (dh - vh * jnp.mean(dh * vh, axis=-1, keepdims=True)), vh


def _head_sum(a, ones_bd):
    hi = a.astype(BF16)
    lo = (a - hi.astype(F32)).astype(BF16)
    return _dot(hi, ones_bd) + _dot(lo, ones_bd)


def _head_rms_scale(v, ones_bd):
    return lax.rsqrt(_head_sum(v * v, ones_bd) * (1.0 / HEAD_DIM) + EPS)


MXU_COLUMNS = 256


def _column_chunks(n):
    width = MXU_COLUMNS if n % MXU_COLUMNS == 0 else n
    return [slice(c, c + width) for c in range(0, n, width)]


def _norm_matmul(name, x, g, ws, tm, tn, swiglu, out_dtype=F32, transposed_w=False):
    t, d = x.shape
    n = ws[0].shape[0] if transposed_w else ws[0].shape[1]
    nw = len(ws)

    def body(x_ref, g_ref, *refs):
        w_refs, h_ref, o_refs = refs[:nw], refs[nw], refs[nw + 1:2 * nw + 1]
        hs = refs[-1]

        @pl.when(pl.program_id(1) == 0)
        def _():
            xv = x_ref[...]
            h = (xv * _rms_scale(xv) * g_ref[...]).astype(BF16)
            hs[...] = h
            h_ref[...] = h

        h = hs[...]
        for cols in _column_chunks(tn):
            outs = [_dot_nt(h, w[cols, :]) if transposed_w else _dot(h, w[:, cols]) for w in w_refs]
            for o_ref, o in zip(o_refs, outs):
                o_ref[:, cols] = o.astype(out_dtype)
            if swiglu:
                refs[2 * nw + 1][:, cols] = (outs[0] * _sigmoid(outs[0]) * outs[1]).astype(BF16)

    row = pl.BlockSpec((tm, d), lambda i, j: (i, 0))
    col = pl.BlockSpec((tm, tn), lambda i, j: (i, j))
    out_shape = [_sds((t, d), BF16)] + [_sds((t, n), out_dtype)] * nw
    out_specs = [row] + [col] * nw
    if swiglu:
        out_shape.append(_sds((t, n), BF16))
        out_specs.append(col)
    return pl.pallas_call(
        body, name=name, grid=(t // tm, n // tn), out_shape=out_shape,
        in_specs=[row, pl.BlockSpec((1, d), lambda i, j: (0, 0))]
        + [pl.BlockSpec((tn, d), lambda i, j: (j, 0), pipeline_mode=_resident(tn == n))
           if transposed_w
           else pl.BlockSpec((d, tn), lambda i, j: (0, j), pipeline_mode=_resident(tn == n))] * nw,
        out_specs=out_specs, scratch_shapes=[pltpu.VMEM((tm, d), BF16)],
        compiler_params=_params("parallel", "arbitrary"),
    )(x, g, *ws)


def _matmul(name, a, w, extras, out_dtypes, epilogue, tm, tn, transposed_w=False, loss=False):
    t, k = a.shape
    n = w.shape[0] if transposed_w else w.shape[1]
    ne, no = len(extras), len(out_dtypes)

    def body(a_ref, w_ref, *refs):
        e_refs, o_refs = refs[:ne], refs[ne:]
        a = a_ref[...]
        total = 0.0
        for cols in _column_chunks(tn):
            acc = _dot_nt(a, w_ref[cols, :]) if transposed_w else _dot(a, w_ref[:, cols])
            res = epilogue(acc, *[e[:, cols] for e in e_refs])
            for o_ref, r in zip(o_refs[:no], res[:no]):
                o_ref[:, cols] = r.astype(o_ref.dtype)
            if loss:
                total = total + res[no]
        if loss:
            first = jnp.logical_and(pl.program_id(0) == 0, pl.program_id(1) == 0)

            @pl.when(first)
            def _():
                o_refs[no][...] = jnp.zeros_like(o_refs[no])

            o_refs[no][...] += total

    col = pl.BlockSpec((tm, tn), lambda i, j: (i, j))
    w_spec = (pl.BlockSpec((tn, k), lambda i, j: (j, 0), pipeline_mode=_resident(tn == n))
              if transposed_w
              else pl.BlockSpec((k, tn), lambda i, j: (0, j), pipeline_mode=_resident(tn == n)))
    out_shape = [_sds((t, n), dt) for dt in out_dtypes]
    out_specs = [col] * no
    if loss:
        out_shape.append(_sds((8, 128), F32))
        out_specs.append(pl.BlockSpec((8, 128), lambda i, j: (0, 0)))
    return pl.pallas_call(
        body, name=name, grid=(t // tm, n // tn), out_shape=out_shape,
        in_specs=[pl.BlockSpec((tm, k), lambda i, j: (i, 0)), w_spec] + [col] * ne,
        out_specs=out_specs,
        compiler_params=_params(*(("arbitrary", "arbitrary") if loss else ("parallel", "parallel"))),
    )(a, w, *extras)


def _matmul_norm_bwd(name, pairs, x, dres, g, tm, carry=None, transposed_w=True):
    t, d = x.shape
    npairs = len(pairs)
    product = _dot_nt if transposed_w else _dot

    def body(*refs):
        a_refs, w_refs = refs[:npairs], refs[npairs:2 * npairs]
        x_ref, r_ref, g_ref, dx_ref, dxb_ref, dg_ref = refs[2 * npairs:]
        dy = product(a_refs[0][...], w_refs[0][...])
        for a_ref, w_ref in zip(a_refs[1:], w_refs[1:]):
            dy = dy + product(a_ref[...], w_ref[...])
        xv = x_ref[...]
        dx, xh = _rms_bwd(xv, _rms_scale(xv), g_ref[...], dy)
        dx = dx + r_ref[...]
        dx_ref[...] = dx
        dxb_ref[...] = dx.astype(BF16)

        @pl.when(pl.program_id(0) == 0)
        def _():
            dg_ref[...] = jnp.zeros_like(dg_ref)

        dg_ref[...] += jnp.sum(dy * xh, axis=0, keepdims=True)

    row = pl.BlockSpec((tm, d), lambda i: (i, 0))
    vec = pl.BlockSpec((1, d), lambda i: (0, 0))
    return _call(
        body, [a for a, _ in pairs] + [w for _, w in pairs] + [x, dres, g], name=name,
        grid=(t // tm,), out_shape=[_sds((t, d), F32), _sds((t, d), BF16), _sds((1, d), F32)],
        in_specs=[pl.BlockSpec((tm, a.shape[1]), lambda i: (i, 0)) for a, _ in pairs]
        + [pl.BlockSpec(w.shape, lambda i: (0, 0), pipeline_mode=pl.Buffered(1)) for _, w in pairs]
        + [row, row, vec],
        out_specs=[row, row, vec], carry=carry)


def _matmul_tn(name, a, g, tn, tk, by_chip=False):
    t, ka = a.shape
    n = g.shape[1]

    def body(a_ref, g_ref, o_ref):
        @pl.when(pl.program_id(1) == 0)
        def _():
            o_ref[...] = jnp.zeros_like(o_ref)

        acc = _dot_tn(a_ref[...], g_ref[...])
        o_ref[...] += acc[None] if by_chip else acc

    return pl.pallas_call(
        body, name=name, grid=(n // tn, t // tk),
        out_shape=_sds((n // tn, ka, tn) if by_chip else (ka, n), F32),
        in_specs=[pl.BlockSpec((tk, ka), lambda j, s: (s, 0)),
                  pl.BlockSpec((tk, tn), lambda j, s: (s, j))],
        out_specs=(pl.BlockSpec((1, ka, tn), lambda j, s: (j, 0, 0)) if by_chip
                   else pl.BlockSpec((ka, tn), lambda j, s: (0, j))),
        compiler_params=_params("parallel", "arbitrary"),
    )(a, g)


def _elementwise(name, fn, ins, out_dtypes, tr):
    r, n = ins[0].shape
    tr = _row_block(r, tr)
    ni = len(ins)

    def body(*refs):
        res = fn(*[ref[...] for ref in refs[:ni]])
        for o_ref, v in zip(refs[ni:], res):
            o_ref[...] = v.astype(o_ref.dtype)

    blk = pl.BlockSpec((tr, n), lambda i: (i, 0))
    return pl.pallas_call(
        body, name=name, grid=(r // tr,), out_shape=[_sds((r, n), dt) for dt in out_dtypes],
        in_specs=[blk] * ni, out_specs=[blk] * len(out_dtypes),
        compiler_params=_params("parallel"),
    )(*ins)


def _adamw_update(w, g, m, v):
    m = ADAM_B1 * m + (1.0 - ADAM_B1) * g
    v = ADAM_B2 * v + (1.0 - ADAM_B2) * (g * g)
    m_hat = m / (1.0 - ADAM_B1 ** ADAM_STEP)
    v_hat = v / (1.0 - ADAM_B2 ** ADAM_STEP)
    return -ADAM_LR * (m_hat / (jnp.sqrt(v_hat) + ADAM_EPS) + ADAM_WD * w), m, v


def _adamw(name, w, g, m, v):
    return _elementwise(name, _adamw_update, [w, g, m, v], [F32] * 3, 256)


def _adamw_shard(name, w, m, v, mine, theirs, where):
    r, n = w.shape
    h = r // 2
    tr = _row_block(h, 128)
    nb = h // tr

    def body(w_ref, p_ref, m_ref, v_ref, a_ref, b_ref, g_ref, d_ref, nm_ref, nv_ref):
        g = jnp.where(pl.program_id(0) == w_ref[0], a_ref[...], b_ref[...])
        g_ref[...] = g
        d_ref[...], nm_ref[...], nv_ref[...] = _adamw_update(p_ref[...], g, m_ref[...], v_ref[...])

    whole = pl.BlockSpec((tr, n), lambda s, i, c: (s * nb + i, 0))
    used = pl.BlockSpec((tr, n), lambda s, i, c: (jnp.where(s == c[0], i, 0), 0))
    unused = pl.BlockSpec((tr, n), lambda s, i, c: (jnp.where(s == c[0], 0, i), 0))
    return pl.pallas_call(
        body, name=name, out_shape=[_sds((r, n), F32)] * 4,
        grid_spec=pltpu.PrefetchScalarGridSpec(
            num_scalar_prefetch=1, grid=(2, nb), in_specs=[whole] * 3 + [used, unused],
            out_specs=[whole] * 4),
        compiler_params=_params("arbitrary", "arbitrary"),
    )(where, w, m, v, mine, theirs)


PAIRS = D_ATTN // BAND


def _in_proj(x, g, w, gq, gk, ones_bd, tm):
    t, dm = x.shape
    n = w.shape[1]
    nd = len(DILATIONS)
    first = 3 * D_CONV

    def body(x_ref, g_ref, w_ref, gq_ref, gk_ref, bd_ref, h_ref, z_ref, *refs):
        outs, slabs = refs[:3 * nd], refs[3 * nd:]
        xv = x_ref[...]
        h = (xv * _rms_scale(xv) * g_ref[...]).astype(BF16)
        h_ref[...] = h
        for cols in _column_chunks(n):
            z_ref[:, cols] = _dot(h, w_ref[:, cols])
        bd = bd_ref[...]
        q = z_ref[:, first:first + D_ATTN]
        k = z_ref[:, first + D_ATTN:first + 2 * D_ATTN]
        vals = [(q * _head_rms_scale(q, bd) * gq_ref[...]) * HEAD_DIM ** -0.5,
                k * _head_rms_scale(k, bd) * gk_ref[...], z_ref[:, first + 2 * D_ATTN:n]]
        for m, val in enumerate(vals):
            for c in range(PAIRS):
                slabs[0][c] = val[:, c * BAND:(c + 1) * BAND]
            cur, before = 0, 1
            for a, d in enumerate(DILATIONS):
                o_ref, src, dst = outs[m * nd + a], slabs[cur], slabs[1 - cur]
                step, count = d // before, tm // d
                keep = step > 1 and a + 1 < nd
                for c in range(PAIRS):
                    for r in range(d):
                        start = (r % before) * (tm // before) + r // before
                        rows = src.at[c][pl.ds(start, count, stride=step), :] if step > 1 else src[c]
                        o_ref[c, r] = rows.astype(BF16)
                        if keep:
                            dst.at[c][pl.ds(r * count, count), :] = rows
                if keep:
                    cur = 1 - cur
                before = d

    row = pl.BlockSpec((tm, dm), lambda i: (i, 0))
    vec = pl.BlockSpec((1, D_ATTN), lambda i: (0, 0))
    return pl.pallas_call(
        body, name="in_proj", grid=(t // tm,),
        out_shape=[_sds((t, dm), BF16), _sds((t, n), F32)]
        + [_sds((PAIRS, d, t // d, BAND), BF16) for _ in range(3) for d in DILATIONS],
        in_specs=[row, pl.BlockSpec((1, dm), lambda i: (0, 0)),
                  pl.BlockSpec((dm, n), lambda i: (0, 0), pipeline_mode=_resident(True)), vec, vec,
                  pl.BlockSpec((D_ATTN, D_ATTN), lambda i: (0, 0), pipeline_mode=_resident(True))],
        out_specs=[row, pl.BlockSpec((tm, n), lambda i: (i, 0))]
        + [pl.BlockSpec((PAIRS, d, tm // d, BAND), lambda i: (0, 0, i, 0))
           for _ in range(3) for d in DILATIONS],
        scratch_shapes=[pltpu.VMEM((PAIRS, tm, BAND), F32)] * 2,
        compiler_params=_params("parallel"),
    )(x, g, w, gq, gk, ones_bd)


TOK = 2048
UNITS = TOK // BAND


def _stack_masks():
    row = lax.broadcasted_iota(jnp.int32, (2 * BAND, 2 * BAND), 0) & (BAND - 1)
    col = lax.broadcasted_iota(jnp.int32, (2 * BAND, 2 * BAND), 1)
    lane = lax.broadcasted_iota(jnp.int32, (BAND, BAND), 1)
    head0 = lane < HEAD_DIM
    ones = [jnp.where(head0, 1.0, 0.0).astype(BF16), jnp.where(head0, 0.0, 1.0).astype(BF16)]
    return col - row, col, head0, ones


def _split3(x):
    hi = x.astype(BF16).astype(F32)
    mid = (x - hi).astype(BF16).astype(F32)
    return hi, mid, x - hi - mid


def _gather(srcs, dst, d, before=1):
    per, step, span = TOK // d, d // before, TOK // before
    at = 0
    for r in range(d):
        start = (r % before) * span + r // before
        for src in srcs:
            rows = src[pl.ds(start, per, stride=step), :] if step > 1 else src[pl.ds(start, per), :]
            dst[pl.ds(at, per), :] = rows.astype(dst.dtype)
            at += per


def _scatter(out_ref, src, d):
    per = TOK // d
    if d == 1:
        out_ref[...] = src[...]
        return
    for r in range(d):
        out_ref[pl.ds(r, per, stride=d), :] = src[pl.ds(r * per, per), :]


def _dilated_specs(nblk, reverse):
    def at(s):
        return (nblk - 1 - s) if reverse else s
    main = [pl.BlockSpec((1, d, TOK // d, BAND), lambda j, s: (j, 0, at(s), 0)) for d in DILATIONS]
    prev = [pl.BlockSpec((1, d, TOK // d, BAND), lambda j, s: (j, 0, jnp.maximum(at(s) - 1, 0), 0))
            for d in DILATIONS]
    return main, prev


def _window_rows(prev_ref, main_ref, dst, d):
    per = TOK // d
    for r in range(d):
        dst[pl.ds(r * (per + BAND), BAND), :] = prev_ref[0, r, pl.ds(per - BAND, BAND), :]
        dst[pl.ds(r * (per + BAND) + BAND, per), :] = main_ref[0, r]


def _attn_fwd(qs, ks, vs, carry=None):
    t = qs[0].shape[2]
    nblk = t // TOK
    nd = len(DILATIONS)

    def body(*refs):
        q_refs, kp_refs, k_refs = refs[:nd], refs[nd:2 * nd], refs[2 * nd:3 * nd]
        vp_refs, v_refs = refs[3 * nd:4 * nd], refs[4 * nd:5 * nd]
        y_ref, l_ref, kw_s, vw_s, ob, lb, on, ln = refs[5 * nd:]
        i = pl.program_id(1)
        diff, col, head0, hm = _stack_masks()
        band_ok = jnp.logical_and(diff >= 0, diff <= BAND)
        for g, d in enumerate(DILATIONS):
            per = TOK // d
            nb = per // BAND
            pad = per + BAND
            _window_rows(kp_refs[g], k_refs[g], kw_s, d)
            _window_rows(vp_refs[g], v_refs[g], vw_s, d)
            q_ref = q_refs[g]

            def unit(u, carry):
                r, b = u // nb, u % nb
                qu = q_ref[0, r, pl.ds(pl.multiple_of(b * BAND, BAND), BAND), :]
                start = pl.multiple_of(r * pad + b * BAND, BAND)
                kw = kw_s[pl.ds(start, 2 * BAND), :]
                vw = vw_s[pl.ds(start, 2 * BAND), :]
                lo = jnp.where(jnp.logical_and(i == 0, b == 0), BAND, 0)
                s = _dot_nt(jnp.concatenate([qu * hm[0], qu * hm[1]], axis=0), kw)
                s = jnp.where(jnp.logical_and(band_ok, col >= lo), s, NEG)
                mx = jnp.max(s, axis=-1, keepdims=True)
                e = jnp.exp(s - mx)
                den = jnp.sum(e, axis=-1, keepdims=True)
                o2 = _dot(e.astype(BF16), vw) / den
                l2 = jnp.broadcast_to(mx + jnp.log(den), (2 * BAND, BAND))
                rows = pl.ds(pl.multiple_of(u * BAND, BAND), BAND)
                ob[rows, :] = jnp.where(head0, o2[:BAND], o2[BAND:])
                lb[rows, :] = jnp.where(head0, l2[:BAND], l2[BAND:])
                return carry

            lax.fori_loop(0, UNITS, unit, 0, unroll=16)
            _scatter(on.at[g], ob, d)
            _scatter(ln.at[g], lb, d)
        ls = [ln[0], ln[1], ln[2]]
        mx = jnp.maximum(jnp.maximum(ls[0], ls[1]), ls[2])
        es = [jnp.exp(l - mx) for l in ls]
        tot = es[0] + es[1] + es[2]
        y_ref[...] = (es[0] * on[0] + es[1] * on[1] + es[2] * on[2]) / tot
        l_ref[...] = mx + jnp.log(tot)

    main, prev = _dilated_specs(nblk, False)
    out = pl.BlockSpec((TOK, BAND), lambda j, i: (i, j))
    win_rows = max(d * (TOK // d + BAND) for d in DILATIONS)
    return _call(
        body, list(qs) + list(ks) + list(ks) + list(vs) + list(vs), name="attn_fwd",
        grid=(PAIRS, nblk), out_shape=[_sds((t, D_ATTN), F32)] * 2,
        in_specs=main + prev + main + prev + main, out_specs=[out, out],
        scratch_shapes=[pltpu.VMEM((win_rows, BAND), BF16)] * 2 + [pltpu.VMEM((TOK, BAND), F32)] * 2
        + [pltpu.VMEM((nd, TOK, BAND), F32)] * 2,
        semantics=("parallel", "parallel"), carry=carry)


def _attn_bwd(qs, ks, vs, do, lse, dd, carry=None):
    t = qs[0].shape[2]
    nblk = t // TOK
    nd = len(DILATIONS)
    offs = [sum(DILATIONS[:g]) * BAND for g in range(nd)]

    def body(*refs):
        q_refs, kp_refs, k_refs = refs[:nd], refs[nd:2 * nd], refs[2 * nd:3 * nd]
        vp_refs, v_refs = refs[3 * nd:4 * nd], refs[4 * nd:5 * nd]
        (do_ref, l_ref, d_ref, dq_ref, dk_ref, dv_ref, kw_s, vw_s, dos, lds, pn, dqb, dkb, dvb, ckb,
         cvb, *more) = refs[5 * nd:]
        folds, mids = more[:6], more[6:]
        step = pl.program_id(1)
        i = nblk - 1 - step
        key = lax.broadcasted_iota(jnp.int32, (2 * BAND, 2 * BAND), 0)
        qry = lax.broadcasted_iota(jnp.int32, (2 * BAND, 2 * BAND), 1) & (BAND - 1)
        off = key - qry
        band_ok = jnp.logical_and(off >= 0, off <= BAND)
        lane = lax.broadcasted_iota(jnp.int32, (BAND, BAND), 1)
        head0 = lane < HEAD_DIM
        hm = [jnp.where(head0, 1.0, 0.0).astype(BF16), jnp.where(head0, 0.0, 1.0).astype(BF16)]
        lane2 = lax.broadcasted_iota(jnp.int32, (2 * BAND, BAND), 1) & (HEAD_DIM - 1)
        ones_l = jnp.where(lane2 < 3, 1.0, 0.0).astype(BF16)
        ones_d = jnp.where(jnp.logical_and(lane2 >= 3, lane2 < 6), 1.0, 0.0).astype(BF16)
        piece = lax.broadcasted_iota(jnp.int32, (TOK, BAND), 1) & (HEAD_DIM - 1)

        def pieces(x, at):
            hi, mid, lo = _split3(-x)
            return jnp.where(piece == at, hi,
                             jnp.where(piece == at + 1, mid, jnp.where(piece == at + 2, lo, 0.0)))

        pn[...] = pieces(l_ref[...], 0) + pieces(d_ref[...], 3)
        order = sorted(range(nd), key=lambda a: -DILATIONS[a])
        assert DILATIONS[order[-1]] == 1
        levels = {1: (do_ref, pn)}
        for n, a in enumerate(reversed(order[1:-1])):
            d, before = DILATIONS[a], DILATIONS[order[-1 - n]]
            levels[d] = (mids[2 * n], mids[2 * n + 1])
            for src, dst in zip(levels[before], levels[d]):
                _gather([src], dst, d, before)
        for pos, g in enumerate(order):
            d = DILATIONS[g]
            per = TOK // d
            nb = per // BAND
            pad = per + BAND
            _window_rows(kp_refs[g], k_refs[g], kw_s, d)
            _window_rows(vp_refs[g], v_refs[g], vw_s, d)
            known = d if d in levels else DILATIONS[order[pos + 1]]
            _gather([levels[known][0]], dos, d, known)
            _gather([levels[known][1]], lds, d, known)
            for r in range(d):
                spare = pl.ds(r * pad, BAND)
                dkb[spare, :] = jnp.zeros((BAND, BAND), F32)
                dvb[spare, :] = jnp.zeros((BAND, BAND), F32)
            q_ref = q_refs[g]

            def unit(u, c_):
                r, b = u // nb, u % nb
                rows = pl.ds(pl.multiple_of(u * BAND, BAND), BAND)
                qu = q_ref[0, r, pl.ds(pl.multiple_of(b * BAND, BAND), BAND), :]
                dou, ldu = dos[rows, :], lds[rows, :]
                q2 = jnp.concatenate([qu * hm[0], qu * hm[1]], axis=0)
                do2 = jnp.concatenate([dou * hm[0], dou * hm[1]], axis=0)
                ld2 = jnp.concatenate([ldu * hm[0], ldu * hm[1]], axis=0)
                acc = pl.ds(pl.multiple_of(r * pad + b * BAND, BAND), 2 * BAND)
                kw = kw_s[acc, :]
                vw = vw_s[acc, :]
                lo = jnp.where(jnp.logical_and(i == 0, b == 0), BAND, 0)
                ok = jnp.logical_and(band_ok, key >= lo)
                st = _dot_nt(jnp.concatenate([kw, ones_l], axis=1), jnp.concatenate([q2, ld2], axis=1))
                dpt = _dot_nt(jnp.concatenate([vw, ones_d], axis=1), jnp.concatenate([do2, ld2], axis=1))
                pt = jnp.where(ok, jnp.exp(st), 0.0)
                dst = (pt * dpt).astype(BF16)
                low = pl.ds(pl.multiple_of(r * pad + b * BAND, BAND), BAND)
                high = pl.ds(pl.multiple_of(r * pad + (b + 1) * BAND, BAND), BAND)
                dkw = _dot(dst, q2)
                dvw = _dot(pt.astype(BF16), do2)
                dkb[low, :] += dkw[:BAND]
                dvb[low, :] += dvw[:BAND]
                dkb[high, :] = dkw[BAND:]
                dvb[high, :] = dvw[BAND:]
                dq2 = _dot_tn(dst, kw)
                dqb[rows, :] = jnp.where(head0, dq2[:BAND], dq2[BAND:])
                return c_

            lax.fori_loop(0, UNITS, unit, 0, unroll=16)

            for r in range(d):
                last = pl.ds(r * pad + per, BAND)
                kept = pl.ds(offs[g] + r * BAND, BAND)

                @pl.when(step > 0)
                def _():
                    dkb[last, :] += ckb[kept, :]
                    dvb[last, :] += cvb[kept, :]

                ckb[kept, :] = dkb[pl.ds(r * pad, BAND), :]
                cvb[kept, :] = dvb[pl.ds(r * pad, BAND), :]
            narrower = DILATIONS[order[pos + 1]] if pos + 1 < nd else None
            for n, (buf, out_ref, stride, at) in enumerate(
                    ((dqb, dq_ref, per, 0), (dkb, dk_ref, pad, BAND), (dvb, dv_ref, pad, BAND))):
                wider, onward = folds[2 * n + pos % 2], folds[2 * n + (pos + 1) % 2]
                for r in range(d):
                    val = buf[pl.ds(r * stride + at, per), :]
                    if pos > 0:
                        val = val + wider[pl.ds(r * per, per), :]
                    if narrower is None:
                        out_ref[...] = val
                    else:
                        start = (r % narrower) * (TOK // narrower) + r // narrower
                        onward[pl.ds(start, per, stride=d // narrower), :] = val

    main, prev = _dilated_specs(nblk, True)
    tok = pl.BlockSpec((TOK, BAND), lambda j, s: (nblk - 1 - s, j))
    acc_rows = max(d * (TOK // d + BAND) for d in DILATIONS)
    kept_rows = sum(DILATIONS) * BAND
    return _call(
        body, list(qs) + list(ks) + list(ks) + list(vs) + list(vs) + [do, lse, dd], name="attn_bwd",
        grid=(PAIRS, nblk), out_shape=[_sds((t, D_ATTN), F32)] * 3,
        in_specs=main + prev + main + prev + main + [tok] * 3, out_specs=[tok] * 3,
        scratch_shapes=[pltpu.VMEM((acc_rows, BAND), BF16)] * 2 + [pltpu.VMEM((TOK, BAND), BF16)] * 2
        + [pltpu.VMEM((TOK, BAND), F32)] * 2 + [pltpu.VMEM((acc_rows, BAND), F32)] * 2
        + [pltpu.VMEM((kept_rows, BAND), F32)] * 2
        + [pltpu.VMEM((TOK, BAND), F32)] * (6 + 2 * (nd - 2)),
        semantics=("parallel", "arbitrary"), carry=carry)


def _halo_rows(tm, t):
    per = tm // 8
    prev = lambda i: (jnp.maximum(i * per - 1, 0), 0)
    nxt = lambda i: (jnp.minimum((i + 1) * per, t // 8 - 1), 0)
    return prev, nxt


def _mixer_out(z, cw, y_attn, g_conv, g_attn, tm, carry=None):
    t = z.shape[0]
    prev, _ = _halo_rows(tm, t)

    def body(z_ref, zp_ref, cw_ref, y_ref, gc_ref, ga_ref, mix_ref):
        i = pl.program_id(0)
        keep = jnp.where(i > 0, 1.0, 0.0)
        cu = jnp.concatenate([zp_ref[:, 0:512] * zp_ref[:, 1024:1536] * keep,
                              z_ref[:, 0:512] * z_ref[:, 1024:1536]], axis=0)
        c = (cw_ref[0:1, :] * pltpu.roll(cu, 2, 0) + cw_ref[1:2, :] * pltpu.roll(cu, 1, 0)
             + cw_ref[2:3, :] * cu)[8:, :]
        yc = z_ref[:, 512:1024] * c
        mix_ref[:, 0:512] = (yc * _rms_scale(yc) * gc_ref[...]).astype(BF16)
        ya = y_ref[...]
        mix_ref[:, 512:1024] = (ya * _rms_scale(ya) * ga_ref[...]).astype(BF16)

    blk = pl.BlockSpec((tm, 512), lambda i: (i, 0))
    vec = pl.BlockSpec((1, 512), lambda i: (0, 0))
    return _call(
        body, [z, z, cw, y_attn, g_conv, g_attn], name="mixer_out", grid=(t // tm,),
        out_shape=_sds((t, 1024), BF16),
        in_specs=[pl.BlockSpec((tm, 1536), lambda i: (i, 0)), pl.BlockSpec((8, 1536), prev),
                  pl.BlockSpec((8, 512), lambda i: (0, 0)), blk, vec, vec],
        out_specs=pl.BlockSpec((tm, 1024), lambda i: (i, 0)),
        semantics=("parallel",), carry=carry)


def _mixer_bwd(z, dx1, wout, y_attn, cw, g_conv, g_attn, ones_bd, tm, carry=None):
    t = z.shape[0]
    nblk = t // tm
    prev, nxt = _halo_rows(tm, t)
    e = tm + 16

    def body(z_ref, zp_ref, zn_ref, dx_ref, dxn_ref, w_ref, y_ref, cw_ref, gc_ref, ga_ref, bd_ref,
             dz_ref, do_ref, dd_ref, dcw_ref, dgc_ref, dga_ref):
        i = pl.program_id(0)
        dm = _dot_nt(dx_ref[...], w_ref[...])
        dmn = _dot_nt(dxn_ref[...], w_ref[0:D_CONV, :])[0:8, :]
        rows = lax.broadcasted_iota(jnp.int32, (e, 1), 0)
        lo = jnp.where(i > 0, 0, 8)
        hi = jnp.where(i < nblk - 1, e, tm + 8)
        ze = jnp.concatenate([zp_ref[...], z_ref[...], zn_ref[...]], axis=0)
        u, gb, gcv = ze[:, 0:512], ze[:, 512:1024], ze[:, 1024:1536]
        w0, w1, w2 = cw_ref[0:1, :], cw_ref[1:2, :], cw_ref[2:3, :]
        cu = jnp.where(rows >= lo, gcv * u, 0.0)
        cu1, cu2 = pltpu.roll(cu, 1, 0), pltpu.roll(cu, 2, 0)
        c = w0 * cu2 + w1 * cu1 + w2 * cu
        yc = gb * c
        dma = jnp.concatenate([jnp.zeros((8, 512), F32), dm[:, 0:512], dmn], axis=0)
        dyc, ych = _rms_bwd(yc, _rms_scale(yc), gc_ref[...], dma)
        dc = jnp.where(jnp.logical_and(rows >= 8, rows < hi), dyc * gb, 0.0)
        dcu = w0 * pltpu.roll(dc, e - 2, 0) + w1 * pltpu.roll(dc, e - 1, 0) + w2 * dc
        mid = slice(8, 8 + tm)
        dz_ref[:, 0:512] = (dcu * gcv)[mid, :].astype(BF16)
        dz_ref[:, 512:1024] = (dyc * c)[mid, :].astype(BF16)
        dz_ref[:, 1024:1536] = (dcu * u)[mid, :].astype(BF16)

        ya = y_ref[...]
        dmb = dm[:, 512:1024]
        dya, yah = _rms_bwd(ya, _rms_scale(ya), ga_ref[...], dmb)
        do_ref[...] = dya
        dd_ref[...] = _head_sum(dya * ya, bd_ref[...])

        @pl.when(i == 0)
        def _():
            dcw_ref[...] = jnp.zeros_like(dcw_ref)
            dgc_ref[...] = jnp.zeros_like(dgc_ref)
            dga_ref[...] = jnp.zeros_like(dga_ref)

        dcm = jnp.where(rows < tm + 8, dc, 0.0)
        dcw_ref[0:1, :] += jnp.sum(dcm * cu2, axis=0, keepdims=True)
        dcw_ref[1:2, :] += jnp.sum(dcm * cu1, axis=0, keepdims=True)
        dcw_ref[2:3, :] += jnp.sum(dcm * cu, axis=0, keepdims=True)
        dgc_ref[...] += jnp.sum((dma * ych)[mid, :], axis=0, keepdims=True)
        dga_ref[...] += jnp.sum(dmb * yah, axis=0, keepdims=True)

    blk = pl.BlockSpec((tm, 512), lambda i: (i, 0))
    vec = pl.BlockSpec((1, 512), lambda i: (0, 0))
    cwb = pl.BlockSpec((8, 512), lambda i: (0, 0))
    next16 = lambda i: (jnp.minimum((i + 1) * (tm // 16), t // 16 - 1), 0)
    return _call(
        body, [z, z, z, dx1, dx1, wout, y_attn, cw, g_conv, g_attn, ones_bd], name="mixer_bwd",
        grid=(nblk,),
        out_shape=[_sds((t, D_IN), BF16), _sds((t, 512), F32), _sds((t, 512), F32),
                   _sds((8, 512), F32), _sds((1, 512), F32), _sds((1, 512), F32)],
        in_specs=[pl.BlockSpec((tm, 1536), lambda i: (i, 0)), pl.BlockSpec((8, 1536), prev),
                  pl.BlockSpec((8, 1536), nxt), pl.BlockSpec((tm, D_MODEL), lambda i: (i, 0)),
                  pl.BlockSpec((16, D_MODEL), next16),
                  pl.BlockSpec(wout.shape, lambda i: (0, 0), pipeline_mode=_resident(True)),
                  blk, cwb, vec, vec, pl.BlockSpec((512, 512), lambda i: (0, 0))],
        out_specs=[pl.BlockSpec((tm, 1536), lambda i: (i, 0)), blk, blk, cwb, vec, vec],
        carry=carry)


def _qkv_bwd(z, dz, dqn, dkn, dv, gq, gk, ones_bd, tm, carry=None):
    t = z.shape[0]

    def body(zq_ref, zk_ref, _, dqn_ref, dkn_ref, dv_ref, gq_ref, gk_ref, bd_ref,
             dz_ref, dgq_ref, dgk_ref):
        bd = bd_ref[...]

        @pl.when(pl.program_id(0) == 0)
        def _():
            dgq_ref[...] = jnp.zeros_like(dgq_ref)
            dgk_ref[...] = jnp.zeros_like(dgk_ref)

        def back(v, dn, g, scale):
            r = _head_rms_scale(v, bd)
            vh = v * r
            dh = dn * (g * scale)
            dv = r * (dh - vh * (_head_sum(dh * vh, bd) * (1.0 / HEAD_DIM)))
            return dv, jnp.sum(dn * scale * vh, axis=0, keepdims=True)

        dq, dgq = back(zq_ref[...], dqn_ref[...], gq_ref[...], HEAD_DIM ** -0.5)
        dk, dgk = back(zk_ref[...], dkn_ref[...], gk_ref[...], 1.0)
        dgq_ref[...] += dgq
        dgk_ref[...] += dgk
        dz_ref[:, 0:512] = dq.astype(BF16)
        dz_ref[:, 512:1024] = dk.astype(BF16)
        dz_ref[:, 1024:1536] = dv_ref[...].astype(BF16)

    blk = pl.BlockSpec((tm, 512), lambda i: (i, 0))
    vec = pl.BlockSpec((1, 512), lambda i: (0, 0))
    return _call(
        body, [z, z, dz, dqn, dkn, dv, gq, gk, ones_bd], name="qkv_bwd", grid=(t // tm,),
        out_shape=[_sds((t, D_IN), BF16), _sds((1, 512), F32), _sds((1, 512), F32)],
        in_specs=[pl.BlockSpec((tm, 512), lambda i: (i, 3)), pl.BlockSpec((tm, 512), lambda i: (i, 4)),
                  ANY] + [blk] * 3 + [vec, vec, pl.BlockSpec((512, 512), lambda i: (0, 0))],
        out_specs=[pl.BlockSpec((tm, 1536), lambda i: (i, 1)), vec, vec],
        carry=carry, aliases={2: 0})


def _columns_from_chips(g):
    return g.transpose(1, 0, 2).reshape(g.shape[1], N_CHIPS * g.shape[2])


def kernel(x, g_mix, w_in, conv_w, g_q, g_k, g_conv_out, g_attn_out, w_out, g_ffn, w_gate, w_up, w_down, loss_target, m_g_mix, m_w_in, m_conv_w, m_g_q, m_g_k, m_g_conv_out, m_g_attn_out, m_w_out, m_g_ffn, m_w_gate, m_w_up, m_w_down, v_g_mix, v_w_in, v_conv_w, v_g_q, v_g_k, v_g_conv_out, v_g_attn_out, v_w_out, v_g_ffn, v_w_gate, v_w_up, v_w_down):
    t = x.shape[1]
    xs = x[0]
    target = loss_target[0]
    tm = min(512, t)
    tm_wide = min(1024, t)
    tmm = min(2048, t)

    cw_pad = jnp.pad(conv_w[0], ((0, 13), (0, 0)))
    gathered = _all_gather([w_in[0].astype(BF16), cw_pad])
    win = _columns_from_chips(gathered[0])
    cw = jnp.pad(gathered[1][:, 0:3, :].transpose(1, 0, 2).reshape(3, D_CONV), ((0, 5), (0, 0)))
    later = [w_out[0].astype(BF16), w_gate[0].T.astype(BF16), w_up[0].T.astype(BF16),
             w_down[0].astype(BF16)]

    head_id = jnp.arange(D_ATTN) // HEAD_DIM
    ones_bd = (head_id[:, None] == head_id[None, :]).astype(BF16)
    gq_t = jnp.tile(g_q, (1, D_ATTN // HEAD_DIM))
    gk_t = jnp.tile(g_k, (1, D_ATTN // HEAD_DIM))

    h1, z, *dilated = _in_proj(xs, g_mix, win, gq_t, gk_t, ones_bd, tm)
    nd = len(DILATIONS)
    qs, ks, vs = dilated[:nd], dilated[nd:2 * nd], dilated[2 * nd:]
    (y_attn, lse), gathered = _attn_fwd(qs, ks, vs, carry=_x_gather_chips(later))
    mix, gathered = _mixer_out(z, cw, y_attn, g_conv_out, g_attn_out, tm,
                               carry=_x_gather_sibling(gathered))
    wout = gathered[0].reshape(D_MODEL, D_MODEL)
    wgate_t = gathered[1].reshape(D_FF, D_MODEL)
    wup_t = gathered[2].reshape(D_FF, D_MODEL)
    wdown = gathered[3].reshape(D_FF, D_MODEL)
    (x1,) = _matmul("out_proj", mix, wout, [xs], [F32], lambda acc, r: (r + acc,), tm, D_MODEL)
    h2, gate, up, act = _norm_matmul("ffn_up", x1, g_ffn, [wgate_t, wup_t], tm, D_FF, True, BF16,
                                     transposed_w=True)

    def loss_epilogue(acc, r, tgt):
        err = r + acc - tgt
        dy = err * (1.0 / D_MODEL)
        return dy, dy, jnp.sum(err * err)

    dx2, dx2b, loss_sum = _matmul("ffn_down_loss", act, wdown, [x1, target], [F32, BF16],
                                  loss_epilogue, tm_wide, D_MODEL, loss=True)

    def swiglu_bwd(da, gt, u):
        gt, u = gt.astype(F32), u.astype(F32)
        s = _sigmoid(gt)
        return da * u * (s * (1.0 + gt * (1.0 - s))), da * (gt * s)

    dgate, dup = _matmul("ffn_down_bwd", dx2b, wdown, [gate, up], [BF16, BF16], swiglu_bwd,
                         tm, D_FF, transposed_w=True)
    gw_down = _matmul_tn("grad_w_down", act, dx2b, 512, tmm)
    gw_gate_t = _matmul_tn("grad_w_gate", dgate, h2, 512, tmm)
    gw_up_t = _matmul_tn("grad_w_up", dup, h2, 512, tmm)

    me = 2 * lax.axis_index("x") + lax.axis_index("y")
    where = jnp.stack([lax.axis_index("c"), me]).astype(jnp.int32)

    def pair_sums(names, full, got):
        return [_pair_sum(f"pair_sum_{nme}", a, b, where) for nme, a, b in zip(names, full, got)]

    def chip_sums(names, pair, got):
        return [_chip_sum(f"chip_sum_{nme}", own, b) for nme, (_, own), b in zip(names, pair, got)]

    ffn = ["w_gate", "w_up", "w_down"]
    full = [g.reshape(N_CHIPS, D_FF // N_CHIPS, D_MODEL) for g in (gw_gate_t, gw_up_t, gw_down)]
    (dx1, dx1b, gg_ffn), got = _matmul_norm_bwd(
        "ffn_up_bwd", [(dgate, wgate_t), (dup, wup_t)], x1, dx2, g_ffn, tm, carry=_x_pair(full),
        transposed_w=False)
    pair = pair_sums(ffn, full, got)
    gw_out = _matmul_tn("grad_w_out", mix, dx1b, 512, tmm)
    full = [gw_out.reshape(N_CHIPS, D_MODEL // N_CHIPS, D_MODEL)]
    (dzc, do, dd, gcw, gg_conv, gg_attn), got = _mixer_bwd(
        z, dx1b, wout, y_attn, cw, g_conv_out, g_attn_out, ones_bd, tm, carry=_x_pair(full))
    pair += pair_sums(["w_out"], full, got)
    early = ffn + ["w_out"]
    (dqn, dkn, dv), got = _attn_bwd(qs, ks, vs, do, lse, dd, carry=_x_chips([p for p, _ in pair]))
    mine = chip_sums(early, pair, got)
    (dz, gg_q, gg_k), theirs = _qkv_bwd(z, dzc, dqn, dkn, dv, gq_t, gk_t, ones_bd, tm,
                                        carry=_x_share(mine))
    full = [_matmul_tn("grad_w_in", h1, dz, D_IN // N_CHIPS, tmm, by_chip=True)]
    got = _exchange_alone("grad_pair_exchange_w_in", _x_pair(full))
    pair = pair_sums(["w_in"], full, got)
    (grad_x, _, gg_mix), got = _matmul_norm_bwd("in_proj_bwd", [(dz, win)], xs, dx1, g_mix, tm_wide,
                                                carry=_x_chips([pair[0][0]]))
    mine += chip_sums(["w_in"], pair, got)
    theirs = list(theirs) + list(_exchange_alone("grad_pair_share_w_in", _x_share(mine[-1:])))
    big = early + ["w_in"]

    small = _small_all_reduce({
        "g_mix": gg_mix, "g_ffn": gg_ffn, "g_conv_out": gg_conv, "g_attn_out": gg_attn,
        "g_q": gg_q, "g_k": gg_k, "loss": loss_sum, "conv_w": gcw})
    heads = D_ATTN // HEAD_DIM
    grads = {
        "g_mix": small[0:1, :], "g_ffn": small[1:2, :],
        "g_conv_out": small[2:3, 0:512], "g_attn_out": small[2:3, 512:1024],
        "g_q": small[3, 0:512].reshape(heads, HEAD_DIM).sum(axis=0)[None, :],
        "g_k": small[3, 512:1024].reshape(heads, HEAD_DIM).sum(axis=0)[None, :],
        "conv_w": lax.dynamic_slice(small[8:11, 0:512], (0, me * (D_CONV // N_CHIPS)),
                                    (3, D_CONV // N_CHIPS)),
    }
    halves = dict(zip(big, zip(mine, theirs)))
    loss = small[4, 0] * 0.5 * (1.0 / D_MODEL)

    weights = dict(g_mix=g_mix, w_in=w_in, conv_w=conv_w, g_q=g_q, g_k=g_k, g_conv_out=g_conv_out,
                   g_attn_out=g_attn_out, w_out=w_out, g_ffn=g_ffn, w_gate=w_gate, w_up=w_up,
                   w_down=w_down)
    moments_m = dict(g_mix=m_g_mix, w_in=m_w_in, conv_w=m_conv_w, g_q=m_g_q, g_k=m_g_k,
                     g_conv_out=m_g_conv_out, g_attn_out=m_g_attn_out, w_out=m_w_out, g_ffn=m_g_ffn,
                     w_gate=m_w_gate, w_up=m_w_up, w_down=m_w_down)
    moments_v = dict(g_mix=v_g_mix, w_in=v_w_in, conv_w=v_conv_w, g_q=v_g_q, g_k=v_g_k,
                     g_conv_out=v_g_conv_out, g_attn_out=v_g_attn_out, w_out=v_w_out, g_ffn=v_g_ffn,
                     w_gate=v_w_gate, w_up=v_w_up, w_down=v_w_down)
    names = list(weights)
    out_g, out_d, out_m, out_v = [], [], [], []
    for nme in names:
        wgt = weights[nme]
        shape2 = wgt.shape[-2:] if wgt.ndim == 3 else wgt.shape
        flip = nme in ("w_gate", "w_up")

        def to2d(a):
            return a.reshape(shape2).T if flip else a.reshape(shape2)

        def back(a):
            return (a.T if flip else a).reshape(wgt.shape)

        state = (to2d(wgt), to2d(moments_m[nme]), to2d(moments_v[nme]))
        if nme in halves:
            g2, dlt, nm, nv = _adamw_shard(f"adamw_{nme}", *state, *halves[nme], where)
        else:
            g2 = grads[nme].reshape(shape2)
            dlt, nm, nv = _adamw(f"adamw_{nme}", state[0], g2, state[1], state[2])
        out_g.append(back(g2))
        out_d.append(back(dlt))
        out_m.append(back(nm))
        out_v.append(back(nv))
    return (loss, grad_x[None], *out_g, *out_d, *out_m, *out_v)
```

```python
import functools
from typing import Any, Callable, NamedTuple, Sequence

import jax
import jax.numpy as jnp
from jax import lax
from jax.experimental import pallas as pl
from jax.experimental.pallas import tpu as pltpu

F32 = jnp.float32
BF16 = jnp.bfloat16
MESH = pl.DeviceIdType.MESH

D_MODEL = 1024
D_CONV = 512
D_ATTN = 512
HEAD_DIM = 64
D_FF = 2816
D_IN = 3 * D_CONV + 3 * D_ATTN
DILATIONS = (1, 4, 16)
BAND = 128
EPS = 1e-6
NEG = -1e30
N_CHIPS = 4

ADAM_LR = 0.001
ADAM_B1 = 0.9
ADAM_B2 = 0.999
ADAM_EPS = 1e-08
ADAM_WD = 0.01
ADAM_STEP = 10

V7X_VMEM_BYTES = 64 * 1024 * 1024
VMEM_LIMIT = V7X_VMEM_BYTES - 8 * 1024 * 1024
ANY = pl.BlockSpec(memory_space=pl.ANY)
VMEM_WHOLE = pl.BlockSpec(memory_space=pltpu.VMEM)


def _params(*sem):
    return pltpu.CompilerParams(dimension_semantics=sem, vmem_limit_bytes=VMEM_LIMIT)


def _sds(shape, dtype):
    return jax.ShapeDtypeStruct(shape, dtype)


def _resident(whole):
    return pl.Buffered(1) if whole else None


def _place():
    x, y, c = lax.axis_index("x"), lax.axis_index("y"), lax.axis_index("c")
    chips = [(1 - x, y), (x, 1 - y), (1 - x, 1 - y)]
    return x, y, c, 2 * x + y, chips, [2 * cx + cy for cx, cy in chips]


def _all_gather(shards):
    n = len(shards)

    def body(*refs):
        ins, outs, stage = refs[:n], refs[n:2 * n], refs[2 * n:3 * n]
        ssem, rsem, fsem, gsem, lsem, osem = refs[3 * n:]
        x, y, c, me, chips, cids = _place()
        sib = (x, y, 1 - c)

        def half(w, which):
            h = shards[w].shape[0] // 2
            return pl.ds(pl.multiple_of(which * h, 8), h)

        loads = [pltpu.make_async_copy(ins[w], stage[w], lsem.at[w]) for w in range(n)]
        local = [pltpu.make_async_copy(stage[w], outs[w].at[me], osem.at[w]) for w in range(n)]
        for cp in loads:
            cp.start()

        def chip_copy(w, j, src_slot):
            rows = half(w, c)
            return pltpu.make_async_remote_copy(
                src_ref=ins[w].at[rows], dst_ref=outs[w].at[src_slot, rows],
                send_sem=ssem.at[3 * w + j], recv_sem=rsem.at[3 * w + j],
                device_id=(*chips[j], c), device_id_type=MESH)

        def sib_copy(w, j, which):
            rows = half(w, which)
            return pltpu.make_async_remote_copy(
                src_ref=outs[w].at[cids[j], rows], dst_ref=outs[w].at[cids[j], rows],
                send_sem=fsem.at[3 * w + j], recv_sem=gsem.at[3 * w + j],
                device_id=sib, device_id_type=MESH)

        sends = [chip_copy(w, j, me) for w in range(n) for j in range(3)]
        for cp in sends:
            cp.start()
        for w in range(n):
            loads[w].wait()
            local[w].start()
        passed = []
        for w in range(n):
            for j in range(3):
                chip_copy(w, j, cids[j]).wait_recv()
                cp = sib_copy(w, j, c)
                cp.start()
                passed.append(cp)
        for w in range(n):
            for j in range(3):
                sib_copy(w, j, 1 - c).wait_recv()
        for cp in sends + passed:
            cp.wait_send()
        for cp in local:
            cp.wait()

    return pl.pallas_call(
        body, name="all_gather_weights",
        out_shape=[_sds((N_CHIPS,) + s.shape, s.dtype) for s in shards],
        in_specs=[ANY] * n, out_specs=[ANY] * n,
        scratch_shapes=[pltpu.VMEM(s.shape, s.dtype) for s in shards]
        + [pltpu.SemaphoreType.DMA((3 * n,))] * 4 + [pltpu.SemaphoreType.DMA((n,))] * 2,
        compiler_params=pltpu.CompilerParams(vmem_limit_bytes=VMEM_LIMIT),
    )(*shards)


class _Exchange(NamedTuple):
    srcs: Sequence[Any]
    lands: Sequence[Any]
    outs: Sequence[Any]
    n_sems: int
    copies: Callable


def _remote(src, dst, ssem, rsem, k, to):
    return pltpu.make_async_remote_copy(src_ref=src, dst_ref=dst, send_sem=ssem.at[k],
                                        recv_sem=rsem.at[k], device_id=to, device_id_type=MESH)


def _x_gather_chips(shards):
    def copies(srcs, lands, outs, ssem, rsem):
        _, _, c, me, chips, cids = _place()
        go, arrive = [], []
        for w, s in enumerate(shards):
            h = s.shape[0] // 2
            rows = pl.ds(pl.multiple_of(c * h, 8), h)
            for j in range(3):
                to = (*chips[j], c)
                go.append(_remote(srcs[w].at[rows], lands[w].at[me, rows], ssem, rsem, 3 * w + j, to))
                arrive.append(_remote(srcs[w].at[rows], lands[w].at[cids[j], rows], ssem, rsem,
                                      3 * w + j, to))
        return go, arrive

    lands = [jnp.broadcast_to(s[None], (N_CHIPS,) + s.shape) for s in shards]
    return _Exchange(shards, lands, [], 3 * len(shards), copies)


def _x_gather_sibling(gathered):
    def copies(srcs, lands, outs, ssem, rsem):
        x, y, c, _, _, cids = _place()
        go, arrive = [], []
        for w, g in enumerate(gathered):
            h = g.shape[1] // 2
            mine = pl.ds(pl.multiple_of(c * h, 8), h)
            theirs = pl.ds(pl.multiple_of((1 - c) * h, 8), h)
            for j in range(3):
                slab = lands[w].at[cids[j]]
                go.append(_remote(slab.at[mine], slab.at[mine], ssem, rsem, 3 * w + j, (x, y, 1 - c)))
                arrive.append(_remote(slab.at[theirs], slab.at[theirs], ssem, rsem, 3 * w + j,
                                      (x, y, 1 - c)))
        return go, arrive

    return _Exchange([], gathered, [], 3 * len(gathered), copies)


def _x_pair(grads):
    def copies(srcs, lands, outs, ssem, rsem):
        x, y, c, _, _, _ = _place()
        go = []
        for w, g in enumerate(grads):
            h = g.shape[1] // 2
            theirs = pl.ds(pl.multiple_of((1 - c) * h, 8), h)
            go.append(_remote(srcs[w].at[:, theirs, :], outs[w], ssem, rsem, w, (x, y, 1 - c)))
        return go, go

    outs = [_sds((N_CHIPS, g.shape[1] // 2, g.shape[2]), g.dtype) for g in grads]
    return _Exchange(grads, [], outs, len(grads), copies)


def _x_chips(parts):
    def copies(srcs, lands, outs, ssem, rsem):
        _, _, c, _, chips, cids = _place()
        go = [_remote(srcs[w].at[cids[j]], outs[w].at[j], ssem, rsem, 3 * w + j, (*chips[j], c))
              for w in range(len(parts)) for j in range(3)]
        return go, go

    outs = [_sds((3,) + p.shape[1:], p.dtype) for p in parts]
    return _Exchange(parts, [], outs, 3 * len(parts), copies)


def _x_share(halves):
    def copies(srcs, lands, outs, ssem, rsem):
        x, y, c, _, _, _ = _place()
        go = [_remote(srcs[w], outs[w], ssem, rsem, w, (x, y, 1 - c)) for w in range(len(halves))]
        return go, go

    return _Exchange(halves, [], [_sds(h.shape, h.dtype) for h in halves], len(halves), copies)


def _call(body, args, *, name, grid, in_specs, out_specs, out_shape, scratch_shapes=(),
          semantics=None, carry=None, aliases=None):
    single = not isinstance(out_shape, (list, tuple))
    out_shape = [out_shape] if single else list(out_shape)
    out_specs = [out_specs] if single else list(out_specs)
    aliases = dict(aliases or {})
    if carry is None:
        res = pl.pallas_call(
            body, name=name, grid=grid, in_specs=list(in_specs), out_specs=out_specs,
            out_shape=out_shape, scratch_shapes=list(scratch_shapes), input_output_aliases=aliases,
            compiler_params=_params(*(semantics or ("arbitrary",) * len(grid))))(*args)
        return res[0] if single else res
    n_in, n_out, n_scr = len(args), len(out_shape), len(scratch_shapes)
    n_src, n_land, n_new = len(carry.srcs), len(carry.lands), len(carry.outs)

    def carrying(*refs):
        at = 0
        parts = []
        for n in (n_in, n_src, n_land, n_out, n_land, n_new, n_scr, 2):
            parts.append(refs[at:at + n])
            at += n
        ins, srcs, _, outs, lands, news, scratch, (ssem, rsem) = parts
        ids = [pl.program_id(a) for a in range(len(grid))]
        first = functools.reduce(jnp.logical_and, [i == 0 for i in ids])
        last = functools.reduce(jnp.logical_and, [i == g - 1 for i, g in zip(ids, grid)])
        go, arrive = carry.copies(srcs, lands, news, ssem, rsem)

        @pl.when(first)
        def _():
            for cp in go:
                cp.start()

        body(*ins, *outs, *scratch)

        @pl.when(last)
        def _():
            for cp in go:
                cp.wait_send()
            for cp in arrive:
                cp.wait_recv()

    res = pl.pallas_call(
        carrying, name=name, grid=grid,
        in_specs=list(in_specs) + [ANY] * (n_src + n_land),
        out_specs=out_specs + [ANY] * (n_land + n_new),
        out_shape=out_shape + [_sds(a.shape, a.dtype) for a in carry.lands] + list(carry.outs),
        input_output_aliases={**aliases, **{n_in + n_src + i: n_out + i for i in range(n_land)}},
        scratch_shapes=list(scratch_shapes) + [pltpu.SemaphoreType.DMA((carry.n_sems,))] * 2,
        compiler_params=_params(*(("arbitrary",) * len(grid))))(*args, *carry.srcs, *carry.lands)
    own = res[:n_out]
    return (own[0] if single else own), res[n_out:]


def _exchange_alone(name, exchange):
    def body(x_ref, o_ref):
        o_ref[...] = x_ref[...]

    blk = pl.BlockSpec((8, 128), lambda i: (0, 0))
    _, res = _call(body, [jnp.zeros((8, 128), F32)], name=name, grid=(1,), in_specs=[blk],
                   out_specs=blk, out_shape=_sds((8, 128), F32), carry=exchange)
    return res


def _row_block(r, want):
    return max(d for d in range(1, min(want, r) + 1) if r % d == 0 and (d % 8 == 0 or d == r))


def _pair_sum(name, full, got, where):
    _, r, n = full.shape
    h = r // 2
    tr = _row_block(h, 512)
    nb = h // tr

    def body(w_ref, a_ref, b_ref, o_ref, own_ref):
        total = a_ref[...] + b_ref[...]
        o_ref[...] = total.astype(BF16)

        @pl.when(pl.program_id(1) == w_ref[1])
        def _():
            own_ref[...] = total[0]

    blk = pl.BlockSpec((1, tr, n), lambda i, s, w: (s, i, 0))
    return pl.pallas_call(
        body, name=name, out_shape=[_sds(got.shape, BF16), _sds((h, n), F32)],
        grid_spec=pltpu.PrefetchScalarGridSpec(
            num_scalar_prefetch=1, grid=(nb, N_CHIPS),
            in_specs=[pl.BlockSpec((1, tr, n), lambda i, s, w: (s, w[0] * nb + i, 0)), blk],
            out_specs=[blk, pl.BlockSpec((tr, n), lambda i, s, w: (i, 0))]),
        compiler_params=_params("parallel", "arbitrary"),
    )(where, full, got)


def _chip_sum(name, own, got):
    h, n = own.shape
    tr = _row_block(h, 256)

    def body(a_ref, b0, b1, b2, o_ref):
        o_ref[...] = ((a_ref[...] + b0[0].astype(F32)) + b1[0].astype(F32)) + b2[0].astype(F32)

    def slot(j):
        return pl.BlockSpec((1, tr, n), lambda i: (j, i, 0))

    blk = pl.BlockSpec((tr, n), lambda i: (i, 0))
    return pl.pallas_call(
        body, name=name, grid=(h // tr,), out_shape=_sds((h, n), F32),
        in_specs=[blk, slot(0), slot(1), slot(2)], out_specs=blk,
        compiler_params=_params("parallel"),
    )(own, got, got, got)


SMALL_ROWS = 16
SMALL_LAYOUT = (
    ("g_mix", 0, 0, 1, 1024), ("g_ffn", 1, 0, 1, 1024), ("g_conv_out", 2, 0, 1, 512),
    ("g_attn_out", 2, 512, 1, 512), ("g_q", 3, 0, 1, 512), ("g_k", 3, 512, 1, 512),
    ("loss", 4, 0, 1, 128), ("conv_w", 8, 0, 8, 512))


def _small_all_reduce(parts):
    names = [s[0] for s in SMALL_LAYOUT]

    def body(*refs):
        ins = refs[:len(names)]
        out_ref, stage, buf, ssem, rsem = refs[len(names):]
        x, y, c, _, _, _ = _place()
        me = 4 * x + 2 * y + c
        stage[...] = jnp.zeros_like(stage)
        for ref, (_, r0, c0, nr, nc) in zip(ins, SMALL_LAYOUT):
            stage[r0:r0 + nr, c0:c0 + nc] = ref[0:nr, :]
        buf[me] = stage[...]
        peers = []
        for d in range(1, 8):
            px = 1 - x if d & 4 else x
            py = 1 - y if d & 2 else y
            pc = 1 - c if d & 1 else c
            peers.append(((px, py, pc), 4 * px + 2 * py + pc))
        sends = [pltpu.make_async_remote_copy(
            src_ref=stage, dst_ref=buf.at[me], send_sem=ssem.at[k], recv_sem=rsem.at[k],
            device_id=peer, device_id_type=MESH) for k, (peer, _) in enumerate(peers)]
        for cp in sends:
            cp.start()
        for k, (peer, pid) in enumerate(peers):
            pltpu.make_async_remote_copy(
                src_ref=stage, dst_ref=buf.at[pid], send_sem=ssem.at[k], recv_sem=rsem.at[k],
                device_id=peer, device_id_type=MESH).wait_recv()
        for cp in sends:
            cp.wait_send()
        acc = buf[0]
        for k in range(1, 8):
            acc = acc + buf[k]
        out_ref[...] = acc

    return pl.pallas_call(
        body, name="small_all_reduce", out_shape=_sds((SMALL_ROWS, 1024), F32),
        in_specs=[VMEM_WHOLE] * len(names), out_specs=VMEM_WHOLE,
        scratch_shapes=[pltpu.VMEM((SMALL_ROWS, 1024), F32), pltpu.VMEM((8, SMALL_ROWS, 1024), F32),
                        pltpu.SemaphoreType.DMA((7,)), pltpu.SemaphoreType.DMA((7,))],
    )(*[parts[k] for k in names])


def _dot(a, b):
    return jnp.dot(a, b, preferred_element_type=F32)


def _dot_nt(a, b):
    return lax.dot_general(a, b, (((1,), (1,)), ((), ())), preferred_element_type=F32)


def _dot_tn(a, b):
    return lax.dot_general(a, b, (((0,), (0,)), ((), ())), preferred_element_type=F32)


def _sigmoid(v):
    return 1.0 / (1.0 + jnp.exp(-v))


def _rms_scale(v):
    return lax.rsqrt(jnp.mean(v * v, axis=-1, keepdims=True) + EPS)


def _rms_bwd(v, r, g, dy):
    vh = v * r
    dh = dy * g
    return r * (dh - vh * jnp.mean(dh * vh, axis=-1, keepdims=True)), vh


def _head_sum(a, ones_bd):
    hi = a.astype(BF16)
    lo = (a - hi.astype(F32)).astype(BF16)
    return _dot(hi, ones_bd) + _dot(lo, ones_bd)


def _head_rms_scale(v, ones_bd):
    return lax.rsqrt(_head_sum(v * v, ones_bd) * (1.0 / HEAD_DIM) + EPS)


MXU_COLUMNS = 256


def _column_chunks(n):
    width = MXU_COLUMNS if n % MXU_COLUMNS == 0 else n
    return [slice(c, c + width) for c in range(0, n, width)]


def _norm_matmul(name, x, g, ws, tm, tn, swiglu, out_dtype=F32, transposed_w=False):
    t, d = x.shape
    n = ws[0].shape[0] if transposed_w else ws[0].shape[1]
    nw = len(ws)

    def body(x_ref, g_ref, *refs):
        w_refs, h_ref, o_refs = refs[:nw], refs[nw], refs[nw + 1:2 * nw + 1]
        hs = refs[-1]

        @pl.when(pl.program_id(1) == 0)
        def _():
            xv = x_ref[...]
            h = (xv * _rms_scale(xv) * g_ref[...]).astype(BF16)
            hs[...] = h
            h_ref[...] = h

        h = hs[...]
        for cols in _column_chunks(tn):
            outs = [_dot_nt(h, w[cols, :]) if transposed_w else _dot(h, w[:, cols]) for w in w_refs]
            for o_ref, o in zip(o_refs, outs):
                o_ref[:, cols] = o.astype(out_dtype)
            if swiglu:
                refs[2 * nw + 1][:, cols] = (outs[0] * _sigmoid(outs[0]) * outs[1]).astype(BF16)

    row = pl.BlockSpec((tm, d), lambda i, j: (i, 0))
    col = pl.BlockSpec((tm, tn), lambda i, j: (i, j))
    out_shape = [_sds((t, d), BF16)] + [_sds((t, n), out_dtype)] * nw
    out_specs = [row] + [col] * nw
    if swiglu:
        out_shape.append(_sds((t, n), BF16))
        out_specs.append(col)
    return pl.pallas_call(
        body, name=name, grid=(t // tm, n // tn), out_shape=out_shape,
        in_specs=[row, pl.BlockSpec((1, d), lambda i, j: (0, 0))]
        + [pl.BlockSpec((tn, d), lambda i, j: (j, 0), pipeline_mode=_resident(tn == n))
           if transposed_w
           else pl.BlockSpec((d, tn), lambda i, j: (0, j), pipeline_mode=_resident(tn == n))] * nw,
        out_specs=out_specs, scratch_shapes=[pltpu.VMEM((tm, d), BF16)],
        compiler_params=_params("parallel", "arbitrary"),
    )(x, g, *ws)


def _matmul(name, a, w, extras, out_dtypes, epilogue, tm, tn, transposed_w=False, loss=False):
    t, k = a.shape
    n = w.shape[0] if transposed_w else w.shape[1]
    ne, no = len(extras), len(out_dtypes)

    def body(a_ref, w_ref, *refs):
        e_refs, o_refs = refs[:ne], refs[ne:]
        a = a_ref[...]
        total = 0.0
        for cols in _column_chunks(tn):
            acc = _dot_nt(a, w_ref[cols, :]) if transposed_w else _dot(a, w_ref[:, cols])
            res = epilogue(acc, *[e[:, cols] for e in e_refs])
            for o_ref, r in zip(o_refs[:no], res[:no]):
                o_ref[:, cols] = r.astype(o_ref.dtype)
            if loss:
                total = total + res[no]
        if loss:
            first = jnp.logical_and(pl.program_id(0) == 0, pl.program_id(1) == 0)

            @pl.when(first)
            def _():
                o_refs[no][...] = jnp.zeros_like(o_refs[no])

            o_refs[no][...] += total

    col = pl.BlockSpec((tm, tn), lambda i, j: (i, j))
    w_spec = (pl.BlockSpec((tn, k), lambda i, j: (j, 0), pipeline_mode=_resident(tn == n))
              if transposed_w
              else pl.BlockSpec((k, tn), lambda i, j: (0, j), pipeline_mode=_resident(tn == n)))
    out_shape = [_sds((t, n), dt) for dt in out_dtypes]
    out_specs = [col] * no
    if loss:
        out_shape.append(_sds((8, 128), F32))
        out_specs.append(pl.BlockSpec((8, 128), lambda i, j: (0, 0)))
    return pl.pallas_call(
        body, name=name, grid=(t // tm, n // tn), out_shape=out_shape,
        in_specs=[pl.BlockSpec((tm, k), lambda i, j: (i, 0)), w_spec] + [col] * ne,
        out_specs=out_specs,
        compiler_params=_params(*(("arbitrary", "arbitrary") if loss else ("parallel", "parallel"))),
    )(a, w, *extras)


def _matmul_norm_bwd(name, pairs, x, dres, g, tm, carry=None, transposed_w=True):
    t, d = x.shape
    npairs = len(pairs)
    product = _dot_nt if transposed_w else _dot

    def body(*refs):
        a_refs, w_refs = refs[:npairs], refs[npairs:2 * npairs]
        x_ref, r_ref, g_ref, dx_ref, dxb_ref, dg_ref = refs[2 * npairs:]
        dy = product(a_refs[0][...], w_refs[0][...])
        for a_ref, w_ref in zip(a_refs[1:], w_refs[1:]):
            dy = dy + product(a_ref[...], w_ref[...])
        xv = x_ref[...]
        dx, xh = _rms_bwd(xv, _rms_scale(xv), g_ref[...], dy)
        dx = dx + r_ref[...]
        dx_ref[...] = dx
        dxb_ref[...] = dx.astype(BF16)

        @pl.when(pl.program_id(0) == 0)
        def _():
            dg_ref[...] = jnp.zeros_like(dg_ref)

        dg_ref[...] += jnp.sum(dy * xh, axis=0, keepdims=True)

    row = pl.BlockSpec((tm, d), lambda i: (i, 0))
    vec = pl.BlockSpec((1, d), lambda i: (0, 0))
    return _call(
        body, [a for a, _ in pairs] + [w for _, w in pairs] + [x, dres, g], name=name,
        grid=(t // tm,), out_shape=[_sds((t, d), F32), _sds((t, d), BF16), _sds((1, d), F32)],
        in_specs=[pl.BlockSpec((tm, a.shape[1]), lambda i: (i, 0)) for a, _ in pairs]
        + [pl.BlockSpec(w.shape, lambda i: (0, 0), pipeline_mode=pl.Buffered(1)) for _, w in pairs]
        + [row, row, vec],
        out_specs=[row, row, vec], carry=carry)


def _matmul_tn(name, a, g, tn, tk, by_chip=False):
    t, ka = a.shape
    n = g.shape[1]

    def body(a_ref, g_ref, o_ref):
        @pl.when(pl.program_id(1) == 0)
        def _():
            o_ref[...] = jnp.zeros_like(o_ref)

        acc = _dot_tn(a_ref[...], g_ref[...])
        o_ref[...] += acc[None] if by_chip else acc

    return pl.pallas_call(
        body, name=name, grid=(n // tn, t // tk),
        out_shape=_sds((n // tn, ka, tn) if by_chip else (ka, n), F32),
        in_specs=[pl.BlockSpec((tk, ka), lambda j, s: (s, 0)),
                  pl.BlockSpec((tk, tn), lambda j, s: (s, j))],
        out_specs=(pl.BlockSpec((1, ka, tn), lambda j, s: (j, 0, 0)) if by_chip
                   else pl.BlockSpec((ka, tn), lambda j, s: (0, j))),
        compiler_params=_params("parallel", "arbitrary"),
    )(a, g)


def _elementwise(name, fn, ins, out_dtypes, tr):
    r, n = ins[0].shape
    tr = _row_block(r, tr)
    ni = len(ins)

    def body(*refs):
        res = fn(*[ref[...] for ref in refs[:ni]])
        for o_ref, v in zip(refs[ni:], res):
            o_ref[...] = v.astype(o_ref.dtype)

    blk = pl.BlockSpec((tr, n), lambda i: (i, 0))
    return pl.pallas_call(
        body, name=name, grid=(r // tr,), out_shape=[_sds((r, n), dt) for dt in out_dtypes],
        in_specs=[blk] * ni, out_specs=[blk] * len(out_dtypes),
        compiler_params=_params("parallel"),
    )(*ins)


def _adamw_update(w, g, m, v):
    m = ADAM_B1 * m + (1.0 - ADAM_B1) * g
    v = ADAM_B2 * v + (1.0 - ADAM_B2) * (g * g)
    m_hat = m / (1.0 - ADAM_B1 ** ADAM_STEP)
    v_hat = v / (1.0 - ADAM_B2 ** ADAM_STEP)
    return -ADAM_LR * (m_hat / (jnp.sqrt(v_hat) + ADAM_EPS) + ADAM_WD * w), m, v


def _adamw(name, w, g, m, v):
    return _elementwise(name, _adamw_update, [w, g, m, v], [F32] * 3, 256)


def _adamw_shard(name, w, m, v, mine, theirs, where):
    r, n = w.shape
    h = r // 2
    tr = _row_block(h, 256)
    nb = h // tr

    def body(w_ref, p_ref, m_ref, v_ref, a_ref, b_ref, g_ref, d_ref, nm_ref, nv_ref):
        g = jnp.where(pl.program_id(0) == w_ref[0], a_ref[...], b_ref[...])
        g_ref[...] = g
        d_ref[...], nm_ref[...], nv_ref[...] = _adamw_update(p_ref[...], g, m_ref[...], v_ref[...])

    whole = pl.BlockSpec((tr, n), lambda s, i, c: (s * nb + i, 0))
    used = pl.BlockSpec((tr, n), lambda s, i, c: (jnp.where(s == c[0], i, 0), 0))
    unused = pl.BlockSpec((tr, n), lambda s, i, c: (jnp.where(s == c[0], 0, i), 0))
    return pl.pallas_call(
        body, name=name, out_shape=[_sds((r, n), F32)] * 4,
        grid_spec=pltpu.PrefetchScalarGridSpec(
            num_scalar_prefetch=1, grid=(2, nb), in_specs=[whole] * 3 + [used, unused],
            out_specs=[whole] * 4),
        compiler_params=_params("arbitrary", "arbitrary"),
    )(where, w, m, v, mine, theirs)


PAIRS = D_ATTN // BAND


def _in_proj(x, g, w, gq, gk, ones_bd, tm):
    t, dm = x.shape
    n = w.shape[1]
    nd = len(DILATIONS)
    first = 3 * D_CONV

    def body(x_ref, g_ref, w_ref, gq_ref, gk_ref, bd_ref, h_ref, z_ref, *refs):
        outs, slabs = refs[:3 * nd], refs[3 * nd:]
        xv = x_ref[...]
        h = (xv * _rms_scale(xv) * g_ref[...]).astype(BF16)
        h_ref[...] = h
        for cols in _column_chunks(n):
            z_ref[:, cols] = _dot(h, w_ref[:, cols])
        bd = bd_ref[...]
        q = z_ref[:, first:first + D_ATTN]
        k = z_ref[:, first + D_ATTN:first + 2 * D_ATTN]
        vals = [(q * _head_rms_scale(q, bd) * gq_ref[...]) * HEAD_DIM ** -0.5,
                k * _head_rms_scale(k, bd) * gk_ref[...], z_ref[:, first + 2 * D_ATTN:n]]
        for m, val in enumerate(vals):
            for c in range(PAIRS):
                slabs[0][c] = val[:, c * BAND:(c + 1) * BAND]
            cur, before = 0, 1
            for a, d in enumerate(DILATIONS):
                o_ref, src, dst = outs[m * nd + a], slabs[cur], slabs[1 - cur]
                step, count = d // before, tm // d
                keep = step > 1 and a + 1 < nd
                for c in range(PAIRS):
                    for r in range(d):
                        start = (r % before) * (tm // before) + r // before
                        rows = src.at[c][pl.ds(start, count, stride=step), :] if step > 1 else src[c]
                        o_ref[c, r] = rows.astype(BF16)
                        if keep:
                            dst.at[c][pl.ds(r * count, count), :] = rows
                if keep:
                    cur = 1 - cur
                before = d

    row = pl.BlockSpec((tm, dm), lambda i: (i, 0))
    vec = pl.BlockSpec((1, D_ATTN), lambda i: (0, 0))
    return pl.pallas_call(
        body, name="in_proj", grid=(t // tm,),
        out_shape=[_sds((t, dm), BF16), _sds((t, n), F32)]
        + [_sds((PAIRS, d, t // d, BAND), BF16) for _ in range(3) for d in DILATIONS],
        in_specs=[row, pl.BlockSpec((1, dm), lambda i: (0, 0)),
                  pl.BlockSpec((dm, n), lambda i: (0, 0), pipeline_mode=_resident(True)), vec, vec,
                  pl.BlockSpec((D_ATTN, D_ATTN), lambda i: (0, 0), pipeline_mode=_resident(True))],
        out_specs=[row, pl.BlockSpec((tm, n), lambda i: (i, 0))]
        + [pl.BlockSpec((PAIRS, d, tm // d, BAND), lambda i: (0, 0, i, 0))
           for _ in range(3) for d in DILATIONS],
        scratch_shapes=[pltpu.VMEM((PAIRS, tm, BAND), F32)] * 2,
        compiler_params=_params("parallel"),
    )(x, g, w, gq, gk, ones_bd)


TOK = 2048
UNITS = TOK // BAND


def _stack_masks():
    row = lax.broadcasted_iota(jnp.int32, (2 * BAND, 2 * BAND), 0) & (BAND - 1)
    col = lax.broadcasted_iota(jnp.int32, (2 * BAND, 2 * BAND), 1)
    lane = lax.broadcasted_iota(jnp.int32, (BAND, BAND), 1)
    head0 = lane < HEAD_DIM
    ones = [jnp.where(head0, 1.0, 0.0).astype(BF16), jnp.where(head0, 0.0, 1.0).astype(BF16)]
    return col - row, col, head0, ones


def _split3(x):
    hi = x.astype(BF16).astype(F32)
    mid = (x - hi).astype(BF16).astype(F32)
    return hi, mid, x - hi - mid


def _gather(srcs, dst, d, before=1):
    per, step, span = TOK // d, d // before, TOK // before
    at = 0
    for r in range(d):
        start = (r % before) * span + r // before
        for src in srcs:
            rows = src[pl.ds(start, per, stride=step), :] if step > 1 else src[pl.ds(start, per), :]
            dst[pl.ds(at, per), :] = rows.astype(dst.dtype)
            at += per


def _scatter(out_ref, src, d):
    per = TOK // d
    if d == 1:
        out_ref[...] = src[...]
        return
    for r in range(d):
        out_ref[pl.ds(r, per, stride=d), :] = src[pl.ds(r * per, per), :]


def _dilated_specs(nblk, reverse):
    def at(s):
        return (nblk - 1 - s) if reverse else s
    main = [pl.BlockSpec((1, d, TOK // d, BAND), lambda j, s: (j, 0, at(s), 0)) for d in DILATIONS]
    prev = [pl.BlockSpec((1, d, TOK // d, BAND), lambda j, s: (j, 0, jnp.maximum(at(s) - 1, 0), 0))
            for d in DILATIONS]
    return main, prev


def _window_rows(prev_ref, main_ref, dst, d):
    per = TOK // d
    for r in range(d):
        dst[pl.ds(r * (per + BAND), BAND), :] = prev_ref[0, r, pl.ds(per - BAND, BAND), :]
        dst[pl.ds(r * (per + BAND) + BAND, per), :] = main_ref[0, r]


def _attn_fwd(qs, ks, vs, carry=None):
    t = qs[0].shape[2]
    nblk = t // TOK
    nd = len(DILATIONS)

    def body(*refs):
        q_refs, kp_refs, k_refs = refs[:nd], refs[nd:2 * nd], refs[2 * nd:3 * nd]
        vp_refs, v_refs = refs[3 * nd:4 * nd], refs[4 * nd:5 * nd]
        y_ref, l_ref, kw_s, vw_s, ob, lb, on, ln = refs[5 * nd:]
        i = pl.program_id(1)
        diff, col, head0, hm = _stack_masks()
        band_ok = jnp.logical_and(diff >= 0, diff <= BAND)
        for g, d in enumerate(DILATIONS):
            per = TOK // d
            nb = per // BAND
            pad = per + BAND
            _window_rows(kp_refs[g], k_refs[g], kw_s, d)
            _window_rows(vp_refs[g], v_refs[g], vw_s, d)
            q_ref = q_refs[g]

            def unit(u, carry):
                r, b = u // nb, u % nb
                qu = q_ref[0, r, pl.ds(pl.multiple_of(b * BAND, BAND), BAND), :]
                start = pl.multiple_of(r * pad + b * BAND, BAND)
                kw = kw_s[pl.ds(start, 2 * BAND), :]
                vw = vw_s[pl.ds(start, 2 * BAND), :]
                lo = jnp.where(jnp.logical_and(i == 0, b == 0), BAND, 0)
                s = _dot_nt(jnp.concatenate([qu * hm[0], qu * hm[1]], axis=0), kw)
                s = jnp.where(jnp.logical_and(band_ok, col >= lo), s, NEG)
                mx = jnp.max(s, axis=-1, keepdims=True)
                e = jnp.exp(s - mx)
                den = jnp.sum(e, axis=-1, keepdims=True)
                o2 = _dot(e.astype(BF16), vw) / den
                l2 = jnp.broadcast_to(mx + jnp.log(den), (2 * BAND, BAND))
                rows = pl.ds(pl.multiple_of(u * BAND, BAND), BAND)
                ob[rows, :] = jnp.where(head0, o2[:BAND], o2[BAND:])
                lb[rows, :] = jnp.where(head0, l2[:BAND], l2[BAND:])
                return carry

            lax.fori_loop(0, UNITS, unit, 0, unroll=16)
            _scatter(on.at[g], ob, d)
            _scatter(ln.at[g], lb, d)
        ls = [ln[0], ln[1], ln[2]]
        mx = jnp.maximum(jnp.maximum(ls[0], ls[1]), ls[2])
        es = [jnp.exp(l - mx) for l in ls]
        tot = es[0] + es[1] + es[2]
        y_ref[...] = (es[0] * on[0] + es[1] * on[1] + es[2] * on[2]) / tot
        l_ref[...] = mx + jnp.log(tot)

    main, prev = _dilated_specs(nblk, False)
    out = pl.BlockSpec((TOK, BAND), lambda j, i: (i, j))
    win_rows = max(d * (TOK // d + BAND) for d in DILATIONS)
    return _call(
        body, list(qs) + list(ks) + list(ks) + list(vs) + list(vs), name="attn_fwd",
        grid=(PAIRS, nblk), out_shape=[_sds((t, D_ATTN), F32)] * 2,
        in_specs=main + prev + main + prev + main, out_specs=[out, out],
        scratch_shapes=[pltpu.VMEM((win_rows, BAND), BF16)] * 2 + [pltpu.VMEM((TOK, BAND), F32)] * 2
        + [pltpu.VMEM((nd, TOK, BAND), F32)] * 2,
        semantics=("parallel", "parallel"), carry=carry)


def _attn_bwd(qs, ks, vs, do, lse, dd, carry=None):
    t = qs[0].shape[2]
    nblk = t // TOK
    nd = len(DILATIONS)
    offs = [sum(DILATIONS[:g]) * BAND for g in range(nd)]

    def body(*refs):
        q_refs, kp_refs, k_refs = refs[:nd], refs[nd:2 * nd], refs[2 * nd:3 * nd]
        vp_refs, v_refs = refs[3 * nd:4 * nd], refs[4 * nd:5 * nd]
        (do_ref, l_ref, d_ref, dq_ref, dk_ref, dv_ref, kw_s, vw_s, dos, lds, pn, dqb, dkb, dvb, ckb,
         cvb, *more) = refs[5 * nd:]
        folds, mids = more[:6], more[6:]
        step = pl.program_id(1)
        i = nblk - 1 - step
        key = lax.broadcasted_iota(jnp.int32, (2 * BAND, 2 * BAND), 0)
        qry = lax.broadcasted_iota(jnp.int32, (2 * BAND, 2 * BAND), 1) & (BAND - 1)
        off = key - qry
        band_ok = jnp.logical_and(off >= 0, off <= BAND)
        lane = lax.broadcasted_iota(jnp.int32, (BAND, BAND), 1)
        head0 = lane < HEAD_DIM
        hm = [jnp.where(head0, 1.0, 0.0).astype(BF16), jnp.where(head0, 0.0, 1.0).astype(BF16)]
        lane2 = lax.broadcasted_iota(jnp.int32, (2 * BAND, BAND), 1) & (HEAD_DIM - 1)
        ones_l = jnp.where(lane2 < 3, 1.0, 0.0).astype(BF16)
        ones_d = jnp.where(jnp.logical_and(lane2 >= 3, lane2 < 6), 1.0, 0.0).astype(BF16)
        piece = lax.broadcasted_iota(jnp.int32, (TOK, BAND), 1) & (HEAD_DIM - 1)

        def pieces(x, at):
            hi, mid, lo = _split3(-x)
            return jnp.where(piece == at, hi,
                             jnp.where(piece == at + 1, mid, jnp.where(piece == at + 2, lo, 0.0)))

        pn[...] = pieces(l_ref[...], 0) + pieces(d_ref[...], 3)
        order = sorted(range(nd), key=lambda a: -DILATIONS[a])
        assert DILATIONS[order[-1]] == 1
        levels = {1: (do_ref, pn)}
        for n, a in enumerate(reversed(order[1:-1])):
            d, before = DILATIONS[a], DILATIONS[order[-1 - n]]
            levels[d] = (mids[2 * n], mids[2 * n + 1])
            for src, dst in zip(levels[before], levels[d]):
                _gather([src], dst, d, before)
        for pos, g in enumerate(order):
            d = DILATIONS[g]
            per = TOK // d
            nb = per // BAND
            pad = per + BAND
            _window_rows(kp_refs[g], k_refs[g], kw_s, d)
            _window_rows(vp_refs[g], v_refs[g], vw_s, d)
            known = d if d in levels else DILATIONS[order[pos + 1]]
            _gather([levels[known][0]], dos, d, known)
            _gather([levels[known][1]], lds, d, known)
            for r in range(d):
                spare = pl.ds(r * pad, BAND)
                dkb[spare, :] = jnp.zeros((BAND, BAND), F32)
                dvb[spare, :] = jnp.zeros((BAND, BAND), F32)
            q_ref = q_refs[g]

            def unit(u, c_):
                r, b = u // nb, u % nb
                rows = pl.ds(pl.multiple_of(u * BAND, BAND), BAND)
                qu = q_ref[0, r, pl.ds(pl.multiple_of(b * BAND, BAND), BAND), :]
                dou, ldu = dos[rows, :], lds[rows, :]
                q2 = jnp.concatenate([qu * hm[0], qu * hm[1]], axis=0)
                do2 = jnp.concatenate([dou * hm[0], dou * hm[1]], axis=0)
                ld2 = jnp.concatenate([ldu * hm[0], ldu * hm[1]], axis=0)
                acc = pl.ds(pl.multiple_of(r * pad + b * BAND, BAND), 2 * BAND)
                kw = kw_s[acc, :]
                vw = vw_s[acc, :]
                lo = jnp.where(jnp.logical_and(i == 0, b == 0), BAND, 0)
                ok = jnp.logical_and(band_ok, key >= lo)
                st = _dot_nt(jnp.concatenate([kw, ones_l], axis=1), jnp.concatenate([q2, ld2], axis=1))
                dpt = _dot_nt(jnp.concatenate([vw, ones_d], axis=1), jnp.concatenate([do2, ld2], axis=1))
                pt = jnp.where(ok, jnp.exp(st), 0.0)
                dst = (pt * dpt).astype(BF16)
                low = pl.ds(pl.multiple_of(r * pad + b * BAND, BAND), BAND)
                high = pl.ds(pl.multiple_of(r * pad + (b + 1) * BAND, BAND), BAND)
                dkw = _dot(dst, q2)
                dvw = _dot(pt.astype(BF16), do2)
                dkb[low, :] += dkw[:BAND]
                dvb[low, :] += dvw[:BAND]
                dkb[high, :] = dkw[BAND:]
                dvb[high, :] = dvw[BAND:]
                dq2 = _dot_tn(dst, kw)
                dqb[rows, :] = jnp.where(head0, dq2[:BAND], dq2[BAND:])
                return c_

            lax.fori_loop(0, UNITS, unit, 0, unroll=16)

            for r in range(d):
                last = pl.ds(r * pad + per, BAND)
                kept = pl.ds(offs[g] + r * BAND, BAND)

                @pl.when(step > 0)
                def _():
                    dkb[last, :] += ckb[kept, :]
                    dvb[last, :] += cvb[kept, :]

                ckb[kept, :] = dkb[pl.ds(r * pad, BAND), :]
                cvb[kept, :] = dvb[pl.ds(r * pad, BAND), :]
            narrower = DILATIONS[order[pos + 1]] if pos + 1 < nd else None
            for n, (buf, out_ref, stride, at) in enumerate(
                    ((dqb, dq_ref, per, 0), (dkb, dk_ref, pad, BAND), (dvb, dv_ref, pad, BAND))):
                wider, onward = folds[2 * n + pos % 2], folds[2 * n + (pos + 1) % 2]
                for r in range(d):
                    val = buf[pl.ds(r * stride + at, per), :]
                    if pos > 0:
                        val = val + wider[pl.ds(r * per, per), :]
                    if narrower is None:
                        out_ref[...] = val
                    else:
                        start = (r % narrower) * (TOK // narrower) + r // narrower
                        onward[pl.ds(start, per, stride=d // narrower), :] = val

    main, prev = _dilated_specs(nblk, True)
    tok = pl.BlockSpec((TOK, BAND), lambda j, s: (nblk - 1 - s, j))
    acc_rows = max(d * (TOK // d + BAND) for d in DILATIONS)
    kept_rows = sum(DILATIONS) * BAND
    return _call(
        body, list(qs) + list(ks) + list(ks) + list(vs) + list(vs) + [do, lse, dd], name="attn_bwd",
        grid=(PAIRS, nblk), out_shape=[_sds((t, D_ATTN), F32)] * 3,
        in_specs=main + prev + main + prev + main + [tok] * 3, out_specs=[tok] * 3,
        scratch_shapes=[pltpu.VMEM((acc_rows, BAND), BF16)] * 2 + [pltpu.VMEM((TOK, BAND), BF16)] * 2
        + [pltpu.VMEM((TOK, BAND), F32)] * 2 + [pltpu.VMEM((acc_rows, BAND), F32)] * 2
        + [pltpu.VMEM((kept_rows, BAND), F32)] * 2
        + [pltpu.VMEM((TOK, BAND), F32)] * (6 + 2 * (nd - 2)),
        semantics=("parallel", "arbitrary"), carry=carry)


def _halo_rows(tm, t):
    per = tm // 8
    prev = lambda i: (jnp.maximum(i * per - 1, 0), 0)
    nxt = lambda i: (jnp.minimum((i + 1) * per, t // 8 - 1), 0)
    return prev, nxt


def _mixer_out(z, cw, y_attn, g_conv, g_attn, tm, carry=None):
    t = z.shape[0]
    prev, _ = _halo_rows(tm, t)

    def body(z_ref, zp_ref, cw_ref, y_ref, gc_ref, ga_ref, mix_ref):
        i = pl.program_id(0)
        keep = jnp.where(i > 0, 1.0, 0.0)
        cu = jnp.concatenate([zp_ref[:, 0:512] * zp_ref[:, 1024:1536] * keep,
                              z_ref[:, 0:512] * z_ref[:, 1024:1536]], axis=0)
        c = (cw_ref[0:1, :] * pltpu.roll(cu, 2, 0) + cw_ref[1:2, :] * pltpu.roll(cu, 1, 0)
             + cw_ref[2:3, :] * cu)[8:, :]
        yc = z_ref[:, 512:1024] * c
        mix_ref[:, 0:512] = (yc * _rms_scale(yc) * gc_ref[...]).astype(BF16)
        ya = y_ref[...]
        mix_ref[:, 512:1024] = (ya * _rms_scale(ya) * ga_ref[...]).astype(BF16)

    blk = pl.BlockSpec((tm, 512), lambda i: (i, 0))
    vec = pl.BlockSpec((1, 512), lambda i: (0, 0))
    return _call(
        body, [z, z, cw, y_attn, g_conv, g_attn], name="mixer_out", grid=(t // tm,),
        out_shape=_sds((t, 1024), BF16),
        in_specs=[pl.BlockSpec((tm, 1536), lambda i: (i, 0)), pl.BlockSpec((8, 1536), prev),
                  pl.BlockSpec((8, 512), lambda i: (0, 0)), blk, vec, vec],
        out_specs=pl.BlockSpec((tm, 1024), lambda i: (i, 0)),
        semantics=("parallel",), carry=carry)


def _mixer_bwd(z, dx1, wout, y_attn, cw, g_conv, g_attn, ones_bd, tm, carry=None):
    t = z.shape[0]
    nblk = t // tm
    prev, nxt = _halo_rows(tm, t)
    e = tm + 16

    def body(z_ref, zp_ref, zn_ref, dx_ref, dxn_ref, w_ref, y_ref, cw_ref, gc_ref, ga_ref, bd_ref,
             dz_ref, do_ref, dd_ref, dcw_ref, dgc_ref, dga_ref):
        i = pl.program_id(0)
        dm = _dot_nt(dx_ref[...], w_ref[...])
        dmn = _dot_nt(dxn_ref[...], w_ref[0:D_CONV, :])[0:8, :]
        rows = lax.broadcasted_iota(jnp.int32, (e, 1), 0)
        lo = jnp.where(i > 0, 0, 8)
        hi = jnp.where(i < nblk - 1, e, tm + 8)
        ze = jnp.concatenate([zp_ref[...], z_ref[...], zn_ref[...]], axis=0)
        u, gb, gcv = ze[:, 0:512], ze[:, 512:1024], ze[:, 1024:1536]
        w0, w1, w2 = cw_ref[0:1, :], cw_ref[1:2, :], cw_ref[2:3, :]
        cu = jnp.where(rows >= lo, gcv * u, 0.0)
        cu1, cu2 = pltpu.roll(cu, 1, 0), pltpu.roll(cu, 2, 0)
        c = w0 * cu2 + w1 * cu1 + w2 * cu
        yc = gb * c
        dma = jnp.concatenate([jnp.zeros((8, 512), F32), dm[:, 0:512], dmn], axis=0)
        dyc, ych = _rms_bwd(yc, _rms_scale(yc), gc_ref[...], dma)
        dc = jnp.where(jnp.logical_and(rows >= 8, rows < hi), dyc * gb, 0.0)
        dcu = w0 * pltpu.roll(dc, e - 2, 0) + w1 * pltpu.roll(dc, e - 1, 0) + w2 * dc
        mid = slice(8, 8 + tm)
        dz_ref[:, 0:512] = (dcu * gcv)[mid, :].astype(BF16)
        dz_ref[:, 512:1024] = (dyc * c)[mid, :].astype(BF16)
        dz_ref[:, 1024:1536] = (dcu * u)[mid, :].astype(BF16)

        ya = y_ref[...]
        dmb = dm[:, 512:1024]
        dya, yah = _rms_bwd(ya, _rms_scale(ya), ga_ref[...], dmb)
        do_ref[...] = dya
        dd_ref[...] = _head_sum(dya * ya, bd_ref[...])

        @pl.when(i == 0)
        def _():
            dcw_ref[...] = jnp.zeros_like(dcw_ref)
            dgc_ref[...] = jnp.zeros_like(dgc_ref)
            dga_ref[...] = jnp.zeros_like(dga_ref)

        dcm = jnp.where(rows < tm + 8, dc, 0.0)
        dcw_ref[0:1, :] += jnp.sum(dcm * cu2, axis=0, keepdims=True)
        dcw_ref[1:2, :] += jnp.sum(dcm * cu1, axis=0, keepdims=True)
        dcw_ref[2:3, :] += jnp.sum(dcm * cu, axis=0, keepdims=True)
        dgc_ref[...] += jnp.sum((dma * ych)[mid, :], axis=0, keepdims=True)
        dga_ref[...] += jnp.sum(dmb * yah, axis=0, keepdims=True)

    blk = pl.BlockSpec((tm, 512), lambda i: (i, 0))
    vec = pl.BlockSpec((1, 512), lambda i: (0, 0))
    cwb = pl.BlockSpec((8, 512), lambda i: (0, 0))
    next16 = lambda i: (jnp.minimum((i + 1) * (tm // 16), t // 16 - 1), 0)
    return _call(
        body, [z, z, z, dx1, dx1, wout, y_attn, cw, g_conv, g_attn, ones_bd], name="mixer_bwd",
        grid=(nblk,),
        out_shape=[_sds((t, D_IN), BF16), _sds((t, 512), F32), _sds((t, 512), F32),
                   _sds((8, 512), F32), _sds((1, 512), F32), _sds((1, 512), F32)],
        in_specs=[pl.BlockSpec((tm, 1536), lambda i: (i, 0)), pl.BlockSpec((8, 1536), prev),
                  pl.BlockSpec((8, 1536), nxt), pl.BlockSpec((tm, D_MODEL), lambda i: (i, 0)),
                  pl.BlockSpec((16, D_MODEL), next16),
                  pl.BlockSpec(wout.shape, lambda i: (0, 0), pipeline_mode=_resident(True)),
                  blk, cwb, vec, vec, pl.BlockSpec((512, 512), lambda i: (0, 0))],
        out_specs=[pl.BlockSpec((tm, 1536), lambda i: (i, 0)), blk, blk, cwb, vec, vec],
        carry=carry)


def _qkv_bwd(z, dz, dqn, dkn, dv, gq, gk, ones_bd, tm, carry=None):
    t = z.shape[0]

    def body(zq_ref, zk_ref, _, dqn_ref, dkn_ref, dv_ref, gq_ref, gk_ref, bd_ref,
             dz_ref, dgq_ref, dgk_ref):
        bd = bd_ref[...]

        @pl.when(pl.program_id(0) == 0)
        def _():
            dgq_ref[...] = jnp.zeros_like(dgq_ref)
            dgk_ref[...] = jnp.zeros_like(dgk_ref)

        def back(v, dn, g, scale):
            r = _head_rms_scale(v, bd)
            vh = v * r
            dh = dn * (g * scale)
            dv = r * (dh - vh * (_head_sum(dh * vh, bd) * (1.0 / HEAD_DIM)))
            return dv, jnp.sum(dn * scale * vh, axis=0, keepdims=True)

        dq, dgq = back(zq_ref[...], dqn_ref[...], gq_ref[...], HEAD_DIM ** -0.5)
        dk, dgk = back(zk_ref[...], dkn_ref[...], gk_ref[...], 1.0)
        dgq_ref[...] += dgq
        dgk_ref[...] += dgk
        dz_ref[:, 0:512] = dq.astype(BF16)
        dz_ref[:, 512:1024] = dk.astype(BF16)
        dz_ref[:, 1024:1536] = dv_ref[...].astype(BF16)

    blk = pl.BlockSpec((tm, 512), lambda i: (i, 0))
    vec = pl.BlockSpec((1, 512), lambda i: (0, 0))
    return _call(
        body, [z, z, dz, dqn, dkn, dv, gq, gk, ones_bd], name="qkv_bwd", grid=(t // tm,),
        out_shape=[_sds((t, D_IN), BF16), _sds((1, 512), F32), _sds((1, 512), F32)],
        in_specs=[pl.BlockSpec((tm, 512), lambda i: (i, 3)), pl.BlockSpec((tm, 512), lambda i: (i, 4)),
                  ANY] + [blk] * 3 + [vec, vec, pl.BlockSpec((512, 512), lambda i: (0, 0))],
        out_specs=[pl.BlockSpec((tm, 1536), lambda i: (i, 1)), vec, vec],
        carry=carry, aliases={2: 0})


def _columns_from_chips(g):
    return g.transpose(1, 0, 2).reshape(g.shape[1], N_CHIPS * g.shape[2])


def kernel(x, g_mix, w_in, conv_w, g_q, g_k, g_conv_out, g_attn_out, w_out, g_ffn, w_gate, w_up, w_down, loss_target, m_g_mix, m_w_in, m_conv_w, m_g_q, m_g_k, m_g_conv_out, m_g_attn_out, m_w_out, m_g_ffn, m_w_gate, m_w_up, m_w_down, v_g_mix, v_w_in, v_conv_w, v_g_q, v_g_k, v_g_conv_out, v_g_attn_out, v_w_out, v_g_ffn, v_w_gate, v_w_up, v_w_down):
    t = x.shape[1]
    xs = x[0]
    target = loss_target[0]
    tm = min(512, t)
    tm_wide = min(1024, t)
    tmm = min(2048, t)

    cw_pad = jnp.pad(conv_w[0], ((0, 13), (0, 0)))
    gathered = _all_gather([w_in[0].astype(BF16), cw_pad])
    win = _columns_from_chips(gathered[0])
    cw = jnp.pad(gathered[1][:, 0:3, :].transpose(1, 0, 2).reshape(3, D_CONV), ((0, 5), (0, 0)))
    later = [w_out[0].astype(BF16), w_gate[0].T.astype(BF16), w_up[0].T.astype(BF16),
             w_down[0].astype(BF16)]

    head_id = jnp.arange(D_ATTN) // HEAD_DIM
    ones_bd = (head_id[:, None] == head_id[None, :]).astype(BF16)
    gq_t = jnp.tile(g_q, (1, D_ATTN // HEAD_DIM))
    gk_t = jnp.tile(g_k, (1, D_ATTN // HEAD_DIM))

    h1, z, *dilated = _in_proj(xs, g_mix, win, gq_t, gk_t, ones_bd, tm)
    nd = len(DILATIONS)
    qs, ks, vs = dilated[:nd], dilated[nd:2 * nd], dilated[2 * nd:]
    (y_attn, lse), gathered = _attn_fwd(qs, ks, vs, carry=_x_gather_chips(later))
    mix, gathered = _mixer_out(z, cw, y_attn, g_conv_out, g_attn_out, tm,
                               carry=_x_gather_sibling(gathered))
    wout = gathered[0].reshape(D_MODEL, D_MODEL)
    wgate_t = gathered[1].reshape(D_FF, D_MODEL)
    wup_t = gathered[2].reshape(D_FF, D_MODEL)
    wdown = gathered[3].reshape(D_FF, D_MODEL)
    (x1,) = _matmul("out_proj", mix, wout, [xs], [F32], lambda acc, r: (r + acc,), tm, D_MODEL)
    h2, gate, up, act = _norm_matmul("ffn_up", x1, g_ffn, [wgate_t, wup_t], tm, D_FF, True, BF16,
                                     transposed_w=True)

    def loss_epilogue(acc, r, tgt):
        err = r + acc - tgt
        dy = err * (1.0 / D_MODEL)
        return dy, dy, jnp.sum(err * err)

    dx2, dx2b, loss_sum = _matmul("ffn_down_loss", act, wdown, [x1, target], [F32, BF16],
                                  loss_epilogue, tm_wide, D_MODEL, loss=True)

    def swiglu_bwd(da, gt, u):
        gt, u = gt.astype(F32), u.astype(F32)
        s = _sigmoid(gt)
        return da * u * (s * (1.0 + gt * (1.0 - s))), da * (gt * s)

    dgate, dup = _matmul("ffn_down_bwd", dx2b, wdown, [gate, up], [BF16, BF16], swiglu_bwd,
                         tm, D_FF, transposed_w=True)
    gw_down = _matmul_tn("grad_w_down", act, dx2b, 512, tmm)
    gw_gate_t = _matmul_tn("grad_w_gate", dgate, h2, 512, tmm)
    gw_up_t = _matmul_tn("grad_w_up", dup, h2, 512, tmm)

    me = 2 * lax.axis_index("x") + lax.axis_index("y")
    where = jnp.stack([lax.axis_index("c"), me]).astype(jnp.int32)

    def pair_sums(names, full, got):
        return [_pair_sum(f"pair_sum_{nme}", a, b, where) for nme, a, b in zip(names, full, got)]

    def chip_sums(names, pair, got):
        return [_chip_sum(f"chip_sum_{nme}", own, b) for nme, (_, own), b in zip(names, pair, got)]

    ffn = ["w_gate", "w_up", "w_down"]
    full = [g.reshape(N_CHIPS, D_FF // N_CHIPS, D_MODEL) for g in (gw_gate_t, gw_up_t, gw_down)]
    (dx1, dx1b, gg_ffn), got = _matmul_norm_bwd(
        "ffn_up_bwd", [(dgate, wgate_t), (dup, wup_t)], x1, dx2, g_ffn, tm, carry=_x_pair(full),
        transposed_w=False)
    pair = pair_sums(ffn, full, got)
    gw_out = _matmul_tn("grad_w_out", mix, dx1b, 512, tmm)
    full = [gw_out.reshape(N_CHIPS, D_MODEL // N_CHIPS, D_MODEL)]
    (dzc, do, dd, gcw, gg_conv, gg_attn), got = _mixer_bwd(
        z, dx1b, wout, y_attn, cw, g_conv_out, g_attn_out, ones_bd, tm, carry=_x_pair(full))
    pair += pair_sums(["w_out"], full, got)
    early = ffn + ["w_out"]
    (dqn, dkn, dv), got = _attn_bwd(qs, ks, vs, do, lse, dd, carry=_x_chips([p for p, _ in pair]))
    mine = chip_sums(early, pair, got)
    (dz, gg_q, gg_k), theirs = _qkv_bwd(z, dzc, dqn, dkn, dv, gq_t, gk_t, ones_bd, tm,
                                        carry=_x_share(mine))
    full = [_matmul_tn("grad_w_in", h1, dz, D_IN // N_CHIPS, tmm, by_chip=True)]
    got = _exchange_alone("grad_pair_exchange_w_in", _x_pair(full))
    pair = pair_sums(["w_in"], full, got)
    (grad_x, _, gg_mix), got = _matmul_norm_bwd("in_proj_bwd", [(dz, win)], xs, dx1, g_mix, tm_wide,
                                                carry=_x_chips([pair[0][0]]))
    mine += chip_sums(["w_in"], pair, got)
    theirs = list(theirs) + list(_exchange_alone("grad_pair_share_w_in", _x_share(mine[-1:])))
    big = early + ["w_in"]

    small = _small_all_reduce({
        "g_mix": gg_mix, "g_ffn": gg_ffn, "g_conv_out": gg_conv, "g_attn_out": gg_attn,
        "g_q": gg_q, "g_k": gg_k, "loss": loss_sum, "conv_w": gcw})
    heads = D_ATTN // HEAD_DIM
    grads = {
        "g_mix": small[0:1, :], "g_ffn": small[1:2, :],
        "g_conv_out": small[2:3, 0:512], "g_attn_out": small[2:3, 512:1024],
        "g_q": small[3, 0:512].reshape(heads, HEAD_DIM).sum(axis=0)[None, :],
        "g_k": small[3, 512:1024].reshape(heads, HEAD_DIM).sum(axis=0)[None, :],
        "conv_w": lax.dynamic_slice(small[8:11, 0:512], (0, me * (D_CONV // N_CHIPS)),
                                    (3, D_CONV // N_CHIPS)),
    }
    halves = dict(zip(big, zip(mine, theirs)))
    loss = small[4, 0] * 0.5 * (1.0 / D_MODEL)

    weights = dict(g_mix=g_mix, w_in=w_in, conv_w=conv_w, g_q=g_q, g_k=g_k, g_conv_out=g_conv_out,
                   g_attn_out=g_attn_out, w_out=w_out, g_ffn=g_ffn, w_gate=w_gate, w_up=w_up,
                   w_down=w_down)
    moments_m = dict(g_mix=m_g_mix, w_in=m_w_in, conv_w=m_conv_w, g_q=m_g_q, g_k=m_g_k,
                     g_conv_out=m_g_conv_out, g_attn_out=m_g_attn_out, w_out=m_w_out, g_ffn=m_g_ffn,
                     w_gate=m_w_gate, w_up=m_w_up, w_down=m_w_down)
    moments_v = dict(g_mix=v_g_mix, w_in=v_w_in, conv_w=v_conv_w, g_q=v_g_q, g_k=v_g_k,
                     g_conv_out=v_g_conv_out, g_attn_out=v_g_attn_out, w_out=v_w_out, g_ffn=v_g_ffn,
                     w_gate=v_w_gate, w_up=v_w_up, w_down=v_w_down)
    names = list(weights)
    out_g, out_d, out_m, out_v = [], [], [], []
    for nme in names:
        wgt = weights[nme]
        shape2 = wgt.shape[-2:] if wgt.ndim == 3 else wgt.shape
        flip = nme in ("w_gate", "w_up")

        def to2d(a):
            return a.reshape(shape2).T if flip else a.reshape(shape2)

        def back(a):
            return (a.T if flip else a).reshape(wgt.shape)

        state = (to2d(wgt), to2d(moments_m[nme]), to2d(moments_v[nme]))
        if nme in halves:
            g2, dlt, nm, nv = _adamw_shard(f"adamw_{nme}", *state, *halves[nme], where)
        else:
            g2 = grads[nme].reshape(shape2)
            dlt, nm, nv = _adamw(f"adamw_{nme}", state[0], g2, state[1], state[2])
        out_g.append(back(g2))
        out_d.append(back(dlt))
        out_m.append(back(nm))
        out_v.append(back(nv))
    return (loss, grad_x[None], *out_g, *out_d, *out_m, *out_v)
```

```python
import functools
from typing import Any, Callable, NamedTuple, Sequence

import jax
import jax.numpy as jnp
from jax import lax
from jax.experimental import pallas as pl
from jax.experimental.pallas import tpu as pltpu

F32 = jnp.float32
BF16 = jnp.bfloat16
MESH = pl.DeviceIdType.MESH

D_MODEL = 1024
D_CONV = 512
D_ATTN = 512
HEAD_DIM = 64
D_FF = 2816
D_IN = 3 * D_CONV + 3 * D_ATTN
DILATIONS = (1, 4, 16)
BAND = 128
EPS = 1e-6
NEG = -1e30
N_CHIPS = 4

ADAM_LR = 0.001
ADAM_B1 = 0.9
ADAM_B2 = 0.999
ADAM_EPS = 1e-08
ADAM_WD = 0.01
ADAM_STEP = 10

V7X_VMEM_BYTES = 64 * 1024 * 1024
VMEM_LIMIT = V7X_VMEM_BYTES - 8 * 1024 * 1024
ANY = pl.BlockSpec(memory_space=pl.ANY)
VMEM_WHOLE = pl.BlockSpec(memory_space=pltpu.VMEM)


def _params(*sem):
    return pltpu.CompilerParams(dimension_semantics=sem, vmem_limit_bytes=VMEM_LIMIT)


def _sds(shape, dtype):
    return jax.ShapeDtypeStruct(shape, dtype)


def _resident(whole):
    return pl.Buffered(1) if whole else None


def _place():
    x, y, c = lax.axis_index("x"), lax.axis_index("y"), lax.axis_index("c")
    chips = [(1 - x, y), (x, 1 - y), (1 - x, 1 - y)]
    return x, y, c, 2 * x + y, chips, [2 * cx + cy for cx, cy in chips]


def _all_gather(shards):
    n = len(shards)

    def body(*refs):
        ins, outs, stage = refs[:n], refs[n:2 * n], refs[2 * n:3 * n]
        ssem, rsem, fsem, gsem, lsem, osem = refs[3 * n:]
        x, y, c, me, chips, cids = _place()
        sib = (x, y, 1 - c)

        def half(w, which):
            h = shards[w].shape[0] // 2
            return pl.ds(pl.multiple_of(which * h, 8), h)

        loads = [pltpu.make_async_copy(ins[w], stage[w], lsem.at[w]) for w in range(n)]
        local = [pltpu.make_async_copy(stage[w], outs[w].at[me], osem.at[w]) for w in range(n)]
        for cp in loads:
            cp.start()

        def chip_copy(w, j, src_slot):
            rows = half(w, c)
            return pltpu.make_async_remote_copy(
                src_ref=ins[w].at[rows], dst_ref=outs[w].at[src_slot, rows],
                send_sem=ssem.at[3 * w + j], recv_sem=rsem.at[3 * w + j],
                device_id=(*chips[j], c), device_id_type=MESH)

        def sib_copy(w, j, which):
            rows = half(w, which)
            return pltpu.make_async_remote_copy(
                src_ref=outs[w].at[cids[j], rows], dst_ref=outs[w].at[cids[j], rows],
                send_sem=fsem.at[3 * w + j], recv_sem=gsem.at[3 * w + j],
                device_id=sib, device_id_type=MESH)

        sends = [chip_copy(w, j, me) for w in range(n) for j in range(3)]
        for cp in sends:
            cp.start()
        for w in range(n):
            loads[w].wait()
            local[w].start()
        passed = []
        for w in range(n):
            for j in range(3):
                chip_copy(w, j, cids[j]).wait_recv()
                cp = sib_copy(w, j, c)
                cp.start()
                passed.append(cp)
        for w in range(n):
            for j in range(3):
                sib_copy(w, j, 1 - c).wait_recv()
        for cp in sends + passed:
            cp.wait_send()
        for cp in local:
            cp.wait()

    return pl.pallas_call(
        body, name="all_gather_weights",
        out_shape=[_sds((N_CHIPS,) + s.shape, s.dtype) for s in shards],
        in_specs=[ANY] * n, out_specs=[ANY] * n,
        scratch_shapes=[pltpu.VMEM(s.shape, s.dtype) for s in shards]
        + [pltpu.SemaphoreType.DMA((3 * n,))] * 4 + [pltpu.SemaphoreType.DMA((n,))] * 2,
        compiler_params=pltpu.CompilerParams(vmem_limit_bytes=VMEM_LIMIT),
    )(*shards)


class _Exchange(NamedTuple):
    srcs: Sequence[Any]
    lands: Sequence[Any]
    outs: Sequence[Any]
    n_sems: int
    copies: Callable


def _remote(src, dst, ssem, rsem, k, to):
    return pltpu.make_async_remote_copy(src_ref=src, dst_ref=dst, send_sem=ssem.at[k],
                                        recv_sem=rsem.at[k], device_id=to, device_id_type=MESH)


def _x_gather_chips(shards):
    def copies(srcs, lands, outs, ssem, rsem):
        _, _, c, me, chips, cids = _place()
        go, arrive = [], []
        for w, s in enumerate(shards):
            h = s.shape[0] // 2
            rows = pl.ds(pl.multiple_of(c * h, 8), h)
            for j in range(3):
                to = (*chips[j], c)
                go.append(_remote(srcs[w].at[rows], lands[w].at[me, rows], ssem, rsem, 3 * w + j, to))
                arrive.append(_remote(srcs[w].at[rows], lands[w].at[cids[j], rows], ssem, rsem,
                                      3 * w + j, to))
        return go, arrive

    lands = [jnp.broadcast_to(s[None], (N_CHIPS,) + s.shape) for s in shards]
    return _Exchange(shards, lands, [], 3 * len(shards), copies)


def _x_gather_sibling(gathered):
    def copies(srcs, lands, outs, ssem, rsem):
        x, y, c, _, _, cids = _place()
        go, arrive = [], []
        for w, g in enumerate(gathered):
            h = g.shape[1] // 2
            mine = pl.ds(pl.multiple_of(c * h, 8), h)
            theirs = pl.ds(pl.multiple_of((1 - c) * h, 8), h)
            for j in range(3):
                slab = lands[w].at[cids[j]]
                go.append(_remote(slab.at[mine], slab.at[mine], ssem, rsem, 3 * w + j, (x, y, 1 - c)))
                arrive.append(_remote(slab.at[theirs], slab.at[theirs], ssem, rsem, 3 * w + j,
                                      (x, y, 1 - c)))
        return go, arrive

    return _Exchange([], gathered, [], 3 * len(gathered), copies)


def _x_pair(grads):
    def copies(srcs, lands, outs, ssem, rsem):
        x, y, c, _, _, _ = _place()
        go = []
        for w, g in enumerate(grads):
            h = g.shape[1] // 2
            theirs = pl.ds(pl.multiple_of((1 - c) * h, 8), h)
            go.append(_remote(srcs[w].at[:, theirs, :], outs[w], ssem, rsem, w, (x, y, 1 - c)))
        return go, go

    outs = [_sds((N_CHIPS, g.shape[1] // 2, g.shape[2]), g.dtype) for g in grads]
    return _Exchange(grads, [], outs, len(grads), copies)


def _x_chips(parts):
    def copies(srcs, lands, outs, ssem, rsem):
        _, _, c, _, chips, cids = _place()
        go = [_remote(srcs[w].at[cids[j]], outs[w].at[j], ssem, rsem, 3 * w + j, (*chips[j], c))
              for w in range(len(parts)) for j in range(3)]
        return go, go

    outs = [_sds((3,) + p.shape[1:], p.dtype) for p in parts]
    return _Exchange(parts, [], outs, 3 * len(parts), copies)


def _x_share(halves):
    def copies(srcs, lands, outs, ssem, rsem):
        x, y, c, _, _, _ = _place()
        go = [_remote(srcs[w], outs[w], ssem, rsem, w, (x, y, 1 - c)) for w in range(len(halves))]
        return go, go

    return _Exchange(halves, [], [_sds(h.shape, h.dtype) for h in halves], len(halves), copies)


def _call(body, args, *, name, grid, in_specs, out_specs, out_shape, scratch_shapes=(),
          semantics=None, carry=None, aliases=None):
    single = not isinstance(out_shape, (list, tuple))
    out_shape = [out_shape] if single else list(out_shape)
    out_specs = [out_specs] if single else list(out_specs)
    aliases = dict(aliases or {})
    if carry is None:
        res = pl.pallas_call(
            body, name=name, grid=grid, in_specs=list(in_specs), out_specs=out_specs,
            out_shape=out_shape, scratch_shapes=list(scratch_shapes), input_output_aliases=aliases,
            compiler_params=_params(*(semantics or ("arbitrary",) * len(grid))))(*args)
        return res[0] if single else res
    n_in, n_out, n_scr = len(args), len(out_shape), len(scratch_shapes)
    n_src, n_land, n_new = len(carry.srcs), len(carry.lands), len(carry.outs)

    def carrying(*refs):
        at = 0
        parts = []
        for n in (n_in, n_src, n_land, n_out, n_land, n_new, n_scr, 2):
            parts.append(refs[at:at + n])
            at += n
        ins, srcs, _, outs, lands, news, scratch, (ssem, rsem) = parts
        ids = [pl.program_id(a) for a in range(len(grid))]
        first = functools.reduce(jnp.logical_and, [i == 0 for i in ids])
        last = functools.reduce(jnp.logical_and, [i == g - 1 for i, g in zip(ids, grid)])
        go, arrive = carry.copies(srcs, lands, news, ssem, rsem)

        @pl.when(first)
        def _():
            for cp in go:
                cp.start()

        body(*ins, *outs, *scratch)

        @pl.when(last)
        def _():
            for cp in go:
                cp.wait_send()
            for cp in arrive:
                cp.wait_recv()

    res = pl.pallas_call(
        carrying, name=name, grid=grid,
        in_specs=list(in_specs) + [ANY] * (n_src + n_land),
        out_specs=out_specs + [ANY] * (n_land + n_new),
        out_shape=out_shape + [_sds(a.shape, a.dtype) for a in carry.lands] + list(carry.outs),
        input_output_aliases={**aliases, **{n_in + n_src + i: n_out + i for i in range(n_land)}},
        scratch_shapes=list(scratch_shapes) + [pltpu.SemaphoreType.DMA((carry.n_sems,))] * 2,
        compiler_params=_params(*(("arbitrary",) * len(grid))))(*args, *carry.srcs, *carry.lands)
    own = res[:n_out]
    return (own[0] if single else own), res[n_out:]


def _exchange_alone(name, exchange):
    def body(x_ref, o_ref):
        o_ref[...] = x_ref[...]

    blk = pl.BlockSpec((8, 128), lambda i: (0, 0))
    _, res = _call(body, [jnp.zeros((8, 128), F32)], name=name, grid=(1,), in_specs=[blk],
                   out_specs=blk, out_shape=_sds((8, 128), F32), carry=exchange)
    return res


def _row_block(r, want):
    return max(d for d in range(1, min(want, r) + 1) if r % d == 0 and (d % 8 == 0 or d == r))


def _pair_sum(name, full, got, where):
    _, r, n = full.shape
    h = r // 2
    tr = _row_block(h, 512)
    nb = h // tr

    def body(w_ref, a_ref, b_ref, o_ref, own_ref):
        total = a_ref[...] + b_ref[...]
        o_ref[...] = total.astype(BF16)

        @pl.when(pl.program_id(1) == w_ref[1])
        def _():
            own_ref[...] = total[0]

    blk = pl.BlockSpec((1, tr, n), lambda i, s, w: (s, i, 0))
    return pl.pallas_call(
        body, name=name, out_shape=[_sds(got.shape, BF16), _sds((h, n), F32)],
        grid_spec=pltpu.PrefetchScalarGridSpec(
            num_scalar_prefetch=1, grid=(nb, N_CHIPS),
            in_specs=[pl.BlockSpec((1, tr, n), lambda i, s, w: (s, w[0] * nb + i, 0)), blk],
            out_specs=[blk, pl.BlockSpec((tr, n), lambda i, s, w: (i, 0))]),
        compiler_params=_params("parallel", "arbitrary"),
    )(where, full, got)


def _chip_sum(name, own, got):
    h, n = own.shape
    tr = _row_block(h, 256)

    def body(a_ref, b0, b1, b2, o_ref):
        o_ref[...] = ((a_ref[...] + b0[0].astype(F32)) + b1[0].astype(F32)) + b2[0].astype(F32)

    def slot(j):
        return pl.BlockSpec((1, tr, n), lambda i: (j, i, 0))

    blk = pl.BlockSpec((tr, n), lambda i: (i, 0))
    return pl.pallas_call(
        body, name=name, grid=(h // tr,), out_shape=_sds((h, n), F32),
        in_specs=[blk, slot(0), slot(1), slot(2)], out_specs=blk,
        compiler_params=_params("parallel"),
    )(own, got, got, got)


SMALL_ROWS = 16
SMALL_LAYOUT = (
    ("g_mix", 0, 0, 1, 1024), ("g_ffn", 1, 0, 1, 1024), ("g_conv_out", 2, 0, 1, 512),
    ("g_attn_out", 2, 512, 1, 512), ("g_q", 3, 0, 1, 512), ("g_k", 3, 512, 1, 512),
    ("loss", 4, 0, 1, 128), ("conv_w", 8, 0, 8, 512))


def _small_all_reduce(parts):
    names = [s[0] for s in SMALL_LAYOUT]

    def body(*refs):
        ins = refs[:len(names)]
        out_ref, stage, buf, ssem, rsem = refs[len(names):]
        x, y, c, _, _, _ = _place()
        me = 4 * x + 2 * y + c
        stage[...] = jnp.zeros_like(stage)
        for ref, (_, r0, c0, nr, nc) in zip(ins, SMALL_LAYOUT):
            stage[r0:r0 + nr, c0:c0 + nc] = ref[0:nr, :]
        buf[me] = stage[...]
        peers = []
        for d in range(1, 8):
            px = 1 - x if d & 4 else x
            py = 1 - y if d & 2 else y
            pc = 1 - c if d & 1 else c
            peers.append(((px, py, pc), 4 * px + 2 * py + pc))
        sends = [pltpu.make_async_remote_copy(
            src_ref=stage, dst_ref=buf.at[me], send_sem=ssem.at[k], recv_sem=rsem.at[k],
            device_id=peer, device_id_type=MESH) for k, (peer, _) in enumerate(peers)]
        for cp in sends:
            cp.start()
        for k, (peer, pid) in enumerate(peers):
            pltpu.make_async_remote_copy(
                src_ref=stage, dst_ref=buf.at[pid], send_sem=ssem.at[k], recv_sem=rsem.at[k],
                device_id=peer, device_id_type=MESH).wait_recv()
        for cp in sends:
            cp.wait_send()
        acc = buf[0]
        for k in range(1, 8):
            acc = acc + buf[k]
        out_ref[...] = acc

    return pl.pallas_call(
        body, name="small_all_reduce", out_shape=_sds((SMALL_ROWS, 1024), F32),
        in_specs=[VMEM_WHOLE] * len(names), out_specs=VMEM_WHOLE,
        scratch_shapes=[pltpu.VMEM((SMALL_ROWS, 1024), F32), pltpu.VMEM((8, SMALL_ROWS, 1024), F32),
                        pltpu.SemaphoreType.DMA((7,)), pltpu.SemaphoreType.DMA((7,))],
    )(*[parts[k] for k in names])


def _dot(a, b):
    return jnp.dot(a, b, preferred_element_type=F32)


def _dot_nt(a, b):
    return lax.dot_general(a, b, (((1,), (1,)), ((), ())), preferred_element_type=F32)


def _dot_tn(a, b):
    return lax.dot_general(a, b, (((0,), (0,)), ((), ())), preferred_element_type=F32)


def _sigmoid(v):
    return 1.0 / (1.0 + jnp.exp(-v))


def _rms_scale(v):
    return lax.rsqrt(jnp.mean(v * v, axis=-1, keepdims=True) + EPS)


def _rms_bwd(v, r, g, dy):
    vh = v * r
    dh = dy * g
    return r * (dh - vh * jnp.mean(dh * vh, axis=-1, keepdims=True)), vh


def _head_sum(a, ones_bd):
    hi = a.astype(BF16)
    lo = (a - hi.astype(F32)).astype(BF16)
    return _dot(hi, ones_bd) + _dot(lo, ones_bd)


def _head_rms_scale(v, ones_bd):
    return lax.rsqrt(_head_sum(v * v, ones_bd) * (1.0 / HEAD_DIM) + EPS)


MXU_COLUMNS = 256


def _column_chunks(n):
    width = MXU_COLUMNS if n % MXU_COLUMNS == 0 else n
    return [slice(c, c + width) for c in range(0, n, width)]


def _norm_matmul(name, x, g, ws, tm, tn, swiglu, out_dtype=F32, transposed_w=False):
    t, d = x.shape
    n = ws[0].shape[0] if transposed_w else ws[0].shape[1]
    nw = len(ws)

    def body(x_ref, g_ref, *refs):
        w_refs, h_ref, o_refs = refs[:nw], refs[nw], refs[nw + 1:2 * nw + 1]
        hs = refs[-1]

        @pl.when(pl.program_id(1) == 0)
        def _():
            xv = x_ref[...]
            h = (xv * _rms_scale(xv) * g_ref[...]).astype(BF16)
            hs[...] = h
            h_ref[...] = h

        h = hs[...]
        for cols in _column_chunks(tn):
            outs = [_dot_nt(h, w[cols, :]) if transposed_w else _dot(h, w[:, cols]) for w in w_refs]
            for o_ref, o in zip(o_refs, outs):
                o_ref[:, cols] = o.astype(out_dtype)
            if swiglu:
                refs[2 * nw + 1][:, cols] = (outs[0] * _sigmoid(outs[0]) * outs[1]).astype(BF16)

    row = pl.BlockSpec((tm, d), lambda i, j: (i, 0))
    col = pl.BlockSpec((tm, tn), lambda i, j: (i, j))
    out_shape = [_sds((t, d), BF16)] + [_sds((t, n), out_dtype)] * nw
    out_specs = [row] + [col] * nw
    if swiglu:
        out_shape.append(_sds((t, n), BF16))
        out_specs.append(col)
    return pl.pallas_call(
        body, name=name, grid=(t // tm, n // tn), out_shape=out_shape,
        in_specs=[row, pl.BlockSpec((1, d), lambda i, j: (0, 0))]
        + [pl.BlockSpec((tn, d), lambda i, j: (j, 0), pipeline_mode=_resident(tn == n))
           if transposed_w
           else pl.BlockSpec((d, tn), lambda i, j: (0, j), pipeline_mode=_resident(tn == n))] * nw,
        out_specs=out_specs, scratch_shapes=[pltpu.VMEM((tm, d), BF16)],
        compiler_params=_params("parallel", "arbitrary"),
    )(x, g, *ws)


def _matmul(name, a, w, extras, out_dtypes, epilogue, tm, tn, transposed_w=False, loss=False):
    t, k = a.shape
    n = w.shape[0] if transposed_w else w.shape[1]
    ne, no = len(extras), len(out_dtypes)

    def body(a_ref, w_ref, *refs):
        e_refs, o_refs = refs[:ne], refs[ne:]
        a = a_ref[...]
        total = 0.0
        for cols in _column_chunks(tn):
            acc = _dot_nt(a, w_ref[cols, :]) if transposed_w else _dot(a, w_ref[:, cols])
            res = epilogue(acc, *[e[:, cols] for e in e_refs])
            for o_ref, r in zip(o_refs[:no], res[:no]):
                o_ref[:, cols] = r.astype(o_ref.dtype)
            if loss:
                total = total + res[no]
        if loss:
            first = jnp.logical_and(pl.program_id(0) == 0, pl.program_id(1) == 0)

            @pl.when(first)
            def _():
                o_refs[no][...] = jnp.zeros_like(o_refs[no])

            o_refs[no][...] += total

    col = pl.BlockSpec((tm, tn), lambda i, j: (i, j))
    w_spec = (pl.BlockSpec((tn, k), lambda i, j: (j, 0), pipeline_mode=_resident(tn == n))
              if transposed_w
              else pl.BlockSpec((k, tn), lambda i, j: (0, j), pipeline_mode=_resident(tn == n)))
    out_shape = [_sds((t, n), dt) for dt in out_dtypes]
    out_specs = [col] * no
    if loss:
        out_shape.append(_sds((8, 128), F32))
        out_specs.append(pl.BlockSpec((8, 128), lambda i, j: (0, 0)))
    return pl.pallas_call(
        body, name=name, grid=(t // tm, n // tn), out_shape=out_shape,
        in_specs=[pl.BlockSpec((tm, k), lambda i, j: (i, 0)), w_spec] + [col] * ne,
        out_specs=out_specs,
        compiler_params=_params(*(("arbitrary", "arbitrary") if loss else ("parallel", "parallel"))),
    )(a, w, *extras)


def _matmul_norm_bwd(name, pairs, x, dres, g, tm, carry=None, transposed_w=True):
    t, d = x.shape
    npairs = len(pairs)
    product = _dot_nt if transposed_w else _dot

    def body(*refs):
        a_refs, w_refs = refs[:npairs], refs[npairs:2 * npairs]
        x_ref, r_ref, g_ref, dx_ref, dxb_ref, dg_ref = refs[2 * npairs:]
        dy = product(a_refs[0][...], w_refs[0][...])
        for a_ref, w_ref in zip(a_refs[1:], w_refs[1:]):
            dy = dy + product(a_ref[...], w_ref[...])
        xv = x_ref[...]
        dx, xh = _rms_bwd(xv, _rms_scale(xv), g_ref[...], dy)
        dx = dx + r_ref[...]
        dx_ref[...] = dx
        dxb_ref[...] = dx.astype(BF16)

        @pl.when(pl.program_id(0) == 0)
        def _():
            dg_ref[...] = jnp.zeros_like(dg_ref)

        dg_ref[...] += jnp.sum(dy * xh, axis=0, keepdims=True)

    row = pl.BlockSpec((tm, d), lambda i: (i, 0))
    vec = pl.BlockSpec((1, d), lambda i: (0, 0))
    return _call(
        body, [a for a, _ in pairs] + [w for _, w in pairs] + [x, dres, g], name=name,
        grid=(t // tm,), out_shape=[_sds((t, d), F32), _sds((t, d), BF16), _sds((1, d), F32)],
        in_specs=[pl.BlockSpec((tm, a.shape[1]), lambda i: (i, 0)) for a, _ in pairs]
        + [pl.BlockSpec(w.shape, lambda i: (0, 0), pipeline_mode=pl.Buffered(1)) for _, w in pairs]
        + [row, row, vec],
        out_specs=[row, row, vec], carry=carry)


def _matmul_tn(name, a, g, tn, tk, by_chip=False):
    t, ka = a.shape
    n = g.shape[1]

    def body(a_ref, g_ref, o_ref):
        @pl.when(pl.program_id(1) == 0)
        def _():
            o_ref[...] = jnp.zeros_like(o_ref)

        acc = _dot_tn(a_ref[...], g_ref[...])
        o_ref[...] += acc[None] if by_chip else acc

    return pl.pallas_call(
        body, name=name, grid=(n // tn, t // tk),
        out_shape=_sds((n // tn, ka, tn) if by_chip else (ka, n), F32),
        in_specs=[pl.BlockSpec((tk, ka), lambda j, s: (s, 0)),
                  pl.BlockSpec((tk, tn), lambda j, s: (s, j))],
        out_specs=(pl.BlockSpec((1, ka, tn), lambda j, s: (j, 0, 0)) if by_chip
                   else pl.BlockSpec((ka, tn), lambda j, s: (0, j))),
        compiler_params=_params("parallel", "arbitrary"),
    )(a, g)


def _elementwise(name, fn, ins, out_dtypes, tr):
    r, n = ins[0].shape
    tr = _row_block(r, tr)
    ni = len(ins)

    def body(*refs):
        res = fn(*[ref[...] for ref in refs[:ni]])
        for o_ref, v in zip(refs[ni:], res):
            o_ref[...] = v.astype(o_ref.dtype)

    blk = pl.BlockSpec((tr, n), lambda i: (i, 0))
    return pl.pallas_call(
        body, name=name, grid=(r // tr,), out_shape=[_sds((r, n), dt) for dt in out_dtypes],
        in_specs=[blk] * ni, out_specs=[blk] * len(out_dtypes),
        compiler_params=_params("parallel"),
    )(*ins)


def _adamw_update(w, g, m, v):
    m = ADAM_B1 * m + (1.0 - ADAM_B1) * g
    v = ADAM_B2 * v + (1.0 - ADAM_B2) * (g * g)
    m_hat = m / (1.0 - ADAM_B1 ** ADAM_STEP)
    v_hat = v / (1.0 - ADAM_B2 ** ADAM_STEP)
    return -ADAM_LR * (m_hat / (jnp.sqrt(v_hat) + ADAM_EPS) + ADAM_WD * w), m, v


def _adamw(name, w, g, m, v):
    return _elementwise(name, _adamw_update, [w, g, m, v], [F32] * 3, 256)


def _adamw_shard(name, w, m, v, mine, theirs, where):
    r, n = w.shape
    h = r // 2
    tr = _row_block(h, 256)
    nb = h // tr

    def body(w_ref, p_ref, m_ref, v_ref, a_ref, b_ref, g_ref, d_ref, nm_ref, nv_ref):
        g = jnp.where(pl.program_id(0) == w_ref[0], a_ref[...], b_ref[...])
        g_ref[...] = g
        d_ref[...], nm_ref[...], nv_ref[...] = _adamw_update(p_ref[...], g, m_ref[...], v_ref[...])

    whole = pl.BlockSpec((tr, n), lambda s, i, c: (s * nb + i, 0))
    used = pl.BlockSpec((tr, n), lambda s, i, c: (jnp.where(s == c[0], i, 0), 0))
    unused = pl.BlockSpec((tr, n), lambda s, i, c: (jnp.where(s == c[0], 0, i), 0))
    return pl.pallas_call(
        body, name=name, out_shape=[_sds((r, n), F32)] * 4,
        grid_spec=pltpu.PrefetchScalarGridSpec(
            num_scalar_prefetch=1, grid=(2, nb), in_specs=[whole] * 3 + [used, unused],
            out_specs=[whole] * 4),
        compiler_params=_params("arbitrary", "arbitrary"),
    )(where, w, m, v, mine, theirs)


PAIRS = D_ATTN // BAND


def _in_proj(x, g, w, gq, gk, ones_bd, tm):
    t, dm = x.shape
    n = w.shape[1]
    nd = len(DILATIONS)
    first = 3 * D_CONV

    def body(x_ref, g_ref, w_ref, gq_ref, gk_ref, bd_ref, h_ref, z_ref, *refs):
        outs, slabs = refs[:3 * nd], refs[3 * nd:]
        xv = x_ref[...]
        h = (xv * _rms_scale(xv) * g_ref[...]).astype(BF16)
        h_ref[...] = h
        for cols in _column_chunks(n):
            z_ref[:, cols] = _dot(h, w_ref[:, cols])
        bd = bd_ref[...]
        q = z_ref[:, first:first + D_ATTN]
        k = z_ref[:, first + D_ATTN:first + 2 * D_ATTN]
        vals = [(q * _head_rms_scale(q, bd) * gq_ref[...]) * HEAD_DIM ** -0.5,
                k * _head_rms_scale(k, bd) * gk_ref[...], z_ref[:, first + 2 * D_ATTN:n]]
        for m, val in enumerate(vals):
            for c in range(PAIRS):
                slabs[0][c] = val[:, c * BAND:(c + 1) * BAND]
            cur, before = 0, 1
            for a, d in enumerate(DILATIONS):
                o_ref, src, dst = outs[m * nd + a], slabs[cur], slabs[1 - cur]
                step, count = d // before, tm // d
                keep = step > 1 and a + 1 < nd
                for c in range(PAIRS):
                    for r in range(d):
                        start = (r % before) * (tm // before) + r // before
                        rows = src.at[c][pl.ds(start, count, stride=step), :] if step > 1 else src[c]
                        o_ref[c, r] = rows.astype(BF16)
                        if keep:
                            dst.at[c][pl.ds(r * count, count), :] = rows
                if keep:
                    cur = 1 - cur
                before = d

    row = pl.BlockSpec((tm, dm), lambda i: (i, 0))
    vec = pl.BlockSpec((1, D_ATTN), lambda i: (0, 0))
    return pl.pallas_call(
        body, name="in_proj", grid=(t // tm,),
        out_shape=[_sds((t, dm), BF16), _sds((t, n), F32)]
        + [_sds((PAIRS, d, t // d, BAND), BF16) for _ in range(3) for d in DILATIONS],
        in_specs=[row, pl.BlockSpec((1, dm), lambda i: (0, 0)),
                  pl.BlockSpec((dm, n), lambda i: (0, 0), pipeline_mode=_resident(True)), vec, vec,
                  pl.BlockSpec((D_ATTN, D_ATTN), lambda i: (0, 0), pipeline_mode=_resident(True))],
        out_specs=[row, pl.BlockSpec((tm, n), lambda i: (i, 0))]
        + [pl.BlockSpec((PAIRS, d, tm // d, BAND), lambda i: (0, 0, i, 0))
           for _ in range(3) for d in DILATIONS],
        scratch_shapes=[pltpu.VMEM((PAIRS, tm, BAND), F32)] * 2,
        compiler_params=_params("parallel"),
    )(x, g, w, gq, gk, ones_bd)


TOK = 2048
UNITS = TOK // BAND


def _stack_masks():
    row = lax.broadcasted_iota(jnp.int32, (2 * BAND, 2 * BAND), 0) & (BAND - 1)
    col = lax.broadcasted_iota(jnp.int32, (2 * BAND, 2 * BAND), 1)
    lane = lax.broadcasted_iota(jnp.int32, (BAND, BAND), 1)
    head0 = lane < HEAD_DIM
    ones = [jnp.where(head0, 1.0, 0.0).astype(BF16), jnp.where(head0, 0.0, 1.0).astype(BF16)]
    return col - row, col, head0, ones


def _split3(x):
    hi = x.astype(BF16).astype(F32)
    mid = (x - hi).astype(BF16).astype(F32)
    return hi, mid, x - hi - mid


def _gather(srcs, dst, d, before=1):
    per, step, span = TOK // d, d // before, TOK // before
    at = 0
    for r in range(d):
        start = (r % before) * span + r // before
        for src in srcs:
            rows = src[pl.ds(start, per, stride=step), :] if step > 1 else src[pl.ds(start, per), :]
            dst[pl.ds(at, per), :] = rows.astype(dst.dtype)
            at += per


def _scatter(out_ref, src, d):
    per = TOK // d
    if d == 1:
        out_ref[...] = src[...]
        return
    for r in range(d):
        out_ref[pl.ds(r, per, stride=d), :] = src[pl.ds(r * per, per), :]


def _dilated_specs(nblk, reverse):
    def at(s):
        return (nblk - 1 - s) if reverse else s
    main = [pl.BlockSpec((1, d, TOK // d, BAND), lambda j, s: (j, 0, at(s), 0)) for d in DILATIONS]
    prev = [pl.BlockSpec((1, d, TOK // d, BAND), lambda j, s: (j, 0, jnp.maximum(at(s) - 1, 0), 0))
            for d in DILATIONS]
    return main, prev


def _window_rows(prev_ref, main_ref, dst, d):
    per = TOK // d
    for r in range(d):
        dst[pl.ds(r * (per + BAND), BAND), :] = prev_ref[0, r, pl.ds(per - BAND, BAND), :]
        dst[pl.ds(r * (per + BAND) + BAND, per), :] = main_ref[0, r]


def _attn_fwd(qs, ks, vs, carry=None):
    t = qs[0].shape[2]
    nblk = t // TOK
    nd = len(DILATIONS)

    def body(*refs):
        q_refs, kp_refs, k_refs = refs[:nd], refs[nd:2 * nd], refs[2 * nd:3 * nd]
        vp_refs, v_refs = refs[3 * nd:4 * nd], refs[4 * nd:5 * nd]
        y_ref, l_ref, kw_s, vw_s, ob, lb, on, ln = refs[5 * nd:]
        i = pl.program_id(1)
        diff, col, head0, hm = _stack_masks()
        band_ok = jnp.logical_and(diff >= 0, diff <= BAND)
        for g, d in enumerate(DILATIONS):
            per = TOK // d
            nb = per // BAND
            pad = per + BAND
            _window_rows(kp_refs[g], k_refs[g], kw_s, d)
            _window_rows(vp_refs[g], v_refs[g], vw_s, d)
            q_ref = q_refs[g]

            def unit(u, carry):
                r, b = u // nb, u % nb
                qu = q_ref[0, r, pl.ds(pl.multiple_of(b * BAND, BAND), BAND), :]
                start = pl.multiple_of(r * pad + b * BAND, BAND)
                kw = kw_s[pl.ds(start, 2 * BAND), :]
                vw = vw_s[pl.ds(start, 2 * BAND), :]
                lo = jnp.where(jnp.logical_and(i == 0, b == 0), BAND, 0)
                s = _dot_nt(jnp.concatenate([qu * hm[0], qu * hm[1]], axis=0), kw)
                s = jnp.where(jnp.logical_and(band_ok, col >= lo), s, NEG)
                mx = jnp.max(s, axis=-1, keepdims=True)
                e = jnp.exp(s - mx)
                den = jnp.sum(e, axis=-1, keepdims=True)
                o2 = _dot(e.astype(BF16), vw) / den
                l2 = jnp.broadcast_to(mx + jnp.log(den), (2 * BAND, BAND))
                rows = pl.ds(pl.multiple_of(u * BAND, BAND), BAND)
                ob[rows, :] = jnp.where(head0, o2[:BAND], o2[BAND:])
                lb[rows, :] = jnp.where(head0, l2[:BAND], l2[BAND:])
                return carry

            lax.fori_loop(0, UNITS, unit, 0, unroll=16)
            _scatter(on.at[g], ob, d)
            _scatter(ln.at[g], lb, d)
        ls = [ln[0], ln[1], ln[2]]
        mx = jnp.maximum(jnp.maximum(ls[0], ls[1]), ls[2])
        es = [jnp.exp(l - mx) for l in ls]
        tot = es[0] + es[1] + es[2]
        y_ref[...] = (es[0] * on[0] + es[1] * on[1] + es[2] * on[2]) / tot
        l_ref[...] = mx + jnp.log(tot)

    main, prev = _dilated_specs(nblk, False)
    out = pl.BlockSpec((TOK, BAND), lambda j, i: (i, j))
    win_rows = max(d * (TOK // d + BAND) for d in DILATIONS)
    return _call(
        body, list(qs) + list(ks) + list(ks) + list(vs) + list(vs), name="attn_fwd",
        grid=(PAIRS, nblk), out_shape=[_sds((t, D_ATTN), F32)] * 2,
        in_specs=main + prev + main + prev + main, out_specs=[out, out],
        scratch_shapes=[pltpu.VMEM((win_rows, BAND), BF16)] * 2 + [pltpu.VMEM((TOK, BAND), F32)] * 2
        + [pltpu.VMEM((nd, TOK, BAND), F32)] * 2,
        semantics=("parallel", "parallel"), carry=carry)


def _attn_bwd(qs, ks, vs, do, lse, dd, carry=None):
    t = qs[0].shape[2]
    nblk = t // TOK
    nd = len(DILATIONS)
    offs = [sum(DILATIONS[:g]) * BAND for g in range(nd)]

    def body(*refs):
        q_refs, kp_refs, k_refs = refs[:nd], refs[nd:2 * nd], refs[2 * nd:3 * nd]
        vp_refs, v_refs = refs[3 * nd:4 * nd], refs[4 * nd:5 * nd]
        (do_ref, l_ref, d_ref, dq_ref, dk_ref, dv_ref, kw_s, vw_s, dos, lds, pn, dqb, dkb, dvb, ckb,
         cvb, *more) = refs[5 * nd:]
        folds, mids = more[:6], more[6:]
        step = pl.program_id(1)
        i = nblk - 1 - step
        key = lax.broadcasted_iota(jnp.int32, (2 * BAND, 2 * BAND), 0)
        qry = lax.broadcasted_iota(jnp.int32, (2 * BAND, 2 * BAND), 1) & (BAND - 1)
        off = key - qry
        band_ok = jnp.logical_and(off >= 0, off <= BAND)
        lane = lax.broadcasted_iota(jnp.int32, (BAND, BAND), 1)
        head0 = lane < HEAD_DIM
        hm = [jnp.where(head0, 1.0, 0.0).astype(BF16), jnp.where(head0, 0.0, 1.0).astype(BF16)]
        lane2 = lax.broadcasted_iota(jnp.int32, (2 * BAND, BAND), 1) & (HEAD_DIM - 1)
        ones_l = jnp.where(lane2 < 3, 1.0, 0.0).astype(BF16)
        ones_d = jnp.where(jnp.logical_and(lane2 >= 3, lane2 < 6), 1.0, 0.0).astype(BF16)
        piece = lax.broadcasted_iota(jnp.int32, (TOK, BAND), 1) & (HEAD_DIM - 1)

        def pieces(x, at):
            hi, mid, lo = _split3(-x)
            return jnp.where(piece == at, hi,
                             jnp.where(piece == at + 1, mid, jnp.where(piece == at + 2, lo, 0.0)))

        pn[...] = pieces(l_ref[...], 0) + pieces(d_ref[...], 3)
        order = sorted(range(nd), key=lambda a: -DILATIONS[a])
        assert DILATIONS[order[-1]] == 1
        levels = {1: (do_ref, pn)}
        for n, a in enumerate(reversed(order[1:-1])):
            d, before = DILATIONS[a], DILATIONS[order[-1 - n]]
            levels[d] = (mids[2 * n], mids[2 * n + 1])
            for src, dst in zip(levels[before], levels[d]):
                _gather([src], dst, d, before)
        for pos, g in enumerate(order):
            d = DILATIONS[g]
            per = TOK // d
            nb = per // BAND
            pad = per + BAND
            _window_rows(kp_refs[g], k_refs[g], kw_s, d)
            _window_rows(vp_refs[g], v_refs[g], vw_s, d)
            known = d if d in levels else DILATIONS[order[pos + 1]]
            _gather([levels[known][0]], dos, d, known)
            _gather([levels[known][1]], lds, d, known)
            for r in range(d):
                spare = pl.ds(r * pad, BAND)
                dkb[spare, :] = jnp.zeros((BAND, BAND), F32)
                dvb[spare, :] = jnp.zeros((BAND, BAND), F32)
            q_ref = q_refs[g]

            def unit(u, c_):
                r, b = u // nb, u % nb
                rows = pl.ds(pl.multiple_of(u * BAND, BAND), BAND)
                qu = q_ref[0, r, pl.ds(pl.multiple_of(b * BAND, BAND), BAND), :]
                dou, ldu = dos[rows, :], lds[rows, :]
                q2 = jnp.concatenate([qu * hm[0], qu * hm[1]], axis=0)
                do2 = jnp.concatenate([dou * hm[0], dou * hm[1]], axis=0)
                ld2 = jnp.concatenate([ldu * hm[0], ldu * hm[1]], axis=0)
                acc = pl.ds(pl.multiple_of(r * pad + b * BAND, BAND), 2 * BAND)
                kw = kw_s[acc, :]
                vw = vw_s[acc, :]
                lo = jnp.where(jnp.logical_and(i == 0, b == 0), BAND, 0)
                ok = jnp.logical_and(band_ok, key >= lo)
                st = _dot_nt(jnp.concatenate([kw, ones_l], axis=1), jnp.concatenate([q2, ld2], axis=1))
                dpt = _dot_nt(jnp.concatenate([vw, ones_d], axis=1), jnp.concatenate([do2, ld2], axis=1))
                pt = jnp.where(ok, jnp.exp(st), 0.0)
                dst = (pt * dpt).astype(BF16)
                low = pl.ds(pl.multiple_of(r * pad + b * BAND, BAND), BAND)
                high = pl.ds(pl.multiple_of(r * pad + (b + 1) * BAND, BAND), BAND)
                dkw = _dot(dst, q2)
                dvw = _dot(pt.astype(BF16), do2)
                dkb[low, :] += dkw[:BAND]
                dvb[low, :] += dvw[:BAND]
                dkb[high, :] = dkw[BAND:]
                dvb[high, :] = dvw[BAND:]
                dq2 = _dot_tn(dst, kw)
                dqb[rows, :] = jnp.where(head0, dq2[:BAND], dq2[BAND:])
                return c_

            lax.fori_loop(0, UNITS, unit, 0, unroll=16)

            for r in range(d):
                last = pl.ds(r * pad + per, BAND)
                kept = pl.ds(offs[g] + r * BAND, BAND)

                @pl.when(step > 0)
                def _():
                    dkb[last, :] += ckb[kept, :]
                    dvb[last, :] += cvb[kept, :]

                ckb[kept, :] = dkb[pl.ds(r * pad, BAND), :]
                cvb[kept, :] = dvb[pl.ds(r * pad, BAND), :]
            narrower = DILATIONS[order[pos + 1]] if pos + 1 < nd else None
            for n, (buf, out_ref, stride, at) in enumerate(
                    ((dqb, dq_ref, per, 0), (dkb, dk_ref, pad, BAND), (dvb, dv_ref, pad, BAND))):
                wider, onward = folds[2 * n + pos % 2], folds[2 * n + (pos + 1) % 2]
                for r in range(d):
                    val = buf[pl.ds(r * stride + at, per), :]
                    if pos > 0:
                        val = val + wider[pl.ds(r * per, per), :]
                    if narrower is None:
                        out_ref[...] = val
                    else:
                        start = (r % narrower) * (TOK // narrower) + r // narrower
                        onward[pl.ds(start, per, stride=d // narrower), :] = val

    main, prev = _dilated_specs(nblk, True)
    tok = pl.BlockSpec((TOK, BAND), lambda j, s: (nblk - 1 - s, j))
    acc_rows = max(d * (TOK // d + BAND) for d in DILATIONS)
    kept_rows = sum(DILATIONS) * BAND
    return _call(
        body, list(qs) + list(ks) + list(ks) + list(vs) + list(vs) + [do, lse, dd], name="attn_bwd",
        grid=(PAIRS, nblk), out_shape=[_sds((t, D_ATTN), F32)] * 3,
        in_specs=main + prev + main + prev + main + [tok] * 3, out_specs=[tok] * 3,
        scratch_shapes=[pltpu.VMEM((acc_rows, BAND), BF16)] * 2 + [pltpu.VMEM((TOK, BAND), BF16)] * 2
        + [pltpu.VMEM((TOK, BAND), F32)] * 2 + [pltpu.VMEM((acc_rows, BAND), F32)] * 2
        + [pltpu.VMEM((kept_rows, BAND), F32)] * 2
        + [pltpu.VMEM((TOK, BAND), F32)] * (6 + 2 * (nd - 2)),
        semantics=("parallel", "arbitrary"), carry=carry)


def _halo_rows(tm, t):
    per = tm // 8
    prev = lambda i: (jnp.maximum(i * per - 1, 0), 0)
    nxt = lambda i: (jnp.minimum((i + 1) * per, t // 8 - 1), 0)
    return prev, nxt


def _mixer_out(z, cw, y_attn, g_conv, g_attn, tm, carry=None):
    t = z.shape[0]
    prev, _ = _halo_rows(tm, t)

    def body(z_ref, zp_ref, cw_ref, y_ref, gc_ref, ga_ref, mix_ref):
        i = pl.program_id(0)
        keep = jnp.where(i > 0, 1.0, 0.0)
        cu = jnp.concatenate([zp_ref[:, 0:512] * zp_ref[:, 1024:1536] * keep,
                              z_ref[:, 0:512] * z_ref[:, 1024:1536]], axis=0)
        c = (cw_ref[0:1, :] * pltpu.roll(cu, 2, 0) + cw_ref[1:2, :] * pltpu.roll(cu, 1, 0)
             + cw_ref[2:3, :] * cu)[8:, :]
        yc = z_ref[:, 512:1024] * c
        mix_ref[:, 0:512] = (yc * _rms_scale(yc) * gc_ref[...]).astype(BF16)
        ya = y_ref[...]
        mix_ref[:, 512:1024] = (ya * _rms_scale(ya) * ga_ref[...]).astype(BF16)

    blk = pl.BlockSpec((tm, 512), lambda i: (i, 0))
    vec = pl.BlockSpec((1, 512), lambda i: (0, 0))
    return _call(
        body, [z, z, cw, y_attn, g_conv, g_attn], name="mixer_out", grid=(t // tm,),
        out_shape=_sds((t, 1024), BF16),
        in_specs=[pl.BlockSpec((tm, 1536), lambda i: (i, 0)), pl.BlockSpec((8, 1536), prev),
                  pl.BlockSpec((8, 512), lambda i: (0, 0)), blk, vec, vec],
        out_specs=pl.BlockSpec((tm, 1024), lambda i: (i, 0)),
        semantics=("parallel",), carry=carry)


def _mixer_bwd(z, dx1, wout, y_attn, cw, g_conv, g_attn, ones_bd, tm, carry=None):
    t = z.shape[0]
    nblk = t // tm
    prev, nxt = _halo_rows(tm, t)
    e = tm + 16

    def body(z_ref, zp_ref, zn_ref, dx_ref, dxn_ref, w_ref, y_ref, cw_ref, gc_ref, ga_ref, bd_ref,
             dz_ref, do_ref, dd_ref, dcw_ref, dgc_ref, dga_ref):
        i = pl.program_id(0)
        dm = _dot_nt(dx_ref[...], w_ref[...])
        dmn = _dot_nt(dxn_ref[...], w_ref[0:D_CONV, :])[0:8, :]
        rows = lax.broadcasted_iota(jnp.int32, (e, 1), 0)
        lo = jnp.where(i > 0, 0, 8)
        hi = jnp.where(i < nblk - 1, e, tm + 8)
        ze = jnp.concatenate([zp_ref[...], z_ref[...], zn_ref[...]], axis=0)
        u, gb, gcv = ze[:, 0:512], ze[:, 512:1024], ze[:, 1024:1536]
        w0, w1, w2 = cw_ref[0:1, :], cw_ref[1:2, :], cw_ref[2:3, :]
        cu = jnp.where(rows >= lo, gcv * u, 0.0)
        cu1, cu2 = pltpu.roll(cu, 1, 0), pltpu.roll(cu, 2, 0)
        c = w0 * cu2 + w1 * cu1 + w2 * cu
        yc = gb * c
        dma = jnp.concatenate([jnp.zeros((8, 512), F32), dm[:, 0:512], dmn], axis=0)
        dyc, ych = _rms_bwd(yc, _rms_scale(yc), gc_ref[...], dma)
        dc = jnp.where(jnp.logical_and(rows >= 8, rows < hi), dyc * gb, 0.0)
        dcu = w0 * pltpu.roll(dc, e - 2, 0) + w1 * pltpu.roll(dc, e - 1, 0) + w2 * dc
        mid = slice(8, 8 + tm)
        dz_ref[:, 0:512] = (dcu * gcv)[mid, :].astype(BF16)
        dz_ref[:, 512:1024] = (dyc * c)[mid, :].astype(BF16)
        dz_ref[:, 1024:1536] = (dcu * u)[mid, :].astype(BF16)

        ya = y_ref[...]
        dmb = dm[:, 512:1024]
        dya, yah = _rms_bwd(ya, _rms_scale(ya), ga_ref[...], dmb)
        do_ref[...] = dya
        dd_ref[...] = _head_sum(dya * ya, bd_ref[...])

        @pl.when(i == 0)
        def _():
            dcw_ref[...] = jnp.zeros_like(dcw_ref)
            dgc_ref[...] = jnp.zeros_like(dgc_ref)
            dga_ref[...] = jnp.zeros_like(dga_ref)

        dcm = jnp.where(rows < tm + 8, dc, 0.0)
        dcw_ref[0:1, :] += jnp.sum(dcm * cu2, axis=0, keepdims=True)
        dcw_ref[1:2, :] += jnp.sum(dcm * cu1, axis=0, keepdims=True)
        dcw_ref[2:3, :] += jnp.sum(dcm * cu, axis=0, keepdims=True)
        dgc_ref[...] += jnp.sum((dma * ych)[mid, :], axis=0, keepdims=True)
        dga_ref[...] += jnp.sum(dmb * yah, axis=0, keepdims=True)

    blk = pl.BlockSpec((tm, 512), lambda i: (i, 0))
    vec = pl.BlockSpec((1, 512), lambda i: (0, 0))
    cwb = pl.BlockSpec((8, 512), lambda i: (0, 0))
    next16 = lambda i: (jnp.minimum((i + 1) * (tm // 16), t // 16 - 1), 0)
    return _call(
        body, [z, z, z, dx1, dx1, wout, y_attn, cw, g_conv, g_attn, ones_bd], name="mixer_bwd",
        grid=(nblk,),
        out_shape=[_sds((t, D_IN), BF16), _sds((t, 512), F32), _sds((t, 512), F32),
                   _sds((8, 512), F32), _sds((1, 512), F32), _sds((1, 512), F32)],
        in_specs=[pl.BlockSpec((tm, 1536), lambda i: (i, 0)), pl.BlockSpec((8, 1536), prev),
                  pl.BlockSpec((8, 1536), nxt), pl.BlockSpec((tm, D_MODEL), lambda i: (i, 0)),
                  pl.BlockSpec((16, D_MODEL), next16),
                  pl.BlockSpec(wout.shape, lambda i: (0, 0), pipeline_mode=_resident(True)),
                  blk, cwb, vec, vec, pl.BlockSpec((512, 512), lambda i: (0, 0))],
        out_specs=[pl.BlockSpec((tm, 1536), lambda i: (i, 0)), blk, blk, cwb, vec, vec],
        carry=carry)


def _qkv_bwd(z, dz, dqn, dkn, dv, gq, gk, ones_bd, tm, carry=None):
    t = z.shape[0]

    def body(zq_ref, zk_ref, _, dqn_ref, dkn_ref, dv_ref, gq_ref, gk_ref, bd_ref,
             dz_ref, dgq_ref, dgk_ref):
        bd = bd_ref[...]

        @pl.when(pl.program_id(0) == 0)
        def _():
            dgq_ref[...] = jnp.zeros_like(dgq_ref)
            dgk_ref[...] = jnp.zeros_like(dgk_ref)

        def back(v, dn, g, scale):
            r = _head_rms_scale(v, bd)
            vh = v * r
            dh = dn * (g * scale)
            dv = r * (dh - vh * (_head_sum(dh * vh, bd) * (1.0 / HEAD_DIM)))
            return dv, jnp.sum(dn * scale * vh, axis=0, keepdims=True)

        dq, dgq = back(zq_ref[...], dqn_ref[...], gq_ref[...], HEAD_DIM ** -0.5)
        dk, dgk = back(zk_ref[...], dkn_ref[...], gk_ref[...], 1.0)
        dgq_ref[...] += dgq
        dgk_ref[...] += dgk
        dz_ref[:, 0:512] = dq.astype(BF16)
        dz_ref[:, 512:1024] = dk.astype(BF16)
        dz_ref[:, 1024:1536] = dv_ref[...].astype(BF16)

    blk = pl.BlockSpec((tm, 512), lambda i: (i, 0))
    vec = pl.BlockSpec((1, 512), lambda i: (0, 0))
    return _call(
        body, [z, z, dz, dqn, dkn, dv, gq, gk, ones_bd], name="qkv_bwd", grid=(t // tm,),
        out_shape=[_sds((t, D_IN), BF16), _sds((1, 512), F32), _sds((1, 512), F32)],
        in_specs=[pl.BlockSpec((tm, 512), lambda i: (i, 3)), pl.BlockSpec((tm, 512), lambda i: (i, 4)),
                  ANY] + [blk] * 3 + [vec, vec, pl.BlockSpec((512, 512), lambda i: (0, 0))],
        out_specs=[pl.BlockSpec((tm, 1536), lambda i: (i, 1)), vec, vec],
        carry=carry, aliases={2: 0})


def _columns_from_chips(g):
    return g.transpose(1, 0, 2).reshape(g.shape[1], N_CHIPS * g.shape[2])


def kernel(x, g_mix, w_in, conv_w, g_q, g_k, g_conv_out, g_attn_out, w_out, g_ffn, w_gate, w_up, w_down, loss_target, m_g_mix, m_w_in, m_conv_w, m_g_q, m_g_k, m_g_conv_out, m_g_attn_out, m_w_out, m_g_ffn, m_w_gate, m_w_up, m_w_down, v_g_mix, v_w_in, v_conv_w, v_g_q, v_g_k, v_g_conv_out, v_g_attn_out, v_w_out, v_g_ffn, v_w_gate, v_w_up, v_w_down):
    t = x.shape[1]
    xs = x[0]
    target = loss_target[0]
    tm = min(512, t)
    tm_wide = min(1024, t)
    tmm = min(2048, t)

    cw_pad = jnp.pad(conv_w[0], ((0, 13), (0, 0)))
    gathered = _all_gather([w_in[0].astype(BF16), cw_pad])
    win = _columns_from_chips(gathered[0])
    cw = jnp.pad(gathered[1][:, 0:3, :].transpose(1, 0, 2).reshape(3, D_CONV), ((0, 5), (0, 0)))
    later = [w_out[0].astype(BF16), w_gate[0].T.astype(BF16), w_up[0].T.astype(BF16),
             w_down[0].astype(BF16)]

    head_id = jnp.arange(D_ATTN) // HEAD_DIM
    ones_bd = (head_id[:, None] == head_id[None, :]).astype(BF16)
    gq_t = jnp.tile(g_q, (1, D_ATTN // HEAD_DIM))
    gk_t = jnp.tile(g_k, (1, D_ATTN // HEAD_DIM))

    h1, z, *dilated = _in_proj(xs, g_mix, win, gq_t, gk_t, ones_bd, tm)
    nd = len(DILATIONS)
    qs, ks, vs = dilated[:nd], dilated[nd:2 * nd], dilated[2 * nd:]
    (y_attn, lse), gathered = _attn_fwd(qs, ks, vs, carry=_x_gather_chips(later))
    mix, gathered = _mixer_out(z, cw, y_attn, g_conv_out, g_attn_out, tm_wide,
                               carry=_x_gather_sibling(gathered))
    wout = gathered[0].reshape(D_MODEL, D_MODEL)
    wgate_t = gathered[1].reshape(D_FF, D_MODEL)
    wup_t = gathered[2].reshape(D_FF, D_MODEL)
    wdown = gathered[3].reshape(D_FF, D_MODEL)
    (x1,) = _matmul("out_proj", mix, wout, [xs], [F32], lambda acc, r: (r + acc,), tm_wide, D_MODEL)
    h2, gate, up, act = _norm_matmul("ffn_up", x1, g_ffn, [wgate_t, wup_t], tm, D_FF, True, BF16,
                                     transposed_w=True)

    def loss_epilogue(acc, r, tgt):
        err = r + acc - tgt
        dy = err * (1.0 / D_MODEL)
        return dy, dy, jnp.sum(err * err)

    dx2, dx2b, loss_sum = _matmul("ffn_down_loss", act, wdown, [x1, target], [F32, BF16],
                                  loss_epilogue, tm_wide, D_MODEL, loss=True)

    def swiglu_bwd(da, gt, u):
        gt, u = gt.astype(F32), u.astype(F32)
        s = _sigmoid(gt)
        return da * u * (s * (1.0 + gt * (1.0 - s))), da * (gt * s)

    dgate, dup = _matmul("ffn_down_bwd", dx2b, wdown, [gate, up], [BF16, BF16], swiglu_bwd,
                         tm, D_FF, transposed_w=True)
    gw_down = _matmul_tn("grad_w_down", act, dx2b, 512, tmm)
    gw_gate_t = _matmul_tn("grad_w_gate", dgate, h2, 512, tmm)
    gw_up_t = _matmul_tn("grad_w_up", dup, h2, 512, tmm)

    me = 2 * lax.axis_index("x") + lax.axis_index("y")
    where = jnp.stack([lax.axis_index("c"), me]).astype(jnp.int32)

    def pair_sums(names, full, got):
        return [_pair_sum(f"pair_sum_{nme}", a, b, where) for nme, a, b in zip(names, full, got)]

    def chip_sums(names, pair, got):
        return [_chip_sum(f"chip_sum_{nme}", own, b) for nme, (_, own), b in zip(names, pair, got)]

    ffn = ["w_gate", "w_up", "w_down"]
    full = [g.reshape(N_CHIPS, D_FF // N_CHIPS, D_MODEL) for g in (gw_gate_t, gw_up_t, gw_down)]
    (dx1, dx1b, gg_ffn), got = _matmul_norm_bwd(
        "ffn_up_bwd", [(dgate, wgate_t), (dup, wup_t)], x1, dx2, g_ffn, tm, carry=_x_pair(full),
        transposed_w=False)
    pair = pair_sums(ffn, full, got)
    gw_out = _matmul_tn("grad_w_out", mix, dx1b, 512, tmm)
    full = [gw_out.reshape(N_CHIPS, D_MODEL // N_CHIPS, D_MODEL)]
    (dzc, do, dd, gcw, gg_conv, gg_attn), got = _mixer_bwd(
        z, dx1b, wout, y_attn, cw, g_conv_out, g_attn_out, ones_bd, tm, carry=_x_pair(full))
    pair += pair_sums(["w_out"], full, got)
    early = ffn + ["w_out"]
    (dqn, dkn, dv), got = _attn_bwd(qs, ks, vs, do, lse, dd, carry=_x_chips([p for p, _ in pair]))
    mine = chip_sums(early, pair, got)
    (dz, gg_q, gg_k), theirs = _qkv_bwd(z, dzc, dqn, dkn, dv, gq_t, gk_t, ones_bd, tm,
                                        carry=_x_share(mine))
    full = [_matmul_tn("grad_w_in", h1, dz, D_IN // N_CHIPS, tmm, by_chip=True)]
    got = _exchange_alone("grad_pair_exchange_w_in", _x_pair(full))
    pair = pair_sums(["w_in"], full, got)
    (grad_x, _, gg_mix), got = _matmul_norm_bwd("in_proj_bwd", [(dz, win)], xs, dx1, g_mix, tm_wide,
                                                carry=_x_chips([pair[0][0]]))
    mine += chip_sums(["w_in"], pair, got)
    theirs = list(theirs) + list(_exchange_alone("grad_pair_share_w_in", _x_share(mine[-1:])))
    big = early + ["w_in"]

    small = _small_all_reduce({
        "g_mix": gg_mix, "g_ffn": gg_ffn, "g_conv_out": gg_conv, "g_attn_out": gg_attn,
        "g_q": gg_q, "g_k": gg_k, "loss": loss_sum, "conv_w": gcw})
    heads = D_ATTN // HEAD_DIM
    grads = {
        "g_mix": small[0:1, :], "g_ffn": small[1:2, :],
        "g_conv_out": small[2:3, 0:512], "g_attn_out": small[2:3, 512:1024],
        "g_q": small[3, 0:512].reshape(heads, HEAD_DIM).sum(axis=0)[None, :],
        "g_k": small[3, 512:1024].reshape(heads, HEAD_DIM).sum(axis=0)[None, :],
        "conv_w": lax.dynamic_slice(small[8:11, 0:512], (0, me * (D_CONV // N_CHIPS)),
                                    (3, D_CONV // N_CHIPS)),
    }
    halves = dict(zip(big, zip(mine, theirs)))
    loss = small[4, 0] * 0.5 * (1.0 / D_MODEL)

    weights = dict(g_mix=g_mix, w_in=w_in, conv_w=conv_w, g_q=g_q, g_k=g_k, g_conv_out=g_conv_out,
                   g_attn_out=g_attn_out, w_out=w_out, g_ffn=g_ffn, w_gate=w_gate, w_up=w_up,
                   w_down=w_down)
    moments_m = dict(g_mix=m_g_mix, w_in=m_w_in, conv_w=m_conv_w, g_q=m_g_q, g_k=m_g_k,
                     g_conv_out=m_g_conv_out, g_attn_out=m_g_attn_out, w_out=m_w_out, g_ffn=m_g_ffn,
                     w_gate=m_w_gate, w_up=m_w_up, w_down=m_w_down)
    moments_v = dict(g_mix=v_g_mix, w_in=v_w_in, conv_w=v_conv_w, g_q=v_g_q, g_k=v_g_k,
                     g_conv_out=v_g_conv_out, g_attn_out=v_g_attn_out, w_out=v_w_out, g_ffn=v_g_ffn,
                     w_gate=v_w_gate, w_up=v_w_up, w_down=v_w_down)
    names = list(weights)
    out_g, out_d, out_m, out_v = [], [], [], []
    for nme in names:
        wgt = weights[nme]
        shape2 = wgt.shape[-2:] if wgt.ndim == 3 else wgt.shape
        flip = nme in ("w_gate", "w_up")

        def to2d(a):
            return a.reshape(shape2).T if flip else a.reshape(shape2)

        def back(a):
            return (a.T if flip else a).reshape(wgt.shape)

        state = (to2d(wgt), to2d(moments_m[nme]), to2d(moments_v[nme]))
        if nme in halves:
            g2, dlt, nm, nv = _adamw_shard(f"adamw_{nme}", *state, *halves[nme], where)
        else:
            g2 = grads[nme].reshape(shape2)
            dlt, nm, nv = _adamw(f"adamw_{nme}", state[0], g2, state[1], state[2])
        out_g.append(back(g2))
        out_d.append(back(dlt))
        out_m.append(back(nm))
        out_v.append(back(nv))
    return (loss, grad_x[None], *out_g, *out_d, *out_m, *out_v)
```

```python
import functools
from typing import Any, Callable, NamedTuple, Sequence

import jax
import jax.numpy as jnp
from jax import lax
from jax.experimental import pallas as pl
from jax.experimental.pallas import tpu as pltpu

F32 = jnp.float32
BF16 = jnp.bfloat16
MESH = pl.DeviceIdType.MESH

D_MODEL = 1024
D_CONV = 512
D_ATTN = 512
HEAD_DIM = 64
D_FF = 2816
D_IN = 3 * D_CONV + 3 * D_ATTN
DILATIONS = (1, 4, 16)
BAND = 128
EPS = 1e-6
NEG = -1e30
N_CHIPS = 4

ADAM_LR = 0.001
ADAM_B1 = 0.9
ADAM_B2 = 0.999
ADAM_EPS = 1e-08
ADAM_WD = 0.01
ADAM_STEP = 10

V7X_VMEM_BYTES = 64 * 1024 * 1024
VMEM_LIMIT = V7X_VMEM_BYTES - 8 * 1024 * 1024
ANY = pl.BlockSpec(memory_space=pl.ANY)
VMEM_WHOLE = pl.BlockSpec(memory_space=pltpu.VMEM)


def _params(*sem):
    return pltpu.CompilerParams(dimension_semantics=sem, vmem_limit_bytes=VMEM_LIMIT)


def _sds(shape, dtype):
    return jax.ShapeDtypeStruct(shape, dtype)


def _resident(whole):
    return pl.Buffered(1) if whole else None


def _place():
    x, y, c = lax.axis_index("x"), lax.axis_index("y"), lax.axis_index("c")
    chips = [(1 - x, y), (x, 1 - y), (1 - x, 1 - y)]
    return x, y, c, 2 * x + y, chips, [2 * cx + cy for cx, cy in chips]


def _all_gather(shards):
    n = len(shards)

    def body(*refs):
        ins, outs, stage = refs[:n], refs[n:2 * n], refs[2 * n:3 * n]
        ssem, rsem, fsem, gsem, lsem, osem = refs[3 * n:]
        x, y, c, me, chips, cids = _place()
        sib = (x, y, 1 - c)

        def half(w, which):
            h = shards[w].shape[0] // 2
            return pl.ds(pl.multiple_of(which * h, 8), h)

        loads = [pltpu.make_async_copy(ins[w], stage[w], lsem.at[w]) for w in range(n)]
        local = [pltpu.make_async_copy(stage[w], outs[w].at[me], osem.at[w]) for w in range(n)]
        for cp in loads:
            cp.start()

        def chip_copy(w, j, src_slot):
            rows = half(w, c)
            return pltpu.make_async_remote_copy(
                src_ref=ins[w].at[rows], dst_ref=outs[w].at[src_slot, rows],
                send_sem=ssem.at[3 * w + j], recv_sem=rsem.at[3 * w + j],
                device_id=(*chips[j], c), device_id_type=MESH)

        def sib_copy(w, j, which):
            rows = half(w, which)
            return pltpu.make_async_remote_copy(
                src_ref=outs[w].at[cids[j], rows], dst_ref=outs[w].at[cids[j], rows],
                send_sem=fsem.at[3 * w + j], recv_sem=gsem.at[3 * w + j],
                device_id=sib, device_id_type=MESH)

        sends = [chip_copy(w, j, me) for w in range(n) for j in range(3)]
        for cp in sends:
            cp.start()
        for w in range(n):
            loads[w].wait()
            local[w].start()
        passed = []
        for w in range(n):
            for j in range(3):
                chip_copy(w, j, cids[j]).wait_recv()
                cp = sib_copy(w, j, c)
                cp.start()
                passed.append(cp)
        for w in range(n):
            for j in range(3):
                sib_copy(w, j, 1 - c).wait_recv()
        for cp in sends + passed:
            cp.wait_send()
        for cp in local:
            cp.wait()

    return pl.pallas_call(
        body, name="all_gather_weights",
        out_shape=[_sds((N_CHIPS,) + s.shape, s.dtype) for s in shards],
        in_specs=[ANY] * n, out_specs=[ANY] * n,
        scratch_shapes=[pltpu.VMEM(s.shape, s.dtype) for s in shards]
        + [pltpu.SemaphoreType.DMA((3 * n,))] * 4 + [pltpu.SemaphoreType.DMA((n,))] * 2,
        compiler_params=pltpu.CompilerParams(vmem_limit_bytes=VMEM_LIMIT),
    )(*shards)


class _Exchange(NamedTuple):
    srcs: Sequence[Any]
    lands: Sequence[Any]
    outs: Sequence[Any]
    n_sems: int
    copies: Callable


def _remote(src, dst, ssem, rsem, k, to):
    return pltpu.make_async_remote_copy(src_ref=src, dst_ref=dst, send_sem=ssem.at[k],
                                        recv_sem=rsem.at[k], device_id=to, device_id_type=MESH)


def _x_gather_chips(shards):
    def copies(srcs, lands, outs, ssem, rsem):
        _, _, c, me, chips, cids = _place()
        go, arrive = [], []
        for w, s in enumerate(shards):
            h = s.shape[0] // 2
            rows = pl.ds(pl.multiple_of(c * h, 8), h)
            for j in range(3):
                to = (*chips[j], c)
                go.append(_remote(srcs[w].at[rows], lands[w].at[me, rows], ssem, rsem, 3 * w + j, to))
                arrive.append(_remote(srcs[w].at[rows], lands[w].at[cids[j], rows], ssem, rsem,
                                      3 * w + j, to))
        return go, arrive

    lands = [jnp.broadcast_to(s[None], (N_CHIPS,) + s.shape) for s in shards]
    return _Exchange(shards, lands, [], 3 * len(shards), copies)


def _x_gather_sibling(gathered):
    def copies(srcs, lands, outs, ssem, rsem):
        x, y, c, _, _, cids = _place()
        go, arrive = [], []
        for w, g in enumerate(gathered):
            h = g.shape[1] // 2
            mine = pl.ds(pl.multiple_of(c * h, 8), h)
            theirs = pl.ds(pl.multiple_of((1 - c) * h, 8), h)
            for j in range(3):
                slab = lands[w].at[cids[j]]
                go.append(_remote(slab.at[mine], slab.at[mine], ssem, rsem, 3 * w + j, (x, y, 1 - c)))
                arrive.append(_remote(slab.at[theirs], slab.at[theirs], ssem, rsem, 3 * w + j,
                                      (x, y, 1 - c)))
        return go, arrive

    return _Exchange([], gathered, [], 3 * len(gathered), copies)


def _x_pair(grads):
    def copies(srcs, lands, outs, ssem, rsem):
        x, y, c, _, _, _ = _place()
        go = []
        for w, g in enumerate(grads):
            h = g.shape[1] // 2
            theirs = pl.ds(pl.multiple_of((1 - c) * h, 8), h)
            go.append(_remote(srcs[w].at[:, theirs, :], outs[w], ssem, rsem, w, (x, y, 1 - c)))
        return go, go

    outs = [_sds((N_CHIPS, g.shape[1] // 2, g.shape[2]), g.dtype) for g in grads]
    return _Exchange(grads, [], outs, len(grads), copies)


def _x_chips(parts):
    def copies(srcs, lands, outs, ssem, rsem):
        _, _, c, _, chips, cids = _place()
        go = [_remote(srcs[w].at[cids[j]], outs[w].at[j], ssem, rsem, 3 * w + j, (*chips[j], c))
              for w in range(len(parts)) for j in range(3)]
        return go, go

    outs = [_sds((3,) + p.shape[1:], p.dtype) for p in parts]
    return _Exchange(parts, [], outs, 3 * len(parts), copies)


def _x_share(halves):
    def copies(srcs, lands, outs, ssem, rsem):
        x, y, c, _, _, _ = _place()
        go = [_remote(srcs[w], outs[w], ssem, rsem, w, (x, y, 1 - c)) for w in range(len(halves))]
        return go, go

    return _Exchange(halves, [], [_sds(h.shape, h.dtype) for h in halves], len(halves), copies)


def _call(body, args, *, name, grid, in_specs, out_specs, out_shape, scratch_shapes=(),
          semantics=None, carry=None, aliases=None):
    single = not isinstance(out_shape, (list, tuple))
    out_shape = [out_shape] if single else list(out_shape)
    out_specs = [out_specs] if single else list(out_specs)
    aliases = dict(aliases or {})
    if carry is None:
        res = pl.pallas_call(
            body, name=name, grid=grid, in_specs=list(in_specs), out_specs=out_specs,
            out_shape=out_shape, scratch_shapes=list(scratch_shapes), input_output_aliases=aliases,
            compiler_params=_params(*(semantics or ("arbitrary",) * len(grid))))(*args)
        return res[0] if single else res
    n_in, n_out, n_scr = len(args), len(out_shape), len(scratch_shapes)
    n_src, n_land, n_new = len(carry.srcs), len(carry.lands), len(carry.outs)

    def carrying(*refs):
        at = 0
        parts = []
        for n in (n_in, n_src, n_land, n_out, n_land, n_new, n_scr, 2):
            parts.append(refs[at:at + n])
            at += n
        ins, srcs, _, outs, lands, news, scratch, (ssem, rsem) = parts
        ids = [pl.program_id(a) for a in range(len(grid))]
        first = functools.reduce(jnp.logical_and, [i == 0 for i in ids])
        last = functools.reduce(jnp.logical_and, [i == g - 1 for i, g in zip(ids, grid)])
        go, arrive = carry.copies(srcs, lands, news, ssem, rsem)

        @pl.when(first)
        def _():
            for cp in go:
                cp.start()

        body(*ins, *outs, *scratch)

        @pl.when(last)
        def _():
            for cp in go:
                cp.wait_send()
            for cp in arrive:
                cp.wait_recv()

    res = pl.pallas_call(
        carrying, name=name, grid=grid,
        in_specs=list(in_specs) + [ANY] * (n_src + n_land),
        out_specs=out_specs + [ANY] * (n_land + n_new),
        out_shape=out_shape + [_sds(a.shape, a.dtype) for a in carry.lands] + list(carry.outs),
        input_output_aliases={**aliases, **{n_in + n_src + i: n_out + i for i in range(n_land)}},
        scratch_shapes=list(scratch_shapes) + [pltpu.SemaphoreType.DMA((carry.n_sems,))] * 2,
        compiler_params=_params(*(("arbitrary",) * len(grid))))(*args, *carry.srcs, *carry.lands)
    own = res[:n_out]
    return (own[0] if single else own), res[n_out:]


def _exchange_alone(name, exchange):
    def body(x_ref, o_ref):
        o_ref[...] = x_ref[...]

    blk = pl.BlockSpec((8, 128), lambda i: (0, 0))
    _, res = _call(body, [jnp.zeros((8, 128), F32)], name=name, grid=(1,), in_specs=[blk],
                   out_specs=blk, out_shape=_sds((8, 128), F32), carry=exchange)
    return res


def _row_block(r, want):
    return max(d for d in range(1, min(want, r) + 1) if r % d == 0 and (d % 8 == 0 or d == r))


def _pair_sum(name, full, got, where):
    _, r, n = full.shape
    h = r // 2
    tr = _row_block(h, 512)
    nb = h // tr

    def body(w_ref, a_ref, b_ref, o_ref, own_ref):
        total = a_ref[...] + b_ref[...]
        o_ref[...] = total.astype(BF16)

        @pl.when(pl.program_id(1) == w_ref[1])
        def _():
            own_ref[...] = total[0]

    blk = pl.BlockSpec((1, tr, n), lambda i, s, w: (s, i, 0))
    return pl.pallas_call(
        body, name=name, out_shape=[_sds(got.shape, BF16), _sds((h, n), F32)],
        grid_spec=pltpu.PrefetchScalarGridSpec(
            num_scalar_prefetch=1, grid=(nb, N_CHIPS),
            in_specs=[pl.BlockSpec((1, tr, n), lambda i, s, w: (s, w[0] * nb + i, 0)), blk],
            out_specs=[blk, pl.BlockSpec((tr, n), lambda i, s, w: (i, 0))]),
        compiler_params=_params("parallel", "arbitrary"),
    )(where, full, got)


def _chip_sum(name, own, got):
    h, n = own.shape
    tr = _row_block(h, 256)

    def body(a_ref, b0, b1, b2, o_ref):
        o_ref[...] = ((a_ref[...] + b0[0].astype(F32)) + b1[0].astype(F32)) + b2[0].astype(F32)

    def slot(j):
        return pl.BlockSpec((1, tr, n), lambda i: (j, i, 0))

    blk = pl.BlockSpec((tr, n), lambda i: (i, 0))
    return pl.pallas_call(
        body, name=name, grid=(h // tr,), out_shape=_sds((h, n), F32),
        in_specs=[blk, slot(0), slot(1), slot(2)], out_specs=blk,
        compiler_params=_params("parallel"),
    )(own, got, got, got)


SMALL_ROWS = 16
SMALL_LAYOUT = (
    ("g_mix", 0, 0, 1, 1024), ("g_ffn", 1, 0, 1, 1024), ("g_conv_out", 2, 0, 1, 512),
    ("g_attn_out", 2, 512, 1, 512), ("g_q", 3, 0, 1, 512), ("g_k", 3, 512, 1, 512),
    ("loss", 4, 0, 1, 128), ("conv_w", 8, 0, 8, 512))


def _small_all_reduce(parts):
    names = [s[0] for s in SMALL_LAYOUT]

    def body(*refs):
        ins = refs[:len(names)]
        out_ref, stage, buf, ssem, rsem = refs[len(names):]
        x, y, c, _, _, _ = _place()
        me = 4 * x + 2 * y + c
        stage[...] = jnp.zeros_like(stage)
        for ref, (_, r0, c0, nr, nc) in zip(ins, SMALL_LAYOUT):
            stage[r0:r0 + nr, c0:c0 + nc] = ref[0:nr, :]
        buf[me] = stage[...]
        peers = []
        for d in range(1, 8):
            px = 1 - x if d & 4 else x
            py = 1 - y if d & 2 else y
            pc = 1 - c if d & 1 else c
            peers.append(((px, py, pc), 4 * px + 2 * py + pc))
        sends = [pltpu.make_async_remote_copy(
            src_ref=stage, dst_ref=buf.at[me], send_sem=ssem.at[k], recv_sem=rsem.at[k],
            device_id=peer, device_id_type=MESH) for k, (peer, _) in enumerate(peers)]
        for cp in sends:
            cp.start()
        for k, (peer, pid) in enumerate(peers):
            pltpu.make_async_remote_copy(
                src_ref=stage, dst_ref=buf.at[pid], send_sem=ssem.at[k], recv_sem=rsem.at[k],
                device_id=peer, device_id_type=MESH).wait_recv()
        for cp in sends:
            cp.wait_send()
        acc = buf[0]
        for k in range(1, 8):
            acc = acc + buf[k]
        out_ref[...] = acc

    return pl.pallas_call(
        body, name="small_all_reduce", out_shape=_sds((SMALL_ROWS, 1024), F32),
        in_specs=[VMEM_WHOLE] * len(names), out_specs=VMEM_WHOLE,
        scratch_shapes=[pltpu.VMEM((SMALL_ROWS, 1024), F32), pltpu.VMEM((8, SMALL_ROWS, 1024), F32),
                        pltpu.SemaphoreType.DMA((7,)), pltpu.SemaphoreType.DMA((7,))],
    )(*[parts[k] for k in names])


def _dot(a, b):
    return jnp.dot(a, b, preferred_element_type=F32)


def _dot_nt(a, b):
    return lax.dot_general(a, b, (((1,), (1,)), ((), ())), preferred_element_type=F32)


def _dot_tn(a, b):
    return lax.dot_general(a, b, (((0,), (0,)), ((), ())), preferred_element_type=F32)


def _sigmoid(v):
    return 1.0 / (1.0 + jnp.exp(-v))


def _rms_scale(v):
    return lax.rsqrt(jnp.mean(v * v, axis=-1, keepdims=True) + EPS)


def _rms_bwd(v, r, g, dy):
    vh = v * r
    dh = dy * g
    return r * (dh - vh * jnp.mean(dh * vh, axis=-1, keepdims=True)), vh


def _head_sum(a, ones_bd):
    hi = a.astype(BF16)
    lo = (a - hi.astype(F32)).astype(BF16)
    return _dot(hi, ones_bd) + _dot(lo, ones_bd)


def _head_rms_scale(v, ones_bd):
    return lax.rsqrt(_head_sum(v * v, ones_bd) * (1.0 / HEAD_DIM) + EPS)


MXU_COLUMNS = 256


def _column_chunks(n):
    width = MXU_COLUMNS if n % MXU_COLUMNS == 0 else n
    return [slice(c, c + width) for c in range(0, n, width)]


def _norm_matmul(name, x, g, ws, tm, tn, swiglu, out_dtype=F32, transposed_w=False):
    t, d = x.shape
    n = ws[0].shape[0] if transposed_w else ws[0].shape[1]
    nw = len(ws)

    def body(x_ref, g_ref, *refs):
        w_refs, h_ref, o_refs = refs[:nw], refs[nw], refs[nw + 1:2 * nw + 1]
        hs = refs[-1]

        @pl.when(pl.program_id(1) == 0)
        def _():
            xv = x_ref[...]
            h = (xv * _rms_scale(xv) * g_ref[...]).astype(BF16)
            hs[...] = h
            h_ref[...] = h

        h = hs[...]
        for cols in _column_chunks(tn):
            outs = [_dot_nt(h, w[cols, :]) if transposed_w else _dot(h, w[:, cols]) for w in w_refs]
            for o_ref, o in zip(o_refs, outs):
                o_ref[:, cols] = o.astype(out_dtype)
            if swiglu:
                refs[2 * nw + 1][:, cols] = (outs[0] * _sigmoid(outs[0]) * outs[1]).astype(BF16)

    row = pl.BlockSpec((tm, d), lambda i, j: (i, 0))
    col = pl.BlockSpec((tm, tn), lambda i, j: (i, j))
    out_shape = [_sds((t, d), BF16)] + [_sds((t, n), out_dtype)] * nw
    out_specs = [row] + [col] * nw
    if swiglu:
        out_shape.append(_sds((t, n), BF16))
        out_specs.append(col)
    return pl.pallas_call(
        body, name=name, grid=(t // tm, n // tn), out_shape=out_shape,
        in_specs=[row, pl.BlockSpec((1, d), lambda i, j: (0, 0))]
        + [pl.BlockSpec((tn, d), lambda i, j: (j, 0), pipeline_mode=_resident(tn == n))
           if transposed_w
           else pl.BlockSpec((d, tn), lambda i, j: (0, j), pipeline_mode=_resident(tn == n))] * nw,
        out_specs=out_specs, scratch_shapes=[pltpu.VMEM((tm, d), BF16)],
        compiler_params=_params("parallel", "arbitrary"),
    )(x, g, *ws)


def _matmul(name, a, w, extras, out_dtypes, epilogue, tm, tn, transposed_w=False, loss=False):
    t, k = a.shape
    n = w.shape[0] if transposed_w else w.shape[1]
    ne, no = len(extras), len(out_dtypes)

    def body(a_ref, w_ref, *refs):
        e_refs, o_refs = refs[:ne], refs[ne:]
        a = a_ref[...]
        total = 0.0
        for cols in _column_chunks(tn):
            acc = _dot_nt(a, w_ref[cols, :]) if transposed_w else _dot(a, w_ref[:, cols])
            res = epilogue(acc, *[e[:, cols] for e in e_refs])
            for o_ref, r in zip(o_refs[:no], res[:no]):
                o_ref[:, cols] = r.astype(o_ref.dtype)
            if loss:
                total = total + res[no]
        if loss:
            first = jnp.logical_and(pl.program_id(0) == 0, pl.program_id(1) == 0)

            @pl.when(first)
            def _():
                o_refs[no][...] = jnp.zeros_like(o_refs[no])

            o_refs[no][...] += total

    col = pl.BlockSpec((tm, tn), lambda i, j: (i, j))
    w_spec = (pl.BlockSpec((tn, k), lambda i, j: (j, 0), pipeline_mode=_resident(tn == n))
              if transposed_w
              else pl.BlockSpec((k, tn), lambda i, j: (0, j), pipeline_mode=_resident(tn == n)))
    out_shape = [_sds((t, n), dt) for dt in out_dtypes]
    out_specs = [col] * no
    if loss:
        out_shape.append(_sds((8, 128), F32))
        out_specs.append(pl.BlockSpec((8, 128), lambda i, j: (0, 0)))
    return pl.pallas_call(
        body, name=name, grid=(t // tm, n // tn), out_shape=out_shape,
        in_specs=[pl.BlockSpec((tm, k), lambda i, j: (i, 0)), w_spec] + [col] * ne,
        out_specs=out_specs,
        compiler_params=_params(*(("arbitrary", "arbitrary") if loss else ("parallel", "parallel"))),
    )(a, w, *extras)


def _matmul_norm_bwd(name, pairs, x, dres, g, tm, carry=None, transposed_w=True):
    t, d = x.shape
    npairs = len(pairs)
    product = _dot_nt if transposed_w else _dot

    def body(*refs):
        a_refs, w_refs = refs[:npairs], refs[npairs:2 * npairs]
        x_ref, r_ref, g_ref, dx_ref, dxb_ref, dg_ref = refs[2 * npairs:]
        dy = product(a_refs[0][...], w_refs[0][...])
        for a_ref, w_ref in zip(a_refs[1:], w_refs[1:]):
            dy = dy + product(a_ref[...], w_ref[...])
        xv = x_ref[...]
        dx, xh = _rms_bwd(xv, _rms_scale(xv), g_ref[...], dy)
        dx = dx + r_ref[...]
        dx_ref[...] = dx
        dxb_ref[...] = dx.astype(BF16)

        @pl.when(pl.program_id(0) == 0)
        def _():
            dg_ref[...] = jnp.zeros_like(dg_ref)

        dg_ref[...] += jnp.sum(dy * xh, axis=0, keepdims=True)

    row = pl.BlockSpec((tm, d), lambda i: (i, 0))
    vec = pl.BlockSpec((1, d), lambda i: (0, 0))
    return _call(
        body, [a for a, _ in pairs] + [w for _, w in pairs] + [x, dres, g], name=name,
        grid=(t // tm,), out_shape=[_sds((t, d), F32), _sds((t, d), BF16), _sds((1, d), F32)],
        in_specs=[pl.BlockSpec((tm, a.shape[1]), lambda i: (i, 0)) for a, _ in pairs]
        + [pl.BlockSpec(w.shape, lambda i: (0, 0), pipeline_mode=pl.Buffered(1)) for _, w in pairs]
        + [row, row, vec],
        out_specs=[row, row, vec], carry=carry)


def _matmul_tn(name, a, g, tn, tk, by_chip=False):
    t, ka = a.shape
    n = g.shape[1]

    def body(a_ref, g_ref, o_ref):
        @pl.when(pl.program_id(1) == 0)
        def _():
            o_ref[...] = jnp.zeros_like(o_ref)

        acc = _dot_tn(a_ref[...], g_ref[...])
        o_ref[...] += acc[None] if by_chip else acc

    return pl.pallas_call(
        body, name=name, grid=(n // tn, t // tk),
        out_shape=_sds((n // tn, ka, tn) if by_chip else (ka, n), F32),
        in_specs=[pl.BlockSpec((tk, ka), lambda j, s: (s, 0)),
                  pl.BlockSpec((tk, tn), lambda j, s: (s, j))],
        out_specs=(pl.BlockSpec((1, ka, tn), lambda j, s: (j, 0, 0)) if by_chip
                   else pl.BlockSpec((ka, tn), lambda j, s: (0, j))),
        compiler_params=_params("parallel", "arbitrary"),
    )(a, g)


def _elementwise(name, fn, ins, out_dtypes, tr):
    r, n = ins[0].shape
    tr = _row_block(r, tr)
    ni = len(ins)

    def body(*refs):
        res = fn(*[ref[...] for ref in refs[:ni]])
        for o_ref, v in zip(refs[ni:], res):
            o_ref[...] = v.astype(o_ref.dtype)

    blk = pl.BlockSpec((tr, n), lambda i: (i, 0))
    return pl.pallas_call(
        body, name=name, grid=(r // tr,), out_shape=[_sds((r, n), dt) for dt in out_dtypes],
        in_specs=[blk] * ni, out_specs=[blk] * len(out_dtypes),
        compiler_params=_params("parallel"),
    )(*ins)


def _adamw_update(w, g, m, v):
    m = ADAM_B1 * m + (1.0 - ADAM_B1) * g
    v = ADAM_B2 * v + (1.0 - ADAM_B2) * (g * g)
    m_hat = m / (1.0 - ADAM_B1 ** ADAM_STEP)
    v_hat = v / (1.0 - ADAM_B2 ** ADAM_STEP)
    return -ADAM_LR * (m_hat / (jnp.sqrt(v_hat) + ADAM_EPS) + ADAM_WD * w), m, v


def _adamw(name, w, g, m, v):
    return _elementwise(name, _adamw_update, [w, g, m, v], [F32] * 3, 256)


def _adamw_shard(name, w, m, v, mine, theirs, where):
    r, n = w.shape
    h = r // 2
    tr = _row_block(h, 256)
    nb = h // tr

    def body(w_ref, p_ref, m_ref, v_ref, a_ref, b_ref, g_ref, d_ref, nm_ref, nv_ref):
        g = jnp.where(pl.program_id(0) == w_ref[0], a_ref[...], b_ref[...])
        g_ref[...] = g
        d_ref[...], nm_ref[...], nv_ref[...] = _adamw_update(p_ref[...], g, m_ref[...], v_ref[...])

    whole = pl.BlockSpec((tr, n), lambda s, i, c: (s * nb + i, 0))
    used = pl.BlockSpec((tr, n), lambda s, i, c: (jnp.where(s == c[0], i, 0), 0))
    unused = pl.BlockSpec((tr, n), lambda s, i, c: (jnp.where(s == c[0], 0, i), 0))
    return pl.pallas_call(
        body, name=name, out_shape=[_sds((r, n), F32)] * 4,
        grid_spec=pltpu.PrefetchScalarGridSpec(
            num_scalar_prefetch=1, grid=(2, nb), in_specs=[whole] * 3 + [used, unused],
            out_specs=[whole] * 4),
        compiler_params=_params("arbitrary", "arbitrary"),
    )(where, w, m, v, mine, theirs)


PAIRS = D_ATTN // BAND


def _in_proj(x, g, w, gq, gk, ones_bd, tm):
    t, dm = x.shape
    n = w.shape[1]
    nd = len(DILATIONS)
    first = 3 * D_CONV

    def body(x_ref, g_ref, w_ref, gq_ref, gk_ref, bd_ref, h_ref, z_ref, *refs):
        outs, slabs = refs[:3 * nd], refs[3 * nd:]
        xv = x_ref[...]
        h = (xv * _rms_scale(xv) * g_ref[...]).astype(BF16)
        h_ref[...] = h
        for cols in _column_chunks(n):
            z_ref[:, cols] = _dot(h, w_ref[:, cols])
        bd = bd_ref[...]
        q = z_ref[:, first:first + D_ATTN]
        k = z_ref[:, first + D_ATTN:first + 2 * D_ATTN]
        vals = [(q * _head_rms_scale(q, bd) * gq_ref[...]) * HEAD_DIM ** -0.5,
                k * _head_rms_scale(k, bd) * gk_ref[...], z_ref[:, first + 2 * D_ATTN:n]]
        for m, val in enumerate(vals):
            for c in range(PAIRS):
                slabs[0][c] = val[:, c * BAND:(c + 1) * BAND]
            cur, before = 0, 1
            for a, d in enumerate(DILATIONS):
                o_ref, src, dst = outs[m * nd + a], slabs[cur], slabs[1 - cur]
                step, count = d // before, tm // d
                keep = step > 1 and a + 1 < nd
                for c in range(PAIRS):
                    for r in range(d):
                        start = (r % before) * (tm // before) + r // before
                        rows = src.at[c][pl.ds(start, count, stride=step), :] if step > 1 else src[c]
                        o_ref[c, r] = rows.astype(BF16)
                        if keep:
                            dst.at[c][pl.ds(r * count, count), :] = rows
                if keep:
                    cur = 1 - cur
                before = d

    row = pl.BlockSpec((tm, dm), lambda i: (i, 0))
    vec = pl.BlockSpec((1, D_ATTN), lambda i: (0, 0))
    return pl.pallas_call(
        body, name="in_proj", grid=(t // tm,),
        out_shape=[_sds((t, dm), BF16), _sds((t, n), F32)]
        + [_sds((PAIRS, d, t // d, BAND), BF16) for _ in range(3) for d in DILATIONS],
        in_specs=[row, pl.BlockSpec((1, dm), lambda i: (0, 0)),
                  pl.BlockSpec((dm, n), lambda i: (0, 0), pipeline_mode=_resident(True)), vec, vec,
                  pl.BlockSpec((D_ATTN, D_ATTN), lambda i: (0, 0), pipeline_mode=_resident(True))],
        out_specs=[row, pl.BlockSpec((tm, n), lambda i: (i, 0))]
        + [pl.BlockSpec((PAIRS, d, tm // d, BAND), lambda i: (0, 0, i, 0))
           for _ in range(3) for d in DILATIONS],
        scratch_shapes=[pltpu.VMEM((PAIRS, tm, BAND), F32)] * 2,
        compiler_params=_params("parallel"),
    )(x, g, w, gq, gk, ones_bd)


TOK = 2048
UNITS = TOK // BAND


def _stack_masks():
    row = lax.broadcasted_iota(jnp.int32, (2 * BAND, 2 * BAND), 0) & (BAND - 1)
    col = lax.broadcasted_iota(jnp.int32, (2 * BAND, 2 * BAND), 1)
    lane = lax.broadcasted_iota(jnp.int32, (BAND, BAND), 1)
    head0 = lane < HEAD_DIM
    ones = [jnp.where(head0, 1.0, 0.0).astype(BF16), jnp.where(head0, 0.0, 1.0).astype(BF16)]
    return col - row, col, head0, ones


def _split3(x):
    hi = x.astype(BF16).astype(F32)
    mid = (x - hi).astype(BF16).astype(F32)
    return hi, mid, x - hi - mid


def _gather(srcs, dst, d, before=1):
    per, step, span = TOK // d, d // before, TOK // before
    at = 0
    for r in range(d):
        start = (r % before) * span + r // before
        for src in srcs:
            rows = src[pl.ds(start, per, stride=step), :] if step > 1 else src[pl.ds(start, per), :]
            dst[pl.ds(at, per), :] = rows.astype(dst.dtype)
            at += per


def _scatter(out_ref, src, d):
    per = TOK // d
    if d == 1:
        out_ref[...] = src[...]
        return
    for r in range(d):
        out_ref[pl.ds(r, per, stride=d), :] = src[pl.ds(r * per, per), :]


def _dilated_specs(nblk, reverse):
    def at(s):
        return (nblk - 1 - s) if reverse else s
    main = [pl.BlockSpec((1, d, TOK // d, BAND), lambda j, s: (j, 0, at(s), 0)) for d in DILATIONS]
    prev = [pl.BlockSpec((1, d, TOK // d, BAND), lambda j, s: (j, 0, jnp.maximum(at(s) - 1, 0), 0))
            for d in DILATIONS]
    return main, prev


def _window_rows(prev_ref, main_ref, dst, d):
    per = TOK // d
    for r in range(d):
        dst[pl.ds(r * (per + BAND), BAND), :] = prev_ref[0, r, pl.ds(per - BAND, BAND), :]
        dst[pl.ds(r * (per + BAND) + BAND, per), :] = main_ref[0, r]


def _attn_fwd(qs, ks, vs, carry=None):
    t = qs[0].shape[2]
    nblk = t // TOK
    nd = len(DILATIONS)

    def body(*refs):
        q_refs, kp_refs, k_refs = refs[:nd], refs[nd:2 * nd], refs[2 * nd:3 * nd]
        vp_refs, v_refs = refs[3 * nd:4 * nd], refs[4 * nd:5 * nd]
        y_ref, l_ref, kw_s, vw_s, ob, lb, on, ln = refs[5 * nd:]
        i = pl.program_id(1)
        diff, col, head0, hm = _stack_masks()
        band_ok = jnp.logical_and(diff >= 0, diff <= BAND)
        for g, d in enumerate(DILATIONS):
            per = TOK // d
            nb = per // BAND
            pad = per + BAND
            _window_rows(kp_refs[g], k_refs[g], kw_s, d)
            _window_rows(vp_refs[g], v_refs[g], vw_s, d)
            q_ref = q_refs[g]

            def unit(u, carry):
                r, b = u // nb, u % nb
                qu = q_ref[0, r, pl.ds(pl.multiple_of(b * BAND, BAND), BAND), :]
                start = pl.multiple_of(r * pad + b * BAND, BAND)
                kw = kw_s[pl.ds(start, 2 * BAND), :]
                vw = vw_s[pl.ds(start, 2 * BAND), :]
                lo = jnp.where(jnp.logical_and(i == 0, b == 0), BAND, 0)
                s = _dot_nt(jnp.concatenate([qu * hm[0], qu * hm[1]], axis=0), kw)
                s = jnp.where(jnp.logical_and(band_ok, col >= lo), s, NEG)
                mx = jnp.max(s, axis=-1, keepdims=True)
                e = jnp.exp(s - mx)
                den = jnp.sum(e, axis=-1, keepdims=True)
                o2 = _dot(e.astype(BF16), vw) / den
                l2 = jnp.broadcast_to(mx + jnp.log(den), (2 * BAND, BAND))
                rows = pl.ds(pl.multiple_of(u * BAND, BAND), BAND)
                ob[rows, :] = jnp.where(head0, o2[:BAND], o2[BAND:])
                lb[rows, :] = jnp.where(head0, l2[:BAND], l2[BAND:])
                return carry

            lax.fori_loop(0, UNITS, unit, 0, unroll=16)
            _scatter(on.at[g], ob, d)
            _scatter(ln.at[g], lb, d)
        ls = [ln[0], ln[1], ln[2]]
        mx = jnp.maximum(jnp.maximum(ls[0], ls[1]), ls[2])
        es = [jnp.exp(l - mx) for l in ls]
        tot = es[0] + es[1] + es[2]
        y_ref[...] = (es[0] * on[0] + es[1] * on[1] + es[2] * on[2]) / tot
        l_ref[...] = mx + jnp.log(tot)

    main, prev = _dilated_specs(nblk, False)
    out = pl.BlockSpec((TOK, BAND), lambda j, i: (i, j))
    win_rows = max(d * (TOK // d + BAND) for d in DILATIONS)
    return _call(
        body, list(qs) + list(ks) + list(ks) + list(vs) + list(vs), name="attn_fwd",
        grid=(PAIRS, nblk), out_shape=[_sds((t, D_ATTN), F32)] * 2,
        in_specs=main + prev + main + prev + main, out_specs=[out, out],
        scratch_shapes=[pltpu.VMEM((win_rows, BAND), BF16)] * 2 + [pltpu.VMEM((TOK, BAND), F32)] * 2
        + [pltpu.VMEM((nd, TOK, BAND), F32)] * 2,
        semantics=("parallel", "parallel"), carry=carry)


def _attn_bwd(qs, ks, vs, do, lse, dd, carry=None):
    t = qs[0].shape[2]
    nblk = t // TOK
    nd = len(DILATIONS)
    offs = [sum(DILATIONS[:g]) * BAND for g in range(nd)]

    def body(*refs):
        q_refs, kp_refs, k_refs = refs[:nd], refs[nd:2 * nd], refs[2 * nd:3 * nd]
        vp_refs, v_refs = refs[3 * nd:4 * nd], refs[4 * nd:5 * nd]
        (do_ref, l_ref, d_ref, dq_ref, dk_ref, dv_ref, kw_s, vw_s, dos, lds, pn, dqb, dkb, dvb, ckb,
         cvb, *more) = refs[5 * nd:]
        folds, mids = more[:6], more[6:]
        step = pl.program_id(1)
        i = nblk - 1 - step
        key = lax.broadcasted_iota(jnp.int32, (2 * BAND, 2 * BAND), 0)
        qry = lax.broadcasted_iota(jnp.int32, (2 * BAND, 2 * BAND), 1) & (BAND - 1)
        off = key - qry
        band_ok = jnp.logical_and(off >= 0, off <= BAND)
        lane = lax.broadcasted_iota(jnp.int32, (BAND, BAND), 1)
        head0 = lane < HEAD_DIM
        hm = [jnp.where(head0, 1.0, 0.0).astype(BF16), jnp.where(head0, 0.0, 1.0).astype(BF16)]
        lane2 = lax.broadcasted_iota(jnp.int32, (2 * BAND, BAND), 1) & (HEAD_DIM - 1)
        ones_l = jnp.where(lane2 < 3, 1.0, 0.0).astype(BF16)
        ones_d = jnp.where(jnp.logical_and(lane2 >= 3, lane2 < 6), 1.0, 0.0).astype(BF16)
        piece = lax.broadcasted_iota(jnp.int32, (TOK, BAND), 1) & (HEAD_DIM - 1)

        def pieces(x, at):
            hi, mid, lo = _split3(-x)
            return jnp.where(piece == at, hi,
                             jnp.where(piece == at + 1, mid, jnp.where(piece == at + 2, lo, 0.0)))

        pn[...] = pieces(l_ref[...], 0) + pieces(d_ref[...], 3)
        order = sorted(range(nd), key=lambda a: -DILATIONS[a])
        assert DILATIONS[order[-1]] == 1
        levels = {1: (do_ref, pn)}
        for n, a in enumerate(reversed(order[1:-1])):
            d, before = DILATIONS[a], DILATIONS[order[-1 - n]]
            levels[d] = (mids[2 * n], mids[2 * n + 1])
            for src, dst in zip(levels[before], levels[d]):
                _gather([src], dst, d, before)
        for pos, g in enumerate(order):
            d = DILATIONS[g]
            per = TOK // d
            nb = per // BAND
            pad = per + BAND
            _window_rows(kp_refs[g], k_refs[g], kw_s, d)
            _window_rows(vp_refs[g], v_refs[g], vw_s, d)
            known = d if d in levels else DILATIONS[order[pos + 1]]
            _gather([levels[known][0]], dos, d, known)
            _gather([levels[known][1]], lds, d, known)
            for r in range(d):
                spare = pl.ds(r * pad, BAND)
                dkb[spare, :] = jnp.zeros((BAND, BAND), F32)
                dvb[spare, :] = jnp.zeros((BAND, BAND), F32)
            q_ref = q_refs[g]

            def unit(u, c_):
                r, b = u // nb, u % nb
                rows = pl.ds(pl.multiple_of(u * BAND, BAND), BAND)
                qu = q_ref[0, r, pl.ds(pl.multiple_of(b * BAND, BAND), BAND), :]
                dou, ldu = dos[rows, :], lds[rows, :]
                q2 = jnp.concatenate([qu * hm[0], qu * hm[1]], axis=0)
                do2 = jnp.concatenate([dou * hm[0], dou * hm[1]], axis=0)
                ld2 = jnp.concatenate([ldu * hm[0], ldu * hm[1]], axis=0)
                acc = pl.ds(pl.multiple_of(r * pad + b * BAND, BAND), 2 * BAND)
                kw = kw_s[acc, :]
                vw = vw_s[acc, :]
                lo = jnp.where(jnp.logical_and(i == 0, b == 0), BAND, 0)
                ok = jnp.logical_and(band_ok, key >= lo)
                st = _dot_nt(jnp.concatenate([kw, ones_l], axis=1), jnp.concatenate([q2, ld2], axis=1))
                dpt = _dot_nt(jnp.concatenate([vw, ones_d], axis=1), jnp.concatenate([do2, ld2], axis=1))
                pt = jnp.where(ok, jnp.exp(st), 0.0)
                dst = (pt * dpt).astype(BF16)
                low = pl.ds(pl.multiple_of(r * pad + b * BAND, BAND), BAND)
                high = pl.ds(pl.multiple_of(r * pad + (b + 1) * BAND, BAND), BAND)
                dkw = _dot(dst, q2)
                dvw = _dot(pt.astype(BF16), do2)
                dkb[low, :] += dkw[:BAND]
                dvb[low, :] += dvw[:BAND]
                dkb[high, :] = dkw[BAND:]
                dvb[high, :] = dvw[BAND:]
                dq2 = _dot_tn(dst, kw)
                dqb[rows, :] = jnp.where(head0, dq2[:BAND], dq2[BAND:])
                return c_

            lax.fori_loop(0, UNITS, unit, 0, unroll=16)

            for r in range(d):
                last = pl.ds(r * pad + per, BAND)
                kept = pl.ds(offs[g] + r * BAND, BAND)

                @pl.when(step > 0)
                def _():
                    dkb[last, :] += ckb[kept, :]
                    dvb[last, :] += cvb[kept, :]

                ckb[kept, :] = dkb[pl.ds(r * pad, BAND), :]
                cvb[kept, :] = dvb[pl.ds(r * pad, BAND), :]
            narrower = DILATIONS[order[pos + 1]] if pos + 1 < nd else None
            for n, (buf, out_ref, stride, at) in enumerate(
                    ((dqb, dq_ref, per, 0), (dkb, dk_ref, pad, BAND), (dvb, dv_ref, pad, BAND))):
                wider, onward = folds[2 * n + pos % 2], folds[2 * n + (pos + 1) % 2]
                for r in range(d):
                    val = buf[pl.ds(r * stride + at, per), :]
                    if pos > 0:
                        val = val + wider[pl.ds(r * per, per), :]
                    if narrower is None:
                        out_ref[...] = val
                    else:
                        start = (r % narrower) * (TOK // narrower) + r // narrower
                        onward[pl.ds(start, per, stride=d // narrower), :] = val

    main, prev = _dilated_specs(nblk, True)
    tok = pl.BlockSpec((TOK, BAND), lambda j, s: (nblk - 1 - s, j))
    acc_rows = max(d * (TOK // d + BAND) for d in DILATIONS)
    kept_rows = sum(DILATIONS) * BAND
    return _call(
        body, list(qs) + list(ks) + list(ks) + list(vs) + list(vs) + [do, lse, dd], name="attn_bwd",
        grid=(PAIRS, nblk), out_shape=[_sds((t, D_ATTN), F32)] * 3,
        in_specs=main + prev + main + prev + main + [tok] * 3, out_specs=[tok] * 3,
        scratch_shapes=[pltpu.VMEM((acc_rows, BAND), BF16)] * 2 + [pltpu.VMEM((TOK, BAND), BF16)] * 2
        + [pltpu.VMEM((TOK, BAND), F32)] * 2 + [pltpu.VMEM((acc_rows, BAND), F32)] * 2
        + [pltpu.VMEM((kept_rows, BAND), F32)] * 2
        + [pltpu.VMEM((TOK, BAND), F32)] * (6 + 2 * (nd - 2)),
        semantics=("parallel", "arbitrary"), carry=carry)


def _halo_rows(tm, t):
    per = tm // 8
    prev = lambda i: (jnp.maximum(i * per - 1, 0), 0)
    nxt = lambda i: (jnp.minimum((i + 1) * per, t // 8 - 1), 0)
    return prev, nxt


def _mixer_out(z, cw, y_attn, g_conv, g_attn, tm, carry=None):
    t = z.shape[0]
    prev, _ = _halo_rows(tm, t)

    def body(z_ref, zp_ref, cw_ref, y_ref, gc_ref, ga_ref, mix_ref):
        i = pl.program_id(0)
        keep = jnp.where(i > 0, 1.0, 0.0)
        cu = jnp.concatenate([zp_ref[:, 0:512] * zp_ref[:, 1024:1536] * keep,
                              z_ref[:, 0:512] * z_ref[:, 1024:1536]], axis=0)
        c = (cw_ref[0:1, :] * pltpu.roll(cu, 2, 0) + cw_ref[1:2, :] * pltpu.roll(cu, 1, 0)
             + cw_ref[2:3, :] * cu)[8:, :]
        yc = z_ref[:, 512:1024] * c
        mix_ref[:, 0:512] = (yc * _rms_scale(yc) * gc_ref[...]).astype(BF16)
        ya = y_ref[...]
        mix_ref[:, 512:1024] = (ya * _rms_scale(ya) * ga_ref[...]).astype(BF16)

    blk = pl.BlockSpec((tm, 512), lambda i: (i, 0))
    vec = pl.BlockSpec((1, 512), lambda i: (0, 0))
    return _call(
        body, [z, z, cw, y_attn, g_conv, g_attn], name="mixer_out", grid=(t // tm,),
        out_shape=_sds((t, 1024), BF16),
        in_specs=[pl.BlockSpec((tm, 1536), lambda i: (i, 0)), pl.BlockSpec((8, 1536), prev),
                  pl.BlockSpec((8, 512), lambda i: (0, 0)), blk, vec, vec],
        out_specs=pl.BlockSpec((tm, 1024), lambda i: (i, 0)),
        semantics=("parallel",), carry=carry)


def _mixer_bwd(z, dx1, wout, y_attn, cw, g_conv, g_attn, ones_bd, tm, carry=None):
    t = z.shape[0]
    nblk = t // tm
    prev, nxt = _halo_rows(tm, t)
    e = tm + 16

    def body(z_ref, zp_ref, zn_ref, dx_ref, dxn_ref, w_ref, y_ref, cw_ref, gc_ref, ga_ref, bd_ref,
             dz_ref, do_ref, dd_ref, dcw_ref, dgc_ref, dga_ref):
        i = pl.program_id(0)
        dm = _dot_nt(dx_ref[...], w_ref[...])
        dmn = _dot_nt(dxn_ref[...], w_ref[0:D_CONV, :])[0:8, :]
        rows = lax.broadcasted_iota(jnp.int32, (e, 1), 0)
        lo = jnp.where(i > 0, 0, 8)
        hi = jnp.where(i < nblk - 1, e, tm + 8)
        ze = jnp.concatenate([zp_ref[...], z_ref[...], zn_ref[...]], axis=0)
        u, gb, gcv = ze[:, 0:512], ze[:, 512:1024], ze[:, 1024:1536]
        w0, w1, w2 = cw_ref[0:1, :], cw_ref[1:2, :], cw_ref[2:3, :]
        cu = jnp.where(rows >= lo, gcv * u, 0.0)
        cu1, cu2 = pltpu.roll(cu, 1, 0), pltpu.roll(cu, 2, 0)
        c = w0 * cu2 + w1 * cu1 + w2 * cu
        yc = gb * c
        dma = jnp.concatenate([jnp.zeros((8, 512), F32), dm[:, 0:512], dmn], axis=0)
        dyc, ych = _rms_bwd(yc, _rms_scale(yc), gc_ref[...], dma)
        dc = jnp.where(jnp.logical_and(rows >= 8, rows < hi), dyc * gb, 0.0)
        dcu = w0 * pltpu.roll(dc, e - 2, 0) + w1 * pltpu.roll(dc, e - 1, 0) + w2 * dc
        mid = slice(8, 8 + tm)
        dz_ref[:, 0:512] = (dcu * gcv)[mid, :].astype(BF16)
        dz_ref[:, 512:1024] = (dyc * c)[mid, :].astype(BF16)
        dz_ref[:, 1024:1536] = (dcu * u)[mid, :].astype(BF16)

        ya = y_ref[...]
        dmb = dm[:, 512:1024]
        dya, yah = _rms_bwd(ya, _rms_scale(ya), ga_ref[...], dmb)
        do_ref[...] = dya
        dd_ref[...] = _head_sum(dya * ya, bd_ref[...])

        @pl.when(i == 0)
        def _():
            dcw_ref[...] = jnp.zeros_like(dcw_ref)
            dgc_ref[...] = jnp.zeros_like(dgc_ref)
            dga_ref[...] = jnp.zeros_like(dga_ref)

        dcm = jnp.where(rows < tm + 8, dc, 0.0)
        dcw_ref[0:1, :] += jnp.sum(dcm * cu2, axis=0, keepdims=True)
        dcw_ref[1:2, :] += jnp.sum(dcm * cu1, axis=0, keepdims=True)
        dcw_ref[2:3, :] += jnp.sum(dcm * cu, axis=0, keepdims=True)
        dgc_ref[...] += jnp.sum((dma * ych)[mid, :], axis=0, keepdims=True)
        dga_ref[...] += jnp.sum(dmb * yah, axis=0, keepdims=True)

    blk = pl.BlockSpec((tm, 512), lambda i: (i, 0))
    vec = pl.BlockSpec((1, 512), lambda i: (0, 0))
    cwb = pl.BlockSpec((8, 512), lambda i: (0, 0))
    next16 = lambda i: (jnp.minimum((i + 1) * (tm // 16), t // 16 - 1), 0)
    return _call(
        body, [z, z, z, dx1, dx1, wout, y_attn, cw, g_conv, g_attn, ones_bd], name="mixer_bwd",
        grid=(nblk,),
        out_shape=[_sds((t, D_IN), BF16), _sds((t, 512), F32), _sds((t, 512), F32),
                   _sds((8, 512), F32), _sds((1, 512), F32), _sds((1, 512), F32)],
        in_specs=[pl.BlockSpec((tm, 1536), lambda i: (i, 0)), pl.BlockSpec((8, 1536), prev),
                  pl.BlockSpec((8, 1536), nxt), pl.BlockSpec((tm, D_MODEL), lambda i: (i, 0)),
                  pl.BlockSpec((16, D_MODEL), next16),
                  pl.BlockSpec(wout.shape, lambda i: (0, 0), pipeline_mode=_resident(True)),
                  blk, cwb, vec, vec, pl.BlockSpec((512, 512), lambda i: (0, 0))],
        out_specs=[pl.BlockSpec((tm, 1536), lambda i: (i, 0)), blk, blk, cwb, vec, vec],
        carry=carry)


def _qkv_bwd(z, dz, dqn, dkn, dv, gq, gk, ones_bd, tm, carry=None):
    t = z.shape[0]

    def body(zq_ref, zk_ref, _, dqn_ref, dkn_ref, dv_ref, gq_ref, gk_ref, bd_ref,
             dz_ref, dgq_ref, dgk_ref):
        bd = bd_ref[...]

        @pl.when(pl.program_id(0) == 0)
        def _():
            dgq_ref[...] = jnp.zeros_like(dgq_ref)
            dgk_ref[...] = jnp.zeros_like(dgk_ref)

        def back(v, dn, g, scale):
            r = _head_rms_scale(v, bd)
            vh = v * r
            dh = dn * (g * scale)
            dv = r * (dh - vh * (_head_sum(dh * vh, bd) * (1.0 / HEAD_DIM)))
            return dv, jnp.sum(dn * scale * vh, axis=0, keepdims=True)

        dq, dgq = back(zq_ref[...], dqn_ref[...], gq_ref[...], HEAD_DIM ** -0.5)
        dk, dgk = back(zk_ref[...], dkn_ref[...], gk_ref[...], 1.0)
        dgq_ref[...] += dgq
        dgk_ref[...] += dgk
        dz_ref[:, 0:512] = dq.astype(BF16)
        dz_ref[:, 512:1024] = dk.astype(BF16)
        dz_ref[:, 1024:1536] = dv_ref[...].astype(BF16)

    blk = pl.BlockSpec((tm, 512), lambda i: (i, 0))
    vec = pl.BlockSpec((1, 512), lambda i: (0, 0))
    return _call(
        body, [z, z, dz, dqn, dkn, dv, gq, gk, ones_bd], name="qkv_bwd", grid=(t // tm,),
        out_shape=[_sds((t, D_IN), BF16), _sds((1, 512), F32), _sds((1, 512), F32)],
        in_specs=[pl.BlockSpec((tm, 512), lambda i: (i, 3)), pl.BlockSpec((tm, 512), lambda i: (i, 4)),
                  ANY] + [blk] * 3 + [vec, vec, pl.BlockSpec((512, 512), lambda i: (0, 0))],
        out_specs=[pl.BlockSpec((tm, 1536), lambda i: (i, 1)), vec, vec],
        carry=carry, aliases={2: 0})


def _columns_from_chips(g):
    return g.transpose(1, 0, 2).reshape(g.shape[1], N_CHIPS * g.shape[2])


def kernel(x, g_mix, w_in, conv_w, g_q, g_k, g_conv_out, g_attn_out, w_out, g_ffn, w_gate, w_up, w_down, loss_target, m_g_mix, m_w_in, m_conv_w, m_g_q, m_g_k, m_g_conv_out, m_g_attn_out, m_w_out, m_g_ffn, m_w_gate, m_w_up, m_w_down, v_g_mix, v_w_in, v_conv_w, v_g_q, v_g_k, v_g_conv_out, v_g_attn_out, v_w_out, v_g_ffn, v_w_gate, v_w_up, v_w_down):
    t = x.shape[1]
    xs = x[0]
    target = loss_target[0]
    tm = min(512, t)
    tm_wide = min(1024, t)
    tmm = min(2048, t)

    cw_pad = jnp.pad(conv_w[0], ((0, 13), (0, 0)))
    gathered = _all_gather([w_in[0].astype(BF16), cw_pad])
    win = _columns_from_chips(gathered[0])
    cw = jnp.pad(gathered[1][:, 0:3, :].transpose(1, 0, 2).reshape(3, D_CONV), ((0, 5), (0, 0)))
    later = [w_out[0].astype(BF16), w_gate[0].T.astype(BF16), w_up[0].T.astype(BF16),
             w_down[0].astype(BF16)]

    head_id = jnp.arange(D_ATTN) // HEAD_DIM
    ones_bd = (head_id[:, None] == head_id[None, :]).astype(BF16)
    gq_t = jnp.tile(g_q, (1, D_ATTN // HEAD_DIM))
    gk_t = jnp.tile(g_k, (1, D_ATTN // HEAD_DIM))

    h1, z, *dilated = _in_proj(xs, g_mix, win, gq_t, gk_t, ones_bd, tm)
    nd = len(DILATIONS)
    qs, ks, vs = dilated[:nd], dilated[nd:2 * nd], dilated[2 * nd:]
    (y_attn, lse), gathered = _attn_fwd(qs, ks, vs, carry=_x_gather_chips(later))
    mix, gathered = _mixer_out(z, cw, y_attn, g_conv_out, g_attn_out, tm_wide,
                               carry=_x_gather_sibling(gathered))
    wout = gathered[0].reshape(D_MODEL, D_MODEL)
    wgate_t = gathered[1].reshape(D_FF, D_MODEL)
    wup_t = gathered[2].reshape(D_FF, D_MODEL)
    wdown = gathered[3].reshape(D_FF, D_MODEL)
    (x1,) = _matmul("out_proj", mix, wout, [xs], [F32], lambda acc, r: (r + acc,), tm_wide, D_MODEL)
    h2, gate, up, act = _norm_matmul("ffn_up", x1, g_ffn, [wgate_t, wup_t], tm, D_FF, True, BF16,
                                     transposed_w=True)

    def loss_epilogue(acc, r, tgt):
        err = r + acc - tgt
        dy = err * (1.0 / D_MODEL)
        return dy, dy, jnp.sum(err * err)

    dx2, dx2b, loss_sum = _matmul("ffn_down_loss", act, wdown, [x1, target], [F32, BF16],
                                  loss_epilogue, tm_wide, D_MODEL, loss=True)

    def swiglu_bwd(da, gt, u):
        gt, u = gt.astype(F32), u.astype(F32)
        s = _sigmoid(gt)
        return da * u * (s * (1.0 + gt * (1.0 - s))), da * (gt * s)

    dgate, dup = _matmul("ffn_down_bwd", dx2b, wdown, [gate, up], [BF16, BF16], swiglu_bwd,
                         tm, D_FF, transposed_w=True)
    gw_down = _matmul_tn("grad_w_down", act, dx2b, 512, tmm)
    gw_gate_t = _matmul_tn("grad_w_gate", dgate, h2, 512, tmm)
    gw_up_t = _matmul_tn("grad_w_up", dup, h2, 512, tmm)

    me = 2 * lax.axis_index("x") + lax.axis_index("y")
    where = jnp.stack([lax.axis_index("c"), me]).astype(jnp.int32)

    def pair_sums(names, full, got):
        return [_pair_sum(f"pair_sum_{nme}", a, b, where) for nme, a, b in zip(names, full, got)]

    def chip_sums(names, pair, got):
        return [_chip_sum(f"chip_sum_{nme}", own, b) for nme, (_, own), b in zip(names, pair, got)]

    ffn = ["w_gate", "w_up", "w_down"]
    full = [g.reshape(N_CHIPS, D_FF // N_CHIPS, D_MODEL) for g in (gw_gate_t, gw_up_t, gw_down)]
    (dx1, dx1b, gg_ffn), got = _matmul_norm_bwd(
        "ffn_up_bwd", [(dgate, wgate_t), (dup, wup_t)], x1, dx2, g_ffn, tm, carry=_x_pair(full),
        transposed_w=False)
    pair = pair_sums(ffn, full, got)
    gw_out = _matmul_tn("grad_w_out", mix, dx1b, 512, tmm)
    full = [gw_out.reshape(N_CHIPS, D_MODEL // N_CHIPS, D_MODEL)]
    (dzc, do, dd, gcw, gg_conv, gg_attn), got = _mixer_bwd(
        z, dx1b, wout, y_attn, cw, g_conv_out, g_attn_out, ones_bd, tm, carry=_x_pair(full))
    pair += pair_sums(["w_out"], full, got)
    early = ffn + ["w_out"]
    (dqn, dkn, dv), got = _attn_bwd(qs, ks, vs, do, lse, dd, carry=_x_chips([p for p, _ in pair]))
    mine = chip_sums(early, pair, got)
    (dz, gg_q, gg_k), theirs = _qkv_bwd(z, dzc, dqn, dkn, dv, gq_t, gk_t, ones_bd, tm_wide,
                                        carry=_x_share(mine))
    full = [_matmul_tn("grad_w_in", h1, dz, D_IN // N_CHIPS, tmm, by_chip=True)]
    got = _exchange_alone("grad_pair_exchange_w_in", _x_pair(full))
    pair = pair_sums(["w_in"], full, got)
    (grad_x, _, gg_mix), got = _matmul_norm_bwd("in_proj_bwd", [(dz, win)], xs, dx1, g_mix, tm_wide,
                                                carry=_x_chips([pair[0][0]]))
    mine += chip_sums(["w_in"], pair, got)
    theirs = list(theirs) + list(_exchange_alone("grad_pair_share_w_in", _x_share(mine[-1:])))
    big = early + ["w_in"]

    small = _small_all_reduce({
        "g_mix": gg_mix, "g_ffn": gg_ffn, "g_conv_out": gg_conv, "g_attn_out": gg_attn,
        "g_q": gg_q, "g_k": gg_k, "loss": loss_sum, "conv_w": gcw})
    heads = D_ATTN // HEAD_DIM
    grads = {
        "g_mix": small[0:1, :], "g_ffn": small[1:2, :],
        "g_conv_out": small[2:3, 0:512], "g_attn_out": small[2:3, 512:1024],
        "g_q": small[3, 0:512].reshape(heads, HEAD_DIM).sum(axis=0)[None, :],
        "g_k": small[3, 512:1024].reshape(heads, HEAD_DIM).sum(axis=0)[None, :],
        "conv_w": lax.dynamic_slice(small[8:11, 0:512], (0, me * (D_CONV // N_CHIPS)),
                                    (3, D_CONV // N_CHIPS)),
    }
    halves = dict(zip(big, zip(mine, theirs)))
    loss = small[4, 0] * 0.5 * (1.0 / D_MODEL)

    weights = dict(g_mix=g_mix, w_in=w_in, conv_w=conv_w, g_q=g_q, g_k=g_k, g_conv_out=g_conv_out,
                   g_attn_out=g_attn_out, w_out=w_out, g_ffn=g_ffn, w_gate=w_gate, w_up=w_up,
                   w_down=w_down)
    moments_m = dict(g_mix=m_g_mix, w_in=m_w_in, conv_w=m_conv_w, g_q=m_g_q, g_k=m_g_k,
                     g_conv_out=m_g_conv_out, g_attn_out=m_g_attn_out, w_out=m_w_out, g_ffn=m_g_ffn,
                     w_gate=m_w_gate, w_up=m_w_up, w_down=m_w_down)
    moments_v = dict(g_mix=v_g_mix, w_in=v_w_in, conv_w=v_conv_w, g_q=v_g_q, g_k=v_g_k,
                     g_conv_out=v_g_conv_out, g_attn_out=v_g_attn_out, w_out=v_w_out, g_ffn=v_g_ffn,
                     w_gate=v_w_gate, w_up=v_w_up, w_down=v_w_down)
    names = list(weights)
    out_g, out_d, out_m, out_v = [], [], [], []
    for nme in names:
        wgt = weights[nme]
        shape2 = wgt.shape[-2:] if wgt.ndim == 3 else wgt.shape
        flip = nme in ("w_gate", "w_up")

        def to2d(a):
            return a.reshape(shape2).T if flip else a.reshape(shape2)

        def back(a):
            return (a.T if flip else a).reshape(wgt.shape)

        state = (to2d(wgt), to2d(moments_m[nme]), to2d(moments_v[nme]))
        if nme in halves:
            g2, dlt, nm, nv = _adamw_shard(f"adamw_{nme}", *state, *halves[nme], where)
        else:
            g2 = grads[nme].reshape(shape2)
            dlt, nm, nv = _adamw(f"adamw_{nme}", state[0], g2, state[1], state[2])
        out_g.append(back(g2))
        out_d.append(back(dlt))
        out_m.append(back(nm))
        out_v.append(back(nv))
    return (loss, grad_x[None], *out_g, *out_d, *out_m, *out_v)
```

```python
import functools
from typing import Any, Callable, NamedTuple, Sequence

import jax
import jax.numpy as jnp
from jax import lax
from jax.experimental import pallas as pl
from jax.experimental.pallas import tpu as pltpu

F32 = jnp.float32
BF16 = jnp.bfloat16
MESH = pl.DeviceIdType.MESH

D_MODEL = 1024
D_CONV = 512
D_ATTN = 512
HEAD_DIM = 64
D_FF = 2816
D_IN = 3 * D_CONV + 3 * D_ATTN
DILATIONS = (1, 4, 16)
BAND = 128
EPS = 1e-6
NEG = -1e30
N_CHIPS = 4

ADAM_LR = 0.001
ADAM_B1 = 0.9
ADAM_B2 = 0.999
ADAM_EPS = 1e-08
ADAM_WD = 0.01
ADAM_STEP = 10

V7X_VMEM_BYTES = 64 * 1024 * 1024
VMEM_LIMIT = V7X_VMEM_BYTES - 8 * 1024 * 1024
ANY = pl.BlockSpec(memory_space=pl.ANY)
VMEM_WHOLE = pl.BlockSpec(memory_space=pltpu.VMEM)


def _params(*sem):
    return pltpu.CompilerParams(dimension_semantics=sem, vmem_limit_bytes=VMEM_LIMIT)


def _sds(shape, dtype):
    return jax.ShapeDtypeStruct(shape, dtype)


def _resident(whole):
    return pl.Buffered(1) if whole else None


def _place():
    x, y, c = lax.axis_index("x"), lax.axis_index("y"), lax.axis_index("c")
    chips = [(1 - x, y), (x, 1 - y), (1 - x, 1 - y)]
    return x, y, c, 2 * x + y, chips, [2 * cx + cy for cx, cy in chips]


def _all_gather(shards):
    n = len(shards)

    def body(*refs):
        ins, outs, stage = refs[:n], refs[n:2 * n], refs[2 * n:3 * n]
        ssem, rsem, fsem, gsem, lsem, osem = refs[3 * n:]
        x, y, c, me, chips, cids = _place()
        sib = (x, y, 1 - c)

        def half(w, which):
            h = shards[w].shape[0] // 2
            return pl.ds(pl.multiple_of(which * h, 8), h)

        loads = [pltpu.make_async_copy(ins[w], stage[w], lsem.at[w]) for w in range(n)]
        local = [pltpu.make_async_copy(stage[w], outs[w].at[me], osem.at[w]) for w in range(n)]
        for cp in loads:
            cp.start()

        def chip_copy(w, j, src_slot):
            rows = half(w, c)
            return pltpu.make_async_remote_copy(
                src_ref=ins[w].at[rows], dst_ref=outs[w].at[src_slot, rows],
                send_sem=ssem.at[3 * w + j], recv_sem=rsem.at[3 * w + j],
                device_id=(*chips[j], c), device_id_type=MESH)

        def sib_copy(w, j, which):
            rows = half(w, which)
            return pltpu.make_async_remote_copy(
                src_ref=outs[w].at[cids[j], rows], dst_ref=outs[w].at[cids[j], rows],
                send_sem=fsem.at[3 * w + j], recv_sem=gsem.at[3 * w + j],
                device_id=sib, device_id_type=MESH)

        sends = [chip_copy(w, j, me) for w in range(n) for j in range(3)]
        for cp in sends:
            cp.start()
        for w in range(n):
            loads[w].wait()
            local[w].start()
        passed = []
        for w in range(n):
            for j in range(3):
                chip_copy(w, j, cids[j]).wait_recv()
                cp = sib_copy(w, j, c)
                cp.start()
                passed.append(cp)
        for w in range(n):
            for j in range(3):
                sib_copy(w, j, 1 - c).wait_recv()
        for cp in sends + passed:
            cp.wait_send()
        for cp in local:
            cp.wait()

    return pl.pallas_call(
        body, name="all_gather_weights",
        out_shape=[_sds((N_CHIPS,) + s.shape, s.dtype) for s in shards],
        in_specs=[ANY] * n, out_specs=[ANY] * n,
        scratch_shapes=[pltpu.VMEM(s.shape, s.dtype) for s in shards]
        + [pltpu.SemaphoreType.DMA((3 * n,))] * 4 + [pltpu.SemaphoreType.DMA((n,))] * 2,
        compiler_params=pltpu.CompilerParams(vmem_limit_bytes=VMEM_LIMIT),
    )(*shards)


class _Exchange(NamedTuple):
    srcs: Sequence[Any]
    lands: Sequence[Any]
    outs: Sequence[Any]
    n_sems: int
    copies: Callable


def _remote(src, dst, ssem, rsem, k, to):
    return pltpu.make_async_remote_copy(src_ref=src, dst_ref=dst, send_sem=ssem.at[k],
                                        recv_sem=rsem.at[k], device_id=to, device_id_type=MESH)


def _x_gather_chips(shards):
    def copies(srcs, lands, outs, ssem, rsem):
        _, _, c, me, chips, cids = _place()
        go, arrive = [], []
        for w, s in enumerate(shards):
            h = s.shape[0] // 2
            rows = pl.ds(pl.multiple_of(c * h, 8), h)
            for j in range(3):
                to = (*chips[j], c)
                go.append(_remote(srcs[w].at[rows], lands[w].at[me, rows], ssem, rsem, 3 * w + j, to))
                arrive.append(_remote(srcs[w].at[rows], lands[w].at[cids[j], rows], ssem, rsem,
                                      3 * w + j, to))
        return go, arrive

    lands = [jnp.broadcast_to(s[None], (N_CHIPS,) + s.shape) for s in shards]
    return _Exchange(shards, lands, [], 3 * len(shards), copies)


def _x_gather_sibling(gathered):
    def copies(srcs, lands, outs, ssem, rsem):
        x, y, c, _, _, cids = _place()
        go, arrive = [], []
        for w, g in enumerate(gathered):
            h = g.shape[1] // 2
            mine = pl.ds(pl.multiple_of(c * h, 8), h)
            theirs = pl.ds(pl.multiple_of((1 - c) * h, 8), h)
            for j in range(3):
                slab = lands[w].at[cids[j]]
                go.append(_remote(slab.at[mine], slab.at[mine], ssem, rsem, 3 * w + j, (x, y, 1 - c)))
                arrive.append(_remote(slab.at[theirs], slab.at[theirs], ssem, rsem, 3 * w + j,
                                      (x, y, 1 - c)))
        return go, arrive

    return _Exchange([], gathered, [], 3 * len(gathered), copies)


def _x_pair(grads):
    def copies(srcs, lands, outs, ssem, rsem):
        x, y, c, _, _, _ = _place()
        go = []
        for w, g in enumerate(grads):
            h = g.shape[1] // 2
            theirs = pl.ds(pl.multiple_of((1 - c) * h, 8), h)
            go.append(_remote(srcs[w].at[:, theirs, :], outs[w], ssem, rsem, w, (x, y, 1 - c)))
        return go, go

    outs = [_sds((N_CHIPS, g.shape[1] // 2, g.shape[2]), g.dtype) for g in grads]
    return _Exchange(grads, [], outs, len(grads), copies)


def _x_chips(parts):
    def copies(srcs, lands, outs, ssem, rsem):
        _, _, c, _, chips, cids = _place()
        go = [_remote(srcs[w].at[cids[j]], outs[w].at[j], ssem, rsem, 3 * w + j, (*chips[j], c))
              for w in range(len(parts)) for j in range(3)]
        return go, go

    outs = [_sds((3,) + p.shape[1:], p.dtype) for p in parts]
    return _Exchange(parts, [], outs, 3 * len(parts), copies)


def _x_share(halves):
    def copies(srcs, lands, outs, ssem, rsem):
        x, y, c, _, _, _ = _place()
        go = [_remote(srcs[w], outs[w], ssem, rsem, w, (x, y, 1 - c)) for w in range(len(halves))]
        return go, go

    return _Exchange(halves, [], [_sds(h.shape, h.dtype) for h in halves], len(halves), copies)


def _call(body, args, *, name, grid, in_specs, out_specs, out_shape, scratch_shapes=(),
          semantics=None, carry=None, aliases=None):
    single = not isinstance(out_shape, (list, tuple))
    out_shape = [out_shape] if single else list(out_shape)
    out_specs = [out_specs] if single else list(out_specs)
    aliases = dict(aliases or {})
    if carry is None:
        res = pl.pallas_call(
            body, name=name, grid=grid, in_specs=list(in_specs), out_specs=out_specs,
            out_shape=out_shape, scratch_shapes=list(scratch_shapes), input_output_aliases=aliases,
            compiler_params=_params(*(semantics or ("arbitrary",) * len(grid))))(*args)
        return res[0] if single else res
    n_in, n_out, n_scr = len(args), len(out_shape), len(scratch_shapes)
    n_src, n_land, n_new = len(carry.srcs), len(carry.lands), len(carry.outs)

    def carrying(*refs):
        at = 0
        parts = []
        for n in (n_in, n_src, n_land, n_out, n_land, n_new, n_scr, 2):
            parts.append(refs[at:at + n])
            at += n
        ins, srcs, _, outs, lands, news, scratch, (ssem, rsem) = parts
        ids = [pl.program_id(a) for a in range(len(grid))]
        first = functools.reduce(jnp.logical_and, [i == 0 for i in ids])
        last = functools.reduce(jnp.logical_and, [i == g - 1 for i, g in zip(ids, grid)])
        go, arrive = carry.copies(srcs, lands, news, ssem, rsem)

        @pl.when(first)
        def _():
            for cp in go:
                cp.start()

        body(*ins, *outs, *scratch)

        @pl.when(last)
        def _():
            for cp in go:
                cp.wait_send()
            for cp in arrive:
                cp.wait_recv()

    res = pl.pallas_call(
        carrying, name=name, grid=grid,
        in_specs=list(in_specs) + [ANY] * (n_src + n_land),
        out_specs=out_specs + [ANY] * (n_land + n_new),
        out_shape=out_shape + [_sds(a.shape, a.dtype) for a in carry.lands] + list(carry.outs),
        input_output_aliases={**aliases, **{n_in + n_src + i: n_out + i for i in range(n_land)}},
        scratch_shapes=list(scratch_shapes) + [pltpu.SemaphoreType.DMA((carry.n_sems,))] * 2,
        compiler_params=_params(*(("arbitrary",) * len(grid))))(*args, *carry.srcs, *carry.lands)
    own = res[:n_out]
    return (own[0] if single else own), res[n_out:]


def _exchange_alone(name, exchange):
    def body(x_ref, o_ref):
        o_ref[...] = x_ref[...]

    blk = pl.BlockSpec((8, 128), lambda i: (0, 0))
    _, res = _call(body, [jnp.zeros((8, 128), F32)], name=name, grid=(1,), in_specs=[blk],
                   out_specs=blk, out_shape=_sds((8, 128), F32), carry=exchange)
    return res


def _row_block(r, want):
    return max(d for d in range(1, min(want, r) + 1) if r % d == 0 and (d % 8 == 0 or d == r))


def _pair_sum(name, full, got, where):
    _, r, n = full.shape
    h = r // 2
    tr = _row_block(h, 512)
    nb = h // tr

    def body(w_ref, a_ref, b_ref, o_ref, own_ref):
        total = a_ref[...] + b_ref[...]
        o_ref[...] = total.astype(BF16)

        @pl.when(pl.program_id(1) == w_ref[1])
        def _():
            own_ref[...] = total[0]

    blk = pl.BlockSpec((1, tr, n), lambda i, s, w: (s, i, 0))
    return pl.pallas_call(
        body, name=name, out_shape=[_sds(got.shape, BF16), _sds((h, n), F32)],
        grid_spec=pltpu.PrefetchScalarGridSpec(
            num_scalar_prefetch=1, grid=(nb, N_CHIPS),
            in_specs=[pl.BlockSpec((1, tr, n), lambda i, s, w: (s, w[0] * nb + i, 0)), blk],
            out_specs=[blk, pl.BlockSpec((tr, n), lambda i, s, w: (i, 0))]),
        compiler_params=_params("parallel", "arbitrary"),
    )(where, full, got)


def _chip_sum(name, own, got):
    h, n = own.shape
    tr = _row_block(h, 256)

    def body(a_ref, b0, b1, b2, o_ref):
        o_ref[...] = ((a_ref[...] + b0[0].astype(F32)) + b1[0].astype(F32)) + b2[0].astype(F32)

    def slot(j):
        return pl.BlockSpec((1, tr, n), lambda i: (j, i, 0))

    blk = pl.BlockSpec((tr, n), lambda i: (i, 0))
    return pl.pallas_call(
        body, name=name, grid=(h // tr,), out_shape=_sds((h, n), F32),
        in_specs=[blk, slot(0), slot(1), slot(2)], out_specs=blk,
        compiler_params=_params("parallel"),
    )(own, got, got, got)


SMALL_ROWS = 16
SMALL_LAYOUT = (
    ("g_mix", 0, 0, 1, 1024), ("g_ffn", 1, 0, 1, 1024), ("g_conv_out", 2, 0, 1, 512),
    ("g_attn_out", 2, 512, 1, 512), ("g_q", 3, 0, 1, 512), ("g_k", 3, 512, 1, 512),
    ("loss", 4, 0, 1, 128), ("conv_w", 8, 0, 8, 512))


def _small_all_reduce(parts):
    names = [s[0] for s in SMALL_LAYOUT]

    def body(*refs):
        ins = refs[:len(names)]
        out_ref, stage, buf, ssem, rsem = refs[len(names):]
        x, y, c, _, _, _ = _place()
        me = 4 * x + 2 * y + c
        stage[...] = jnp.zeros_like(stage)
        for ref, (_, r0, c0, nr, nc) in zip(ins, SMALL_LAYOUT):
            stage[r0:r0 + nr, c0:c0 + nc] = ref[0:nr, :]
        buf[me] = stage[...]
        peers = []
        for d in range(1, 8):
            px = 1 - x if d & 4 else x
            py = 1 - y if d & 2 else y
            pc = 1 - c if d & 1 else c
            peers.append(((px, py, pc), 4 * px + 2 * py + pc))
        sends = [pltpu.make_async_remote_copy(
            src_ref=stage, dst_ref=buf.at[me], send_sem=ssem.at[k], recv_sem=rsem.at[k],
            device_id=peer, device_id_type=MESH) for k, (peer, _) in enumerate(peers)]
        for cp in sends:
            cp.start()
        for k, (peer, pid) in enumerate(peers):
            pltpu.make_async_remote_copy(
                src_ref=stage, dst_ref=buf.at[pid], send_sem=ssem.at[k], recv_sem=rsem.at[k],
                device_id=peer, device_id_type=MESH).wait_recv()
        for cp in sends:
            cp.wait_send()
        acc = buf[0]
        for k in range(1, 8):
            acc = acc + buf[k]
        out_ref[...] = acc

    return pl.pallas_call(
        body, name="small_all_reduce", out_shape=_sds((SMALL_ROWS, 1024), F32),
        in_specs=[VMEM_WHOLE] * len(names), out_specs=VMEM_WHOLE,
        scratch_shapes=[pltpu.VMEM((SMALL_ROWS, 1024), F32), pltpu.VMEM((8, SMALL_ROWS, 1024), F32),
                        pltpu.SemaphoreType.DMA((7,)), pltpu.SemaphoreType.DMA((7,))],
    )(*[parts[k] for k in names])


def _dot(a, b):
    return jnp.dot(a, b, preferred_element_type=F32)


def _dot_nt(a, b):
    return lax.dot_general(a, b, (((1,), (1,)), ((), ())), preferred_element_type=F32)


def _dot_tn(a, b):
    return lax.dot_general(a, b, (((0,), (0,)), ((), ())), preferred_element_type=F32)


def _sigmoid(v):
    return 1.0 / (1.0 + jnp.exp(-v))


def _rms_scale(v):
    return lax.rsqrt(jnp.mean(v * v, axis=-1, keepdims=True) + EPS)


def _rms_bwd(v, r, g, dy):
    vh = v * r
    dh = dy * g
    return r * (dh - vh * jnp.mean(dh * vh, axis=-1, keepdims=True)), vh


def _head_sum(a, ones_bd):
    hi = a.astype(BF16)
    lo = (a - hi.astype(F32)).astype(BF16)
    return _dot(hi, ones_bd) + _dot(lo, ones_bd)


def _head_rms_scale(v, ones_bd):
    return lax.rsqrt(_head_sum(v * v, ones_bd) * (1.0 / HEAD_DIM) + EPS)


MXU_COLUMNS = 256


def _column_chunks(n):
    width = MXU_COLUMNS if n % MXU_COLUMNS == 0 else n
    return [slice(c, c + width) for c in range(0, n, width)]


def _norm_matmul(name, x, g, ws, tm, tn, swiglu, out_dtype=F32, transposed_w=False):
    t, d = x.shape
    n = ws[0].shape[0] if transposed_w else ws[0].shape[1]
    nw = len(ws)

    def body(x_ref, g_ref, *refs):
        w_refs, h_ref, o_refs = refs[:nw], refs[nw], refs[nw + 1:2 * nw + 1]
        hs = refs[-1]

        @pl.when(pl.program_id(1) == 0)
        def _():
            xv = x_ref[...]
            h = (xv * _rms_scale(xv) * g_ref[...]).astype(BF16)
            hs[...] = h
            h_ref[...] = h

        h = hs[...]
        for cols in _column_chunks(tn):
            outs = [_dot_nt(h, w[cols, :]) if transposed_w else _dot(h, w[:, cols]) for w in w_refs]
            for o_ref, o in zip(o_refs, outs):
                o_ref[:, cols] = o.astype(out_dtype)
            if swiglu:
                refs[2 * nw + 1][:, cols] = (outs[0] * _sigmoid(outs[0]) * outs[1]).astype(BF16)

    row = pl.BlockSpec((tm, d), lambda i, j: (i, 0))
    col = pl.BlockSpec((tm, tn), lambda i, j: (i, j))
    out_shape = [_sds((t, d), BF16)] + [_sds((t, n), out_dtype)] * nw
    out_specs = [row] + [col] * nw
    if swiglu:
        out_shape.append(_sds((t, n), BF16))
        out_specs.append(col)
    return pl.pallas_call(
        body, name=name, grid=(t // tm, n // tn), out_shape=out_shape,
        in_specs=[row, pl.BlockSpec((1, d), lambda i, j: (0, 0))]
        + [pl.BlockSpec((tn, d), lambda i, j: (j, 0), pipeline_mode=_resident(tn == n))
           if transposed_w
           else pl.BlockSpec((d, tn), lambda i, j: (0, j), pipeline_mode=_resident(tn == n))] * nw,
        out_specs=out_specs, scratch_shapes=[pltpu.VMEM((tm, d), BF16)],
        compiler_params=_params("parallel", "arbitrary"),
    )(x, g, *ws)


def _matmul(name, a, w, extras, out_dtypes, epilogue, tm, tn, transposed_w=False, loss=False):
    t, k = a.shape
    n = w.shape[0] if transposed_w else w.shape[1]
    ne, no = len(extras), len(out_dtypes)

    def body(a_ref, w_ref, *refs):
        e_refs, o_refs = refs[:ne], refs[ne:]
        a = a_ref[...]
        total = 0.0
        for cols in _column_chunks(tn):
            acc = _dot_nt(a, w_ref[cols, :]) if transposed_w else _dot(a, w_ref[:, cols])
            res = epilogue(acc, *[e[:, cols] for e in e_refs])
            for o_ref, r in zip(o_refs[:no], res[:no]):
                o_ref[:, cols] = r.astype(o_ref.dtype)
            if loss:
                total = total + res[no]
        if loss:
            first = jnp.logical_and(pl.program_id(0) == 0, pl.program_id(1) == 0)

            @pl.when(first)
            def _():
                o_refs[no][...] = jnp.zeros_like(o_refs[no])

            o_refs[no][...] += total

    col = pl.BlockSpec((tm, tn), lambda i, j: (i, j))
    w_spec = (pl.BlockSpec((tn, k), lambda i, j: (j, 0), pipeline_mode=_resident(tn == n))
              if transposed_w
              else pl.BlockSpec((k, tn), lambda i, j: (0, j), pipeline_mode=_resident(tn == n)))
    out_shape = [_sds((t, n), dt) for dt in out_dtypes]
    out_specs = [col] * no
    if loss:
        out_shape.append(_sds((8, 128), F32))
        out_specs.append(pl.BlockSpec((8, 128), lambda i, j: (0, 0)))
    return pl.pallas_call(
        body, name=name, grid=(t // tm, n // tn), out_shape=out_shape,
        in_specs=[pl.BlockSpec((tm, k), lambda i, j: (i, 0)), w_spec] + [col] * ne,
        out_specs=out_specs,
        compiler_params=_params(*(("arbitrary", "arbitrary") if loss else ("parallel", "parallel"))),
    )(a, w, *extras)


def _matmul_norm_bwd(name, pairs, x, dres, g, tm, carry=None, transposed_w=True):
    t, d = x.shape
    npairs = len(pairs)
    product = _dot_nt if transposed_w else _dot

    def body(*refs):
        a_refs, w_refs = refs[:npairs], refs[npairs:2 * npairs]
        x_ref, r_ref, g_ref, dx_ref, dxb_ref, dg_ref = refs[2 * npairs:]
        dy = product(a_refs[0][...], w_refs[0][...])
        for a_ref, w_ref in zip(a_refs[1:], w_refs[1:]):
            dy = dy + product(a_ref[...], w_ref[...])
        xv = x_ref[...]
        dx, xh = _rms_bwd(xv, _rms_scale(xv), g_ref[...], dy)
        dx = dx + r_ref[...]
        dx_ref[...] = dx
        dxb_ref[...] = dx.astype(BF16)

        @pl.when(pl.program_id(0) == 0)
        def _():
            dg_ref[...] = jnp.zeros_like(dg_ref)

        dg_ref[...] += jnp.sum(dy * xh, axis=0, keepdims=True)

    row = pl.BlockSpec((tm, d), lambda i: (i, 0))
    vec = pl.BlockSpec((1, d), lambda i: (0, 0))
    return _call(
        body, [a for a, _ in pairs] + [w for _, w in pairs] + [x, dres, g], name=name,
        grid=(t // tm,), out_shape=[_sds((t, d), F32), _sds((t, d), BF16), _sds((1, d), F32)],
        in_specs=[pl.BlockSpec((tm, a.shape[1]), lambda i: (i, 0)) for a, _ in pairs]
        + [pl.BlockSpec(w.shape, lambda i: (0, 0), pipeline_mode=pl.Buffered(1)) for _, w in pairs]
        + [row, row, vec],
        out_specs=[row, row, vec], carry=carry)


def _matmul_tn(name, a, g, tn, tk, by_chip=False):
    t, ka = a.shape
    n = g.shape[1]

    def body(a_ref, g_ref, o_ref):
        @pl.when(pl.program_id(1) == 0)
        def _():
            o_ref[...] = jnp.zeros_like(o_ref)

        acc = _dot_tn(a_ref[...], g_ref[...])
        o_ref[...] += acc[None] if by_chip else acc

    return pl.pallas_call(
        body, name=name, grid=(n // tn, t // tk),
        out_shape=_sds((n // tn, ka, tn) if by_chip else (ka, n), F32),
        in_specs=[pl.BlockSpec((tk, ka), lambda j, s: (s, 0)),
                  pl.BlockSpec((tk, tn), lambda j, s: (s, j))],
        out_specs=(pl.BlockSpec((1, ka, tn), lambda j, s: (j, 0, 0)) if by_chip
                   else pl.BlockSpec((ka, tn), lambda j, s: (0, j))),
        compiler_params=_params("parallel", "arbitrary"),
    )(a, g)


def _elementwise(name, fn, ins, out_dtypes, tr):
    r, n = ins[0].shape
    tr = _row_block(r, tr)
    ni = len(ins)

    def body(*refs):
        res = fn(*[ref[...] for ref in refs[:ni]])
        for o_ref, v in zip(refs[ni:], res):
            o_ref[...] = v.astype(o_ref.dtype)

    blk = pl.BlockSpec((tr, n), lambda i: (i, 0))
    return pl.pallas_call(
        body, name=name, grid=(r // tr,), out_shape=[_sds((r, n), dt) for dt in out_dtypes],
        in_specs=[blk] * ni, out_specs=[blk] * len(out_dtypes),
        compiler_params=_params("parallel"),
    )(*ins)


def _adamw_update(w, g, m, v):
    m = ADAM_B1 * m + (1.0 - ADAM_B1) * g
    v = ADAM_B2 * v + (1.0 - ADAM_B2) * (g * g)
    m_hat = m / (1.0 - ADAM_B1 ** ADAM_STEP)
    v_hat = v / (1.0 - ADAM_B2 ** ADAM_STEP)
    return -ADAM_LR * (m_hat / (jnp.sqrt(v_hat) + ADAM_EPS) + ADAM_WD * w), m, v


def _adamw(name, w, g, m, v):
    return _elementwise(name, _adamw_update, [w, g, m, v], [F32] * 3, 256)


def _adamw_shard(name, w, m, v, mine, theirs, where):
    r, n = w.shape
    h = r // 2

    def body(w_ref, p_ref, m_ref, v_ref, a_ref, b_ref, g_ref, d_ref, nm_ref, nv_ref):
        first = w_ref[0] == 0
        a, b = a_ref[...], b_ref[...]
        for s, g in enumerate([jnp.where(first, a, b), jnp.where(first, b, a)]):
            rows = slice(s * h, (s + 1) * h)
            g_ref[rows, :] = g
            d_ref[rows, :], nm_ref[rows, :], nv_ref[rows, :] = _adamw_update(
                p_ref[rows, :], g, m_ref[rows, :], v_ref[rows, :])

    whole = pl.BlockSpec((r, n), lambda i, c: (0, 0))
    half = pl.BlockSpec((h, n), lambda i, c: (0, 0))
    return pl.pallas_call(
        body, name=name, out_shape=[_sds((r, n), F32)] * 4,
        grid_spec=pltpu.PrefetchScalarGridSpec(
            num_scalar_prefetch=1, grid=(1,), in_specs=[whole] * 3 + [half, half],
            out_specs=[whole] * 4),
        compiler_params=_params("arbitrary"),
    )(where, w, m, v, mine, theirs)


PAIRS = D_ATTN // BAND


def _in_proj(x, g, w, gq, gk, ones_bd, tm):
    t, dm = x.shape
    n = w.shape[1]
    nd = len(DILATIONS)
    first = 3 * D_CONV

    def body(x_ref, g_ref, w_ref, gq_ref, gk_ref, bd_ref, h_ref, z_ref, *refs):
        outs, slabs = refs[:3 * nd], refs[3 * nd:]
        xv = x_ref[...]
        h = (xv * _rms_scale(xv) * g_ref[...]).astype(BF16)
        h_ref[...] = h
        for cols in _column_chunks(n):
            z_ref[:, cols] = _dot(h, w_ref[:, cols])
        bd = bd_ref[...]
        q = z_ref[:, first:first + D_ATTN]
        k = z_ref[:, first + D_ATTN:first + 2 * D_ATTN]
        vals = [(q * _head_rms_scale(q, bd) * gq_ref[...]) * HEAD_DIM ** -0.5,
                k * _head_rms_scale(k, bd) * gk_ref[...], z_ref[:, first + 2 * D_ATTN:n]]
        for m, val in enumerate(vals):
            for c in range(PAIRS):
                slabs[0][c] = val[:, c * BAND:(c + 1) * BAND]
            cur, before = 0, 1
            for a, d in enumerate(DILATIONS):
                o_ref, src, dst = outs[m * nd + a], slabs[cur], slabs[1 - cur]
                step, count = d // before, tm // d
                keep = step > 1 and a + 1 < nd
                for c in range(PAIRS):
                    for r in range(d):
                        start = (r % before) * (tm // before) + r // before
                        rows = src.at[c][pl.ds(start, count, stride=step), :] if step > 1 else src[c]
                        o_ref[c, r] = rows.astype(BF16)
                        if keep:
                            dst.at[c][pl.ds(r * count, count), :] = rows
                if keep:
                    cur = 1 - cur
                before = d

    row = pl.BlockSpec((tm, dm), lambda i: (i, 0))
    vec = pl.BlockSpec((1, D_ATTN), lambda i: (0, 0))
    return pl.pallas_call(
        body, name="in_proj", grid=(t // tm,),
        out_shape=[_sds((t, dm), BF16), _sds((t, n), F32)]
        + [_sds((PAIRS, d, t // d, BAND), BF16) for _ in range(3) for d in DILATIONS],
        in_specs=[row, pl.BlockSpec((1, dm), lambda i: (0, 0)),
                  pl.BlockSpec((dm, n), lambda i: (0, 0), pipeline_mode=_resident(True)), vec, vec,
                  pl.BlockSpec((D_ATTN, D_ATTN), lambda i: (0, 0), pipeline_mode=_resident(True))],
        out_specs=[row, pl.BlockSpec((tm, n), lambda i: (i, 0))]
        + [pl.BlockSpec((PAIRS, d, tm // d, BAND), lambda i: (0, 0, i, 0))
           for _ in range(3) for d in DILATIONS],
        scratch_shapes=[pltpu.VMEM((PAIRS, tm, BAND), F32)] * 2,
        compiler_params=_params("parallel"),
    )(x, g, w, gq, gk, ones_bd)


TOK = 2048
UNITS = TOK // BAND


def _stack_masks():
    row = lax.broadcasted_iota(jnp.int32, (2 * BAND, 2 * BAND), 0) & (BAND - 1)
    col = lax.broadcasted_iota(jnp.int32, (2 * BAND, 2 * BAND), 1)
    lane = lax.broadcasted_iota(jnp.int32, (BAND, BAND), 1)
    head0 = lane < HEAD_DIM
    ones = [jnp.where(head0, 1.0, 0.0).astype(BF16), jnp.where(head0, 0.0, 1.0).astype(BF16)]
    return col - row, col, head0, ones


def _split3(x):
    hi = x.astype(BF16).astype(F32)
    mid = (x - hi).astype(BF16).astype(F32)
    return hi, mid, x - hi - mid


def _gather(srcs, dst, d, before=1):
    per, step, span = TOK // d, d // before, TOK // before
    at = 0
    for r in range(d):
        start = (r % before) * span + r // before
        for src in srcs:
            rows = src[pl.ds(start, per, stride=step), :] if step > 1 else src[pl.ds(start, per), :]
            dst[pl.ds(at, per), :] = rows.astype(dst.dtype)
            at += per


def _scatter(out_ref, src, d):
    per = TOK // d
    if d == 1:
        out_ref[...] = src[...]
        return
    for r in range(d):
        out_ref[pl.ds(r, per, stride=d), :] = src[pl.ds(r * per, per), :]


def _dilated_specs(nblk, reverse):
    def at(s):
        return (nblk - 1 - s) if reverse else s
    main = [pl.BlockSpec((1, d, TOK // d, BAND), lambda j, s: (j, 0, at(s), 0)) for d in DILATIONS]
    prev = [pl.BlockSpec((1, d, TOK // d, BAND), lambda j, s: (j, 0, jnp.maximum(at(s) - 1, 0), 0))
            for d in DILATIONS]
    return main, prev


def _window_rows(prev_ref, main_ref, dst, d):
    per = TOK // d
    for r in range(d):
        dst[pl.ds(r * (per + BAND), BAND), :] = prev_ref[0, r, pl.ds(per - BAND, BAND), :]
        dst[pl.ds(r * (per + BAND) + BAND, per), :] = main_ref[0, r]


def _attn_fwd(qs, ks, vs, carry=None):
    t = qs[0].shape[2]
    nblk = t // TOK
    nd = len(DILATIONS)

    def body(*refs):
        q_refs, kp_refs, k_refs = refs[:nd], refs[nd:2 * nd], refs[2 * nd:3 * nd]
        vp_refs, v_refs = refs[3 * nd:4 * nd], refs[4 * nd:5 * nd]
        y_ref, l_ref, kw_s, vw_s, ob, lb, on, ln = refs[5 * nd:]
        i = pl.program_id(1)
        diff, col, head0, hm = _stack_masks()
        band_ok = jnp.logical_and(diff >= 0, diff <= BAND)
        for g, d in enumerate(DILATIONS):
            per = TOK // d
            nb = per // BAND
            pad = per + BAND
            _window_rows(kp_refs[g], k_refs[g], kw_s, d)
            _window_rows(vp_refs[g], v_refs[g], vw_s, d)
            q_ref = q_refs[g]

            def unit(u, carry):
                r, b = u // nb, u % nb
                qu = q_ref[0, r, pl.ds(pl.multiple_of(b * BAND, BAND), BAND), :]
                start = pl.multiple_of(r * pad + b * BAND, BAND)
                kw = kw_s[pl.ds(start, 2 * BAND), :]
                vw = vw_s[pl.ds(start, 2 * BAND), :]
                lo = jnp.where(jnp.logical_and(i == 0, b == 0), BAND, 0)
                s = _dot_nt(jnp.concatenate([qu * hm[0], qu * hm[1]], axis=0), kw)
                s = jnp.where(jnp.logical_and(band_ok, col >= lo), s, NEG)
                mx = jnp.max(s, axis=-1, keepdims=True)
                e = jnp.exp(s - mx)
                den = jnp.sum(e, axis=-1, keepdims=True)
                o2 = _dot(e.astype(BF16), vw) / den
                l2 = jnp.broadcast_to(mx + jnp.log(den), (2 * BAND, BAND))
                rows = pl.ds(pl.multiple_of(u * BAND, BAND), BAND)
                ob[rows, :] = jnp.where(head0, o2[:BAND], o2[BAND:])
                lb[rows, :] = jnp.where(head0, l2[:BAND], l2[BAND:])
                return carry

            lax.fori_loop(0, UNITS, unit, 0, unroll=16)
            _scatter(on.at[g], ob, d)
            _scatter(ln.at[g], lb, d)
        ls = [ln[0], ln[1], ln[2]]
        mx = jnp.maximum(jnp.maximum(ls[0], ls[1]), ls[2])
        es = [jnp.exp(l - mx) for l in ls]
        tot = es[0] + es[1] + es[2]
        y_ref[...] = (es[0] * on[0] + es[1] * on[1] + es[2] * on[2]) / tot
        l_ref[...] = mx + jnp.log(tot)

    main, prev = _dilated_specs(nblk, False)
    out = pl.BlockSpec((TOK, BAND), lambda j, i: (i, j))
    win_rows = max(d * (TOK // d + BAND) for d in DILATIONS)
    return _call(
        body, list(qs) + list(ks) + list(ks) + list(vs) + list(vs), name="attn_fwd",
        grid=(PAIRS, nblk), out_shape=[_sds((t, D_ATTN), F32)] * 2,
        in_specs=main + prev + main + prev + main, out_specs=[out, out],
        scratch_shapes=[pltpu.VMEM((win_rows, BAND), BF16)] * 2 + [pltpu.VMEM((TOK, BAND), F32)] * 2
        + [pltpu.VMEM((nd, TOK, BAND), F32)] * 2,
        semantics=("parallel", "parallel"), carry=carry)


def _attn_bwd(qs, ks, vs, do, lse, dd, carry=None):
    t = qs[0].shape[2]
    nblk = t // TOK
    nd = len(DILATIONS)
    offs = [sum(DILATIONS[:g]) * BAND for g in range(nd)]

    def body(*refs):
        q_refs, kp_refs, k_refs = refs[:nd], refs[nd:2 * nd], refs[2 * nd:3 * nd]
        vp_refs, v_refs = refs[3 * nd:4 * nd], refs[4 * nd:5 * nd]
        (do_ref, l_ref, d_ref, dq_ref, dk_ref, dv_ref, kw_s, vw_s, dos, lds, pn, dqb, dkb, dvb, ckb,
         cvb, *more) = refs[5 * nd:]
        folds, mids = more[:6], more[6:]
        step = pl.program_id(1)
        i = nblk - 1 - step
        key = lax.broadcasted_iota(jnp.int32, (2 * BAND, 2 * BAND), 0)
        qry = lax.broadcasted_iota(jnp.int32, (2 * BAND, 2 * BAND), 1) & (BAND - 1)
        off = key - qry
        band_ok = jnp.logical_and(off >= 0, off <= BAND)
        lane = lax.broadcasted_iota(jnp.int32, (BAND, BAND), 1)
        head0 = lane < HEAD_DIM
        hm = [jnp.where(head0, 1.0, 0.0).astype(BF16), jnp.where(head0, 0.0, 1.0).astype(BF16)]
        lane2 = lax.broadcasted_iota(jnp.int32, (2 * BAND, BAND), 1) & (HEAD_DIM - 1)
        ones_l = jnp.where(lane2 < 3, 1.0, 0.0).astype(BF16)
        ones_d = jnp.where(jnp.logical_and(lane2 >= 3, lane2 < 6), 1.0, 0.0).astype(BF16)
        piece = lax.broadcasted_iota(jnp.int32, (TOK, BAND), 1) & (HEAD_DIM - 1)

        def pieces(x, at):
            hi, mid, lo = _split3(-x)
            return jnp.where(piece == at, hi,
                             jnp.where(piece == at + 1, mid, jnp.where(piece == at + 2, lo, 0.0)))

        pn[...] = pieces(l_ref[...], 0) + pieces(d_ref[...], 3)
        order = sorted(range(nd), key=lambda a: -DILATIONS[a])
        assert DILATIONS[order[-1]] == 1
        levels = {1: (do_ref, pn)}
        for n, a in enumerate(reversed(order[1:-1])):
            d, before = DILATIONS[a], DILATIONS[order[-1 - n]]
            levels[d] = (mids[2 * n], mids[2 * n + 1])
            for src, dst in zip(levels[before], levels[d]):
                _gather([src], dst, d, before)
        for pos, g in enumerate(order):
            d = DILATIONS[g]
            per = TOK // d
            nb = per // BAND
            pad = per + BAND
            _window_rows(kp_refs[g], k_refs[g], kw_s, d)
            _window_rows(vp_refs[g], v_refs[g], vw_s, d)
            known = d if d in levels else DILATIONS[order[pos + 1]]
            _gather([levels[known][0]], dos, d, known)
            _gather([levels[known][1]], lds, d, known)
            for r in range(d):
                spare = pl.ds(r * pad, BAND)
                dkb[spare, :] = jnp.zeros((BAND, BAND), F32)
                dvb[spare, :] = jnp.zeros((BAND, BAND), F32)
            q_ref = q_refs[g]

            def unit(u, c_):
                r, b = u // nb, u % nb
                rows = pl.ds(pl.multiple_of(u * BAND, BAND), BAND)
                qu = q_ref[0, r, pl.ds(pl.multiple_of(b * BAND, BAND), BAND), :]
                dou, ldu = dos[rows, :], lds[rows, :]
                q2 = jnp.concatenate([qu * hm[0], qu * hm[1]], axis=0)
                do2 = jnp.concatenate([dou * hm[0], dou * hm[1]], axis=0)
                ld2 = jnp.concatenate([ldu * hm[0], ldu * hm[1]], axis=0)
                acc = pl.ds(pl.multiple_of(r * pad + b * BAND, BAND), 2 * BAND)
                kw = kw_s[acc, :]
                vw = vw_s[acc, :]
                lo = jnp.where(jnp.logical_and(i == 0, b == 0), BAND, 0)
                ok = jnp.logical_and(band_ok, key >= lo)
                st = _dot_nt(jnp.concatenate([kw, ones_l], axis=1), jnp.concatenate([q2, ld2], axis=1))
                dpt = _dot_nt(jnp.concatenate([vw, ones_d], axis=1), jnp.concatenate([do2, ld2], axis=1))
                pt = jnp.where(ok, jnp.exp(st), 0.0)
                dst = (pt * dpt).astype(BF16)
                low = pl.ds(pl.multiple_of(r * pad + b * BAND, BAND), BAND)
                high = pl.ds(pl.multiple_of(r * pad + (b + 1) * BAND, BAND), BAND)
                dkw = _dot(dst, q2)
                dvw = _dot(pt.astype(BF16), do2)
                dkb[low, :] += dkw[:BAND]
                dvb[low, :] += dvw[:BAND]
                dkb[high, :] = dkw[BAND:]
                dvb[high, :] = dvw[BAND:]
                dq2 = _dot_tn(dst, kw)
                dqb[rows, :] = jnp.where(head0, dq2[:BAND], dq2[BAND:])
                return c_

            lax.fori_loop(0, UNITS, unit, 0, unroll=16)

            for r in range(d):
                last = pl.ds(r * pad + per, BAND)
                kept = pl.ds(offs[g] + r * BAND, BAND)

                @pl.when(step > 0)
                def _():
                    dkb[last, :] += ckb[kept, :]
                    dvb[last, :] += cvb[kept, :]

                ckb[kept, :] = dkb[pl.ds(r * pad, BAND), :]
                cvb[kept, :] = dvb[pl.ds(r * pad, BAND), :]
            narrower = DILATIONS[order[pos + 1]] if pos + 1 < nd else None
            for n, (buf, out_ref, stride, at) in enumerate(
                    ((dqb, dq_ref, per, 0), (dkb, dk_ref, pad, BAND), (dvb, dv_ref, pad, BAND))):
                wider, onward = folds[2 * n + pos % 2], folds[2 * n + (pos + 1) % 2]
                for r in range(d):
                    val = buf[pl.ds(r * stride + at, per), :]
                    if pos > 0:
                        val = val + wider[pl.ds(r * per, per), :]
                    if narrower is None:
                        out_ref[...] = val
                    else:
                        start = (r % narrower) * (TOK // narrower) + r // narrower
                        onward[pl.ds(start, per, stride=d // narrower), :] = val

    main, prev = _dilated_specs(nblk, True)
    tok = pl.BlockSpec((TOK, BAND), lambda j, s: (nblk - 1 - s, j))
    acc_rows = max(d * (TOK // d + BAND) for d in DILATIONS)
    kept_rows = sum(DILATIONS) * BAND
    return _call(
        body, list(qs) + list(ks) + list(ks) + list(vs) + list(vs) + [do, lse, dd], name="attn_bwd",
        grid=(PAIRS, nblk), out_shape=[_sds((t, D_ATTN), F32)] * 3,
        in_specs=main + prev + main + prev + main + [tok] * 3, out_specs=[tok] * 3,
        scratch_shapes=[pltpu.VMEM((acc_rows, BAND), BF16)] * 2 + [pltpu.VMEM((TOK, BAND), BF16)] * 2
        + [pltpu.VMEM((TOK, BAND), F32)] * 2 + [pltpu.VMEM((acc_rows, BAND), F32)] * 2
        + [pltpu.VMEM((kept_rows, BAND), F32)] * 2
        + [pltpu.VMEM((TOK, BAND), F32)] * (6 + 2 * (nd - 2)),
        semantics=("parallel", "arbitrary"), carry=carry)


def _halo_rows(tm, t):
    per = tm // 8
    prev = lambda i: (jnp.maximum(i * per - 1, 0), 0)
    nxt = lambda i: (jnp.minimum((i + 1) * per, t // 8 - 1), 0)
    return prev, nxt


def _mixer_out(z, cw, y_attn, g_conv, g_attn, tm, carry=None):
    t = z.shape[0]
    prev, _ = _halo_rows(tm, t)

    def body(z_ref, zp_ref, cw_ref, y_ref, gc_ref, ga_ref, mix_ref):
        i = pl.program_id(0)
        keep = jnp.where(i > 0, 1.0, 0.0)
        cu = jnp.concatenate([zp_ref[:, 0:512] * zp_ref[:, 1024:1536] * keep,
                              z_ref[:, 0:512] * z_ref[:, 1024:1536]], axis=0)
        c = (cw_ref[0:1, :] * pltpu.roll(cu, 2, 0) + cw_ref[1:2, :] * pltpu.roll(cu, 1, 0)
             + cw_ref[2:3, :] * cu)[8:, :]
        yc = z_ref[:, 512:1024] * c
        mix_ref[:, 0:512] = (yc * _rms_scale(yc) * gc_ref[...]).astype(BF16)
        ya = y_ref[...]
        mix_ref[:, 512:1024] = (ya * _rms_scale(ya) * ga_ref[...]).astype(BF16)

    blk = pl.BlockSpec((tm, 512), lambda i: (i, 0))
    vec = pl.BlockSpec((1, 512), lambda i: (0, 0))
    return _call(
        body, [z, z, cw, y_attn, g_conv, g_attn], name="mixer_out", grid=(t // tm,),
        out_shape=_sds((t, 1024), BF16),
        in_specs=[pl.BlockSpec((tm, 1536), lambda i: (i, 0)), pl.BlockSpec((8, 1536), prev),
                  pl.BlockSpec((8, 512), lambda i: (0, 0)), blk, vec, vec],
        out_specs=pl.BlockSpec((tm, 1024), lambda i: (i, 0)),
        semantics=("parallel",), carry=carry)


def _mixer_bwd(z, dx1, wout, y_attn, cw, g_conv, g_attn, ones_bd, tm, carry=None):
    t = z.shape[0]
    nblk = t // tm
    prev, nxt = _halo_rows(tm, t)
    e = tm + 16

    def body(z_ref, zp_ref, zn_ref, dx_ref, dxn_ref, w_ref, y_ref, cw_ref, gc_ref, ga_ref, bd_ref,
             dz_ref, do_ref, dd_ref, dcw_ref, dgc_ref, dga_ref):
        i = pl.program_id(0)
        dm = _dot_nt(dx_ref[...], w_ref[...])
        dmn = _dot_nt(dxn_ref[...], w_ref[0:D_CONV, :])[0:8, :]
        rows = lax.broadcasted_iota(jnp.int32, (e, 1), 0)
        lo = jnp.where(i > 0, 0, 8)
        hi = jnp.where(i < nblk - 1, e, tm + 8)
        ze = jnp.concatenate([zp_ref[...], z_ref[...], zn_ref[...]], axis=0)
        u, gb, gcv = ze[:, 0:512], ze[:, 512:1024], ze[:, 1024:1536]
        w0, w1, w2 = cw_ref[0:1, :], cw_ref[1:2, :], cw_ref[2:3, :]
        cu = jnp.where(rows >= lo, gcv * u, 0.0)
        cu1, cu2 = pltpu.roll(cu, 1, 0), pltpu.roll(cu, 2, 0)
        c = w0 * cu2 + w1 * cu1 + w2 * cu
        yc = gb * c
        dma = jnp.concatenate([jnp.zeros((8, 512), F32), dm[:, 0:512], dmn], axis=0)
        dyc, ych = _rms_bwd(yc, _rms_scale(yc), gc_ref[...], dma)
        dc = jnp.where(jnp.logical_and(rows >= 8, rows < hi), dyc * gb, 0.0)
        dcu = w0 * pltpu.roll(dc, e - 2, 0) + w1 * pltpu.roll(dc, e - 1, 0) + w2 * dc
        mid = slice(8, 8 + tm)
        dz_ref[:, 0:512] = (dcu * gcv)[mid, :].astype(BF16)
        dz_ref[:, 512:1024] = (dyc * c)[mid, :].astype(BF16)
        dz_ref[:, 1024:1536] = (dcu * u)[mid, :].astype(BF16)

        ya = y_ref[...]
        dmb = dm[:, 512:1024]
        dya, yah = _rms_bwd(ya, _rms_scale(ya), ga_ref[...], dmb)
        do_ref[...] = dya
        dd_ref[...] = _head_sum(dya * ya, bd_ref[...])

        @pl.when(i == 0)
        def _():
            dcw_ref[...] = jnp.zeros_like(dcw_ref)
            dgc_ref[...] = jnp.zeros_like(dgc_ref)
            dga_ref[...] = jnp.zeros_like(dga_ref)

        dcm = jnp.where(rows < tm + 8, dc, 0.0)
        dcw_ref[0:1, :] += jnp.sum(dcm * cu2, axis=0, keepdims=True)
        dcw_ref[1:2, :] += jnp.sum(dcm * cu1, axis=0, keepdims=True)
        dcw_ref[2:3, :] += jnp.sum(dcm * cu, axis=0, keepdims=True)
        dgc_ref[...] += jnp.sum((dma * ych)[mid, :], axis=0, keepdims=True)
        dga_ref[...] += jnp.sum(dmb * yah, axis=0, keepdims=True)

    blk = pl.BlockSpec((tm, 512), lambda i: (i, 0))
    vec = pl.BlockSpec((1, 512), lambda i: (0, 0))
    cwb = pl.BlockSpec((8, 512), lambda i: (0, 0))
    next16 = lambda i: (jnp.minimum((i + 1) * (tm // 16), t // 16 - 1), 0)
    return _call(
        body, [z, z, z, dx1, dx1, wout, y_attn, cw, g_conv, g_attn, ones_bd], name="mixer_bwd",
        grid=(nblk,),
        out_shape=[_sds((t, D_IN), BF16), _sds((t, 512), F32), _sds((t, 512), F32),
                   _sds((8, 512), F32), _sds((1, 512), F32), _sds((1, 512), F32)],
        in_specs=[pl.BlockSpec((tm, 1536), lambda i: (i, 0)), pl.BlockSpec((8, 1536), prev),
                  pl.BlockSpec((8, 1536), nxt), pl.BlockSpec((tm, D_MODEL), lambda i: (i, 0)),
                  pl.BlockSpec((16, D_MODEL), next16),
                  pl.BlockSpec(wout.shape, lambda i: (0, 0), pipeline_mode=_resident(True)),
                  blk, cwb, vec, vec, pl.BlockSpec((512, 512), lambda i: (0, 0))],
        out_specs=[pl.BlockSpec((tm, 1536), lambda i: (i, 0)), blk, blk, cwb, vec, vec],
        carry=carry)


def _qkv_bwd(z, dz, dqn, dkn, dv, gq, gk, ones_bd, tm, carry=None):
    t = z.shape[0]

    def body(zq_ref, zk_ref, _, dqn_ref, dkn_ref, dv_ref, gq_ref, gk_ref, bd_ref,
             dz_ref, dgq_ref, dgk_ref):
        bd = bd_ref[...]

        @pl.when(pl.program_id(0) == 0)
        def _():
            dgq_ref[...] = jnp.zeros_like(dgq_ref)
            dgk_ref[...] = jnp.zeros_like(dgk_ref)

        def back(v, dn, g, scale):
            r = _head_rms_scale(v, bd)
            vh = v * r
            dh = dn * (g * scale)
            dv = r * (dh - vh * (_head_sum(dh * vh, bd) * (1.0 / HEAD_DIM)))
            return dv, jnp.sum(dn * scale * vh, axis=0, keepdims=True)

        dq, dgq = back(zq_ref[...], dqn_ref[...], gq_ref[...], HEAD_DIM ** -0.5)
        dk, dgk = back(zk_ref[...], dkn_ref[...], gk_ref[...], 1.0)
        dgq_ref[...] += dgq
        dgk_ref[...] += dgk
        dz_ref[:, 0:512] = dq.astype(BF16)
        dz_ref[:, 512:1024] = dk.astype(BF16)
        dz_ref[:, 1024:1536] = dv_ref[...].astype(BF16)

    blk = pl.BlockSpec((tm, 512), lambda i: (i, 0))
    vec = pl.BlockSpec((1, 512), lambda i: (0, 0))
    return _call(
        body, [z, z, dz, dqn, dkn, dv, gq, gk, ones_bd], name="qkv_bwd", grid=(t // tm,),
        out_shape=[_sds((t, D_IN), BF16), _sds((1, 512), F32), _sds((1, 512), F32)],
        in_specs=[pl.BlockSpec((tm, 512), lambda i: (i, 3)), pl.BlockSpec((tm, 512), lambda i: (i, 4)),
                  ANY] + [blk] * 3 + [vec, vec, pl.BlockSpec((512, 512), lambda i: (0, 0))],
        out_specs=[pl.BlockSpec((tm, 1536), lambda i: (i, 1)), vec, vec],
        carry=carry, aliases={2: 0})


def _columns_from_chips(g):
    return g.transpose(1, 0, 2).reshape(g.shape[1], N_CHIPS * g.shape[2])


def kernel(x, g_mix, w_in, conv_w, g_q, g_k, g_conv_out, g_attn_out, w_out, g_ffn, w_gate, w_up, w_down, loss_target, m_g_mix, m_w_in, m_conv_w, m_g_q, m_g_k, m_g_conv_out, m_g_attn_out, m_w_out, m_g_ffn, m_w_gate, m_w_up, m_w_down, v_g_mix, v_w_in, v_conv_w, v_g_q, v_g_k, v_g_conv_out, v_g_attn_out, v_w_out, v_g_ffn, v_w_gate, v_w_up, v_w_down):
    t = x.shape[1]
    xs = x[0]
    target = loss_target[0]
    tm = min(512, t)
    tm_wide = min(1024, t)
    tmm = min(2048, t)

    cw_pad = jnp.pad(conv_w[0], ((0, 13), (0, 0)))
    gathered = _all_gather([w_in[0].astype(BF16), cw_pad])
    win = _columns_from_chips(gathered[0])
    cw = jnp.pad(gathered[1][:, 0:3, :].transpose(1, 0, 2).reshape(3, D_CONV), ((0, 5), (0, 0)))
    later = [w_out[0].astype(BF16), w_gate[0].T.astype(BF16), w_up[0].T.astype(BF16),
             w_down[0].astype(BF16)]

    head_id = jnp.arange(D_ATTN) // HEAD_DIM
    ones_bd = (head_id[:, None] == head_id[None, :]).astype(BF16)
    gq_t = jnp.tile(g_q, (1, D_ATTN // HEAD_DIM))
    gk_t = jnp.tile(g_k, (1, D_ATTN // HEAD_DIM))

    h1, z, *dilated = _in_proj(xs, g_mix, win, gq_t, gk_t, ones_bd, tm)
    nd = len(DILATIONS)
    qs, ks, vs = dilated[:nd], dilated[nd:2 * nd], dilated[2 * nd:]
    (y_attn, lse), gathered = _attn_fwd(qs, ks, vs, carry=_x_gather_chips(later))
    mix, gathered = _mixer_out(z, cw, y_attn, g_conv_out, g_attn_out, tm_wide,
                               carry=_x_gather_sibling(gathered))
    wout = gathered[0].reshape(D_MODEL, D_MODEL)
    wgate_t = gathered[1].reshape(D_FF, D_MODEL)
    wup_t = gathered[2].reshape(D_FF, D_MODEL)
    wdown = gathered[3].reshape(D_FF, D_MODEL)
    (x1,) = _matmul("out_proj", mix, wout, [xs], [F32], lambda acc, r: (r + acc,), tm_wide, D_MODEL)
    h2, gate, up, act = _norm_matmul("ffn_up", x1, g_ffn, [wgate_t, wup_t], tm, D_FF, True, BF16,
                                     transposed_w=True)

    def loss_epilogue(acc, r, tgt):
        err = r + acc - tgt
        dy = err * (1.0 / D_MODEL)
        return dy, dy, jnp.sum(err * err)

    dx2, dx2b, loss_sum = _matmul("ffn_down_loss", act, wdown, [x1, target], [F32, BF16],
                                  loss_epilogue, tm_wide, D_MODEL, loss=True)

    def swiglu_bwd(da, gt, u):
        gt, u = gt.astype(F32), u.astype(F32)
        s = _sigmoid(gt)
        return da * u * (s * (1.0 + gt * (1.0 - s))), da * (gt * s)

    dgate, dup = _matmul("ffn_down_bwd", dx2b, wdown, [gate, up], [BF16, BF16], swiglu_bwd,
                         tm, D_FF, transposed_w=True)
    gw_down = _matmul_tn("grad_w_down", act, dx2b, 512, tmm)
    gw_gate_t = _matmul_tn("grad_w_gate", dgate, h2, 512, tmm)
    gw_up_t = _matmul_tn("grad_w_up", dup, h2, 512, tmm)

    me = 2 * lax.axis_index("x") + lax.axis_index("y")
    where = jnp.stack([lax.axis_index("c"), me]).astype(jnp.int32)

    def pair_sums(names, full, got):
        return [_pair_sum(f"pair_sum_{nme}", a, b, where) for nme, a, b in zip(names, full, got)]

    def chip_sums(names, pair, got):
        return [_chip_sum(f"chip_sum_{nme}", own, b) for nme, (_, own), b in zip(names, pair, got)]

    ffn = ["w_gate", "w_up", "w_down"]
    full = [g.reshape(N_CHIPS, D_FF // N_CHIPS, D_MODEL) for g in (gw_gate_t, gw_up_t, gw_down)]
    (dx1, dx1b, gg_ffn), got = _matmul_norm_bwd(
        "ffn_up_bwd", [(dgate, wgate_t), (dup, wup_t)], x1, dx2, g_ffn, tm, carry=_x_pair(full),
        transposed_w=False)
    pair = pair_sums(ffn, full, got)
    gw_out = _matmul_tn("grad_w_out", mix, dx1b, 512, tmm)
    full = [gw_out.reshape(N_CHIPS, D_MODEL // N_CHIPS, D_MODEL)]
    (dzc, do, dd, gcw, gg_conv, gg_attn), got = _mixer_bwd(
        z, dx1b, wout, y_attn, cw, g_conv_out, g_attn_out, ones_bd, tm, carry=_x_pair(full))
    pair += pair_sums(["w_out"], full, got)
    early = ffn + ["w_out"]
    (dqn, dkn, dv), got = _attn_bwd(qs, ks, vs, do, lse, dd, carry=_x_chips([p for p, _ in pair]))
    mine = chip_sums(early, pair, got)
    (dz, gg_q, gg_k), theirs = _qkv_bwd(z, dzc, dqn, dkn, dv, gq_t, gk_t, ones_bd, tm,
                                        carry=_x_share(mine))
    full = [_matmul_tn("grad_w_in", h1, dz, D_IN // N_CHIPS, tmm, by_chip=True)]
    got = _exchange_alone("grad_pair_exchange_w_in", _x_pair(full))
    pair = pair_sums(["w_in"], full, got)
    (grad_x, _, gg_mix), got = _matmul_norm_bwd("in_proj_bwd", [(dz, win)], xs, dx1, g_mix, tm_wide,
                                                carry=_x_chips([pair[0][0]]))
    mine += chip_sums(["w_in"], pair, got)
    theirs = list(theirs) + list(_exchange_alone("grad_pair_share_w_in", _x_share(mine[-1:])))
    big = early + ["w_in"]

    small = _small_all_reduce({
        "g_mix": gg_mix, "g_ffn": gg_ffn, "g_conv_out": gg_conv, "g_attn_out": gg_attn,
        "g_q": gg_q, "g_k": gg_k, "loss": loss_sum, "conv_w": gcw})
    heads = D_ATTN // HEAD_DIM
    grads = {
        "g_mix": small[0:1, :], "g_ffn": small[1:2, :],
        "g_conv_out": small[2:3, 0:512], "g_attn_out": small[2:3, 512:1024],
        "g_q": small[3, 0:512].reshape(heads, HEAD_DIM).sum(axis=0)[None, :],
        "g_k": small[3, 512:1024].reshape(heads, HEAD_DIM).sum(axis=0)[None, :],
        "conv_w": lax.dynamic_slice(small[8:11, 0:512], (0, me * (D_CONV // N_CHIPS)),
                                    (3, D_CONV // N_CHIPS)),
    }
    halves = dict(zip(big, zip(mine, theirs)))
    loss = small[4, 0] * 0.5 * (1.0 / D_MODEL)

    weights = dict(g_mix=g_mix, w_in=w_in, conv_w=conv_w, g_q=g_q, g_k=g_k, g_conv_out=g_conv_out,
                   g_attn_out=g_attn_out, w_out=w_out, g_ffn=g_ffn, w_gate=w_gate, w_up=w_up,
                   w_down=w_down)
    moments_m = dict(g_mix=m_g_mix, w_in=m_w_in, conv_w=m_conv_w, g_q=m_g_q, g_k=m_g_k,
                     g_conv_out=m_g_conv_out, g_attn_out=m_g_attn_out, w_out=m_w_out, g_ffn=m_g_ffn,
                     w_gate=m_w_gate, w_up=m_w_up, w_down=m_w_down)
    moments_v = dict(g_mix=v_g_mix, w_in=v_w_in, conv_w=v_conv_w, g_q=v_g_q, g_k=v_g_k,
                     g_conv_out=v_g_conv_out, g_attn_out=v_g_attn_out, w_out=v_w_out, g_ffn=v_g_ffn,
                     w_gate=v_w_gate, w_up=v_w_up, w_down=v_w_down)
    names = list(weights)
    out_g, out_d, out_m, out_v = [], [], [], []
    for nme in names:
        wgt = weights[nme]
        shape2 = wgt.shape[-2:] if wgt.ndim == 3 else wgt.shape
        flip = nme in ("w_gate", "w_up")

        def to2d(a):
            return a.reshape(shape2).T if flip else a.reshape(shape2)

        def back(a):
            return (a.T if flip else a).reshape(wgt.shape)

        state = (to2d(wgt), to2d(moments_m[nme]), to2d(moments_v[nme]))
        if nme in halves:
            g2, dlt, nm, nv = _adamw_shard(f"adamw_{nme}", *state, *halves[nme], where)
        else:
            g2 = grads[nme].reshape(shape2)
            dlt, nm, nv = _adamw(f"adamw_{nme}", state[0], g2, state[1], state[2])
        out_g.append(back(g2))
        out_d.append(back(dlt))
        out_m.append(back(nm))
        out_v.append(back(nv))
    return (loss, grad_x[None], *out_g, *out_d, *out_m, *out_v)
```
